```python
import jax, jax.numpy as jnp
from jax import lax
import numpy as np

D_MODEL = 1024
BATCH = 16
SEQ = 4096
DEPTH = 2

HEAD_DIM = 64
N_HEADS = D_MODEL // HEAD_DIM
DILATED_GROUPS = ((128, 1), (512, 4), (2048, 16))
N_GROUPS = len(DILATED_GROUPS)
BAND_BLOCK = 128
ROT_DIM = HEAD_DIM // 4
ROPE_THETA = 500000.0
FOX_BLOCK = 128
D_FF = 2816
N_A_LAYERS = DEPTH // 2
N_B_LAYERS = DEPTH - N_A_LAYERS
HD = N_HEADS * HEAD_DIM
EPS = 1e-6

kernel_name = "yoco_dilated_fox_macaron_trunk"


def rms_norm(x, g):
    xf = x.astype(jnp.float32)
    y = xf * lax.rsqrt(jnp.mean(xf * xf, axis=-1, keepdims=True) + EPS)
    return (y * g.astype(jnp.float32)).astype(x.dtype)


def swiglu(x, w_in, w_out):
    gate, up = jnp.split(x @ w_in, 2, axis=-1)
    return (jax.nn.silu(gate) * up) @ w_out


def rope_partial(x, positions):
    half = ROT_DIM // 2
    inv_freq = ROPE_THETA ** (-jnp.arange(0, ROT_DIM, 2, dtype=jnp.float32) / ROT_DIM)
    ang = positions.astype(jnp.float32)[..., None] * inv_freq
    cos, sin = jnp.cos(ang)[:, :, None, :], jnp.sin(ang)[:, :, None, :]
    xr = x[..., :ROT_DIM].astype(jnp.float32)
    x1, x2 = xr[..., :half], xr[..., half:]
    rot = jnp.concatenate([x1 * cos - x2 * sin, x2 * cos + x1 * sin], axis=-1)
    return jnp.concatenate([rot.astype(x.dtype), x[..., ROT_DIM:]], axis=-1)


def dilated_band_attention(q, k, v, window, dilation):
    b, s, h, dh = q.shape
    n_steps = window // dilation
    span = dilation * BAND_BLOCK
    s_pad = -(-s // span) * span
    seq_len = s_pad // dilation
    nb = seq_len // BAND_BLOCK
    pad = ((0, 0), (0, s_pad - s), (0, 0), (0, 0))

    def to_blocks(t):
        t = jnp.pad(t, pad).reshape(b, seq_len, dilation, h, dh).transpose(0, 2, 1, 3, 4)
        return t.reshape(b, dilation, nb, BAND_BLOCK, h, dh)

    def with_prev(t):
        prev = jnp.pad(t, ((0, 0), (0, 0), (1, 0), (0, 0), (0, 0), (0, 0)))[:, :, :-1]
        return jnp.concatenate([prev, t], axis=3)

    qb = to_blocks(q)
    kk = with_prev(to_blocks(k))
    vv = with_prev(to_blocks(v))
    scores = jnp.einsum('brnqhd,brnkhd->brnhqk', qb, kk).astype(jnp.float32) * (dh ** -0.5)
    qi = jnp.arange(BAND_BLOCK)[:, None]
    kj = jnp.arange(2 * BAND_BLOCK)[None, :]
    dist = qi + BAND_BLOCK - kj
    blk = jnp.arange(nb)[:, None, None]
    valid = (dist >= 0) & (dist <= n_steps) & ((blk > 0) | (kj >= BAND_BLOCK))
    scores = jnp.where(valid[:, None], scores, -jnp.inf)
    lse = jax.nn.logsumexp(scores, axis=-1)
    probs = jnp.exp(scores - lse[..., None])
    out = jnp.einsum('brnhqk,brnkhd->brnqhd', probs.astype(v.dtype), vv)
    out = out.transpose(0, 2, 3, 1, 4, 5).reshape(b, s_pad, h, dh)[:, :s]
    lse = lse.transpose(0, 2, 4, 1, 3).reshape(b, s_pad, h)[:, :s]
    return out, lse


def dilated_mixture_mixer(hn, positions, w_qkv, q_norm, k_norm, w_o):
    b, s, _ = hn.shape
    qkv = (hn @ w_qkv).reshape(b, s, N_GROUPS, 3, N_HEADS, HEAD_DIM)
    outs, lses = [], []
    for g, (window, dilation) in enumerate(DILATED_GROUPS):
        q = rope_partial(rms_norm(qkv[:, :, g, 0], q_norm[g]), positions)
        k = rope_partial(rms_norm(qkv[:, :, g, 1], k_norm[g]), positions)
        o, lse = dilated_band_attention(q, k, qkv[:, :, g, 2], window, dilation)
        outs.append(o.astype(jnp.float32))
        lses.append(lse)
    alpha = jax.nn.softmax(jnp.stack(lses, axis=0), axis=0)
    mixed = jnp.sum(alpha[..., None] * jnp.stack(outs, axis=0), axis=0).astype(hn.dtype)
    return mixed.reshape(b, s, HD) @ w_o


def shared_kv(hn, w_kv, b_f, k_norm):
    b, s, _ = hn.shape
    proj = hn @ w_kv
    k = rms_norm(proj[..., :HD].reshape(b, s, N_HEADS, HEAD_DIM), k_norm)
    v = proj[..., HD:2 * HD].reshape(b, s, N_HEADS, HEAD_DIM)
    log_f = jax.nn.log_sigmoid(proj[..., 2 * HD:].astype(jnp.float32) + b_f.astype(jnp.float32))
    cum = jnp.cumsum(log_f, axis=1)
    return k, v, cum


def forgetting_attention(hn, k, v, cum, w_q, q_norm, w_o):
    b, s, _ = hn.shape
    q = rms_norm((hn @ w_q).reshape(b, s, N_HEADS, HEAD_DIM), q_norm)
    nb = s // FOX_BLOCK
    q_blocks = q.reshape(b, nb, FOX_BLOCK, N_HEADS, HEAD_DIM).transpose(1, 0, 2, 3, 4)
    c_blocks = cum.reshape(b, nb, FOX_BLOCK, N_HEADS).transpose(1, 0, 2, 3)
    ck = cum.transpose(0, 2, 1)[:, :, None, :]
    key_pos = jnp.arange(s)
    scale = HEAD_DIM ** -0.5

    def block(args):
        qb, cb, bi = args
        logits = jnp.einsum('bqhd,bshd->bhqs', qb, k).astype(jnp.float32) * scale
        logits = logits + (cb.transpose(0, 2, 1)[..., None] - ck)
        qpos = bi * FOX_BLOCK + jnp.arange(FOX_BLOCK)
        logits = jnp.where(key_pos[None, :] <= qpos[:, None], logits, -jnp.inf)
        p = jax.nn.softmax(logits, axis=-1)
        return jnp.einsum('bhqs,bshd->bqhd', p.astype(v.dtype), v)

    o = lax.map(block, (q_blocks, c_blocks, jnp.arange(nb)))
    o = o.transpose(1, 0, 2, 3, 4).reshape(b, s, HD)
    return o @ w_o


def _fwd_setup_inputs(seed: int = 0) -> dict:
    key = jax.random.key(seed)
    ks = jax.random.split(key, 20)
    f32 = jnp.float32

    def nrm(k, shape, fan_in):
        return jax.random.normal(k, shape, f32) * (fan_in ** -0.5)

    def gain(k, shape):
        return 1.0 + 0.05 * jax.random.normal(k, shape, f32)

    x = jax.random.normal(ks[0], (BATCH, SEQ, D_MODEL), f32)
    offset = jax.random.randint(ks[1], (BATCH, 1), 0, 1024, dtype=jnp.int32)
    positions = (jnp.arange(SEQ, dtype=jnp.int32)[None, :] + offset).astype(jnp.int32)
    return {
        "x": x,
        "positions": positions,
        "ffn_norm": gain(ks[2], (DEPTH, 2, D_MODEL)),
        "ffn_w_in": nrm(ks[3], (DEPTH, 2, D_MODEL, 2 * D_FF), D_MODEL),
        "ffn_w_out": nrm(ks[4], (DEPTH, 2, D_FF, D_MODEL), D_FF),
        "mix_norm": gain(ks[5], (DEPTH, D_MODEL)),
        "a_w_qkv": nrm(ks[6], (N_A_LAYERS, D_MODEL, N_GROUPS * 3 * HD), D_MODEL),
        "a_q_norm": gain(ks[7], (N_A_LAYERS, N_GROUPS, HEAD_DIM)),
        "a_k_norm": gain(ks[8], (N_A_LAYERS, N_GROUPS, HEAD_DIM)),
        "a_w_o": nrm(ks[9], (N_A_LAYERS, HD, D_MODEL), HD),
        "kv_norm": gain(ks[10], (D_MODEL,)),
        "kv_w": nrm(ks[11], (D_MODEL, 2 * HD + N_HEADS), D_MODEL),
        "kv_b_f": 0.1 * jax.random.normal(ks[12], (N_HEADS,), f32),
        "kv_k_norm": gain(ks[13], (HEAD_DIM,)),
        "b_w_q": nrm(ks[14], (N_B_LAYERS, D_MODEL, HD), D_MODEL),
        "b_q_norm": gain(ks[15], (N_B_LAYERS, HEAD_DIM)),
        "b_w_o": nrm(ks[16], (N_B_LAYERS, HD, D_MODEL), HD),
    }


def _fwd_reference(x, positions, ffn_norm, ffn_w_in, ffn_w_out, mix_norm, a_w_qkv, a_q_norm, a_k_norm, a_w_o,
              kv_norm, kv_w, kv_b_f, kv_k_norm, b_w_q, b_q_norm, b_w_o):
    h = x
    k_sh = v_sh = cum_sh = None
    for layer in range(DEPTH):
        if layer == N_A_LAYERS:
            k_sh, v_sh, cum_sh = shared_kv(rms_norm(h, kv_norm), kv_w, kv_b_f, kv_k_norm)
        h = h + 0.5 * swiglu(rms_norm(h, ffn_norm[layer, 0]), ffn_w_in[layer, 0], ffn_w_out[layer, 0])
        hn = rms_norm(h, mix_norm[layer])
        if layer < N_A_LAYERS:
            h = h + dilated_mixture_mixer(hn, positions, a_w_qkv[layer], a_q_norm[layer],
                                          a_k_norm[layer], a_w_o[layer])
        else:
            j = layer - N_A_LAYERS
            h = h + forgetting_attention(hn, k_sh, v_sh, cum_sh, b_w_q[j], b_q_norm[j], b_w_o[j])
        h = h + 0.5 * swiglu(rms_norm(h, ffn_norm[layer, 1]), ffn_w_in[layer, 1], ffn_w_out[layer, 1])
    return h


import jax as _jax
import jax.numpy as _jnp

TWIN_FORMAT = 'train_step'
FWD_PARAMS = ['x', 'positions', 'ffn_norm', 'ffn_w_in', 'ffn_w_out', 'mix_norm', 'a_w_qkv', 'a_q_norm', 'a_k_norm', 'a_w_o', 'kv_norm', 'kv_w', 'kv_b_f', 'kv_k_norm', 'b_w_q', 'b_q_norm', 'b_w_o']
TWIN_WEIGHTS = ['ffn_norm', 'ffn_w_in', 'ffn_w_out', 'mix_norm', 'a_w_qkv', 'a_q_norm', 'a_k_norm', 'a_w_o', 'kv_norm', 'kv_w', 'kv_b_f', 'kv_k_norm', 'b_w_q', 'b_q_norm', 'b_w_o']
TWIN_DIFF_INPUT = 'x'
TWIN_INPUTS = ['x', 'positions', 'ffn_norm', 'ffn_w_in', 'ffn_w_out', 'mix_norm', 'a_w_qkv', 'a_q_norm', 'a_k_norm', 'a_w_o', 'kv_norm', 'kv_w', 'kv_b_f', 'kv_k_norm', 'b_w_q', 'b_q_norm', 'b_w_o', 'loss_target', 'm_ffn_norm', 'm_ffn_w_in', 'm_ffn_w_out', 'm_mix_norm', 'm_a_w_qkv', 'm_a_q_norm', 'm_a_k_norm', 'm_a_w_o', 'm_kv_norm', 'm_kv_w', 'm_kv_b_f', 'm_kv_k_norm', 'm_b_w_q', 'm_b_q_norm', 'm_b_w_o', 'v_ffn_norm', 'v_ffn_w_in', 'v_ffn_w_out', 'v_mix_norm', 'v_a_w_qkv', 'v_a_q_norm', 'v_a_k_norm', 'v_a_w_o', 'v_kv_norm', 'v_kv_w', 'v_kv_b_f', 'v_kv_k_norm', 'v_b_w_q', 'v_b_q_norm', 'v_b_w_o']
TWIN_OUTPUTS = ['loss', 'grad_x', 'grad_ffn_norm', 'grad_ffn_w_in', 'grad_ffn_w_out', 'grad_mix_norm', 'grad_a_w_qkv', 'grad_a_q_norm', 'grad_a_k_norm', 'grad_a_w_o', 'grad_kv_norm', 'grad_kv_w', 'grad_kv_b_f', 'grad_kv_k_norm', 'grad_b_w_q', 'grad_b_q_norm', 'grad_b_w_o', 'delta_ffn_norm', 'delta_ffn_w_in', 'delta_ffn_w_out', 'delta_mix_norm', 'delta_a_w_qkv', 'delta_a_q_norm', 'delta_a_k_norm', 'delta_a_w_o', 'delta_kv_norm', 'delta_kv_w', 'delta_kv_b_f', 'delta_kv_k_norm', 'delta_b_w_q', 'delta_b_q_norm', 'delta_b_w_o', 'new_m_ffn_norm', 'new_m_ffn_w_in', 'new_m_ffn_w_out', 'new_m_mix_norm', 'new_m_a_w_qkv', 'new_m_a_q_norm', 'new_m_a_k_norm', 'new_m_a_w_o', 'new_m_kv_norm', 'new_m_kv_w', 'new_m_kv_b_f', 'new_m_kv_k_norm', 'new_m_b_w_q', 'new_m_b_q_norm', 'new_m_b_w_o', 'new_v_ffn_norm', 'new_v_ffn_w_in', 'new_v_ffn_w_out', 'new_v_mix_norm', 'new_v_a_w_qkv', 'new_v_a_q_norm', 'new_v_a_k_norm', 'new_v_a_w_o', 'new_v_kv_norm', 'new_v_kv_w', 'new_v_kv_b_f', 'new_v_kv_k_norm', 'new_v_b_w_q', 'new_v_b_q_norm', 'new_v_b_w_o']
TWIN_LEAF_KINDS = {'loss': 'loss', 'grad_x': 'grad_x', 'grad_ffn_norm': 'grad_w', 'grad_ffn_w_in': 'grad_w', 'grad_ffn_w_out': 'grad_w', 'grad_mix_norm': 'grad_w', 'grad_a_w_qkv': 'grad_w', 'grad_a_q_norm': 'grad_w', 'grad_a_k_norm': 'grad_w', 'grad_a_w_o': 'grad_w', 'grad_kv_norm': 'grad_w', 'grad_kv_w': 'grad_w', 'grad_kv_b_f': 'grad_w', 'grad_kv_k_norm': 'grad_w', 'grad_b_w_q': 'grad_w', 'grad_b_q_norm': 'grad_w', 'grad_b_w_o': 'grad_w', 'delta_ffn_norm': 'delta_w', 'delta_ffn_w_in': 'delta_w', 'delta_ffn_w_out': 'delta_w', 'delta_mix_norm': 'delta_w', 'delta_a_w_qkv': 'delta_w', 'delta_a_q_norm': 'delta_w', 'delta_a_k_norm': 'delta_w', 'delta_a_w_o': 'delta_w', 'delta_kv_norm': 'delta_w', 'delta_kv_w': 'delta_w', 'delta_kv_b_f': 'delta_w', 'delta_kv_k_norm': 'delta_w', 'delta_b_w_q': 'delta_w', 'delta_b_q_norm': 'delta_w', 'delta_b_w_o': 'delta_w', 'new_m_ffn_norm': 'new_m', 'new_m_ffn_w_in': 'new_m', 'new_m_ffn_w_out': 'new_m', 'new_m_mix_norm': 'new_m', 'new_m_a_w_qkv': 'new_m', 'new_m_a_q_norm': 'new_m', 'new_m_a_k_norm': 'new_m', 'new_m_a_w_o': 'new_m', 'new_m_kv_norm': 'new_m', 'new_m_kv_w': 'new_m', 'new_m_kv_b_f': 'new_m', 'new_m_kv_k_norm': 'new_m', 'new_m_b_w_q': 'new_m', 'new_m_b_q_norm': 'new_m', 'new_m_b_w_o': 'new_m', 'new_v_ffn_norm': 'new_v', 'new_v_ffn_w_in': 'new_v', 'new_v_ffn_w_out': 'new_v', 'new_v_mix_norm': 'new_v', 'new_v_a_w_qkv': 'new_v', 'new_v_a_q_norm': 'new_v', 'new_v_a_k_norm': 'new_v', 'new_v_a_w_o': 'new_v', 'new_v_kv_norm': 'new_v', 'new_v_kv_w': 'new_v', 'new_v_kv_b_f': 'new_v', 'new_v_kv_k_norm': 'new_v', 'new_v_b_w_q': 'new_v', 'new_v_b_q_norm': 'new_v', 'new_v_b_w_o': 'new_v'}


def _forward(args):
    return _fwd_reference(*[args[k] for k in FWD_PARAMS])


def _output_shape():
    out = _jax.eval_shape(lambda: _forward(_fwd_setup_inputs(0)))
    return out.shape, out.dtype

N_MICROBATCH = 1
ADAM_LR = 0.001
ADAM_B1 = 0.9
ADAM_B2 = 0.999
ADAM_EPS = 1e-08
ADAM_WD = 0.01
ADAM_STEP = 10
PER_EXAMPLE_BATCH_AXIS = {'x': 0, 'positions': 0, 'loss_target': 0}
SHARED_INPUTS = []
_WEIGHT_DTYPES = {'ffn_norm': _jnp.float32, 'ffn_w_in': _jnp.float32, 'ffn_w_out': _jnp.float32, 'mix_norm': _jnp.float32, 'a_w_qkv': _jnp.float32, 'a_q_norm': _jnp.float32, 'a_k_norm': _jnp.float32, 'a_w_o': _jnp.float32, 'kv_norm': _jnp.float32, 'kv_w': _jnp.float32, 'kv_b_f': _jnp.float32, 'kv_k_norm': _jnp.float32, 'b_w_q': _jnp.float32, 'b_q_norm': _jnp.float32, 'b_w_o': _jnp.float32}
MOMENT_SCALE = {'ffn_norm': 1.218516e+01, 'ffn_w_in': 1.418437e-01, 'ffn_w_out': 2.488856e-01, 'mix_norm': 3.512955e-01, 'a_w_qkv': 9.257089e-02, 'a_q_norm': 1.266416e+00, 'a_k_norm': 1.271879e+00, 'a_w_o': 2.154147e-01, 'kv_norm': 2.980986e+01, 'kv_w': 7.143961e-01, 'kv_b_f': 3.217426e+02, 'kv_k_norm': 4.938753e+01, 'b_w_q': 4.291376e-01, 'b_q_norm': 5.007434e+01, 'b_w_o': 8.868029e-01}


def _to_microbatches(a, axis):
    t = _jnp.moveaxis(a, axis, 0)
    t = t.reshape((N_MICROBATCH, t.shape[0] // N_MICROBATCH) + t.shape[1:])
    return _jnp.moveaxis(t, 1, axis + 1)


def setup_inputs(seed: int = 0) -> dict:
    inp = _fwd_setup_inputs(seed)
    key = _jax.random.fold_in(_jax.random.key(seed), 7919)
    shape, _ = _output_shape()
    out = dict(inp)
    out["loss_target"] = _jax.random.normal(_jax.random.fold_in(key, 0), shape, _jnp.float32)
    for i, name in enumerate(TWIN_WEIGHTS):
        w = inp[name].astype(_jnp.float32)
        if MOMENT_SCALE is None:
            s = _jnp.sqrt(_jnp.mean(_jnp.square(w)) + 1e-30)
        else:
            s = MOMENT_SCALE[name]
        km, kv = _jax.random.split(_jax.random.fold_in(key, i + 1))
        out[name] = w
        out["m_" + name] = s * _jax.random.normal(km, w.shape, _jnp.float32)
        out["v_" + name] = (s * s) * _jax.random.uniform(kv, w.shape, _jnp.float32, 0.5, 1.5)
    if N_MICROBATCH > 1:
        for name, axis in PER_EXAMPLE_BATCH_AXIS.items():
            out[name] = _to_microbatches(out[name], axis)
    return {'x': out['x'], 'positions': out['positions'], 'ffn_norm': out['ffn_norm'], 'ffn_w_in': out['ffn_w_in'], 'ffn_w_out': out['ffn_w_out'], 'mix_norm': out['mix_norm'], 'a_w_qkv': out['a_w_qkv'], 'a_q_norm': out['a_q_norm'], 'a_k_norm': out['a_k_norm'], 'a_w_o': out['a_w_o'], 'kv_norm': out['kv_norm'], 'kv_w': out['kv_w'], 'kv_b_f': out['kv_b_f'], 'kv_k_norm': out['kv_k_norm'], 'b_w_q': out['b_w_q'], 'b_q_norm': out['b_q_norm'], 'b_w_o': out['b_w_o'], 'loss_target': out['loss_target'], 'm_ffn_norm': out['m_ffn_norm'], 'm_ffn_w_in': out['m_ffn_w_in'], 'm_ffn_w_out': out['m_ffn_w_out'], 'm_mix_norm': out['m_mix_norm'], 'm_a_w_qkv': out['m_a_w_qkv'], 'm_a_q_norm': out['m_a_q_norm'], 'm_a_k_norm': out['m_a_k_norm'], 'm_a_w_o': out['m_a_w_o'], 'm_kv_norm': out['m_kv_norm'], 'm_kv_w': out['m_kv_w'], 'm_kv_b_f': out['m_kv_b_f'], 'm_kv_k_norm': out['m_kv_k_norm'], 'm_b_w_q': out['m_b_w_q'], 'm_b_q_norm': out['m_b_q_norm'], 'm_b_w_o': out['m_b_w_o'], 'v_ffn_norm': out['v_ffn_norm'], 'v_ffn_w_in': out['v_ffn_w_in'], 'v_ffn_w_out': out['v_ffn_w_out'], 'v_mix_norm': out['v_mix_norm'], 'v_a_w_qkv': out['v_a_w_qkv'], 'v_a_q_norm': out['v_a_q_norm'], 'v_a_k_norm': out['v_a_k_norm'], 'v_a_w_o': out['v_a_w_o'], 'v_kv_norm': out['v_kv_norm'], 'v_kv_w': out['v_kv_w'], 'v_kv_b_f': out['v_kv_b_f'], 'v_kv_k_norm': out['v_kv_k_norm'], 'v_b_w_q': out['v_b_w_q'], 'v_b_q_norm': out['v_b_q_norm'], 'v_b_w_o': out['v_b_w_o']}


def _loss(weights, diff, rest, loss_target):
    with _jax.named_scope("forward"):
        args = {**rest, TWIN_DIFF_INPUT: diff, **{k: w.astype(_WEIGHT_DTYPES[k]) for k, w in weights.items()}}
        y = _forward(args)
    with _jax.named_scope("loss_head"):
        err = _jnp.square(y.astype(_jnp.float32) - loss_target)
        return 0.5 * _jnp.sum(_jnp.mean(err, axis=-1)) if err.ndim else 0.5 * err


def _adamw(w, g, m, v):
    m = ADAM_B1 * m + (1.0 - ADAM_B1) * g
    v = ADAM_B2 * v + (1.0 - ADAM_B2) * _jnp.square(g)
    m_hat = m / (1.0 - ADAM_B1 ** ADAM_STEP)
    v_hat = v / (1.0 - ADAM_B2 ** ADAM_STEP)
    delta = -ADAM_LR * (m_hat / (_jnp.sqrt(v_hat) + ADAM_EPS) + ADAM_WD * w)
    return delta, m, v


def reference(x, positions, ffn_norm, ffn_w_in, ffn_w_out, mix_norm, a_w_qkv, a_q_norm, a_k_norm, a_w_o, kv_norm, kv_w, kv_b_f, kv_k_norm, b_w_q, b_q_norm, b_w_o, loss_target, m_ffn_norm, m_ffn_w_in, m_ffn_w_out, m_mix_norm, m_a_w_qkv, m_a_q_norm, m_a_k_norm, m_a_w_o, m_kv_norm, m_kv_w, m_kv_b_f, m_kv_k_norm, m_b_w_q, m_b_q_norm, m_b_w_o, v_ffn_norm, v_ffn_w_in, v_ffn_w_out, v_mix_norm, v_a_w_qkv, v_a_q_norm, v_a_k_norm, v_a_w_o, v_kv_norm, v_kv_w, v_kv_b_f, v_kv_k_norm, v_b_w_q, v_b_q_norm, v_b_w_o):
    given = dict(x=x, positions=positions, ffn_norm=ffn_norm, ffn_w_in=ffn_w_in, ffn_w_out=ffn_w_out, mix_norm=mix_norm, a_w_qkv=a_w_qkv, a_q_norm=a_q_norm, a_k_norm=a_k_norm, a_w_o=a_w_o, kv_norm=kv_norm, kv_w=kv_w, kv_b_f=kv_b_f, kv_k_norm=kv_k_norm, b_w_q=b_w_q, b_q_norm=b_q_norm, b_w_o=b_w_o, loss_target=loss_target, m_ffn_norm=m_ffn_norm, m_ffn_w_in=m_ffn_w_in, m_ffn_w_out=m_ffn_w_out, m_mix_norm=m_mix_norm, m_a_w_qkv=m_a_w_qkv, m_a_q_norm=m_a_q_norm, m_a_k_norm=m_a_k_norm, m_a_w_o=m_a_w_o, m_kv_norm=m_kv_norm, m_kv_w=m_kv_w, m_kv_b_f=m_kv_b_f, m_kv_k_norm=m_kv_k_norm, m_b_w_q=m_b_w_q, m_b_q_norm=m_b_q_norm, m_b_w_o=m_b_w_o, v_ffn_norm=v_ffn_norm, v_ffn_w_in=v_ffn_w_in, v_ffn_w_out=v_ffn_w_out, v_mix_norm=v_mix_norm, v_a_w_qkv=v_a_w_qkv, v_a_q_norm=v_a_q_norm, v_a_k_norm=v_a_k_norm, v_a_w_o=v_a_w_o, v_kv_norm=v_kv_norm, v_kv_w=v_kv_w, v_kv_b_f=v_kv_b_f, v_kv_k_norm=v_kv_k_norm, v_b_w_q=v_b_w_q, v_b_q_norm=v_b_q_norm, v_b_w_o=v_b_w_o)
    weights = {n: given[n] for n in TWIN_WEIGHTS}
    shared = {n: given[n] for n in SHARED_INPUTS}
    per_example = {n: given[n] for n in ['x', 'positions']}
    grad_fn = _jax.value_and_grad(_loss, argnums=(0, 1))

    def one_microbatch(ex, loss_target):
        ex = dict(ex)
        diff = ex.pop(TWIN_DIFF_INPUT)
        return grad_fn(weights, diff, {**shared, **ex}, loss_target)

    if N_MICROBATCH == 1:
        loss, (grad_w, grad_x) = one_microbatch(per_example, given["loss_target"])
    else:
        def body(carry, xs):
            loss_sum, grad_sum = carry
            l_k, (gw_k, gx_k) = one_microbatch(xs[0], xs[1])
            with _jax.named_scope("update"):
                return (loss_sum + l_k, _jax.tree.map(_jnp.add, grad_sum, gw_k)), gx_k

        init = (_jnp.zeros((), _jnp.float32), _jax.tree.map(_jnp.zeros_like, weights))
        (loss, grad_w), grad_x = _jax.lax.scan(body, init, (per_example, given["loss_target"]))
    with _jax.named_scope("update"):
        delta_w, new_m, new_v = {}, {}, {}
        for n in TWIN_WEIGHTS:
            delta_w[n], new_m[n], new_v[n] = _adamw(weights[n], grad_w[n], given["m_" + n], given["v_" + n])
    return (loss, grad_x, *[grad_w[n] for n in TWIN_WEIGHTS], *[delta_w[n] for n in TWIN_WEIGHTS],
            *[new_m[n] for n in TWIN_WEIGHTS], *[new_v[n] for n in TWIN_WEIGHTS])
```

```python
import functools
from typing import NamedTuple

import jax
import jax.numpy as jnp
from jax import lax
from jax.experimental import pallas as pl
from jax.experimental.pallas import tpu as pltpu

F32 = jnp.float32
BF16 = jnp.bfloat16

HEAD_DIM = 64
LANES = 128
SUBLANES = 8
ROT_DIM = HEAD_DIM // 4
ROPE_THETA = 500000.0
NORM_EPS = 1e-6
BAND = 128
DILATIONS = (1, 4, 16)
NEG = -1e30
Q_SCALE = HEAD_DIM ** -0.5
N_CHIPS = 4
N_DEV = 8
VMEM_LIMIT = 48 * 1024 * 1024

ADAM_LR = 0.001
ADAM_B1 = 0.9
ADAM_B2 = 0.999
ADAM_EPS = 1e-08
ADAM_WD = 0.01
ADAM_STEP = 10


class Cfg(NamedTuple):
    d_model: int
    d_ff: int
    seq: int
    batch: int

    @property
    def heads(self):
        return self.d_model // HEAD_DIM

    @property
    def tokens(self):
        return self.batch * self.seq

    @property
    def pairs(self):
        return self.d_model // LANES


def _params(sem):
    return pltpu.CompilerParams(dimension_semantics=sem, vmem_limit_bytes=VMEM_LIMIT)


def _blk(dim, want):
    if dim <= want:
        return dim
    for b in range(want // LANES * LANES, 0, -LANES):
        if dim % b == 0:
            return b
    b = want
    while dim % b:
        b //= 2
    return b


def _mm(a, b, *, form, out_dtype, name, bm=1024, bn=1024, bk=1024, res=None, scale=1.0):
    if form == "F":
        m, kdim = a.shape
        jn, _, ns = b.shape
        bm, bn, bk = _blk(m, bm), _blk(ns, bn), _blk(kdim, bk)
        npj = ns // bn
        grid = (m // bm, jn * npj, kdim // bk)
        a_spec = pl.BlockSpec((bm, bk), lambda i, n, k: (i, k))
        b_spec = pl.BlockSpec((None, bk, bn), lambda i, n, k: (n // npj, k, n % npj))
        o_spec = pl.BlockSpec((bm, bn), lambda i, n, k: (i, n))
        o_shape = jax.ShapeDtypeStruct((m, jn * ns), out_dtype)
        dims = (((1,), (0,)), ((), ()))
    elif form == "B":
        m = a.shape[0]
        jn, kdim, ns = b.shape
        bm, bn, bk = _blk(m, bm), _blk(kdim, bn), _blk(ns, bk)
        kpj = ns // bk
        grid = (m // bm, kdim // bn, jn * kpj)
        a_spec = pl.BlockSpec((bm, bk), lambda i, n, k: (i, k))
        b_spec = pl.BlockSpec((None, bn, bk), lambda i, n, k: (k // kpj, n, k % kpj))
        o_spec = pl.BlockSpec((bm, bn), lambda i, n, k: (i, n))
        o_shape = jax.ShapeDtypeStruct((m, kdim), out_dtype)
        dims = (((1,), (1,)), ((), ()))
    else:
        raise ValueError(form)
    nk = grid[2]

    def body(*refs):
        if res is None:
            a_ref, b_ref, o_ref, acc_ref = refs
            r_ref = None
        else:
            a_ref, b_ref, r_ref, o_ref, acc_ref = refs
        k = pl.program_id(2)

        @pl.when(k == 0)
        def _():
            acc_ref[...] = jnp.zeros_like(acc_ref)

        acc_ref[...] += lax.dot_general(a_ref[...].astype(BF16), b_ref[...].astype(BF16), dims,
                                        preferred_element_type=F32)

        @pl.when(k == nk - 1)
        def _():
            r = acc_ref[...]
            if scale != 1.0:
                r = r * scale
            if r_ref is not None:
                r = r_ref[...] + r
            o_ref[...] = r.astype(o_ref.dtype)

    in_specs = [a_spec, b_spec]
    args = [a, b]
    if res is not None:
        in_specs.append(pl.BlockSpec((bm, bn), lambda i, n, k: (i, n)))
        args.append(res)
    return pl.pallas_call(
        body, name=name, grid=grid, in_specs=in_specs, out_specs=o_spec, out_shape=o_shape,
        scratch_shapes=[pltpu.VMEM((bm, bn), F32)],
        compiler_params=_params(("parallel", "parallel", "arbitrary")),
    )(*args)


def _mm_grad(a, dy, jn, *, name, scale=1.0, bm=1024, bn=1024, bk=1024):
    t, kdim = a.shape
    ns = dy.shape[1] // jn
    bm, bn, bk = _blk(kdim, bm), _blk(ns, bn), _blk(t, bk)
    npj = ns // bn
    grid = (kdim // bm, jn * npj, t // bk)
    nk = grid[2]
    dims = (((0,), (0,)), ((), ()))

    def body(a_ref, b_ref, o_ref, acc_ref):
        k = pl.program_id(2)

        @pl.when(k == 0)
        def _():
            acc_ref[...] = jnp.zeros_like(acc_ref)

        acc_ref[...] += lax.dot_general(a_ref[...].astype(BF16), b_ref[...].astype(BF16), dims,
                                        preferred_element_type=F32)

        @pl.when(k == nk - 1)
        def _():
            r = acc_ref[...]
            if scale != 1.0:
                r = r * scale
            o_ref[...] = r

    return pl.pallas_call(
        body, name=name, grid=grid,
        in_specs=[pl.BlockSpec((bk, bm), lambda m, n, k: (k, m)),
                  pl.BlockSpec((bk, bn), lambda m, n, k: (k, n))],
        out_specs=pl.BlockSpec((None, bm, bn), lambda m, n, k: (n // npj, m, n % npj)),
        out_shape=jax.ShapeDtypeStruct((jn, kdim, ns), F32),
        scratch_shapes=[pltpu.VMEM((bm, bn), F32)],
        compiler_params=_params(("parallel", "parallel", "arbitrary")),
    )(a, dy)


ROW_BLOCK = 512


def _fold8(x):
    return jnp.sum(x.reshape(x.shape[0] // SUBLANES, SUBLANES, x.shape[1]), axis=0)


def _rms_fwd(x, g, *, name):
    t, d = x.shape
    tr = _blk(t, ROW_BLOCK)

    def body(x_ref, g_ref, o_ref):
        xv = x_ref[...]
        rstd = lax.rsqrt(jnp.mean(xv * xv, axis=-1, keepdims=True) + NORM_EPS)
        o_ref[...] = ((xv * rstd) * g_ref[...]).astype(BF16)

    return pl.pallas_call(
        body, name=name, grid=(t // tr,),
        in_specs=[pl.BlockSpec((tr, d), lambda i: (i, 0)), pl.BlockSpec((1, d), lambda i: (0, 0))],
        out_specs=pl.BlockSpec((tr, d), lambda i: (i, 0)),
        out_shape=jax.ShapeDtypeStruct((t, d), BF16),
        compiler_params=_params(("parallel",)),
    )(x, g.reshape(1, d))


def _rms_bwd(x, g, dy, dres, *, name):
    t, d = x.shape
    tr = _blk(t, ROW_BLOCK)

    def body(x_ref, g_ref, dy_ref, dres_ref, dx_ref, dg_ref):
        i = pl.program_id(0)
        xv = x_ref[...]
        rstd = lax.rsqrt(jnp.mean(xv * xv, axis=-1, keepdims=True) + NORM_EPS)
        xhat = xv * rstd
        dyv = dy_ref[...]
        dyg = dyv * g_ref[...]
        proj = jnp.mean(dyg * xhat, axis=-1, keepdims=True)
        dx_ref[...] = dres_ref[...] + rstd * (dyg - xhat * proj)

        @pl.when(i == 0)
        def _():
            dg_ref[...] = jnp.zeros_like(dg_ref)

        dg_ref[...] += _fold8(dyv * xhat)

    dx, dg = pl.pallas_call(
        body, name=name, grid=(t // tr,),
        in_specs=[pl.BlockSpec((tr, d), lambda i: (i, 0)), pl.BlockSpec((1, d), lambda i: (0, 0)),
                  pl.BlockSpec((tr, d), lambda i: (i, 0)), pl.BlockSpec((tr, d), lambda i: (i, 0))],
        out_specs=[pl.BlockSpec((tr, d), lambda i: (i, 0)), pl.BlockSpec((SUBLANES, d), lambda i: (0, 0))],
        out_shape=[jax.ShapeDtypeStruct((t, d), F32), jax.ShapeDtypeStruct((SUBLANES, d), F32)],
        compiler_params=_params(("arbitrary",)),
    )(x, g.reshape(1, d), dy, dres)
    return dx, jnp.sum(dg, axis=0)


def _swiglu_fwd(u, *, name):
    t, f2 = u.shape
    f = f2 // 2
    tr = _blk(t, 256)

    def body(g_ref, u_ref, o_ref):
        gv = g_ref[...]
        o_ref[...] = (gv * jax.nn.sigmoid(gv) * u_ref[...]).astype(BF16)

    return pl.pallas_call(
        body, name=name, grid=(t // tr,),
        in_specs=[pl.BlockSpec((tr, f), lambda i: (i, 0)), pl.BlockSpec((tr, f), lambda i: (i, 1))],
        out_specs=pl.BlockSpec((tr, f), lambda i: (i, 0)),
        out_shape=jax.ShapeDtypeStruct((t, f), BF16),
        compiler_params=_params(("parallel",)),
    )(u, u)


def _swiglu_bwd(u, da, *, name):
    t, f2 = u.shape
    f = f2 // 2
    tr = _blk(t, 256)

    def body(g_ref, u_ref, da_ref, o_ref):
        gv = g_ref[...]
        sg = jax.nn.sigmoid(gv)
        silu = gv * sg
        dav = da_ref[...]
        o_ref[:, :f] = (dav * u_ref[...] * (sg + silu * (1.0 - sg))).astype(BF16)
        o_ref[:, f:] = (dav * silu).astype(BF16)

    return pl.pallas_call(
        body, name=name, grid=(t // tr,),
        in_specs=[pl.BlockSpec((tr, f), lambda i: (i, 0)), pl.BlockSpec((tr, f), lambda i: (i, 1)),
                  pl.BlockSpec((tr, f), lambda i: (i, 0))],
        out_specs=pl.BlockSpec((tr, f2), lambda i: (i, 0)),
        out_shape=jax.ShapeDtypeStruct((t, f2), BF16),
        compiler_params=_params(("parallel",)),
    )(u, u, da)


def _loss_fwd_bwd(h, target, *, name):
    t, d = h.shape
    tr = _blk(t, ROW_BLOCK)

    def body(h_ref, t_ref, dh_ref, l_ref):
        i = pl.program_id(0)
        err = h_ref[...] - t_ref[...]
        dh_ref[...] = err * (1.0 / d)

        @pl.when(i == 0)
        def _():
            l_ref[...] = jnp.zeros_like(l_ref)

        l_ref[...] += _fold8(err * err)

    dh, part = pl.pallas_call(
        body, name=name, grid=(t // tr,),
        in_specs=[pl.BlockSpec((tr, d), lambda i: (i, 0)), pl.BlockSpec((tr, d), lambda i: (i, 0))],
        out_specs=[pl.BlockSpec((tr, d), lambda i: (i, 0)), pl.BlockSpec((SUBLANES, d), lambda i: (0, 0))],
        out_shape=[jax.ShapeDtypeStruct((t, d), F32), jax.ShapeDtypeStruct((SUBLANES, d), F32)],
        compiler_params=_params(("arbitrary",)),
    )(h, target)
    return jnp.sum(part) * (0.5 / d), dh


def _seg_matrix():
    r = lax.broadcasted_iota(jnp.int32, (LANES, LANES), 0) // HEAD_DIM
    c = lax.broadcasted_iota(jnp.int32, (LANES, LANES), 1) // HEAD_DIM
    return (r == c).astype(BF16)


def _head_sum(x, seg):
    hi = x.astype(BF16)
    r1 = x - hi.astype(F32)
    mid = r1.astype(BF16)
    lo = (r1 - mid.astype(F32)).astype(BF16)
    dot = functools.partial(jnp.dot, preferred_element_type=F32)
    return dot(hi, seg) + dot(mid, seg) + dot(lo, seg)


def _lane_in_head(shape):
    return lax.broadcasted_iota(jnp.int32, shape, 1) % HEAD_DIM


def _rot_partner(x):
    up = pltpu.roll(x, LANES - ROT_DIM // 2, 1)
    down = pltpu.roll(x, ROT_DIM // 2, 1)
    return jnp.where(_lane_in_head(x.shape) < ROT_DIM // 2, up, down)


def _rope_tables(positions):
    inv_freq = ROPE_THETA ** (-jnp.arange(0, ROT_DIM, 2, dtype=F32) / ROT_DIM)
    ang = positions.astype(F32)[:, None] * inv_freq
    t = ang.shape[0]
    rest = HEAD_DIM - ROT_DIM
    cos = jnp.concatenate([jnp.cos(ang), jnp.cos(ang), jnp.ones((t, rest), F32)], axis=1)
    sin = jnp.concatenate([-jnp.sin(ang), jnp.sin(ang), jnp.zeros((t, rest), F32)], axis=1)
    return jnp.tile(cos, (1, LANES // HEAD_DIM)), jnp.tile(sin, (1, LANES // HEAD_DIM))


def _kind_is(j, kinds, kind):
    hits = [j == jj for jj, k in enumerate(kinds) if k == kind]
    return functools.reduce(jnp.logical_or, hits) if hits else None


def _hn_fwd(x, gains, kinds, d, cos, sin, *, name):
    t = x.shape[0]
    n = len(kinds)
    tr = _blk(t, ROW_BLOCK)
    seg = _seg_matrix()
    g8 = jnp.repeat(gains.astype(F32), SUBLANES, axis=0)

    def body(x_ref, g_ref, seg_ref, cos_ref, sin_ref, o_ref):
        j = pl.program_id(1)

        def normed(rope):
            for c in range(d // LANES):
                sl = slice(c * LANES, (c + 1) * LANES)
                xv = x_ref[:, sl]
                ms = _head_sum(xv * xv, seg_ref[...]) * (1.0 / HEAD_DIM)
                y = (xv * lax.rsqrt(ms + NORM_EPS)) * g_ref[0:1, sl]
                if rope:
                    y = y * cos_ref[...] + _rot_partner(y) * sin_ref[...]
                o_ref[:, sl] = y.astype(BF16)

        for kind in ("rope", "norm"):
            hit = _kind_is(j, kinds, kind)
            if hit is not None:
                pl.when(hit)(functools.partial(normed, kind == "rope"))
        hit = _kind_is(j, kinds, "cast")
        if hit is not None:
            @pl.when(hit)
            def _():
                o_ref[...] = x_ref[...].astype(BF16)

    return pl.pallas_call(
        body, name=name, grid=(t // tr, n),
        in_specs=[pl.BlockSpec((tr, d), lambda i, j: (i, j)), pl.BlockSpec((SUBLANES, d), lambda i, j: (j, 0)),
                  pl.BlockSpec((LANES, LANES), lambda i, j: (0, 0)),
                  pl.BlockSpec((tr, LANES), lambda i, j: (i, 0)), pl.BlockSpec((tr, LANES), lambda i, j: (i, 0))],
        out_specs=pl.BlockSpec((tr, d), lambda i, j: (i, j)),
        out_shape=jax.ShapeDtypeStruct((t, n * d), BF16),
        compiler_params=_params(("parallel", "parallel")),
    )(x, g8, seg, cos, sin)


def _hn_bwd(x, dy, gains, kinds, d, cos, sin, *, name):
    t = x.shape[0]
    n = len(kinds)
    tr = _blk(t, ROW_BLOCK)
    seg = _seg_matrix()
    g8 = jnp.repeat(gains.astype(F32), SUBLANES, axis=0)

    def body(x_ref, dy_ref, g_ref, seg_ref, cos_ref, sin_ref, dx_ref, dg_ref):
        j = pl.program_id(0)
        i = pl.program_id(1)

        @pl.when(i == 0)
        def _():
            dg_ref[...] = jnp.zeros_like(dg_ref)

        def normed(rope):
            for c in range(d // LANES):
                sl = slice(c * LANES, (c + 1) * LANES)
                xv = x_ref[:, sl]
                dyv = dy_ref[:, sl]
                if rope:
                    dyv = dyv * cos_ref[...] - _rot_partner(dyv) * sin_ref[...]
                ms = _head_sum(xv * xv, seg_ref[...]) * (1.0 / HEAD_DIM)
                rstd = lax.rsqrt(ms + NORM_EPS)
                xhat = xv * rstd
                dg_ref[:, sl] += _fold8(dyv * xhat)
                dyg = dyv * g_ref[0:1, sl]
                proj = _head_sum(dyg * xhat, seg_ref[...]) * (1.0 / HEAD_DIM)
                dx_ref[:, sl] = (rstd * (dyg - xhat * proj)).astype(BF16)

        for kind in ("rope", "norm"):
            hit = _kind_is(j, kinds, kind)
            if hit is not None:
                pl.when(hit)(functools.partial(normed, kind == "rope"))
        hit = _kind_is(j, kinds, "cast")
        if hit is not None:
            @pl.when(hit)
            def _():
                dx_ref[...] = dy_ref[...].astype(BF16)

    dx, dg = pl.pallas_call(
        body, name=name, grid=(n, t // tr),
        in_specs=[pl.BlockSpec((tr, d), lambda j, i: (i, j)), pl.BlockSpec((tr, d), lambda j, i: (i, j)),
                  pl.BlockSpec((SUBLANES, d), lambda j, i: (j, 0)),
                  pl.BlockSpec((LANES, LANES), lambda j, i: (0, 0)),
                  pl.BlockSpec((tr, LANES), lambda j, i: (i, 0)), pl.BlockSpec((tr, LANES), lambda j, i: (i, 0))],
        out_specs=[pl.BlockSpec((tr, d), lambda j, i: (i, j)), pl.BlockSpec((SUBLANES, d), lambda j, i: (j, 0))],
        out_shape=[jax.ShapeDtypeStruct((t, n * d), BF16), jax.ShapeDtypeStruct((n * SUBLANES, d), F32)],
        compiler_params=_params(("arbitrary", "arbitrary")),
    )(x, dy, g8, seg, cos, sin)
    dg = dg.reshape(n, SUBLANES, d // HEAD_DIM, HEAD_DIM).sum(axis=(1, 2))
    return dx, dg


def _head_dot(a, b, *, name):
    t, d = a.shape
    tr = _blk(t, ROW_BLOCK)
    seg = _seg_matrix()

    def body(a_ref, b_ref, seg_ref, o_ref):
        for c in range(d // LANES):
            sl = slice(c * LANES, (c + 1) * LANES)
            o_ref[:, sl] = _head_sum(a_ref[:, sl].astype(BF16).astype(F32) * b_ref[:, sl], seg_ref[...])

    return pl.pallas_call(
        body, name=name, grid=(t // tr,),
        in_specs=[pl.BlockSpec((tr, d), lambda i: (i, 0)), pl.BlockSpec((tr, d), lambda i: (i, 0)),
                  pl.BlockSpec((LANES, LANES), lambda i: (0, 0))],
        out_specs=pl.BlockSpec((tr, d), lambda i: (i, 0)),
        out_shape=jax.ShapeDtypeStruct((t, d), F32),
        compiler_params=_params(("parallel",)),
    )(a, b, seg)


def _half_mask(shape):
    return lax.broadcasted_iota(jnp.int32, shape, 1) < HEAD_DIM


def _band_valid(first):
    qi = lax.broadcasted_iota(jnp.int32, (BAND, 2 * BAND), 0)
    kj = lax.broadcasted_iota(jnp.int32, (BAND, 2 * BAND), 1)
    dist = qi + BAND - kj
    return (dist >= 0) & (dist <= BAND) & ((kj >= BAND) | jnp.logical_not(first))


_NT = (((1,), (1,)), ((), ()))
_TN = (((0,), (0,)), ((), ()))


def _dot2(p, v):
    hi = p.astype(BF16)
    lo = (p - hi.astype(F32)).astype(BF16)
    return jnp.dot(hi, v, preferred_element_type=F32) + jnp.dot(lo, v, preferred_element_type=F32)


def _band_fwd(qkv, g, dil, cfg, *, name):
    t, d = cfg.tokens, cfg.d_model
    w = 9 * d
    rows = t // dil
    nbt = rows // BAND
    nb = cfg.seq // (dil * BAND)
    view = qkv.reshape(rows, dil * w)
    ncol = w // d

    def body(q_ref, kp_ref, kc_ref, vp_ref, vc_ref, o_ref, lse_ref):
        i = pl.program_id(1)
        valid = _band_valid(i % nb == 0)
        half = _half_mask((BAND, LANES))
        for hp in range(d // LANES):
            sl = slice(hp * LANES, (hp + 1) * LANES)
            q2 = q_ref[:, sl]
            kk = jnp.concatenate([kp_ref[:, sl], kc_ref[:, sl]], axis=0)
            vv = jnp.concatenate([vp_ref[:, sl], vc_ref[:, sl]], axis=0)
            outs, lses = [], []
            for e in range(2):
                qe = jnp.where(half == (e == 0), q2, jnp.zeros_like(q2))
                s = lax.dot_general(qe, kk, _NT, preferred_element_type=F32)
                s = jnp.where(valid, s, NEG)
                m = jnp.max(s, axis=1, keepdims=True)
                p = jnp.exp(s - m)
                l = jnp.sum(p, axis=1, keepdims=True)
                outs.append(_dot2(p * (1.0 / l), vv))
                lses.append(m + jnp.log(l))
            o_ref[:, sl] = jnp.where(half, outs[0], outs[1])
            lse_ref[:, sl] = jnp.where(half, lses[0], lses[1])

    def col(which):
        return lambda r, i: (i, r * ncol + 3 * g + which)

    def col_prev(which):
        return lambda r, i: (jnp.maximum(i - 1, 0), r * ncol + 3 * g + which)

    blk = (BAND, d)
    o, lse = pl.pallas_call(
        body, name=name, grid=(dil, nbt),
        in_specs=[pl.BlockSpec(blk, col(0)), pl.BlockSpec(blk, col_prev(1)), pl.BlockSpec(blk, col(1)),
                  pl.BlockSpec(blk, col_prev(2)), pl.BlockSpec(blk, col(2))],
        out_specs=[pl.BlockSpec(blk, lambda r, i: (i, r)), pl.BlockSpec(blk, lambda r, i: (i, r))],
        out_shape=[jax.ShapeDtypeStruct((rows, dil * d), F32), jax.ShapeDtypeStruct((rows, dil * d), F32)],
        compiler_params=_params(("parallel", "arbitrary")),
    )(view, view, view, view, view)
    return o.reshape(t, d), lse.reshape(t, d)


def _band_bwd(qkv, dmixed, lse_all, dsum, g, dil, cfg, *, name):
    t, d = cfg.tokens, cfg.d_model
    w = 9 * d
    rows = t // dil
    nbt = rows // BAND
    nb = cfg.seq // (dil * BAND)
    view = qkv.reshape(rows, dil * w)
    ncol = w // d
    do_v, l_v, d_v = (z.reshape(rows, dil * d) for z in (dmixed, lse_all, dsum))

    def body(q_ref, kp_ref, kc_ref, vp_ref, vc_ref, do_ref, l_ref, ds_ref, dq_ref, dk_ref, dv_ref, ck_ref, cv_ref):
        i = pl.program_id(1)

        @pl.when(i < nbt)
        def _():
            valid = _band_valid(i % nb == 0)
            half = _half_mask((BAND, LANES))
            half2 = _half_mask((2 * BAND, LANES))
            for hp in range(d // LANES):
                sl = slice(hp * LANES, (hp + 1) * LANES)
                q2 = q_ref[:, sl]
                kk = jnp.concatenate([kp_ref[:, sl], kc_ref[:, sl]], axis=0)
                vv = jnp.concatenate([vp_ref[:, sl], vc_ref[:, sl]], axis=0)
                do2 = do_ref[:, sl].astype(BF16)
                dqs, dks, dvs = [], [], []
                for e in range(2):
                    lane0 = e * HEAD_DIM
                    keep = half == (e == 0)
                    qe = jnp.where(keep, q2, jnp.zeros_like(q2))
                    doe = jnp.where(keep, do2, jnp.zeros_like(do2))
                    s = lax.dot_general(qe, kk, _NT, preferred_element_type=F32)
                    s = jnp.where(valid, s, NEG)
                    p = jnp.exp(s - l_ref[:, hp * LANES + lane0:hp * LANES + lane0 + 1])
                    dp = lax.dot_general(doe, vv, _NT, preferred_element_type=F32)
                    dsc = (p * (dp - ds_ref[:, hp * LANES + lane0:hp * LANES + lane0 + 1])).astype(BF16)
                    dqs.append(jnp.dot(dsc, kk, preferred_element_type=F32))
                    dks.append(lax.dot_general(dsc, q2, _TN, preferred_element_type=F32))
                    dvs.append(lax.dot_general(p.astype(BF16), do2, _TN, preferred_element_type=F32))
                dq_ref[:, sl] = jnp.where(half, dqs[0], dqs[1])
                dkk = jnp.where(half2, dks[0], dks[1])
                dvv = jnp.where(half2, dvs[0], dvs[1])

                @pl.when(i > 0)
                def _():
                    dk_ref[:, sl] = ck_ref[:, sl] + dkk[:BAND]
                    dv_ref[:, sl] = cv_ref[:, sl] + dvv[:BAND]

                ck_ref[:, sl] = dkk[BAND:]
                cv_ref[:, sl] = dvv[BAND:]

        @pl.when(i == nbt)
        def _():
            dk_ref[...] = ck_ref[...]
            dv_ref[...] = cv_ref[...]

    def cur(i):
        return jnp.minimum(i, nbt - 1)

    def col(which):
        return lambda r, i: (cur(i), r * ncol + 3 * g + which)

    def col_prev(which):
        return lambda r, i: (jnp.maximum(cur(i) - 1, 0), r * ncol + 3 * g + which)

    blk = (BAND, d)
    here = pl.BlockSpec(blk, lambda r, i: (cur(i), r))
    behind = pl.BlockSpec(blk, lambda r, i: (jnp.maximum(i - 1, 0), r))
    shape = jax.ShapeDtypeStruct((rows, dil * d), F32)
    dq, dk, dv = pl.pallas_call(
        body, name=name, grid=(dil, nbt + 1),
        in_specs=[pl.BlockSpec(blk, col(0)), pl.BlockSpec(blk, col_prev(1)), pl.BlockSpec(blk, col(1)),
                  pl.BlockSpec(blk, col_prev(2)), pl.BlockSpec(blk, col(2)), here, here, here],
        out_specs=[here, behind, behind],
        out_shape=[shape, shape, shape],
        scratch_shapes=[pltpu.VMEM(blk, F32), pltpu.VMEM(blk, F32)],
        compiler_params=_params(("arbitrary", "arbitrary")),
    )(view, view, view, view, view, do_v, l_v, d_v)
    return dq.reshape(t, d), dk.reshape(t, d), dv.reshape(t, d)


def _mix_fwd(outs, lses, *, name):
    t, d = outs[0].shape
    tr = _blk(t, ROW_BLOCK)
    ng = len(outs)

    def body(*refs):
        o_refs, l_refs = refs[:ng], refs[ng:2 * ng]
        mixed_ref, lse_ref = refs[2 * ng:]
        ls = [r[...] for r in l_refs]
        m = functools.reduce(jnp.maximum, ls)
        es = [jnp.exp(l - m) for l in ls]
        tot = functools.reduce(jnp.add, es)
        inv = 1.0 / tot
        mixed_ref[...] = functools.reduce(jnp.add, [(e * inv) * r[...] for e, r in zip(es, o_refs)])
        lse_ref[...] = m + jnp.log(tot)

    spec = pl.BlockSpec((tr, d), lambda i: (i, 0))
    return pl.pallas_call(
        body, name=name, grid=(t // tr,),
        in_specs=[spec] * (2 * ng), out_specs=[spec, spec],
        out_shape=[jax.ShapeDtypeStruct((t, d), F32), jax.ShapeDtypeStruct((t, d), F32)],
        compiler_params=_params(("parallel",)),
    )(*outs, *lses)


GATE_BLOCK = 256


def _tri(n, upper):
    r = lax.broadcasted_iota(jnp.int32, (n, n), 0)
    c = lax.broadcasted_iota(jnp.int32, (n, n), 1)
    return ((c >= r) if upper else (c <= r)).astype(BF16)


def _tri_dot(tri, x):
    hi = x.astype(BF16)
    r1 = x - hi.astype(F32)
    mid = r1.astype(BF16)
    lo = (r1 - mid.astype(F32)).astype(BF16)
    dot = functools.partial(jnp.dot, preferred_element_type=F32)
    return dot(tri, hi) + dot(tri, mid) + dot(tri, lo)


def _log_sigmoid(z):
    return jnp.minimum(z, 0.0) - jnp.log(1.0 + jnp.exp(-jnp.abs(z)))


def _gate_fwd(proj, col_block, bias, cfg, *, name):
    tr = _blk(cfg.seq, GATE_BLOCK)
    nblk = cfg.seq // tr

    def body(z_ref, b_ref, tri_ref, o_ref, carry_ref):
        i = pl.program_id(1)

        @pl.when(i == 0)
        def _():
            carry_ref[...] = jnp.zeros_like(carry_ref)

        logf = _log_sigmoid(z_ref[...] + b_ref[0:1, :])
        cum = _tri_dot(tri_ref[...], logf) + carry_ref[0:1, :]
        o_ref[...] = cum
        carry_ref[...] = jnp.broadcast_to(cum[tr - 1:tr, :], carry_ref.shape)

    return pl.pallas_call(
        body, name=name, grid=(cfg.batch, nblk),
        in_specs=[pl.BlockSpec((tr, LANES), lambda b, i: (b * nblk + i, col_block)),
                  pl.BlockSpec((SUBLANES, LANES), lambda b, i: (0, 0)),
                  pl.BlockSpec((tr, tr), lambda b, i: (0, 0))],
        out_specs=pl.BlockSpec((tr, LANES), lambda b, i: (b * nblk + i, 0)),
        out_shape=jax.ShapeDtypeStruct((cfg.tokens, LANES), F32),
        scratch_shapes=[pltpu.VMEM((SUBLANES, LANES), F32)],
        compiler_params=_params(("arbitrary", "arbitrary")),
    )(proj, jnp.broadcast_to(bias, (SUBLANES, LANES)), _tri(tr, upper=False))


def _gate_bwd(proj, col_block, bias, dcum, cfg, *, name):
    tr = _blk(cfg.seq, GATE_BLOCK)
    nblk = cfg.seq // tr

    def body(z_ref, b_ref, tri_ref, dc_ref, dz_ref, db_ref, carry_ref):
        b = pl.program_id(0)
        i = pl.program_id(1)

        @pl.when(i == 0)
        def _():
            carry_ref[...] = jnp.zeros_like(carry_ref)

        @pl.when((i == 0) & (b == 0))
        def _():
            db_ref[...] = jnp.zeros_like(db_ref)

        dcv = dc_ref[...]
        dlogf = _tri_dot(tri_ref[...], dcv) + carry_ref[0:1, :]
        carry_ref[...] = jnp.broadcast_to(dlogf[0:1, :], carry_ref.shape)
        dz = dlogf * jax.nn.sigmoid(-(z_ref[...] + b_ref[0:1, :]))
        dz_ref[...] = dz
        db_ref[...] += _fold8(dz)

    def rev(b, i):
        return (b * nblk + nblk - 1 - i, 0)

    dz, db = pl.pallas_call(
        body, name=name, grid=(cfg.batch, nblk),
        in_specs=[pl.BlockSpec((tr, LANES), lambda b, i: (b * nblk + nblk - 1 - i, col_block)),
                  pl.BlockSpec((SUBLANES, LANES), lambda b, i: (0, 0)),
                  pl.BlockSpec((tr, tr), lambda b, i: (0, 0)),
                  pl.BlockSpec((tr, LANES), rev)],
        out_specs=[pl.BlockSpec((tr, LANES), rev), pl.BlockSpec((SUBLANES, LANES), lambda b, i: (0, 0))],
        out_shape=[jax.ShapeDtypeStruct((cfg.tokens, LANES), F32), jax.ShapeDtypeStruct((SUBLANES, LANES), F32)],
        scratch_shapes=[pltpu.VMEM((SUBLANES, LANES), F32)],
        compiler_params=_params(("arbitrary", "arbitrary")),
    )(proj, jnp.broadcast_to(bias, (SUBLANES, LANES)), _tri(tr, upper=True), dcum)
    return dz, jnp.sum(db, axis=0)


FOX_BLOCK = 256


def _fox_scores(q2, k2, e, half, mask, cref, ck_row):
    qe = jnp.where(half == (e == 0), q2, jnp.zeros_like(q2))
    s = lax.dot_general(qe, k2, _NT, preferred_element_type=F32)
    return jnp.where(mask, s + (cref - ck_row), NEG)


def _causal(qi, ki, tq):
    r = lax.broadcasted_iota(jnp.int32, (tq, tq), 0) + qi * tq
    c = lax.broadcasted_iota(jnp.int32, (tq, tq), 1) + ki * tq
    return r >= c


def _fox_fwd(q, kv, cum_t, cfg, *, name):
    t, d, hrows = cfg.tokens, cfg.d_model, cum_t.shape[0]
    tq = _blk(cfg.seq, FOX_BLOCK)
    nq = cfg.seq // tq

    def body(q_ref, k_ref, v_ref, cq_ref, ck_ref, o_ref, lse_ref, m_ref, l_ref, acc_ref):
        qi, ki = pl.program_id(1), pl.program_id(2)

        @pl.when(ki == 0)
        def _():
            m_ref[...] = jnp.full_like(m_ref, NEG)
            l_ref[...] = jnp.zeros_like(l_ref)
            acc_ref[...] = jnp.zeros_like(acc_ref)

        @pl.when(ki <= qi)
        def _():
            mask = _causal(qi, ki, tq)
            half = _half_mask((tq, LANES))
            for hp in range(d // LANES):
                sl = slice(hp * LANES, (hp + 1) * LANES)
                q2, k2, v2 = q_ref[:, sl], k_ref[:, sl], v_ref[:, sl]
                alphas, pvs = [], []
                for e in range(2):
                    h = 2 * hp + e
                    s = _fox_scores(q2, k2, e, half, mask, cq_ref[h:h + 1, 0:1], ck_ref[h:h + 1, :])
                    m_prev = m_ref[h]
                    m_new = jnp.maximum(m_prev, jnp.max(s, axis=1, keepdims=True))
                    alpha = jnp.exp(m_prev - m_new)
                    p = jnp.exp(s - m_new[:, 0:1])
                    l_ref[h] = alpha * l_ref[h] + jnp.sum(p, axis=1, keepdims=True)
                    m_ref[h] = m_new
                    alphas.append(alpha)
                    pvs.append(_dot2(p, v2))
                acc = acc_ref[:, sl]
                acc_ref[:, sl] = jnp.where(half, alphas[0] * acc + pvs[0], alphas[1] * acc + pvs[1])

        @pl.when(ki == qi)
        def _():
            half = _half_mask((tq, LANES))
            for hp in range(d // LANES):
                sl = slice(hp * LANES, (hp + 1) * LANES)
                h0, h1 = 2 * hp, 2 * hp + 1
                inv = jnp.where(half, 1.0 / l_ref[h0], 1.0 / l_ref[h1])
                o_ref[:, sl] = acc_ref[:, sl] * inv
                lse0 = m_ref[h0] + jnp.log(l_ref[h0]) - cq_ref[h0:h0 + 1, 0:1]
                lse1 = m_ref[h1] + jnp.log(l_ref[h1]) - cq_ref[h1:h1 + 1, 0:1]
                lse_ref[:, sl] = jnp.where(half, lse0, lse1)

    def qrow(b, qi, ki):
        return (b * nq + qi, 0)

    def krow(b, qi, ki):
        return (b * nq + jnp.minimum(ki, qi), 0)

    o, lse = pl.pallas_call(
        body, name=name, grid=(cfg.batch, nq, nq),
        in_specs=[pl.BlockSpec((tq, d), qrow),
                  pl.BlockSpec((tq, d), krow),
                  pl.BlockSpec((tq, d), lambda b, qi, ki: (b * nq + jnp.minimum(ki, qi), 1)),
                  pl.BlockSpec((hrows, tq), lambda b, qi, ki: (0, b * nq + qi)),
                  pl.BlockSpec((hrows, tq), lambda b, qi, ki: (0, b * nq + jnp.minimum(ki, qi)))],
        out_specs=[pl.BlockSpec((tq, d), qrow), pl.BlockSpec((tq, d), qrow)],
        out_shape=[jax.ShapeDtypeStruct((t, d), F32), jax.ShapeDtypeStruct((t, d), F32)],
        scratch_shapes=[pltpu.VMEM((cfg.heads, tq, LANES), F32), pltpu.VMEM((cfg.heads, tq, LANES), F32),
                        pltpu.VMEM((tq, d), F32)],
        compiler_params=_params(("parallel", "parallel", "arbitrary")),
    )(q, kv, kv, cum_t, cum_t)
    return o, lse


def _fox_bwd_q(q, kv, cum_t, do, lse, dsum, cfg, *, name):
    t, d, hrows = cfg.tokens, cfg.d_model, cum_t.shape[0]
    tq = _blk(cfg.seq, FOX_BLOCK)
    nq = cfg.seq // tq

    def body(q_ref, k_ref, v_ref, cq_ref, ck_ref, do_ref, l_ref, ds_ref, dq_ref, acc_ref):
        qi, ki = pl.program_id(1), pl.program_id(2)

        @pl.when(ki == 0)
        def _():
            acc_ref[...] = jnp.zeros_like(acc_ref)

        @pl.when(ki <= qi)
        def _():
            mask = _causal(qi, ki, tq)
            half = _half_mask((tq, LANES))
            for hp in range(d // LANES):
                sl = slice(hp * LANES, (hp + 1) * LANES)
                q2, k2, v2 = q_ref[:, sl], k_ref[:, sl], v_ref[:, sl]
                do2 = do_ref[:, sl].astype(BF16)
                dqs = []
                for e in range(2):
                    h = 2 * hp + e
                    lane0 = hp * LANES + e * HEAD_DIM
                    cref = cq_ref[h:h + 1, 0:1]
                    s = _fox_scores(q2, k2, e, half, mask, cref, ck_ref[h:h + 1, :])
                    p = jnp.exp(s - (l_ref[:, lane0:lane0 + 1] + cref))
                    doe = jnp.where(half == (e == 0), do2, jnp.zeros_like(do2))
                    dp = lax.dot_general(doe, v2, _NT, preferred_element_type=F32)
                    dsc = (p * (dp - ds_ref[:, lane0:lane0 + 1])).astype(BF16)
                    dqs.append(jnp.dot(dsc, k2, preferred_element_type=F32))
                acc_ref[:, sl] += jnp.where(half, dqs[0], dqs[1])

        @pl.when(ki == qi)
        def _():
            dq_ref[...] = acc_ref[...]

    def qrow(b, qi, ki):
        return (b * nq + qi, 0)

    return pl.pallas_call(
        body, name=name, grid=(cfg.batch, nq, nq),
        in_specs=[pl.BlockSpec((tq, d), qrow),
                  pl.BlockSpec((tq, d), lambda b, qi, ki: (b * nq + jnp.minimum(ki, qi), 0)),
                  pl.BlockSpec((tq, d), lambda b, qi, ki: (b * nq + jnp.minimum(ki, qi), 1)),
                  pl.BlockSpec((hrows, tq), lambda b, qi, ki: (0, b * nq + qi)),
                  pl.BlockSpec((hrows, tq), lambda b, qi, ki: (0, b * nq + jnp.minimum(ki, qi))),
                  pl.BlockSpec((tq, d), qrow), pl.BlockSpec((tq, d), qrow), pl.BlockSpec((tq, d), qrow)],
        out_specs=pl.BlockSpec((tq, d), qrow),
        out_shape=jax.ShapeDtypeStruct((t, d), F32),
        scratch_shapes=[pltpu.VMEM((tq, d), F32)],
        compiler_params=_params(("parallel", "parallel", "arbitrary")),
    )(q, kv, kv, cum_t, cum_t, do, lse, dsum)


def _fox_bwd_kv(q, kv, cum_t, do, lse, dsum, cfg, *, name):
    t, d, hrows = cfg.tokens, cfg.d_model, cum_t.shape[0]
    tq = _blk(cfg.seq, FOX_BLOCK)
    nq = cfg.seq // tq

    def body(q_ref, k_ref, v_ref, cq_ref, ck_ref, do_ref, l_ref, ds_ref, dk_ref, dv_ref, dc_ref,
             kacc_ref, vacc_ref, cacc_ref):
        ki, qi = pl.program_id(1), pl.program_id(2)

        @pl.when(qi == 0)
        def _():
            kacc_ref[...] = jnp.zeros_like(kacc_ref)
            vacc_ref[...] = jnp.zeros_like(vacc_ref)
            cacc_ref[...] = jnp.zeros_like(cacc_ref)

        @pl.when(qi >= ki)
        def _():
            mask = _causal(qi, ki, tq)
            half = _half_mask((tq, LANES))
            for hp in range(d // LANES):
                sl = slice(hp * LANES, (hp + 1) * LANES)
                q2, k2, v2 = q_ref[:, sl], k_ref[:, sl], v_ref[:, sl]
                do2 = do_ref[:, sl].astype(BF16)
                dks, dvs = [], []
                for e in range(2):
                    h = 2 * hp + e
                    lane0 = hp * LANES + e * HEAD_DIM
                    cref = cq_ref[h:h + 1, 0:1]
                    s = _fox_scores(q2, k2, e, half, mask, cref, ck_ref[h:h + 1, :])
                    p = jnp.exp(s - (l_ref[:, lane0:lane0 + 1] + cref))
                    doe = jnp.where(half == (e == 0), do2, jnp.zeros_like(do2))
                    dp = lax.dot_general(doe, v2, _NT, preferred_element_type=F32)
                    dsf = p * (dp - ds_ref[:, lane0:lane0 + 1])
                    cacc_ref[h:h + 1, :] -= jnp.sum(dsf, axis=0, keepdims=True)
                    dks.append(lax.dot_general(dsf.astype(BF16), q2, _TN, preferred_element_type=F32))
                    dvs.append(lax.dot_general(p.astype(BF16), do2, _TN, preferred_element_type=F32))
                kacc_ref[:, sl] += jnp.where(half, dks[0], dks[1])
                vacc_ref[:, sl] += jnp.where(half, dvs[0], dvs[1])

        @pl.when(qi == nq - 1)
        def _():
            dk_ref[...] = kacc_ref[...]
            dv_ref[...] = vacc_ref[...]
            dc_ref[...] = cacc_ref[...]

    def qrow(b, ki, qi):
        return (b * nq + jnp.maximum(qi, ki), 0)

    def krow(b, ki, qi):
        return (b * nq + ki, 0)

    return pl.pallas_call(
        body, name=name, grid=(cfg.batch, nq, nq),
        in_specs=[pl.BlockSpec((tq, d), qrow),
                  pl.BlockSpec((tq, d), krow),
                  pl.BlockSpec((tq, d), lambda b, ki, qi: (b * nq + ki, 1)),
                  pl.BlockSpec((hrows, tq), lambda b, ki, qi: (0, b * nq + jnp.maximum(qi, ki))),
                  pl.BlockSpec((hrows, tq), lambda b, ki, qi: (0, b * nq + ki)),
                  pl.BlockSpec((tq, d), qrow), pl.BlockSpec((tq, d), qrow), pl.BlockSpec((tq, d), qrow)],
        out_specs=[pl.BlockSpec((tq, d), krow), pl.BlockSpec((tq, d), krow),
                   pl.BlockSpec((hrows, tq), lambda b, ki, qi: (0, b * nq + ki))],
        out_shape=[jax.ShapeDtypeStruct((t, d), F32), jax.ShapeDtypeStruct((t, d), F32),
                   jax.ShapeDtypeStruct((hrows, t), F32)],
        scratch_shapes=[pltpu.VMEM((tq, d), F32), pltpu.VMEM((tq, d), F32), pltpu.VMEM((hrows, tq), F32)],
        compiler_params=_params(("parallel", "parallel", "arbitrary")),
    )(q, kv, kv, cum_t, cum_t, do, lse, dsum)


WIDE = 1536


def _fwd(a, w, *, name, res=None, scale=1.0):
    return _mm(a, w, form="F", out_dtype=F32, name=name, bn=WIDE, bk=WIDE, res=res, scale=scale)


def _bwd(dy, w, *, name, scale=1.0):
    return _mm(dy, w, form="B", out_dtype=F32, name=name, bn=WIDE, bk=WIDE, scale=scale)


def _wgrad(a, dy, w, *, name, scale=1.0):
    return _mm_grad(a, dy, w.shape[0], name=name, bm=WIDE, bn=WIDE, scale=scale)


def _ffn_fwd(h, g, w_in, w_out, tag):
    n = _rms_fwd(h, g, name=f"{tag}_norm")
    u = _fwd(n, w_in, name=f"{tag}_in")
    a = _swiglu_fwd(u, name=f"{tag}_act")
    return _fwd(a, w_out, name=f"{tag}_out", res=h, scale=0.5), (n, u, a)


def _ffn_bwd(dh_out, h, g, w_in, w_out, saved, tag):
    n, u, a = saved
    da = _bwd(dh_out, w_out, name=f"{tag}_out_dx", scale=0.5)
    dw_out = _wgrad(a, dh_out, w_out, name=f"{tag}_out_dw", scale=0.5)
    du = _swiglu_bwd(u, da, name=f"{tag}_act_bwd")
    dn = _bwd(du, w_in, name=f"{tag}_in_dx")
    dw_in = _wgrad(n, du, w_in, name=f"{tag}_in_dw")
    dh, dg = _rms_bwd(h, g, dn, dh_out, name=f"{tag}_norm_bwd")
    return dh, dg, dw_in, dw_out


def _head_gain(g, heads, scale=1.0):
    return jnp.tile(g.astype(F32) * scale, heads)


def _local_step(cfg, x, positions, target, w, s):
    d, hh = cfg.d_model, cfg.heads
    cos, sin = _rope_tables(positions)
    ones = jnp.ones((d,), F32)

    h1, ffn0 = _ffn_fwd(x, s["ffn_norm"][0, 0], w["ffn_w_in"][0][0], w["ffn_w_out"][0][0], "ffn00")
    hn_a = _rms_fwd(h1, s["mix_norm"][0], name="a_norm")
    qkv = _fwd(hn_a, w["a_w_qkv"], name="a_qkv")
    kinds_a = ["rope", "rope", "cast"] * len(DILATIONS)
    gains_a = jnp.stack([z for g in range(len(DILATIONS)) for z in (
        _head_gain(s["a_q_norm"][g], hh, Q_SCALE), _head_gain(s["a_k_norm"][g], hh), ones)])
    qkvp = _hn_fwd(qkv, gains_a, kinds_a, d, cos, sin, name="a_qk_norm")
    band = [_band_fwd(qkvp, g, dil, cfg, name=f"a_band{g}") for g, dil in enumerate(DILATIONS)]
    mixed, lse_a = _mix_fwd([o for o, _ in band], [l for _, l in band], name="a_mix")
    h2 = _fwd(mixed, w["a_w_o"], name="a_out", res=h1)
    h3, ffn1 = _ffn_fwd(h2, s["ffn_norm"][0, 1], w["ffn_w_in"][0][1], w["ffn_w_out"][0][1], "ffn01")

    kn = _rms_fwd(h3, s["kv_norm"], name="kv_norm")
    proj = _fwd(kn, w["kv_w"], name="kv_proj")
    kinds_kv = ["norm", "cast"]
    gains_kv = jnp.stack([_head_gain(s["kv_k_norm"], hh), ones])
    kvp = _hn_fwd(proj, gains_kv, kinds_kv, d, cos, sin, name="kv_k_norm")
    gate_col = 2 * d // LANES
    bias = jnp.pad(s["kv_b_f"].astype(F32), (0, LANES - hh))
    cum = _gate_fwd(proj, gate_col, bias, cfg, name="kv_gate")
    cum_t = cum[:, :hh].T

    h4, ffn2 = _ffn_fwd(h3, s["ffn_norm"][1, 0], w["ffn_w_in"][1][0], w["ffn_w_out"][1][0], "ffn10")
    hn_b = _rms_fwd(h4, s["mix_norm"][1], name="b_norm")
    qraw = _fwd(hn_b, w["b_w_q"], name="b_q")
    gains_b = _head_gain(s["b_q_norm"][0], hh, Q_SCALE)[None]
    qp = _hn_fwd(qraw, gains_b, ["norm"], d, cos, sin, name="b_q_norm")
    o_b, lse_b = _fox_fwd(qp, kvp, cum_t, cfg, name="b_fox")
    h5 = _fwd(o_b, w["b_w_o"], name="b_out", res=h4)
    h6, ffn3 = _ffn_fwd(h5, s["ffn_norm"][1, 1], w["ffn_w_in"][1][1], w["ffn_w_out"][1][1], "ffn11")

    loss, dh6 = _loss_fwd_bwd(h6, target, name="loss")

    dh5, dg11, dwi11, dwo11 = _ffn_bwd(dh6, h5, s["ffn_norm"][1, 1], w["ffn_w_in"][1][1], w["ffn_w_out"][1][1],
                                       ffn3, "ffn11")
    do_b = _bwd(dh5, w["b_w_o"], name="b_out_dx")
    dw_bo = _wgrad(o_b, dh5, w["b_w_o"], name="b_out_dw")
    dsum_b = _head_dot(do_b, o_b, name="b_dsum")
    dq_b = _fox_bwd_q(qp, kvp, cum_t, do_b, lse_b, dsum_b, cfg, name="b_fox_dq")
    dk_b, dv_b, dcum_t = _fox_bwd_kv(qp, kvp, cum_t, do_b, lse_b, dsum_b, cfg, name="b_fox_dkv")
    dqraw, dgq = _hn_bwd(qraw, dq_b, gains_b, ["norm"], d, cos, sin, name="b_q_norm_bwd")
    dhn_b = _bwd(dqraw, w["b_w_q"], name="b_q_dx")
    dw_bq = _wgrad(hn_b, dqraw, w["b_w_q"], name="b_q_dw")
    dh4, dmix1 = _rms_bwd(h4, s["mix_norm"][1], dhn_b, dh5, name="b_norm_bwd")
    dh3, dg10, dwi10, dwo10 = _ffn_bwd(dh4, h3, s["ffn_norm"][1, 0], w["ffn_w_in"][1][0], w["ffn_w_out"][1][0],
                                       ffn2, "ffn10")

    dkvraw, dgk = _hn_bwd(proj, jnp.concatenate([dk_b, dv_b], axis=1), gains_kv, kinds_kv, d, cos, sin,
                          name="kv_k_norm_bwd")
    dcum = jnp.pad(dcum_t.T, ((0, 0), (0, LANES - hh)))
    dz, dbias = _gate_bwd(proj, gate_col, bias, dcum, cfg, name="kv_gate_bwd")
    pad_cols = w["kv_w"].shape[2] - 2 * d - LANES
    dproj = jnp.concatenate([dkvraw, dz.astype(BF16), jnp.zeros((cfg.tokens, pad_cols), BF16)], axis=1)
    dkn = _bwd(dproj, w["kv_w"], name="kv_proj_dx")
    dw_kv = _wgrad(kn, dproj, w["kv_w"], name="kv_proj_dw")
    dh3, dkvn = _rms_bwd(h3, s["kv_norm"], dkn, dh3, name="kv_norm_bwd")

    dh2, dg01, dwi01, dwo01 = _ffn_bwd(dh3, h2, s["ffn_norm"][0, 1], w["ffn_w_in"][0][1], w["ffn_w_out"][0][1],
                                       ffn1, "ffn01")
    dmixed = _bwd(dh2, w["a_w_o"], name="a_out_dx")
    dw_ao = _wgrad(mixed, dh2, w["a_w_o"], name="a_out_dw")
    dsum_a = _head_dot(dmixed, mixed, name="a_dsum")
    dqkvp = []
    for g, dil in enumerate(DILATIONS):
        dqkvp += _band_bwd(qkvp, dmixed, lse_a, dsum_a, g, dil, cfg, name=f"a_band{g}_bwd")
    dqkv, dga = _hn_bwd(qkv, jnp.concatenate(dqkvp, axis=1), gains_a, kinds_a, d, cos, sin, name="a_qk_norm_bwd")
    dhn_a = _bwd(dqkv, w["a_w_qkv"], name="a_qkv_dx")
    dw_qkv = _wgrad(hn_a, dqkv, w["a_w_qkv"], name="a_qkv_dw")
    dh1, dmix0 = _rms_bwd(h1, s["mix_norm"][0], dhn_a, dh2, name="a_norm_bwd")
    dx, dg00, dwi00, dwo00 = _ffn_bwd(dh1, x, s["ffn_norm"][0, 0], w["ffn_w_in"][0][0], w["ffn_w_out"][0][0],
                                      ffn0, "ffn00")

    dw = {
        "ffn_w_in": [[dwi00, dwi01], [dwi10, dwi11]],
        "ffn_w_out": [[dwo00, dwo01], [dwo10, dwo11]],
        "a_w_qkv": dw_qkv, "a_w_o": dw_ao, "kv_w": dw_kv, "b_w_q": dw_bq, "b_w_o": dw_bo,
    }
    ds = {
        "ffn_norm": jnp.stack([jnp.stack([dg00, dg01]), jnp.stack([dg10, dg11])]),
        "mix_norm": jnp.stack([dmix0, dmix1]),
        "a_q_norm": jnp.stack([dga[3 * g] for g in range(len(DILATIONS))])[None] * Q_SCALE,
        "a_k_norm": jnp.stack([dga[3 * g + 1] for g in range(len(DILATIONS))])[None],
        "kv_norm": dkvn,
        "kv_b_f": dbias[:hh],
        "kv_k_norm": dgk[0],
        "b_q_norm": dgq * Q_SCALE,
    }
    return loss, dx, dw, ds


MESH_ID = pl.DeviceIdType.MESH
ANY = pl.BlockSpec(memory_space=pl.ANY)
PACK_COLS = 1024
PACK_ROW_ALIGN = 32


def _me():
    return lax.axis_index("x"), lax.axis_index("y"), lax.axis_index("c")


def _other_chips(x, y):
    return [(1 - x, y), (x, 1 - y), (1 - x, 1 - y)]


def _all_gather_small(v, *, name):
    r = v.shape[0]

    def body(v_ref, out_ref, send_sems, recv_sems):
        x, y, c = _me()
        me = 4 * x + 2 * y + c
        out_ref[me] = v_ref[...]
        copies = []
        for k in range(1, N_DEV):
            fx, fy, fc = (k >> 2) & 1, (k >> 1) & 1, k & 1
            peer = (1 - x if fx else x, 1 - y if fy else y, 1 - c if fc else c)
            copies.append(pltpu.make_async_remote_copy(
                src_ref=v_ref, dst_ref=out_ref.at[me], send_sem=send_sems.at[k - 1], recv_sem=recv_sems.at[k - 1],
                device_id=peer, device_id_type=MESH_ID))
        for cp in copies:
            cp.start()
        for cp in copies:
            cp.wait()

    return pl.pallas_call(
        body, name=name,
        in_specs=[pl.BlockSpec(memory_space=pltpu.VMEM)], out_specs=pl.BlockSpec(memory_space=pltpu.VMEM),
        out_shape=jax.ShapeDtypeStruct((N_DEV, r, LANES), v.dtype),
        scratch_shapes=[pltpu.SemaphoreType.DMA((N_DEV - 1,)), pltpu.SemaphoreType.DMA((N_DEV - 1,))],
    )(v)


def _all_gather_chips(v, *, name):
    def body(v_ref, out_ref, send_sems, recv_sems, local_sem):
        x, y, c = _me()
        j = 2 * x + y
        local = pltpu.make_async_copy(v_ref, out_ref.at[j], local_sem)
        local.start()
        copies = [pltpu.make_async_remote_copy(
            src_ref=v_ref, dst_ref=out_ref.at[j], send_sem=send_sems.at[k], recv_sem=recv_sems.at[k],
            device_id=(px, py, c), device_id_type=MESH_ID) for k, (px, py) in enumerate(_other_chips(x, y))]
        for cp in copies:
            cp.start()
        for cp in copies:
            cp.wait()
        local.wait()

    return pl.pallas_call(
        body, name=name, in_specs=[ANY], out_specs=ANY,
        out_shape=jax.ShapeDtypeStruct((N_CHIPS,) + v.shape, v.dtype),
        scratch_shapes=[pltpu.SemaphoreType.DMA((N_CHIPS - 1,)), pltpu.SemaphoreType.DMA((N_CHIPS - 1,)),
                        pltpu.SemaphoreType.DMA],
    )(v)


def _swap_halves(g, *, name):
    n, r, cols = g.shape
    rh = r // 2

    def body(g_ref, out_ref, send_sem, recv_sem):
        x, y, c = _me()
        cp = pltpu.make_async_remote_copy(
            src_ref=g_ref.at[:, pl.ds((1 - c) * rh, rh)], dst_ref=out_ref, send_sem=send_sem, recv_sem=recv_sem,
            device_id=(x, y, 1 - c), device_id_type=MESH_ID)
        cp.start()
        cp.wait()

    return pl.pallas_call(
        body, name=name, in_specs=[ANY], out_specs=ANY,
        out_shape=jax.ShapeDtypeStruct((n, rh, cols), g.dtype),
        scratch_shapes=[pltpu.SemaphoreType.DMA, pltpu.SemaphoreType.DMA],
    )(g)


def _scatter_chips(v, *, name):
    def body(v_ref, out_ref, send_sems, recv_sems, local_sem):
        x, y, c = _me()
        j = 2 * x + y
        local = pltpu.make_async_copy(v_ref.at[j], out_ref.at[j], local_sem)
        local.start()
        copies = [pltpu.make_async_remote_copy(
            src_ref=v_ref.at[2 * px + py], dst_ref=out_ref.at[j], send_sem=send_sems.at[k], recv_sem=recv_sems.at[k],
            device_id=(px, py, c), device_id_type=MESH_ID) for k, (px, py) in enumerate(_other_chips(x, y))]
        for cp in copies:
            cp.start()
        for cp in copies:
            cp.wait()
        local.wait()

    return pl.pallas_call(
        body, name=name, in_specs=[ANY], out_specs=ANY,
        out_shape=jax.ShapeDtypeStruct(v.shape, v.dtype),
        scratch_shapes=[pltpu.SemaphoreType.DMA((N_CHIPS - 1,)), pltpu.SemaphoreType.DMA((N_CHIPS - 1,)),
                        pltpu.SemaphoreType.DMA],
    )(v)


def _join_halves(v, *, name):
    def body(v_ref, out_ref, send_sem, recv_sem, local_sem):
        x, y, c = _me()
        local = pltpu.make_async_copy(v_ref, out_ref.at[c], local_sem)
        local.start()
        cp = pltpu.make_async_remote_copy(
            src_ref=v_ref, dst_ref=out_ref.at[c], send_sem=send_sem, recv_sem=recv_sem,
            device_id=(x, y, 1 - c), device_id_type=MESH_ID)
        cp.start()
        cp.wait()
        local.wait()

    return pl.pallas_call(
        body, name=name, in_specs=[ANY], out_specs=ANY,
        out_shape=jax.ShapeDtypeStruct((2,) + v.shape, v.dtype),
        scratch_shapes=[pltpu.SemaphoreType.DMA, pltpu.SemaphoreType.DMA, pltpu.SemaphoreType.DMA],
    )(v)


def _row_blk(rows, want):
    for b in range(min(rows, want) // SUBLANES * SUBLANES, 0, -SUBLANES):
        if rows % b == 0:
            return b
    return rows


def _add_own_half(g, got, *, name):
    n, r, cols = g.shape
    rh = r // 2
    tr = _row_blk(rh, 512)
    nb = rh // tr

    def body(c_ref, g_ref, got_ref, o_ref):
        del c_ref
        o_ref[...] = (g_ref[...] + got_ref[...]).astype(BF16)

    grid_spec = pltpu.PrefetchScalarGridSpec(
        num_scalar_prefetch=1, grid=(n, nb),
        in_specs=[pl.BlockSpec((None, tr, cols), lambda j, i, c: (j, c[0] * nb + i, 0)),
                  pl.BlockSpec((None, tr, cols), lambda j, i, c: (j, i, 0))],
        out_specs=pl.BlockSpec((None, tr, cols), lambda j, i, c: (j, i, 0)))
    return pl.pallas_call(
        body, name=name, grid_spec=grid_spec, out_shape=jax.ShapeDtypeStruct((n, rh, cols), BF16),
        compiler_params=_params(("parallel", "parallel")),
    )(lax.axis_index("c").astype(jnp.int32).reshape(1), g, got)


def _sum_parts(parts, *, name):
    n, r, cols = parts.shape
    tr = _row_blk(r, 512)

    def body(*refs):
        o_ref = refs[n]
        acc = refs[0][...].astype(F32)
        for p_ref in refs[1:n]:
            acc = acc + p_ref[...].astype(F32)
        o_ref[...] = acc

    return pl.pallas_call(
        body, name=name, grid=(r // tr,),
        in_specs=[pl.BlockSpec((None, tr, cols), functools.partial(lambda j, i: (j, i, 0), j)) for j in range(n)],
        out_specs=pl.BlockSpec((tr, cols), lambda i: (i, 0)),
        out_shape=jax.ShapeDtypeStruct((r, cols), F32),
        compiler_params=_params(("parallel",)),
    )(*([parts] * n))


def _adamw(w, m, v, g, *, name):
    shape = w.shape
    cols = shape[-1]
    w2, m2, v2, g2 = (z.reshape(-1, cols) for z in (w, m, v, g))
    rows = w2.shape[0]
    tr = _row_blk(rows, max(SUBLANES, (1 << 20) // (4 * cols)))

    def body(w_ref, m_ref, v_ref, g_ref, d_ref, nm_ref, nv_ref):
        gv = g_ref[...]
        nm = ADAM_B1 * m_ref[...] + (1.0 - ADAM_B1) * gv
        nv = ADAM_B2 * v_ref[...] + (1.0 - ADAM_B2) * jnp.square(gv)
        m_hat = nm / (1.0 - ADAM_B1 ** ADAM_STEP)
        v_hat = nv / (1.0 - ADAM_B2 ** ADAM_STEP)
        d_ref[...] = -ADAM_LR * (m_hat / (jnp.sqrt(v_hat) + ADAM_EPS) + ADAM_WD * w_ref[...])
        nm_ref[...] = nm
        nv_ref[...] = nv

    spec = pl.BlockSpec((tr, cols), lambda i: (i, 0))
    out = jax.ShapeDtypeStruct((rows, cols), F32)
    d, nm, nv = pl.pallas_call(
        body, name=name, grid=(rows // tr,), in_specs=[spec] * 4, out_specs=[spec] * 3, out_shape=[out] * 3,
        compiler_params=_params(("parallel",)),
    )(w2, m2, v2, g2)
    return d.reshape(shape), nm.reshape(shape), nv.reshape(shape)


def _pack_rows(size, cols, align):
    return -(-size // (cols * align)) * align


def _pack(arrs, lead, cols, align, total_align):
    lead_shape = arrs[0].shape[:lead]
    parts = []
    for a in arrs:
        flat = a.reshape(lead_shape + (-1,))
        size = flat.shape[-1]
        rows = _pack_rows(size, cols, align)
        flat = jnp.pad(flat, [(0, 0)] * lead + [(0, rows * cols - size)])
        parts.append(flat.reshape(lead_shape + (rows, cols)))
    total = sum(p.shape[lead] for p in parts)
    extra = -total % total_align
    if extra:
        parts.append(jnp.zeros(lead_shape + (extra, cols), parts[0].dtype))
    return jnp.concatenate(parts, axis=lead)


def _unpack(buf, shapes, lead, cols, align):
    lead_shape = buf.shape[:lead]
    out, row = [], 0
    for shp in shapes:
        size = 1
        for n in shp:
            size *= n
        rows = _pack_rows(size, cols, align)
        piece = lax.slice_in_dim(buf, row, row + rows, axis=lead).reshape(lead_shape + (-1,))
        out.append(piece[..., :size].reshape(lead_shape + tuple(shp)))
        row += rows
    return out


BIG = ("ffn_w_in", "ffn_w_out", "a_w_qkv", "a_w_o", "kv_w", "b_w_q", "b_w_o")
SMALL = ("ffn_norm", "mix_norm", "a_q_norm", "a_k_norm", "kv_norm", "kv_b_f", "kv_k_norm", "b_q_norm")
WEIGHTS = ("ffn_norm", "ffn_w_in", "ffn_w_out", "mix_norm", "a_w_qkv", "a_q_norm", "a_k_norm", "a_w_o",
           "kv_norm", "kv_w", "kv_b_f", "kv_k_norm", "b_w_q", "b_q_norm", "b_w_o")
GATE_PAD = 2 * LANES


def _stack_weights(sh, d):
    depth = sh["ffn_w_in"].shape[1]
    kv = sh["kv_w"].transpose(1, 0, 2).reshape(d, -1)
    kv = jnp.pad(kv, ((0, 0), (0, 2 * d + GATE_PAD - kv.shape[1])))
    return {
        "ffn_w_in": [[sh["ffn_w_in"][:, l, i] for i in range(2)] for l in range(depth)],
        "ffn_w_out": [[sh["ffn_w_out"][:, l, i].reshape(1, -1, d) for i in range(2)] for l in range(depth)],
        "a_w_qkv": sh["a_w_qkv"][:, 0],
        "a_w_o": sh["a_w_o"].reshape(1, d, d),
        "kv_w": kv[None],
        "b_w_q": sh["b_w_q"].reshape(1, d, d),
        "b_w_o": sh["b_w_o"].reshape(1, d, d),
    }


def _unstack_grads(dw, d, heads):
    def rows4(z):
        return z.reshape(N_CHIPS, -1, d)

    kv_cols = 2 * d + heads
    kv = dw["kv_w"][0][:, :kv_cols].reshape(d, N_CHIPS, kv_cols // N_CHIPS).transpose(1, 0, 2)
    return [
        jnp.stack([jnp.stack(row, axis=1) for row in dw["ffn_w_in"]], axis=1),
        jnp.stack([jnp.stack([rows4(z) for z in row], axis=1) for row in dw["ffn_w_out"]], axis=1),
        dw["a_w_qkv"][:, None],
        rows4(dw["a_w_o"])[:, None],
        kv,
        rows4(dw["b_w_q"])[:, None],
        rows4(dw["b_w_o"])[:, None],
    ]


def kernel(x, positions, ffn_norm, ffn_w_in, ffn_w_out, mix_norm, a_w_qkv, a_q_norm, a_k_norm, a_w_o, kv_norm, kv_w, kv_b_f, kv_k_norm, b_w_q, b_q_norm, b_w_o, loss_target, m_ffn_norm, m_ffn_w_in, m_ffn_w_out, m_mix_norm, m_a_w_qkv, m_a_q_norm, m_a_k_norm, m_a_w_o, m_kv_norm, m_kv_w, m_kv_b_f, m_kv_k_norm, m_b_w_q, m_b_q_norm, m_b_w_o, v_ffn_norm, v_ffn_w_in, v_ffn_w_out, v_mix_norm, v_a_w_qkv, v_a_q_norm, v_a_k_norm, v_a_w_o, v_kv_norm, v_kv_w, v_kv_b_f, v_kv_k_norm, v_b_w_q, v_b_q_norm, v_b_w_o):
    wts = dict(zip(WEIGHTS, (ffn_norm, ffn_w_in, ffn_w_out, mix_norm, a_w_qkv, a_q_norm, a_k_norm, a_w_o, kv_norm,
                             kv_w, kv_b_f, kv_k_norm, b_w_q, b_q_norm, b_w_o)))
    mom = dict(zip(WEIGHTS, (m_ffn_norm, m_ffn_w_in, m_ffn_w_out, m_mix_norm, m_a_w_qkv, m_a_q_norm, m_a_k_norm,
                             m_a_w_o, m_kv_norm, m_kv_w, m_kv_b_f, m_kv_k_norm, m_b_w_q, m_b_q_norm, m_b_w_o)))
    var = dict(zip(WEIGHTS, (v_ffn_norm, v_ffn_w_in, v_ffn_w_out, v_mix_norm, v_a_w_qkv, v_a_q_norm, v_a_k_norm,
                             v_a_w_o, v_kv_norm, v_kv_w, v_kv_b_f, v_kv_k_norm, v_b_w_q, v_b_q_norm, v_b_w_o)))
    batch, seq, d = x.shape
    cfg = Cfg(d_model=d, d_ff=ffn_w_out.shape[2] * N_CHIPS, seq=seq, batch=batch)
    chip = 2 * lax.axis_index("x") + lax.axis_index("y")
    big_shapes = [wts[n].shape for n in BIG]

    shard = _pack([wts[n].astype(BF16) for n in BIG], 0, PACK_COLS, PACK_ROW_ALIGN, PACK_COLS)
    gathered = _all_gather_chips(shard, name="gather_weights")
    w = _stack_weights(dict(zip(BIG, _unpack(gathered, big_shapes, 1, PACK_COLS, PACK_ROW_ALIGN))), d)
    norm_shard = _pack([ffn_norm], 0, LANES, SUBLANES, SUBLANES)
    norms = _all_gather_small(norm_shard, name="gather_ffn_norm")[0::2]
    (norms,) = _unpack(norms, [ffn_norm.shape], 1, LANES, SUBLANES)
    small = {"ffn_norm": jnp.moveaxis(norms, 0, 2).reshape(ffn_norm.shape[:2] + (d,)),
             "mix_norm": mix_norm, "a_q_norm": a_q_norm[0], "a_k_norm": a_k_norm[0], "kv_norm": kv_norm,
             "kv_b_f": kv_b_f, "kv_k_norm": kv_k_norm, "b_q_norm": b_q_norm}

    loss, dx, dw, ds = _local_step(cfg, x.reshape(cfg.tokens, d), positions.reshape(cfg.tokens),
                                   loss_target.reshape(cfg.tokens, d), w, small)
    loss = lax.psum(loss, ("x", "y", "c"))

    g = _pack(_unstack_grads(dw, d, cfg.heads), 1, PACK_COLS, PACK_ROW_ALIGN, PACK_COLS)
    chip_half = _add_own_half(g, _swap_halves(g, name="swap_halves"), name="add_halves")
    mine = _sum_parts(_scatter_chips(chip_half, name="scatter_chips"), name="sum_chips")
    g_big = _join_halves(mine, name="join_halves").reshape(g.shape[1:])
    grads = dict(zip(BIG, _unpack(g_big, big_shapes, 0, PACK_COLS, PACK_ROW_ALIGN)))

    small_shapes = [ds[n].shape for n in SMALL]
    parts = _all_gather_small(_pack([ds[n] for n in SMALL], 0, LANES, SUBLANES, SUBLANES), name="gather_small")
    g_small = dict(zip(SMALL, _unpack(_sum_parts(parts, name="sum_small"), small_shapes, 0, LANES, SUBLANES)))
    quarter = d // N_CHIPS
    g_small["ffn_norm"] = lax.dynamic_slice_in_dim(g_small["ffn_norm"], chip * quarter, quarter, axis=2)
    grads.update(g_small)

    delta, new_m, new_v = {}, {}, {}
    for n in BIG:
        delta[n], new_m[n], new_v[n] = _adamw(wts[n], mom[n], var[n], grads[n], name=f"adamw_{n}")
    packed = [_pack([z[n] for n in SMALL], 0, LANES, SUBLANES, SUBLANES) for z in (wts, mom, var, grads)]
    small_out = _adamw(*packed, name="adamw_small")
    shard_shapes = [wts[n].shape for n in SMALL]
    for out, res in zip((delta, new_m, new_v), small_out):
        out.update(zip(SMALL, _unpack(res, shard_shapes, 0, LANES, SUBLANES)))

    return (loss, dx.reshape(x.shape), *[grads[n] for n in WEIGHTS], *[delta[n] for n in WEIGHTS],
            *[new_m[n] for n in WEIGHTS], *[new_v[n] for n in WEIGHTS])
```

```python
import functools
from typing import NamedTuple

import jax
import jax.numpy as jnp
from jax import lax
from jax.experimental import pallas as pl
from jax.experimental.pallas import tpu as pltpu

F32 = jnp.float32
BF16 = jnp.bfloat16

HEAD_DIM = 64
LANES = 128
SUBLANES = 8
ROT_DIM = HEAD_DIM // 4
ROPE_THETA = 500000.0
NORM_EPS = 1e-6
BAND = 128
DILATIONS = (1, 4, 16)
NEG = -1e30
Q_SCALE = HEAD_DIM ** -0.5
N_CHIPS = 4
N_DEV = 8
VMEM_LIMIT = 48 * 1024 * 1024

ADAM_LR = 0.001
ADAM_B1 = 0.9
ADAM_B2 = 0.999
ADAM_EPS = 1e-08
ADAM_WD = 0.01
ADAM_STEP = 10


class Cfg(NamedTuple):
    d_model: int
    d_ff: int
    seq: int
    batch: int

    @property
    def heads(self):
        return self.d_model // HEAD_DIM

    @property
    def tokens(self):
        return self.batch * self.seq

    @property
    def pairs(self):
        return self.d_model // LANES


def _params(sem):
    return pltpu.CompilerParams(dimension_semantics=sem, vmem_limit_bytes=VMEM_LIMIT)


def _blk(dim, want):
    if dim <= want:
        return dim
    for b in range(want // LANES * LANES, 0, -LANES):
        if dim % b == 0:
            return b
    b = want
    while dim % b:
        b //= 2
    return b


def _mm(a, b, *, form, out_dtype, name, bm=1024, bn=1024, bk=1024, res=None, scale=1.0):
    if form == "F":
        m, kdim = a.shape
        jn, _, ns = b.shape
        bm, bn, bk = _blk(m, bm), _blk(ns, bn), _blk(kdim, bk)
        npj = ns // bn
        grid = (m // bm, jn * npj, kdim // bk)
        a_spec = pl.BlockSpec((bm, bk), lambda i, n, k: (i, k))
        b_spec = pl.BlockSpec((None, bk, bn), lambda i, n, k: (n // npj, k, n % npj))
        o_spec = pl.BlockSpec((bm, bn), lambda i, n, k: (i, n))
        o_shape = jax.ShapeDtypeStruct((m, jn * ns), out_dtype)
        dims = (((1,), (0,)), ((), ()))
    elif form == "B":
        m = a.shape[0]
        jn, kdim, ns = b.shape
        bm, bn, bk = _blk(m, bm), _blk(kdim, bn), _blk(ns, bk)
        kpj = ns // bk
        grid = (m // bm, kdim // bn, jn * kpj)
        a_spec = pl.BlockSpec((bm, bk), lambda i, n, k: (i, k))
        b_spec = pl.BlockSpec((None, bn, bk), lambda i, n, k: (k // kpj, n, k % kpj))
        o_spec = pl.BlockSpec((bm, bn), lambda i, n, k: (i, n))
        o_shape = jax.ShapeDtypeStruct((m, kdim), out_dtype)
        dims = (((1,), (1,)), ((), ()))
    else:
        raise ValueError(form)
    nk = grid[2]

    def body(*refs):
        if res is None:
            a_ref, b_ref, o_ref, acc_ref = refs
            r_ref = None
        else:
            a_ref, b_ref, r_ref, o_ref, acc_ref = refs
        k = pl.program_id(2)

        @pl.when(k == 0)
        def _():
            acc_ref[...] = jnp.zeros_like(acc_ref)

        acc_ref[...] += lax.dot_general(a_ref[...].astype(BF16), b_ref[...].astype(BF16), dims,
                                        preferred_element_type=F32)

        @pl.when(k == nk - 1)
        def _():
            r = acc_ref[...]
            if scale != 1.0:
                r = r * scale
            if r_ref is not None:
                r = r_ref[...] + r
            o_ref[...] = r.astype(o_ref.dtype)

    in_specs = [a_spec, b_spec]
    args = [a, b]
    if res is not None:
        in_specs.append(pl.BlockSpec((bm, bn), lambda i, n, k: (i, n)))
        args.append(res)
    return pl.pallas_call(
        body, name=name, grid=grid, in_specs=in_specs, out_specs=o_spec, out_shape=o_shape,
        scratch_shapes=[pltpu.VMEM((bm, bn), F32)],
        compiler_params=_params(("parallel", "parallel", "arbitrary")),
    )(*args)


def _mm_grad(a, dy, jn, *, name, scale=1.0, bm=1024, bn=1024, bk=1024):
    t, kdim = a.shape
    ns = dy.shape[1] // jn
    bm, bn, bk = _blk(kdim, bm), _blk(ns, bn), _blk(t, bk)
    npj = ns // bn
    grid = (kdim // bm, jn * npj, t // bk)
    nk = grid[2]
    dims = (((0,), (0,)), ((), ()))

    def body(a_ref, b_ref, o_ref, acc_ref):
        k = pl.program_id(2)

        @pl.when(k == 0)
        def _():
            acc_ref[...] = jnp.zeros_like(acc_ref)

        acc_ref[...] += lax.dot_general(a_ref[...].astype(BF16), b_ref[...].astype(BF16), dims,
                                        preferred_element_type=F32)

        @pl.when(k == nk - 1)
        def _():
            r = acc_ref[...]
            if scale != 1.0:
                r = r * scale
            o_ref[...] = r

    return pl.pallas_call(
        body, name=name, grid=grid,
        in_specs=[pl.BlockSpec((bk, bm), lambda m, n, k: (k, m)),
                  pl.BlockSpec((bk, bn), lambda m, n, k: (k, n))],
        out_specs=pl.BlockSpec((None, bm, bn), lambda m, n, k: (n // npj, m, n % npj)),
        out_shape=jax.ShapeDtypeStruct((jn, kdim, ns), F32),
        scratch_shapes=[pltpu.VMEM((bm, bn), F32)],
        compiler_params=_params(("parallel", "parallel", "arbitrary")),
    )(a, dy)


ROW_BLOCK = 512


def _fold8(x):
    return jnp.sum(x.reshape(x.shape[0] // SUBLANES, SUBLANES, x.shape[1]), axis=0)


def _rms_fwd(x, g, *, name):
    t, d = x.shape
    tr = _blk(t, ROW_BLOCK)

    def body(x_ref, g_ref, o_ref):
        xv = x_ref[...]
        rstd = lax.rsqrt(jnp.mean(xv * xv, axis=-1, keepdims=True) + NORM_EPS)
        o_ref[...] = ((xv * rstd) * g_ref[...]).astype(BF16)

    return pl.pallas_call(
        body, name=name, grid=(t // tr,),
        in_specs=[pl.BlockSpec((tr, d), lambda i: (i, 0)), pl.BlockSpec((1, d), lambda i: (0, 0))],
        out_specs=pl.BlockSpec((tr, d), lambda i: (i, 0)),
        out_shape=jax.ShapeDtypeStruct((t, d), BF16),
        compiler_params=_params(("parallel",)),
    )(x, g.reshape(1, d))


def _rms_bwd(x, g, dy, dres, *, name):
    t, d = x.shape
    tr = _blk(t, ROW_BLOCK)

    def body(x_ref, g_ref, dy_ref, dres_ref, dx_ref, dg_ref):
        i = pl.program_id(0)
        xv = x_ref[...]
        rstd = lax.rsqrt(jnp.mean(xv * xv, axis=-1, keepdims=True) + NORM_EPS)
        xhat = xv * rstd
        dyv = dy_ref[...]
        dyg = dyv * g_ref[...]
        proj = jnp.mean(dyg * xhat, axis=-1, keepdims=True)
        dx_ref[...] = dres_ref[...] + rstd * (dyg - xhat * proj)

        @pl.when(i == 0)
        def _():
            dg_ref[...] = jnp.zeros_like(dg_ref)

        dg_ref[...] += _fold8(dyv * xhat)

    dx, dg = pl.pallas_call(
        body, name=name, grid=(t // tr,),
        in_specs=[pl.BlockSpec((tr, d), lambda i: (i, 0)), pl.BlockSpec((1, d), lambda i: (0, 0)),
                  pl.BlockSpec((tr, d), lambda i: (i, 0)), pl.BlockSpec((tr, d), lambda i: (i, 0))],
        out_specs=[pl.BlockSpec((tr, d), lambda i: (i, 0)), pl.BlockSpec((SUBLANES, d), lambda i: (0, 0))],
        out_shape=[jax.ShapeDtypeStruct((t, d), F32), jax.ShapeDtypeStruct((SUBLANES, d), F32)],
        compiler_params=_params(("arbitrary",)),
    )(x, g.reshape(1, d), dy, dres)
    return dx, jnp.sum(dg, axis=0)


def _swiglu_fwd(u, *, name):
    t, f2 = u.shape
    f = f2 // 2
    tr = _blk(t, 256)

    def body(g_ref, u_ref, o_ref):
        gv = g_ref[...]
        o_ref[...] = (gv * jax.nn.sigmoid(gv) * u_ref[...]).astype(BF16)

    return pl.pallas_call(
        body, name=name, grid=(t // tr,),
        in_specs=[pl.BlockSpec((tr, f), lambda i: (i, 0)), pl.BlockSpec((tr, f), lambda i: (i, 1))],
        out_specs=pl.BlockSpec((tr, f), lambda i: (i, 0)),
        out_shape=jax.ShapeDtypeStruct((t, f), BF16),
        compiler_params=_params(("parallel",)),
    )(u, u)


def _swiglu_bwd(u, da, *, name):
    t, f2 = u.shape
    f = f2 // 2
    tr = _blk(t, 256)

    def body(g_ref, u_ref, da_ref, o_ref):
        gv = g_ref[...]
        sg = jax.nn.sigmoid(gv)
        silu = gv * sg
        dav = da_ref[...]
        o_ref[:, :f] = (dav * u_ref[...] * (sg + silu * (1.0 - sg))).astype(BF16)
        o_ref[:, f:] = (dav * silu).astype(BF16)

    return pl.pallas_call(
        body, name=name, grid=(t // tr,),
        in_specs=[pl.BlockSpec((tr, f), lambda i: (i, 0)), pl.BlockSpec((tr, f), lambda i: (i, 1)),
                  pl.BlockSpec((tr, f), lambda i: (i, 0))],
        out_specs=pl.BlockSpec((tr, f2), lambda i: (i, 0)),
        out_shape=jax.ShapeDtypeStruct((t, f2), BF16),
        compiler_params=_params(("parallel",)),
    )(u, u, da)


def _loss_fwd_bwd(h, target, *, name):
    t, d = h.shape
    tr = _blk(t, ROW_BLOCK)

    def body(h_ref, t_ref, dh_ref, l_ref):
        i = pl.program_id(0)
        err = h_ref[...] - t_ref[...]
        dh_ref[...] = err * (1.0 / d)

        @pl.when(i == 0)
        def _():
            l_ref[...] = jnp.zeros_like(l_ref)

        l_ref[...] += _fold8(err * err)

    dh, part = pl.pallas_call(
        body, name=name, grid=(t // tr,),
        in_specs=[pl.BlockSpec((tr, d), lambda i: (i, 0)), pl.BlockSpec((tr, d), lambda i: (i, 0))],
        out_specs=[pl.BlockSpec((tr, d), lambda i: (i, 0)), pl.BlockSpec((SUBLANES, d), lambda i: (0, 0))],
        out_shape=[jax.ShapeDtypeStruct((t, d), F32), jax.ShapeDtypeStruct((SUBLANES, d), F32)],
        compiler_params=_params(("arbitrary",)),
    )(h, target)
    return jnp.sum(part) * (0.5 / d), dh


def _seg_matrix():
    r = lax.broadcasted_iota(jnp.int32, (LANES, LANES), 0) // HEAD_DIM
    c = lax.broadcasted_iota(jnp.int32, (LANES, LANES), 1) // HEAD_DIM
    return (r == c).astype(BF16)


def _head_sum(x, seg):
    hi = x.astype(BF16)
    r1 = x - hi.astype(F32)
    mid = r1.astype(BF16)
    lo = (r1 - mid.astype(F32)).astype(BF16)
    dot = functools.partial(jnp.dot, preferred_element_type=F32)
    return dot(hi, seg) + dot(mid, seg) + dot(lo, seg)


def _lane_in_head(shape):
    return lax.broadcasted_iota(jnp.int32, shape, 1) % HEAD_DIM


def _rot_partner(x):
    up = pltpu.roll(x, LANES - ROT_DIM // 2, 1)
    down = pltpu.roll(x, ROT_DIM // 2, 1)
    return jnp.where(_lane_in_head(x.shape) < ROT_DIM // 2, up, down)


def _rope_tables(positions):
    inv_freq = ROPE_THETA ** (-jnp.arange(0, ROT_DIM, 2, dtype=F32) / ROT_DIM)
    ang = positions.astype(F32)[:, None] * inv_freq
    t = ang.shape[0]
    rest = HEAD_DIM - ROT_DIM
    cos = jnp.concatenate([jnp.cos(ang), jnp.cos(ang), jnp.ones((t, rest), F32)], axis=1)
    sin = jnp.concatenate([-jnp.sin(ang), jnp.sin(ang), jnp.zeros((t, rest), F32)], axis=1)
    return jnp.tile(cos, (1, LANES // HEAD_DIM)), jnp.tile(sin, (1, LANES // HEAD_DIM))


def _kind_is(j, kinds, kind):
    hits = [j == jj for jj, k in enumerate(kinds) if k == kind]
    return functools.reduce(jnp.logical_or, hits) if hits else None


def _hn_fwd(x, gains, kinds, d, cos, sin, *, name):
    t = x.shape[0]
    n = len(kinds)
    tr = _blk(t, ROW_BLOCK)
    seg = _seg_matrix()
    g8 = jnp.repeat(gains.astype(F32), SUBLANES, axis=0)

    def body(x_ref, g_ref, seg_ref, cos_ref, sin_ref, o_ref):
        j = pl.program_id(1)

        def normed(rope):
            for c in range(d // LANES):
                sl = slice(c * LANES, (c + 1) * LANES)
                xv = x_ref[:, sl]
                ms = _head_sum(xv * xv, seg_ref[...]) * (1.0 / HEAD_DIM)
                y = (xv * lax.rsqrt(ms + NORM_EPS)) * g_ref[0:1, sl]
                if rope:
                    y = y * cos_ref[...] + _rot_partner(y) * sin_ref[...]
                o_ref[:, sl] = y.astype(BF16)

        for kind in ("rope", "norm"):
            hit = _kind_is(j, kinds, kind)
            if hit is not None:
                pl.when(hit)(functools.partial(normed, kind == "rope"))
        hit = _kind_is(j, kinds, "cast")
        if hit is not None:
            @pl.when(hit)
            def _():
                o_ref[...] = x_ref[...].astype(BF16)

    return pl.pallas_call(
        body, name=name, grid=(t // tr, n),
        in_specs=[pl.BlockSpec((tr, d), lambda i, j: (i, j)), pl.BlockSpec((SUBLANES, d), lambda i, j: (j, 0)),
                  pl.BlockSpec((LANES, LANES), lambda i, j: (0, 0)),
                  pl.BlockSpec((tr, LANES), lambda i, j: (i, 0)), pl.BlockSpec((tr, LANES), lambda i, j: (i, 0))],
        out_specs=pl.BlockSpec((tr, d), lambda i, j: (i, j)),
        out_shape=jax.ShapeDtypeStruct((t, n * d), BF16),
        compiler_params=_params(("parallel", "parallel")),
    )(x, g8, seg, cos, sin)


def _hn_bwd(x, dy, gains, kinds, d, cos, sin, *, name):
    t = x.shape[0]
    n = len(kinds)
    tr = _blk(t, ROW_BLOCK)
    seg = _seg_matrix()
    g8 = jnp.repeat(gains.astype(F32), SUBLANES, axis=0)

    def body(x_ref, dy_ref, g_ref, seg_ref, cos_ref, sin_ref, dx_ref, dg_ref):
        j = pl.program_id(0)
        i = pl.program_id(1)

        @pl.when(i == 0)
        def _():
            dg_ref[...] = jnp.zeros_like(dg_ref)

        def normed(rope):
            for c in range(d // LANES):
                sl = slice(c * LANES, (c + 1) * LANES)
                xv = x_ref[:, sl]
                dyv = dy_ref[:, sl]
                if rope:
                    dyv = dyv * cos_ref[...] - _rot_partner(dyv) * sin_ref[...]
                ms = _head_sum(xv * xv, seg_ref[...]) * (1.0 / HEAD_DIM)
                rstd = lax.rsqrt(ms + NORM_EPS)
                xhat = xv * rstd
                dg_ref[:, sl] += _fold8(dyv * xhat)
                dyg = dyv * g_ref[0:1, sl]
                proj = _head_sum(dyg * xhat, seg_ref[...]) * (1.0 / HEAD_DIM)
                dx_ref[:, sl] = (rstd * (dyg - xhat * proj)).astype(BF16)

        for kind in ("rope", "norm"):
            hit = _kind_is(j, kinds, kind)
            if hit is not None:
                pl.when(hit)(functools.partial(normed, kind == "rope"))
        hit = _kind_is(j, kinds, "cast")
        if hit is not None:
            @pl.when(hit)
            def _():
                dx_ref[...] = dy_ref[...].astype(BF16)

    dx, dg = pl.pallas_call(
        body, name=name, grid=(n, t // tr),
        in_specs=[pl.BlockSpec((tr, d), lambda j, i: (i, j)), pl.BlockSpec((tr, d), lambda j, i: (i, j)),
                  pl.BlockSpec((SUBLANES, d), lambda j, i: (j, 0)),
                  pl.BlockSpec((LANES, LANES), lambda j, i: (0, 0)),
                  pl.BlockSpec((tr, LANES), lambda j, i: (i, 0)), pl.BlockSpec((tr, LANES), lambda j, i: (i, 0))],
        out_specs=[pl.BlockSpec((tr, d), lambda j, i: (i, j)), pl.BlockSpec((SUBLANES, d), lambda j, i: (j, 0))],
        out_shape=[jax.ShapeDtypeStruct((t, n * d), BF16), jax.ShapeDtypeStruct((n * SUBLANES, d), F32)],
        compiler_params=_params(("arbitrary", "arbitrary")),
    )(x, dy, g8, seg, cos, sin)
    dg = dg.reshape(n, SUBLANES, d // HEAD_DIM, HEAD_DIM).sum(axis=(1, 2))
    return dx, dg


def _head_dot(a, b, *, name):
    t, d = a.shape
    tr = _blk(t, ROW_BLOCK)
    seg = _seg_matrix()

    def body(a_ref, b_ref, seg_ref, o_ref):
        for c in range(d // LANES):
            sl = slice(c * LANES, (c + 1) * LANES)
            o_ref[:, sl] = _head_sum(a_ref[:, sl].astype(BF16).astype(F32) * b_ref[:, sl], seg_ref[...])

    return pl.pallas_call(
        body, name=name, grid=(t // tr,),
        in_specs=[pl.BlockSpec((tr, d), lambda i: (i, 0)), pl.BlockSpec((tr, d), lambda i: (i, 0)),
                  pl.BlockSpec((LANES, LANES), lambda i: (0, 0))],
        out_specs=pl.BlockSpec((tr, d), lambda i: (i, 0)),
        out_shape=jax.ShapeDtypeStruct((t, d), F32),
        compiler_params=_params(("parallel",)),
    )(a, b, seg)


def _half_mask(shape):
    return lax.broadcasted_iota(jnp.int32, shape, 1) < HEAD_DIM


def _band_valid(first):
    qi = lax.broadcasted_iota(jnp.int32, (BAND, 2 * BAND), 0)
    kj = lax.broadcasted_iota(jnp.int32, (BAND, 2 * BAND), 1)
    dist = qi + BAND - kj
    return (dist >= 0) & (dist <= BAND) & ((kj >= BAND) | jnp.logical_not(first))


_NT = (((1,), (1,)), ((), ()))
_TN = (((0,), (0,)), ((), ()))


def _dot2(p, v):
    hi = p.astype(BF16)
    lo = (p - hi.astype(F32)).astype(BF16)
    return jnp.dot(hi, v, preferred_element_type=F32) + jnp.dot(lo, v, preferred_element_type=F32)


def _band_fwd(qkv, g, dil, cfg, *, name):
    t, d = cfg.tokens, cfg.d_model
    w = 9 * d
    rows = t // dil
    nbt = rows // BAND
    nb = cfg.seq // (dil * BAND)
    view = qkv.reshape(rows, dil * w)
    ncol = w // d

    def body(q_ref, kp_ref, kc_ref, vp_ref, vc_ref, o_ref, lse_ref):
        i = pl.program_id(1)
        valid = _band_valid(i % nb == 0)
        half = _half_mask((BAND, LANES))
        for hp in range(d // LANES):
            sl = slice(hp * LANES, (hp + 1) * LANES)
            q2 = q_ref[:, sl]
            kk = jnp.concatenate([kp_ref[:, sl], kc_ref[:, sl]], axis=0)
            vv = jnp.concatenate([vp_ref[:, sl], vc_ref[:, sl]], axis=0)
            outs, lses = [], []
            for e in range(2):
                qe = jnp.where(half == (e == 0), q2, jnp.zeros_like(q2))
                s = lax.dot_general(qe, kk, _NT, preferred_element_type=F32)
                s = jnp.where(valid, s, NEG)
                m = jnp.max(s, axis=1, keepdims=True)
                p = jnp.exp(s - m)
                l = jnp.sum(p, axis=1, keepdims=True)
                outs.append(_dot2(p * (1.0 / l), vv))
                lses.append(m + jnp.log(l))
            o_ref[:, sl] = jnp.where(half, outs[0], outs[1])
            lse_ref[:, sl] = jnp.where(half, lses[0], lses[1])

    def col(which):
        return lambda r, i: (i, r * ncol + 3 * g + which)

    def col_prev(which):
        return lambda r, i: (jnp.maximum(i - 1, 0), r * ncol + 3 * g + which)

    blk = (BAND, d)
    o, lse = pl.pallas_call(
        body, name=name, grid=(dil, nbt),
        in_specs=[pl.BlockSpec(blk, col(0)), pl.BlockSpec(blk, col_prev(1)), pl.BlockSpec(blk, col(1)),
                  pl.BlockSpec(blk, col_prev(2)), pl.BlockSpec(blk, col(2))],
        out_specs=[pl.BlockSpec(blk, lambda r, i: (i, r)), pl.BlockSpec(blk, lambda r, i: (i, r))],
        out_shape=[jax.ShapeDtypeStruct((rows, dil * d), F32), jax.ShapeDtypeStruct((rows, dil * d), F32)],
        compiler_params=_params(("parallel", "arbitrary")),
    )(view, view, view, view, view)
    return o.reshape(t, d), lse.reshape(t, d)


def _band_bwd(qkv, dmixed, lse_all, dsum, g, dil, cfg, *, name):
    t, d = cfg.tokens, cfg.d_model
    w = 9 * d
    rows = t // dil
    nbt = rows // BAND
    nb = cfg.seq // (dil * BAND)
    view = qkv.reshape(rows, dil * w)
    ncol = w // d
    do_v, l_v, d_v = (z.reshape(rows, dil * d) for z in (dmixed, lse_all, dsum))

    def body(q_ref, kp_ref, kc_ref, vp_ref, vc_ref, do_ref, l_ref, ds_ref, dq_ref, dk_ref, dv_ref, ck_ref, cv_ref):
        i = pl.program_id(1)

        @pl.when(i < nbt)
        def _():
            valid = _band_valid(i % nb == 0)
            half = _half_mask((BAND, LANES))
            half2 = _half_mask((2 * BAND, LANES))
            for hp in range(d // LANES):
                sl = slice(hp * LANES, (hp + 1) * LANES)
                q2 = q_ref[:, sl]
                kk = jnp.concatenate([kp_ref[:, sl], kc_ref[:, sl]], axis=0)
                vv = jnp.concatenate([vp_ref[:, sl], vc_ref[:, sl]], axis=0)
                do2 = do_ref[:, sl].astype(BF16)
                dqs, dks, dvs = [], [], []
                for e in range(2):
                    lane0 = e * HEAD_DIM
                    keep = half == (e == 0)
                    qe = jnp.where(keep, q2, jnp.zeros_like(q2))
                    doe = jnp.where(keep, do2, jnp.zeros_like(do2))
                    s = lax.dot_general(qe, kk, _NT, preferred_element_type=F32)
                    s = jnp.where(valid, s, NEG)
                    p = jnp.exp(s - l_ref[:, hp * LANES + lane0:hp * LANES + lane0 + 1])
                    dp = lax.dot_general(doe, vv, _NT, preferred_element_type=F32)
                    dsc = (p * (dp - ds_ref[:, hp * LANES + lane0:hp * LANES + lane0 + 1])).astype(BF16)
                    dqs.append(jnp.dot(dsc, kk, preferred_element_type=F32))
                    dks.append(lax.dot_general(dsc, q2, _TN, preferred_element_type=F32))
                    dvs.append(lax.dot_general(p.astype(BF16), do2, _TN, preferred_element_type=F32))
                dq_ref[:, sl] = jnp.where(half, dqs[0], dqs[1])
                dkk = jnp.where(half2, dks[0], dks[1])
                dvv = jnp.where(half2, dvs[0], dvs[1])

                @pl.when(i > 0)
                def _():
                    dk_ref[:, sl] = ck_ref[:, sl] + dkk[:BAND]
                    dv_ref[:, sl] = cv_ref[:, sl] + dvv[:BAND]

                ck_ref[:, sl] = dkk[BAND:]
                cv_ref[:, sl] = dvv[BAND:]

        @pl.when(i == nbt)
        def _():
            dk_ref[...] = ck_ref[...]
            dv_ref[...] = cv_ref[...]

    def cur(i):
        return jnp.minimum(i, nbt - 1)

    def col(which):
        return lambda r, i: (cur(i), r * ncol + 3 * g + which)

    def col_prev(which):
        return lambda r, i: (jnp.maximum(cur(i) - 1, 0), r * ncol + 3 * g + which)

    blk = (BAND, d)
    here = pl.BlockSpec(blk, lambda r, i: (cur(i), r))
    behind = pl.BlockSpec(blk, lambda r, i: (jnp.maximum(i - 1, 0), r))
    shape = jax.ShapeDtypeStruct((rows, dil * d), F32)
    dq, dk, dv = pl.pallas_call(
        body, name=name, grid=(dil, nbt + 1),
        in_specs=[pl.BlockSpec(blk, col(0)), pl.BlockSpec(blk, col_prev(1)), pl.BlockSpec(blk, col(1)),
                  pl.BlockSpec(blk, col_prev(2)), pl.BlockSpec(blk, col(2)), here, here, here],
        out_specs=[here, behind, behind],
        out_shape=[shape, shape, shape],
        scratch_shapes=[pltpu.VMEM(blk, F32), pltpu.VMEM(blk, F32)],
        compiler_params=_params(("arbitrary", "arbitrary")),
    )(view, view, view, view, view, do_v, l_v, d_v)
    return dq.reshape(t, d), dk.reshape(t, d), dv.reshape(t, d)


def _mix_fwd(outs, lses, *, name):
    t, d = outs[0].shape
    tr = _blk(t, ROW_BLOCK)
    ng = len(outs)

    def body(*refs):
        o_refs, l_refs = refs[:ng], refs[ng:2 * ng]
        mixed_ref, lse_ref = refs[2 * ng:]
        ls = [r[...] for r in l_refs]
        m = functools.reduce(jnp.maximum, ls)
        es = [jnp.exp(l - m) for l in ls]
        tot = functools.reduce(jnp.add, es)
        inv = 1.0 / tot
        mixed_ref[...] = functools.reduce(jnp.add, [(e * inv) * r[...] for e, r in zip(es, o_refs)])
        lse_ref[...] = m + jnp.log(tot)

    spec = pl.BlockSpec((tr, d), lambda i: (i, 0))
    return pl.pallas_call(
        body, name=name, grid=(t // tr,),
        in_specs=[spec] * (2 * ng), out_specs=[spec, spec],
        out_shape=[jax.ShapeDtypeStruct((t, d), F32), jax.ShapeDtypeStruct((t, d), F32)],
        compiler_params=_params(("parallel",)),
    )(*outs, *lses)


GATE_BLOCK = 256


def _tri(n, upper):
    r = lax.broadcasted_iota(jnp.int32, (n, n), 0)
    c = lax.broadcasted_iota(jnp.int32, (n, n), 1)
    return ((c >= r) if upper else (c <= r)).astype(BF16)


def _tri_dot(tri, x):
    hi = x.astype(BF16)
    r1 = x - hi.astype(F32)
    mid = r1.astype(BF16)
    lo = (r1 - mid.astype(F32)).astype(BF16)
    dot = functools.partial(jnp.dot, preferred_element_type=F32)
    return dot(tri, hi) + dot(tri, mid) + dot(tri, lo)


def _log_sigmoid(z):
    return jnp.minimum(z, 0.0) - jnp.log(1.0 + jnp.exp(-jnp.abs(z)))


def _gate_fwd(proj, col_block, bias, cfg, *, name):
    tr = _blk(cfg.seq, GATE_BLOCK)
    nblk = cfg.seq // tr

    def body(z_ref, b_ref, tri_ref, o_ref, carry_ref):
        i = pl.program_id(1)

        @pl.when(i == 0)
        def _():
            carry_ref[...] = jnp.zeros_like(carry_ref)

        logf = _log_sigmoid(z_ref[...] + b_ref[0:1, :])
        cum = _tri_dot(tri_ref[...], logf) + carry_ref[0:1, :]
        o_ref[...] = cum
        carry_ref[...] = jnp.broadcast_to(cum[tr - 1:tr, :], carry_ref.shape)

    return pl.pallas_call(
        body, name=name, grid=(cfg.batch, nblk),
        in_specs=[pl.BlockSpec((tr, LANES), lambda b, i: (b * nblk + i, col_block)),
                  pl.BlockSpec((SUBLANES, LANES), lambda b, i: (0, 0)),
                  pl.BlockSpec((tr, tr), lambda b, i: (0, 0))],
        out_specs=pl.BlockSpec((tr, LANES), lambda b, i: (b * nblk + i, 0)),
        out_shape=jax.ShapeDtypeStruct((cfg.tokens, LANES), F32),
        scratch_shapes=[pltpu.VMEM((SUBLANES, LANES), F32)],
        compiler_params=_params(("arbitrary", "arbitrary")),
    )(proj, jnp.broadcast_to(bias, (SUBLANES, LANES)), _tri(tr, upper=False))


def _gate_bwd(proj, col_block, bias, dcum, cfg, *, name):
    tr = _blk(cfg.seq, GATE_BLOCK)
    nblk = cfg.seq // tr

    def body(z_ref, b_ref, tri_ref, dc_ref, dz_ref, db_ref, carry_ref):
        b = pl.program_id(0)
        i = pl.program_id(1)

        @pl.when(i == 0)
        def _():
            carry_ref[...] = jnp.zeros_like(carry_ref)

        @pl.when((i == 0) & (b == 0))
        def _():
            db_ref[...] = jnp.zeros_like(db_ref)

        dcv = dc_ref[...]
        dlogf = _tri_dot(tri_ref[...], dcv) + carry_ref[0:1, :]
        carry_ref[...] = jnp.broadcast_to(dlogf[0:1, :], carry_ref.shape)
        dz = dlogf * jax.nn.sigmoid(-(z_ref[...] + b_ref[0:1, :]))
        dz_ref[...] = dz
        db_ref[...] += _fold8(dz)

    def rev(b, i):
        return (b * nblk + nblk - 1 - i, 0)

    dz, db = pl.pallas_call(
        body, name=name, grid=(cfg.batch, nblk),
        in_specs=[pl.BlockSpec((tr, LANES), lambda b, i: (b * nblk + nblk - 1 - i, col_block)),
                  pl.BlockSpec((SUBLANES, LANES), lambda b, i: (0, 0)),
                  pl.BlockSpec((tr, tr), lambda b, i: (0, 0)),
                  pl.BlockSpec((tr, LANES), rev)],
        out_specs=[pl.BlockSpec((tr, LANES), rev), pl.BlockSpec((SUBLANES, LANES), lambda b, i: (0, 0))],
        out_shape=[jax.ShapeDtypeStruct((cfg.tokens, LANES), F32), jax.ShapeDtypeStruct((SUBLANES, LANES), F32)],
        scratch_shapes=[pltpu.VMEM((SUBLANES, LANES), F32)],
        compiler_params=_params(("arbitrary", "arbitrary")),
    )(proj, jnp.broadcast_to(bias, (SUBLANES, LANES)), _tri(tr, upper=True), dcum)
    return dz, jnp.sum(db, axis=0)


FOX_BLOCK = 256


def _fox_scores(q2, k2, e, half, mask, cref, ck_row):
    qe = jnp.where(half == (e == 0), q2, jnp.zeros_like(q2))
    s = lax.dot_general(qe, k2, _NT, preferred_element_type=F32)
    return jnp.where(mask, s + (cref - ck_row), NEG)


def _causal(qi, ki, tq):
    r = lax.broadcasted_iota(jnp.int32, (tq, tq), 0) + qi * tq
    c = lax.broadcasted_iota(jnp.int32, (tq, tq), 1) + ki * tq
    return r >= c


def _fox_fwd(q, kv, cum_t, cfg, *, name):
    t, d, hrows = cfg.tokens, cfg.d_model, cum_t.shape[0]
    tq = _blk(cfg.seq, FOX_BLOCK)
    nq = cfg.seq // tq

    def body(q_ref, k_ref, v_ref, cq_ref, ck_ref, o_ref, lse_ref, m_ref, l_ref, acc_ref):
        qi, ki = pl.program_id(1), pl.program_id(2)

        @pl.when(ki == 0)
        def _():
            m_ref[...] = jnp.full_like(m_ref, NEG)
            l_ref[...] = jnp.zeros_like(l_ref)
            acc_ref[...] = jnp.zeros_like(acc_ref)

        @pl.when(ki <= qi)
        def _():
            mask = _causal(qi, ki, tq)
            half = _half_mask((tq, LANES))
            for hp in range(d // LANES):
                sl = slice(hp * LANES, (hp + 1) * LANES)
                q2, k2, v2 = q_ref[:, sl], k_ref[:, sl], v_ref[:, sl]
                alphas, pvs = [], []
                for e in range(2):
                    h = 2 * hp + e
                    s = _fox_scores(q2, k2, e, half, mask, cq_ref[h:h + 1, 0:1], ck_ref[h:h + 1, :])
                    m_prev = m_ref[h]
                    m_new = jnp.maximum(m_prev, jnp.max(s, axis=1, keepdims=True))
                    alpha = jnp.exp(m_prev - m_new)
                    p = jnp.exp(s - m_new[:, 0:1])
                    l_ref[h] = alpha * l_ref[h] + jnp.sum(p, axis=1, keepdims=True)
                    m_ref[h] = m_new
                    alphas.append(alpha)
                    pvs.append(_dot2(p, v2))
                acc = acc_ref[:, sl]
                acc_ref[:, sl] = jnp.where(half, alphas[0] * acc + pvs[0], alphas[1] * acc + pvs[1])

        @pl.when(ki == qi)
        def _():
            half = _half_mask((tq, LANES))
            for hp in range(d // LANES):
                sl = slice(hp * LANES, (hp + 1) * LANES)
                h0, h1 = 2 * hp, 2 * hp + 1
                inv = jnp.where(half, 1.0 / l_ref[h0], 1.0 / l_ref[h1])
                o_ref[:, sl] = acc_ref[:, sl] * inv
                lse0 = m_ref[h0] + jnp.log(l_ref[h0]) - cq_ref[h0:h0 + 1, 0:1]
                lse1 = m_ref[h1] + jnp.log(l_ref[h1]) - cq_ref[h1:h1 + 1, 0:1]
                lse_ref[:, sl] = jnp.where(half, lse0, lse1)

    def qrow(b, qi, ki):
        return (b * nq + qi, 0)

    def krow(b, qi, ki):
        return (b * nq + jnp.minimum(ki, qi), 0)

    o, lse = pl.pallas_call(
        body, name=name, grid=(cfg.batch, nq, nq),
        in_specs=[pl.BlockSpec((tq, d), qrow),
                  pl.BlockSpec((tq, d), krow),
                  pl.BlockSpec((tq, d), lambda b, qi, ki: (b * nq + jnp.minimum(ki, qi), 1)),
                  pl.BlockSpec((hrows, tq), lambda b, qi, ki: (0, b * nq + qi)),
                  pl.BlockSpec((hrows, tq), lambda b, qi, ki: (0, b * nq + jnp.minimum(ki, qi)))],
        out_specs=[pl.BlockSpec((tq, d), qrow), pl.BlockSpec((tq, d), qrow)],
        out_shape=[jax.ShapeDtypeStruct((t, d), F32), jax.ShapeDtypeStruct((t, d), F32)],
        scratch_shapes=[pltpu.VMEM((cfg.heads, tq, LANES), F32), pltpu.VMEM((cfg.heads, tq, LANES), F32),
                        pltpu.VMEM((tq, d), F32)],
        compiler_params=_params(("parallel", "parallel", "arbitrary")),
    )(q, kv, kv, cum_t, cum_t)
    return o, lse


def _fox_bwd_q(q, kv, cum_t, do, lse, dsum, cfg, *, name):
    t, d, hrows = cfg.tokens, cfg.d_model, cum_t.shape[0]
    tq = _blk(cfg.seq, FOX_BLOCK)
    nq = cfg.seq // tq

    def body(q_ref, k_ref, v_ref, cq_ref, ck_ref, do_ref, l_ref, ds_ref, dq_ref, acc_ref):
        qi, ki = pl.program_id(1), pl.program_id(2)

        @pl.when(ki == 0)
        def _():
            acc_ref[...] = jnp.zeros_like(acc_ref)

        @pl.when(ki <= qi)
        def _():
            mask = _causal(qi, ki, tq)
            half = _half_mask((tq, LANES))
            for hp in range(d // LANES):
                sl = slice(hp * LANES, (hp + 1) * LANES)
                q2, k2, v2 = q_ref[:, sl], k_ref[:, sl], v_ref[:, sl]
                do2 = do_ref[:, sl].astype(BF16)
                dqs = []
                for e in range(2):
                    h = 2 * hp + e
                    lane0 = hp * LANES + e * HEAD_DIM
                    cref = cq_ref[h:h + 1, 0:1]
                    s = _fox_scores(q2, k2, e, half, mask, cref, ck_ref[h:h + 1, :])
                    p = jnp.exp(s - (l_ref[:, lane0:lane0 + 1] + cref))
                    doe = jnp.where(half == (e == 0), do2, jnp.zeros_like(do2))
                    dp = lax.dot_general(doe, v2, _NT, preferred_element_type=F32)
                    dsc = (p * (dp - ds_ref[:, lane0:lane0 + 1])).astype(BF16)
                    dqs.append(jnp.dot(dsc, k2, preferred_element_type=F32))
                acc_ref[:, sl] += jnp.where(half, dqs[0], dqs[1])

        @pl.when(ki == qi)
        def _():
            dq_ref[...] = acc_ref[...]

    def qrow(b, qi, ki):
        return (b * nq + qi, 0)

    return pl.pallas_call(
        body, name=name, grid=(cfg.batch, nq, nq),
        in_specs=[pl.BlockSpec((tq, d), qrow),
                  pl.BlockSpec((tq, d), lambda b, qi, ki: (b * nq + jnp.minimum(ki, qi), 0)),
                  pl.BlockSpec((tq, d), lambda b, qi, ki: (b * nq + jnp.minimum(ki, qi), 1)),
                  pl.BlockSpec((hrows, tq), lambda b, qi, ki: (0, b * nq + qi)),
                  pl.BlockSpec((hrows, tq), lambda b, qi, ki: (0, b * nq + jnp.minimum(ki, qi))),
                  pl.BlockSpec((tq, d), qrow), pl.BlockSpec((tq, d), qrow), pl.BlockSpec((tq, d), qrow)],
        out_specs=pl.BlockSpec((tq, d), qrow),
        out_shape=jax.ShapeDtypeStruct((t, d), F32),
        scratch_shapes=[pltpu.VMEM((tq, d), F32)],
        compiler_params=_params(("parallel", "parallel", "arbitrary")),
    )(q, kv, kv, cum_t, cum_t, do, lse, dsum)


def _fox_bwd_kv(q, kv, cum_t, do, lse, dsum, cfg, *, name):
    t, d, hrows = cfg.tokens, cfg.d_model, cum_t.shape[0]
    tq = _blk(cfg.seq, FOX_BLOCK)
    nq = cfg.seq // tq

    def body(q_ref, k_ref, v_ref, cq_ref, ck_ref, do_ref, l_ref, ds_ref, dk_ref, dv_ref, dc_ref,
             kacc_ref, vacc_ref, cacc_ref):
        ki, qi = pl.program_id(1), pl.program_id(2)

        @pl.when(qi == 0)
        def _():
            kacc_ref[...] = jnp.zeros_like(kacc_ref)
            vacc_ref[...] = jnp.zeros_like(vacc_ref)
            cacc_ref[...] = jnp.zeros_like(cacc_ref)

        @pl.when(qi >= ki)
        def _():
            mask = _causal(qi, ki, tq)
            half = _half_mask((tq, LANES))
            for hp in range(d // LANES):
                sl = slice(hp * LANES, (hp + 1) * LANES)
                q2, k2, v2 = q_ref[:, sl], k_ref[:, sl], v_ref[:, sl]
                do2 = do_ref[:, sl].astype(BF16)
                dks, dvs = [], []
                for e in range(2):
                    h = 2 * hp + e
                    lane0 = hp * LANES + e * HEAD_DIM
                    cref = cq_ref[h:h + 1, 0:1]
                    s = _fox_scores(q2, k2, e, half, mask, cref, ck_ref[h:h + 1, :])
                    p = jnp.exp(s - (l_ref[:, lane0:lane0 + 1] + cref))
                    doe = jnp.where(half == (e == 0), do2, jnp.zeros_like(do2))
                    dp = lax.dot_general(doe, v2, _NT, preferred_element_type=F32)
                    dsf = p * (dp - ds_ref[:, lane0:lane0 + 1])
                    cacc_ref[h:h + 1, :] -= jnp.sum(dsf, axis=0, keepdims=True)
                    dks.append(lax.dot_general(dsf.astype(BF16), q2, _TN, preferred_element_type=F32))
                    dvs.append(lax.dot_general(p.astype(BF16), do2, _TN, preferred_element_type=F32))
                kacc_ref[:, sl] += jnp.where(half, dks[0], dks[1])
                vacc_ref[:, sl] += jnp.where(half, dvs[0], dvs[1])

        @pl.when(qi == nq - 1)
        def _():
            dk_ref[...] = kacc_ref[...]
            dv_ref[...] = vacc_ref[...]
            dc_ref[...] = cacc_ref[...]

    def qrow(b, ki, qi):
        return (b * nq + jnp.maximum(qi, ki), 0)

    def krow(b, ki, qi):
        return (b * nq + ki, 0)

    return pl.pallas_call(
        body, name=name, grid=(cfg.batch, nq, nq),
        in_specs=[pl.BlockSpec((tq, d), qrow),
                  pl.BlockSpec((tq, d), krow),
                  pl.BlockSpec((tq, d), lambda b, ki, qi: (b * nq + ki, 1)),
                  pl.BlockSpec((hrows, tq), lambda b, ki, qi: (0, b * nq + jnp.maximum(qi, ki))),
                  pl.BlockSpec((hrows, tq), lambda b, ki, qi: (0, b * nq + ki)),
                  pl.BlockSpec((tq, d), qrow), pl.BlockSpec((tq, d), qrow), pl.BlockSpec((tq, d), qrow)],
        out_specs=[pl.BlockSpec((tq, d), krow), pl.BlockSpec((tq, d), krow),
                   pl.BlockSpec((hrows, tq), lambda b, ki, qi: (0, b * nq + ki))],
        out_shape=[jax.ShapeDtypeStruct((t, d), F32), jax.ShapeDtypeStruct((t, d), F32),
                   jax.ShapeDtypeStruct((hrows, t), F32)],
        scratch_shapes=[pltpu.VMEM((tq, d), F32), pltpu.VMEM((tq, d), F32), pltpu.VMEM((hrows, tq), F32)],
        compiler_params=_params(("parallel", "parallel", "arbitrary")),
    )(q, kv, kv, cum_t, cum_t, do, lse, dsum)


AUG = LANES
BIAS_TERMS = 3


def _fox_aug_q(qp, cfg):
    t, hh = cfg.tokens, cfg.heads
    q3 = qp.reshape(t, hh, HEAD_DIM)
    ones = jnp.ones((t, hh, BIAS_TERMS), BF16)
    zeros = jnp.zeros((t, hh, AUG - HEAD_DIM - BIAS_TERMS), BF16)
    return jnp.concatenate([q3, ones, zeros], axis=2).reshape(t, hh * AUG).T


def _fox_aug_k(k, cum, cfg):
    t, hh = cfg.tokens, cfg.heads
    c = -cum
    hi = lax.reduce_precision(c, 8, 7)
    mid = lax.reduce_precision(c - hi, 8, 7)
    lo = c - hi - mid
    zeros = jnp.zeros((t, hh, AUG - HEAD_DIM - BIAS_TERMS), BF16)
    parts = [k.reshape(t, hh, HEAD_DIM)] + [z.astype(BF16)[..., None] for z in (hi, mid, lo)] + [zeros]
    return jnp.concatenate(parts, axis=2).reshape(t, hh * AUG)


def _keys_visible(tq):
    s = lax.broadcasted_iota(jnp.int32, (tq, tq), 0)
    t = lax.broadcasted_iota(jnp.int32, (tq, tq), 1)
    return s <= t


def _fox_fwd_t(qa_t, k_aug, v_t, cfg, *, name):
    t, d, hh = cfg.tokens, cfg.d_model, cfg.heads
    tq = _blk(cfg.seq, FOX_BLOCK)
    nq = cfg.seq // tq

    def body(qa_ref, ka_ref, vt_ref, o_ref, lse_ref, m_ref, l_ref, acc_ref):
        qi, ki = pl.program_id(1), pl.program_id(2)

        @pl.when(ki == 0)
        def _():
            m_ref[...] = jnp.full_like(m_ref, NEG)
            l_ref[...] = jnp.zeros_like(l_ref)
            acc_ref[...] = jnp.zeros_like(acc_ref)

        def step(diagonal):
            for h in range(hh):
                rows = slice(h * HEAD_DIM, (h + 1) * HEAD_DIM)
                s = jnp.dot(ka_ref[:, h * AUG:(h + 1) * AUG], qa_ref[h * AUG:(h + 1) * AUG, :],
                            preferred_element_type=F32)
                if diagonal:
                    s = jnp.where(_keys_visible(tq), s, NEG)
                m_prev = m_ref[h:h + 1, :]
                m_new = jnp.maximum(m_prev, jnp.max(s, axis=0, keepdims=True))
                alpha = jnp.exp(m_prev - m_new)
                p = jnp.exp(s - m_new)
                l_ref[h:h + 1, :] = alpha * l_ref[h:h + 1, :] + jnp.sum(p, axis=0, keepdims=True)
                m_ref[h:h + 1, :] = m_new
                hi = p.astype(BF16)
                lo = (p - hi.astype(F32)).astype(BF16)
                vt = vt_ref[rows, :]
                acc_ref[rows, :] = (alpha * acc_ref[rows, :] + jnp.dot(vt, hi, preferred_element_type=F32)
                                    + jnp.dot(vt, lo, preferred_element_type=F32))

        pl.when(ki < qi)(functools.partial(step, False))
        pl.when(ki == qi)(functools.partial(step, True))

        @pl.when(ki == qi)
        def _():
            for h in range(hh):
                rows = slice(h * HEAD_DIM, (h + 1) * HEAD_DIM)
                o_ref[rows, :] = acc_ref[rows, :] * (1.0 / l_ref[h:h + 1, :])
            lse_ref[...] = m_ref[...] + jnp.log(l_ref[...])

    def qcol(b, qi, ki):
        return (0, b * nq + qi)

    return pl.pallas_call(
        body, name=name, grid=(cfg.batch, nq, nq),
        in_specs=[pl.BlockSpec((hh * AUG, tq), qcol),
                  pl.BlockSpec((tq, hh * AUG), lambda b, qi, ki: (b * nq + jnp.minimum(ki, qi), 0)),
                  pl.BlockSpec((d, tq), lambda b, qi, ki: (0, b * nq + jnp.minimum(ki, qi)))],
        out_specs=[pl.BlockSpec((d, tq), qcol), pl.BlockSpec((hh, tq), qcol)],
        out_shape=[jax.ShapeDtypeStruct((d, t), F32), jax.ShapeDtypeStruct((hh, t), F32)],
        scratch_shapes=[pltpu.VMEM((hh, tq), F32), pltpu.VMEM((hh, tq), F32), pltpu.VMEM((d, tq), F32)],
        compiler_params=_params(("parallel", "parallel", "arbitrary")),
    )(qa_t, k_aug, v_t)


def _head_dot_t(a_t, b_t, cfg, *, name):
    t, d, hh = cfg.tokens, cfg.d_model, cfg.heads
    tc = _blk(t, 2 * ROW_BLOCK)

    def body(a_ref, b_ref, o_ref):
        for h in range(hh):
            rows = slice(h * HEAD_DIM, (h + 1) * HEAD_DIM)
            o_ref[h:h + 1, :] = jnp.sum(a_ref[rows, :].astype(F32) * b_ref[rows, :], axis=0, keepdims=True)

    return pl.pallas_call(
        body, name=name, grid=(t // tc,),
        in_specs=[pl.BlockSpec((d, tc), lambda i: (0, i)), pl.BlockSpec((d, tc), lambda i: (0, i))],
        out_specs=pl.BlockSpec((hh, tc), lambda i: (0, i)),
        out_shape=jax.ShapeDtypeStruct((hh, t), F32),
        compiler_params=_params(("parallel",)),
    )(a_t, b_t)


def _fox_bwd_t(qa_t, k_aug, k_t, v, do_t, do, lse, dsum, cfg, *, name):
    t, d, hh = cfg.tokens, cfg.d_model, cfg.heads
    tq = _blk(cfg.seq, FOX_BLOCK)
    nq = cfg.seq // tq

    def body(qa_ref, ka_ref, kt_ref, v_ref, dot_ref, do_ref, lse_ref, ds_ref, dq_hbm, dk_ref, dv_ref, dc_ref,
             dq_acc, sem):
        b, ki, qi = pl.program_id(0), pl.program_id(1), pl.program_id(2)
        qq = jnp.maximum(qi, ki)

        @pl.when((ki == 0) & (qi == 0))
        def _():
            dq_acc[...] = jnp.zeros_like(dq_acc)

        @pl.when(qi == 0)
        def _():
            dk_ref[...] = jnp.zeros_like(dk_ref)
            dv_ref[...] = jnp.zeros_like(dv_ref)
            dc_ref[...] = jnp.zeros_like(dc_ref)

        def step(diagonal):
            upper = lax.broadcasted_iota(jnp.int32, (LANES, tq), 0) < HEAD_DIM
            half = _half_mask((tq, LANES))
            for hp in range(hh // 2):
                pair = slice(hp * LANES, (hp + 1) * LANES)
                dvs = []
                for e in range(2):
                    h = 2 * hp + e
                    rows = slice(h * HEAD_DIM, (h + 1) * HEAD_DIM)
                    aug = slice(h * AUG, (h + 1) * AUG)
                    s = jnp.dot(ka_ref[:, aug], qa_ref[aug, :], preferred_element_type=F32)
                    if diagonal:
                        s = jnp.where(_keys_visible(tq), s, NEG)
                    p = jnp.exp(s - lse_ref[h:h + 1, :])
                    dot2 = dot_ref[pair, :]
                    dote = jnp.where(upper == (e == 0), dot2, jnp.zeros_like(dot2))
                    dp = jnp.dot(v_ref[:, pair], dote, preferred_element_type=F32)
                    dsf = p * (dp - ds_ref[h:h + 1, :])
                    dc_ref[:, h:h + 1] -= jnp.sum(dsf, axis=1, keepdims=True)
                    dsc = dsf.astype(BF16)
                    dvs.append(jnp.dot(p.astype(BF16), do_ref[:, pair], preferred_element_type=F32))
                    dk_ref[:, aug] += lax.dot_general(dsc, qa_ref[aug, :], _NT, preferred_element_type=F32)
                    dq_acc[qq, rows, :] += jnp.dot(kt_ref[rows, :], dsc, preferred_element_type=F32)
                dv_ref[:, pair] += jnp.where(half, dvs[0], dvs[1])

        pl.when(qi > ki)(functools.partial(step, False))
        pl.when(qi == ki)(functools.partial(step, True))

        @pl.when((ki == nq - 1) & (qi == nq - 1))
        def _():
            cp = pltpu.make_async_copy(dq_acc, dq_hbm.at[b], sem)
            cp.start()
            cp.wait()

    def qcol(b, ki, qi):
        return (0, b * nq + jnp.maximum(qi, ki))

    def krow(b, ki, qi):
        return (b * nq + ki, 0)

    return pl.pallas_call(
        body, name=name, grid=(cfg.batch, nq, nq),
        in_specs=[pl.BlockSpec((hh * AUG, tq), qcol),
                  pl.BlockSpec((tq, hh * AUG), krow),
                  pl.BlockSpec((d, tq), lambda b, ki, qi: (0, b * nq + ki)),
                  pl.BlockSpec((tq, d), krow),
                  pl.BlockSpec((d, tq), qcol),
                  pl.BlockSpec((tq, d), lambda b, ki, qi: (b * nq + jnp.maximum(qi, ki), 0)),
                  pl.BlockSpec((hh, tq), qcol), pl.BlockSpec((hh, tq), qcol)],
        out_specs=[pl.BlockSpec(memory_space=pl.ANY), pl.BlockSpec((tq, hh * AUG), krow),
                   pl.BlockSpec((tq, d), krow), pl.BlockSpec((tq, LANES), krow)],
        out_shape=[jax.ShapeDtypeStruct((cfg.batch, nq, d, tq), F32), jax.ShapeDtypeStruct((t, hh * AUG), F32),
                   jax.ShapeDtypeStruct((t, d), F32), jax.ShapeDtypeStruct((t, LANES), F32)],
        scratch_shapes=[pltpu.VMEM((nq, d, tq), F32), pltpu.SemaphoreType.DMA],
        compiler_params=_params(("arbitrary", "arbitrary", "arbitrary")),
    )(qa_t, k_aug, k_t, v, do_t, do, lse, dsum)


WIDE = 1536


def _fwd(a, w, *, name, res=None, scale=1.0):
    return _mm(a, w, form="F", out_dtype=F32, name=name, bn=WIDE, bk=WIDE, res=res, scale=scale)


def _bwd(dy, w, *, name, scale=1.0):
    return _mm(dy, w, form="B", out_dtype=F32, name=name, bn=WIDE, bk=WIDE, scale=scale)


def _wgrad(a, dy, w, *, name, scale=1.0):
    return _mm_grad(a, dy, w.shape[0], name=name, bm=WIDE, bn=WIDE, scale=scale)


def _ffn_fwd(h, g, w_in, w_out, tag):
    n = _rms_fwd(h, g, name=f"{tag}_norm")
    u = _fwd(n, w_in, name=f"{tag}_in")
    a = _swiglu_fwd(u, name=f"{tag}_act")
    return _fwd(a, w_out, name=f"{tag}_out", res=h, scale=0.5), (n, u, a)


def _ffn_bwd(dh_out, h, g, w_in, w_out, saved, tag):
    n, u, a = saved
    da = _bwd(dh_out, w_out, name=f"{tag}_out_dx", scale=0.5)
    dw_out = _wgrad(a, dh_out, w_out, name=f"{tag}_out_dw", scale=0.5)
    du = _swiglu_bwd(u, da, name=f"{tag}_act_bwd")
    dn = _bwd(du, w_in, name=f"{tag}_in_dx")
    dw_in = _wgrad(n, du, w_in, name=f"{tag}_in_dw")
    dh, dg = _rms_bwd(h, g, dn, dh_out, name=f"{tag}_norm_bwd")
    return dh, dg, dw_in, dw_out


def _head_gain(g, heads, scale=1.0):
    return jnp.tile(g.astype(F32) * scale, heads)


def _local_step(cfg, x, positions, target, w, s):
    d, hh = cfg.d_model, cfg.heads
    cos, sin = _rope_tables(positions)
    ones = jnp.ones((d,), F32)

    h1, ffn0 = _ffn_fwd(x, s["ffn_norm"][0, 0], w["ffn_w_in"][0][0], w["ffn_w_out"][0][0], "ffn00")
    hn_a = _rms_fwd(h1, s["mix_norm"][0], name="a_norm")
    qkv = _fwd(hn_a, w["a_w_qkv"], name="a_qkv")
    kinds_a = ["rope", "rope", "cast"] * len(DILATIONS)
    gains_a = jnp.stack([z for g in range(len(DILATIONS)) for z in (
        _head_gain(s["a_q_norm"][g], hh, Q_SCALE), _head_gain(s["a_k_norm"][g], hh), ones)])
    qkvp = _hn_fwd(qkv, gains_a, kinds_a, d, cos, sin, name="a_qk_norm")
    band = [_band_fwd(qkvp, g, dil, cfg, name=f"a_band{g}") for g, dil in enumerate(DILATIONS)]
    mixed, lse_a = _mix_fwd([o for o, _ in band], [l for _, l in band], name="a_mix")
    h2 = _fwd(mixed, w["a_w_o"], name="a_out", res=h1)
    h3, ffn1 = _ffn_fwd(h2, s["ffn_norm"][0, 1], w["ffn_w_in"][0][1], w["ffn_w_out"][0][1], "ffn01")

    kn = _rms_fwd(h3, s["kv_norm"], name="kv_norm")
    proj = _fwd(kn, w["kv_w"], name="kv_proj")
    kinds_kv = ["norm", "cast"]
    gains_kv = jnp.stack([_head_gain(s["kv_k_norm"], hh), ones])
    kvp = _hn_fwd(proj, gains_kv, kinds_kv, d, cos, sin, name="kv_k_norm")
    gate_col = 2 * d // LANES
    bias = jnp.pad(s["kv_b_f"].astype(F32), (0, LANES - hh))
    cum = _gate_fwd(proj, gate_col, bias, cfg, name="kv_gate")
    k_b, v_b = kvp[:, :d], kvp[:, d:]
    k_aug = _fox_aug_k(k_b, cum[:, :hh], cfg)

    h4, ffn2 = _ffn_fwd(h3, s["ffn_norm"][1, 0], w["ffn_w_in"][1][0], w["ffn_w_out"][1][0], "ffn10")
    hn_b = _rms_fwd(h4, s["mix_norm"][1], name="b_norm")
    qraw = _fwd(hn_b, w["b_w_q"], name="b_q")
    gains_b = _head_gain(s["b_q_norm"][0], hh, Q_SCALE)[None]
    qp = _hn_fwd(qraw, gains_b, ["norm"], d, cos, sin, name="b_q_norm")
    qa_t = _fox_aug_q(qp, cfg)
    o_t, lse_b = _fox_fwd_t(qa_t, k_aug, v_b.T, cfg, name="b_fox")
    o_b = o_t.T
    h5 = _fwd(o_b, w["b_w_o"], name="b_out", res=h4)
    h6, ffn3 = _ffn_fwd(h5, s["ffn_norm"][1, 1], w["ffn_w_in"][1][1], w["ffn_w_out"][1][1], "ffn11")

    loss, dh6 = _loss_fwd_bwd(h6, target, name="loss")

    dh5, dg11, dwi11, dwo11 = _ffn_bwd(dh6, h5, s["ffn_norm"][1, 1], w["ffn_w_in"][1][1], w["ffn_w_out"][1][1],
                                       ffn3, "ffn11")
    do_b = _bwd(dh5, w["b_w_o"], name="b_out_dx")
    dw_bo = _wgrad(o_b, dh5, w["b_w_o"], name="b_out_dw")
    do_bf = do_b.astype(BF16)
    do_t = do_bf.T
    dsum_b = _head_dot_t(do_t, o_t, cfg, name="b_dsum")
    dq4, dk_aug, dv_b, dcum = _fox_bwd_t(qa_t, k_aug, k_b.T, v_b, do_t, do_bf, lse_b, dsum_b, cfg,
                                         name="b_fox_bwd")
    dq_b = dq4.transpose(0, 1, 3, 2).reshape(cfg.tokens, d)
    dk_b = dk_aug.reshape(cfg.tokens, hh, AUG)[:, :, :HEAD_DIM].reshape(cfg.tokens, d)
    dqraw, dgq = _hn_bwd(qraw, dq_b, gains_b, ["norm"], d, cos, sin, name="b_q_norm_bwd")
    dhn_b = _bwd(dqraw, w["b_w_q"], name="b_q_dx")
    dw_bq = _wgrad(hn_b, dqraw, w["b_w_q"], name="b_q_dw")
    dh4, dmix1 = _rms_bwd(h4, s["mix_norm"][1], dhn_b, dh5, name="b_norm_bwd")
    dh3, dg10, dwi10, dwo10 = _ffn_bwd(dh4, h3, s["ffn_norm"][1, 0], w["ffn_w_in"][1][0], w["ffn_w_out"][1][0],
                                       ffn2, "ffn10")

    dkvraw, dgk = _hn_bwd(proj, jnp.concatenate([dk_b, dv_b], axis=1), gains_kv, kinds_kv, d, cos, sin,
                          name="kv_k_norm_bwd")
    dz, dbias = _gate_bwd(proj, gate_col, bias, dcum, cfg, name="kv_gate_bwd")
    pad_cols = w["kv_w"].shape[2] - 2 * d - LANES
    dproj = jnp.concatenate([dkvraw, dz.astype(BF16), jnp.zeros((cfg.tokens, pad_cols), BF16)], axis=1)
    dkn = _bwd(dproj, w["kv_w"], name="kv_proj_dx")
    dw_kv = _wgrad(kn, dproj, w["kv_w"], name="kv_proj_dw")
    dh3, dkvn = _rms_bwd(h3, s["kv_norm"], dkn, dh3, name="kv_norm_bwd")

    dh2, dg01, dwi01, dwo01 = _ffn_bwd(dh3, h2, s["ffn_norm"][0, 1], w["ffn_w_in"][0][1], w["ffn_w_out"][0][1],
                                       ffn1, "ffn01")
    dmixed = _bwd(dh2, w["a_w_o"], name="a_out_dx")
    dw_ao = _wgrad(mixed, dh2, w["a_w_o"], name="a_out_dw")
    dsum_a = _head_dot(dmixed, mixed, name="a_dsum")
    dqkvp = []
    for g, dil in enumerate(DILATIONS):
        dqkvp += _band_bwd(qkvp, dmixed, lse_a, dsum_a, g, dil, cfg, name=f"a_band{g}_bwd")
    dqkv, dga = _hn_bwd(qkv, jnp.concatenate(dqkvp, axis=1), gains_a, kinds_a, d, cos, sin, name="a_qk_norm_bwd")
    dhn_a = _bwd(dqkv, w["a_w_qkv"], name="a_qkv_dx")
    dw_qkv = _wgrad(hn_a, dqkv, w["a_w_qkv"], name="a_qkv_dw")
    dh1, dmix0 = _rms_bwd(h1, s["mix_norm"][0], dhn_a, dh2, name="a_norm_bwd")
    dx, dg00, dwi00, dwo00 = _ffn_bwd(dh1, x, s["ffn_norm"][0, 0], w["ffn_w_in"][0][0], w["ffn_w_out"][0][0],
                                      ffn0, "ffn00")

    dw = {
        "ffn_w_in": [[dwi00, dwi01], [dwi10, dwi11]],
        "ffn_w_out": [[dwo00, dwo01], [dwo10, dwo11]],
        "a_w_qkv": dw_qkv, "a_w_o": dw_ao, "kv_w": dw_kv, "b_w_q": dw_bq, "b_w_o": dw_bo,
    }
    ds = {
        "ffn_norm": jnp.stack([jnp.stack([dg00, dg01]), jnp.stack([dg10, dg11])]),
        "mix_norm": jnp.stack([dmix0, dmix1]),
        "a_q_norm": jnp.stack([dga[3 * g] for g in range(len(DILATIONS))])[None] * Q_SCALE,
        "a_k_norm": jnp.stack([dga[3 * g + 1] for g in range(len(DILATIONS))])[None],
        "kv_norm": dkvn,
        "kv_b_f": dbias[:hh],
        "kv_k_norm": dgk[0],
        "b_q_norm": dgq * Q_SCALE,
    }
    return loss, dx, dw, ds


MESH_ID = pl.DeviceIdType.MESH
ANY = pl.BlockSpec(memory_space=pl.ANY)
PACK_COLS = 1024
PACK_ROW_ALIGN = 32


def _me():
    return lax.axis_index("x"), lax.axis_index("y"), lax.axis_index("c")


def _other_chips(x, y):
    return [(1 - x, y), (x, 1 - y), (1 - x, 1 - y)]


def _all_gather_small(v, *, name):
    r = v.shape[0]

    def body(v_ref, out_ref, send_sems, recv_sems):
        x, y, c = _me()
        me = 4 * x + 2 * y + c
        out_ref[me] = v_ref[...]
        copies = []
        for k in range(1, N_DEV):
            fx, fy, fc = (k >> 2) & 1, (k >> 1) & 1, k & 1
            peer = (1 - x if fx else x, 1 - y if fy else y, 1 - c if fc else c)
            copies.append(pltpu.make_async_remote_copy(
                src_ref=v_ref, dst_ref=out_ref.at[me], send_sem=send_sems.at[k - 1], recv_sem=recv_sems.at[k - 1],
                device_id=peer, device_id_type=MESH_ID))
        for cp in copies:
            cp.start()
        for cp in copies:
            cp.wait()

    return pl.pallas_call(
        body, name=name,
        in_specs=[pl.BlockSpec(memory_space=pltpu.VMEM)], out_specs=pl.BlockSpec(memory_space=pltpu.VMEM),
        out_shape=jax.ShapeDtypeStruct((N_DEV, r, LANES), v.dtype),
        scratch_shapes=[pltpu.SemaphoreType.DMA((N_DEV - 1,)), pltpu.SemaphoreType.DMA((N_DEV - 1,))],
    )(v)


def _all_gather_chips(v, *, name):
    def body(v_ref, out_ref, send_sems, recv_sems, local_sem):
        x, y, c = _me()
        j = 2 * x + y
        local = pltpu.make_async_copy(v_ref, out_ref.at[j], local_sem)
        local.start()
        copies = [pltpu.make_async_remote_copy(
            src_ref=v_ref, dst_ref=out_ref.at[j], send_sem=send_sems.at[k], recv_sem=recv_sems.at[k],
            device_id=(px, py, c), device_id_type=MESH_ID) for k, (px, py) in enumerate(_other_chips(x, y))]
        for cp in copies:
            cp.start()
        for cp in copies:
            cp.wait()
        local.wait()

    return pl.pallas_call(
        body, name=name, in_specs=[ANY], out_specs=ANY,
        out_shape=jax.ShapeDtypeStruct((N_CHIPS,) + v.shape, v.dtype),
        scratch_shapes=[pltpu.SemaphoreType.DMA((N_CHIPS - 1,)), pltpu.SemaphoreType.DMA((N_CHIPS - 1,)),
                        pltpu.SemaphoreType.DMA],
    )(v)


def _swap_halves(g, *, name):
    n, r, cols = g.shape
    rh = r // 2

    def body(g_ref, out_ref, send_sem, recv_sem):
        x, y, c = _me()
        cp = pltpu.make_async_remote_copy(
            src_ref=g_ref.at[:, pl.ds((1 - c) * rh, rh)], dst_ref=out_ref, send_sem=send_sem, recv_sem=recv_sem,
            device_id=(x, y, 1 - c), device_id_type=MESH_ID)
        cp.start()
        cp.wait()

    return pl.pallas_call(
        body, name=name, in_specs=[ANY], out_specs=ANY,
        out_shape=jax.ShapeDtypeStruct((n, rh, cols), g.dtype),
        scratch_shapes=[pltpu.SemaphoreType.DMA, pltpu.SemaphoreType.DMA],
    )(g)


def _scatter_chips(v, *, name):
    def body(v_ref, out_ref, send_sems, recv_sems, local_sem):
        x, y, c = _me()
        j = 2 * x + y
        local = pltpu.make_async_copy(v_ref.at[j], out_ref.at[j], local_sem)
        local.start()
        copies = [pltpu.make_async_remote_copy(
            src_ref=v_ref.at[2 * px + py], dst_ref=out_ref.at[j], send_sem=send_sems.at[k], recv_sem=recv_sems.at[k],
            device_id=(px, py, c), device_id_type=MESH_ID) for k, (px, py) in enumerate(_other_chips(x, y))]
        for cp in copies:
            cp.start()
        for cp in copies:
            cp.wait()
        local.wait()

    return pl.pallas_call(
        body, name=name, in_specs=[ANY], out_specs=ANY,
        out_shape=jax.ShapeDtypeStruct(v.shape, v.dtype),
        scratch_shapes=[pltpu.SemaphoreType.DMA((N_CHIPS - 1,)), pltpu.SemaphoreType.DMA((N_CHIPS - 1,)),
                        pltpu.SemaphoreType.DMA],
    )(v)


def _join_halves(v, *, name):
    def body(v_ref, out_ref, send_sem, recv_sem, local_sem):
        x, y, c = _me()
        local = pltpu.make_async_copy(v_ref, out_ref.at[c], local_sem)
        local.start()
        cp = pltpu.make_async_remote_copy(
            src_ref=v_ref, dst_ref=out_ref.at[c], send_sem=send_sem, recv_sem=recv_sem,
            device_id=(x, y, 1 - c), device_id_type=MESH_ID)
        cp.start()
        cp.wait()
        local.wait()

    return pl.pallas_call(
        body, name=name, in_specs=[ANY], out_specs=ANY,
        out_shape=jax.ShapeDtypeStruct((2,) + v.shape, v.dtype),
        scratch_shapes=[pltpu.SemaphoreType.DMA, pltpu.SemaphoreType.DMA, pltpu.SemaphoreType.DMA],
    )(v)


def _row_blk(rows, want):
    for b in range(min(rows, want) // SUBLANES * SUBLANES, 0, -SUBLANES):
        if rows % b == 0:
            return b
    return rows


def _add_own_half(g, got, *, name):
    n, r, cols = g.shape
    rh = r // 2
    tr = _row_blk(rh, 512)
    nb = rh // tr

    def body(c_ref, g_ref, got_ref, o_ref):
        del c_ref
        o_ref[...] = (g_ref[...] + got_ref[...]).astype(BF16)

    grid_spec = pltpu.PrefetchScalarGridSpec(
        num_scalar_prefetch=1, grid=(n, nb),
        in_specs=[pl.BlockSpec((None, tr, cols), lambda j, i, c: (j, c[0] * nb + i, 0)),
                  pl.BlockSpec((None, tr, cols), lambda j, i, c: (j, i, 0))],
        out_specs=pl.BlockSpec((None, tr, cols), lambda j, i, c: (j, i, 0)))
    return pl.pallas_call(
        body, name=name, grid_spec=grid_spec, out_shape=jax.ShapeDtypeStruct((n, rh, cols), BF16),
        compiler_params=_params(("parallel", "parallel")),
    )(lax.axis_index("c").astype(jnp.int32).reshape(1), g, got)


def _sum_parts(parts, *, name):
    n, r, cols = parts.shape
    tr = _row_blk(r, 512)

    def body(*refs):
        o_ref = refs[n]
        acc = refs[0][...].astype(F32)
        for p_ref in refs[1:n]:
            acc = acc + p_ref[...].astype(F32)
        o_ref[...] = acc

    return pl.pallas_call(
        body, name=name, grid=(r // tr,),
        in_specs=[pl.BlockSpec((None, tr, cols), functools.partial(lambda j, i: (j, i, 0), j)) for j in range(n)],
        out_specs=pl.BlockSpec((tr, cols), lambda i: (i, 0)),
        out_shape=jax.ShapeDtypeStruct((r, cols), F32),
        compiler_params=_params(("parallel",)),
    )(*([parts] * n))


def _adamw(w, m, v, g, *, name):
    shape = w.shape
    cols = shape[-1]
    w2, m2, v2, g2 = (z.reshape(-1, cols) for z in (w, m, v, g))
    rows = w2.shape[0]
    tr = _row_blk(rows, max(SUBLANES, (1 << 20) // (4 * cols)))

    def body(w_ref, m_ref, v_ref, g_ref, d_ref, nm_ref, nv_ref):
        gv = g_ref[...]
        nm = ADAM_B1 * m_ref[...] + (1.0 - ADAM_B1) * gv
        nv = ADAM_B2 * v_ref[...] + (1.0 - ADAM_B2) * jnp.square(gv)
        m_hat = nm / (1.0 - ADAM_B1 ** ADAM_STEP)
        v_hat = nv / (1.0 - ADAM_B2 ** ADAM_STEP)
        d_ref[...] = -ADAM_LR * (m_hat / (jnp.sqrt(v_hat) + ADAM_EPS) + ADAM_WD * w_ref[...])
        nm_ref[...] = nm
        nv_ref[...] = nv

    spec = pl.BlockSpec((tr, cols), lambda i: (i, 0))
    out = jax.ShapeDtypeStruct((rows, cols), F32)
    d, nm, nv = pl.pallas_call(
        body, name=name, grid=(rows // tr,), in_specs=[spec] * 4, out_specs=[spec] * 3, out_shape=[out] * 3,
        compiler_params=_params(("parallel",)),
    )(w2, m2, v2, g2)
    return d.reshape(shape), nm.reshape(shape), nv.reshape(shape)


def _pack_rows(size, cols, align):
    return -(-size // (cols * align)) * align


def _pack(arrs, lead, cols, align, total_align):
    lead_shape = arrs[0].shape[:lead]
    parts = []
    for a in arrs:
        flat = a.reshape(lead_shape + (-1,))
        size = flat.shape[-1]
        rows = _pack_rows(size, cols, align)
        flat = jnp.pad(flat, [(0, 0)] * lead + [(0, rows * cols - size)])
        parts.append(flat.reshape(lead_shape + (rows, cols)))
    total = sum(p.shape[lead] for p in parts)
    extra = -total % total_align
    if extra:
        parts.append(jnp.zeros(lead_shape + (extra, cols), parts[0].dtype))
    return jnp.concatenate(parts, axis=lead)


def _unpack(buf, shapes, lead, cols, align):
    lead_shape = buf.shape[:lead]
    out, row = [], 0
    for shp in shapes:
        size = 1
        for n in shp:
            size *= n
        rows = _pack_rows(size, cols, align)
        piece = lax.slice_in_dim(buf, row, row + rows, axis=lead).reshape(lead_shape + (-1,))
        out.append(piece[..., :size].reshape(lead_shape + tuple(shp)))
        row += rows
    return out


BIG = ("ffn_w_in", "ffn_w_out", "a_w_qkv", "a_w_o", "kv_w", "b_w_q", "b_w_o")
SMALL = ("ffn_norm", "mix_norm", "a_q_norm", "a_k_norm", "kv_norm", "kv_b_f", "kv_k_norm", "b_q_norm")
WEIGHTS = ("ffn_norm", "ffn_w_in", "ffn_w_out", "mix_norm", "a_w_qkv", "a_q_norm", "a_k_norm", "a_w_o",
           "kv_norm", "kv_w", "kv_b_f", "kv_k_norm", "b_w_q", "b_q_norm", "b_w_o")
GATE_PAD = 2 * LANES


def _stack_weights(sh, d):
    depth = sh["ffn_w_in"].shape[1]
    kv = sh["kv_w"].transpose(1, 0, 2).reshape(d, -1)
    kv = jnp.pad(kv, ((0, 0), (0, 2 * d + GATE_PAD - kv.shape[1])))
    return {
        "ffn_w_in": [[sh["ffn_w_in"][:, l, i] for i in range(2)] for l in range(depth)],
        "ffn_w_out": [[sh["ffn_w_out"][:, l, i].reshape(1, -1, d) for i in range(2)] for l in range(depth)],
        "a_w_qkv": sh["a_w_qkv"][:, 0],
        "a_w_o": sh["a_w_o"].reshape(1, d, d),
        "kv_w": kv[None],
        "b_w_q": sh["b_w_q"].reshape(1, d, d),
        "b_w_o": sh["b_w_o"].reshape(1, d, d),
    }


def _unstack_grads(dw, d, heads):
    def rows4(z):
        return z.reshape(N_CHIPS, -1, d)

    kv_cols = 2 * d + heads
    kv = dw["kv_w"][0][:, :kv_cols].reshape(d, N_CHIPS, kv_cols // N_CHIPS).transpose(1, 0, 2)
    return [
        jnp.stack([jnp.stack(row, axis=1) for row in dw["ffn_w_in"]], axis=1),
        jnp.stack([jnp.stack([rows4(z) for z in row], axis=1) for row in dw["ffn_w_out"]], axis=1),
        dw["a_w_qkv"][:, None],
        rows4(dw["a_w_o"])[:, None],
        kv,
        rows4(dw["b_w_q"])[:, None],
        rows4(dw["b_w_o"])[:, None],
    ]


def kernel(x, positions, ffn_norm, ffn_w_in, ffn_w_out, mix_norm, a_w_qkv, a_q_norm, a_k_norm, a_w_o, kv_norm, kv_w, kv_b_f, kv_k_norm, b_w_q, b_q_norm, b_w_o, loss_target, m_ffn_norm, m_ffn_w_in, m_ffn_w_out, m_mix_norm, m_a_w_qkv, m_a_q_norm, m_a_k_norm, m_a_w_o, m_kv_norm, m_kv_w, m_kv_b_f, m_kv_k_norm, m_b_w_q, m_b_q_norm, m_b_w_o, v_ffn_norm, v_ffn_w_in, v_ffn_w_out, v_mix_norm, v_a_w_qkv, v_a_q_norm, v_a_k_norm, v_a_w_o, v_kv_norm, v_kv_w, v_kv_b_f, v_kv_k_norm, v_b_w_q, v_b_q_norm, v_b_w_o):
    wts = dict(zip(WEIGHTS, (ffn_norm, ffn_w_in, ffn_w_out, mix_norm, a_w_qkv, a_q_norm, a_k_norm, a_w_o, kv_norm,
                             kv_w, kv_b_f, kv_k_norm, b_w_q, b_q_norm, b_w_o)))
    mom = dict(zip(WEIGHTS, (m_ffn_norm, m_ffn_w_in, m_ffn_w_out, m_mix_norm, m_a_w_qkv, m_a_q_norm, m_a_k_norm,
                             m_a_w_o, m_kv_norm, m_kv_w, m_kv_b_f, m_kv_k_norm, m_b_w_q, m_b_q_norm, m_b_w_o)))
    var = dict(zip(WEIGHTS, (v_ffn_norm, v_ffn_w_in, v_ffn_w_out, v_mix_norm, v_a_w_qkv, v_a_q_norm, v_a_k_norm,
                             v_a_w_o, v_kv_norm, v_kv_w, v_kv_b_f, v_kv_k_norm, v_b_w_q, v_b_q_norm, v_b_w_o)))
    batch, seq, d = x.shape
    cfg = Cfg(d_model=d, d_ff=ffn_w_out.shape[2] * N_CHIPS, seq=seq, batch=batch)
    chip = 2 * lax.axis_index("x") + lax.axis_index("y")
    big_shapes = [wts[n].shape for n in BIG]

    shard = _pack([wts[n].astype(BF16) for n in BIG], 0, PACK_COLS, PACK_ROW_ALIGN, PACK_COLS)
    gathered = _all_gather_chips(shard, name="gather_weights")
    w = _stack_weights(dict(zip(BIG, _unpack(gathered, big_shapes, 1, PACK_COLS, PACK_ROW_ALIGN))), d)
    norm_shard = _pack([ffn_norm], 0, LANES, SUBLANES, SUBLANES)
    norms = _all_gather_small(norm_shard, name="gather_ffn_norm")[0::2]
    (norms,) = _unpack(norms, [ffn_norm.shape], 1, LANES, SUBLANES)
    small = {"ffn_norm": jnp.moveaxis(norms, 0, 2).reshape(ffn_norm.shape[:2] + (d,)),
             "mix_norm": mix_norm, "a_q_norm": a_q_norm[0], "a_k_norm": a_k_norm[0], "kv_norm": kv_norm,
             "kv_b_f": kv_b_f, "kv_k_norm": kv_k_norm, "b_q_norm": b_q_norm}

    loss, dx, dw, ds = _local_step(cfg, x.reshape(cfg.tokens, d), positions.reshape(cfg.tokens),
                                   loss_target.reshape(cfg.tokens, d), w, small)
    loss = lax.psum(loss, ("x", "y", "c"))

    g = _pack(_unstack_grads(dw, d, cfg.heads), 1, PACK_COLS, PACK_ROW_ALIGN, PACK_COLS)
    chip_half = _add_own_half(g, _swap_halves(g, name="swap_halves"), name="add_halves")
    mine = _sum_parts(_scatter_chips(chip_half, name="scatter_chips"), name="sum_chips")
    g_big = _join_halves(mine, name="join_halves").reshape(g.shape[1:])
    grads = dict(zip(BIG, _unpack(g_big, big_shapes, 0, PACK_COLS, PACK_ROW_ALIGN)))

    small_shapes = [ds[n].shape for n in SMALL]
    parts = _all_gather_small(_pack([ds[n] for n in SMALL], 0, LANES, SUBLANES, SUBLANES), name="gather_small")
    g_small = dict(zip(SMALL, _unpack(_sum_parts(parts, name="sum_small"), small_shapes, 0, LANES, SUBLANES)))
    quarter = d // N_CHIPS
    g_small["ffn_norm"] = lax.dynamic_slice_in_dim(g_small["ffn_norm"], chip * quarter, quarter, axis=2)
    grads.update(g_small)

    delta, new_m, new_v = {}, {}, {}
    for n in BIG:
        delta[n], new_m[n], new_v[n] = _adamw(wts[n], mom[n], var[n], grads[n], name=f"adamw_{n}")
    packed = [_pack([z[n] for n in SMALL], 0, LANES, SUBLANES, SUBLANES) for z in (wts, mom, var, grads)]
    small_out = _adamw(*packed, name="adamw_small")
    shard_shapes = [wts[n].shape for n in SMALL]
    for out, res in zip((delta, new_m, new_v), small_out):
        out.update(zip(SMALL, _unpack(res, shard_shapes, 0, LANES, SUBLANES)))

    return (loss, dx.reshape(x.shape), *[grads[n] for n in WEIGHTS], *[delta[n] for n in WEIGHTS],
            *[new_m[n] for n in WEIGHTS], *[new_v[n] for n in WEIGHTS])
```

```python
import functools
from typing import NamedTuple

import jax
import jax.numpy as jnp
from jax import lax
from jax.experimental import pallas as pl
from jax.experimental.pallas import tpu as pltpu

F32 = jnp.float32
BF16 = jnp.bfloat16

HEAD_DIM = 64
LANES = 128
SUBLANES = 8
ROT_DIM = HEAD_DIM // 4
ROPE_THETA = 500000.0
NORM_EPS = 1e-6
BAND = 128
DILATIONS = (1, 4, 16)
NEG = -1e30
Q_SCALE = HEAD_DIM ** -0.5
N_CHIPS = 4
N_DEV = 8
VMEM_LIMIT = 48 * 1024 * 1024

ADAM_LR = 0.001
ADAM_B1 = 0.9
ADAM_B2 = 0.999
ADAM_EPS = 1e-08
ADAM_WD = 0.01
ADAM_STEP = 10


class Cfg(NamedTuple):
    d_model: int
    d_ff: int
    seq: int
    batch: int

    @property
    def heads(self):
        return self.d_model // HEAD_DIM

    @property
    def tokens(self):
        return self.batch * self.seq

    @property
    def pairs(self):
        return self.d_model // LANES


def _params(sem):
    return pltpu.CompilerParams(dimension_semantics=sem, vmem_limit_bytes=VMEM_LIMIT)


def _blk(dim, want):
    if dim <= want:
        return dim
    for b in range(want // LANES * LANES, 0, -LANES):
        if dim % b == 0:
            return b
    b = want
    while dim % b:
        b //= 2
    return b


def _mm(a, b, *, form, out_dtype, name, bm=1024, bn=1024, bk=1024, res=None, scale=1.0):
    if form == "F":
        m, kdim = a.shape
        jn, _, ns = b.shape
        bm, bn, bk = _blk(m, bm), _blk(ns, bn), _blk(kdim, bk)
        npj = ns // bn
        grid = (m // bm, jn * npj, kdim // bk)
        a_spec = pl.BlockSpec((bm, bk), lambda i, n, k: (i, k))
        b_spec = pl.BlockSpec((None, bk, bn), lambda i, n, k: (n // npj, k, n % npj))
        o_spec = pl.BlockSpec((bm, bn), lambda i, n, k: (i, n))
        o_shape = jax.ShapeDtypeStruct((m, jn * ns), out_dtype)
        dims = (((1,), (0,)), ((), ()))
    elif form == "B":
        m = a.shape[0]
        jn, kdim, ns = b.shape
        bm, bn, bk = _blk(m, bm), _blk(kdim, bn), _blk(ns, bk)
        kpj = ns // bk
        grid = (m // bm, kdim // bn, jn * kpj)
        a_spec = pl.BlockSpec((bm, bk), lambda i, n, k: (i, k))
        b_spec = pl.BlockSpec((None, bn, bk), lambda i, n, k: (k // kpj, n, k % kpj))
        o_spec = pl.BlockSpec((bm, bn), lambda i, n, k: (i, n))
        o_shape = jax.ShapeDtypeStruct((m, kdim), out_dtype)
        dims = (((1,), (1,)), ((), ()))
    else:
        raise ValueError(form)
    nk = grid[2]

    def body(*refs):
        if res is None:
            a_ref, b_ref, o_ref, acc_ref = refs
            r_ref = None
        else:
            a_ref, b_ref, r_ref, o_ref, acc_ref = refs
        k = pl.program_id(2)

        @pl.when(k == 0)
        def _():
            acc_ref[...] = jnp.zeros_like(acc_ref)

        acc_ref[...] += lax.dot_general(a_ref[...].astype(BF16), b_ref[...].astype(BF16), dims,
                                        preferred_element_type=F32)

        @pl.when(k == nk - 1)
        def _():
            r = acc_ref[...]
            if scale != 1.0:
                r = r * scale
            if r_ref is not None:
                r = r_ref[...] + r
            o_ref[...] = r.astype(o_ref.dtype)

    in_specs = [a_spec, b_spec]
    args = [a, b]
    if res is not None:
        in_specs.append(pl.BlockSpec((bm, bn), lambda i, n, k: (i, n)))
        args.append(res)
    return pl.pallas_call(
        body, name=name, grid=grid, in_specs=in_specs, out_specs=o_spec, out_shape=o_shape,
        scratch_shapes=[pltpu.VMEM((bm, bn), F32)],
        compiler_params=_params(("parallel", "parallel", "arbitrary")),
    )(*args)


def _mm_grad(a, dy, jn, *, name, scale=1.0, bm=1024, bn=1024, bk=1024):
    t, kdim = a.shape
    ns = dy.shape[1] // jn
    bm, bn, bk = _blk(kdim, bm), _blk(ns, bn), _blk(t, bk)
    npj = ns // bn
    grid = (kdim // bm, jn * npj, t // bk)
    nk = grid[2]
    dims = (((0,), (0,)), ((), ()))

    def body(a_ref, b_ref, o_ref, acc_ref):
        k = pl.program_id(2)

        @pl.when(k == 0)
        def _():
            acc_ref[...] = jnp.zeros_like(acc_ref)

        acc_ref[...] += lax.dot_general(a_ref[...].astype(BF16), b_ref[...].astype(BF16), dims,
                                        preferred_element_type=F32)

        @pl.when(k == nk - 1)
        def _():
            r = acc_ref[...]
            if scale != 1.0:
                r = r * scale
            o_ref[...] = r

    return pl.pallas_call(
        body, name=name, grid=grid,
        in_specs=[pl.BlockSpec((bk, bm), lambda m, n, k: (k, m)),
                  pl.BlockSpec((bk, bn), lambda m, n, k: (k, n))],
        out_specs=pl.BlockSpec((None, bm, bn), lambda m, n, k: (n // npj, m, n % npj)),
        out_shape=jax.ShapeDtypeStruct((jn, kdim, ns), F32),
        scratch_shapes=[pltpu.VMEM((bm, bn), F32)],
        compiler_params=_params(("parallel", "parallel", "arbitrary")),
    )(a, dy)


ROW_BLOCK = 512


def _fold8(x):
    return jnp.sum(x.reshape(x.shape[0] // SUBLANES, SUBLANES, x.shape[1]), axis=0)


def _rms_fwd(x, g, *, name):
    t, d = x.shape
    tr = _blk(t, ROW_BLOCK)

    def body(x_ref, g_ref, o_ref):
        xv = x_ref[...]
        rstd = lax.rsqrt(jnp.mean(xv * xv, axis=-1, keepdims=True) + NORM_EPS)
        o_ref[...] = ((xv * rstd) * g_ref[...]).astype(BF16)

    return pl.pallas_call(
        body, name=name, grid=(t // tr,),
        in_specs=[pl.BlockSpec((tr, d), lambda i: (i, 0)), pl.BlockSpec((1, d), lambda i: (0, 0))],
        out_specs=pl.BlockSpec((tr, d), lambda i: (i, 0)),
        out_shape=jax.ShapeDtypeStruct((t, d), BF16),
        compiler_params=_params(("parallel",)),
    )(x, g.reshape(1, d))


def _rms_bwd(x, g, dy, dres, *, name):
    t, d = x.shape
    tr = _blk(t, ROW_BLOCK)

    def body(x_ref, g_ref, dy_ref, dres_ref, dx_ref, dg_ref):
        i = pl.program_id(0)
        xv = x_ref[...]
        rstd = lax.rsqrt(jnp.mean(xv * xv, axis=-1, keepdims=True) + NORM_EPS)
        xhat = xv * rstd
        dyv = dy_ref[...]
        dyg = dyv * g_ref[...]
        proj = jnp.mean(dyg * xhat, axis=-1, keepdims=True)
        dx_ref[...] = dres_ref[...] + rstd * (dyg - xhat * proj)

        @pl.when(i == 0)
        def _():
            dg_ref[...] = jnp.zeros_like(dg_ref)

        dg_ref[...] += _fold8(dyv * xhat)

    dx, dg = pl.pallas_call(
        body, name=name, grid=(t // tr,),
        in_specs=[pl.BlockSpec((tr, d), lambda i: (i, 0)), pl.BlockSpec((1, d), lambda i: (0, 0)),
                  pl.BlockSpec((tr, d), lambda i: (i, 0)), pl.BlockSpec((tr, d), lambda i: (i, 0))],
        out_specs=[pl.BlockSpec((tr, d), lambda i: (i, 0)), pl.BlockSpec((SUBLANES, d), lambda i: (0, 0))],
        out_shape=[jax.ShapeDtypeStruct((t, d), F32), jax.ShapeDtypeStruct((SUBLANES, d), F32)],
        compiler_params=_params(("arbitrary",)),
    )(x, g.reshape(1, d), dy, dres)
    return dx, jnp.sum(dg, axis=0)


def _swiglu_fwd(u, *, name):
    t, f2 = u.shape
    f = f2 // 2
    tr = _blk(t, 256)

    def body(g_ref, u_ref, o_ref):
        gv = g_ref[...]
        o_ref[...] = (gv * jax.nn.sigmoid(gv) * u_ref[...]).astype(BF16)

    return pl.pallas_call(
        body, name=name, grid=(t // tr,),
        in_specs=[pl.BlockSpec((tr, f), lambda i: (i, 0)), pl.BlockSpec((tr, f), lambda i: (i, 1))],
        out_specs=pl.BlockSpec((tr, f), lambda i: (i, 0)),
        out_shape=jax.ShapeDtypeStruct((t, f), BF16),
        compiler_params=_params(("parallel",)),
    )(u, u)


def _swiglu_bwd(u, da, *, name):
    t, f2 = u.shape
    f = f2 // 2
    tr = _blk(t, 256)

    def body(g_ref, u_ref, da_ref, o_ref):
        gv = g_ref[...]
        sg = jax.nn.sigmoid(gv)
        silu = gv * sg
        dav = da_ref[...]
        o_ref[:, :f] = (dav * u_ref[...] * (sg + silu * (1.0 - sg))).astype(BF16)
        o_ref[:, f:] = (dav * silu).astype(BF16)

    return pl.pallas_call(
        body, name=name, grid=(t // tr,),
        in_specs=[pl.BlockSpec((tr, f), lambda i: (i, 0)), pl.BlockSpec((tr, f), lambda i: (i, 1)),
                  pl.BlockSpec((tr, f), lambda i: (i, 0))],
        out_specs=pl.BlockSpec((tr, f2), lambda i: (i, 0)),
        out_shape=jax.ShapeDtypeStruct((t, f2), BF16),
        compiler_params=_params(("parallel",)),
    )(u, u, da)


def _loss_fwd_bwd(h, target, *, name):
    t, d = h.shape
    tr = _blk(t, ROW_BLOCK)

    def body(h_ref, t_ref, dh_ref, l_ref):
        i = pl.program_id(0)
        err = h_ref[...] - t_ref[...]
        dh_ref[...] = err * (1.0 / d)

        @pl.when(i == 0)
        def _():
            l_ref[...] = jnp.zeros_like(l_ref)

        l_ref[...] += _fold8(err * err)

    dh, part = pl.pallas_call(
        body, name=name, grid=(t // tr,),
        in_specs=[pl.BlockSpec((tr, d), lambda i: (i, 0)), pl.BlockSpec((tr, d), lambda i: (i, 0))],
        out_specs=[pl.BlockSpec((tr, d), lambda i: (i, 0)), pl.BlockSpec((SUBLANES, d), lambda i: (0, 0))],
        out_shape=[jax.ShapeDtypeStruct((t, d), F32), jax.ShapeDtypeStruct((SUBLANES, d), F32)],
        compiler_params=_params(("arbitrary",)),
    )(h, target)
    return jnp.sum(part) * (0.5 / d), dh


def _seg_matrix():
    r = lax.broadcasted_iota(jnp.int32, (LANES, LANES), 0) // HEAD_DIM
    c = lax.broadcasted_iota(jnp.int32, (LANES, LANES), 1) // HEAD_DIM
    return (r == c).astype(BF16)


def _head_sum(x, seg):
    hi = x.astype(BF16)
    r1 = x - hi.astype(F32)
    mid = r1.astype(BF16)
    lo = (r1 - mid.astype(F32)).astype(BF16)
    dot = functools.partial(jnp.dot, preferred_element_type=F32)
    return dot(hi, seg) + dot(mid, seg) + dot(lo, seg)


def _lane_in_head(shape):
    return lax.broadcasted_iota(jnp.int32, shape, 1) % HEAD_DIM


def _rot_partner(x):
    up = pltpu.roll(x, LANES - ROT_DIM // 2, 1)
    down = pltpu.roll(x, ROT_DIM // 2, 1)
    return jnp.where(_lane_in_head(x.shape) < ROT_DIM // 2, up, down)


def _rope_tables(positions):
    inv_freq = ROPE_THETA ** (-jnp.arange(0, ROT_DIM, 2, dtype=F32) / ROT_DIM)
    ang = positions.astype(F32)[:, None] * inv_freq
    t = ang.shape[0]
    rest = HEAD_DIM - ROT_DIM
    cos = jnp.concatenate([jnp.cos(ang), jnp.cos(ang), jnp.ones((t, rest), F32)], axis=1)
    sin = jnp.concatenate([-jnp.sin(ang), jnp.sin(ang), jnp.zeros((t, rest), F32)], axis=1)
    return jnp.tile(cos, (1, LANES // HEAD_DIM)), jnp.tile(sin, (1, LANES // HEAD_DIM))


def _kind_is(j, kinds, kind):
    hits = [j == jj for jj, k in enumerate(kinds) if k == kind]
    return functools.reduce(jnp.logical_or, hits) if hits else None


def _hn_fwd(x, gains, kinds, d, cos, sin, *, name, col0=0):
    t = x.shape[0]
    n = len(kinds)
    tr = _blk(t, ROW_BLOCK)
    seg = _seg_matrix()
    g8 = jnp.repeat(gains.astype(F32), SUBLANES, axis=0)

    def body(x_ref, g_ref, seg_ref, cos_ref, sin_ref, o_ref):
        j = pl.program_id(1)

        def normed(rope):
            for c in range(d // LANES):
                sl = slice(c * LANES, (c + 1) * LANES)
                xv = x_ref[:, sl]
                ms = _head_sum(xv * xv, seg_ref[...]) * (1.0 / HEAD_DIM)
                y = (xv * lax.rsqrt(ms + NORM_EPS)) * g_ref[0:1, sl]
                if rope:
                    y = y * cos_ref[...] + _rot_partner(y) * sin_ref[...]
                o_ref[:, sl] = y.astype(BF16)

        for kind in ("rope", "norm"):
            hit = _kind_is(j, kinds, kind)
            if hit is not None:
                pl.when(hit)(functools.partial(normed, kind == "rope"))
        hit = _kind_is(j, kinds, "cast")
        if hit is not None:
            @pl.when(hit)
            def _():
                o_ref[...] = x_ref[...].astype(BF16)

    return pl.pallas_call(
        body, name=name, grid=(t // tr, n),
        in_specs=[pl.BlockSpec((tr, d), lambda i, j: (i, col0 + j)), pl.BlockSpec((SUBLANES, d), lambda i, j: (j, 0)),
                  pl.BlockSpec((LANES, LANES), lambda i, j: (0, 0)),
                  pl.BlockSpec((tr, LANES), lambda i, j: (i, 0)), pl.BlockSpec((tr, LANES), lambda i, j: (i, 0))],
        out_specs=pl.BlockSpec((tr, d), lambda i, j: (i, j)),
        out_shape=jax.ShapeDtypeStruct((t, n * d), BF16),
        compiler_params=_params(("parallel", "parallel")),
    )(x, g8, seg, cos, sin)


def _hn_bwd(x, dy, gains, kinds, d, cos, sin, *, name):
    t = x.shape[0]
    n = len(kinds)
    tr = _blk(t, ROW_BLOCK)
    seg = _seg_matrix()
    g8 = jnp.repeat(gains.astype(F32), SUBLANES, axis=0)

    def body(x_ref, dy_ref, g_ref, seg_ref, cos_ref, sin_ref, dx_ref, dg_ref):
        j = pl.program_id(0)
        i = pl.program_id(1)

        @pl.when(i == 0)
        def _():
            dg_ref[...] = jnp.zeros_like(dg_ref)

        def normed(rope):
            for c in range(d // LANES):
                sl = slice(c * LANES, (c + 1) * LANES)
                xv = x_ref[:, sl]
                dyv = dy_ref[:, sl]
                if rope:
                    dyv = dyv * cos_ref[...] - _rot_partner(dyv) * sin_ref[...]
                ms = _head_sum(xv * xv, seg_ref[...]) * (1.0 / HEAD_DIM)
                rstd = lax.rsqrt(ms + NORM_EPS)
                xhat = xv * rstd
                dg_ref[:, sl] += _fold8(dyv * xhat)
                dyg = dyv * g_ref[0:1, sl]
                proj = _head_sum(dyg * xhat, seg_ref[...]) * (1.0 / HEAD_DIM)
                dx_ref[:, sl] = (rstd * (dyg - xhat * proj)).astype(BF16)

        for kind in ("rope", "norm"):
            hit = _kind_is(j, kinds, kind)
            if hit is not None:
                pl.when(hit)(functools.partial(normed, kind == "rope"))
        hit = _kind_is(j, kinds, "cast")
        if hit is not None:
            @pl.when(hit)
            def _():
                dx_ref[...] = dy_ref[...].astype(BF16)

    dx, dg = pl.pallas_call(
        body, name=name, grid=(n, t // tr),
        in_specs=[pl.BlockSpec((tr, d), lambda j, i: (i, j)), pl.BlockSpec((tr, d), lambda j, i: (i, j)),
                  pl.BlockSpec((SUBLANES, d), lambda j, i: (j, 0)),
                  pl.BlockSpec((LANES, LANES), lambda j, i: (0, 0)),
                  pl.BlockSpec((tr, LANES), lambda j, i: (i, 0)), pl.BlockSpec((tr, LANES), lambda j, i: (i, 0))],
        out_specs=[pl.BlockSpec((tr, d), lambda j, i: (i, j)), pl.BlockSpec((SUBLANES, d), lambda j, i: (j, 0))],
        out_shape=[jax.ShapeDtypeStruct((t, n * d), BF16), jax.ShapeDtypeStruct((n * SUBLANES, d), F32)],
        compiler_params=_params(("arbitrary", "arbitrary")),
    )(x, dy, g8, seg, cos, sin)
    dg = dg.reshape(n, SUBLANES, d // HEAD_DIM, HEAD_DIM).sum(axis=(1, 2))
    return dx, dg


def _head_dot(a, b, *, name):
    t, d = a.shape
    tr = _blk(t, ROW_BLOCK)
    seg = _seg_matrix()

    def body(a_ref, b_ref, seg_ref, o_ref):
        for c in range(d // LANES):
            sl = slice(c * LANES, (c + 1) * LANES)
            o_ref[:, sl] = _head_sum(a_ref[:, sl].astype(BF16).astype(F32) * b_ref[:, sl], seg_ref[...])

    return pl.pallas_call(
        body, name=name, grid=(t // tr,),
        in_specs=[pl.BlockSpec((tr, d), lambda i: (i, 0)), pl.BlockSpec((tr, d), lambda i: (i, 0)),
                  pl.BlockSpec((LANES, LANES), lambda i: (0, 0))],
        out_specs=pl.BlockSpec((tr, d), lambda i: (i, 0)),
        out_shape=jax.ShapeDtypeStruct((t, d), F32),
        compiler_params=_params(("parallel",)),
    )(a, b, seg)


def _half_mask(shape):
    return lax.broadcasted_iota(jnp.int32, shape, 1) < HEAD_DIM


def _band_valid(first):
    qi = lax.broadcasted_iota(jnp.int32, (BAND, 2 * BAND), 0)
    kj = lax.broadcasted_iota(jnp.int32, (BAND, 2 * BAND), 1)
    dist = qi + BAND - kj
    return (dist >= 0) & (dist <= BAND) & ((kj >= BAND) | jnp.logical_not(first))


_NT = (((1,), (1,)), ((), ()))
_TN = (((0,), (0,)), ((), ()))


def _dot2(p, v):
    hi = p.astype(BF16)
    lo = (p - hi.astype(F32)).astype(BF16)
    return jnp.dot(hi, v, preferred_element_type=F32) + jnp.dot(lo, v, preferred_element_type=F32)


def _band_fwd(qkv, dil, cfg, *, name):
    t, d = cfg.tokens, cfg.d_model
    w = 3 * d
    rows = t // dil
    nbt = rows // BAND
    nb = cfg.seq // (dil * BAND)
    view = qkv.reshape(rows, dil * w)
    ncol = w // d

    def body(q_ref, kp_ref, kc_ref, vp_ref, vc_ref, o_ref, lse_ref):
        i = pl.program_id(1)
        valid = _band_valid(i % nb == 0)
        half = _half_mask((BAND, LANES))
        for hp in range(d // LANES):
            sl = slice(hp * LANES, (hp + 1) * LANES)
            q2 = q_ref[:, sl]
            kk = jnp.concatenate([kp_ref[:, sl], kc_ref[:, sl]], axis=0)
            vv = jnp.concatenate([vp_ref[:, sl], vc_ref[:, sl]], axis=0)
            outs, lses = [], []
            for e in range(2):
                qe = jnp.where(half == (e == 0), q2, jnp.zeros_like(q2))
                s = lax.dot_general(qe, kk, _NT, preferred_element_type=F32)
                s = jnp.where(valid, s, NEG)
                m = jnp.max(s, axis=1, keepdims=True)
                p = jnp.exp(s - m)
                l = jnp.sum(p, axis=1, keepdims=True)
                outs.append(_dot2(p * (1.0 / l), vv))
                lses.append(m + jnp.log(l))
            o_ref[:, sl] = jnp.where(half, outs[0], outs[1])
            lse_ref[:, sl] = jnp.where(half, lses[0], lses[1])

    def col(which):
        return lambda r, i: (i, r * ncol + which)

    def col_prev(which):
        return lambda r, i: (jnp.maximum(i - 1, 0), r * ncol + which)

    blk = (BAND, d)
    o, lse = pl.pallas_call(
        body, name=name, grid=(dil, nbt),
        in_specs=[pl.BlockSpec(blk, col(0)), pl.BlockSpec(blk, col_prev(1)), pl.BlockSpec(blk, col(1)),
                  pl.BlockSpec(blk, col_prev(2)), pl.BlockSpec(blk, col(2))],
        out_specs=[pl.BlockSpec(blk, lambda r, i: (i, r)), pl.BlockSpec(blk, lambda r, i: (i, r))],
        out_shape=[jax.ShapeDtypeStruct((rows, dil * d), F32), jax.ShapeDtypeStruct((rows, dil * d), F32)],
        compiler_params=_params(("parallel", "arbitrary")),
    )(view, view, view, view, view)
    return o.reshape(t, d), lse.reshape(t, d)


def _band_bwd(qkv, dmixed, lse_all, dsum, dil, cfg, *, name):
    t, d = cfg.tokens, cfg.d_model
    w = 3 * d
    rows = t // dil
    nbt = rows // BAND
    nb = cfg.seq // (dil * BAND)
    view = qkv.reshape(rows, dil * w)
    ncol = w // d
    do_v, l_v, d_v = (z.reshape(rows, dil * d) for z in (dmixed, lse_all, dsum))

    def body(q_ref, kp_ref, kc_ref, vp_ref, vc_ref, do_ref, l_ref, ds_ref, dq_ref, dk_ref, dv_ref, ck_ref, cv_ref):
        i = pl.program_id(1)

        @pl.when(i < nbt)
        def _():
            valid = _band_valid(i % nb == 0)
            half = _half_mask((BAND, LANES))
            half2 = _half_mask((2 * BAND, LANES))
            for hp in range(d // LANES):
                sl = slice(hp * LANES, (hp + 1) * LANES)
                q2 = q_ref[:, sl]
                kk = jnp.concatenate([kp_ref[:, sl], kc_ref[:, sl]], axis=0)
                vv = jnp.concatenate([vp_ref[:, sl], vc_ref[:, sl]], axis=0)
                do2 = do_ref[:, sl].astype(BF16)
                dqs, dks, dvs = [], [], []
                for e in range(2):
                    lane0 = e * HEAD_DIM
                    keep = half == (e == 0)
                    qe = jnp.where(keep, q2, jnp.zeros_like(q2))
                    doe = jnp.where(keep, do2, jnp.zeros_like(do2))
                    s = lax.dot_general(qe, kk, _NT, preferred_element_type=F32)
                    s = jnp.where(valid, s, NEG)
                    p = jnp.exp(s - l_ref[:, hp * LANES + lane0:hp * LANES + lane0 + 1])
                    dp = lax.dot_general(doe, vv, _NT, preferred_element_type=F32)
                    dsc = (p * (dp - ds_ref[:, hp * LANES + lane0:hp * LANES + lane0 + 1])).astype(BF16)
                    dqs.append(jnp.dot(dsc, kk, preferred_element_type=F32))
                    dks.append(lax.dot_general(dsc, q2, _TN, preferred_element_type=F32))
                    dvs.append(lax.dot_general(p.astype(BF16), do2, _TN, preferred_element_type=F32))
                dq_ref[:, sl] = jnp.where(half, dqs[0], dqs[1])
                dkk = jnp.where(half2, dks[0], dks[1])
                dvv = jnp.where(half2, dvs[0], dvs[1])

                @pl.when(i > 0)
                def _():
                    dk_ref[:, sl] = ck_ref[:, sl] + dkk[:BAND]
                    dv_ref[:, sl] = cv_ref[:, sl] + dvv[:BAND]

                ck_ref[:, sl] = dkk[BAND:]
                cv_ref[:, sl] = dvv[BAND:]

        @pl.when(i == nbt)
        def _():
            dk_ref[...] = ck_ref[...]
            dv_ref[...] = cv_ref[...]

    def cur(i):
        return jnp.minimum(i, nbt - 1)

    def col(which):
        return lambda r, i: (cur(i), r * ncol + which)

    def col_prev(which):
        return lambda r, i: (jnp.maximum(cur(i) - 1, 0), r * ncol + which)

    blk = (BAND, d)
    here = pl.BlockSpec(blk, lambda r, i: (cur(i), r))
    behind = pl.BlockSpec(blk, lambda r, i: (jnp.maximum(i - 1, 0), r))
    shape = jax.ShapeDtypeStruct((rows, dil * d), F32)
    dq, dk, dv = pl.pallas_call(
        body, name=name, grid=(dil, nbt + 1),
        in_specs=[pl.BlockSpec(blk, col(0)), pl.BlockSpec(blk, col_prev(1)), pl.BlockSpec(blk, col(1)),
                  pl.BlockSpec(blk, col_prev(2)), pl.BlockSpec(blk, col(2)), here, here, here],
        out_specs=[here, behind, behind],
        out_shape=[shape, shape, shape],
        scratch_shapes=[pltpu.VMEM(blk, F32), pltpu.VMEM(blk, F32)],
        compiler_params=_params(("arbitrary", "arbitrary")),
    )(view, view, view, view, view, do_v, l_v, d_v)
    return dq.reshape(t, d), dk.reshape(t, d), dv.reshape(t, d)


def _mix_fwd(outs, lses, *, name):
    t, d = outs[0].shape
    tr = _blk(t, ROW_BLOCK)
    ng = len(outs)

    def body(*refs):
        o_refs, l_refs = refs[:ng], refs[ng:2 * ng]
        mixed_ref, lse_ref = refs[2 * ng:]
        ls = [r[...] for r in l_refs]
        m = functools.reduce(jnp.maximum, ls)
        es = [jnp.exp(l - m) for l in ls]
        tot = functools.reduce(jnp.add, es)
        inv = 1.0 / tot
        mixed_ref[...] = functools.reduce(jnp.add, [(e * inv) * r[...] for e, r in zip(es, o_refs)])
        lse_ref[...] = m + jnp.log(tot)

    spec = pl.BlockSpec((tr, d), lambda i: (i, 0))
    return pl.pallas_call(
        body, name=name, grid=(t // tr,),
        in_specs=[spec] * (2 * ng), out_specs=[spec, spec],
        out_shape=[jax.ShapeDtypeStruct((t, d), F32), jax.ShapeDtypeStruct((t, d), F32)],
        compiler_params=_params(("parallel",)),
    )(*outs, *lses)


GATE_BLOCK = 256


def _tri(n, upper):
    r = lax.broadcasted_iota(jnp.int32, (n, n), 0)
    c = lax.broadcasted_iota(jnp.int32, (n, n), 1)
    return ((c >= r) if upper else (c <= r)).astype(BF16)


def _tri_dot(tri, x):
    hi = x.astype(BF16)
    r1 = x - hi.astype(F32)
    mid = r1.astype(BF16)
    lo = (r1 - mid.astype(F32)).astype(BF16)
    dot = functools.partial(jnp.dot, preferred_element_type=F32)
    return dot(tri, hi) + dot(tri, mid) + dot(tri, lo)


def _log_sigmoid(z):
    return jnp.minimum(z, 0.0) - jnp.log(1.0 + jnp.exp(-jnp.abs(z)))


def _gate_fwd(proj, col_block, bias, cfg, *, name):
    tr = _blk(cfg.seq, GATE_BLOCK)
    nblk = cfg.seq // tr

    def body(z_ref, b_ref, tri_ref, o_ref, carry_ref):
        i = pl.program_id(1)

        @pl.when(i == 0)
        def _():
            carry_ref[...] = jnp.zeros_like(carry_ref)

        logf = _log_sigmoid(z_ref[...] + b_ref[0:1, :])
        cum = _tri_dot(tri_ref[...], logf) + carry_ref[0:1, :]
        o_ref[...] = cum
        carry_ref[...] = jnp.broadcast_to(cum[tr - 1:tr, :], carry_ref.shape)

    return pl.pallas_call(
        body, name=name, grid=(cfg.batch, nblk),
        in_specs=[pl.BlockSpec((tr, LANES), lambda b, i: (b * nblk + i, col_block)),
                  pl.BlockSpec((SUBLANES, LANES), lambda b, i: (0, 0)),
                  pl.BlockSpec((tr, tr), lambda b, i: (0, 0))],
        out_specs=pl.BlockSpec((tr, LANES), lambda b, i: (b * nblk + i, 0)),
        out_shape=jax.ShapeDtypeStruct((cfg.tokens, LANES), F32),
        scratch_shapes=[pltpu.VMEM((SUBLANES, LANES), F32)],
        compiler_params=_params(("arbitrary", "arbitrary")),
    )(proj, jnp.broadcast_to(bias, (SUBLANES, LANES)), _tri(tr, upper=False))


def _gate_bwd(proj, col_block, bias, dcum, cfg, *, name):
    tr = _blk(cfg.seq, GATE_BLOCK)
    nblk = cfg.seq // tr

    def body(z_ref, b_ref, tri_ref, dc_ref, dz_ref, db_ref, carry_ref):
        b = pl.program_id(0)
        i = pl.program_id(1)

        @pl.when(i == 0)
        def _():
            carry_ref[...] = jnp.zeros_like(carry_ref)

        @pl.when((i == 0) & (b == 0))
        def _():
            db_ref[...] = jnp.zeros_like(db_ref)

        dcv = dc_ref[...]
        dlogf = _tri_dot(tri_ref[...], dcv) + carry_ref[0:1, :]
        carry_ref[...] = jnp.broadcast_to(dlogf[0:1, :], carry_ref.shape)
        dz = dlogf * jax.nn.sigmoid(-(z_ref[...] + b_ref[0:1, :]))
        dz_ref[...] = dz
        db_ref[...] += _fold8(dz)

    def rev(b, i):
        return (b * nblk + nblk - 1 - i, 0)

    dz, db = pl.pallas_call(
        body, name=name, grid=(cfg.batch, nblk),
        in_specs=[pl.BlockSpec((tr, LANES), lambda b, i: (b * nblk + nblk - 1 - i, col_block)),
                  pl.BlockSpec((SUBLANES, LANES), lambda b, i: (0, 0)),
                  pl.BlockSpec((tr, tr), lambda b, i: (0, 0)),
                  pl.BlockSpec((tr, LANES), rev)],
        out_specs=[pl.BlockSpec((tr, LANES), rev), pl.BlockSpec((SUBLANES, LANES), lambda b, i: (0, 0))],
        out_shape=[jax.ShapeDtypeStruct((cfg.tokens, LANES), F32), jax.ShapeDtypeStruct((SUBLANES, LANES), F32)],
        scratch_shapes=[pltpu.VMEM((SUBLANES, LANES), F32)],
        compiler_params=_params(("arbitrary", "arbitrary")),
    )(proj, jnp.broadcast_to(bias, (SUBLANES, LANES)), _tri(tr, upper=True), dcum)
    return dz, jnp.sum(db, axis=0)


FOX_BLOCK = 256


def _fox_scores(q2, k2, e, half, mask, cref, ck_row):
    qe = jnp.where(half == (e == 0), q2, jnp.zeros_like(q2))
    s = lax.dot_general(qe, k2, _NT, preferred_element_type=F32)
    return jnp.where(mask, s + (cref - ck_row), NEG)


def _causal(qi, ki, tq):
    r = lax.broadcasted_iota(jnp.int32, (tq, tq), 0) + qi * tq
    c = lax.broadcasted_iota(jnp.int32, (tq, tq), 1) + ki * tq
    return r >= c


def _fox_fwd(q, kv, cum_t, cfg, *, name):
    t, d, hrows = cfg.tokens, cfg.d_model, cum_t.shape[0]
    tq = _blk(cfg.seq, FOX_BLOCK)
    nq = cfg.seq // tq

    def body(q_ref, k_ref, v_ref, cq_ref, ck_ref, o_ref, lse_ref, m_ref, l_ref, acc_ref):
        qi, ki = pl.program_id(1), pl.program_id(2)

        @pl.when(ki == 0)
        def _():
            m_ref[...] = jnp.full_like(m_ref, NEG)
            l_ref[...] = jnp.zeros_like(l_ref)
            acc_ref[...] = jnp.zeros_like(acc_ref)

        @pl.when(ki <= qi)
        def _():
            mask = _causal(qi, ki, tq)
            half = _half_mask((tq, LANES))
            for hp in range(d // LANES):
                sl = slice(hp * LANES, (hp + 1) * LANES)
                q2, k2, v2 = q_ref[:, sl], k_ref[:, sl], v_ref[:, sl]
                alphas, pvs = [], []
                for e in range(2):
                    h = 2 * hp + e
                    s = _fox_scores(q2, k2, e, half, mask, cq_ref[h:h + 1, 0:1], ck_ref[h:h + 1, :])
                    m_prev = m_ref[h]
                    m_new = jnp.maximum(m_prev, jnp.max(s, axis=1, keepdims=True))
                    alpha = jnp.exp(m_prev - m_new)
                    p = jnp.exp(s - m_new[:, 0:1])
                    l_ref[h] = alpha * l_ref[h] + jnp.sum(p, axis=1, keepdims=True)
                    m_ref[h] = m_new
                    alphas.append(alpha)
                    pvs.append(_dot2(p, v2))
                acc = acc_ref[:, sl]
                acc_ref[:, sl] = jnp.where(half, alphas[0] * acc + pvs[0], alphas[1] * acc + pvs[1])

        @pl.when(ki == qi)
        def _():
            half = _half_mask((tq, LANES))
            for hp in range(d // LANES):
                sl = slice(hp * LANES, (hp + 1) * LANES)
                h0, h1 = 2 * hp, 2 * hp + 1
                inv = jnp.where(half, 1.0 / l_ref[h0], 1.0 / l_ref[h1])
                o_ref[:, sl] = acc_ref[:, sl] * inv
                lse0 = m_ref[h0] + jnp.log(l_ref[h0]) - cq_ref[h0:h0 + 1, 0:1]
                lse1 = m_ref[h1] + jnp.log(l_ref[h1]) - cq_ref[h1:h1 + 1, 0:1]
                lse_ref[:, sl] = jnp.where(half, lse0, lse1)

    def qrow(b, qi, ki):
        return (b * nq + qi, 0)

    def krow(b, qi, ki):
        return (b * nq + jnp.minimum(ki, qi), 0)

    o, lse = pl.pallas_call(
        body, name=name, grid=(cfg.batch, nq, nq),
        in_specs=[pl.BlockSpec((tq, d), qrow),
                  pl.BlockSpec((tq, d), krow),
                  pl.BlockSpec((tq, d), lambda b, qi, ki: (b * nq + jnp.minimum(ki, qi), 1)),
                  pl.BlockSpec((hrows, tq), lambda b, qi, ki: (0, b * nq + qi)),
                  pl.BlockSpec((hrows, tq), lambda b, qi, ki: (0, b * nq + jnp.minimum(ki, qi)))],
        out_specs=[pl.BlockSpec((tq, d), qrow), pl.BlockSpec((tq, d), qrow)],
        out_shape=[jax.ShapeDtypeStruct((t, d), F32), jax.ShapeDtypeStruct((t, d), F32)],
        scratch_shapes=[pltpu.VMEM((cfg.heads, tq, LANES), F32), pltpu.VMEM((cfg.heads, tq, LANES), F32),
                        pltpu.VMEM((tq, d), F32)],
        compiler_params=_params(("parallel", "parallel", "arbitrary")),
    )(q, kv, kv, cum_t, cum_t)
    return o, lse


def _fox_bwd_q(q, kv, cum_t, do, lse, dsum, cfg, *, name):
    t, d, hrows = cfg.tokens, cfg.d_model, cum_t.shape[0]
    tq = _blk(cfg.seq, FOX_BLOCK)
    nq = cfg.seq // tq

    def body(q_ref, k_ref, v_ref, cq_ref, ck_ref, do_ref, l_ref, ds_ref, dq_ref, acc_ref):
        qi, ki = pl.program_id(1), pl.program_id(2)

        @pl.when(ki == 0)
        def _():
            acc_ref[...] = jnp.zeros_like(acc_ref)

        @pl.when(ki <= qi)
        def _():
            mask = _causal(qi, ki, tq)
            half = _half_mask((tq, LANES))
            for hp in range(d // LANES):
                sl = slice(hp * LANES, (hp + 1) * LANES)
                q2, k2, v2 = q_ref[:, sl], k_ref[:, sl], v_ref[:, sl]
                do2 = do_ref[:, sl].astype(BF16)
                dqs = []
                for e in range(2):
                    h = 2 * hp + e
                    lane0 = hp * LANES + e * HEAD_DIM
                    cref = cq_ref[h:h + 1, 0:1]
                    s = _fox_scores(q2, k2, e, half, mask, cref, ck_ref[h:h + 1, :])
                    p = jnp.exp(s - (l_ref[:, lane0:lane0 + 1] + cref))
                    doe = jnp.where(half == (e == 0), do2, jnp.zeros_like(do2))
                    dp = lax.dot_general(doe, v2, _NT, preferred_element_type=F32)
                    dsc = (p * (dp - ds_ref[:, lane0:lane0 + 1])).astype(BF16)
                    dqs.append(jnp.dot(dsc, k2, preferred_element_type=F32))
                acc_ref[:, sl] += jnp.where(half, dqs[0], dqs[1])

        @pl.when(ki == qi)
        def _():
            dq_ref[...] = acc_ref[...]

    def qrow(b, qi, ki):
        return (b * nq + qi, 0)

    return pl.pallas_call(
        body, name=name, grid=(cfg.batch, nq, nq),
        in_specs=[pl.BlockSpec((tq, d), qrow),
                  pl.BlockSpec((tq, d), lambda b, qi, ki: (b * nq + jnp.minimum(ki, qi), 0)),
                  pl.BlockSpec((tq, d), lambda b, qi, ki: (b * nq + jnp.minimum(ki, qi), 1)),
                  pl.BlockSpec((hrows, tq), lambda b, qi, ki: (0, b * nq + qi)),
                  pl.BlockSpec((hrows, tq), lambda b, qi, ki: (0, b * nq + jnp.minimum(ki, qi))),
                  pl.BlockSpec((tq, d), qrow), pl.BlockSpec((tq, d), qrow), pl.BlockSpec((tq, d), qrow)],
        out_specs=pl.BlockSpec((tq, d), qrow),
        out_shape=jax.ShapeDtypeStruct((t, d), F32),
        scratch_shapes=[pltpu.VMEM((tq, d), F32)],
        compiler_params=_params(("parallel", "parallel", "arbitrary")),
    )(q, kv, kv, cum_t, cum_t, do, lse, dsum)


def _fox_bwd_kv(q, kv, cum_t, do, lse, dsum, cfg, *, name):
    t, d, hrows = cfg.tokens, cfg.d_model, cum_t.shape[0]
    tq = _blk(cfg.seq, FOX_BLOCK)
    nq = cfg.seq // tq

    def body(q_ref, k_ref, v_ref, cq_ref, ck_ref, do_ref, l_ref, ds_ref, dk_ref, dv_ref, dc_ref,
             kacc_ref, vacc_ref, cacc_ref):
        ki, qi = pl.program_id(1), pl.program_id(2)

        @pl.when(qi == 0)
        def _():
            kacc_ref[...] = jnp.zeros_like(kacc_ref)
            vacc_ref[...] = jnp.zeros_like(vacc_ref)
            cacc_ref[...] = jnp.zeros_like(cacc_ref)

        @pl.when(qi >= ki)
        def _():
            mask = _causal(qi, ki, tq)
            half = _half_mask((tq, LANES))
            for hp in range(d // LANES):
                sl = slice(hp * LANES, (hp + 1) * LANES)
                q2, k2, v2 = q_ref[:, sl], k_ref[:, sl], v_ref[:, sl]
                do2 = do_ref[:, sl].astype(BF16)
                dks, dvs = [], []
                for e in range(2):
                    h = 2 * hp + e
                    lane0 = hp * LANES + e * HEAD_DIM
                    cref = cq_ref[h:h + 1, 0:1]
                    s = _fox_scores(q2, k2, e, half, mask, cref, ck_ref[h:h + 1, :])
                    p = jnp.exp(s - (l_ref[:, lane0:lane0 + 1] + cref))
                    doe = jnp.where(half == (e == 0), do2, jnp.zeros_like(do2))
                    dp = lax.dot_general(doe, v2, _NT, preferred_element_type=F32)
                    dsf = p * (dp - ds_ref[:, lane0:lane0 + 1])
                    cacc_ref[h:h + 1, :] -= jnp.sum(dsf, axis=0, keepdims=True)
                    dks.append(lax.dot_general(dsf.astype(BF16), q2, _TN, preferred_element_type=F32))
                    dvs.append(lax.dot_general(p.astype(BF16), do2, _TN, preferred_element_type=F32))
                kacc_ref[:, sl] += jnp.where(half, dks[0], dks[1])
                vacc_ref[:, sl] += jnp.where(half, dvs[0], dvs[1])

        @pl.when(qi == nq - 1)
        def _():
            dk_ref[...] = kacc_ref[...]
            dv_ref[...] = vacc_ref[...]
            dc_ref[...] = cacc_ref[...]

    def qrow(b, ki, qi):
        return (b * nq + jnp.maximum(qi, ki), 0)

    def krow(b, ki, qi):
        return (b * nq + ki, 0)

    return pl.pallas_call(
        body, name=name, grid=(cfg.batch, nq, nq),
        in_specs=[pl.BlockSpec((tq, d), qrow),
                  pl.BlockSpec((tq, d), krow),
                  pl.BlockSpec((tq, d), lambda b, ki, qi: (b * nq + ki, 1)),
                  pl.BlockSpec((hrows, tq), lambda b, ki, qi: (0, b * nq + jnp.maximum(qi, ki))),
                  pl.BlockSpec((hrows, tq), lambda b, ki, qi: (0, b * nq + ki)),
                  pl.BlockSpec((tq, d), qrow), pl.BlockSpec((tq, d), qrow), pl.BlockSpec((tq, d), qrow)],
        out_specs=[pl.BlockSpec((tq, d), krow), pl.BlockSpec((tq, d), krow),
                   pl.BlockSpec((hrows, tq), lambda b, ki, qi: (0, b * nq + ki))],
        out_shape=[jax.ShapeDtypeStruct((t, d), F32), jax.ShapeDtypeStruct((t, d), F32),
                   jax.ShapeDtypeStruct((hrows, t), F32)],
        scratch_shapes=[pltpu.VMEM((tq, d), F32), pltpu.VMEM((tq, d), F32), pltpu.VMEM((hrows, tq), F32)],
        compiler_params=_params(("parallel", "parallel", "arbitrary")),
    )(q, kv, kv, cum_t, cum_t, do, lse, dsum)


AUG = LANES
BIAS_TERMS = 3


def _fox_aug_q(qp, cfg):
    t, hh = cfg.tokens, cfg.heads
    q3 = qp.reshape(t, hh, HEAD_DIM)
    ones = jnp.ones((t, hh, BIAS_TERMS), BF16)
    zeros = jnp.zeros((t, hh, AUG - HEAD_DIM - BIAS_TERMS), BF16)
    return jnp.concatenate([q3, ones, zeros], axis=2).reshape(t, hh * AUG).T


def _fox_aug_k(k, cum, cfg):
    t, hh = cfg.tokens, cfg.heads
    c = -cum
    hi = lax.reduce_precision(c, 8, 7)
    mid = lax.reduce_precision(c - hi, 8, 7)
    lo = c - hi - mid
    zeros = jnp.zeros((t, hh, AUG - HEAD_DIM - BIAS_TERMS), BF16)
    parts = [k.reshape(t, hh, HEAD_DIM)] + [z.astype(BF16)[..., None] for z in (hi, mid, lo)] + [zeros]
    return jnp.concatenate(parts, axis=2).reshape(t, hh * AUG)


def _keys_visible(tq):
    s = lax.broadcasted_iota(jnp.int32, (tq, tq), 0)
    t = lax.broadcasted_iota(jnp.int32, (tq, tq), 1)
    return s <= t


def _fox_fwd_t(qa_t, k_aug, v_t, cfg, *, name):
    t, d, hh = cfg.tokens, cfg.d_model, cfg.heads
    tq = _blk(cfg.seq, FOX_BLOCK)
    nq = cfg.seq // tq

    def body(qa_ref, ka_ref, vt_ref, o_ref, lse_ref, m_ref, l_ref, acc_ref):
        qi, ki = pl.program_id(1), pl.program_id(2)

        @pl.when(ki == 0)
        def _():
            m_ref[...] = jnp.full_like(m_ref, NEG)
            l_ref[...] = jnp.zeros_like(l_ref)
            acc_ref[...] = jnp.zeros_like(acc_ref)

        def step(diagonal):
            for h in range(hh):
                rows = slice(h * HEAD_DIM, (h + 1) * HEAD_DIM)
                s = jnp.dot(ka_ref[:, h * AUG:(h + 1) * AUG], qa_ref[h * AUG:(h + 1) * AUG, :],
                            preferred_element_type=F32)
                if diagonal:
                    s = jnp.where(_keys_visible(tq), s, NEG)
                m_prev = m_ref[h:h + 1, :]
                m_new = jnp.maximum(m_prev, jnp.max(s, axis=0, keepdims=True))
                alpha = jnp.exp(m_prev - m_new)
                p = jnp.exp(s - m_new)
                l_ref[h:h + 1, :] = alpha * l_ref[h:h + 1, :] + jnp.sum(p, axis=0, keepdims=True)
                m_ref[h:h + 1, :] = m_new
                hi = p.astype(BF16)
                lo = (p - hi.astype(F32)).astype(BF16)
                vt = vt_ref[rows, :]
                acc_ref[rows, :] = (alpha * acc_ref[rows, :] + jnp.dot(vt, hi, preferred_element_type=F32)
                                    + jnp.dot(vt, lo, preferred_element_type=F32))

        pl.when(ki < qi)(functools.partial(step, False))
        pl.when(ki == qi)(functools.partial(step, True))

        @pl.when(ki == qi)
        def _():
            for h in range(hh):
                rows = slice(h * HEAD_DIM, (h + 1) * HEAD_DIM)
                o_ref[rows, :] = acc_ref[rows, :] * (1.0 / l_ref[h:h + 1, :])
            lse_ref[...] = m_ref[...] + jnp.log(l_ref[...])

    def qcol(b, qi, ki):
        return (0, b * nq + qi)

    return pl.pallas_call(
        body, name=name, grid=(cfg.batch, nq, nq),
        in_specs=[pl.BlockSpec((hh * AUG, tq), qcol),
                  pl.BlockSpec((tq, hh * AUG), lambda b, qi, ki: (b * nq + jnp.minimum(ki, qi), 0)),
                  pl.BlockSpec((d, tq), lambda b, qi, ki: (0, b * nq + jnp.minimum(ki, qi)))],
        out_specs=[pl.BlockSpec((d, tq), qcol), pl.BlockSpec((hh, tq), qcol)],
        out_shape=[jax.ShapeDtypeStruct((d, t), F32), jax.ShapeDtypeStruct((hh, t), F32)],
        scratch_shapes=[pltpu.VMEM((hh, tq), F32), pltpu.VMEM((hh, tq), F32), pltpu.VMEM((d, tq), F32)],
        compiler_params=_params(("parallel", "parallel", "arbitrary")),
    )(qa_t, k_aug, v_t)


def _head_dot_t(a_t, b_t, cfg, *, name):
    t, d, hh = cfg.tokens, cfg.d_model, cfg.heads
    tc = _blk(t, 2 * ROW_BLOCK)

    def body(a_ref, b_ref, o_ref):
        for h in range(hh):
            rows = slice(h * HEAD_DIM, (h + 1) * HEAD_DIM)
            o_ref[h:h + 1, :] = jnp.sum(a_ref[rows, :].astype(F32) * b_ref[rows, :], axis=0, keepdims=True)

    return pl.pallas_call(
        body, name=name, grid=(t // tc,),
        in_specs=[pl.BlockSpec((d, tc), lambda i: (0, i)), pl.BlockSpec((d, tc), lambda i: (0, i))],
        out_specs=pl.BlockSpec((hh, tc), lambda i: (0, i)),
        out_shape=jax.ShapeDtypeStruct((hh, t), F32),
        compiler_params=_params(("parallel",)),
    )(a_t, b_t)


def _fox_bwd_t(qa_t, k_aug, k_t, v, do_t, do, lse, dsum, cfg, *, name):
    t, d, hh = cfg.tokens, cfg.d_model, cfg.heads
    tq = _blk(cfg.seq, FOX_BLOCK)
    nq = cfg.seq // tq

    def body(qa_ref, ka_ref, kt_ref, v_ref, dot_ref, do_ref, lse_ref, ds_ref, dq_hbm, dk_ref, dv_ref, dc_ref,
             dq_acc, sem):
        b, ki, qi = pl.program_id(0), pl.program_id(1), pl.program_id(2)
        qq = jnp.maximum(qi, ki)

        @pl.when((ki == 0) & (qi == 0))
        def _():
            dq_acc[...] = jnp.zeros_like(dq_acc)

        @pl.when(qi == 0)
        def _():
            dk_ref[...] = jnp.zeros_like(dk_ref)
            dv_ref[...] = jnp.zeros_like(dv_ref)
            dc_ref[...] = jnp.zeros_like(dc_ref)

        def step(diagonal):
            upper = lax.broadcasted_iota(jnp.int32, (LANES, tq), 0) < HEAD_DIM
            half = _half_mask((tq, LANES))
            for hp in range(hh // 2):
                pair = slice(hp * LANES, (hp + 1) * LANES)
                dvs = []
                for e in range(2):
                    h = 2 * hp + e
                    rows = slice(h * HEAD_DIM, (h + 1) * HEAD_DIM)
                    aug = slice(h * AUG, (h + 1) * AUG)
                    s = jnp.dot(ka_ref[:, aug], qa_ref[aug, :], preferred_element_type=F32)
                    if diagonal:
                        s = jnp.where(_keys_visible(tq), s, NEG)
                    p = jnp.exp(s - lse_ref[h:h + 1, :])
                    dot2 = dot_ref[pair, :]
                    dote = jnp.where(upper == (e == 0), dot2, jnp.zeros_like(dot2))
                    dp = jnp.dot(v_ref[:, pair], dote, preferred_element_type=F32)
                    dsf = p * (dp - ds_ref[h:h + 1, :])
                    dc_ref[:, h:h + 1] -= jnp.sum(dsf, axis=1, keepdims=True)
                    dsc = dsf.astype(BF16)
                    dvs.append(jnp.dot(p.astype(BF16), do_ref[:, pair], preferred_element_type=F32))
                    dk_ref[:, aug] += lax.dot_general(dsc, qa_ref[aug, :], _NT, preferred_element_type=F32)
                    dq_acc[qq, rows, :] += jnp.dot(kt_ref[rows, :], dsc, preferred_element_type=F32)
                dv_ref[:, pair] += jnp.where(half, dvs[0], dvs[1])

        pl.when(qi > ki)(functools.partial(step, False))
        pl.when(qi == ki)(functools.partial(step, True))

        @pl.when((ki == nq - 1) & (qi == nq - 1))
        def _():
            cp = pltpu.make_async_copy(dq_acc, dq_hbm.at[b], sem)
            cp.start()
            cp.wait()

    def qcol(b, ki, qi):
        return (0, b * nq + jnp.maximum(qi, ki))

    def krow(b, ki, qi):
        return (b * nq + ki, 0)

    return pl.pallas_call(
        body, name=name, grid=(cfg.batch, nq, nq),
        in_specs=[pl.BlockSpec((hh * AUG, tq), qcol),
                  pl.BlockSpec((tq, hh * AUG), krow),
                  pl.BlockSpec((d, tq), lambda b, ki, qi: (0, b * nq + ki)),
                  pl.BlockSpec((tq, d), krow),
                  pl.BlockSpec((d, tq), qcol),
                  pl.BlockSpec((tq, d), lambda b, ki, qi: (b * nq + jnp.maximum(qi, ki), 0)),
                  pl.BlockSpec((hh, tq), qcol), pl.BlockSpec((hh, tq), qcol)],
        out_specs=[pl.BlockSpec(memory_space=pl.ANY), pl.BlockSpec((tq, hh * AUG), krow),
                   pl.BlockSpec((tq, d), krow), pl.BlockSpec((tq, LANES), krow)],
        out_shape=[jax.ShapeDtypeStruct((cfg.batch, nq, d, tq), F32), jax.ShapeDtypeStruct((t, hh * AUG), F32),
                   jax.ShapeDtypeStruct((t, d), F32), jax.ShapeDtypeStruct((t, LANES), F32)],
        scratch_shapes=[pltpu.VMEM((nq, d, tq), F32), pltpu.SemaphoreType.DMA],
        compiler_params=_params(("arbitrary", "arbitrary", "arbitrary")),
    )(qa_t, k_aug, k_t, v, do_t, do, lse, dsum)


WIDE = 1536


def _fwd(a, w, *, name, res=None, scale=1.0):
    return _mm(a, w, form="F", out_dtype=F32, name=name, bn=WIDE, bk=WIDE, res=res, scale=scale)


def _bwd(dy, w, *, name, scale=1.0):
    return _mm(dy, w, form="B", out_dtype=F32, name=name, bn=WIDE, bk=WIDE, scale=scale)


def _wgrad(a, dy, w, *, name, scale=1.0):
    return _mm_grad(a, dy, w.shape[0], name=name, bm=WIDE, bn=WIDE, scale=scale)


def _ffn_fwd(h, g, w_in, w_out, tag):
    n = _rms_fwd(h, g, name=f"{tag}_norm")
    u = _fwd(n, w_in, name=f"{tag}_in")
    a = _swiglu_fwd(u, name=f"{tag}_act")
    return _fwd(a, w_out, name=f"{tag}_out", res=h, scale=0.5), (n, u, a)


def _ffn_bwd(dh_out, h, g, w_in, w_out, saved, tag):
    n, u, a = saved
    da = _bwd(dh_out, w_out, name=f"{tag}_out_dx", scale=0.5)
    dw_out = _wgrad(a, dh_out, w_out, name=f"{tag}_out_dw", scale=0.5)
    du = _swiglu_bwd(u, da, name=f"{tag}_act_bwd")
    dn = _bwd(du, w_in, name=f"{tag}_in_dx")
    dw_in = _wgrad(n, du, w_in, name=f"{tag}_in_dw")
    dh, dg = _rms_bwd(h, g, dn, dh_out, name=f"{tag}_norm_bwd")
    return dh, dg, dw_in, dw_out


def _head_gain(g, heads, scale=1.0):
    return jnp.tile(g.astype(F32) * scale, heads)


def _local_step(cfg, x, positions, target, w, s):
    d, hh = cfg.d_model, cfg.heads
    cos, sin = _rope_tables(positions)
    ones = jnp.ones((d,), F32)

    h1, ffn0 = _ffn_fwd(x, s["ffn_norm"][0, 0], w["ffn_w_in"][0][0], w["ffn_w_out"][0][0], "ffn00")
    hn_a = _rms_fwd(h1, s["mix_norm"][0], name="a_norm")
    qkv = _fwd(hn_a, w["a_w_qkv"], name="a_qkv")
    kinds_a = ["rope", "rope", "cast"] * len(DILATIONS)
    gains_a = jnp.stack([z for g in range(len(DILATIONS)) for z in (
        _head_gain(s["a_q_norm"][g], hh, Q_SCALE), _head_gain(s["a_k_norm"][g], hh), ones)])
    qkvp = [_hn_fwd(qkv, gains_a[3 * g:3 * g + 3], kinds_a[:3], d, cos, sin, name=f"a_qk_norm{g}", col0=3 * g)
            for g in range(len(DILATIONS))]
    band = [_band_fwd(qkvp[g], dil, cfg, name=f"a_band{g}") for g, dil in enumerate(DILATIONS)]
    mixed, lse_a = _mix_fwd([o for o, _ in band], [l for _, l in band], name="a_mix")
    h2 = _fwd(mixed, w["a_w_o"], name="a_out", res=h1)
    h3, ffn1 = _ffn_fwd(h2, s["ffn_norm"][0, 1], w["ffn_w_in"][0][1], w["ffn_w_out"][0][1], "ffn01")

    kn = _rms_fwd(h3, s["kv_norm"], name="kv_norm")
    proj = _fwd(kn, w["kv_w"], name="kv_proj")
    kinds_kv = ["norm", "cast"]
    gains_kv = jnp.stack([_head_gain(s["kv_k_norm"], hh), ones])
    kvp = _hn_fwd(proj, gains_kv, kinds_kv, d, cos, sin, name="kv_k_norm")
    gate_col = 2 * d // LANES
    bias = jnp.pad(s["kv_b_f"].astype(F32), (0, LANES - hh))
    cum = _gate_fwd(proj, gate_col, bias, cfg, name="kv_gate")
    k_b, v_b = kvp[:, :d], kvp[:, d:]
    k_aug = _fox_aug_k(k_b, cum[:, :hh], cfg)

    h4, ffn2 = _ffn_fwd(h3, s["ffn_norm"][1, 0], w["ffn_w_in"][1][0], w["ffn_w_out"][1][0], "ffn10")
    hn_b = _rms_fwd(h4, s["mix_norm"][1], name="b_norm")
    qraw = _fwd(hn_b, w["b_w_q"], name="b_q")
    gains_b = _head_gain(s["b_q_norm"][0], hh, Q_SCALE)[None]
    qp = _hn_fwd(qraw, gains_b, ["norm"], d, cos, sin, name="b_q_norm")
    qa_t = _fox_aug_q(qp, cfg)
    o_t, lse_b = _fox_fwd_t(qa_t, k_aug, v_b.T, cfg, name="b_fox")
    o_b = o_t.T
    h5 = _fwd(o_b, w["b_w_o"], name="b_out", res=h4)
    h6, ffn3 = _ffn_fwd(h5, s["ffn_norm"][1, 1], w["ffn_w_in"][1][1], w["ffn_w_out"][1][1], "ffn11")

    loss, dh6 = _loss_fwd_bwd(h6, target, name="loss")

    dh5, dg11, dwi11, dwo11 = _ffn_bwd(dh6, h5, s["ffn_norm"][1, 1], w["ffn_w_in"][1][1], w["ffn_w_out"][1][1],
                                       ffn3, "ffn11")
    do_b = _bwd(dh5, w["b_w_o"], name="b_out_dx")
    dw_bo = _wgrad(o_b, dh5, w["b_w_o"], name="b_out_dw")
    do_bf = do_b.astype(BF16)
    do_t = do_bf.T
    dsum_b = _head_dot_t(do_t, o_t, cfg, name="b_dsum")
    dq4, dk_aug, dv_b, dcum = _fox_bwd_t(qa_t, k_aug, k_b.T, v_b, do_t, do_bf, lse_b, dsum_b, cfg,
                                         name="b_fox_bwd")
    dq_b = dq4.transpose(0, 1, 3, 2).reshape(cfg.tokens, d)
    dk_b = dk_aug.reshape(cfg.tokens, hh, AUG)[:, :, :HEAD_DIM].reshape(cfg.tokens, d)
    dqraw, dgq = _hn_bwd(qraw, dq_b, gains_b, ["norm"], d, cos, sin, name="b_q_norm_bwd")
    dhn_b = _bwd(dqraw, w["b_w_q"], name="b_q_dx")
    dw_bq = _wgrad(hn_b, dqraw, w["b_w_q"], name="b_q_dw")
    dh4, dmix1 = _rms_bwd(h4, s["mix_norm"][1], dhn_b, dh5, name="b_norm_bwd")
    dh3, dg10, dwi10, dwo10 = _ffn_bwd(dh4, h3, s["ffn_norm"][1, 0], w["ffn_w_in"][1][0], w["ffn_w_out"][1][0],
                                       ffn2, "ffn10")

    dkvraw, dgk = _hn_bwd(proj, jnp.concatenate([dk_b, dv_b], axis=1), gains_kv, kinds_kv, d, cos, sin,
                          name="kv_k_norm_bwd")
    dz, dbias = _gate_bwd(proj, gate_col, bias, dcum, cfg, name="kv_gate_bwd")
    pad_cols = w["kv_w"].shape[2] - 2 * d - LANES
    dproj = jnp.concatenate([dkvraw, dz.astype(BF16), jnp.zeros((cfg.tokens, pad_cols), BF16)], axis=1)
    dkn = _bwd(dproj, w["kv_w"], name="kv_proj_dx")
    dw_kv = _wgrad(kn, dproj, w["kv_w"], name="kv_proj_dw")
    dh3, dkvn = _rms_bwd(h3, s["kv_norm"], dkn, dh3, name="kv_norm_bwd")

    dh2, dg01, dwi01, dwo01 = _ffn_bwd(dh3, h2, s["ffn_norm"][0, 1], w["ffn_w_in"][0][1], w["ffn_w_out"][0][1],
                                       ffn1, "ffn01")
    dmixed = _bwd(dh2, w["a_w_o"], name="a_out_dx")
    dw_ao = _wgrad(mixed, dh2, w["a_w_o"], name="a_out_dw")
    dsum_a = _head_dot(dmixed, mixed, name="a_dsum")
    dqkvp = []
    for g, dil in enumerate(DILATIONS):
        dqkvp += _band_bwd(qkvp[g], dmixed, lse_a, dsum_a, dil, cfg, name=f"a_band{g}_bwd")
    dqkv, dga = _hn_bwd(qkv, jnp.concatenate(dqkvp, axis=1), gains_a, kinds_a, d, cos, sin, name="a_qk_norm_bwd")
    dhn_a = _bwd(dqkv, w["a_w_qkv"], name="a_qkv_dx")
    dw_qkv = _wgrad(hn_a, dqkv, w["a_w_qkv"], name="a_qkv_dw")
    dh1, dmix0 = _rms_bwd(h1, s["mix_norm"][0], dhn_a, dh2, name="a_norm_bwd")
    dx, dg00, dwi00, dwo00 = _ffn_bwd(dh1, x, s["ffn_norm"][0, 0], w["ffn_w_in"][0][0], w["ffn_w_out"][0][0],
                                      ffn0, "ffn00")

    dw = {
        "ffn_w_in": [[dwi00, dwi01], [dwi10, dwi11]],
        "ffn_w_out": [[dwo00, dwo01], [dwo10, dwo11]],
        "a_w_qkv": dw_qkv, "a_w_o": dw_ao, "kv_w": dw_kv, "b_w_q": dw_bq, "b_w_o": dw_bo,
    }
    ds = {
        "ffn_norm": jnp.stack([jnp.stack([dg00, dg01]), jnp.stack([dg10, dg11])]),
        "mix_norm": jnp.stack([dmix0, dmix1]),
        "a_q_norm": jnp.stack([dga[3 * g] for g in range(len(DILATIONS))])[None] * Q_SCALE,
        "a_k_norm": jnp.stack([dga[3 * g + 1] for g in range(len(DILATIONS))])[None],
        "kv_norm": dkvn,
        "kv_b_f": dbias[:hh],
        "kv_k_norm": dgk[0],
        "b_q_norm": dgq * Q_SCALE,
    }
    return loss, dx, dw, ds


MESH_ID = pl.DeviceIdType.MESH
ANY = pl.BlockSpec(memory_space=pl.ANY)
PACK_COLS = 1024
PACK_ROW_ALIGN = 32


def _me():
    return lax.axis_index("x"), lax.axis_index("y"), lax.axis_index("c")


def _other_chips(x, y):
    return [(1 - x, y), (x, 1 - y), (1 - x, 1 - y)]


def _all_gather_small(v, *, name):
    r = v.shape[0]

    def body(v_ref, out_ref, send_sems, recv_sems):
        x, y, c = _me()
        me = 4 * x + 2 * y + c
        out_ref[me] = v_ref[...]
        copies = []
        for k in range(1, N_DEV):
            fx, fy, fc = (k >> 2) & 1, (k >> 1) & 1, k & 1
            peer = (1 - x if fx else x, 1 - y if fy else y, 1 - c if fc else c)
            copies.append(pltpu.make_async_remote_copy(
                src_ref=v_ref, dst_ref=out_ref.at[me], send_sem=send_sems.at[k - 1], recv_sem=recv_sems.at[k - 1],
                device_id=peer, device_id_type=MESH_ID))
        for cp in copies:
            cp.start()
        for cp in copies:
            cp.wait()

    return pl.pallas_call(
        body, name=name,
        in_specs=[pl.BlockSpec(memory_space=pltpu.VMEM)], out_specs=pl.BlockSpec(memory_space=pltpu.VMEM),
        out_shape=jax.ShapeDtypeStruct((N_DEV, r, LANES), v.dtype),
        scratch_shapes=[pltpu.SemaphoreType.DMA((N_DEV - 1,)), pltpu.SemaphoreType.DMA((N_DEV - 1,))],
    )(v)


def _all_gather_chips(v, *, name):
    rh = v.shape[0] // 2

    def body(v_ref, out_ref, send_sems, recv_sems, local_sem):
        x, y, c = _me()
        j = 2 * x + y
        chips = _other_chips(x, y)
        local = pltpu.make_async_copy(v_ref, out_ref.at[j], local_sem)
        local.start()

        def half(chip, core):
            return out_ref.at[chip, pl.ds(core * rh, rh)]

        first = [pltpu.make_async_remote_copy(
            src_ref=v_ref.at[pl.ds(c * rh, rh)], dst_ref=half(j, c), send_sem=send_sems.at[k],
            recv_sem=recv_sems.at[k], device_id=(px, py, c), device_id_type=MESH_ID)
            for k, (px, py) in enumerate(chips)]
        for cp in first:
            cp.start()
        passed = [pltpu.make_async_remote_copy(
            src_ref=half(2 * px + py, c), dst_ref=half(2 * px + py, c), send_sem=send_sems.at[3 + k],
            recv_sem=recv_sems.at[3 + k], device_id=(x, y, 1 - c), device_id_type=MESH_ID)
            for k, (px, py) in enumerate(chips)]
        for k in range(len(chips)):
            first[k].wait_recv()
            passed[k].start()
        for k, (px, py) in enumerate(chips):
            pltpu.make_async_remote_copy(
                src_ref=half(2 * px + py, 1 - c), dst_ref=half(2 * px + py, 1 - c), send_sem=send_sems.at[3 + k],
                recv_sem=recv_sems.at[3 + k], device_id=(x, y, 1 - c), device_id_type=MESH_ID).wait_recv()
        for cp in first + passed:
            cp.wait_send()
        local.wait()

    return pl.pallas_call(
        body, name=name, in_specs=[ANY], out_specs=ANY,
        out_shape=jax.ShapeDtypeStruct((N_CHIPS,) + v.shape, v.dtype),
        scratch_shapes=[pltpu.SemaphoreType.DMA((2 * (N_CHIPS - 1),)), pltpu.SemaphoreType.DMA((2 * (N_CHIPS - 1),)),
                        pltpu.SemaphoreType.DMA],
    )(v)


def _swap_halves(g, *, name):
    n, r, cols = g.shape
    rh = r // 2

    def body(g_ref, out_ref, send_sem, recv_sem):
        x, y, c = _me()
        cp = pltpu.make_async_remote_copy(
            src_ref=g_ref.at[:, pl.ds((1 - c) * rh, rh)], dst_ref=out_ref, send_sem=send_sem, recv_sem=recv_sem,
            device_id=(x, y, 1 - c), device_id_type=MESH_ID)
        cp.start()
        cp.wait()

    return pl.pallas_call(
        body, name=name, in_specs=[ANY], out_specs=ANY,
        out_shape=jax.ShapeDtypeStruct((n, rh, cols), g.dtype),
        scratch_shapes=[pltpu.SemaphoreType.DMA, pltpu.SemaphoreType.DMA],
    )(g)


def _scatter_chips(v, *, name):
    def body(v_ref, out_ref, send_sems, recv_sems, local_sem):
        x, y, c = _me()
        j = 2 * x + y
        local = pltpu.make_async_copy(v_ref.at[j], out_ref.at[j], local_sem)
        local.start()
        copies = [pltpu.make_async_remote_copy(
            src_ref=v_ref.at[2 * px + py], dst_ref=out_ref.at[j], send_sem=send_sems.at[k], recv_sem=recv_sems.at[k],
            device_id=(px, py, c), device_id_type=MESH_ID) for k, (px, py) in enumerate(_other_chips(x, y))]
        for cp in copies:
            cp.start()
        for cp in copies:
            cp.wait()
        local.wait()

    return pl.pallas_call(
        body, name=name, in_specs=[ANY], out_specs=ANY,
        out_shape=jax.ShapeDtypeStruct(v.shape, v.dtype),
        scratch_shapes=[pltpu.SemaphoreType.DMA((N_CHIPS - 1,)), pltpu.SemaphoreType.DMA((N_CHIPS - 1,)),
                        pltpu.SemaphoreType.DMA],
    )(v)


def _join_halves(v, *, name):
    def body(v_ref, out_ref, send_sem, recv_sem, local_sem):
        x, y, c = _me()
        for core in range(2):
            @pl.when(c == core)
            def _(core=core):
                local = pltpu.make_async_copy(v_ref, out_ref.at[core], local_sem)
                local.start()
                cp = pltpu.make_async_remote_copy(
                    src_ref=v_ref, dst_ref=out_ref.at[core], send_sem=send_sem, recv_sem=recv_sem,
                    device_id=(x, y, 1 - core), device_id_type=MESH_ID)
                cp.start()
                cp.wait()
                local.wait()

    return pl.pallas_call(
        body, name=name, in_specs=[ANY], out_specs=ANY,
        out_shape=jax.ShapeDtypeStruct((2,) + v.shape, v.dtype),
        scratch_shapes=[pltpu.SemaphoreType.DMA, pltpu.SemaphoreType.DMA, pltpu.SemaphoreType.DMA],
    )(v)


def _row_blk(rows, want):
    for b in range(min(rows, want) // SUBLANES * SUBLANES, 0, -SUBLANES):
        if rows % b == 0:
            return b
    return rows


def _add_own_half(g, got, *, name):
    n, r, cols = g.shape
    rh = r // 2
    tr = _row_blk(rh, 512)
    nb = rh // tr

    def body(c_ref, g_ref, got_ref, o_ref):
        del c_ref
        o_ref[...] = (g_ref[...] + got_ref[...]).astype(BF16)

    grid_spec = pltpu.PrefetchScalarGridSpec(
        num_scalar_prefetch=1, grid=(n, nb),
        in_specs=[pl.BlockSpec((None, tr, cols), lambda j, i, c: (j, c[0] * nb + i, 0)),
                  pl.BlockSpec((None, tr, cols), lambda j, i, c: (j, i, 0))],
        out_specs=pl.BlockSpec((None, tr, cols), lambda j, i, c: (j, i, 0)))
    return pl.pallas_call(
        body, name=name, grid_spec=grid_spec, out_shape=jax.ShapeDtypeStruct((n, rh, cols), BF16),
        compiler_params=_params(("parallel", "parallel")),
    )(lax.axis_index("c").astype(jnp.int32).reshape(1), g, got)


def _sum_parts(parts, *, name):
    n, r, cols = parts.shape
    tr = _row_blk(r, 512)

    def body(*refs):
        o_ref = refs[n]
        acc = refs[0][...].astype(F32)
        for p_ref in refs[1:n]:
            acc = acc + p_ref[...].astype(F32)
        o_ref[...] = acc

    return pl.pallas_call(
        body, name=name, grid=(r // tr,),
        in_specs=[pl.BlockSpec((None, tr, cols), functools.partial(lambda j, i: (j, i, 0), j)) for j in range(n)],
        out_specs=pl.BlockSpec((tr, cols), lambda i: (i, 0)),
        out_shape=jax.ShapeDtypeStruct((r, cols), F32),
        compiler_params=_params(("parallel",)),
    )(*([parts] * n))


def _adamw(w, m, v, g, *, name):
    shape = w.shape
    cols = shape[-1]
    w2, m2, v2, g2 = (z.reshape(-1, cols) for z in (w, m, v, g))
    rows = w2.shape[0]
    tr = _row_blk(rows, max(SUBLANES, (1 << 20) // (4 * cols)))

    def body(w_ref, m_ref, v_ref, g_ref, d_ref, nm_ref, nv_ref):
        gv = g_ref[...]
        nm = ADAM_B1 * m_ref[...] + (1.0 - ADAM_B1) * gv
        nv = ADAM_B2 * v_ref[...] + (1.0 - ADAM_B2) * jnp.square(gv)
        m_hat = nm / (1.0 - ADAM_B1 ** ADAM_STEP)
        v_hat = nv / (1.0 - ADAM_B2 ** ADAM_STEP)
        d_ref[...] = -ADAM_LR * (m_hat / (jnp.sqrt(v_hat) + ADAM_EPS) + ADAM_WD * w_ref[...])
        nm_ref[...] = nm
        nv_ref[...] = nv

    spec = pl.BlockSpec((tr, cols), lambda i: (i, 0))
    out = jax.ShapeDtypeStruct((rows, cols), F32)
    d, nm, nv = pl.pallas_call(
        body, name=name, grid=(rows // tr,), in_specs=[spec] * 4, out_specs=[spec] * 3, out_shape=[out] * 3,
        compiler_params=_params(("parallel",)),
    )(w2, m2, v2, g2)
    return d.reshape(shape), nm.reshape(shape), nv.reshape(shape)


def _pack_rows(size, cols, align):
    return -(-size // (cols * align)) * align


def _pack(arrs, lead, cols, align, total_align):
    lead_shape = arrs[0].shape[:lead]
    parts = []
    for a in arrs:
        flat = a.reshape(lead_shape + (-1,))
        size = flat.shape[-1]
        rows = _pack_rows(size, cols, align)
        flat = jnp.pad(flat, [(0, 0)] * lead + [(0, rows * cols - size)])
        parts.append(flat.reshape(lead_shape + (rows, cols)))
    total = sum(p.shape[lead] for p in parts)
    extra = -total % total_align
    if extra:
        parts.append(jnp.zeros(lead_shape + (extra, cols), parts[0].dtype))
    return jnp.concatenate(parts, axis=lead)


def _unpack(buf, shapes, lead, cols, align):
    lead_shape = buf.shape[:lead]
    out, row = [], 0
    for shp in shapes:
        size = 1
        for n in shp:
            size *= n
        rows = _pack_rows(size, cols, align)
        piece = lax.slice_in_dim(buf, row, row + rows, axis=lead).reshape(lead_shape + (-1,))
        out.append(piece[..., :size].reshape(lead_shape + tuple(shp)))
        row += rows
    return out


BIG = ("ffn_w_in", "ffn_w_out", "a_w_qkv", "a_w_o", "kv_w", "b_w_q", "b_w_o")
SMALL = ("ffn_norm", "mix_norm", "a_q_norm", "a_k_norm", "kv_norm", "kv_b_f", "kv_k_norm", "b_q_norm")
WEIGHTS = ("ffn_norm", "ffn_w_in", "ffn_w_out", "mix_norm", "a_w_qkv", "a_q_norm", "a_k_norm", "a_w_o",
           "kv_norm", "kv_w", "kv_b_f", "kv_k_norm", "b_w_q", "b_q_norm", "b_w_o")
GATE_PAD = 2 * LANES


def _stack_weights(sh, d):
    depth = sh["ffn_w_in"].shape[1]
    kv = sh["kv_w"].transpose(1, 0, 2).reshape(d, -1)
    kv = jnp.pad(kv, ((0, 0), (0, 2 * d + GATE_PAD - kv.shape[1])))
    return {
        "ffn_w_in": [[sh["ffn_w_in"][:, l, i] for i in range(2)] for l in range(depth)],
        "ffn_w_out": [[sh["ffn_w_out"][:, l, i].reshape(1, -1, d) for i in range(2)] for l in range(depth)],
        "a_w_qkv": sh["a_w_qkv"][:, 0],
        "a_w_o": sh["a_w_o"].reshape(1, d, d),
        "kv_w": kv[None],
        "b_w_q": sh["b_w_q"].reshape(1, d, d),
        "b_w_o": sh["b_w_o"].reshape(1, d, d),
    }


def _unstack_grads(dw, d, heads):
    def rows4(z):
        return z.reshape(N_CHIPS, -1, d)

    kv_cols = 2 * d + heads
    kv = dw["kv_w"][0][:, :kv_cols].reshape(d, N_CHIPS, kv_cols // N_CHIPS).transpose(1, 0, 2)
    return [
        jnp.stack([jnp.stack(row, axis=1) for row in dw["ffn_w_in"]], axis=1),
        jnp.stack([jnp.stack([rows4(z) for z in row], axis=1) for row in dw["ffn_w_out"]], axis=1),
        dw["a_w_qkv"][:, None],
        rows4(dw["a_w_o"])[:, None],
        kv,
        rows4(dw["b_w_q"])[:, None],
        rows4(dw["b_w_o"])[:, None],
    ]


def kernel(x, positions, ffn_norm, ffn_w_in, ffn_w_out, mix_norm, a_w_qkv, a_q_norm, a_k_norm, a_w_o, kv_norm, kv_w, kv_b_f, kv_k_norm, b_w_q, b_q_norm, b_w_o, loss_target, m_ffn_norm, m_ffn_w_in, m_ffn_w_out, m_mix_norm, m_a_w_qkv, m_a_q_norm, m_a_k_norm, m_a_w_o, m_kv_norm, m_kv_w, m_kv_b_f, m_kv_k_norm, m_b_w_q, m_b_q_norm, m_b_w_o, v_ffn_norm, v_ffn_w_in, v_ffn_w_out, v_mix_norm, v_a_w_qkv, v_a_q_norm, v_a_k_norm, v_a_w_o, v_kv_norm, v_kv_w, v_kv_b_f, v_kv_k_norm, v_b_w_q, v_b_q_norm, v_b_w_o):
    wts = dict(zip(WEIGHTS, (ffn_norm, ffn_w_in, ffn_w_out, mix_norm, a_w_qkv, a_q_norm, a_k_norm, a_w_o, kv_norm,
                             kv_w, kv_b_f, kv_k_norm, b_w_q, b_q_norm, b_w_o)))
    mom = dict(zip(WEIGHTS, (m_ffn_norm, m_ffn_w_in, m_ffn_w_out, m_mix_norm, m_a_w_qkv, m_a_q_norm, m_a_k_norm,
                             m_a_w_o, m_kv_norm, m_kv_w, m_kv_b_f, m_kv_k_norm, m_b_w_q, m_b_q_norm, m_b_w_o)))
    var = dict(zip(WEIGHTS, (v_ffn_norm, v_ffn_w_in, v_ffn_w_out, v_mix_norm, v_a_w_qkv, v_a_q_norm, v_a_k_norm,
                             v_a_w_o, v_kv_norm, v_kv_w, v_kv_b_f, v_kv_k_norm, v_b_w_q, v_b_q_norm, v_b_w_o)))
    batch, seq, d = x.shape
    cfg = Cfg(d_model=d, d_ff=ffn_w_out.shape[2] * N_CHIPS, seq=seq, batch=batch)
    chip = 2 * lax.axis_index("x") + lax.axis_index("y")
    big_shapes = [wts[n].shape for n in BIG]

    shard = _pack([wts[n].astype(BF16) for n in BIG], 0, PACK_COLS, PACK_ROW_ALIGN, PACK_COLS)
    gathered = _all_gather_chips(shard, name="gather_weights")
    w = _stack_weights(dict(zip(BIG, _unpack(gathered, big_shapes, 1, PACK_COLS, PACK_ROW_ALIGN))), d)
    norm_shard = _pack([ffn_norm], 0, LANES, SUBLANES, SUBLANES)
    norms = _all_gather_small(norm_shard, name="gather_ffn_norm")[0::2]
    (norms,) = _unpack(norms, [ffn_norm.shape], 1, LANES, SUBLANES)
    small = {"ffn_norm": jnp.moveaxis(norms, 0, 2).reshape(ffn_norm.shape[:2] + (d,)),
             "mix_norm": mix_norm, "a_q_norm": a_q_norm[0], "a_k_norm": a_k_norm[0], "kv_norm": kv_norm,
             "kv_b_f": kv_b_f, "kv_k_norm": kv_k_norm, "b_q_norm": b_q_norm}

    loss, dx, dw, ds = _local_step(cfg, x.reshape(cfg.tokens, d), positions.reshape(cfg.tokens),
                                   loss_target.reshape(cfg.tokens, d), w, small)
    loss = lax.psum(loss, ("x", "y", "c"))

    g = _pack(_unstack_grads(dw, d, cfg.heads), 1, PACK_COLS, PACK_ROW_ALIGN, PACK_COLS)
    chip_half = _add_own_half(g, _swap_halves(g, name="swap_halves"), name="add_halves")
    mine = _sum_parts(_scatter_chips(chip_half, name="scatter_chips"), name="sum_chips")
    g_big = _join_halves(mine, name="join_halves").reshape(g.shape[1:])
    grads = dict(zip(BIG, _unpack(g_big, big_shapes, 0, PACK_COLS, PACK_ROW_ALIGN)))

    small_shapes = [ds[n].shape for n in SMALL]
    parts = _all_gather_small(_pack([ds[n] for n in SMALL], 0, LANES, SUBLANES, SUBLANES), name="gather_small")
    g_small = dict(zip(SMALL, _unpack(_sum_parts(parts, name="sum_small"), small_shapes, 0, LANES, SUBLANES)))
    quarter = d // N_CHIPS
    g_small["ffn_norm"] = lax.dynamic_slice_in_dim(g_small["ffn_norm"], chip * quarter, quarter, axis=2)
    grads.update(g_small)

    delta, new_m, new_v = {}, {}, {}
    for n in BIG:
        delta[n], new_m[n], new_v[n] = _adamw(wts[n], mom[n], var[n], grads[n], name=f"adamw_{n}")
    packed = [_pack([z[n] for n in SMALL], 0, LANES, SUBLANES, SUBLANES) for z in (wts, mom, var, grads)]
    small_out = _adamw(*packed, name="adamw_small")
    shard_shapes = [wts[n].shape for n in SMALL]
    for out, res in zip((delta, new_m, new_v), small_out):
        out.update(zip(SMALL, _unpack(res, shard_shapes, 0, LANES, SUBLANES)))

    return (loss, dx.reshape(x.shape), *[grads[n] for n in WEIGHTS], *[delta[n] for n in WEIGHTS],
            *[new_m[n] for n in WEIGHTS], *[new_v[n] for n in WEIGHTS])
```

```python
import functools
from typing import NamedTuple

import jax
import jax.numpy as jnp
from jax import lax
from jax.experimental import pallas as pl
from jax.experimental.pallas import tpu as pltpu

F32 = jnp.float32
BF16 = jnp.bfloat16

HEAD_DIM = 64
LANES = 128
SUBLANES = 8
ROT_DIM = HEAD_DIM // 4
ROPE_THETA = 500000.0
NORM_EPS = 1e-6
BAND = 128
DILATIONS = (1, 4, 16)
NEG = -1e30
Q_SCALE = HEAD_DIM ** -0.5
N_CHIPS = 4
N_DEV = 8
VMEM_LIMIT = 48 * 1024 * 1024

ADAM_LR = 0.001
ADAM_B1 = 0.9
ADAM_B2 = 0.999
ADAM_EPS = 1e-08
ADAM_WD = 0.01
ADAM_STEP = 10


class Cfg(NamedTuple):
    d_model: int
    d_ff: int
    seq: int
    batch: int

    @property
    def heads(self):
        return self.d_model // HEAD_DIM

    @property
    def tokens(self):
        return self.batch * self.seq

    @property
    def pairs(self):
        return self.d_model // LANES


def _params(sem):
    return pltpu.CompilerParams(dimension_semantics=sem, vmem_limit_bytes=VMEM_LIMIT)


def _blk(dim, want):
    if dim <= want:
        return dim
    for b in range(want // LANES * LANES, 0, -LANES):
        if dim % b == 0:
            return b
    b = want
    while dim % b:
        b //= 2
    return b


def _mm(a, b, *, form, out_dtype, name, bm=1024, bn=1024, bk=1024, res=None, scale=1.0):
    if form == "F":
        m, kdim = a.shape
        jn, _, ns = b.shape
        bm, bn, bk = _blk(m, bm), _blk(ns, bn), _blk(kdim, bk)
        npj = ns // bn
        grid = (m // bm, jn * npj, kdim // bk)
        a_spec = pl.BlockSpec((bm, bk), lambda i, n, k: (i, k))
        b_spec = pl.BlockSpec((None, bk, bn), lambda i, n, k: (n // npj, k, n % npj))
        o_spec = pl.BlockSpec((bm, bn), lambda i, n, k: (i, n))
        o_shape = jax.ShapeDtypeStruct((m, jn * ns), out_dtype)
        dims = (((1,), (0,)), ((), ()))
    elif form == "B":
        m = a.shape[0]
        jn, kdim, ns = b.shape
        bm, bn, bk = _blk(m, bm), _blk(kdim, bn), _blk(ns, bk)
        kpj = ns // bk
        grid = (m // bm, kdim // bn, jn * kpj)
        a_spec = pl.BlockSpec((bm, bk), lambda i, n, k: (i, k))
        b_spec = pl.BlockSpec((None, bn, bk), lambda i, n, k: (k // kpj, n, k % kpj))
        o_spec = pl.BlockSpec((bm, bn), lambda i, n, k: (i, n))
        o_shape = jax.ShapeDtypeStruct((m, kdim), out_dtype)
        dims = (((1,), (1,)), ((), ()))
    else:
        raise ValueError(form)
    nk = grid[2]

    def body(*refs):
        if res is None:
            a_ref, b_ref, o_ref, acc_ref = refs
            r_ref = None
        else:
            a_ref, b_ref, r_ref, o_ref, acc_ref = refs
        k = pl.program_id(2)

        @pl.when(k == 0)
        def _():
            acc_ref[...] = jnp.zeros_like(acc_ref)

        acc_ref[...] += lax.dot_general(a_ref[...].astype(BF16), b_ref[...].astype(BF16), dims,
                                        preferred_element_type=F32)

        @pl.when(k == nk - 1)
        def _():
            r = acc_ref[...]
            if scale != 1.0:
                r = r * scale
            if r_ref is not None:
                r = r_ref[...] + r
            o_ref[...] = r.astype(o_ref.dtype)

    in_specs = [a_spec, b_spec]
    args = [a, b]
    if res is not None:
        in_specs.append(pl.BlockSpec((bm, bn), lambda i, n, k: (i, n)))
        args.append(res)
    return pl.pallas_call(
        body, name=name, grid=grid, in_specs=in_specs, out_specs=o_spec, out_shape=o_shape,
        scratch_shapes=[pltpu.VMEM((bm, bn), F32)],
        compiler_params=_params(("parallel", "parallel", "arbitrary")),
    )(*args)


def _mm_grad(a, dy, jn, *, name, scale=1.0, bm=1024, bn=1024, bk=1024):
    t, kdim = a.shape
    ns = dy.shape[1] // jn
    bm, bn, bk = _blk(kdim, bm), _blk(ns, bn), _blk(t, bk)
    npj = ns // bn
    grid = (kdim // bm, jn * npj, t // bk)
    nk = grid[2]
    dims = (((0,), (0,)), ((), ()))

    def body(a_ref, b_ref, o_ref, acc_ref):
        k = pl.program_id(2)

        @pl.when(k == 0)
        def _():
            acc_ref[...] = jnp.zeros_like(acc_ref)

        acc_ref[...] += lax.dot_general(a_ref[...].astype(BF16), b_ref[...].astype(BF16), dims,
                                        preferred_element_type=F32)

        @pl.when(k == nk - 1)
        def _():
            r = acc_ref[...]
            if scale != 1.0:
                r = r * scale
            o_ref[...] = r

    return pl.pallas_call(
        body, name=name, grid=grid,
        in_specs=[pl.BlockSpec((bk, bm), lambda m, n, k: (k, m)),
                  pl.BlockSpec((bk, bn), lambda m, n, k: (k, n))],
        out_specs=pl.BlockSpec((None, bm, bn), lambda m, n, k: (n // npj, m, n % npj)),
        out_shape=jax.ShapeDtypeStruct((jn, kdim, ns), F32),
        scratch_shapes=[pltpu.VMEM((bm, bn), F32)],
        compiler_params=_params(("parallel", "parallel", "arbitrary")),
    )(a, dy)


ROW_BLOCK = 512


def _fold8(x):
    return jnp.sum(x.reshape(x.shape[0] // SUBLANES, SUBLANES, x.shape[1]), axis=0)


def _rms_fwd(x, g, *, name):
    t, d = x.shape
    tr = _blk(t, ROW_BLOCK)

    def body(x_ref, g_ref, o_ref):
        xv = x_ref[...]
        rstd = lax.rsqrt(jnp.mean(xv * xv, axis=-1, keepdims=True) + NORM_EPS)
        o_ref[...] = ((xv * rstd) * g_ref[...]).astype(BF16)

    return pl.pallas_call(
        body, name=name, grid=(t // tr,),
        in_specs=[pl.BlockSpec((tr, d), lambda i: (i, 0)), pl.BlockSpec((1, d), lambda i: (0, 0))],
        out_specs=pl.BlockSpec((tr, d), lambda i: (i, 0)),
        out_shape=jax.ShapeDtypeStruct((t, d), BF16),
        compiler_params=_params(("parallel",)),
    )(x, g.reshape(1, d))


def _rms_bwd(x, g, dy, dres, *, name):
    t, d = x.shape
    tr = _blk(t, ROW_BLOCK)

    def body(x_ref, g_ref, dy_ref, dres_ref, dx_ref, dg_ref):
        i = pl.program_id(0)
        xv = x_ref[...]
        rstd = lax.rsqrt(jnp.mean(xv * xv, axis=-1, keepdims=True) + NORM_EPS)
        xhat = xv * rstd
        dyv = dy_ref[...]
        dyg = dyv * g_ref[...]
        proj = jnp.mean(dyg * xhat, axis=-1, keepdims=True)
        dx_ref[...] = dres_ref[...] + rstd * (dyg - xhat * proj)

        @pl.when(i == 0)
        def _():
            dg_ref[...] = jnp.zeros_like(dg_ref)

        dg_ref[...] += _fold8(dyv * xhat)

    dx, dg = pl.pallas_call(
        body, name=name, grid=(t // tr,),
        in_specs=[pl.BlockSpec((tr, d), lambda i: (i, 0)), pl.BlockSpec((1, d), lambda i: (0, 0)),
                  pl.BlockSpec((tr, d), lambda i: (i, 0)), pl.BlockSpec((tr, d), lambda i: (i, 0))],
        out_specs=[pl.BlockSpec((tr, d), lambda i: (i, 0)), pl.BlockSpec((SUBLANES, d), lambda i: (0, 0))],
        out_shape=[jax.ShapeDtypeStruct((t, d), F32), jax.ShapeDtypeStruct((SUBLANES, d), F32)],
        compiler_params=_params(("arbitrary",)),
    )(x, g.reshape(1, d), dy, dres)
    return dx, jnp.sum(dg, axis=0)


def _swiglu_fwd(u, *, name):
    t, f2 = u.shape
    f = f2 // 2
    tr = _blk(t, 256)

    def body(g_ref, u_ref, o_ref):
        gv = g_ref[...]
        o_ref[...] = (gv * jax.nn.sigmoid(gv) * u_ref[...]).astype(BF16)

    return pl.pallas_call(
        body, name=name, grid=(t // tr,),
        in_specs=[pl.BlockSpec((tr, f), lambda i: (i, 0)), pl.BlockSpec((tr, f), lambda i: (i, 1))],
        out_specs=pl.BlockSpec((tr, f), lambda i: (i, 0)),
        out_shape=jax.ShapeDtypeStruct((t, f), BF16),
        compiler_params=_params(("parallel",)),
    )(u, u)


def _swiglu_bwd(u, da, *, name):
    t, f2 = u.shape
    f = f2 // 2
    tr = _blk(t, 256)

    def body(g_ref, u_ref, da_ref, o_ref):
        gv = g_ref[...]
        sg = jax.nn.sigmoid(gv)
        silu = gv * sg
        dav = da_ref[...]
        o_ref[:, :f] = (dav * u_ref[...] * (sg + silu * (1.0 - sg))).astype(BF16)
        o_ref[:, f:] = (dav * silu).astype(BF16)

    return pl.pallas_call(
        body, name=name, grid=(t // tr,),
        in_specs=[pl.BlockSpec((tr, f), lambda i: (i, 0)), pl.BlockSpec((tr, f), lambda i: (i, 1)),
                  pl.BlockSpec((tr, f), lambda i: (i, 0))],
        out_specs=pl.BlockSpec((tr, f2), lambda i: (i, 0)),
        out_shape=jax.ShapeDtypeStruct((t, f2), BF16),
        compiler_params=_params(("parallel",)),
    )(u, u, da)


def _loss_fwd_bwd(h, target, *, name):
    t, d = h.shape
    tr = _blk(t, ROW_BLOCK)

    def body(h_ref, t_ref, dh_ref, l_ref):
        i = pl.program_id(0)
        err = h_ref[...] - t_ref[...]
        dh_ref[...] = err * (1.0 / d)

        @pl.when(i == 0)
        def _():
            l_ref[...] = jnp.zeros_like(l_ref)

        l_ref[...] += _fold8(err * err)

    dh, part = pl.pallas_call(
        body, name=name, grid=(t // tr,),
        in_specs=[pl.BlockSpec((tr, d), lambda i: (i, 0)), pl.BlockSpec((tr, d), lambda i: (i, 0))],
        out_specs=[pl.BlockSpec((tr, d), lambda i: (i, 0)), pl.BlockSpec((SUBLANES, d), lambda i: (0, 0))],
        out_shape=[jax.ShapeDtypeStruct((t, d), F32), jax.ShapeDtypeStruct((SUBLANES, d), F32)],
        compiler_params=_params(("arbitrary",)),
    )(h, target)
    return jnp.sum(part) * (0.5 / d), dh


def _seg_matrix():
    r = lax.broadcasted_iota(jnp.int32, (LANES, LANES), 0) // HEAD_DIM
    c = lax.broadcasted_iota(jnp.int32, (LANES, LANES), 1) // HEAD_DIM
    return (r == c).astype(BF16)


def _head_sum(x, seg):
    hi = x.astype(BF16)
    r1 = x - hi.astype(F32)
    mid = r1.astype(BF16)
    lo = (r1 - mid.astype(F32)).astype(BF16)
    dot = functools.partial(jnp.dot, preferred_element_type=F32)
    return dot(hi, seg) + dot(mid, seg) + dot(lo, seg)


def _lane_in_head(shape):
    return lax.broadcasted_iota(jnp.int32, shape, 1) % HEAD_DIM


def _rot_partner(x):
    up = pltpu.roll(x, LANES - ROT_DIM // 2, 1)
    down = pltpu.roll(x, ROT_DIM // 2, 1)
    return jnp.where(_lane_in_head(x.shape) < ROT_DIM // 2, up, down)


def _rope_tables(positions):
    inv_freq = ROPE_THETA ** (-jnp.arange(0, ROT_DIM, 2, dtype=F32) / ROT_DIM)
    ang = positions.astype(F32)[:, None] * inv_freq
    t = ang.shape[0]
    rest = HEAD_DIM - ROT_DIM
    cos = jnp.concatenate([jnp.cos(ang), jnp.cos(ang), jnp.ones((t, rest), F32)], axis=1)
    sin = jnp.concatenate([-jnp.sin(ang), jnp.sin(ang), jnp.zeros((t, rest), F32)], axis=1)
    return jnp.tile(cos, (1, LANES // HEAD_DIM)), jnp.tile(sin, (1, LANES // HEAD_DIM))


def _kind_is(j, kinds, kind):
    hits = [j == jj for jj, k in enumerate(kinds) if k == kind]
    return functools.reduce(jnp.logical_or, hits) if hits else None


def _hn_fwd(x, gains, kinds, d, cos, sin, *, name, col0=0):
    t = x.shape[0]
    n = len(kinds)
    tr = _blk(t, ROW_BLOCK)
    seg = _seg_matrix()
    g8 = jnp.repeat(gains.astype(F32), SUBLANES, axis=0)

    def body(x_ref, g_ref, seg_ref, cos_ref, sin_ref, o_ref):
        j = pl.program_id(1)

        def normed(rope):
            for c in range(d // LANES):
                sl = slice(c * LANES, (c + 1) * LANES)
                xv = x_ref[:, sl]
                ms = _head_sum(xv * xv, seg_ref[...]) * (1.0 / HEAD_DIM)
                y = (xv * lax.rsqrt(ms + NORM_EPS)) * g_ref[0:1, sl]
                if rope:
                    y = y * cos_ref[...] + _rot_partner(y) * sin_ref[...]
                o_ref[:, sl] = y.astype(BF16)

        for kind in ("rope", "norm"):
            hit = _kind_is(j, kinds, kind)
            if hit is not None:
                pl.when(hit)(functools.partial(normed, kind == "rope"))
        hit = _kind_is(j, kinds, "cast")
        if hit is not None:
            @pl.when(hit)
            def _():
                o_ref[...] = x_ref[...].astype(BF16)

    return pl.pallas_call(
        body, name=name, grid=(t // tr, n),
        in_specs=[pl.BlockSpec((tr, d), lambda i, j: (i, col0 + j)), pl.BlockSpec((SUBLANES, d), lambda i, j: (j, 0)),
                  pl.BlockSpec((LANES, LANES), lambda i, j: (0, 0)),
                  pl.BlockSpec((tr, LANES), lambda i, j: (i, 0)), pl.BlockSpec((tr, LANES), lambda i, j: (i, 0))],
        out_specs=pl.BlockSpec((tr, d), lambda i, j: (i, j)),
        out_shape=jax.ShapeDtypeStruct((t, n * d), BF16),
        compiler_params=_params(("parallel", "parallel")),
    )(x, g8, seg, cos, sin)


def _hn_bwd(x, dy, gains, kinds, d, cos, sin, *, name):
    t = x.shape[0]
    n = len(kinds)
    tr = _blk(t, ROW_BLOCK)
    seg = _seg_matrix()
    g8 = jnp.repeat(gains.astype(F32), SUBLANES, axis=0)

    def body(x_ref, dy_ref, g_ref, seg_ref, cos_ref, sin_ref, dx_ref, dg_ref):
        j = pl.program_id(0)
        i = pl.program_id(1)

        @pl.when(i == 0)
        def _():
            dg_ref[...] = jnp.zeros_like(dg_ref)

        def normed(rope):
            for c in range(d // LANES):
                sl = slice(c * LANES, (c + 1) * LANES)
                xv = x_ref[:, sl]
                dyv = dy_ref[:, sl]
                if rope:
                    dyv = dyv * cos_ref[...] - _rot_partner(dyv) * sin_ref[...]
                ms = _head_sum(xv * xv, seg_ref[...]) * (1.0 / HEAD_DIM)
                rstd = lax.rsqrt(ms + NORM_EPS)
                xhat = xv * rstd
                dg_ref[:, sl] += _fold8(dyv * xhat)
                dyg = dyv * g_ref[0:1, sl]
                proj = _head_sum(dyg * xhat, seg_ref[...]) * (1.0 / HEAD_DIM)
                dx_ref[:, sl] = (rstd * (dyg - xhat * proj)).astype(BF16)

        for kind in ("rope", "norm"):
            hit = _kind_is(j, kinds, kind)
            if hit is not None:
                pl.when(hit)(functools.partial(normed, kind == "rope"))
        hit = _kind_is(j, kinds, "cast")
        if hit is not None:
            @pl.when(hit)
            def _():
                dx_ref[...] = dy_ref[...].astype(BF16)

    dx, dg = pl.pallas_call(
        body, name=name, grid=(n, t // tr),
        in_specs=[pl.BlockSpec((tr, d), lambda j, i: (i, j)), pl.BlockSpec((tr, d), lambda j, i: (i, j)),
                  pl.BlockSpec((SUBLANES, d), lambda j, i: (j, 0)),
                  pl.BlockSpec((LANES, LANES), lambda j, i: (0, 0)),
                  pl.BlockSpec((tr, LANES), lambda j, i: (i, 0)), pl.BlockSpec((tr, LANES), lambda j, i: (i, 0))],
        out_specs=[pl.BlockSpec((tr, d), lambda j, i: (i, j)), pl.BlockSpec((SUBLANES, d), lambda j, i: (j, 0))],
        out_shape=[jax.ShapeDtypeStruct((t, n * d), BF16), jax.ShapeDtypeStruct((n * SUBLANES, d), F32)],
        compiler_params=_params(("arbitrary", "arbitrary")),
    )(x, dy, g8, seg, cos, sin)
    dg = dg.reshape(n, SUBLANES, d // HEAD_DIM, HEAD_DIM).sum(axis=(1, 2))
    return dx, dg


def _head_dot(a, b, *, name):
    t, d = a.shape
    tr = _blk(t, ROW_BLOCK)
    seg = _seg_matrix()

    def body(a_ref, b_ref, seg_ref, o_ref):
        for c in range(d // LANES):
            sl = slice(c * LANES, (c + 1) * LANES)
            o_ref[:, sl] = _head_sum(a_ref[:, sl].astype(BF16).astype(F32) * b_ref[:, sl], seg_ref[...])

    return pl.pallas_call(
        body, name=name, grid=(t // tr,),
        in_specs=[pl.BlockSpec((tr, d), lambda i: (i, 0)), pl.BlockSpec((tr, d), lambda i: (i, 0)),
                  pl.BlockSpec((LANES, LANES), lambda i: (0, 0))],
        out_specs=pl.BlockSpec((tr, d), lambda i: (i, 0)),
        out_shape=jax.ShapeDtypeStruct((t, d), F32),
        compiler_params=_params(("parallel",)),
    )(a, b, seg)


def _half_mask(shape):
    return lax.broadcasted_iota(jnp.int32, shape, 1) < HEAD_DIM


def _band_valid(first):
    qi = lax.broadcasted_iota(jnp.int32, (BAND, 2 * BAND), 0)
    kj = lax.broadcasted_iota(jnp.int32, (BAND, 2 * BAND), 1)
    dist = qi + BAND - kj
    return (dist >= 0) & (dist <= BAND) & ((kj >= BAND) | jnp.logical_not(first))


_NT = (((1,), (1,)), ((), ()))
_TN = (((0,), (0,)), ((), ()))


def _dot2(p, v):
    hi = p.astype(BF16)
    lo = (p - hi.astype(F32)).astype(BF16)
    return jnp.dot(hi, v, preferred_element_type=F32) + jnp.dot(lo, v, preferred_element_type=F32)


def _band_fwd(qkv, dil, cfg, *, name):
    t, d = cfg.tokens, cfg.d_model
    w = 3 * d
    rows = t // dil
    nbt = rows // BAND
    nb = cfg.seq // (dil * BAND)
    view = qkv.reshape(rows, dil * w)
    ncol = w // d

    def body(q_ref, kp_ref, kc_ref, vp_ref, vc_ref, o_ref, lse_ref):
        i = pl.program_id(1)
        valid = _band_valid(i % nb == 0)
        half = _half_mask((BAND, LANES))
        for hp in range(d // LANES):
            sl = slice(hp * LANES, (hp + 1) * LANES)
            q2 = q_ref[:, sl]
            kk = jnp.concatenate([kp_ref[:, sl], kc_ref[:, sl]], axis=0)
            vv = jnp.concatenate([vp_ref[:, sl], vc_ref[:, sl]], axis=0)
            outs, lses = [], []
            for e in range(2):
                qe = jnp.where(half == (e == 0), q2, jnp.zeros_like(q2))
                s = lax.dot_general(qe, kk, _NT, preferred_element_type=F32)
                s = jnp.where(valid, s, NEG)
                m = jnp.max(s, axis=1, keepdims=True)
                p = jnp.exp(s - m)
                l = jnp.sum(p, axis=1, keepdims=True)
                outs.append(_dot2(p * (1.0 / l), vv))
                lses.append(m + jnp.log(l))
            o_ref[:, sl] = jnp.where(half, outs[0], outs[1])
            lse_ref[:, sl] = jnp.where(half, lses[0], lses[1])

    def col(which):
        return lambda r, i: (i, r * ncol + which)

    def col_prev(which):
        return lambda r, i: (jnp.maximum(i - 1, 0), r * ncol + which)

    blk = (BAND, d)
    o, lse = pl.pallas_call(
        body, name=name, grid=(dil, nbt),
        in_specs=[pl.BlockSpec(blk, col(0)), pl.BlockSpec(blk, col_prev(1)), pl.BlockSpec(blk, col(1)),
                  pl.BlockSpec(blk, col_prev(2)), pl.BlockSpec(blk, col(2))],
        out_specs=[pl.BlockSpec(blk, lambda r, i: (i, r)), pl.BlockSpec(blk, lambda r, i: (i, r))],
        out_shape=[jax.ShapeDtypeStruct((rows, dil * d), F32), jax.ShapeDtypeStruct((rows, dil * d), F32)],
        compiler_params=_params(("parallel", "arbitrary")),
    )(view, view, view, view, view)
    return o.reshape(t, d), lse.reshape(t, d)


def _band_bwd(qkv, dmixed, lse_all, dsum, dil, cfg, *, name):
    t, d = cfg.tokens, cfg.d_model
    w = 3 * d
    rows = t // dil
    nbt = rows // BAND
    nb = cfg.seq // (dil * BAND)
    view = qkv.reshape(rows, dil * w)
    ncol = w // d
    do_v, l_v, d_v = (z.reshape(rows, dil * d) for z in (dmixed, lse_all, dsum))

    def body(q_ref, kp_ref, kc_ref, vp_ref, vc_ref, do_ref, l_ref, ds_ref, dq_ref, dk_ref, dv_ref, ck_ref, cv_ref):
        i = pl.program_id(1)

        @pl.when(i < nbt)
        def _():
            valid = _band_valid(i % nb == 0)
            half = _half_mask((BAND, LANES))
            half2 = _half_mask((2 * BAND, LANES))
            for hp in range(d // LANES):
                sl = slice(hp * LANES, (hp + 1) * LANES)
                q2 = q_ref[:, sl]
                kk = jnp.concatenate([kp_ref[:, sl], kc_ref[:, sl]], axis=0)
                vv = jnp.concatenate([vp_ref[:, sl], vc_ref[:, sl]], axis=0)
                do2 = do_ref[:, sl].astype(BF16)
                dqs, dks, dvs = [], [], []
                for e in range(2):
                    lane0 = e * HEAD_DIM
                    keep = half == (e == 0)
                    qe = jnp.where(keep, q2, jnp.zeros_like(q2))
                    doe = jnp.where(keep, do2, jnp.zeros_like(do2))
                    s = lax.dot_general(qe, kk, _NT, preferred_element_type=F32)
                    s = jnp.where(valid, s, NEG)
                    p = jnp.exp(s - l_ref[:, hp * LANES + lane0:hp * LANES + lane0 + 1])
                    dp = lax.dot_general(doe, vv, _NT, preferred_element_type=F32)
                    dsc = (p * (dp - ds_ref[:, hp * LANES + lane0:hp * LANES + lane0 + 1])).astype(BF16)
                    dqs.append(jnp.dot(dsc, kk, preferred_element_type=F32))
                    dks.append(lax.dot_general(dsc, q2, _TN, preferred_element_type=F32))
                    dvs.append(lax.dot_general(p.astype(BF16), do2, _TN, preferred_element_type=F32))
                dq_ref[:, sl] = jnp.where(half, dqs[0], dqs[1])
                dkk = jnp.where(half2, dks[0], dks[1])
                dvv = jnp.where(half2, dvs[0], dvs[1])

                @pl.when(i > 0)
                def _():
                    dk_ref[:, sl] = ck_ref[:, sl] + dkk[:BAND]
                    dv_ref[:, sl] = cv_ref[:, sl] + dvv[:BAND]

                ck_ref[:, sl] = dkk[BAND:]
                cv_ref[:, sl] = dvv[BAND:]

        @pl.when(i == nbt)
        def _():
            dk_ref[...] = ck_ref[...]
            dv_ref[...] = cv_ref[...]

    def cur(i):
        return jnp.minimum(i, nbt - 1)

    def col(which):
        return lambda r, i: (cur(i), r * ncol + which)

    def col_prev(which):
        return lambda r, i: (jnp.maximum(cur(i) - 1, 0), r * ncol + which)

    blk = (BAND, d)
    here = pl.BlockSpec(blk, lambda r, i: (cur(i), r))
    behind = pl.BlockSpec(blk, lambda r, i: (jnp.maximum(i - 1, 0), r))
    shape = jax.ShapeDtypeStruct((rows, dil * d), F32)
    dq, dk, dv = pl.pallas_call(
        body, name=name, grid=(dil, nbt + 1),
        in_specs=[pl.BlockSpec(blk, col(0)), pl.BlockSpec(blk, col_prev(1)), pl.BlockSpec(blk, col(1)),
                  pl.BlockSpec(blk, col_prev(2)), pl.BlockSpec(blk, col(2)), here, here, here],
        out_specs=[here, behind, behind],
        out_shape=[shape, shape, shape],
        scratch_shapes=[pltpu.VMEM(blk, F32), pltpu.VMEM(blk, F32)],
        compiler_params=_params(("arbitrary", "arbitrary")),
    )(view, view, view, view, view, do_v, l_v, d_v)
    return dq.reshape(t, d), dk.reshape(t, d), dv.reshape(t, d)


def _mix_fwd(outs, lses, *, name):
    t, d = outs[0].shape
    tr = _blk(t, ROW_BLOCK)
    ng = len(outs)

    def body(*refs):
        o_refs, l_refs = refs[:ng], refs[ng:2 * ng]
        mixed_ref, lse_ref = refs[2 * ng:]
        ls = [r[...] for r in l_refs]
        m = functools.reduce(jnp.maximum, ls)
        es = [jnp.exp(l - m) for l in ls]
        tot = functools.reduce(jnp.add, es)
        inv = 1.0 / tot
        mixed_ref[...] = functools.reduce(jnp.add, [(e * inv) * r[...] for e, r in zip(es, o_refs)])
        lse_ref[...] = m + jnp.log(tot)

    spec = pl.BlockSpec((tr, d), lambda i: (i, 0))
    return pl.pallas_call(
        body, name=name, grid=(t // tr,),
        in_specs=[spec] * (2 * ng), out_specs=[spec, spec],
        out_shape=[jax.ShapeDtypeStruct((t, d), F32), jax.ShapeDtypeStruct((t, d), F32)],
        compiler_params=_params(("parallel",)),
    )(*outs, *lses)


GATE_BLOCK = 256


def _tri(n, upper):
    r = lax.broadcasted_iota(jnp.int32, (n, n), 0)
    c = lax.broadcasted_iota(jnp.int32, (n, n), 1)
    return ((c >= r) if upper else (c <= r)).astype(BF16)


def _tri_dot(tri, x):
    hi = x.astype(BF16)
    r1 = x - hi.astype(F32)
    mid = r1.astype(BF16)
    lo = (r1 - mid.astype(F32)).astype(BF16)
    dot = functools.partial(jnp.dot, preferred_element_type=F32)
    return dot(tri, hi) + dot(tri, mid) + dot(tri, lo)


def _log_sigmoid(z):
    return jnp.minimum(z, 0.0) - jnp.log(1.0 + jnp.exp(-jnp.abs(z)))


def _gate_fwd(proj, col_block, bias, cfg, *, name):
    tr = _blk(cfg.seq, GATE_BLOCK)
    nblk = cfg.seq // tr

    def body(z_ref, b_ref, tri_ref, o_ref, carry_ref):
        i = pl.program_id(1)

        @pl.when(i == 0)
        def _():
            carry_ref[...] = jnp.zeros_like(carry_ref)

        logf = _log_sigmoid(z_ref[...] + b_ref[0:1, :])
        cum = _tri_dot(tri_ref[...], logf) + carry_ref[0:1, :]
        o_ref[...] = cum
        carry_ref[...] = jnp.broadcast_to(cum[tr - 1:tr, :], carry_ref.shape)

    return pl.pallas_call(
        body, name=name, grid=(cfg.batch, nblk),
        in_specs=[pl.BlockSpec((tr, LANES), lambda b, i: (b * nblk + i, col_block)),
                  pl.BlockSpec((SUBLANES, LANES), lambda b, i: (0, 0)),
                  pl.BlockSpec((tr, tr), lambda b, i: (0, 0))],
        out_specs=pl.BlockSpec((tr, LANES), lambda b, i: (b * nblk + i, 0)),
        out_shape=jax.ShapeDtypeStruct((cfg.tokens, LANES), F32),
        scratch_shapes=[pltpu.VMEM((SUBLANES, LANES), F32)],
        compiler_params=_params(("arbitrary", "arbitrary")),
    )(proj, jnp.broadcast_to(bias, (SUBLANES, LANES)), _tri(tr, upper=False))


def _gate_bwd(proj, col_block, bias, dcum, cfg, *, name):
    tr = _blk(cfg.seq, GATE_BLOCK)
    nblk = cfg.seq // tr

    def body(z_ref, b_ref, tri_ref, dc_ref, dz_ref, db_ref, carry_ref):
        b = pl.program_id(0)
        i = pl.program_id(1)

        @pl.when(i == 0)
        def _():
            carry_ref[...] = jnp.zeros_like(carry_ref)

        @pl.when((i == 0) & (b == 0))
        def _():
            db_ref[...] = jnp.zeros_like(db_ref)

        dcv = dc_ref[...]
        dlogf = _tri_dot(tri_ref[...], dcv) + carry_ref[0:1, :]
        carry_ref[...] = jnp.broadcast_to(dlogf[0:1, :], carry_ref.shape)
        dz = dlogf * jax.nn.sigmoid(-(z_ref[...] + b_ref[0:1, :]))
        dz_ref[...] = dz
        db_ref[...] += _fold8(dz)

    def rev(b, i):
        return (b * nblk + nblk - 1 - i, 0)

    dz, db = pl.pallas_call(
        body, name=name, grid=(cfg.batch, nblk),
        in_specs=[pl.BlockSpec((tr, LANES), lambda b, i: (b * nblk + nblk - 1 - i, col_block)),
                  pl.BlockSpec((SUBLANES, LANES), lambda b, i: (0, 0)),
                  pl.BlockSpec((tr, tr), lambda b, i: (0, 0)),
                  pl.BlockSpec((tr, LANES), rev)],
        out_specs=[pl.BlockSpec((tr, LANES), rev), pl.BlockSpec((SUBLANES, LANES), lambda b, i: (0, 0))],
        out_shape=[jax.ShapeDtypeStruct((cfg.tokens, LANES), F32), jax.ShapeDtypeStruct((SUBLANES, LANES), F32)],
        scratch_shapes=[pltpu.VMEM((SUBLANES, LANES), F32)],
        compiler_params=_params(("arbitrary", "arbitrary")),
    )(proj, jnp.broadcast_to(bias, (SUBLANES, LANES)), _tri(tr, upper=True), dcum)
    return dz, jnp.sum(db, axis=0)


FOX_BLOCK = 256


def _fox_scores(q2, k2, e, half, mask, cref, ck_row):
    qe = jnp.where(half == (e == 0), q2, jnp.zeros_like(q2))
    s = lax.dot_general(qe, k2, _NT, preferred_element_type=F32)
    return jnp.where(mask, s + (cref - ck_row), NEG)


def _causal(qi, ki, tq):
    r = lax.broadcasted_iota(jnp.int32, (tq, tq), 0) + qi * tq
    c = lax.broadcasted_iota(jnp.int32, (tq, tq), 1) + ki * tq
    return r >= c


def _fox_fwd(q, kv, cum_t, cfg, *, name):
    t, d, hrows = cfg.tokens, cfg.d_model, cum_t.shape[0]
    tq = _blk(cfg.seq, FOX_BLOCK)
    nq = cfg.seq // tq

    def body(q_ref, k_ref, v_ref, cq_ref, ck_ref, o_ref, lse_ref, m_ref, l_ref, acc_ref):
        qi, ki = pl.program_id(1), pl.program_id(2)

        @pl.when(ki == 0)
        def _():
            m_ref[...] = jnp.full_like(m_ref, NEG)
            l_ref[...] = jnp.zeros_like(l_ref)
            acc_ref[...] = jnp.zeros_like(acc_ref)

        @pl.when(ki <= qi)
        def _():
            mask = _causal(qi, ki, tq)
            half = _half_mask((tq, LANES))
            for hp in range(d // LANES):
                sl = slice(hp * LANES, (hp + 1) * LANES)
                q2, k2, v2 = q_ref[:, sl], k_ref[:, sl], v_ref[:, sl]
                alphas, pvs = [], []
                for e in range(2):
                    h = 2 * hp + e
                    s = _fox_scores(q2, k2, e, half, mask, cq_ref[h:h + 1, 0:1], ck_ref[h:h + 1, :])
                    m_prev = m_ref[h]
                    m_new = jnp.maximum(m_prev, jnp.max(s, axis=1, keepdims=True))
                    alpha = jnp.exp(m_prev - m_new)
                    p = jnp.exp(s - m_new[:, 0:1])
                    l_ref[h] = alpha * l_ref[h] + jnp.sum(p, axis=1, keepdims=True)
                    m_ref[h] = m_new
                    alphas.append(alpha)
                    pvs.append(_dot2(p, v2))
                acc = acc_ref[:, sl]
                acc_ref[:, sl] = jnp.where(half, alphas[0] * acc + pvs[0], alphas[1] * acc + pvs[1])

        @pl.when(ki == qi)
        def _():
            half = _half_mask((tq, LANES))
            for hp in range(d // LANES):
                sl = slice(hp * LANES, (hp + 1) * LANES)
                h0, h1 = 2 * hp, 2 * hp + 1
                inv = jnp.where(half, 1.0 / l_ref[h0], 1.0 / l_ref[h1])
                o_ref[:, sl] = acc_ref[:, sl] * inv
                lse0 = m_ref[h0] + jnp.log(l_ref[h0]) - cq_ref[h0:h0 + 1, 0:1]
                lse1 = m_ref[h1] + jnp.log(l_ref[h1]) - cq_ref[h1:h1 + 1, 0:1]
                lse_ref[:, sl] = jnp.where(half, lse0, lse1)

    def qrow(b, qi, ki):
        return (b * nq + qi, 0)

    def krow(b, qi, ki):
        return (b * nq + jnp.minimum(ki, qi), 0)

    o, lse = pl.pallas_call(
        body, name=name, grid=(cfg.batch, nq, nq),
        in_specs=[pl.BlockSpec((tq, d), qrow),
                  pl.BlockSpec((tq, d), krow),
                  pl.BlockSpec((tq, d), lambda b, qi, ki: (b * nq + jnp.minimum(ki, qi), 1)),
                  pl.BlockSpec((hrows, tq), lambda b, qi, ki: (0, b * nq + qi)),
                  pl.BlockSpec((hrows, tq), lambda b, qi, ki: (0, b * nq + jnp.minimum(ki, qi)))],
        out_specs=[pl.BlockSpec((tq, d), qrow), pl.BlockSpec((tq, d), qrow)],
        out_shape=[jax.ShapeDtypeStruct((t, d), F32), jax.ShapeDtypeStruct((t, d), F32)],
        scratch_shapes=[pltpu.VMEM((cfg.heads, tq, LANES), F32), pltpu.VMEM((cfg.heads, tq, LANES), F32),
                        pltpu.VMEM((tq, d), F32)],
        compiler_params=_params(("parallel", "parallel", "arbitrary")),
    )(q, kv, kv, cum_t, cum_t)
    return o, lse


def _fox_bwd_q(q, kv, cum_t, do, lse, dsum, cfg, *, name):
    t, d, hrows = cfg.tokens, cfg.d_model, cum_t.shape[0]
    tq = _blk(cfg.seq, FOX_BLOCK)
    nq = cfg.seq // tq

    def body(q_ref, k_ref, v_ref, cq_ref, ck_ref, do_ref, l_ref, ds_ref, dq_ref, acc_ref):
        qi, ki = pl.program_id(1), pl.program_id(2)

        @pl.when(ki == 0)
        def _():
            acc_ref[...] = jnp.zeros_like(acc_ref)

        @pl.when(ki <= qi)
        def _():
            mask = _causal(qi, ki, tq)
            half = _half_mask((tq, LANES))
            for hp in range(d // LANES):
                sl = slice(hp * LANES, (hp + 1) * LANES)
                q2, k2, v2 = q_ref[:, sl], k_ref[:, sl], v_ref[:, sl]
                do2 = do_ref[:, sl].astype(BF16)
                dqs = []
                for e in range(2):
                    h = 2 * hp + e
                    lane0 = hp * LANES + e * HEAD_DIM
                    cref = cq_ref[h:h + 1, 0:1]
                    s = _fox_scores(q2, k2, e, half, mask, cref, ck_ref[h:h + 1, :])
                    p = jnp.exp(s - (l_ref[:, lane0:lane0 + 1] + cref))
                    doe = jnp.where(half == (e == 0), do2, jnp.zeros_like(do2))
                    dp = lax.dot_general(doe, v2, _NT, preferred_element_type=F32)
                    dsc = (p * (dp - ds_ref[:, lane0:lane0 + 1])).astype(BF16)
                    dqs.append(jnp.dot(dsc, k2, preferred_element_type=F32))
                acc_ref[:, sl] += jnp.where(half, dqs[0], dqs[1])

        @pl.when(ki == qi)
        def _():
            dq_ref[...] = acc_ref[...]

    def qrow(b, qi, ki):
        return (b * nq + qi, 0)

    return pl.pallas_call(
        body, name=name, grid=(cfg.batch, nq, nq),
        in_specs=[pl.BlockSpec((tq, d), qrow),
                  pl.BlockSpec((tq, d), lambda b, qi, ki: (b * nq + jnp.minimum(ki, qi), 0)),
                  pl.BlockSpec((tq, d), lambda b, qi, ki: (b * nq + jnp.minimum(ki, qi), 1)),
                  pl.BlockSpec((hrows, tq), lambda b, qi, ki: (0, b * nq + qi)),
                  pl.BlockSpec((hrows, tq), lambda b, qi, ki: (0, b * nq + jnp.minimum(ki, qi))),
                  pl.BlockSpec((tq, d), qrow), pl.BlockSpec((tq, d), qrow), pl.BlockSpec((tq, d), qrow)],
        out_specs=pl.BlockSpec((tq, d), qrow),
        out_shape=jax.ShapeDtypeStruct((t, d), F32),
        scratch_shapes=[pltpu.VMEM((tq, d), F32)],
        compiler_params=_params(("parallel", "parallel", "arbitrary")),
    )(q, kv, kv, cum_t, cum_t, do, lse, dsum)


def _fox_bwd_kv(q, kv, cum_t, do, lse, dsum, cfg, *, name):
    t, d, hrows = cfg.tokens, cfg.d_model, cum_t.shape[0]
    tq = _blk(cfg.seq, FOX_BLOCK)
    nq = cfg.seq // tq

    def body(q_ref, k_ref, v_ref, cq_ref, ck_ref, do_ref, l_ref, ds_ref, dk_ref, dv_ref, dc_ref,
             kacc_ref, vacc_ref, cacc_ref):
        ki, qi = pl.program_id(1), pl.program_id(2)

        @pl.when(qi == 0)
        def _():
            kacc_ref[...] = jnp.zeros_like(kacc_ref)
            vacc_ref[...] = jnp.zeros_like(vacc_ref)
            cacc_ref[...] = jnp.zeros_like(cacc_ref)

        @pl.when(qi >= ki)
        def _():
            mask = _causal(qi, ki, tq)
            half = _half_mask((tq, LANES))
            for hp in range(d // LANES):
                sl = slice(hp * LANES, (hp + 1) * LANES)
                q2, k2, v2 = q_ref[:, sl], k_ref[:, sl], v_ref[:, sl]
                do2 = do_ref[:, sl].astype(BF16)
                dks, dvs = [], []
                for e in range(2):
                    h = 2 * hp + e
                    lane0 = hp * LANES + e * HEAD_DIM
                    cref = cq_ref[h:h + 1, 0:1]
                    s = _fox_scores(q2, k2, e, half, mask, cref, ck_ref[h:h + 1, :])
                    p = jnp.exp(s - (l_ref[:, lane0:lane0 + 1] + cref))
                    doe = jnp.where(half == (e == 0), do2, jnp.zeros_like(do2))
                    dp = lax.dot_general(doe, v2, _NT, preferred_element_type=F32)
                    dsf = p * (dp - ds_ref[:, lane0:lane0 + 1])
                    cacc_ref[h:h + 1, :] -= jnp.sum(dsf, axis=0, keepdims=True)
                    dks.append(lax.dot_general(dsf.astype(BF16), q2, _TN, preferred_element_type=F32))
                    dvs.append(lax.dot_general(p.astype(BF16), do2, _TN, preferred_element_type=F32))
                kacc_ref[:, sl] += jnp.where(half, dks[0], dks[1])
                vacc_ref[:, sl] += jnp.where(half, dvs[0], dvs[1])

        @pl.when(qi == nq - 1)
        def _():
            dk_ref[...] = kacc_ref[...]
            dv_ref[...] = vacc_ref[...]
            dc_ref[...] = cacc_ref[...]

    def qrow(b, ki, qi):
        return (b * nq + jnp.maximum(qi, ki), 0)

    def krow(b, ki, qi):
        return (b * nq + ki, 0)

    return pl.pallas_call(
        body, name=name, grid=(cfg.batch, nq, nq),
        in_specs=[pl.BlockSpec((tq, d), qrow),
                  pl.BlockSpec((tq, d), krow),
                  pl.BlockSpec((tq, d), lambda b, ki, qi: (b * nq + ki, 1)),
                  pl.BlockSpec((hrows, tq), lambda b, ki, qi: (0, b * nq + jnp.maximum(qi, ki))),
                  pl.BlockSpec((hrows, tq), lambda b, ki, qi: (0, b * nq + ki)),
                  pl.BlockSpec((tq, d), qrow), pl.BlockSpec((tq, d), qrow), pl.BlockSpec((tq, d), qrow)],
        out_specs=[pl.BlockSpec((tq, d), krow), pl.BlockSpec((tq, d), krow),
                   pl.BlockSpec((hrows, tq), lambda b, ki, qi: (0, b * nq + ki))],
        out_shape=[jax.ShapeDtypeStruct((t, d), F32), jax.ShapeDtypeStruct((t, d), F32),
                   jax.ShapeDtypeStruct((hrows, t), F32)],
        scratch_shapes=[pltpu.VMEM((tq, d), F32), pltpu.VMEM((tq, d), F32), pltpu.VMEM((hrows, tq), F32)],
        compiler_params=_params(("parallel", "parallel", "arbitrary")),
    )(q, kv, kv, cum_t, cum_t, do, lse, dsum)


AUG = LANES
BIAS_TERMS = 3


def _fox_aug_q(qp, cfg):
    t, hh = cfg.tokens, cfg.heads
    q3 = qp.reshape(t, hh, HEAD_DIM)
    ones = jnp.ones((t, hh, BIAS_TERMS), BF16)
    zeros = jnp.zeros((t, hh, AUG - HEAD_DIM - BIAS_TERMS), BF16)
    return jnp.concatenate([q3, ones, zeros], axis=2).reshape(t, hh * AUG).T


def _fox_aug_k(k, cum, cfg):
    t, hh = cfg.tokens, cfg.heads
    c = -cum
    hi = lax.reduce_precision(c, 8, 7)
    mid = lax.reduce_precision(c - hi, 8, 7)
    lo = c - hi - mid
    zeros = jnp.zeros((t, hh, AUG - HEAD_DIM - BIAS_TERMS), BF16)
    parts = [k.reshape(t, hh, HEAD_DIM)] + [z.astype(BF16)[..., None] for z in (hi, mid, lo)] + [zeros]
    return jnp.concatenate(parts, axis=2).reshape(t, hh * AUG)


def _keys_visible(tq):
    s = lax.broadcasted_iota(jnp.int32, (tq, tq), 0)
    t = lax.broadcasted_iota(jnp.int32, (tq, tq), 1)
    return s <= t


def _fox_fwd_t(qa_t, k_aug, v_t, cfg, *, name):
    t, d, hh = cfg.tokens, cfg.d_model, cfg.heads
    tq = _blk(cfg.seq, FOX_BLOCK)
    nq = cfg.seq // tq

    def body(qa_ref, ka_ref, vt_ref, o_ref, lse_ref, m_ref, l_ref, acc_ref):
        qi, ki = pl.program_id(1), pl.program_id(2)

        @pl.when(ki == 0)
        def _():
            m_ref[...] = jnp.full_like(m_ref, NEG)
            l_ref[...] = jnp.zeros_like(l_ref)
            acc_ref[...] = jnp.zeros_like(acc_ref)

        def step(diagonal):
            for h in range(hh):
                rows = slice(h * HEAD_DIM, (h + 1) * HEAD_DIM)
                s = jnp.dot(ka_ref[:, h * AUG:(h + 1) * AUG], qa_ref[h * AUG:(h + 1) * AUG, :],
                            preferred_element_type=F32)
                if diagonal:
                    s = jnp.where(_keys_visible(tq), s, NEG)
                m_prev = m_ref[h:h + 1, :]
                m_new = jnp.maximum(m_prev, jnp.max(s, axis=0, keepdims=True))
                alpha = jnp.exp(m_prev - m_new)
                p = jnp.exp(s - m_new)
                l_ref[h:h + 1, :] = alpha * l_ref[h:h + 1, :] + jnp.sum(p, axis=0, keepdims=True)
                m_ref[h:h + 1, :] = m_new
                hi = p.astype(BF16)
                lo = (p - hi.astype(F32)).astype(BF16)
                vt = vt_ref[rows, :]
                acc_ref[rows, :] = (alpha * acc_ref[rows, :] + jnp.dot(vt, hi, preferred_element_type=F32)
                                    + jnp.dot(vt, lo, preferred_element_type=F32))

        pl.when(ki < qi)(functools.partial(step, False))
        pl.when(ki == qi)(functools.partial(step, True))

        @pl.when(ki == qi)
        def _():
            for h in range(hh):
                rows = slice(h * HEAD_DIM, (h + 1) * HEAD_DIM)
                o_ref[rows, :] = acc_ref[rows, :] * (1.0 / l_ref[h:h + 1, :])
            lse_ref[...] = m_ref[...] + jnp.log(l_ref[...])

    def qcol(b, qi, ki):
        return (0, b * nq + qi)

    return pl.pallas_call(
        body, name=name, grid=(cfg.batch, nq, nq),
        in_specs=[pl.BlockSpec((hh * AUG, tq), qcol),
                  pl.BlockSpec((tq, hh * AUG), lambda b, qi, ki: (b * nq + jnp.minimum(ki, qi), 0)),
                  pl.BlockSpec((d, tq), lambda b, qi, ki: (0, b * nq + jnp.minimum(ki, qi)))],
        out_specs=[pl.BlockSpec((d, tq), qcol), pl.BlockSpec((hh, tq), qcol)],
        out_shape=[jax.ShapeDtypeStruct((d, t), F32), jax.ShapeDtypeStruct((hh, t), F32)],
        scratch_shapes=[pltpu.VMEM((hh, tq), F32), pltpu.VMEM((hh, tq), F32), pltpu.VMEM((d, tq), F32)],
        compiler_params=_params(("parallel", "parallel", "arbitrary")),
    )(qa_t, k_aug, v_t)


def _head_dot_t(a_t, b_t, cfg, *, name):
    t, d, hh = cfg.tokens, cfg.d_model, cfg.heads
    tc = _blk(t, 2 * ROW_BLOCK)

    def body(a_ref, b_ref, o_ref):
        for h in range(hh):
            rows = slice(h * HEAD_DIM, (h + 1) * HEAD_DIM)
            o_ref[h:h + 1, :] = jnp.sum(a_ref[rows, :].astype(F32) * b_ref[rows, :], axis=0, keepdims=True)

    return pl.pallas_call(
        body, name=name, grid=(t // tc,),
        in_specs=[pl.BlockSpec((d, tc), lambda i: (0, i)), pl.BlockSpec((d, tc), lambda i: (0, i))],
        out_specs=pl.BlockSpec((hh, tc), lambda i: (0, i)),
        out_shape=jax.ShapeDtypeStruct((hh, t), F32),
        compiler_params=_params(("parallel",)),
    )(a_t, b_t)


def _fox_bwd_t(qa_t, k_aug, k_t, v, do_t, do, lse, dsum, cfg, *, name):
    t, d, hh = cfg.tokens, cfg.d_model, cfg.heads
    tq = _blk(cfg.seq, FOX_BLOCK)
    nq = cfg.seq // tq

    def body(qa_ref, ka_ref, kt_ref, v_ref, dot_ref, do_ref, lse_ref, ds_ref, dq_hbm, dk_ref, dv_ref, dc_ref,
             dq_acc, sem):
        b, ki, qi = pl.program_id(0), pl.program_id(1), pl.program_id(2)
        qq = jnp.maximum(qi, ki)

        @pl.when((ki == 0) & (qi == 0))
        def _():
            dq_acc[...] = jnp.zeros_like(dq_acc)

        @pl.when(qi == 0)
        def _():
            dk_ref[...] = jnp.zeros_like(dk_ref)
            dv_ref[...] = jnp.zeros_like(dv_ref)
            dc_ref[...] = jnp.zeros_like(dc_ref)

        def step(diagonal):
            upper = lax.broadcasted_iota(jnp.int32, (LANES, tq), 0) < HEAD_DIM
            half = _half_mask((tq, LANES))
            for hp in range(hh // 2):
                pair = slice(hp * LANES, (hp + 1) * LANES)
                dvs = []
                for e in range(2):
                    h = 2 * hp + e
                    rows = slice(h * HEAD_DIM, (h + 1) * HEAD_DIM)
                    aug = slice(h * AUG, (h + 1) * AUG)
                    s = jnp.dot(ka_ref[:, aug], qa_ref[aug, :], preferred_element_type=F32)
                    if diagonal:
                        s = jnp.where(_keys_visible(tq), s, NEG)
                    p = jnp.exp(s - lse_ref[h:h + 1, :])
                    dot2 = dot_ref[pair, :]
                    dote = jnp.where(upper == (e == 0), dot2, jnp.zeros_like(dot2))
                    dp = jnp.dot(v_ref[:, pair], dote, preferred_element_type=F32)
                    dsf = p * (dp - ds_ref[h:h + 1, :])
                    dc_ref[:, h:h + 1] -= jnp.sum(dsf, axis=1, keepdims=True)
                    dsc = dsf.astype(BF16)
                    dvs.append(jnp.dot(p.astype(BF16), do_ref[:, pair], preferred_element_type=F32))
                    dk_ref[:, aug] += lax.dot_general(dsc, qa_ref[aug, :], _NT, preferred_element_type=F32)
                    dq_acc[qq, rows, :] += jnp.dot(kt_ref[rows, :], dsc, preferred_element_type=F32)
                dv_ref[:, pair] += jnp.where(half, dvs[0], dvs[1])

        pl.when(qi > ki)(functools.partial(step, False))
        pl.when(qi == ki)(functools.partial(step, True))

        @pl.when((ki == nq - 1) & (qi == nq - 1))
        def _():
            cp = pltpu.make_async_copy(dq_acc, dq_hbm.at[b], sem)
            cp.start()
            cp.wait()

    def qcol(b, ki, qi):
        return (0, b * nq + jnp.maximum(qi, ki))

    def krow(b, ki, qi):
        return (b * nq + ki, 0)

    return pl.pallas_call(
        body, name=name, grid=(cfg.batch, nq, nq),
        in_specs=[pl.BlockSpec((hh * AUG, tq), qcol),
                  pl.BlockSpec((tq, hh * AUG), krow),
                  pl.BlockSpec((d, tq), lambda b, ki, qi: (0, b * nq + ki)),
                  pl.BlockSpec((tq, d), krow),
                  pl.BlockSpec((d, tq), qcol),
                  pl.BlockSpec((tq, d), lambda b, ki, qi: (b * nq + jnp.maximum(qi, ki), 0)),
                  pl.BlockSpec((hh, tq), qcol), pl.BlockSpec((hh, tq), qcol)],
        out_specs=[pl.BlockSpec(memory_space=pl.ANY), pl.BlockSpec((tq, hh * AUG), krow),
                   pl.BlockSpec((tq, d), krow), pl.BlockSpec((tq, LANES), krow)],
        out_shape=[jax.ShapeDtypeStruct((cfg.batch, nq, d, tq), F32), jax.ShapeDtypeStruct((t, hh * AUG), F32),
                   jax.ShapeDtypeStruct((t, d), F32), jax.ShapeDtypeStruct((t, LANES), F32)],
        scratch_shapes=[pltpu.VMEM((nq, d, tq), F32), pltpu.SemaphoreType.DMA],
        compiler_params=_params(("arbitrary", "arbitrary", "arbitrary")),
    )(qa_t, k_aug, k_t, v, do_t, do, lse, dsum)


WIDE = 1536


def _fwd(a, w, *, name, res=None, scale=1.0):
    return _mm(a, w, form="F", out_dtype=F32, name=name, bn=WIDE, bk=WIDE, res=res, scale=scale)


def _bwd(dy, w, *, name, scale=1.0):
    return _mm(dy, w, form="B", out_dtype=F32, name=name, bn=WIDE, bk=WIDE, scale=scale)


def _wgrad(a, dy, w, *, name, scale=1.0):
    return _mm_grad(a, dy, w.shape[0], name=name, bm=WIDE, bn=WIDE, scale=scale)


def _ffn_fwd(h, g, w_in, w_out, tag):
    n = _rms_fwd(h, g, name=f"{tag}_norm")
    u = _fwd(n, w_in, name=f"{tag}_in")
    a = _swiglu_fwd(u, name=f"{tag}_act")
    return _fwd(a, w_out, name=f"{tag}_out", res=h, scale=0.5), (n, u, a)


def _ffn_bwd(dh_out, h, g, w_in, w_out, saved, tag):
    n, u, a = saved
    da = _bwd(dh_out, w_out, name=f"{tag}_out_dx", scale=0.5)
    dw_out = _wgrad(a, dh_out, w_out, name=f"{tag}_out_dw", scale=0.5)
    du = _swiglu_bwd(u, da, name=f"{tag}_act_bwd")
    dn = _bwd(du, w_in, name=f"{tag}_in_dx")
    dw_in = _wgrad(n, du, w_in, name=f"{tag}_in_dw")
    dh, dg = _rms_bwd(h, g, dn, dh_out, name=f"{tag}_norm_bwd")
    return dh, dg, dw_in, dw_out


def _head_gain(g, heads, scale=1.0):
    return jnp.tile(g.astype(F32) * scale, heads)


def _local_step(cfg, x, positions, target, w, s):
    d, hh = cfg.d_model, cfg.heads
    cos, sin = _rope_tables(positions)
    ones = jnp.ones((d,), F32)

    h1, ffn0 = _ffn_fwd(x, s["ffn_norm"][0, 0], w["ffn_w_in"][0][0], w["ffn_w_out"][0][0], "ffn00")
    hn_a = _rms_fwd(h1, s["mix_norm"][0], name="a_norm")
    qkv = _fwd(hn_a, w["a_w_qkv"], name="a_qkv")
    kinds_a = ["rope", "rope", "cast"] * len(DILATIONS)
    gains_a = jnp.stack([z for g in range(len(DILATIONS)) for z in (
        _head_gain(s["a_q_norm"][g], hh, Q_SCALE), _head_gain(s["a_k_norm"][g], hh), ones)])
    qkvp = [_hn_fwd(qkv, gains_a[3 * g:3 * g + 3], kinds_a[:3], d, cos, sin, name=f"a_qk_norm{g}", col0=3 * g)
            for g in range(len(DILATIONS))]
    band = [_band_fwd(qkvp[g], dil, cfg, name=f"a_band{g}") for g, dil in enumerate(DILATIONS)]
    mixed, lse_a = _mix_fwd([o for o, _ in band], [l for _, l in band], name="a_mix")
    h2 = _fwd(mixed, w["a_w_o"], name="a_out", res=h1)
    h3, ffn1 = _ffn_fwd(h2, s["ffn_norm"][0, 1], w["ffn_w_in"][0][1], w["ffn_w_out"][0][1], "ffn01")

    kn = _rms_fwd(h3, s["kv_norm"], name="kv_norm")
    proj = _fwd(kn, w["kv_w"], name="kv_proj")
    kinds_kv = ["norm", "cast"]
    gains_kv = jnp.stack([_head_gain(s["kv_k_norm"], hh), ones])
    kvp = _hn_fwd(proj, gains_kv, kinds_kv, d, cos, sin, name="kv_k_norm")
    gate_col = 2 * d // LANES
    bias = jnp.pad(s["kv_b_f"].astype(F32), (0, LANES - hh))
    cum = _gate_fwd(proj, gate_col, bias, cfg, name="kv_gate")
    k_b, v_b = kvp[:, :d], kvp[:, d:]
    k_aug = _fox_aug_k(k_b, cum[:, :hh], cfg)

    h4, ffn2 = _ffn_fwd(h3, s["ffn_norm"][1, 0], w["ffn_w_in"][1][0], w["ffn_w_out"][1][0], "ffn10")
    hn_b = _rms_fwd(h4, s["mix_norm"][1], name="b_norm")
    qraw = _fwd(hn_b, w["b_w_q"], name="b_q")
    gains_b = _head_gain(s["b_q_norm"][0], hh, Q_SCALE)[None]
    qp = _hn_fwd(qraw, gains_b, ["norm"], d, cos, sin, name="b_q_norm")
    qa_t = _fox_aug_q(qp, cfg)
    o_t, lse_b = _fox_fwd_t(qa_t, k_aug, v_b.T, cfg, name="b_fox")
    o_b = o_t.T
    h5 = _fwd(o_b, w["b_w_o"], name="b_out", res=h4)
    h6, ffn3 = _ffn_fwd(h5, s["ffn_norm"][1, 1], w["ffn_w_in"][1][1], w["ffn_w_out"][1][1], "ffn11")

    loss, dh6 = _loss_fwd_bwd(h6, target, name="loss")

    dh5, dg11, dwi11, dwo11 = _ffn_bwd(dh6, h5, s["ffn_norm"][1, 1], w["ffn_w_in"][1][1], w["ffn_w_out"][1][1],
                                       ffn3, "ffn11")
    do_b = _bwd(dh5, w["b_w_o"], name="b_out_dx")
    dw_bo = _wgrad(o_b, dh5, w["b_w_o"], name="b_out_dw")
    do_bf = do_b.astype(BF16)
    do_t = do_bf.T
    dsum_b = _head_dot_t(do_t, o_t, cfg, name="b_dsum")
    dq4, dk_aug, dv_b, dcum = _fox_bwd_t(qa_t, k_aug, k_b.T, v_b, do_t, do_bf, lse_b, dsum_b, cfg,
                                         name="b_fox_bwd")
    dq_b = dq4.transpose(0, 1, 3, 2).reshape(cfg.tokens, d)
    dk_b = dk_aug.reshape(cfg.tokens, hh, AUG)[:, :, :HEAD_DIM].reshape(cfg.tokens, d)
    dqraw, dgq = _hn_bwd(qraw, dq_b, gains_b, ["norm"], d, cos, sin, name="b_q_norm_bwd")
    dhn_b = _bwd(dqraw, w["b_w_q"], name="b_q_dx")
    dw_bq = _wgrad(hn_b, dqraw, w["b_w_q"], name="b_q_dw")
    dh4, dmix1 = _rms_bwd(h4, s["mix_norm"][1], dhn_b, dh5, name="b_norm_bwd")
    dh3, dg10, dwi10, dwo10 = _ffn_bwd(dh4, h3, s["ffn_norm"][1, 0], w["ffn_w_in"][1][0], w["ffn_w_out"][1][0],
                                       ffn2, "ffn10")

    dkvraw, dgk = _hn_bwd(proj, jnp.concatenate([dk_b, dv_b], axis=1), gains_kv, kinds_kv, d, cos, sin,
                          name="kv_k_norm_bwd")
    dz, dbias = _gate_bwd(proj, gate_col, bias, dcum, cfg, name="kv_gate_bwd")
    pad_cols = w["kv_w"].shape[2] - 2 * d - LANES
    dproj = jnp.concatenate([dkvraw, dz.astype(BF16), jnp.zeros((cfg.tokens, pad_cols), BF16)], axis=1)
    dkn = _bwd(dproj, w["kv_w"], name="kv_proj_dx")
    dw_kv = _wgrad(kn, dproj, w["kv_w"], name="kv_proj_dw")
    dh3, dkvn = _rms_bwd(h3, s["kv_norm"], dkn, dh3, name="kv_norm_bwd")

    dh2, dg01, dwi01, dwo01 = _ffn_bwd(dh3, h2, s["ffn_norm"][0, 1], w["ffn_w_in"][0][1], w["ffn_w_out"][0][1],
                                       ffn1, "ffn01")
    dmixed = _bwd(dh2, w["a_w_o"], name="a_out_dx")
    dw_ao = _wgrad(mixed, dh2, w["a_w_o"], name="a_out_dw")
    dsum_a = _head_dot(dmixed, mixed, name="a_dsum")
    dqkvp = []
    for g, dil in enumerate(DILATIONS):
        dqkvp += _band_bwd(qkvp[g], dmixed, lse_a, dsum_a, dil, cfg, name=f"a_band{g}_bwd")
    dqkv, dga = _hn_bwd(qkv, jnp.concatenate(dqkvp, axis=1), gains_a, kinds_a, d, cos, sin, name="a_qk_norm_bwd")
    dhn_a = _bwd(dqkv, w["a_w_qkv"], name="a_qkv_dx")
    dw_qkv = _wgrad(hn_a, dqkv, w["a_w_qkv"], name="a_qkv_dw")
    dh1, dmix0 = _rms_bwd(h1, s["mix_norm"][0], dhn_a, dh2, name="a_norm_bwd")
    dx, dg00, dwi00, dwo00 = _ffn_bwd(dh1, x, s["ffn_norm"][0, 0], w["ffn_w_in"][0][0], w["ffn_w_out"][0][0],
                                      ffn0, "ffn00")

    dw = {
        "ffn_w_in": [[dwi00, dwi01], [dwi10, dwi11]],
        "ffn_w_out": [[dwo00, dwo01], [dwo10, dwo11]],
        "a_w_qkv": dw_qkv, "a_w_o": dw_ao, "kv_w": dw_kv, "b_w_q": dw_bq, "b_w_o": dw_bo,
    }
    ds = {
        "ffn_norm": jnp.stack([jnp.stack([dg00, dg01]), jnp.stack([dg10, dg11])]),
        "mix_norm": jnp.stack([dmix0, dmix1]),
        "a_q_norm": jnp.stack([dga[3 * g] for g in range(len(DILATIONS))])[None] * Q_SCALE,
        "a_k_norm": jnp.stack([dga[3 * g + 1] for g in range(len(DILATIONS))])[None],
        "kv_norm": dkvn,
        "kv_b_f": dbias[:hh],
        "kv_k_norm": dgk[0],
        "b_q_norm": dgq * Q_SCALE,
    }
    return loss, dx, dw, ds


MESH_ID = pl.DeviceIdType.MESH
ANY = pl.BlockSpec(memory_space=pl.ANY)
PACK_COLS = 1024
PACK_ROW_ALIGN = 32


def _me():
    return lax.axis_index("x"), lax.axis_index("y"), lax.axis_index("c")


def _other_chips(x, y):
    return [(1 - x, y), (x, 1 - y), (1 - x, 1 - y)]


def _all_gather_small(v, *, name):
    r = v.shape[0]

    def body(v_ref, out_ref, send_sems, recv_sems):
        x, y, c = _me()
        me = 4 * x + 2 * y + c
        out_ref[me] = v_ref[...]
        copies = []
        for k in range(1, N_DEV):
            fx, fy, fc = (k >> 2) & 1, (k >> 1) & 1, k & 1
            peer = (1 - x if fx else x, 1 - y if fy else y, 1 - c if fc else c)
            copies.append(pltpu.make_async_remote_copy(
                src_ref=v_ref, dst_ref=out_ref.at[me], send_sem=send_sems.at[k - 1], recv_sem=recv_sems.at[k - 1],
                device_id=peer, device_id_type=MESH_ID))
        for cp in copies:
            cp.start()
        for cp in copies:
            cp.wait()

    return pl.pallas_call(
        body, name=name,
        in_specs=[pl.BlockSpec(memory_space=pltpu.VMEM)], out_specs=pl.BlockSpec(memory_space=pltpu.VMEM),
        out_shape=jax.ShapeDtypeStruct((N_DEV, r, LANES), v.dtype),
        scratch_shapes=[pltpu.SemaphoreType.DMA((N_DEV - 1,)), pltpu.SemaphoreType.DMA((N_DEV - 1,))],
    )(v)


def _all_gather_chips(v, *, name):
    rh = v.shape[0] // 2

    def body(v_ref, out_ref, send_sems, recv_sems):
        x, y, c = _me()
        j = 2 * x + y
        chips = _other_chips(x, y)

        def half(chip, core):
            return out_ref.at[chip, pl.ds(core * rh, rh)]

        first = [pltpu.make_async_remote_copy(
            src_ref=v_ref.at[pl.ds(c * rh, rh)], dst_ref=half(j, c), send_sem=send_sems.at[k],
            recv_sem=recv_sems.at[k], device_id=(px, py, c), device_id_type=MESH_ID)
            for k, (px, py) in enumerate(chips)]
        for cp in first:
            cp.start()
        passed = [pltpu.make_async_remote_copy(
            src_ref=half(2 * px + py, c), dst_ref=half(2 * px + py, c), send_sem=send_sems.at[3 + k],
            recv_sem=recv_sems.at[3 + k], device_id=(x, y, 1 - c), device_id_type=MESH_ID)
            for k, (px, py) in enumerate(chips)]
        for k in range(len(chips)):
            first[k].wait_recv()
            passed[k].start()
        for k, (px, py) in enumerate(chips):
            pltpu.make_async_remote_copy(
                src_ref=half(2 * px + py, 1 - c), dst_ref=half(2 * px + py, 1 - c), send_sem=send_sems.at[3 + k],
                recv_sem=recv_sems.at[3 + k], device_id=(x, y, 1 - c), device_id_type=MESH_ID).wait_recv()
        for cp in first + passed:
            cp.wait_send()

    return pl.pallas_call(
        body, name=name, in_specs=[ANY], out_specs=ANY,
        out_shape=jax.ShapeDtypeStruct((N_CHIPS,) + v.shape, v.dtype),
        scratch_shapes=[pltpu.SemaphoreType.DMA((2 * (N_CHIPS - 1),)), pltpu.SemaphoreType.DMA((2 * (N_CHIPS - 1),))],
    )(v)


def _swap_halves(g, *, name):
    n, r, cols = g.shape
    rh = r // 2

    def body(g_ref, out_ref, send_sem, recv_sem):
        x, y, c = _me()
        cp = pltpu.make_async_remote_copy(
            src_ref=g_ref.at[:, pl.ds((1 - c) * rh, rh)], dst_ref=out_ref, send_sem=send_sem, recv_sem=recv_sem,
            device_id=(x, y, 1 - c), device_id_type=MESH_ID)
        cp.start()
        cp.wait()

    return pl.pallas_call(
        body, name=name, in_specs=[ANY], out_specs=ANY,
        out_shape=jax.ShapeDtypeStruct((n, rh, cols), g.dtype),
        scratch_shapes=[pltpu.SemaphoreType.DMA, pltpu.SemaphoreType.DMA],
    )(g)


def _scatter_chips(v, *, name):
    def body(v_ref, out_ref, send_sems, recv_sems):
        x, y, c = _me()
        j = 2 * x + y
        copies = [pltpu.make_async_remote_copy(
            src_ref=v_ref.at[2 * px + py], dst_ref=out_ref.at[j], send_sem=send_sems.at[k], recv_sem=recv_sems.at[k],
            device_id=(px, py, c), device_id_type=MESH_ID) for k, (px, py) in enumerate(_other_chips(x, y))]
        for cp in copies:
            cp.start()
        for cp in copies:
            cp.wait()

    return pl.pallas_call(
        body, name=name, in_specs=[ANY], out_specs=ANY,
        out_shape=jax.ShapeDtypeStruct(v.shape, v.dtype),
        scratch_shapes=[pltpu.SemaphoreType.DMA((N_CHIPS - 1,)), pltpu.SemaphoreType.DMA((N_CHIPS - 1,))],
    )(v)


def _join_halves(v, *, name):
    def body(v_ref, out_ref, send_sem, recv_sem):
        x, y, c = _me()
        cp = pltpu.make_async_remote_copy(
            src_ref=v_ref, dst_ref=out_ref.at[c], send_sem=send_sem, recv_sem=recv_sem,
            device_id=(x, y, 1 - c), device_id_type=MESH_ID)
        cp.start()
        cp.wait()

    return pl.pallas_call(
        body, name=name, in_specs=[ANY], out_specs=ANY,
        out_shape=jax.ShapeDtypeStruct((2,) + v.shape, v.dtype),
        scratch_shapes=[pltpu.SemaphoreType.DMA, pltpu.SemaphoreType.DMA],
    )(v)


def _row_blk(rows, want):
    for b in range(min(rows, want) // SUBLANES * SUBLANES, 0, -SUBLANES):
        if rows % b == 0:
            return b
    return rows


def _add_own_half(g, got, *, name):
    n, r, cols = g.shape
    rh = r // 2
    tr = _row_blk(rh, 512)
    nb = rh // tr

    def body(c_ref, g_ref, got_ref, o_ref):
        del c_ref
        o_ref[...] = (g_ref[...] + got_ref[...]).astype(BF16)

    grid_spec = pltpu.PrefetchScalarGridSpec(
        num_scalar_prefetch=1, grid=(n, nb),
        in_specs=[pl.BlockSpec((None, tr, cols), lambda j, i, c: (j, c[0] * nb + i, 0)),
                  pl.BlockSpec((None, tr, cols), lambda j, i, c: (j, i, 0))],
        out_specs=pl.BlockSpec((None, tr, cols), lambda j, i, c: (j, i, 0)))
    return pl.pallas_call(
        body, name=name, grid_spec=grid_spec, out_shape=jax.ShapeDtypeStruct((n, rh, cols), BF16),
        compiler_params=_params(("parallel", "parallel")),
    )(lax.axis_index("c").astype(jnp.int32).reshape(1), g, got)


def _sum_parts(parts, *, name):
    n, r, cols = parts.shape
    tr = _row_blk(r, 512)

    def body(*refs):
        o_ref = refs[n]
        acc = refs[0][...].astype(F32)
        for p_ref in refs[1:n]:
            acc = acc + p_ref[...].astype(F32)
        o_ref[...] = acc

    return pl.pallas_call(
        body, name=name, grid=(r // tr,),
        in_specs=[pl.BlockSpec((None, tr, cols), functools.partial(lambda j, i: (j, i, 0), j)) for j in range(n)],
        out_specs=pl.BlockSpec((tr, cols), lambda i: (i, 0)),
        out_shape=jax.ShapeDtypeStruct((r, cols), F32),
        compiler_params=_params(("parallel",)),
    )(*([parts] * n))


def _adamw(w, m, v, g, *, name):
    shape = w.shape
    cols = shape[-1]
    w2, m2, v2, g2 = (z.reshape(-1, cols) for z in (w, m, v, g))
    rows = w2.shape[0]
    tr = _row_blk(rows, max(SUBLANES, (1 << 20) // (4 * cols)))

    def body(w_ref, m_ref, v_ref, g_ref, d_ref, nm_ref, nv_ref):
        gv = g_ref[...]
        nm = ADAM_B1 * m_ref[...] + (1.0 - ADAM_B1) * gv
        nv = ADAM_B2 * v_ref[...] + (1.0 - ADAM_B2) * jnp.square(gv)
        m_hat = nm / (1.0 - ADAM_B1 ** ADAM_STEP)
        v_hat = nv / (1.0 - ADAM_B2 ** ADAM_STEP)
        d_ref[...] = -ADAM_LR * (m_hat / (jnp.sqrt(v_hat) + ADAM_EPS) + ADAM_WD * w_ref[...])
        nm_ref[...] = nm
        nv_ref[...] = nv

    spec = pl.BlockSpec((tr, cols), lambda i: (i, 0))
    out = jax.ShapeDtypeStruct((rows, cols), F32)
    d, nm, nv = pl.pallas_call(
        body, name=name, grid=(rows // tr,), in_specs=[spec] * 4, out_specs=[spec] * 3, out_shape=[out] * 3,
        compiler_params=_params(("parallel",)),
    )(w2, m2, v2, g2)
    return d.reshape(shape), nm.reshape(shape), nv.reshape(shape)


def _pack_rows(size, cols, align):
    return -(-size // (cols * align)) * align


def _pack(arrs, lead, cols, align, total_align):
    lead_shape = arrs[0].shape[:lead]
    parts = []
    for a in arrs:
        flat = a.reshape(lead_shape + (-1,))
        size = flat.shape[-1]
        rows = _pack_rows(size, cols, align)
        flat = jnp.pad(flat, [(0, 0)] * lead + [(0, rows * cols - size)])
        parts.append(flat.reshape(lead_shape + (rows, cols)))
    total = sum(p.shape[lead] for p in parts)
    extra = -total % total_align
    if extra:
        parts.append(jnp.zeros(lead_shape + (extra, cols), parts[0].dtype))
    return jnp.concatenate(parts, axis=lead)


def _unpack(buf, shapes, lead, cols, align):
    lead_shape = buf.shape[:lead]
    out, row = [], 0
    for shp in shapes:
        size = 1
        for n in shp:
            size *= n
        rows = _pack_rows(size, cols, align)
        piece = lax.slice_in_dim(buf, row, row + rows, axis=lead).reshape(lead_shape + (-1,))
        out.append(piece[..., :size].reshape(lead_shape + tuple(shp)))
        row += rows
    return out


BIG = ("ffn_w_in", "ffn_w_out", "a_w_qkv", "a_w_o", "kv_w", "b_w_q", "b_w_o")
SMALL = ("ffn_norm", "mix_norm", "a_q_norm", "a_k_norm", "kv_norm", "kv_b_f", "kv_k_norm", "b_q_norm")
WEIGHTS = ("ffn_norm", "ffn_w_in", "ffn_w_out", "mix_norm", "a_w_qkv", "a_q_norm", "a_k_norm", "a_w_o",
           "kv_norm", "kv_w", "kv_b_f", "kv_k_norm", "b_w_q", "b_q_norm", "b_w_o")
GATE_PAD = 2 * LANES


def _stack_weights(sh, d):
    depth = sh["ffn_w_in"].shape[1]
    kv = sh["kv_w"].transpose(1, 0, 2).reshape(d, -1)
    kv = jnp.pad(kv, ((0, 0), (0, 2 * d + GATE_PAD - kv.shape[1])))
    return {
        "ffn_w_in": [[sh["ffn_w_in"][:, l, i] for i in range(2)] for l in range(depth)],
        "ffn_w_out": [[sh["ffn_w_out"][:, l, i].reshape(1, -1, d) for i in range(2)] for l in range(depth)],
        "a_w_qkv": sh["a_w_qkv"][:, 0],
        "a_w_o": sh["a_w_o"].reshape(1, d, d),
        "kv_w": kv[None],
        "b_w_q": sh["b_w_q"].reshape(1, d, d),
        "b_w_o": sh["b_w_o"].reshape(1, d, d),
    }


def _unstack_grads(dw, d, heads):
    def rows4(z):
        return z.reshape(N_CHIPS, -1, d)

    kv_cols = 2 * d + heads
    kv = dw["kv_w"][0][:, :kv_cols].reshape(d, N_CHIPS, kv_cols // N_CHIPS).transpose(1, 0, 2)
    return [
        jnp.stack([jnp.stack(row, axis=1) for row in dw["ffn_w_in"]], axis=1),
        jnp.stack([jnp.stack([rows4(z) for z in row], axis=1) for row in dw["ffn_w_out"]], axis=1),
        dw["a_w_qkv"][:, None],
        rows4(dw["a_w_o"])[:, None],
        kv,
        rows4(dw["b_w_q"])[:, None],
        rows4(dw["b_w_o"])[:, None],
    ]


def kernel(x, positions, ffn_norm, ffn_w_in, ffn_w_out, mix_norm, a_w_qkv, a_q_norm, a_k_norm, a_w_o, kv_norm, kv_w, kv_b_f, kv_k_norm, b_w_q, b_q_norm, b_w_o, loss_target, m_ffn_norm, m_ffn_w_in, m_ffn_w_out, m_mix_norm, m_a_w_qkv, m_a_q_norm, m_a_k_norm, m_a_w_o, m_kv_norm, m_kv_w, m_kv_b_f, m_kv_k_norm, m_b_w_q, m_b_q_norm, m_b_w_o, v_ffn_norm, v_ffn_w_in, v_ffn_w_out, v_mix_norm, v_a_w_qkv, v_a_q_norm, v_a_k_norm, v_a_w_o, v_kv_norm, v_kv_w, v_kv_b_f, v_kv_k_norm, v_b_w_q, v_b_q_norm, v_b_w_o):
    wts = dict(zip(WEIGHTS, (ffn_norm, ffn_w_in, ffn_w_out, mix_norm, a_w_qkv, a_q_norm, a_k_norm, a_w_o, kv_norm,
                             kv_w, kv_b_f, kv_k_norm, b_w_q, b_q_norm, b_w_o)))
    mom = dict(zip(WEIGHTS, (m_ffn_norm, m_ffn_w_in, m_ffn_w_out, m_mix_norm, m_a_w_qkv, m_a_q_norm, m_a_k_norm,
                             m_a_w_o, m_kv_norm, m_kv_w, m_kv_b_f, m_kv_k_norm, m_b_w_q, m_b_q_norm, m_b_w_o)))
    var = dict(zip(WEIGHTS, (v_ffn_norm, v_ffn_w_in, v_ffn_w_out, v_mix_norm, v_a_w_qkv, v_a_q_norm, v_a_k_norm,
                             v_a_w_o, v_kv_norm, v_kv_w, v_kv_b_f, v_kv_k_norm, v_b_w_q, v_b_q_norm, v_b_w_o)))
    batch, seq, d = x.shape
    cfg = Cfg(d_model=d, d_ff=ffn_w_out.shape[2] * N_CHIPS, seq=seq, batch=batch)
    chip = 2 * lax.axis_index("x") + lax.axis_index("y")
    big_shapes = [wts[n].shape for n in BIG]

    shard = _pack([wts[n].astype(BF16) for n in BIG], 0, PACK_COLS, PACK_ROW_ALIGN, PACK_COLS)
    gathered = _all_gather_chips(shard, name="gather_weights")
    gathered = lax.dynamic_update_slice_in_dim(gathered, shard[None], chip, axis=0)
    w = _stack_weights(dict(zip(BIG, _unpack(gathered, big_shapes, 1, PACK_COLS, PACK_ROW_ALIGN))), d)
    norm_shard = _pack([ffn_norm], 0, LANES, SUBLANES, SUBLANES)
    norms = _all_gather_small(norm_shard, name="gather_ffn_norm")[0::2]
    (norms,) = _unpack(norms, [ffn_norm.shape], 1, LANES, SUBLANES)
    small = {"ffn_norm": jnp.moveaxis(norms, 0, 2).reshape(ffn_norm.shape[:2] + (d,)),
             "mix_norm": mix_norm, "a_q_norm": a_q_norm[0], "a_k_norm": a_k_norm[0], "kv_norm": kv_norm,
             "kv_b_f": kv_b_f, "kv_k_norm": kv_k_norm, "b_q_norm": b_q_norm}

    loss, dx, dw, ds = _local_step(cfg, x.reshape(cfg.tokens, d), positions.reshape(cfg.tokens),
                                   loss_target.reshape(cfg.tokens, d), w, small)
    loss = lax.psum(loss, ("x", "y", "c"))

    g = _pack(_unstack_grads(dw, d, cfg.heads), 1, PACK_COLS, PACK_ROW_ALIGN, PACK_COLS)
    chip_half = _add_own_half(g, _swap_halves(g, name="swap_halves"), name="add_halves")
    parts = _scatter_chips(chip_half, name="scatter_chips")
    parts = lax.dynamic_update_slice_in_dim(parts, lax.dynamic_slice_in_dim(chip_half, chip, 1, axis=0), chip, axis=0)
    mine = _sum_parts(parts, name="sum_chips")
    both = _join_halves(mine, name="join_halves")
    g_big = lax.dynamic_update_slice_in_dim(both, mine[None], lax.axis_index("c"), axis=0).reshape(g.shape[1:])
    grads = dict(zip(BIG, _unpack(g_big, big_shapes, 0, PACK_COLS, PACK_ROW_ALIGN)))

    small_shapes = [ds[n].shape for n in SMALL]
    parts = _all_gather_small(_pack([ds[n] for n in SMALL], 0, LANES, SUBLANES, SUBLANES), name="gather_small")
    g_small = dict(zip(SMALL, _unpack(_sum_parts(parts, name="sum_small"), small_shapes, 0, LANES, SUBLANES)))
    quarter = d // N_CHIPS
    g_small["ffn_norm"] = lax.dynamic_slice_in_dim(g_small["ffn_norm"], chip * quarter, quarter, axis=2)
    grads.update(g_small)

    delta, new_m, new_v = {}, {}, {}
    for n in BIG:
        delta[n], new_m[n], new_v[n] = _adamw(wts[n], mom[n], var[n], grads[n], name=f"adamw_{n}")
    packed = [_pack([z[n] for n in SMALL], 0, LANES, SUBLANES, SUBLANES) for z in (wts, mom, var, grads)]
    small_out = _adamw(*packed, name="adamw_small")
    shard_shapes = [wts[n].shape for n in SMALL]
    for out, res in zip((delta, new_m, new_v), small_out):
        out.update(zip(SMALL, _unpack(res, shard_shapes, 0, LANES, SUBLANES)))

    return (loss, dx.reshape(x.shape), *[grads[n] for n in WEIGHTS], *[delta[n] for n in WEIGHTS],
            *[new_m[n] for n in WEIGHTS], *[new_v[n] for n in WEIGHTS])
```

```python
import functools
from typing import NamedTuple

import jax
import jax.numpy as jnp
from jax import lax
from jax.experimental import pallas as pl
from jax.experimental.pallas import tpu as pltpu

F32 = jnp.float32
BF16 = jnp.bfloat16

HEAD_DIM = 64
LANES = 128
SUBLANES = 8
ROT_DIM = HEAD_DIM // 4
ROPE_THETA = 500000.0
NORM_EPS = 1e-6
BAND = 128
DILATIONS = (1, 4, 16)
NEG = -1e30
Q_SCALE = HEAD_DIM ** -0.5
N_CHIPS = 4
N_DEV = 8
VMEM_LIMIT = 48 * 1024 * 1024

ADAM_LR = 0.001
ADAM_B1 = 0.9
ADAM_B2 = 0.999
ADAM_EPS = 1e-08
ADAM_WD = 0.01
ADAM_STEP = 10


class Cfg(NamedTuple):
    d_model: int
    d_ff: int
    seq: int
    batch: int

    @property
    def heads(self):
        return self.d_model // HEAD_DIM

    @property
    def tokens(self):
        return self.batch * self.seq

    @property
    def pairs(self):
        return self.d_model // LANES


def _params(sem):
    return pltpu.CompilerParams(dimension_semantics=sem, vmem_limit_bytes=VMEM_LIMIT)


def _blk(dim, want):
    if dim <= want:
        return dim
    for b in range(want // LANES * LANES, 0, -LANES):
        if dim % b == 0:
            return b
    b = want
    while dim % b:
        b //= 2
    return b


def _mm(a, b, *, form, out_dtype, name, bm=1024, bn=1024, bk=1024, res=None, scale=1.0):
    if form == "F":
        m, kdim = a.shape
        jn, _, ns = b.shape
        bm, bn, bk = _blk(m, bm), _blk(ns, bn), _blk(kdim, bk)
        npj = ns // bn
        grid = (m // bm, jn * npj, kdim // bk)
        a_spec = pl.BlockSpec((bm, bk), lambda i, n, k: (i, k))
        b_spec = pl.BlockSpec((None, bk, bn), lambda i, n, k: (n // npj, k, n % npj))
        o_spec = pl.BlockSpec((bm, bn), lambda i, n, k: (i, n))
        o_shape = jax.ShapeDtypeStruct((m, jn * ns), out_dtype)
        dims = (((1,), (0,)), ((), ()))
    elif form == "B":
        m = a.shape[0]
        jn, kdim, ns = b.shape
        bm, bn, bk = _blk(m, bm), _blk(kdim, bn), _blk(ns, bk)
        kpj = ns // bk
        grid = (m // bm, kdim // bn, jn * kpj)
        a_spec = pl.BlockSpec((bm, bk), lambda i, n, k: (i, k))
        b_spec = pl.BlockSpec((None, bn, bk), lambda i, n, k: (k // kpj, n, k % kpj))
        o_spec = pl.BlockSpec((bm, bn), lambda i, n, k: (i, n))
        o_shape = jax.ShapeDtypeStruct((m, kdim), out_dtype)
        dims = (((1,), (1,)), ((), ()))
    else:
        raise ValueError(form)
    nk = grid[2]

    def body(*refs):
        if res is None:
            a_ref, b_ref, o_ref, acc_ref = refs
            r_ref = None
        else:
            a_ref, b_ref, r_ref, o_ref, acc_ref = refs
        k = pl.program_id(2)

        @pl.when(k == 0)
        def _():
            acc_ref[...] = jnp.zeros_like(acc_ref)

        acc_ref[...] += lax.dot_general(a_ref[...].astype(BF16), b_ref[...].astype(BF16), dims,
                                        preferred_element_type=F32)

        @pl.when(k == nk - 1)
        def _():
            r = acc_ref[...]
            if scale != 1.0:
                r = r * scale
            if r_ref is not None:
                r = r_ref[...] + r
            o_ref[...] = r.astype(o_ref.dtype)

    in_specs = [a_spec, b_spec]
    args = [a, b]
    if res is not None:
        in_specs.append(pl.BlockSpec((bm, bn), lambda i, n, k: (i, n)))
        args.append(res)
    return pl.pallas_call(
        body, name=name, grid=grid, in_specs=in_specs, out_specs=o_spec, out_shape=o_shape,
        scratch_shapes=[pltpu.VMEM((bm, bn), F32)],
        compiler_params=_params(("parallel", "parallel", "arbitrary")),
    )(*args)


def _mm_grad(a, dy, jn, *, name, scale=1.0, bm=1024, bn=1024, bk=1024):
    t, kdim = a.shape
    ns = dy.shape[1] // jn
    bm, bn, bk = _blk(kdim, bm), _blk(ns, bn), _blk(t, bk)
    npj = ns // bn
    grid = (kdim // bm, jn * npj, t // bk)
    nk = grid[2]
    dims = (((0,), (0,)), ((), ()))

    def body(a_ref, b_ref, o_ref, acc_ref):
        k = pl.program_id(2)

        @pl.when(k == 0)
        def _():
            acc_ref[...] = jnp.zeros_like(acc_ref)

        acc_ref[...] += lax.dot_general(a_ref[...].astype(BF16), b_ref[...].astype(BF16), dims,
                                        preferred_element_type=F32)

        @pl.when(k == nk - 1)
        def _():
            r = acc_ref[...]
            if scale != 1.0:
                r = r * scale
            o_ref[...] = r

    return pl.pallas_call(
        body, name=name, grid=grid,
        in_specs=[pl.BlockSpec((bk, bm), lambda m, n, k: (k, m)),
                  pl.BlockSpec((bk, bn), lambda m, n, k: (k, n))],
        out_specs=pl.BlockSpec((None, bm, bn), lambda m, n, k: (n // npj, m, n % npj)),
        out_shape=jax.ShapeDtypeStruct((jn, kdim, ns), F32),
        scratch_shapes=[pltpu.VMEM((bm, bn), F32)],
        compiler_params=_params(("parallel", "parallel", "arbitrary")),
    )(a, dy)


ROW_BLOCK = 512


def _fold8(x):
    return jnp.sum(x.reshape(x.shape[0] // SUBLANES, SUBLANES, x.shape[1]), axis=0)


def _rms_fwd(x, g, *, name):
    t, d = x.shape
    tr = _blk(t, ROW_BLOCK)

    def body(x_ref, g_ref, o_ref):
        xv = x_ref[...]
        rstd = lax.rsqrt(jnp.mean(xv * xv, axis=-1, keepdims=True) + NORM_EPS)
        o_ref[...] = ((xv * rstd) * g_ref[...]).astype(BF16)

    return pl.pallas_call(
        body, name=name, grid=(t // tr,),
        in_specs=[pl.BlockSpec((tr, d), lambda i: (i, 0)), pl.BlockSpec((1, d), lambda i: (0, 0))],
        out_specs=pl.BlockSpec((tr, d), lambda i: (i, 0)),
        out_shape=jax.ShapeDtypeStruct((t, d), BF16),
        compiler_params=_params(("parallel",)),
    )(x, g.reshape(1, d))


def _rms_bwd(x, g, dy, dres, *, name):
    t, d = x.shape
    tr = _blk(t, ROW_BLOCK)

    def body(x_ref, g_ref, dy_ref, dres_ref, dx_ref, dg_ref):
        i = pl.program_id(0)
        xv = x_ref[...]
        rstd = lax.rsqrt(jnp.mean(xv * xv, axis=-1, keepdims=True) + NORM_EPS)
        xhat = xv * rstd
        dyv = dy_ref[...]
        dyg = dyv * g_ref[...]
        proj = jnp.mean(dyg * xhat, axis=-1, keepdims=True)
        dx_ref[...] = dres_ref[...] + rstd * (dyg - xhat * proj)

        @pl.when(i == 0)
        def _():
            dg_ref[...] = jnp.zeros_like(dg_ref)

        dg_ref[...] += _fold8(dyv * xhat)

    dx, dg = pl.pallas_call(
        body, name=name, grid=(t // tr,),
        in_specs=[pl.BlockSpec((tr, d), lambda i: (i, 0)), pl.BlockSpec((1, d), lambda i: (0, 0)),
                  pl.BlockSpec((tr, d), lambda i: (i, 0)), pl.BlockSpec((tr, d), lambda i: (i, 0))],
        out_specs=[pl.BlockSpec((tr, d), lambda i: (i, 0)), pl.BlockSpec((SUBLANES, d), lambda i: (0, 0))],
        out_shape=[jax.ShapeDtypeStruct((t, d), F32), jax.ShapeDtypeStruct((SUBLANES, d), F32)],
        compiler_params=_params(("arbitrary",)),
    )(x, g.reshape(1, d), dy, dres)
    return dx, jnp.sum(dg, axis=0)


def _swiglu_fwd(u, *, name):
    t, f2 = u.shape
    f = f2 // 2
    tr = _blk(t, 256)

    def body(g_ref, u_ref, o_ref):
        gv = g_ref[...]
        o_ref[...] = (gv * jax.nn.sigmoid(gv) * u_ref[...]).astype(BF16)

    return pl.pallas_call(
        body, name=name, grid=(t // tr,),
        in_specs=[pl.BlockSpec((tr, f), lambda i: (i, 0)), pl.BlockSpec((tr, f), lambda i: (i, 1))],
        out_specs=pl.BlockSpec((tr, f), lambda i: (i, 0)),
        out_shape=jax.ShapeDtypeStruct((t, f), BF16),
        compiler_params=_params(("parallel",)),
    )(u, u)


def _swiglu_bwd(u, da, *, name):
    t, f2 = u.shape
    f = f2 // 2
    tr = _blk(t, 256)

    def body(g_ref, u_ref, da_ref, o_ref):
        gv = g_ref[...]
        sg = jax.nn.sigmoid(gv)
        silu = gv * sg
        dav = da_ref[...]
        o_ref[:, :f] = (dav * u_ref[...] * (sg + silu * (1.0 - sg))).astype(BF16)
        o_ref[:, f:] = (dav * silu).astype(BF16)

    return pl.pallas_call(
        body, name=name, grid=(t // tr,),
        in_specs=[pl.BlockSpec((tr, f), lambda i: (i, 0)), pl.BlockSpec((tr, f), lambda i: (i, 1)),
                  pl.BlockSpec((tr, f), lambda i: (i, 0))],
        out_specs=pl.BlockSpec((tr, f2), lambda i: (i, 0)),
        out_shape=jax.ShapeDtypeStruct((t, f2), BF16),
        compiler_params=_params(("parallel",)),
    )(u, u, da)


def _loss_fwd_bwd(h, target, *, name):
    t, d = h.shape
    tr = _blk(t, ROW_BLOCK)

    def body(h_ref, t_ref, dh_ref, l_ref):
        i = pl.program_id(0)
        err = h_ref[...] - t_ref[...]
        dh_ref[...] = err * (1.0 / d)

        @pl.when(i == 0)
        def _():
            l_ref[...] = jnp.zeros_like(l_ref)

        l_ref[...] += _fold8(err * err)

    dh, part = pl.pallas_call(
        body, name=name, grid=(t // tr,),
        in_specs=[pl.BlockSpec((tr, d), lambda i: (i, 0)), pl.BlockSpec((tr, d), lambda i: (i, 0))],
        out_specs=[pl.BlockSpec((tr, d), lambda i: (i, 0)), pl.BlockSpec((SUBLANES, d), lambda i: (0, 0))],
        out_shape=[jax.ShapeDtypeStruct((t, d), F32), jax.ShapeDtypeStruct((SUBLANES, d), F32)],
        compiler_params=_params(("arbitrary",)),
    )(h, target)
    return jnp.sum(part) * (0.5 / d), dh


def _seg_matrix():
    r = lax.broadcasted_iota(jnp.int32, (LANES, LANES), 0) // HEAD_DIM
    c = lax.broadcasted_iota(jnp.int32, (LANES, LANES), 1) // HEAD_DIM
    return (r == c).astype(BF16)


def _head_sum(x, seg):
    hi = x.astype(BF16)
    r1 = x - hi.astype(F32)
    mid = r1.astype(BF16)
    lo = (r1 - mid.astype(F32)).astype(BF16)
    dot = functools.partial(jnp.dot, preferred_element_type=F32)
    return dot(hi, seg) + dot(mid, seg) + dot(lo, seg)


def _lane_in_head(shape):
    return lax.broadcasted_iota(jnp.int32, shape, 1) % HEAD_DIM


def _rot_partner(x):
    up = pltpu.roll(x, LANES - ROT_DIM // 2, 1)
    down = pltpu.roll(x, ROT_DIM // 2, 1)
    return jnp.where(_lane_in_head(x.shape) < ROT_DIM // 2, up, down)


def _rope_tables(positions):
    inv_freq = ROPE_THETA ** (-jnp.arange(0, ROT_DIM, 2, dtype=F32) / ROT_DIM)
    ang = positions.astype(F32)[:, None] * inv_freq
    t = ang.shape[0]
    rest = HEAD_DIM - ROT_DIM
    cos = jnp.concatenate([jnp.cos(ang), jnp.cos(ang), jnp.ones((t, rest), F32)], axis=1)
    sin = jnp.concatenate([-jnp.sin(ang), jnp.sin(ang), jnp.zeros((t, rest), F32)], axis=1)
    return jnp.tile(cos, (1, LANES // HEAD_DIM)), jnp.tile(sin, (1, LANES // HEAD_DIM))


def _kind_is(j, kinds, kind):
    hits = [j == jj for jj, k in enumerate(kinds) if k == kind]
    return functools.reduce(jnp.logical_or, hits) if hits else None


def _hn_fwd(x, gains, kinds, d, cos, sin, *, name, col0=0):
    t = x.shape[0]
    n = len(kinds)
    tr = _blk(t, ROW_BLOCK)
    seg = _seg_matrix()
    g8 = jnp.repeat(gains.astype(F32), SUBLANES, axis=0)

    def body(x_ref, g_ref, seg_ref, cos_ref, sin_ref, o_ref):
        j = pl.program_id(1)

        def normed(rope):
            for c in range(d // LANES):
                sl = slice(c * LANES, (c + 1) * LANES)
                xv = x_ref[:, sl]
                ms = _head_sum(xv * xv, seg_ref[...]) * (1.0 / HEAD_DIM)
                y = (xv * lax.rsqrt(ms + NORM_EPS)) * g_ref[0:1, sl]
                if rope:
                    y = y * cos_ref[...] + _rot_partner(y) * sin_ref[...]
                o_ref[:, sl] = y.astype(BF16)

        for kind in ("rope", "norm"):
            hit = _kind_is(j, kinds, kind)
            if hit is not None:
                pl.when(hit)(functools.partial(normed, kind == "rope"))
        hit = _kind_is(j, kinds, "cast")
        if hit is not None:
            @pl.when(hit)
            def _():
                o_ref[...] = x_ref[...].astype(BF16)

    return pl.pallas_call(
        body, name=name, grid=(t // tr, n),
        in_specs=[pl.BlockSpec((tr, d), lambda i, j: (i, col0 + j)), pl.BlockSpec((SUBLANES, d), lambda i, j: (j, 0)),
                  pl.BlockSpec((LANES, LANES), lambda i, j: (0, 0)),
                  pl.BlockSpec((tr, LANES), lambda i, j: (i, 0)), pl.BlockSpec((tr, LANES), lambda i, j: (i, 0))],
        out_specs=pl.BlockSpec((tr, d), lambda i, j: (i, j)),
        out_shape=jax.ShapeDtypeStruct((t, n * d), BF16),
        compiler_params=_params(("parallel", "parallel")),
    )(x, g8, seg, cos, sin)


def _hn_bwd(x, dy, gains, kinds, d, cos, sin, *, name):
    t = x.shape[0]
    n = len(kinds)
    tr = _blk(t, ROW_BLOCK)
    seg = _seg_matrix()
    g8 = jnp.repeat(gains.astype(F32), SUBLANES, axis=0)

    def body(x_ref, dy_ref, g_ref, seg_ref, cos_ref, sin_ref, dx_ref, dg_ref):
        j = pl.program_id(0)
        i = pl.program_id(1)

        @pl.when(i == 0)
        def _():
            dg_ref[...] = jnp.zeros_like(dg_ref)

        def normed(rope):
            for c in range(d // LANES):
                sl = slice(c * LANES, (c + 1) * LANES)
                xv = x_ref[:, sl]
                dyv = dy_ref[:, sl]
                if rope:
                    dyv = dyv * cos_ref[...] - _rot_partner(dyv) * sin_ref[...]
                ms = _head_sum(xv * xv, seg_ref[...]) * (1.0 / HEAD_DIM)
                rstd = lax.rsqrt(ms + NORM_EPS)
                xhat = xv * rstd
                dg_ref[:, sl] += _fold8(dyv * xhat)
                dyg = dyv * g_ref[0:1, sl]
                proj = _head_sum(dyg * xhat, seg_ref[...]) * (1.0 / HEAD_DIM)
                dx_ref[:, sl] = (rstd * (dyg - xhat * proj)).astype(BF16)

        for kind in ("rope", "norm"):
            hit = _kind_is(j, kinds, kind)
            if hit is not None:
                pl.when(hit)(functools.partial(normed, kind == "rope"))
        hit = _kind_is(j, kinds, "cast")
        if hit is not None:
            @pl.when(hit)
            def _():
                dx_ref[...] = dy_ref[...].astype(BF16)

    dx, dg = pl.pallas_call(
        body, name=name, grid=(n, t // tr),
        in_specs=[pl.BlockSpec((tr, d), lambda j, i: (i, j)), pl.BlockSpec((tr, d), lambda j, i: (i, j)),
                  pl.BlockSpec((SUBLANES, d), lambda j, i: (j, 0)),
                  pl.BlockSpec((LANES, LANES), lambda j, i: (0, 0)),
                  pl.BlockSpec((tr, LANES), lambda j, i: (i, 0)), pl.BlockSpec((tr, LANES), lambda j, i: (i, 0))],
        out_specs=[pl.BlockSpec((tr, d), lambda j, i: (i, j)), pl.BlockSpec((SUBLANES, d), lambda j, i: (j, 0))],
        out_shape=[jax.ShapeDtypeStruct((t, n * d), BF16), jax.ShapeDtypeStruct((n * SUBLANES, d), F32)],
        compiler_params=_params(("arbitrary", "arbitrary")),
    )(x, dy, g8, seg, cos, sin)
    dg = dg.reshape(n, SUBLANES, d // HEAD_DIM, HEAD_DIM).sum(axis=(1, 2))
    return dx, dg


def _head_dot(a, b, *, name):
    t, d = a.shape
    tr = _blk(t, ROW_BLOCK)
    seg = _seg_matrix()

    def body(a_ref, b_ref, seg_ref, o_ref):
        for c in range(d // LANES):
            sl = slice(c * LANES, (c + 1) * LANES)
            o_ref[:, sl] = _head_sum(a_ref[:, sl].astype(BF16).astype(F32) * b_ref[:, sl], seg_ref[...])

    return pl.pallas_call(
        body, name=name, grid=(t // tr,),
        in_specs=[pl.BlockSpec((tr, d), lambda i: (i, 0)), pl.BlockSpec((tr, d), lambda i: (i, 0)),
                  pl.BlockSpec((LANES, LANES), lambda i: (0, 0))],
        out_specs=pl.BlockSpec((tr, d), lambda i: (i, 0)),
        out_shape=jax.ShapeDtypeStruct((t, d), F32),
        compiler_params=_params(("parallel",)),
    )(a, b, seg)


def _half_mask(shape):
    return lax.broadcasted_iota(jnp.int32, shape, 1) < HEAD_DIM


def _band_valid(first):
    qi = lax.broadcasted_iota(jnp.int32, (BAND, 2 * BAND), 0)
    kj = lax.broadcasted_iota(jnp.int32, (BAND, 2 * BAND), 1)
    dist = qi + BAND - kj
    return (dist >= 0) & (dist <= BAND) & ((kj >= BAND) | jnp.logical_not(first))


_NT = (((1,), (1,)), ((), ()))
_TN = (((0,), (0,)), ((), ()))


def _dot2(p, v):
    hi = p.astype(BF16)
    lo = (p - hi.astype(F32)).astype(BF16)
    return jnp.dot(hi, v, preferred_element_type=F32) + jnp.dot(lo, v, preferred_element_type=F32)


def _band_fwd(qkv, dil, cfg, *, name):
    t, d = cfg.tokens, cfg.d_model
    w = 3 * d
    rows = t // dil
    nbt = rows // BAND
    nb = cfg.seq // (dil * BAND)
    view = qkv.reshape(rows, dil * w)
    ncol = w // d

    def body(q_ref, kp_ref, kc_ref, vp_ref, vc_ref, o_ref, lse_ref):
        i = pl.program_id(1)
        valid = _band_valid(i % nb == 0)
        half = _half_mask((BAND, LANES))
        for hp in range(d // LANES):
            sl = slice(hp * LANES, (hp + 1) * LANES)
            q2 = q_ref[:, sl]
            kk = jnp.concatenate([kp_ref[:, sl], kc_ref[:, sl]], axis=0)
            vv = jnp.concatenate([vp_ref[:, sl], vc_ref[:, sl]], axis=0)
            outs, lses = [], []
            for e in range(2):
                qe = jnp.where(half == (e == 0), q2, jnp.zeros_like(q2))
                s = lax.dot_general(qe, kk, _NT, preferred_element_type=F32)
                s = jnp.where(valid, s, NEG)
                m = jnp.max(s, axis=1, keepdims=True)
                p = jnp.exp(s - m)
                l = jnp.sum(p, axis=1, keepdims=True)
                outs.append(_dot2(p * (1.0 / l), vv))
                lses.append(m + jnp.log(l))
            o_ref[:, sl] = jnp.where(half, outs[0], outs[1])
            lse_ref[:, sl] = jnp.where(half, lses[0], lses[1])

    def col(which):
        return lambda r, i: (i, r * ncol + which)

    def col_prev(which):
        return lambda r, i: (jnp.maximum(i - 1, 0), r * ncol + which)

    blk = (BAND, d)
    o, lse = pl.pallas_call(
        body, name=name, grid=(dil, nbt),
        in_specs=[pl.BlockSpec(blk, col(0)), pl.BlockSpec(blk, col_prev(1)), pl.BlockSpec(blk, col(1)),
                  pl.BlockSpec(blk, col_prev(2)), pl.BlockSpec(blk, col(2))],
        out_specs=[pl.BlockSpec(blk, lambda r, i: (i, r)), pl.BlockSpec(blk, lambda r, i: (i, r))],
        out_shape=[jax.ShapeDtypeStruct((rows, dil * d), F32), jax.ShapeDtypeStruct((rows, dil * d), F32)],
        compiler_params=_params(("parallel", "arbitrary")),
    )(view, view, view, view, view)
    return o.reshape(t, d), lse.reshape(t, d)


def _band_bwd(qkv, dmixed, lse_all, dsum, dil, cfg, *, name):
    t, d = cfg.tokens, cfg.d_model
    w = 3 * d
    rows = t // dil
    nbt = rows // BAND
    nb = cfg.seq // (dil * BAND)
    view = qkv.reshape(rows, dil * w)
    ncol = w // d
    do_v, l_v, d_v = (z.reshape(rows, dil * d) for z in (dmixed, lse_all, dsum))

    def body(q_ref, kp_ref, kc_ref, vp_ref, vc_ref, do_ref, l_ref, ds_ref, dq_ref, dk_ref, dv_ref, ck_ref, cv_ref):
        i = pl.program_id(1)

        @pl.when(i < nbt)
        def _():
            valid = _band_valid(i % nb == 0)
            half = _half_mask((BAND, LANES))
            half2 = _half_mask((2 * BAND, LANES))
            for hp in range(d // LANES):
                sl = slice(hp * LANES, (hp + 1) * LANES)
                q2 = q_ref[:, sl]
                kk = jnp.concatenate([kp_ref[:, sl], kc_ref[:, sl]], axis=0)
                vv = jnp.concatenate([vp_ref[:, sl], vc_ref[:, sl]], axis=0)
                do2 = do_ref[:, sl].astype(BF16)
                dqs, dks, dvs = [], [], []
                for e in range(2):
                    lane0 = e * HEAD_DIM
                    keep = half == (e == 0)
                    qe = jnp.where(keep, q2, jnp.zeros_like(q2))
                    doe = jnp.where(keep, do2, jnp.zeros_like(do2))
                    s = lax.dot_general(qe, kk, _NT, preferred_element_type=F32)
                    s = jnp.where(valid, s, NEG)
                    p = jnp.exp(s - l_ref[:, hp * LANES + lane0:hp * LANES + lane0 + 1])
                    dp = lax.dot_general(doe, vv, _NT, preferred_element_type=F32)
                    dsc = (p * (dp - ds_ref[:, hp * LANES + lane0:hp * LANES + lane0 + 1])).astype(BF16)
                    dqs.append(jnp.dot(dsc, kk, preferred_element_type=F32))
                    dks.append(lax.dot_general(dsc, q2, _TN, preferred_element_type=F32))
                    dvs.append(lax.dot_general(p.astype(BF16), do2, _TN, preferred_element_type=F32))
                dq_ref[:, sl] = jnp.where(half, dqs[0], dqs[1])
                dkk = jnp.where(half2, dks[0], dks[1])
                dvv = jnp.where(half2, dvs[0], dvs[1])

                @pl.when(i > 0)
                def _():
                    dk_ref[:, sl] = ck_ref[:, sl] + dkk[:BAND]
                    dv_ref[:, sl] = cv_ref[:, sl] + dvv[:BAND]

                ck_ref[:, sl] = dkk[BAND:]
                cv_ref[:, sl] = dvv[BAND:]

        @pl.when(i == nbt)
        def _():
            dk_ref[...] = ck_ref[...]
            dv_ref[...] = cv_ref[...]

    def cur(i):
        return jnp.minimum(i, nbt - 1)

    def col(which):
        return lambda r, i: (cur(i), r * ncol + which)

    def col_prev(which):
        return lambda r, i: (jnp.maximum(cur(i) - 1, 0), r * ncol + which)

    blk = (BAND, d)
    here = pl.BlockSpec(blk, lambda r, i: (cur(i), r))
    behind = pl.BlockSpec(blk, lambda r, i: (jnp.maximum(i - 1, 0), r))
    shape = jax.ShapeDtypeStruct((rows, dil * d), F32)
    dq, dk, dv = pl.pallas_call(
        body, name=name, grid=(dil, nbt + 1),
        in_specs=[pl.BlockSpec(blk, col(0)), pl.BlockSpec(blk, col_prev(1)), pl.BlockSpec(blk, col(1)),
                  pl.BlockSpec(blk, col_prev(2)), pl.BlockSpec(blk, col(2)), here, here, here],
        out_specs=[here, behind, behind],
        out_shape=[shape, shape, shape],
        scratch_shapes=[pltpu.VMEM(blk, F32), pltpu.VMEM(blk, F32)],
        compiler_params=_params(("arbitrary", "arbitrary")),
    )(view, view, view, view, view, do_v, l_v, d_v)
    return dq.reshape(t, d), dk.reshape(t, d), dv.reshape(t, d)


def _band_valid_t(first):
    s = lax.broadcasted_iota(jnp.int32, (2 * BAND, BAND), 0)
    t = lax.broadcasted_iota(jnp.int32, (2 * BAND, BAND), 1)
    dist = t + BAND - s
    return (dist >= 0) & (dist <= BAND) & ((s >= BAND) | jnp.logical_not(first))


def _band_layouts(qkv, dil, cfg):
    rows = cfg.tokens // dil
    d = cfg.d_model
    return qkv.reshape(rows, dil * 3 * d), qkv.reshape(rows, dil, 3, d).transpose(1, 2, 3, 0)


def _to_classes_t(z, dil, width):
    return z.reshape(z.shape[0] // dil, dil, width).transpose(1, 2, 0)


def _from_classes_t(z):
    dil, width, rows = z.shape
    return z.transpose(2, 0, 1).reshape(rows * dil, width)


def _band_fwd_t(nat, tr, dil, cfg, *, name):
    d, hh = cfg.d_model, cfg.heads
    rows = cfg.tokens // dil
    nbt = rows // BAND
    nb = cfg.seq // (dil * BAND)

    def body(qt_ref, kp_ref, kc_ref, vtp_ref, vtc_ref, o_ref, lse_ref):
        i = pl.program_id(1)
        valid = _band_valid_t(i % nb == 0)
        upper = lax.broadcasted_iota(jnp.int32, (LANES, BAND), 0) < HEAD_DIM
        for hp in range(d // LANES):
            pair = slice(hp * LANES, (hp + 1) * LANES)
            qt2 = qt_ref[pair, :]
            kk = jnp.concatenate([kp_ref[:, pair], kc_ref[:, pair]], axis=0)
            for e in range(2):
                h = 2 * hp + e
                hrows = slice(h * HEAD_DIM, (h + 1) * HEAD_DIM)
                qte = jnp.where(upper == (e == 0), qt2, jnp.zeros_like(qt2))
                s = jnp.where(valid, jnp.dot(kk, qte, preferred_element_type=F32), NEG)
                m = jnp.max(s, axis=0, keepdims=True)
                p = jnp.exp(s - m)
                l = jnp.sum(p, axis=0, keepdims=True)
                hi = p.astype(BF16)
                lo = (p - hi.astype(F32)).astype(BF16)
                vvt = jnp.concatenate([vtp_ref[hrows, :], vtc_ref[hrows, :]], axis=1)
                o = jnp.dot(vvt, hi, preferred_element_type=F32) + jnp.dot(vvt, lo, preferred_element_type=F32)
                o_ref[hrows, :] = o * (1.0 / l)
                lse_ref[h:h + 1, :] = m + jnp.log(l)

    def prev(i):
        return jnp.maximum(i - 1, 0)

    tblk = (None, None, d, BAND)
    return pl.pallas_call(
        body, name=name, grid=(dil, nbt),
        in_specs=[pl.BlockSpec(tblk, lambda r, i: (r, 0, 0, i)),
                  pl.BlockSpec((BAND, d), lambda r, i: (prev(i), r * 3 + 1)),
                  pl.BlockSpec((BAND, d), lambda r, i: (i, r * 3 + 1)),
                  pl.BlockSpec(tblk, lambda r, i: (r, 2, 0, prev(i))),
                  pl.BlockSpec(tblk, lambda r, i: (r, 2, 0, i))],
        out_specs=[pl.BlockSpec((None, d, BAND), lambda r, i: (r, 0, i)),
                   pl.BlockSpec((None, hh, BAND), lambda r, i: (r, 0, i))],
        out_shape=[jax.ShapeDtypeStruct((dil, d, rows), F32), jax.ShapeDtypeStruct((dil, hh, rows), F32)],
        compiler_params=_params(("parallel", "arbitrary")),
    )(tr, nat, nat, tr, tr)


def _band_bwd_t(nat, tr, do_t, do_nat, lse_c, dsum_c, dil, cfg, *, name):
    d, hh = cfg.d_model, cfg.heads
    rows = cfg.tokens // dil
    nbt = rows // BAND
    nb = cfg.seq // (dil * BAND)

    def body(qt_ref, qn_ref, kp_ref, kc_ref, ktp_ref, ktc_ref, vp_ref, vc_ref, dot_ref, don_ref, l_ref, ds_ref,
             dq_ref, dk_ref, dv_ref, ck_ref, cv_ref):
        i = pl.program_id(1)

        @pl.when(i < nbt)
        def _():
            valid = _band_valid_t(i % nb == 0)
            upper = lax.broadcasted_iota(jnp.int32, (LANES, BAND), 0) < HEAD_DIM
            half2 = _half_mask((2 * BAND, LANES))
            for hp in range(d // LANES):
                pair = slice(hp * LANES, (hp + 1) * LANES)
                qt2, dot2 = qt_ref[pair, :], dot_ref[pair, :]
                qn2, don2 = qn_ref[:, pair], don_ref[:, pair]
                kk = jnp.concatenate([kp_ref[:, pair], kc_ref[:, pair]], axis=0)
                vv = jnp.concatenate([vp_ref[:, pair], vc_ref[:, pair]], axis=0)
                dks, dvs = [], []
                for e in range(2):
                    h = 2 * hp + e
                    hrows = slice(h * HEAD_DIM, (h + 1) * HEAD_DIM)
                    keep = upper == (e == 0)
                    qte = jnp.where(keep, qt2, jnp.zeros_like(qt2))
                    dote = jnp.where(keep, dot2, jnp.zeros_like(dot2))
                    s = jnp.where(valid, jnp.dot(kk, qte, preferred_element_type=F32), NEG)
                    p = jnp.exp(s - l_ref[h:h + 1, :])
                    dp = jnp.dot(vv, dote, preferred_element_type=F32)
                    dsb = (p * (dp - ds_ref[h:h + 1, :])).astype(BF16)
                    kkt = jnp.concatenate([ktp_ref[hrows, :], ktc_ref[hrows, :]], axis=1)
                    dq_ref[hrows, :] = jnp.dot(kkt, dsb, preferred_element_type=F32)
                    dks.append(jnp.dot(dsb, qn2, preferred_element_type=F32))
                    dvs.append(jnp.dot(p.astype(BF16), don2, preferred_element_type=F32))
                dkk = jnp.where(half2, dks[0], dks[1])
                dvv = jnp.where(half2, dvs[0], dvs[1])

                @pl.when(i > 0)
                def _():
                    dk_ref[:, pair] = ck_ref[:, pair] + dkk[:BAND]
                    dv_ref[:, pair] = cv_ref[:, pair] + dvv[:BAND]

                ck_ref[:, pair] = dkk[BAND:]
                cv_ref[:, pair] = dvv[BAND:]

        @pl.when(i == nbt)
        def _():
            dk_ref[...] = ck_ref[...]
            dv_ref[...] = cv_ref[...]

    def cur(i):
        return jnp.minimum(i, nbt - 1)

    def prev(i):
        return jnp.maximum(cur(i) - 1, 0)

    tblk = (None, None, d, BAND)
    cblk = (None, hh, BAND)
    blk = (BAND, d)
    behind = pl.BlockSpec(blk, lambda r, i: (jnp.maximum(i - 1, 0), r))
    shape = jax.ShapeDtypeStruct((rows, dil * d), F32)
    return pl.pallas_call(
        body, name=name, grid=(dil, nbt + 1),
        in_specs=[pl.BlockSpec(tblk, lambda r, i: (r, 0, 0, cur(i))),
                  pl.BlockSpec(blk, lambda r, i: (cur(i), r * 3)),
                  pl.BlockSpec(blk, lambda r, i: (prev(i), r * 3 + 1)),
                  pl.BlockSpec(blk, lambda r, i: (cur(i), r * 3 + 1)),
                  pl.BlockSpec(tblk, lambda r, i: (r, 1, 0, prev(i))),
                  pl.BlockSpec(tblk, lambda r, i: (r, 1, 0, cur(i))),
                  pl.BlockSpec(blk, lambda r, i: (prev(i), r * 3 + 2)),
                  pl.BlockSpec(blk, lambda r, i: (cur(i), r * 3 + 2)),
                  pl.BlockSpec((None, d, BAND), lambda r, i: (r, 0, cur(i))),
                  pl.BlockSpec(blk, lambda r, i: (cur(i), r)),
                  pl.BlockSpec(cblk, lambda r, i: (r, 0, cur(i))),
                  pl.BlockSpec(cblk, lambda r, i: (r, 0, cur(i)))],
        out_specs=[pl.BlockSpec((None, d, BAND), lambda r, i: (r, 0, cur(i))), behind, behind],
        out_shape=[jax.ShapeDtypeStruct((dil, d, rows), F32), shape, shape],
        scratch_shapes=[pltpu.VMEM(blk, F32), pltpu.VMEM(blk, F32)],
        compiler_params=_params(("arbitrary", "arbitrary")),
    )(tr, nat, nat, nat, tr, tr, nat, nat, do_t, do_nat, lse_c, dsum_c)


def _mix_fwd(outs, lses, *, name):
    t, d = outs[0].shape
    tr = _blk(t, ROW_BLOCK)
    ng = len(outs)

    def body(*refs):
        o_refs, l_refs = refs[:ng], refs[ng:2 * ng]
        mixed_ref, lse_ref = refs[2 * ng:]
        ls = [r[...] for r in l_refs]
        m = functools.reduce(jnp.maximum, ls)
        es = [jnp.exp(l - m) for l in ls]
        tot = functools.reduce(jnp.add, es)
        inv = 1.0 / tot
        mixed_ref[...] = functools.reduce(jnp.add, [(e * inv) * r[...] for e, r in zip(es, o_refs)])
        lse_ref[...] = m + jnp.log(tot)

    spec = pl.BlockSpec((tr, d), lambda i: (i, 0))
    return pl.pallas_call(
        body, name=name, grid=(t // tr,),
        in_specs=[spec] * (2 * ng), out_specs=[spec, spec],
        out_shape=[jax.ShapeDtypeStruct((t, d), F32), jax.ShapeDtypeStruct((t, d), F32)],
        compiler_params=_params(("parallel",)),
    )(*outs, *lses)


GATE_BLOCK = 256


def _tri(n, upper):
    r = lax.broadcasted_iota(jnp.int32, (n, n), 0)
    c = lax.broadcasted_iota(jnp.int32, (n, n), 1)
    return ((c >= r) if upper else (c <= r)).astype(BF16)


def _tri_dot(tri, x):
    hi = x.astype(BF16)
    r1 = x - hi.astype(F32)
    mid = r1.astype(BF16)
    lo = (r1 - mid.astype(F32)).astype(BF16)
    dot = functools.partial(jnp.dot, preferred_element_type=F32)
    return dot(tri, hi) + dot(tri, mid) + dot(tri, lo)


def _log_sigmoid(z):
    return jnp.minimum(z, 0.0) - jnp.log(1.0 + jnp.exp(-jnp.abs(z)))


def _gate_fwd(proj, col_block, bias, cfg, *, name):
    tr = _blk(cfg.seq, GATE_BLOCK)
    nblk = cfg.seq // tr

    def body(z_ref, b_ref, tri_ref, o_ref, carry_ref):
        i = pl.program_id(1)

        @pl.when(i == 0)
        def _():
            carry_ref[...] = jnp.zeros_like(carry_ref)

        logf = _log_sigmoid(z_ref[...] + b_ref[0:1, :])
        cum = _tri_dot(tri_ref[...], logf) + carry_ref[0:1, :]
        o_ref[...] = cum
        carry_ref[...] = jnp.broadcast_to(cum[tr - 1:tr, :], carry_ref.shape)

    return pl.pallas_call(
        body, name=name, grid=(cfg.batch, nblk),
        in_specs=[pl.BlockSpec((tr, LANES), lambda b, i: (b * nblk + i, col_block)),
                  pl.BlockSpec((SUBLANES, LANES), lambda b, i: (0, 0)),
                  pl.BlockSpec((tr, tr), lambda b, i: (0, 0))],
        out_specs=pl.BlockSpec((tr, LANES), lambda b, i: (b * nblk + i, 0)),
        out_shape=jax.ShapeDtypeStruct((cfg.tokens, LANES), F32),
        scratch_shapes=[pltpu.VMEM((SUBLANES, LANES), F32)],
        compiler_params=_params(("arbitrary", "arbitrary")),
    )(proj, jnp.broadcast_to(bias, (SUBLANES, LANES)), _tri(tr, upper=False))


def _gate_bwd(proj, col_block, bias, dcum, cfg, *, name):
    tr = _blk(cfg.seq, GATE_BLOCK)
    nblk = cfg.seq // tr

    def body(z_ref, b_ref, tri_ref, dc_ref, dz_ref, db_ref, carry_ref):
        b = pl.program_id(0)
        i = pl.program_id(1)

        @pl.when(i == 0)
        def _():
            carry_ref[...] = jnp.zeros_like(carry_ref)

        @pl.when((i == 0) & (b == 0))
        def _():
            db_ref[...] = jnp.zeros_like(db_ref)

        dcv = dc_ref[...]
        dlogf = _tri_dot(tri_ref[...], dcv) + carry_ref[0:1, :]
        carry_ref[...] = jnp.broadcast_to(dlogf[0:1, :], carry_ref.shape)
        dz = dlogf * jax.nn.sigmoid(-(z_ref[...] + b_ref[0:1, :]))
        dz_ref[...] = dz
        db_ref[...] += _fold8(dz)

    def rev(b, i):
        return (b * nblk + nblk - 1 - i, 0)

    dz, db = pl.pallas_call(
        body, name=name, grid=(cfg.batch, nblk),
        in_specs=[pl.BlockSpec((tr, LANES), lambda b, i: (b * nblk + nblk - 1 - i, col_block)),
                  pl.BlockSpec((SUBLANES, LANES), lambda b, i: (0, 0)),
                  pl.BlockSpec((tr, tr), lambda b, i: (0, 0)),
                  pl.BlockSpec((tr, LANES), rev)],
        out_specs=[pl.BlockSpec((tr, LANES), rev), pl.BlockSpec((SUBLANES, LANES), lambda b, i: (0, 0))],
        out_shape=[jax.ShapeDtypeStruct((cfg.tokens, LANES), F32), jax.ShapeDtypeStruct((SUBLANES, LANES), F32)],
        scratch_shapes=[pltpu.VMEM((SUBLANES, LANES), F32)],
        compiler_params=_params(("arbitrary", "arbitrary")),
    )(proj, jnp.broadcast_to(bias, (SUBLANES, LANES)), _tri(tr, upper=True), dcum)
    return dz, jnp.sum(db, axis=0)


FOX_BLOCK = 256


def _fox_scores(q2, k2, e, half, mask, cref, ck_row):
    qe = jnp.where(half == (e == 0), q2, jnp.zeros_like(q2))
    s = lax.dot_general(qe, k2, _NT, preferred_element_type=F32)
    return jnp.where(mask, s + (cref - ck_row), NEG)


def _causal(qi, ki, tq):
    r = lax.broadcasted_iota(jnp.int32, (tq, tq), 0) + qi * tq
    c = lax.broadcasted_iota(jnp.int32, (tq, tq), 1) + ki * tq
    return r >= c


def _fox_fwd(q, kv, cum_t, cfg, *, name):
    t, d, hrows = cfg.tokens, cfg.d_model, cum_t.shape[0]
    tq = _blk(cfg.seq, FOX_BLOCK)
    nq = cfg.seq // tq

    def body(q_ref, k_ref, v_ref, cq_ref, ck_ref, o_ref, lse_ref, m_ref, l_ref, acc_ref):
        qi, ki = pl.program_id(1), pl.program_id(2)

        @pl.when(ki == 0)
        def _():
            m_ref[...] = jnp.full_like(m_ref, NEG)
            l_ref[...] = jnp.zeros_like(l_ref)
            acc_ref[...] = jnp.zeros_like(acc_ref)

        @pl.when(ki <= qi)
        def _():
            mask = _causal(qi, ki, tq)
            half = _half_mask((tq, LANES))
            for hp in range(d // LANES):
                sl = slice(hp * LANES, (hp + 1) * LANES)
                q2, k2, v2 = q_ref[:, sl], k_ref[:, sl], v_ref[:, sl]
                alphas, pvs = [], []
                for e in range(2):
                    h = 2 * hp + e
                    s = _fox_scores(q2, k2, e, half, mask, cq_ref[h:h + 1, 0:1], ck_ref[h:h + 1, :])
                    m_prev = m_ref[h]
                    m_new = jnp.maximum(m_prev, jnp.max(s, axis=1, keepdims=True))
                    alpha = jnp.exp(m_prev - m_new)
                    p = jnp.exp(s - m_new[:, 0:1])
                    l_ref[h] = alpha * l_ref[h] + jnp.sum(p, axis=1, keepdims=True)
                    m_ref[h] = m_new
                    alphas.append(alpha)
                    pvs.append(_dot2(p, v2))
                acc = acc_ref[:, sl]
                acc_ref[:, sl] = jnp.where(half, alphas[0] * acc + pvs[0], alphas[1] * acc + pvs[1])

        @pl.when(ki == qi)
        def _():
            half = _half_mask((tq, LANES))
            for hp in range(d // LANES):
                sl = slice(hp * LANES, (hp + 1) * LANES)
                h0, h1 = 2 * hp, 2 * hp + 1
                inv = jnp.where(half, 1.0 / l_ref[h0], 1.0 / l_ref[h1])
                o_ref[:, sl] = acc_ref[:, sl] * inv
                lse0 = m_ref[h0] + jnp.log(l_ref[h0]) - cq_ref[h0:h0 + 1, 0:1]
                lse1 = m_ref[h1] + jnp.log(l_ref[h1]) - cq_ref[h1:h1 + 1, 0:1]
                lse_ref[:, sl] = jnp.where(half, lse0, lse1)

    def qrow(b, qi, ki):
        return (b * nq + qi, 0)

    def krow(b, qi, ki):
        return (b * nq + jnp.minimum(ki, qi), 0)

    o, lse = pl.pallas_call(
        body, name=name, grid=(cfg.batch, nq, nq),
        in_specs=[pl.BlockSpec((tq, d), qrow),
                  pl.BlockSpec((tq, d), krow),
                  pl.BlockSpec((tq, d), lambda b, qi, ki: (b * nq + jnp.minimum(ki, qi), 1)),
                  pl.BlockSpec((hrows, tq), lambda b, qi, ki: (0, b * nq + qi)),
                  pl.BlockSpec((hrows, tq), lambda b, qi, ki: (0, b * nq + jnp.minimum(ki, qi)))],
        out_specs=[pl.BlockSpec((tq, d), qrow), pl.BlockSpec((tq, d), qrow)],
        out_shape=[jax.ShapeDtypeStruct((t, d), F32), jax.ShapeDtypeStruct((t, d), F32)],
        scratch_shapes=[pltpu.VMEM((cfg.heads, tq, LANES), F32), pltpu.VMEM((cfg.heads, tq, LANES), F32),
                        pltpu.VMEM((tq, d), F32)],
        compiler_params=_params(("parallel", "parallel", "arbitrary")),
    )(q, kv, kv, cum_t, cum_t)
    return o, lse


def _fox_bwd_q(q, kv, cum_t, do, lse, dsum, cfg, *, name):
    t, d, hrows = cfg.tokens, cfg.d_model, cum_t.shape[0]
    tq = _blk(cfg.seq, FOX_BLOCK)
    nq = cfg.seq // tq

    def body(q_ref, k_ref, v_ref, cq_ref, ck_ref, do_ref, l_ref, ds_ref, dq_ref, acc_ref):
        qi, ki = pl.program_id(1), pl.program_id(2)

        @pl.when(ki == 0)
        def _():
            acc_ref[...] = jnp.zeros_like(acc_ref)

        @pl.when(ki <= qi)
        def _():
            mask = _causal(qi, ki, tq)
            half = _half_mask((tq, LANES))
            for hp in range(d // LANES):
                sl = slice(hp * LANES, (hp + 1) * LANES)
                q2, k2, v2 = q_ref[:, sl], k_ref[:, sl], v_ref[:, sl]
                do2 = do_ref[:, sl].astype(BF16)
                dqs = []
                for e in range(2):
                    h = 2 * hp + e
                    lane0 = hp * LANES + e * HEAD_DIM
                    cref = cq_ref[h:h + 1, 0:1]
                    s = _fox_scores(q2, k2, e, half, mask, cref, ck_ref[h:h + 1, :])
                    p = jnp.exp(s - (l_ref[:, lane0:lane0 + 1] + cref))
                    doe = jnp.where(half == (e == 0), do2, jnp.zeros_like(do2))
                    dp = lax.dot_general(doe, v2, _NT, preferred_element_type=F32)
                    dsc = (p * (dp - ds_ref[:, lane0:lane0 + 1])).astype(BF16)
                    dqs.append(jnp.dot(dsc, k2, preferred_element_type=F32))
                acc_ref[:, sl] += jnp.where(half, dqs[0], dqs[1])

        @pl.when(ki == qi)
        def _():
            dq_ref[...] = acc_ref[...]

    def qrow(b, qi, ki):
        return (b * nq + qi, 0)

    return pl.pallas_call(
        body, name=name, grid=(cfg.batch, nq, nq),
        in_specs=[pl.BlockSpec((tq, d), qrow),
                  pl.BlockSpec((tq, d), lambda b, qi, ki: (b * nq + jnp.minimum(ki, qi), 0)),
                  pl.BlockSpec((tq, d), lambda b, qi, ki: (b * nq + jnp.minimum(ki, qi), 1)),
                  pl.BlockSpec((hrows, tq), lambda b, qi, ki: (0, b * nq + qi)),
                  pl.BlockSpec((hrows, tq), lambda b, qi, ki: (0, b * nq + jnp.minimum(ki, qi))),
                  pl.BlockSpec((tq, d), qrow), pl.BlockSpec((tq, d), qrow), pl.BlockSpec((tq, d), qrow)],
        out_specs=pl.BlockSpec((tq, d), qrow),
        out_shape=jax.ShapeDtypeStruct((t, d), F32),
        scratch_shapes=[pltpu.VMEM((tq, d), F32)],
        compiler_params=_params(("parallel", "parallel", "arbitrary")),
    )(q, kv, kv, cum_t, cum_t, do, lse, dsum)


def _fox_bwd_kv(q, kv, cum_t, do, lse, dsum, cfg, *, name):
    t, d, hrows = cfg.tokens, cfg.d_model, cum_t.shape[0]
    tq = _blk(cfg.seq, FOX_BLOCK)
    nq = cfg.seq // tq

    def body(q_ref, k_ref, v_ref, cq_ref, ck_ref, do_ref, l_ref, ds_ref, dk_ref, dv_ref, dc_ref,
             kacc_ref, vacc_ref, cacc_ref):
        ki, qi = pl.program_id(1), pl.program_id(2)

        @pl.when(qi == 0)
        def _():
            kacc_ref[...] = jnp.zeros_like(kacc_ref)
            vacc_ref[...] = jnp.zeros_like(vacc_ref)
            cacc_ref[...] = jnp.zeros_like(cacc_ref)

        @pl.when(qi >= ki)
        def _():
            mask = _causal(qi, ki, tq)
            half = _half_mask((tq, LANES))
            for hp in range(d // LANES):
                sl = slice(hp * LANES, (hp + 1) * LANES)
                q2, k2, v2 = q_ref[:, sl], k_ref[:, sl], v_ref[:, sl]
                do2 = do_ref[:, sl].astype(BF16)
                dks, dvs = [], []
                for e in range(2):
                    h = 2 * hp + e
                    lane0 = hp * LANES + e * HEAD_DIM
                    cref = cq_ref[h:h + 1, 0:1]
                    s = _fox_scores(q2, k2, e, half, mask, cref, ck_ref[h:h + 1, :])
                    p = jnp.exp(s - (l_ref[:, lane0:lane0 + 1] + cref))
                    doe = jnp.where(half == (e == 0), do2, jnp.zeros_like(do2))
                    dp = lax.dot_general(doe, v2, _NT, preferred_element_type=F32)
                    dsf = p * (dp - ds_ref[:, lane0:lane0 + 1])
                    cacc_ref[h:h + 1, :] -= jnp.sum(dsf, axis=0, keepdims=True)
                    dks.append(lax.dot_general(dsf.astype(BF16), q2, _TN, preferred_element_type=F32))
                    dvs.append(lax.dot_general(p.astype(BF16), do2, _TN, preferred_element_type=F32))
                kacc_ref[:, sl] += jnp.where(half, dks[0], dks[1])
                vacc_ref[:, sl] += jnp.where(half, dvs[0], dvs[1])

        @pl.when(qi == nq - 1)
        def _():
            dk_ref[...] = kacc_ref[...]
            dv_ref[...] = vacc_ref[...]
            dc_ref[...] = cacc_ref[...]

    def qrow(b, ki, qi):
        return (b * nq + jnp.maximum(qi, ki), 0)

    def krow(b, ki, qi):
        return (b * nq + ki, 0)

    return pl.pallas_call(
        body, name=name, grid=(cfg.batch, nq, nq),
        in_specs=[pl.BlockSpec((tq, d), qrow),
                  pl.BlockSpec((tq, d), krow),
                  pl.BlockSpec((tq, d), lambda b, ki, qi: (b * nq + ki, 1)),
                  pl.BlockSpec((hrows, tq), lambda b, ki, qi: (0, b * nq + jnp.maximum(qi, ki))),
                  pl.BlockSpec((hrows, tq), lambda b, ki, qi: (0, b * nq + ki)),
                  pl.BlockSpec((tq, d), qrow), pl.BlockSpec((tq, d), qrow), pl.BlockSpec((tq, d), qrow)],
        out_specs=[pl.BlockSpec((tq, d), krow), pl.BlockSpec((tq, d), krow),
                   pl.BlockSpec((hrows, tq), lambda b, ki, qi: (0, b * nq + ki))],
        out_shape=[jax.ShapeDtypeStruct((t, d), F32), jax.ShapeDtypeStruct((t, d), F32),
                   jax.ShapeDtypeStruct((hrows, t), F32)],
        scratch_shapes=[pltpu.VMEM((tq, d), F32), pltpu.VMEM((tq, d), F32), pltpu.VMEM((hrows, tq), F32)],
        compiler_params=_params(("parallel", "parallel", "arbitrary")),
    )(q, kv, kv, cum_t, cum_t, do, lse, dsum)


AUG = LANES
BIAS_TERMS = 3


def _fox_aug_q(qp, cfg):
    t, hh = cfg.tokens, cfg.heads
    q3 = qp.reshape(t, hh, HEAD_DIM)
    ones = jnp.ones((t, hh, BIAS_TERMS), BF16)
    zeros = jnp.zeros((t, hh, AUG - HEAD_DIM - BIAS_TERMS), BF16)
    return jnp.concatenate([q3, ones, zeros], axis=2).reshape(t, hh * AUG).T


def _fox_aug_k(k, cum, cfg):
    t, hh = cfg.tokens, cfg.heads
    c = -cum
    hi = lax.reduce_precision(c, 8, 7)
    mid = lax.reduce_precision(c - hi, 8, 7)
    lo = c - hi - mid
    zeros = jnp.zeros((t, hh, AUG - HEAD_DIM - BIAS_TERMS), BF16)
    parts = [k.reshape(t, hh, HEAD_DIM)] + [z.astype(BF16)[..., None] for z in (hi, mid, lo)] + [zeros]
    return jnp.concatenate(parts, axis=2).reshape(t, hh * AUG)


def _keys_visible(tq):
    s = lax.broadcasted_iota(jnp.int32, (tq, tq), 0)
    t = lax.broadcasted_iota(jnp.int32, (tq, tq), 1)
    return s <= t


def _fox_fwd_t(qa_t, k_aug, v_t, cfg, *, name):
    t, d, hh = cfg.tokens, cfg.d_model, cfg.heads
    tq = _blk(cfg.seq, FOX_BLOCK)
    nq = cfg.seq // tq

    def body(qa_ref, ka_ref, vt_ref, o_ref, lse_ref, m_ref, l_ref, acc_ref):
        qi, ki = pl.program_id(1), pl.program_id(2)

        @pl.when(ki == 0)
        def _():
            m_ref[...] = jnp.full_like(m_ref, NEG)
            l_ref[...] = jnp.zeros_like(l_ref)
            acc_ref[...] = jnp.zeros_like(acc_ref)

        def step(diagonal):
            for h in range(hh):
                rows = slice(h * HEAD_DIM, (h + 1) * HEAD_DIM)
                s = jnp.dot(ka_ref[:, h * AUG:(h + 1) * AUG], qa_ref[h * AUG:(h + 1) * AUG, :],
                            preferred_element_type=F32)
                if diagonal:
                    s = jnp.where(_keys_visible(tq), s, NEG)
                m_prev = m_ref[h:h + 1, :]
                m_new = jnp.maximum(m_prev, jnp.max(s, axis=0, keepdims=True))
                alpha = jnp.exp(m_prev - m_new)
                p = jnp.exp(s - m_new)
                l_ref[h:h + 1, :] = alpha * l_ref[h:h + 1, :] + jnp.sum(p, axis=0, keepdims=True)
                m_ref[h:h + 1, :] = m_new
                hi = p.astype(BF16)
                lo = (p - hi.astype(F32)).astype(BF16)
                vt = vt_ref[rows, :]
                acc_ref[rows, :] = (alpha * acc_ref[rows, :] + jnp.dot(vt, hi, preferred_element_type=F32)
                                    + jnp.dot(vt, lo, preferred_element_type=F32))

        pl.when(ki < qi)(functools.partial(step, False))
        pl.when(ki == qi)(functools.partial(step, True))

        @pl.when(ki == qi)
        def _():
            for h in range(hh):
                rows = slice(h * HEAD_DIM, (h + 1) * HEAD_DIM)
                o_ref[rows, :] = acc_ref[rows, :] * (1.0 / l_ref[h:h + 1, :])
            lse_ref[...] = m_ref[...] + jnp.log(l_ref[...])

    def qcol(b, qi, ki):
        return (0, b * nq + qi)

    return pl.pallas_call(
        body, name=name, grid=(cfg.batch, nq, nq),
        in_specs=[pl.BlockSpec((hh * AUG, tq), qcol),
                  pl.BlockSpec((tq, hh * AUG), lambda b, qi, ki: (b * nq + jnp.minimum(ki, qi), 0)),
                  pl.BlockSpec((d, tq), lambda b, qi, ki: (0, b * nq + jnp.minimum(ki, qi)))],
        out_specs=[pl.BlockSpec((d, tq), qcol), pl.BlockSpec((hh, tq), qcol)],
        out_shape=[jax.ShapeDtypeStruct((d, t), F32), jax.ShapeDtypeStruct((hh, t), F32)],
        scratch_shapes=[pltpu.VMEM((hh, tq), F32), pltpu.VMEM((hh, tq), F32), pltpu.VMEM((d, tq), F32)],
        compiler_params=_params(("parallel", "parallel", "arbitrary")),
    )(qa_t, k_aug, v_t)


def _head_dot_t(a_t, b_t, cfg, *, name):
    t, d, hh = cfg.tokens, cfg.d_model, cfg.heads
    tc = _blk(t, 2 * ROW_BLOCK)

    def body(a_ref, b_ref, o_ref):
        for h in range(hh):
            rows = slice(h * HEAD_DIM, (h + 1) * HEAD_DIM)
            o_ref[h:h + 1, :] = jnp.sum(a_ref[rows, :].astype(F32) * b_ref[rows, :], axis=0, keepdims=True)

    return pl.pallas_call(
        body, name=name, grid=(t // tc,),
        in_specs=[pl.BlockSpec((d, tc), lambda i: (0, i)), pl.BlockSpec((d, tc), lambda i: (0, i))],
        out_specs=pl.BlockSpec((hh, tc), lambda i: (0, i)),
        out_shape=jax.ShapeDtypeStruct((hh, t), F32),
        compiler_params=_params(("parallel",)),
    )(a_t, b_t)


def _fox_bwd_t(qa_t, k_aug, k_t, v, do_t, do, lse, dsum, cfg, *, name):
    t, d, hh = cfg.tokens, cfg.d_model, cfg.heads
    tq = _blk(cfg.seq, FOX_BLOCK)
    nq = cfg.seq // tq

    def body(qa_ref, ka_ref, kt_ref, v_ref, dot_ref, do_ref, lse_ref, ds_ref, dq_hbm, dk_ref, dv_ref, dc_ref,
             dq_acc, sem):
        b, ki, qi = pl.program_id(0), pl.program_id(1), pl.program_id(2)
        qq = jnp.maximum(qi, ki)

        @pl.when((ki == 0) & (qi == 0))
        def _():
            dq_acc[...] = jnp.zeros_like(dq_acc)

        @pl.when(qi == 0)
        def _():
            dk_ref[...] = jnp.zeros_like(dk_ref)
            dv_ref[...] = jnp.zeros_like(dv_ref)
            dc_ref[...] = jnp.zeros_like(dc_ref)

        def step(diagonal):
            upper = lax.broadcasted_iota(jnp.int32, (LANES, tq), 0) < HEAD_DIM
            half = _half_mask((tq, LANES))
            for hp in range(hh // 2):
                pair = slice(hp * LANES, (hp + 1) * LANES)
                dvs = []
                for e in range(2):
                    h = 2 * hp + e
                    rows = slice(h * HEAD_DIM, (h + 1) * HEAD_DIM)
                    aug = slice(h * AUG, (h + 1) * AUG)
                    s = jnp.dot(ka_ref[:, aug], qa_ref[aug, :], preferred_element_type=F32)
                    if diagonal:
                        s = jnp.where(_keys_visible(tq), s, NEG)
                    p = jnp.exp(s - lse_ref[h:h + 1, :])
                    dot2 = dot_ref[pair, :]
                    dote = jnp.where(upper == (e == 0), dot2, jnp.zeros_like(dot2))
                    dp = jnp.dot(v_ref[:, pair], dote, preferred_element_type=F32)
                    dsf = p * (dp - ds_ref[h:h + 1, :])
                    dc_ref[:, h:h + 1] -= jnp.sum(dsf, axis=1, keepdims=True)
                    dsc = dsf.astype(BF16)
                    dvs.append(jnp.dot(p.astype(BF16), do_ref[:, pair], preferred_element_type=F32))
                    dk_ref[:, aug] += lax.dot_general(dsc, qa_ref[aug, :], _NT, preferred_element_type=F32)
                    dq_acc[qq, rows, :] += jnp.dot(kt_ref[rows, :], dsc, preferred_element_type=F32)
                dv_ref[:, pair] += jnp.where(half, dvs[0], dvs[1])

        pl.when(qi > ki)(functools.partial(step, False))
        pl.when(qi == ki)(functools.partial(step, True))

        @pl.when((ki == nq - 1) & (qi == nq - 1))
        def _():
            cp = pltpu.make_async_copy(dq_acc, dq_hbm.at[b], sem)
            cp.start()
            cp.wait()

    def qcol(b, ki, qi):
        return (0, b * nq + jnp.maximum(qi, ki))

    def krow(b, ki, qi):
        return (b * nq + ki, 0)

    return pl.pallas_call(
        body, name=name, grid=(cfg.batch, nq, nq),
        in_specs=[pl.BlockSpec((hh * AUG, tq), qcol),
                  pl.BlockSpec((tq, hh * AUG), krow),
                  pl.BlockSpec((d, tq), lambda b, ki, qi: (0, b * nq + ki)),
                  pl.BlockSpec((tq, d), krow),
                  pl.BlockSpec((d, tq), qcol),
                  pl.BlockSpec((tq, d), lambda b, ki, qi: (b * nq + jnp.maximum(qi, ki), 0)),
                  pl.BlockSpec((hh, tq), qcol), pl.BlockSpec((hh, tq), qcol)],
        out_specs=[pl.BlockSpec(memory_space=pl.ANY), pl.BlockSpec((tq, hh * AUG), krow),
                   pl.BlockSpec((tq, d), krow), pl.BlockSpec((tq, LANES), krow)],
        out_shape=[jax.ShapeDtypeStruct((cfg.batch, nq, d, tq), F32), jax.ShapeDtypeStruct((t, hh * AUG), F32),
                   jax.ShapeDtypeStruct((t, d), F32), jax.ShapeDtypeStruct((t, LANES), F32)],
        scratch_shapes=[pltpu.VMEM((nq, d, tq), F32), pltpu.SemaphoreType.DMA],
        compiler_params=_params(("arbitrary", "arbitrary", "arbitrary")),
    )(qa_t, k_aug, k_t, v, do_t, do, lse, dsum)


WIDE = 1536


def _fwd(a, w, *, name, res=None, scale=1.0):
    return _mm(a, w, form="F", out_dtype=F32, name=name, bn=WIDE, bk=WIDE, res=res, scale=scale)


def _bwd(dy, w, *, name, scale=1.0):
    return _mm(dy, w, form="B", out_dtype=F32, name=name, bn=WIDE, bk=WIDE, scale=scale)


def _wgrad(a, dy, w, *, name, scale=1.0):
    return _mm_grad(a, dy, w.shape[0], name=name, bm=WIDE, bn=WIDE, scale=scale)


def _ffn_fwd(h, g, w_in, w_out, tag):
    n = _rms_fwd(h, g, name=f"{tag}_norm")
    u = _fwd(n, w_in, name=f"{tag}_in")
    a = _swiglu_fwd(u, name=f"{tag}_act")
    return _fwd(a, w_out, name=f"{tag}_out", res=h, scale=0.5), (n, u, a)


def _ffn_bwd(dh_out, h, g, w_in, w_out, saved, tag):
    n, u, a = saved
    da = _bwd(dh_out, w_out, name=f"{tag}_out_dx", scale=0.5)
    dw_out = _wgrad(a, dh_out, w_out, name=f"{tag}_out_dw", scale=0.5)
    du = _swiglu_bwd(u, da, name=f"{tag}_act_bwd")
    dn = _bwd(du, w_in, name=f"{tag}_in_dx")
    dw_in = _wgrad(n, du, w_in, name=f"{tag}_in_dw")
    dh, dg = _rms_bwd(h, g, dn, dh_out, name=f"{tag}_norm_bwd")
    return dh, dg, dw_in, dw_out


def _head_gain(g, heads, scale=1.0):
    return jnp.tile(g.astype(F32) * scale, heads)


def _local_step(cfg, x, positions, target, w, s):
    d, hh = cfg.d_model, cfg.heads
    cos, sin = _rope_tables(positions)
    ones = jnp.ones((d,), F32)

    h1, ffn0 = _ffn_fwd(x, s["ffn_norm"][0, 0], w["ffn_w_in"][0][0], w["ffn_w_out"][0][0], "ffn00")
    hn_a = _rms_fwd(h1, s["mix_norm"][0], name="a_norm")
    qkv = _fwd(hn_a, w["a_w_qkv"], name="a_qkv")
    kinds_a = ["rope", "rope", "cast"] * len(DILATIONS)
    gains_a = jnp.stack([z for g in range(len(DILATIONS)) for z in (
        _head_gain(s["a_q_norm"][g], hh, Q_SCALE), _head_gain(s["a_k_norm"][g], hh), ones)])
    qkvp = [_hn_fwd(qkv, gains_a[3 * g:3 * g + 3], kinds_a[:3], d, cos, sin, name=f"a_qk_norm{g}", col0=3 * g)
            for g in range(len(DILATIONS))]
    lay = [_band_layouts(qkvp[g], dil, cfg) for g, dil in enumerate(DILATIONS)]
    band = [_band_fwd_t(*lay[g], dil, cfg, name=f"a_band{g}") for g, dil in enumerate(DILATIONS)]
    mixed, lse_a = _mix_fwd([_from_classes_t(o) for o, _ in band],
                            [jnp.repeat(_from_classes_t(l), HEAD_DIM, axis=1) for _, l in band], name="a_mix")
    h2 = _fwd(mixed, w["a_w_o"], name="a_out", res=h1)
    h3, ffn1 = _ffn_fwd(h2, s["ffn_norm"][0, 1], w["ffn_w_in"][0][1], w["ffn_w_out"][0][1], "ffn01")

    kn = _rms_fwd(h3, s["kv_norm"], name="kv_norm")
    proj = _fwd(kn, w["kv_w"], name="kv_proj")
    kinds_kv = ["norm", "cast"]
    gains_kv = jnp.stack([_head_gain(s["kv_k_norm"], hh), ones])
    kvp = _hn_fwd(proj, gains_kv, kinds_kv, d, cos, sin, name="kv_k_norm")
    gate_col = 2 * d // LANES
    bias = jnp.pad(s["kv_b_f"].astype(F32), (0, LANES - hh))
    cum = _gate_fwd(proj, gate_col, bias, cfg, name="kv_gate")
    k_b, v_b = kvp[:, :d], kvp[:, d:]
    k_aug = _fox_aug_k(k_b, cum[:, :hh], cfg)

    h4, ffn2 = _ffn_fwd(h3, s["ffn_norm"][1, 0], w["ffn_w_in"][1][0], w["ffn_w_out"][1][0], "ffn10")
    hn_b = _rms_fwd(h4, s["mix_norm"][1], name="b_norm")
    qraw = _fwd(hn_b, w["b_w_q"], name="b_q")
    gains_b = _head_gain(s["b_q_norm"][0], hh, Q_SCALE)[None]
    qp = _hn_fwd(qraw, gains_b, ["norm"], d, cos, sin, name="b_q_norm")
    qa_t = _fox_aug_q(qp, cfg)
    o_t, lse_b = _fox_fwd_t(qa_t, k_aug, v_b.T, cfg, name="b_fox")
    o_b = o_t.T
    h5 = _fwd(o_b, w["b_w_o"], name="b_out", res=h4)
    h6, ffn3 = _ffn_fwd(h5, s["ffn_norm"][1, 1], w["ffn_w_in"][1][1], w["ffn_w_out"][1][1], "ffn11")

    loss, dh6 = _loss_fwd_bwd(h6, target, name="loss")

    dh5, dg11, dwi11, dwo11 = _ffn_bwd(dh6, h5, s["ffn_norm"][1, 1], w["ffn_w_in"][1][1], w["ffn_w_out"][1][1],
                                       ffn3, "ffn11")
    do_b = _bwd(dh5, w["b_w_o"], name="b_out_dx")
    dw_bo = _wgrad(o_b, dh5, w["b_w_o"], name="b_out_dw")
    do_bf = do_b.astype(BF16)
    do_t = do_bf.T
    dsum_b = _head_dot_t(do_t, o_t, cfg, name="b_dsum")
    dq4, dk_aug, dv_b, dcum = _fox_bwd_t(qa_t, k_aug, k_b.T, v_b, do_t, do_bf, lse_b, dsum_b, cfg,
                                         name="b_fox_bwd")
    dq_b = dq4.transpose(0, 1, 3, 2).reshape(cfg.tokens, d)
    dk_b = dk_aug.reshape(cfg.tokens, hh, AUG)[:, :, :HEAD_DIM].reshape(cfg.tokens, d)
    dqraw, dgq = _hn_bwd(qraw, dq_b, gains_b, ["norm"], d, cos, sin, name="b_q_norm_bwd")
    dhn_b = _bwd(dqraw, w["b_w_q"], name="b_q_dx")
    dw_bq = _wgrad(hn_b, dqraw, w["b_w_q"], name="b_q_dw")
    dh4, dmix1 = _rms_bwd(h4, s["mix_norm"][1], dhn_b, dh5, name="b_norm_bwd")
    dh3, dg10, dwi10, dwo10 = _ffn_bwd(dh4, h3, s["ffn_norm"][1, 0], w["ffn_w_in"][1][0], w["ffn_w_out"][1][0],
                                       ffn2, "ffn10")

    dkvraw, dgk = _hn_bwd(proj, jnp.concatenate([dk_b, dv_b], axis=1), gains_kv, kinds_kv, d, cos, sin,
                          name="kv_k_norm_bwd")
    dz, dbias = _gate_bwd(proj, gate_col, bias, dcum, cfg, name="kv_gate_bwd")
    pad_cols = w["kv_w"].shape[2] - 2 * d - LANES
    dproj = jnp.concatenate([dkvraw, dz.astype(BF16), jnp.zeros((cfg.tokens, pad_cols), BF16)], axis=1)
    dkn = _bwd(dproj, w["kv_w"], name="kv_proj_dx")
    dw_kv = _wgrad(kn, dproj, w["kv_w"], name="kv_proj_dw")
    dh3, dkvn = _rms_bwd(h3, s["kv_norm"], dkn, dh3, name="kv_norm_bwd")

    dh2, dg01, dwi01, dwo01 = _ffn_bwd(dh3, h2, s["ffn_norm"][0, 1], w["ffn_w_in"][0][1], w["ffn_w_out"][0][1],
                                       ffn1, "ffn01")
    dmixed = _bwd(dh2, w["a_w_o"], name="a_out_dx")
    dw_ao = _wgrad(mixed, dh2, w["a_w_o"], name="a_out_dw")
    dsum_a = _head_dot(dmixed, mixed, name="a_dsum")
    dqkvp = []
    dmixed_bf = dmixed.astype(BF16)
    lse_h, dsum_h = lse_a[:, ::HEAD_DIM], dsum_a[:, ::HEAD_DIM]
    for g, dil in enumerate(DILATIONS):
        dq_t, dk_a, dv_a = _band_bwd_t(
            *lay[g], _to_classes_t(dmixed_bf, dil, d), dmixed_bf.reshape(cfg.tokens // dil, dil * d),
            _to_classes_t(lse_h, dil, hh), _to_classes_t(dsum_h, dil, hh), dil, cfg, name=f"a_band{g}_bwd")
        dqkvp += [_from_classes_t(dq_t), dk_a.reshape(cfg.tokens, d), dv_a.reshape(cfg.tokens, d)]
    dqkv, dga = _hn_bwd(qkv, jnp.concatenate(dqkvp, axis=1), gains_a, kinds_a, d, cos, sin, name="a_qk_norm_bwd")
    dhn_a = _bwd(dqkv, w["a_w_qkv"], name="a_qkv_dx")
    dw_qkv = _wgrad(hn_a, dqkv, w["a_w_qkv"], name="a_qkv_dw")
    dh1, dmix0 = _rms_bwd(h1, s["mix_norm"][0], dhn_a, dh2, name="a_norm_bwd")
    dx, dg00, dwi00, dwo00 = _ffn_bwd(dh1, x, s["ffn_norm"][0, 0], w["ffn_w_in"][0][0], w["ffn_w_out"][0][0],
                                      ffn0, "ffn00")

    dw = {
        "ffn_w_in": [[dwi00, dwi01], [dwi10, dwi11]],
        "ffn_w_out": [[dwo00, dwo01], [dwo10, dwo11]],
        "a_w_qkv": dw_qkv, "a_w_o": dw_ao, "kv_w": dw_kv, "b_w_q": dw_bq, "b_w_o": dw_bo,
    }
    ds = {
        "ffn_norm": jnp.stack([jnp.stack([dg00, dg01]), jnp.stack([dg10, dg11])]),
        "mix_norm": jnp.stack([dmix0, dmix1]),
        "a_q_norm": jnp.stack([dga[3 * g] for g in range(len(DILATIONS))])[None] * Q_SCALE,
        "a_k_norm": jnp.stack([dga[3 * g + 1] for g in range(len(DILATIONS))])[None],
        "kv_norm": dkvn,
        "kv_b_f": dbias[:hh],
        "kv_k_norm": dgk[0],
        "b_q_norm": dgq * Q_SCALE,
    }
    return loss, dx, dw, ds


MESH_ID = pl.DeviceIdType.MESH
ANY = pl.BlockSpec(memory_space=pl.ANY)
PACK_COLS = 1024
PACK_ROW_ALIGN = 32


def _me():
    return lax.axis_index("x"), lax.axis_index("y"), lax.axis_index("c")


def _other_chips(x, y):
    return [(1 - x, y), (x, 1 - y), (1 - x, 1 - y)]


def _all_gather_small(v, *, name):
    r = v.shape[0]

    def body(v_ref, out_ref, send_sems, recv_sems):
        x, y, c = _me()
        me = 4 * x + 2 * y + c
        out_ref[me] = v_ref[...]
        copies = []
        for k in range(1, N_DEV):
            fx, fy, fc = (k >> 2) & 1, (k >> 1) & 1, k & 1
            peer = (1 - x if fx else x, 1 - y if fy else y, 1 - c if fc else c)
            copies.append(pltpu.make_async_remote_copy(
                src_ref=v_ref, dst_ref=out_ref.at[me], send_sem=send_sems.at[k - 1], recv_sem=recv_sems.at[k - 1],
                device_id=peer, device_id_type=MESH_ID))
        for cp in copies:
            cp.start()
        for cp in copies:
            cp.wait()

    return pl.pallas_call(
        body, name=name,
        in_specs=[pl.BlockSpec(memory_space=pltpu.VMEM)], out_specs=pl.BlockSpec(memory_space=pltpu.VMEM),
        out_shape=jax.ShapeDtypeStruct((N_DEV, r, LANES), v.dtype),
        scratch_shapes=[pltpu.SemaphoreType.DMA((N_DEV - 1,)), pltpu.SemaphoreType.DMA((N_DEV - 1,))],
    )(v)


def _all_gather_chips(v, *, name):
    rh = v.shape[0] // 2

    def body(v_ref, out_ref, send_sems, recv_sems):
        x, y, c = _me()
        j = 2 * x + y
        chips = _other_chips(x, y)

        def half(chip, core):
            return out_ref.at[chip, pl.ds(core * rh, rh)]

        first = [pltpu.make_async_remote_copy(
            src_ref=v_ref.at[pl.ds(c * rh, rh)], dst_ref=half(j, c), send_sem=send_sems.at[k],
            recv_sem=recv_sems.at[k], device_id=(px, py, c), device_id_type=MESH_ID)
            for k, (px, py) in enumerate(chips)]
        for cp in first:
            cp.start()
        passed = [pltpu.make_async_remote_copy(
            src_ref=half(2 * px + py, c), dst_ref=half(2 * px + py, c), send_sem=send_sems.at[3 + k],
            recv_sem=recv_sems.at[3 + k], device_id=(x, y, 1 - c), device_id_type=MESH_ID)
            for k, (px, py) in enumerate(chips)]
        for k in range(len(chips)):
            first[k].wait_recv()
            passed[k].start()
        for k, (px, py) in enumerate(chips):
            pltpu.make_async_remote_copy(
                src_ref=half(2 * px + py, 1 - c), dst_ref=half(2 * px + py, 1 - c), send_sem=send_sems.at[3 + k],
                recv_sem=recv_sems.at[3 + k], device_id=(x, y, 1 - c), device_id_type=MESH_ID).wait_recv()
        for cp in first + passed:
            cp.wait_send()

    return pl.pallas_call(
        body, name=name, in_specs=[ANY], out_specs=ANY,
        out_shape=jax.ShapeDtypeStruct((N_CHIPS,) + v.shape, v.dtype),
        scratch_shapes=[pltpu.SemaphoreType.DMA((2 * (N_CHIPS - 1),)), pltpu.SemaphoreType.DMA((2 * (N_CHIPS - 1),))],
    )(v)


def _swap_halves(g, *, name):
    n, r, cols = g.shape
    rh = r // 2

    def body(g_ref, out_ref, send_sem, recv_sem):
        x, y, c = _me()
        cp = pltpu.make_async_remote_copy(
            src_ref=g_ref.at[:, pl.ds((1 - c) * rh, rh)], dst_ref=out_ref, send_sem=send_sem, recv_sem=recv_sem,
            device_id=(x, y, 1 - c), device_id_type=MESH_ID)
        cp.start()
        cp.wait()

    return pl.pallas_call(
        body, name=name, in_specs=[ANY], out_specs=ANY,
        out_shape=jax.ShapeDtypeStruct((n, rh, cols), g.dtype),
        scratch_shapes=[pltpu.SemaphoreType.DMA, pltpu.SemaphoreType.DMA],
    )(g)


def _scatter_chips(v, *, name):
    def body(v_ref, out_ref, send_sems, recv_sems):
        x, y, c = _me()
        j = 2 * x + y
        copies = [pltpu.make_async_remote_copy(
            src_ref=v_ref.at[2 * px + py], dst_ref=out_ref.at[j], send_sem=send_sems.at[k], recv_sem=recv_sems.at[k],
            device_id=(px, py, c), device_id_type=MESH_ID) for k, (px, py) in enumerate(_other_chips(x, y))]
        for cp in copies:
            cp.start()
        for cp in copies:
            cp.wait()

    return pl.pallas_call(
        body, name=name, in_specs=[ANY], out_specs=ANY,
        out_shape=jax.ShapeDtypeStruct(v.shape, v.dtype),
        scratch_shapes=[pltpu.SemaphoreType.DMA((N_CHIPS - 1,)), pltpu.SemaphoreType.DMA((N_CHIPS - 1,))],
    )(v)


def _join_halves(v, *, name):
    def body(v_ref, out_ref, send_sem, recv_sem):
        x, y, c = _me()
        cp = pltpu.make_async_remote_copy(
            src_ref=v_ref, dst_ref=out_ref.at[c], send_sem=send_sem, recv_sem=recv_sem,
            device_id=(x, y, 1 - c), device_id_type=MESH_ID)
        cp.start()
        cp.wait()

    return pl.pallas_call(
        body, name=name, in_specs=[ANY], out_specs=ANY,
        out_shape=jax.ShapeDtypeStruct((2,) + v.shape, v.dtype),
        scratch_shapes=[pltpu.SemaphoreType.DMA, pltpu.SemaphoreType.DMA],
    )(v)


def _row_blk(rows, want):
    for b in range(min(rows, want) // SUBLANES * SUBLANES, 0, -SUBLANES):
        if rows % b == 0:
            return b
    return rows


def _add_own_half(g, got, *, name):
    n, r, cols = g.shape
    rh = r // 2
    tr = _row_blk(rh, 512)
    nb = rh // tr

    def body(c_ref, g_ref, got_ref, o_ref):
        del c_ref
        o_ref[...] = (g_ref[...] + got_ref[...]).astype(BF16)

    grid_spec = pltpu.PrefetchScalarGridSpec(
        num_scalar_prefetch=1, grid=(n, nb),
        in_specs=[pl.BlockSpec((None, tr, cols), lambda j, i, c: (j, c[0] * nb + i, 0)),
                  pl.BlockSpec((None, tr, cols), lambda j, i, c: (j, i, 0))],
        out_specs=pl.BlockSpec((None, tr, cols), lambda j, i, c: (j, i, 0)))
    return pl.pallas_call(
        body, name=name, grid_spec=grid_spec, out_shape=jax.ShapeDtypeStruct((n, rh, cols), BF16),
        compiler_params=_params(("parallel", "parallel")),
    )(lax.axis_index("c").astype(jnp.int32).reshape(1), g, got)


def _sum_parts(parts, *, name):
    n, r, cols = parts.shape
    tr = _row_blk(r, 512)

    def body(*refs):
        o_ref = refs[n]
        acc = refs[0][...].astype(F32)
        for p_ref in refs[1:n]:
            acc = acc + p_ref[...].astype(F32)
        o_ref[...] = acc

    return pl.pallas_call(
        body, name=name, grid=(r // tr,),
        in_specs=[pl.BlockSpec((None, tr, cols), functools.partial(lambda j, i: (j, i, 0), j)) for j in range(n)],
        out_specs=pl.BlockSpec((tr, cols), lambda i: (i, 0)),
        out_shape=jax.ShapeDtypeStruct((r, cols), F32),
        compiler_params=_params(("parallel",)),
    )(*([parts] * n))


def _adamw(w, m, v, g, *, name):
    shape = w.shape
    cols = shape[-1]
    w2, m2, v2, g2 = (z.reshape(-1, cols) for z in (w, m, v, g))
    rows = w2.shape[0]
    tr = _row_blk(rows, max(SUBLANES, (1 << 20) // (4 * cols)))

    def body(w_ref, m_ref, v_ref, g_ref, d_ref, nm_ref, nv_ref):
        gv = g_ref[...]
        nm = ADAM_B1 * m_ref[...] + (1.0 - ADAM_B1) * gv
        nv = ADAM_B2 * v_ref[...] + (1.0 - ADAM_B2) * jnp.square(gv)
        m_hat = nm / (1.0 - ADAM_B1 ** ADAM_STEP)
        v_hat = nv / (1.0 - ADAM_B2 ** ADAM_STEP)
        d_ref[...] = -ADAM_LR * (m_hat / (jnp.sqrt(v_hat) + ADAM_EPS) + ADAM_WD * w_ref[...])
        nm_ref[...] = nm
        nv_ref[...] = nv

    spec = pl.BlockSpec((tr, cols), lambda i: (i, 0))
    out = jax.ShapeDtypeStruct((rows, cols), F32)
    d, nm, nv = pl.pallas_call(
        body, name=name, grid=(rows // tr,), in_specs=[spec] * 4, out_specs=[spec] * 3, out_shape=[out] * 3,
        compiler_params=_params(("parallel",)),
    )(w2, m2, v2, g2)
    return d.reshape(shape), nm.reshape(shape), nv.reshape(shape)


def _pack_rows(size, cols, align):
    return -(-size // (cols * align)) * align


def _pack(arrs, lead, cols, align, total_align):
    lead_shape = arrs[0].shape[:lead]
    parts = []
    for a in arrs:
        flat = a.reshape(lead_shape + (-1,))
        size = flat.shape[-1]
        rows = _pack_rows(size, cols, align)
        flat = jnp.pad(flat, [(0, 0)] * lead + [(0, rows * cols - size)])
        parts.append(flat.reshape(lead_shape + (rows, cols)))
    total = sum(p.shape[lead] for p in parts)
    extra = -total % total_align
    if extra:
        parts.append(jnp.zeros(lead_shape + (extra, cols), parts[0].dtype))
    return jnp.concatenate(parts, axis=lead)


def _unpack(buf, shapes, lead, cols, align):
    lead_shape = buf.shape[:lead]
    out, row = [], 0
    for shp in shapes:
        size = 1
        for n in shp:
            size *= n
        rows = _pack_rows(size, cols, align)
        piece = lax.slice_in_dim(buf, row, row + rows, axis=lead).reshape(lead_shape + (-1,))
        out.append(piece[..., :size].reshape(lead_shape + tuple(shp)))
        row += rows
    return out


BIG = ("ffn_w_in", "ffn_w_out", "a_w_qkv", "a_w_o", "kv_w", "b_w_q", "b_w_o")
SMALL = ("ffn_norm", "mix_norm", "a_q_norm", "a_k_norm", "kv_norm", "kv_b_f", "kv_k_norm", "b_q_norm")
WEIGHTS = ("ffn_norm", "ffn_w_in", "ffn_w_out", "mix_norm", "a_w_qkv", "a_q_norm", "a_k_norm", "a_w_o",
           "kv_norm", "kv_w", "kv_b_f", "kv_k_norm", "b_w_q", "b_q_norm", "b_w_o")
GATE_PAD = 2 * LANES


def _stack_weights(sh, d):
    depth = sh["ffn_w_in"].shape[1]
    kv = sh["kv_w"].transpose(1, 0, 2).reshape(d, -1)
    kv = jnp.pad(kv, ((0, 0), (0, 2 * d + GATE_PAD - kv.shape[1])))
    return {
        "ffn_w_in": [[sh["ffn_w_in"][:, l, i] for i in range(2)] for l in range(depth)],
        "ffn_w_out": [[sh["ffn_w_out"][:, l, i].reshape(1, -1, d) for i in range(2)] for l in range(depth)],
        "a_w_qkv": sh["a_w_qkv"][:, 0],
        "a_w_o": sh["a_w_o"].reshape(1, d, d),
        "kv_w": kv[None],
        "b_w_q": sh["b_w_q"].reshape(1, d, d),
        "b_w_o": sh["b_w_o"].reshape(1, d, d),
    }


def _unstack_grads(dw, d, heads):
    def rows4(z):
        return z.reshape(N_CHIPS, -1, d)

    kv_cols = 2 * d + heads
    kv = dw["kv_w"][0][:, :kv_cols].reshape(d, N_CHIPS, kv_cols // N_CHIPS).transpose(1, 0, 2)
    return [
        jnp.stack([jnp.stack(row, axis=1) for row in dw["ffn_w_in"]], axis=1),
        jnp.stack([jnp.stack([rows4(z) for z in row], axis=1) for row in dw["ffn_w_out"]], axis=1),
        dw["a_w_qkv"][:, None],
        rows4(dw["a_w_o"])[:, None],
        kv,
        rows4(dw["b_w_q"])[:, None],
        rows4(dw["b_w_o"])[:, None],
    ]


def kernel(x, positions, ffn_norm, ffn_w_in, ffn_w_out, mix_norm, a_w_qkv, a_q_norm, a_k_norm, a_w_o, kv_norm, kv_w, kv_b_f, kv_k_norm, b_w_q, b_q_norm, b_w_o, loss_target, m_ffn_norm, m_ffn_w_in, m_ffn_w_out, m_mix_norm, m_a_w_qkv, m_a_q_norm, m_a_k_norm, m_a_w_o, m_kv_norm, m_kv_w, m_kv_b_f, m_kv_k_norm, m_b_w_q, m_b_q_norm, m_b_w_o, v_ffn_norm, v_ffn_w_in, v_ffn_w_out, v_mix_norm, v_a_w_qkv, v_a_q_norm, v_a_k_norm, v_a_w_o, v_kv_norm, v_kv_w, v_kv_b_f, v_kv_k_norm, v_b_w_q, v_b_q_norm, v_b_w_o):
    wts = dict(zip(WEIGHTS, (ffn_norm, ffn_w_in, ffn_w_out, mix_norm, a_w_qkv, a_q_norm, a_k_norm, a_w_o, kv_norm,
                             kv_w, kv_b_f, kv_k_norm, b_w_q, b_q_norm, b_w_o)))
    mom = dict(zip(WEIGHTS, (m_ffn_norm, m_ffn_w_in, m_ffn_w_out, m_mix_norm, m_a_w_qkv, m_a_q_norm, m_a_k_norm,
                             m_a_w_o, m_kv_norm, m_kv_w, m_kv_b_f, m_kv_k_norm, m_b_w_q, m_b_q_norm, m_b_w_o)))
    var = dict(zip(WEIGHTS, (v_ffn_norm, v_ffn_w_in, v_ffn_w_out, v_mix_norm, v_a_w_qkv, v_a_q_norm, v_a_k_norm,
                             v_a_w_o, v_kv_norm, v_kv_w, v_kv_b_f, v_kv_k_norm, v_b_w_q, v_b_q_norm, v_b_w_o)))
    batch, seq, d = x.shape
    cfg = Cfg(d_model=d, d_ff=ffn_w_out.shape[2] * N_CHIPS, seq=seq, batch=batch)
    chip = 2 * lax.axis_index("x") + lax.axis_index("y")
    big_shapes = [wts[n].shape for n in BIG]

    shard = _pack([wts[n].astype(BF16) for n in BIG], 0, PACK_COLS, PACK_ROW_ALIGN, PACK_COLS)
    gathered = _all_gather_chips(shard, name="gather_weights")
    gathered = lax.dynamic_update_slice_in_dim(gathered, shard[None], chip, axis=0)
    w = _stack_weights(dict(zip(BIG, _unpack(gathered, big_shapes, 1, PACK_COLS, PACK_ROW_ALIGN))), d)
    norm_shard = _pack([ffn_norm], 0, LANES, SUBLANES, SUBLANES)
    norms = _all_gather_small(norm_shard, name="gather_ffn_norm")[0::2]
    (norms,) = _unpack(norms, [ffn_norm.shape], 1, LANES, SUBLANES)
    small = {"ffn_norm": jnp.moveaxis(norms, 0, 2).reshape(ffn_norm.shape[:2] + (d,)),
             "mix_norm": mix_norm, "a_q_norm": a_q_norm[0], "a_k_norm": a_k_norm[0], "kv_norm": kv_norm,
             "kv_b_f": kv_b_f, "kv_k_norm": kv_k_norm, "b_q_norm": b_q_norm}

    loss, dx, dw, ds = _local_step(cfg, x.reshape(cfg.tokens, d), positions.reshape(cfg.tokens),
                                   loss_target.reshape(cfg.tokens, d), w, small)
    loss = lax.psum(loss, ("x", "y", "c"))

    g = _pack(_unstack_grads(dw, d, cfg.heads), 1, PACK_COLS, PACK_ROW_ALIGN, PACK_COLS)
    chip_half = _add_own_half(g, _swap_halves(g, name="swap_halves"), name="add_halves")
    parts = _scatter_chips(chip_half, name="scatter_chips")
    parts = lax.dynamic_update_slice_in_dim(parts, lax.dynamic_slice_in_dim(chip_half, chip, 1, axis=0), chip, axis=0)
    mine = _sum_parts(parts, name="sum_chips")
    both = _join_halves(mine, name="join_halves")
    g_big = lax.dynamic_update_slice_in_dim(both, mine[None], lax.axis_index("c"), axis=0).reshape(g.shape[1:])
    grads = dict(zip(BIG, _unpack(g_big, big_shapes, 0, PACK_COLS, PACK_ROW_ALIGN)))

    small_shapes = [ds[n].shape for n in SMALL]
    parts = _all_gather_small(_pack([ds[n] for n in SMALL], 0, LANES, SUBLANES, SUBLANES), name="gather_small")
    g_small = dict(zip(SMALL, _unpack(_sum_parts(parts, name="sum_small"), small_shapes, 0, LANES, SUBLANES)))
    quarter = d // N_CHIPS
    g_small["ffn_norm"] = lax.dynamic_slice_in_dim(g_small["ffn_norm"], chip * quarter, quarter, axis=2)
    grads.update(g_small)

    delta, new_m, new_v = {}, {}, {}
    for n in BIG:
        delta[n], new_m[n], new_v[n] = _adamw(wts[n], mom[n], var[n], grads[n], name=f"adamw_{n}")
    packed = [_pack([z[n] for n in SMALL], 0, LANES, SUBLANES, SUBLANES) for z in (wts, mom, var, grads)]
    small_out = _adamw(*packed, name="adamw_small")
    shard_shapes = [wts[n].shape for n in SMALL]
    for out, res in zip((delta, new_m, new_v), small_out):
        out.update(zip(SMALL, _unpack(res, shard_shapes, 0, LANES, SUBLANES)))

    return (loss, dx.reshape(x.shape), *[grads[n] for n in WEIGHTS], *[delta[n] for n in WEIGHTS],
            *[new_m[n] for n in WEIGHTS], *[new_v[n] for n in WEIGHTS])
```

```python
import functools
from typing import NamedTuple

import jax
import jax.numpy as jnp
from jax import lax
from jax.experimental import pallas as pl
from jax.experimental.pallas import tpu as pltpu

F32 = jnp.float32
BF16 = jnp.bfloat16

HEAD_DIM = 64
LANES = 128
SUBLANES = 8
ROT_DIM = HEAD_DIM // 4
ROPE_THETA = 500000.0
NORM_EPS = 1e-6
BAND = 128
DILATIONS = (1, 4, 16)
NEG = -1e30
Q_SCALE = HEAD_DIM ** -0.5
N_CHIPS = 4
N_DEV = 8
VMEM_LIMIT = 48 * 1024 * 1024

ADAM_LR = 0.001
ADAM_B1 = 0.9
ADAM_B2 = 0.999
ADAM_EPS = 1e-08
ADAM_WD = 0.01
ADAM_STEP = 10


class Cfg(NamedTuple):
    d_model: int
    d_ff: int
    seq: int
    batch: int

    @property
    def heads(self):
        return self.d_model // HEAD_DIM

    @property
    def tokens(self):
        return self.batch * self.seq

    @property
    def pairs(self):
        return self.d_model // LANES


def _params(sem):
    return pltpu.CompilerParams(dimension_semantics=sem, vmem_limit_bytes=VMEM_LIMIT)


def _blk(dim, want):
    if dim <= want:
        return dim
    for b in range(want // LANES * LANES, 0, -LANES):
        if dim % b == 0:
            return b
    b = want
    while dim % b:
        b //= 2
    return b


def _mm(a, b, *, form, out_dtype, name, bm=1024, bn=1024, bk=1024, res=None, scale=1.0):
    if form == "F":
        m, kdim = a.shape
        jn, _, ns = b.shape
        bm, bn, bk = _blk(m, bm), _blk(ns, bn), _blk(kdim, bk)
        npj = ns // bn
        grid = (m // bm, jn * npj, kdim // bk)
        a_spec = pl.BlockSpec((bm, bk), lambda i, n, k: (i, k))
        b_spec = pl.BlockSpec((None, bk, bn), lambda i, n, k: (n // npj, k, n % npj))
        o_spec = pl.BlockSpec((bm, bn), lambda i, n, k: (i, n))
        o_shape = jax.ShapeDtypeStruct((m, jn * ns), out_dtype)
        dims = (((1,), (0,)), ((), ()))
    elif form == "B":
        m = a.shape[0]
        jn, kdim, ns = b.shape
        bm, bn, bk = _blk(m, bm), _blk(kdim, bn), _blk(ns, bk)
        kpj = ns // bk
        grid = (m // bm, kdim // bn, jn * kpj)
        a_spec = pl.BlockSpec((bm, bk), lambda i, n, k: (i, k))
        b_spec = pl.BlockSpec((None, bn, bk), lambda i, n, k: (k // kpj, n, k % kpj))
        o_spec = pl.BlockSpec((bm, bn), lambda i, n, k: (i, n))
        o_shape = jax.ShapeDtypeStruct((m, kdim), out_dtype)
        dims = (((1,), (1,)), ((), ()))
    else:
        raise ValueError(form)
    nk = grid[2]

    def body(*refs):
        if res is None:
            a_ref, b_ref, o_ref, acc_ref = refs
            r_ref = None
        else:
            a_ref, b_ref, r_ref, o_ref, acc_ref = refs
        k = pl.program_id(2)

        @pl.when(k == 0)
        def _():
            acc_ref[...] = jnp.zeros_like(acc_ref)

        acc_ref[...] += lax.dot_general(a_ref[...].astype(BF16), b_ref[...].astype(BF16), dims,
                                        preferred_element_type=F32)

        @pl.when(k == nk - 1)
        def _():
            r = acc_ref[...]
            if scale != 1.0:
                r = r * scale
            if r_ref is not None:
                r = r_ref[...] + r
            o_ref[...] = r.astype(o_ref.dtype)

    in_specs = [a_spec, b_spec]
    args = [a, b]
    if res is not None:
        in_specs.append(pl.BlockSpec((bm, bn), lambda i, n, k: (i, n)))
        args.append(res)
    return pl.pallas_call(
        body, name=name, grid=grid, in_specs=in_specs, out_specs=o_spec, out_shape=o_shape,
        scratch_shapes=[pltpu.VMEM((bm, bn), F32)],
        compiler_params=_params(("parallel", "parallel", "arbitrary")),
    )(*args)


def _mm_grad(a, dy, jn, *, name, scale=1.0, bm=1024, bn=1024, bk=1024):
    halves = dy if isinstance(dy, (tuple, list)) else (dy,)
    t, kdim = a.shape
    ns = len(halves) * halves[0].shape[1] // jn
    bm, bn, bk = _blk(kdim, bm), _blk(ns, bn), _blk(t, bk)
    npj = ns // bn
    grid = (kdim // bm, jn * npj, t // bk)
    nk = grid[2]
    nhalf = jn * npj // len(halves)
    dims = (((0,), (0,)), ((), ()))

    def body(a_ref, *refs):
        b_refs, o_ref, acc_ref = refs[:len(halves)], refs[-2], refs[-1]
        n, k = pl.program_id(1), pl.program_id(2)

        @pl.when(k == 0)
        def _():
            acc_ref[...] = jnp.zeros_like(acc_ref)

        for which, b_ref in enumerate(b_refs):
            @pl.when(n // nhalf == which)
            def _(b_ref=b_ref):
                acc_ref[...] += lax.dot_general(a_ref[...].astype(BF16), b_ref[...].astype(BF16), dims,
                                                preferred_element_type=F32)

        @pl.when(k == nk - 1)
        def _():
            r = acc_ref[...]
            if scale != 1.0:
                r = r * scale
            o_ref[...] = r

    def half_spec(which):
        return pl.BlockSpec((bk, bn), lambda m, n, k: (jnp.where(n // nhalf == which, k, 0),
                                                        jnp.where(n // nhalf == which, n % nhalf, 0)))

    return pl.pallas_call(
        body, name=name, grid=grid,
        in_specs=[pl.BlockSpec((bk, bm), lambda m, n, k: (k, m))] + [half_spec(w) for w in range(len(halves))],
        out_specs=pl.BlockSpec((None, bm, bn), lambda m, n, k: (n // npj, m, n % npj)),
        out_shape=jax.ShapeDtypeStruct((jn, kdim, ns), F32),
        scratch_shapes=[pltpu.VMEM((bm, bn), F32)],
        compiler_params=_params(("parallel", "parallel", "arbitrary")),
    )(a, *halves)


def _mm_back2(dy_halves, w, *, name, bm=1024, bn=1024, bk=1024):
    m = dy_halves[0].shape[0]
    jn, kdim, ns = w.shape
    bm, bn, bk = _blk(m, bm), _blk(kdim, bn), _blk(ns, bk)
    kpj = ns // bk
    nk = jn * kpj
    khalf = nk // 2

    def body(a0_ref, a1_ref, b_ref, o_ref, acc_ref):
        k = pl.program_id(2)

        @pl.when(k == 0)
        def _():
            acc_ref[...] = jnp.zeros_like(acc_ref)

        for which, a_ref in enumerate((a0_ref, a1_ref)):
            @pl.when(k // khalf == which)
            def _(a_ref=a_ref):
                acc_ref[...] += lax.dot_general(a_ref[...], b_ref[...], _NT, preferred_element_type=F32)

        @pl.when(k == nk - 1)
        def _():
            o_ref[...] = acc_ref[...]

    def half_spec(which):
        return pl.BlockSpec((bm, bk), lambda i, n, k: (i, jnp.clip(k - which * khalf, 0, khalf - 1)))

    return pl.pallas_call(
        body, name=name, grid=(m // bm, kdim // bn, nk),
        in_specs=[half_spec(0), half_spec(1),
                  pl.BlockSpec((None, bn, bk), lambda i, n, k: (k // kpj, n, k % kpj))],
        out_specs=pl.BlockSpec((bm, bn), lambda i, n, k: (i, n)),
        out_shape=jax.ShapeDtypeStruct((m, kdim), F32),
        scratch_shapes=[pltpu.VMEM((bm, bn), F32)],
        compiler_params=_params(("parallel", "parallel", "arbitrary")),
    )(dy_halves[0], dy_halves[1], w)


def _ffn_in_act(n, w_in, *, name, bm=512):
    m, kdim = n.shape
    jn, _, ns = w_in.shape
    f = jn * ns // 2
    bm = _blk(m, bm)
    bn = _blk(ns, WIDE)
    npj = ns // bn
    nf = f // bn

    def body(n_ref, wg_ref, wu_ref, g_ref, u_ref, a_ref):
        nv = n_ref[...]
        g = jnp.dot(nv, wg_ref[...], preferred_element_type=F32)
        u = jnp.dot(nv, wu_ref[...], preferred_element_type=F32)
        g_ref[...] = g.astype(BF16)
        u_ref[...] = u.astype(BF16)
        a_ref[...] = (g * jax.nn.sigmoid(g) * u).astype(BF16)

    out = jax.ShapeDtypeStruct((m, f), BF16)
    ospec = pl.BlockSpec((bm, bn), lambda i, c: (i, c))
    return pl.pallas_call(
        body, name=name, grid=(m // bm, nf),
        in_specs=[pl.BlockSpec((bm, kdim), lambda i, c: (i, 0)),
                  pl.BlockSpec((None, kdim, bn), lambda i, c: (c // npj, 0, c % npj)),
                  pl.BlockSpec((None, kdim, bn), lambda i, c: ((c + nf) // npj, 0, (c + nf) % npj))],
        out_specs=[ospec, ospec, ospec], out_shape=[out, out, out],
        compiler_params=_params(("parallel", "parallel")),
    )(n, w_in, w_in)


def _ffn_out_dx_act(dh, w_out, gate, up, *, name, scale, bm=512):
    m, d = dh.shape
    f = w_out.shape[1]
    bm = _blk(m, bm)
    bn = _blk(f, WIDE)

    def body(dh_ref, w_ref, g_ref, u_ref, dg_ref, du_ref):
        da = lax.dot_general(dh_ref[...].astype(BF16), w_ref[...], _NT, preferred_element_type=F32) * scale
        g = g_ref[...].astype(F32)
        sg = jax.nn.sigmoid(g)
        silu = g * sg
        dg_ref[...] = (da * u_ref[...].astype(F32) * (sg + silu * (1.0 - sg))).astype(BF16)
        du_ref[...] = (da * silu).astype(BF16)

    out = jax.ShapeDtypeStruct((m, f), BF16)
    spec = pl.BlockSpec((bm, bn), lambda i, c: (i, c))
    return pl.pallas_call(
        body, name=name, grid=(m // bm, f // bn),
        in_specs=[pl.BlockSpec((bm, d), lambda i, c: (i, 0)), pl.BlockSpec((None, bn, d), lambda i, c: (0, c, 0)),
                  spec, spec],
        out_specs=[spec, spec], out_shape=[out, out],
        compiler_params=_params(("parallel", "parallel")),
    )(dh, w_out, gate, up)


ROW_BLOCK = 512


def _fold8(x):
    return jnp.sum(x.reshape(x.shape[0] // SUBLANES, SUBLANES, x.shape[1]), axis=0)


def _rms_fwd(x, g, *, name):
    t, d = x.shape
    tr = _blk(t, ROW_BLOCK)

    def body(x_ref, g_ref, o_ref):
        xv = x_ref[...]
        rstd = lax.rsqrt(jnp.mean(xv * xv, axis=-1, keepdims=True) + NORM_EPS)
        o_ref[...] = ((xv * rstd) * g_ref[...]).astype(BF16)

    return pl.pallas_call(
        body, name=name, grid=(t // tr,),
        in_specs=[pl.BlockSpec((tr, d), lambda i: (i, 0)), pl.BlockSpec((1, d), lambda i: (0, 0))],
        out_specs=pl.BlockSpec((tr, d), lambda i: (i, 0)),
        out_shape=jax.ShapeDtypeStruct((t, d), BF16),
        compiler_params=_params(("parallel",)),
    )(x, g.reshape(1, d))


def _rms_bwd(x, g, dy, dres, *, name):
    t, d = x.shape
    tr = _blk(t, ROW_BLOCK)

    def body(x_ref, g_ref, dy_ref, dres_ref, dx_ref, dg_ref):
        i = pl.program_id(0)
        xv = x_ref[...]
        rstd = lax.rsqrt(jnp.mean(xv * xv, axis=-1, keepdims=True) + NORM_EPS)
        xhat = xv * rstd
        dyv = dy_ref[...]
        dyg = dyv * g_ref[...]
        proj = jnp.mean(dyg * xhat, axis=-1, keepdims=True)
        dx_ref[...] = dres_ref[...] + rstd * (dyg - xhat * proj)

        @pl.when(i == 0)
        def _():
            dg_ref[...] = jnp.zeros_like(dg_ref)

        dg_ref[...] += _fold8(dyv * xhat)

    dx, dg = pl.pallas_call(
        body, name=name, grid=(t // tr,),
        in_specs=[pl.BlockSpec((tr, d), lambda i: (i, 0)), pl.BlockSpec((1, d), lambda i: (0, 0)),
                  pl.BlockSpec((tr, d), lambda i: (i, 0)), pl.BlockSpec((tr, d), lambda i: (i, 0))],
        out_specs=[pl.BlockSpec((tr, d), lambda i: (i, 0)), pl.BlockSpec((SUBLANES, d), lambda i: (0, 0))],
        out_shape=[jax.ShapeDtypeStruct((t, d), F32), jax.ShapeDtypeStruct((SUBLANES, d), F32)],
        compiler_params=_params(("arbitrary",)),
    )(x, g.reshape(1, d), dy, dres)
    return dx, jnp.sum(dg, axis=0)


def _swiglu_fwd(u, *, name):
    t, f2 = u.shape
    f = f2 // 2
    tr = _blk(t, 256)

    def body(g_ref, u_ref, o_ref):
        gv = g_ref[...]
        o_ref[...] = (gv * jax.nn.sigmoid(gv) * u_ref[...]).astype(BF16)

    return pl.pallas_call(
        body, name=name, grid=(t // tr,),
        in_specs=[pl.BlockSpec((tr, f), lambda i: (i, 0)), pl.BlockSpec((tr, f), lambda i: (i, 1))],
        out_specs=pl.BlockSpec((tr, f), lambda i: (i, 0)),
        out_shape=jax.ShapeDtypeStruct((t, f), BF16),
        compiler_params=_params(("parallel",)),
    )(u, u)


def _swiglu_bwd(u, da, *, name):
    t, f2 = u.shape
    f = f2 // 2
    tr = _blk(t, 256)

    def body(g_ref, u_ref, da_ref, o_ref):
        gv = g_ref[...]
        sg = jax.nn.sigmoid(gv)
        silu = gv * sg
        dav = da_ref[...]
        o_ref[:, :f] = (dav * u_ref[...] * (sg + silu * (1.0 - sg))).astype(BF16)
        o_ref[:, f:] = (dav * silu).astype(BF16)

    return pl.pallas_call(
        body, name=name, grid=(t // tr,),
        in_specs=[pl.BlockSpec((tr, f), lambda i: (i, 0)), pl.BlockSpec((tr, f), lambda i: (i, 1)),
                  pl.BlockSpec((tr, f), lambda i: (i, 0))],
        out_specs=pl.BlockSpec((tr, f2), lambda i: (i, 0)),
        out_shape=jax.ShapeDtypeStruct((t, f2), BF16),
        compiler_params=_params(("parallel",)),
    )(u, u, da)


def _loss_fwd_bwd(h, target, *, name):
    t, d = h.shape
    tr = _blk(t, ROW_BLOCK)

    def body(h_ref, t_ref, dh_ref, l_ref):
        i = pl.program_id(0)
        err = h_ref[...] - t_ref[...]
        dh_ref[...] = err * (1.0 / d)

        @pl.when(i == 0)
        def _():
            l_ref[...] = jnp.zeros_like(l_ref)

        l_ref[...] += _fold8(err * err)

    dh, part = pl.pallas_call(
        body, name=name, grid=(t // tr,),
        in_specs=[pl.BlockSpec((tr, d), lambda i: (i, 0)), pl.BlockSpec((tr, d), lambda i: (i, 0))],
        out_specs=[pl.BlockSpec((tr, d), lambda i: (i, 0)), pl.BlockSpec((SUBLANES, d), lambda i: (0, 0))],
        out_shape=[jax.ShapeDtypeStruct((t, d), F32), jax.ShapeDtypeStruct((SUBLANES, d), F32)],
        compiler_params=_params(("arbitrary",)),
    )(h, target)
    return jnp.sum(part) * (0.5 / d), dh


def _seg_matrix():
    r = lax.broadcasted_iota(jnp.int32, (LANES, LANES), 0) // HEAD_DIM
    c = lax.broadcasted_iota(jnp.int32, (LANES, LANES), 1) // HEAD_DIM
    return (r == c).astype(BF16)


def _head_sum(x, seg):
    hi = x.astype(BF16)
    r1 = x - hi.astype(F32)
    mid = r1.astype(BF16)
    lo = (r1 - mid.astype(F32)).astype(BF16)
    dot = functools.partial(jnp.dot, preferred_element_type=F32)
    return dot(hi, seg) + dot(mid, seg) + dot(lo, seg)


def _lane_in_head(shape):
    return lax.broadcasted_iota(jnp.int32, shape, 1) % HEAD_DIM


def _rot_partner(x):
    up = pltpu.roll(x, LANES - ROT_DIM // 2, 1)
    down = pltpu.roll(x, ROT_DIM // 2, 1)
    return jnp.where(_lane_in_head(x.shape) < ROT_DIM // 2, up, down)


def _rope_tables(positions):
    inv_freq = ROPE_THETA ** (-jnp.arange(0, ROT_DIM, 2, dtype=F32) / ROT_DIM)
    ang = positions.astype(F32)[:, None] * inv_freq
    t = ang.shape[0]
    rest = HEAD_DIM - ROT_DIM
    cos = jnp.concatenate([jnp.cos(ang), jnp.cos(ang), jnp.ones((t, rest), F32)], axis=1)
    sin = jnp.concatenate([-jnp.sin(ang), jnp.sin(ang), jnp.zeros((t, rest), F32)], axis=1)
    return jnp.tile(cos, (1, LANES // HEAD_DIM)), jnp.tile(sin, (1, LANES // HEAD_DIM))


def _kind_is(j, kinds, kind):
    hits = [j == jj for jj, k in enumerate(kinds) if k == kind]
    return functools.reduce(jnp.logical_or, hits) if hits else None


def _hn_fwd(x, gains, kinds, d, cos, sin, *, name, col0=0):
    t = x.shape[0]
    n = len(kinds)
    tr = _blk(t, ROW_BLOCK)
    seg = _seg_matrix()
    g8 = jnp.repeat(gains.astype(F32), SUBLANES, axis=0)

    def body(x_ref, g_ref, seg_ref, cos_ref, sin_ref, o_ref):
        j = pl.program_id(1)

        def normed(rope):
            for c in range(d // LANES):
                sl = slice(c * LANES, (c + 1) * LANES)
                xv = x_ref[:, sl]
                ms = _head_sum(xv * xv, seg_ref[...]) * (1.0 / HEAD_DIM)
                y = (xv * lax.rsqrt(ms + NORM_EPS)) * g_ref[0:1, sl]
                if rope:
                    y = y * cos_ref[...] + _rot_partner(y) * sin_ref[...]
                o_ref[:, sl] = y.astype(BF16)

        for kind in ("rope", "norm"):
            hit = _kind_is(j, kinds, kind)
            if hit is not None:
                pl.when(hit)(functools.partial(normed, kind == "rope"))
        hit = _kind_is(j, kinds, "cast")
        if hit is not None:
            @pl.when(hit)
            def _():
                o_ref[...] = x_ref[...].astype(BF16)

    return pl.pallas_call(
        body, name=name, grid=(t // tr, n),
        in_specs=[pl.BlockSpec((tr, d), lambda i, j: (i, col0 + j)), pl.BlockSpec((SUBLANES, d), lambda i, j: (j, 0)),
                  pl.BlockSpec((LANES, LANES), lambda i, j: (0, 0)),
                  pl.BlockSpec((tr, LANES), lambda i, j: (i, 0)), pl.BlockSpec((tr, LANES), lambda i, j: (i, 0))],
        out_specs=pl.BlockSpec((tr, d), lambda i, j: (i, j)),
        out_shape=jax.ShapeDtypeStruct((t, n * d), BF16),
        compiler_params=_params(("parallel", "parallel")),
    )(x, g8, seg, cos, sin)


def _hn_bwd(x, dys, gains, kinds, d, cos, sin, *, name, col0=0):
    t = x.shape[0]
    n = len(kinds)
    tr = _blk(t, ROW_BLOCK // 2)
    seg = _seg_matrix()
    g8 = jnp.repeat(gains.astype(F32), SUBLANES, axis=0)

    def body(x_ref, *refs):
        dy_refs = refs[:n]
        g_ref, seg_ref, cos_ref, sin_ref, dx_ref, dg_ref = refs[n:]
        j = pl.program_id(0)
        i = pl.program_id(1)

        @pl.when(i == 0)
        def _():
            dg_ref[...] = jnp.zeros_like(dg_ref)

        def normed(rope, dy_ref):
            for c in range(d // LANES):
                sl = slice(c * LANES, (c + 1) * LANES)
                xv = x_ref[:, sl]
                dyv = dy_ref[:, sl]
                if rope:
                    dyv = dyv * cos_ref[...] - _rot_partner(dyv) * sin_ref[...]
                ms = _head_sum(xv * xv, seg_ref[...]) * (1.0 / HEAD_DIM)
                rstd = lax.rsqrt(ms + NORM_EPS)
                xhat = xv * rstd
                dg_ref[:, sl] += _fold8(dyv * xhat)
                dyg = dyv * g_ref[0:1, sl]
                proj = _head_sum(dyg * xhat, seg_ref[...]) * (1.0 / HEAD_DIM)
                dx_ref[:, sl] = (rstd * (dyg - xhat * proj)).astype(BF16)

        def cast(dy_ref):
            dx_ref[...] = dy_ref[...].astype(BF16)

        for jj, kind in enumerate(kinds):
            if kind == "cast":
                pl.when(j == jj)(functools.partial(cast, dy_refs[jj]))
            else:
                pl.when(j == jj)(functools.partial(normed, kind == "rope", dy_refs[jj]))

    def dy_spec(jj):
        return pl.BlockSpec((tr, d), lambda j, i: (jnp.where(j == jj, i, 0), 0))

    dx, dg = pl.pallas_call(
        body, name=name, grid=(n, t // tr),
        in_specs=[pl.BlockSpec((tr, d), lambda j, i: (i, col0 + j))] + [dy_spec(jj) for jj in range(n)] + [
                  pl.BlockSpec((SUBLANES, d), lambda j, i: (j, 0)),
                  pl.BlockSpec((LANES, LANES), lambda j, i: (0, 0)),
                  pl.BlockSpec((tr, LANES), lambda j, i: (i, 0)), pl.BlockSpec((tr, LANES), lambda j, i: (i, 0))],
        out_specs=[pl.BlockSpec((tr, d), lambda j, i: (i, j)), pl.BlockSpec((SUBLANES, d), lambda j, i: (j, 0))],
        out_shape=[jax.ShapeDtypeStruct((t, n * d), BF16), jax.ShapeDtypeStruct((n * SUBLANES, d), F32)],
        compiler_params=_params(("arbitrary", "arbitrary")),
    )(x, *dys, g8, seg, cos, sin)
    dg = dg.reshape(n, SUBLANES, d // HEAD_DIM, HEAD_DIM).sum(axis=(1, 2))
    return dx, dg


def _head_dot(a, b, *, name):
    t, d = a.shape
    tr = _blk(t, ROW_BLOCK)
    seg = _seg_matrix()

    def body(a_ref, b_ref, seg_ref, o_ref):
        for c in range(d // LANES):
            sl = slice(c * LANES, (c + 1) * LANES)
            o_ref[:, sl] = _head_sum(a_ref[:, sl].astype(BF16).astype(F32) * b_ref[:, sl], seg_ref[...])

    return pl.pallas_call(
        body, name=name, grid=(t // tr,),
        in_specs=[pl.BlockSpec((tr, d), lambda i: (i, 0)), pl.BlockSpec((tr, d), lambda i: (i, 0)),
                  pl.BlockSpec((LANES, LANES), lambda i: (0, 0))],
        out_specs=pl.BlockSpec((tr, d), lambda i: (i, 0)),
        out_shape=jax.ShapeDtypeStruct((t, d), F32),
        compiler_params=_params(("parallel",)),
    )(a, b, seg)


def _half_mask(shape):
    return lax.broadcasted_iota(jnp.int32, shape, 1) < HEAD_DIM


def _band_valid(first):
    qi = lax.broadcasted_iota(jnp.int32, (BAND, 2 * BAND), 0)
    kj = lax.broadcasted_iota(jnp.int32, (BAND, 2 * BAND), 1)
    dist = qi + BAND - kj
    return (dist >= 0) & (dist <= BAND) & ((kj >= BAND) | jnp.logical_not(first))


_NT = (((1,), (1,)), ((), ()))
_TN = (((0,), (0,)), ((), ()))


def _dot2(p, v):
    hi = p.astype(BF16)
    lo = (p - hi.astype(F32)).astype(BF16)
    return jnp.dot(hi, v, preferred_element_type=F32) + jnp.dot(lo, v, preferred_element_type=F32)


def _band_fwd(qkv, dil, cfg, *, name):
    t, d = cfg.tokens, cfg.d_model
    w = 3 * d
    rows = t // dil
    nbt = rows // BAND
    nb = cfg.seq // (dil * BAND)
    view = qkv.reshape(rows, dil * w)
    ncol = w // d

    def body(q_ref, kp_ref, kc_ref, vp_ref, vc_ref, o_ref, lse_ref):
        i = pl.program_id(1)
        valid = _band_valid(i % nb == 0)
        half = _half_mask((BAND, LANES))
        for hp in range(d // LANES):
            sl = slice(hp * LANES, (hp + 1) * LANES)
            q2 = q_ref[:, sl]
            kk = jnp.concatenate([kp_ref[:, sl], kc_ref[:, sl]], axis=0)
            vv = jnp.concatenate([vp_ref[:, sl], vc_ref[:, sl]], axis=0)
            outs, lses = [], []
            for e in range(2):
                qe = jnp.where(half == (e == 0), q2, jnp.zeros_like(q2))
                s = lax.dot_general(qe, kk, _NT, preferred_element_type=F32)
                s = jnp.where(valid, s, NEG)
                m = jnp.max(s, axis=1, keepdims=True)
                p = jnp.exp(s - m)
                l = jnp.sum(p, axis=1, keepdims=True)
                outs.append(_dot2(p * (1.0 / l), vv))
                lses.append(m + jnp.log(l))
            o_ref[:, sl] = jnp.where(half, outs[0], outs[1])
            lse_ref[:, sl] = jnp.where(half, lses[0], lses[1])

    def col(which):
        return lambda r, i: (i, r * ncol + which)

    def col_prev(which):
        return lambda r, i: (jnp.maximum(i - 1, 0), r * ncol + which)

    blk = (BAND, d)
    o, lse = pl.pallas_call(
        body, name=name, grid=(dil, nbt),
        in_specs=[pl.BlockSpec(blk, col(0)), pl.BlockSpec(blk, col_prev(1)), pl.BlockSpec(blk, col(1)),
                  pl.BlockSpec(blk, col_prev(2)), pl.BlockSpec(blk, col(2))],
        out_specs=[pl.BlockSpec(blk, lambda r, i: (i, r)), pl.BlockSpec(blk, lambda r, i: (i, r))],
        out_shape=[jax.ShapeDtypeStruct((rows, dil * d), F32), jax.ShapeDtypeStruct((rows, dil * d), F32)],
        compiler_params=_params(("parallel", "arbitrary")),
    )(view, view, view, view, view)
    return o.reshape(t, d), lse.reshape(t, d)


def _band_bwd(qkv, dmixed, lse_all, dsum, dil, cfg, *, name):
    t, d = cfg.tokens, cfg.d_model
    w = 3 * d
    rows = t // dil
    nbt = rows // BAND
    nb = cfg.seq // (dil * BAND)
    view = qkv.reshape(rows, dil * w)
    ncol = w // d
    do_v, l_v, d_v = (z.reshape(rows, dil * d) for z in (dmixed, lse_all, dsum))

    def body(q_ref, kp_ref, kc_ref, vp_ref, vc_ref, do_ref, l_ref, ds_ref, dq_ref, dk_ref, dv_ref, ck_ref, cv_ref):
        i = pl.program_id(1)

        @pl.when(i < nbt)
        def _():
            valid = _band_valid(i % nb == 0)
            half = _half_mask((BAND, LANES))
            half2 = _half_mask((2 * BAND, LANES))
            for hp in range(d // LANES):
                sl = slice(hp * LANES, (hp + 1) * LANES)
                q2 = q_ref[:, sl]
                kk = jnp.concatenate([kp_ref[:, sl], kc_ref[:, sl]], axis=0)
                vv = jnp.concatenate([vp_ref[:, sl], vc_ref[:, sl]], axis=0)
                do2 = do_ref[:, sl].astype(BF16)
                dqs, dks, dvs = [], [], []
                for e in range(2):
                    lane0 = e * HEAD_DIM
                    keep = half == (e == 0)
                    qe = jnp.where(keep, q2, jnp.zeros_like(q2))
                    doe = jnp.where(keep, do2, jnp.zeros_like(do2))
                    s = lax.dot_general(qe, kk, _NT, preferred_element_type=F32)
                    s = jnp.where(valid, s, NEG)
                    p = jnp.exp(s - l_ref[:, hp * LANES + lane0:hp * LANES + lane0 + 1])
                    dp = lax.dot_general(doe, vv, _NT, preferred_element_type=F32)
                    dsc = (p * (dp - ds_ref[:, hp * LANES + lane0:hp * LANES + lane0 + 1])).astype(BF16)
                    dqs.append(jnp.dot(dsc, kk, preferred_element_type=F32))
                    dks.append(lax.dot_general(dsc, q2, _TN, preferred_element_type=F32))
                    dvs.append(lax.dot_general(p.astype(BF16), do2, _TN, preferred_element_type=F32))
                dq_ref[:, sl] = jnp.where(half, dqs[0], dqs[1])
                dkk = jnp.where(half2, dks[0], dks[1])
                dvv = jnp.where(half2, dvs[0], dvs[1])

                @pl.when(i > 0)
                def _():
                    dk_ref[:, sl] = ck_ref[:, sl] + dkk[:BAND]
                    dv_ref[:, sl] = cv_ref[:, sl] + dvv[:BAND]

                ck_ref[:, sl] = dkk[BAND:]
                cv_ref[:, sl] = dvv[BAND:]

        @pl.when(i == nbt)
        def _():
            dk_ref[...] = ck_ref[...]
            dv_ref[...] = cv_ref[...]

    def cur(i):
        return jnp.minimum(i, nbt - 1)

    def col(which):
        return lambda r, i: (cur(i), r * ncol + which)

    def col_prev(which):
        return lambda r, i: (jnp.maximum(cur(i) - 1, 0), r * ncol + which)

    blk = (BAND, d)
    here = pl.BlockSpec(blk, lambda r, i: (cur(i), r))
    behind = pl.BlockSpec(blk, lambda r, i: (jnp.maximum(i - 1, 0), r))
    shape = jax.ShapeDtypeStruct((rows, dil * d), F32)
    dq, dk, dv = pl.pallas_call(
        body, name=name, grid=(dil, nbt + 1),
        in_specs=[pl.BlockSpec(blk, col(0)), pl.BlockSpec(blk, col_prev(1)), pl.BlockSpec(blk, col(1)),
                  pl.BlockSpec(blk, col_prev(2)), pl.BlockSpec(blk, col(2)), here, here, here],
        out_specs=[here, behind, behind],
        out_shape=[shape, shape, shape],
        scratch_shapes=[pltpu.VMEM(blk, F32), pltpu.VMEM(blk, F32)],
        compiler_params=_params(("arbitrary", "arbitrary")),
    )(view, view, view, view, view, do_v, l_v, d_v)
    return dq.reshape(t, d), dk.reshape(t, d), dv.reshape(t, d)


def _band_valid_t(first):
    s = lax.broadcasted_iota(jnp.int32, (2 * BAND, BAND), 0)
    t = lax.broadcasted_iota(jnp.int32, (2 * BAND, BAND), 1)
    dist = t + BAND - s
    return (dist >= 0) & (dist <= BAND) & ((s >= BAND) | jnp.logical_not(first))


def _band_layouts(qkv, dil, cfg):
    rows = cfg.tokens // dil
    d = cfg.d_model
    return qkv.reshape(rows, dil * 3 * d), qkv.reshape(rows, dil, 3, d).transpose(1, 2, 3, 0)


def _to_classes_t(z, dil, width):
    return z.reshape(z.shape[0] // dil, dil, width).transpose(1, 2, 0)


def _from_classes_t(z):
    dil, width, rows = z.shape
    return z.transpose(2, 0, 1).reshape(rows * dil, width)


def _band_fwd_t(nat, tr, dil, cfg, *, name):
    d, hh = cfg.d_model, cfg.heads
    rows = cfg.tokens // dil
    nbt = rows // BAND
    nb = cfg.seq // (dil * BAND)

    def body(qt_ref, kp_ref, kc_ref, vtp_ref, vtc_ref, o_ref, lse_ref):
        i = pl.program_id(1)
        valid = _band_valid_t(i % nb == 0)
        upper = lax.broadcasted_iota(jnp.int32, (LANES, BAND), 0) < HEAD_DIM
        for hp in range(d // LANES):
            pair = slice(hp * LANES, (hp + 1) * LANES)
            qt2 = qt_ref[pair, :]
            kk = jnp.concatenate([kp_ref[:, pair], kc_ref[:, pair]], axis=0)
            for e in range(2):
                h = 2 * hp + e
                hrows = slice(h * HEAD_DIM, (h + 1) * HEAD_DIM)
                qte = jnp.where(upper == (e == 0), qt2, jnp.zeros_like(qt2))
                s = jnp.where(valid, jnp.dot(kk, qte, preferred_element_type=F32), NEG)
                m = jnp.max(s, axis=0, keepdims=True)
                p = jnp.exp(s - m)
                l = jnp.sum(p, axis=0, keepdims=True)
                hi = p.astype(BF16)
                lo = (p - hi.astype(F32)).astype(BF16)
                vvt = jnp.concatenate([vtp_ref[hrows, :], vtc_ref[hrows, :]], axis=1)
                o = jnp.dot(vvt, hi, preferred_element_type=F32) + jnp.dot(vvt, lo, preferred_element_type=F32)
                o_ref[hrows, :] = o * (1.0 / l)
                lse_ref[h:h + 1, :] = m + jnp.log(l)

    def prev(i):
        return jnp.maximum(i - 1, 0)

    tblk = (None, None, d, BAND)
    return pl.pallas_call(
        body, name=name, grid=(dil, nbt),
        in_specs=[pl.BlockSpec(tblk, lambda r, i: (r, 0, 0, i)),
                  pl.BlockSpec((BAND, d), lambda r, i: (prev(i), r * 3 + 1)),
                  pl.BlockSpec((BAND, d), lambda r, i: (i, r * 3 + 1)),
                  pl.BlockSpec(tblk, lambda r, i: (r, 2, 0, prev(i))),
                  pl.BlockSpec(tblk, lambda r, i: (r, 2, 0, i))],
        out_specs=[pl.BlockSpec((None, d, BAND), lambda r, i: (r, 0, i)),
                   pl.BlockSpec((None, hh, BAND), lambda r, i: (r, 0, i))],
        out_shape=[jax.ShapeDtypeStruct((dil, d, rows), F32), jax.ShapeDtypeStruct((dil, hh, rows), F32)],
        compiler_params=_params(("parallel", "arbitrary")),
    )(tr, nat, nat, tr, tr)


def _band_bwd_t(nat, tr, do_t, do_nat, lse_c, dsum_c, dil, cfg, *, name):
    d, hh = cfg.d_model, cfg.heads
    rows = cfg.tokens // dil
    nbt = rows // BAND
    nb = cfg.seq // (dil * BAND)

    def body(qt_ref, qn_ref, kp_ref, kc_ref, ktp_ref, ktc_ref, vp_ref, vc_ref, dot_ref, don_ref, l_ref, ds_ref,
             dq_ref, dk_ref, dv_ref, ck_ref, cv_ref):
        i = pl.program_id(1)

        @pl.when(i < nbt)
        def _():
            valid = _band_valid_t(i % nb == 0)
            upper = lax.broadcasted_iota(jnp.int32, (LANES, BAND), 0) < HEAD_DIM
            half2 = _half_mask((2 * BAND, LANES))
            for hp in range(d // LANES):
                pair = slice(hp * LANES, (hp + 1) * LANES)
                qt2, dot2 = qt_ref[pair, :], dot_ref[pair, :]
                qn2, don2 = qn_ref[:, pair], don_ref[:, pair]
                kk = jnp.concatenate([kp_ref[:, pair], kc_ref[:, pair]], axis=0)
                vv = jnp.concatenate([vp_ref[:, pair], vc_ref[:, pair]], axis=0)
                dks, dvs = [], []
                for e in range(2):
                    h = 2 * hp + e
                    hrows = slice(h * HEAD_DIM, (h + 1) * HEAD_DIM)
                    keep = upper == (e == 0)
                    qte = jnp.where(keep, qt2, jnp.zeros_like(qt2))
                    dote = jnp.where(keep, dot2, jnp.zeros_like(dot2))
                    s = jnp.where(valid, jnp.dot(kk, qte, preferred_element_type=F32), NEG)
                    p = jnp.exp(s - l_ref[h:h + 1, :])
                    dp = jnp.dot(vv, dote, preferred_element_type=F32)
                    dsb = (p * (dp - ds_ref[h:h + 1, :])).astype(BF16)
                    kkt = jnp.concatenate([ktp_ref[hrows, :], ktc_ref[hrows, :]], axis=1)
                    dq_ref[hrows, :] = jnp.dot(kkt, dsb, preferred_element_type=F32)
                    dks.append(jnp.dot(dsb, qn2, preferred_element_type=F32))
                    dvs.append(jnp.dot(p.astype(BF16), don2, preferred_element_type=F32))
                dkk = jnp.where(half2, dks[0], dks[1])
                dvv = jnp.where(half2, dvs[0], dvs[1])

                @pl.when(i > 0)
                def _():
                    dk_ref[:, pair] = ck_ref[:, pair] + dkk[:BAND]
                    dv_ref[:, pair] = cv_ref[:, pair] + dvv[:BAND]

                ck_ref[:, pair] = dkk[BAND:]
                cv_ref[:, pair] = dvv[BAND:]

        @pl.when(i == nbt)
        def _():
            dk_ref[...] = ck_ref[...]
            dv_ref[...] = cv_ref[...]

    def cur(i):
        return jnp.minimum(i, nbt - 1)

    def prev(i):
        return jnp.maximum(cur(i) - 1, 0)

    tblk = (None, None, d, BAND)
    cblk = (None, hh, BAND)
    blk = (BAND, d)
    behind = pl.BlockSpec(blk, lambda r, i: (jnp.maximum(i - 1, 0), r))
    shape = jax.ShapeDtypeStruct((rows, dil * d), F32)
    return pl.pallas_call(
        body, name=name, grid=(dil, nbt + 1),
        in_specs=[pl.BlockSpec(tblk, lambda r, i: (r, 0, 0, cur(i))),
                  pl.BlockSpec(blk, lambda r, i: (cur(i), r * 3)),
                  pl.BlockSpec(blk, lambda r, i: (prev(i), r * 3 + 1)),
                  pl.BlockSpec(blk, lambda r, i: (cur(i), r * 3 + 1)),
                  pl.BlockSpec(tblk, lambda r, i: (r, 1, 0, prev(i))),
                  pl.BlockSpec(tblk, lambda r, i: (r, 1, 0, cur(i))),
                  pl.BlockSpec(blk, lambda r, i: (prev(i), r * 3 + 2)),
                  pl.BlockSpec(blk, lambda r, i: (cur(i), r * 3 + 2)),
                  pl.BlockSpec((None, d, BAND), lambda r, i: (r, 0, cur(i))),
                  pl.BlockSpec(blk, lambda r, i: (cur(i), r)),
                  pl.BlockSpec(cblk, lambda r, i: (r, 0, cur(i))),
                  pl.BlockSpec(cblk, lambda r, i: (r, 0, cur(i)))],
        out_specs=[pl.BlockSpec((None, d, BAND), lambda r, i: (r, 0, cur(i))), behind, behind],
        out_shape=[jax.ShapeDtypeStruct((dil, d, rows), F32), shape, shape],
        scratch_shapes=[pltpu.VMEM(blk, F32), pltpu.VMEM(blk, F32)],
        compiler_params=_params(("arbitrary", "arbitrary")),
    )(tr, nat, nat, nat, tr, tr, nat, nat, do_t, do_nat, lse_c, dsum_c)


def _mix_fwd(outs, lses, *, name):
    t, d = outs[0].shape
    tr = _blk(t, ROW_BLOCK)
    ng = len(outs)

    def body(*refs):
        o_refs, l_refs = refs[:ng], refs[ng:2 * ng]
        mixed_ref, lse_ref = refs[2 * ng:]
        ls = [r[...] for r in l_refs]
        m = functools.reduce(jnp.maximum, ls)
        es = [jnp.exp(l - m) for l in ls]
        tot = functools.reduce(jnp.add, es)
        inv = 1.0 / tot
        mixed_ref[...] = functools.reduce(jnp.add, [(e * inv) * r[...] for e, r in zip(es, o_refs)])
        lse_ref[...] = m + jnp.log(tot)

    spec = pl.BlockSpec((tr, d), lambda i: (i, 0))
    return pl.pallas_call(
        body, name=name, grid=(t // tr,),
        in_specs=[spec] * (2 * ng), out_specs=[spec, spec],
        out_shape=[jax.ShapeDtypeStruct((t, d), F32), jax.ShapeDtypeStruct((t, d), F32)],
        compiler_params=_params(("parallel",)),
    )(*outs, *lses)


GATE_BLOCK = 256


def _tri(n, upper):
    r = lax.broadcasted_iota(jnp.int32, (n, n), 0)
    c = lax.broadcasted_iota(jnp.int32, (n, n), 1)
    return ((c >= r) if upper else (c <= r)).astype(BF16)


def _tri_dot(tri, x):
    hi = x.astype(BF16)
    r1 = x - hi.astype(F32)
    mid = r1.astype(BF16)
    lo = (r1 - mid.astype(F32)).astype(BF16)
    dot = functools.partial(jnp.dot, preferred_element_type=F32)
    return dot(tri, hi) + dot(tri, mid) + dot(tri, lo)


def _log_sigmoid(z):
    return jnp.minimum(z, 0.0) - jnp.log(1.0 + jnp.exp(-jnp.abs(z)))


def _gate_fwd(proj, col_block, bias, cfg, *, name):
    tr = _blk(cfg.seq, GATE_BLOCK)
    nblk = cfg.seq // tr

    def body(z_ref, b_ref, tri_ref, o_ref, carry_ref):
        i = pl.program_id(1)

        @pl.when(i == 0)
        def _():
            carry_ref[...] = jnp.zeros_like(carry_ref)

        logf = _log_sigmoid(z_ref[...] + b_ref[0:1, :])
        cum = _tri_dot(tri_ref[...], logf) + carry_ref[0:1, :]
        o_ref[...] = cum
        carry_ref[...] = jnp.broadcast_to(cum[tr - 1:tr, :], carry_ref.shape)

    return pl.pallas_call(
        body, name=name, grid=(cfg.batch, nblk),
        in_specs=[pl.BlockSpec((tr, LANES), lambda b, i: (b * nblk + i, col_block)),
                  pl.BlockSpec((SUBLANES, LANES), lambda b, i: (0, 0)),
                  pl.BlockSpec((tr, tr), lambda b, i: (0, 0))],
        out_specs=pl.BlockSpec((tr, LANES), lambda b, i: (b * nblk + i, 0)),
        out_shape=jax.ShapeDtypeStruct((cfg.tokens, LANES), F32),
        scratch_shapes=[pltpu.VMEM((SUBLANES, LANES), F32)],
        compiler_params=_params(("arbitrary", "arbitrary")),
    )(proj, jnp.broadcast_to(bias, (SUBLANES, LANES)), _tri(tr, upper=False))


def _gate_bwd(proj, col_block, bias, dcum, cfg, *, name):
    tr = _blk(cfg.seq, GATE_BLOCK)
    nblk = cfg.seq // tr

    def body(z_ref, b_ref, tri_ref, dc_ref, dz_ref, db_ref, carry_ref):
        b = pl.program_id(0)
        i = pl.program_id(1)

        @pl.when(i == 0)
        def _():
            carry_ref[...] = jnp.zeros_like(carry_ref)

        @pl.when((i == 0) & (b == 0))
        def _():
            db_ref[...] = jnp.zeros_like(db_ref)

        dcv = dc_ref[...]
        dlogf = _tri_dot(tri_ref[...], dcv) + carry_ref[0:1, :]
        carry_ref[...] = jnp.broadcast_to(dlogf[0:1, :], carry_ref.shape)
        dz = dlogf * jax.nn.sigmoid(-(z_ref[...] + b_ref[0:1, :]))
        dz_ref[...] = dz
        db_ref[...] += _fold8(dz)

    def rev(b, i):
        return (b * nblk + nblk - 1 - i, 0)

    dz, db = pl.pallas_call(
        body, name=name, grid=(cfg.batch, nblk),
        in_specs=[pl.BlockSpec((tr, LANES), lambda b, i: (b * nblk + nblk - 1 - i, col_block)),
                  pl.BlockSpec((SUBLANES, LANES), lambda b, i: (0, 0)),
                  pl.BlockSpec((tr, tr), lambda b, i: (0, 0)),
                  pl.BlockSpec((tr, LANES), rev)],
        out_specs=[pl.BlockSpec((tr, LANES), rev), pl.BlockSpec((SUBLANES, LANES), lambda b, i: (0, 0))],
        out_shape=[jax.ShapeDtypeStruct((cfg.tokens, LANES), F32), jax.ShapeDtypeStruct((SUBLANES, LANES), F32)],
        scratch_shapes=[pltpu.VMEM((SUBLANES, LANES), F32)],
        compiler_params=_params(("arbitrary", "arbitrary")),
    )(proj, jnp.broadcast_to(bias, (SUBLANES, LANES)), _tri(tr, upper=True), dcum)
    return dz, jnp.sum(db, axis=0)


FOX_BLOCK = 256


def _fox_scores(q2, k2, e, half, mask, cref, ck_row):
    qe = jnp.where(half == (e == 0), q2, jnp.zeros_like(q2))
    s = lax.dot_general(qe, k2, _NT, preferred_element_type=F32)
    return jnp.where(mask, s + (cref - ck_row), NEG)


def _causal(qi, ki, tq):
    r = lax.broadcasted_iota(jnp.int32, (tq, tq), 0) + qi * tq
    c = lax.broadcasted_iota(jnp.int32, (tq, tq), 1) + ki * tq
    return r >= c


def _fox_fwd(q, kv, cum_t, cfg, *, name):
    t, d, hrows = cfg.tokens, cfg.d_model, cum_t.shape[0]
    tq = _blk(cfg.seq, FOX_BLOCK)
    nq = cfg.seq // tq

    def body(q_ref, k_ref, v_ref, cq_ref, ck_ref, o_ref, lse_ref, m_ref, l_ref, acc_ref):
        qi, ki = pl.program_id(1), pl.program_id(2)

        @pl.when(ki == 0)
        def _():
            m_ref[...] = jnp.full_like(m_ref, NEG)
            l_ref[...] = jnp.zeros_like(l_ref)
            acc_ref[...] = jnp.zeros_like(acc_ref)

        @pl.when(ki <= qi)
        def _():
            mask = _causal(qi, ki, tq)
            half = _half_mask((tq, LANES))
            for hp in range(d // LANES):
                sl = slice(hp * LANES, (hp + 1) * LANES)
                q2, k2, v2 = q_ref[:, sl], k_ref[:, sl], v_ref[:, sl]
                alphas, pvs = [], []
                for e in range(2):
                    h = 2 * hp + e
                    s = _fox_scores(q2, k2, e, half, mask, cq_ref[h:h + 1, 0:1], ck_ref[h:h + 1, :])
                    m_prev = m_ref[h]
                    m_new = jnp.maximum(m_prev, jnp.max(s, axis=1, keepdims=True))
                    alpha = jnp.exp(m_prev - m_new)
                    p = jnp.exp(s - m_new[:, 0:1])
                    l_ref[h] = alpha * l_ref[h] + jnp.sum(p, axis=1, keepdims=True)
                    m_ref[h] = m_new
                    alphas.append(alpha)
                    pvs.append(_dot2(p, v2))
                acc = acc_ref[:, sl]
                acc_ref[:, sl] = jnp.where(half, alphas[0] * acc + pvs[0], alphas[1] * acc + pvs[1])

        @pl.when(ki == qi)
        def _():
            half = _half_mask((tq, LANES))
            for hp in range(d // LANES):
                sl = slice(hp * LANES, (hp + 1) * LANES)
                h0, h1 = 2 * hp, 2 * hp + 1
                inv = jnp.where(half, 1.0 / l_ref[h0], 1.0 / l_ref[h1])
                o_ref[:, sl] = acc_ref[:, sl] * inv
                lse0 = m_ref[h0] + jnp.log(l_ref[h0]) - cq_ref[h0:h0 + 1, 0:1]
                lse1 = m_ref[h1] + jnp.log(l_ref[h1]) - cq_ref[h1:h1 + 1, 0:1]
                lse_ref[:, sl] = jnp.where(half, lse0, lse1)

    def qrow(b, qi, ki):
        return (b * nq + qi, 0)

    def krow(b, qi, ki):
        return (b * nq + jnp.minimum(ki, qi), 0)

    o, lse = pl.pallas_call(
        body, name=name, grid=(cfg.batch, nq, nq),
        in_specs=[pl.BlockSpec((tq, d), qrow),
                  pl.BlockSpec((tq, d), krow),
                  pl.BlockSpec((tq, d), lambda b, qi, ki: (b * nq + jnp.minimum(ki, qi), 1)),
                  pl.BlockSpec((hrows, tq), lambda b, qi, ki: (0, b * nq + qi)),
                  pl.BlockSpec((hrows, tq), lambda b, qi, ki: (0, b * nq + jnp.minimum(ki, qi)))],
        out_specs=[pl.BlockSpec((tq, d), qrow), pl.BlockSpec((tq, d), qrow)],
        out_shape=[jax.ShapeDtypeStruct((t, d), F32), jax.ShapeDtypeStruct((t, d), F32)],
        scratch_shapes=[pltpu.VMEM((cfg.heads, tq, LANES), F32), pltpu.VMEM((cfg.heads, tq, LANES), F32),
                        pltpu.VMEM((tq, d), F32)],
        compiler_params=_params(("parallel", "parallel", "arbitrary")),
    )(q, kv, kv, cum_t, cum_t)
    return o, lse


def _fox_bwd_q(q, kv, cum_t, do, lse, dsum, cfg, *, name):
    t, d, hrows = cfg.tokens, cfg.d_model, cum_t.shape[0]
    tq = _blk(cfg.seq, FOX_BLOCK)
    nq = cfg.seq // tq

    def body(q_ref, k_ref, v_ref, cq_ref, ck_ref, do_ref, l_ref, ds_ref, dq_ref, acc_ref):
        qi, ki = pl.program_id(1), pl.program_id(2)

        @pl.when(ki == 0)
        def _():
            acc_ref[...] = jnp.zeros_like(acc_ref)

        @pl.when(ki <= qi)
        def _():
            mask = _causal(qi, ki, tq)
            half = _half_mask((tq, LANES))
            for hp in range(d // LANES):
                sl = slice(hp * LANES, (hp + 1) * LANES)
                q2, k2, v2 = q_ref[:, sl], k_ref[:, sl], v_ref[:, sl]
                do2 = do_ref[:, sl].astype(BF16)
                dqs = []
                for e in range(2):
                    h = 2 * hp + e
                    lane0 = hp * LANES + e * HEAD_DIM
                    cref = cq_ref[h:h + 1, 0:1]
                    s = _fox_scores(q2, k2, e, half, mask, cref, ck_ref[h:h + 1, :])
                    p = jnp.exp(s - (l_ref[:, lane0:lane0 + 1] + cref))
                    doe = jnp.where(half == (e == 0), do2, jnp.zeros_like(do2))
                    dp = lax.dot_general(doe, v2, _NT, preferred_element_type=F32)
                    dsc = (p * (dp - ds_ref[:, lane0:lane0 + 1])).astype(BF16)
                    dqs.append(jnp.dot(dsc, k2, preferred_element_type=F32))
                acc_ref[:, sl] += jnp.where(half, dqs[0], dqs[1])

        @pl.when(ki == qi)
        def _():
            dq_ref[...] = acc_ref[...]

    def qrow(b, qi, ki):
        return (b * nq + qi, 0)

    return pl.pallas_call(
        body, name=name, grid=(cfg.batch, nq, nq),
        in_specs=[pl.BlockSpec((tq, d), qrow),
                  pl.BlockSpec((tq, d), lambda b, qi, ki: (b * nq + jnp.minimum(ki, qi), 0)),
                  pl.BlockSpec((tq, d), lambda b, qi, ki: (b * nq + jnp.minimum(ki, qi), 1)),
                  pl.BlockSpec((hrows, tq), lambda b, qi, ki: (0, b * nq + qi)),
                  pl.BlockSpec((hrows, tq), lambda b, qi, ki: (0, b * nq + jnp.minimum(ki, qi))),
                  pl.BlockSpec((tq, d), qrow), pl.BlockSpec((tq, d), qrow), pl.BlockSpec((tq, d), qrow)],
        out_specs=pl.BlockSpec((tq, d), qrow),
        out_shape=jax.ShapeDtypeStruct((t, d), F32),
        scratch_shapes=[pltpu.VMEM((tq, d), F32)],
        compiler_params=_params(("parallel", "parallel", "arbitrary")),
    )(q, kv, kv, cum_t, cum_t, do, lse, dsum)


def _fox_bwd_kv(q, kv, cum_t, do, lse, dsum, cfg, *, name):
    t, d, hrows = cfg.tokens, cfg.d_model, cum_t.shape[0]
    tq = _blk(cfg.seq, FOX_BLOCK)
    nq = cfg.seq // tq

    def body(q_ref, k_ref, v_ref, cq_ref, ck_ref, do_ref, l_ref, ds_ref, dk_ref, dv_ref, dc_ref,
             kacc_ref, vacc_ref, cacc_ref):
        ki, qi = pl.program_id(1), pl.program_id(2)

        @pl.when(qi == 0)
        def _():
            kacc_ref[...] = jnp.zeros_like(kacc_ref)
            vacc_ref[...] = jnp.zeros_like(vacc_ref)
            cacc_ref[...] = jnp.zeros_like(cacc_ref)

        @pl.when(qi >= ki)
        def _():
            mask = _causal(qi, ki, tq)
            half = _half_mask((tq, LANES))
            for hp in range(d // LANES):
                sl = slice(hp * LANES, (hp + 1) * LANES)
                q2, k2, v2 = q_ref[:, sl], k_ref[:, sl], v_ref[:, sl]
                do2 = do_ref[:, sl].astype(BF16)
                dks, dvs = [], []
                for e in range(2):
                    h = 2 * hp + e
                    lane0 = hp * LANES + e * HEAD_DIM
                    cref = cq_ref[h:h + 1, 0:1]
                    s = _fox_scores(q2, k2, e, half, mask, cref, ck_ref[h:h + 1, :])
                    p = jnp.exp(s - (l_ref[:, lane0:lane0 + 1] + cref))
                    doe = jnp.where(half == (e == 0), do2, jnp.zeros_like(do2))
                    dp = lax.dot_general(doe, v2, _NT, preferred_element_type=F32)
                    dsf = p * (dp - ds_ref[:, lane0:lane0 + 1])
                    cacc_ref[h:h + 1, :] -= jnp.sum(dsf, axis=0, keepdims=True)
                    dks.append(lax.dot_general(dsf.astype(BF16), q2, _TN, preferred_element_type=F32))
                    dvs.append(lax.dot_general(p.astype(BF16), do2, _TN, preferred_element_type=F32))
                kacc_ref[:, sl] += jnp.where(half, dks[0], dks[1])
                vacc_ref[:, sl] += jnp.where(half, dvs[0], dvs[1])

        @pl.when(qi == nq - 1)
        def _():
            dk_ref[...] = kacc_ref[...]
            dv_ref[...] = vacc_ref[...]
            dc_ref[...] = cacc_ref[...]

    def qrow(b, ki, qi):
        return (b * nq + jnp.maximum(qi, ki), 0)

    def krow(b, ki, qi):
        return (b * nq + ki, 0)

    return pl.pallas_call(
        body, name=name, grid=(cfg.batch, nq, nq),
        in_specs=[pl.BlockSpec((tq, d), qrow),
                  pl.BlockSpec((tq, d), krow),
                  pl.BlockSpec((tq, d), lambda b, ki, qi: (b * nq + ki, 1)),
                  pl.BlockSpec((hrows, tq), lambda b, ki, qi: (0, b * nq + jnp.maximum(qi, ki))),
                  pl.BlockSpec((hrows, tq), lambda b, ki, qi: (0, b * nq + ki)),
                  pl.BlockSpec((tq, d), qrow), pl.BlockSpec((tq, d), qrow), pl.BlockSpec((tq, d), qrow)],
        out_specs=[pl.BlockSpec((tq, d), krow), pl.BlockSpec((tq, d), krow),
                   pl.BlockSpec((hrows, tq), lambda b, ki, qi: (0, b * nq + ki))],
        out_shape=[jax.ShapeDtypeStruct((t, d), F32), jax.ShapeDtypeStruct((t, d), F32),
                   jax.ShapeDtypeStruct((hrows, t), F32)],
        scratch_shapes=[pltpu.VMEM((tq, d), F32), pltpu.VMEM((tq, d), F32), pltpu.VMEM((hrows, tq), F32)],
        compiler_params=_params(("parallel", "parallel", "arbitrary")),
    )(q, kv, kv, cum_t, cum_t, do, lse, dsum)


AUG = LANES
BIAS_TERMS = 3


def _fox_aug_q(qp, cfg):
    t, hh = cfg.tokens, cfg.heads
    q3 = qp.reshape(t, hh, HEAD_DIM)
    ones = jnp.ones((t, hh, BIAS_TERMS), BF16)
    zeros = jnp.zeros((t, hh, AUG - HEAD_DIM - BIAS_TERMS), BF16)
    return jnp.concatenate([q3, ones, zeros], axis=2).reshape(t, hh * AUG).T


def _fox_aug_k(k, cum, cfg):
    t, hh = cfg.tokens, cfg.heads
    c = -cum
    hi = lax.reduce_precision(c, 8, 7)
    mid = lax.reduce_precision(c - hi, 8, 7)
    lo = c - hi - mid
    zeros = jnp.zeros((t, hh, AUG - HEAD_DIM - BIAS_TERMS), BF16)
    parts = [k.reshape(t, hh, HEAD_DIM)] + [z.astype(BF16)[..., None] for z in (hi, mid, lo)] + [zeros]
    return jnp.concatenate(parts, axis=2).reshape(t, hh * AUG)


def _keys_visible(tq):
    s = lax.broadcasted_iota(jnp.int32, (tq, tq), 0)
    t = lax.broadcasted_iota(jnp.int32, (tq, tq), 1)
    return s <= t


def _fox_fwd_t(qa_t, k_aug, v_t, cfg, *, name):
    t, d, hh = cfg.tokens, cfg.d_model, cfg.heads
    tq = _blk(cfg.seq, FOX_BLOCK)
    nq = cfg.seq // tq

    def body(qa_ref, ka_ref, vt_ref, o_ref, lse_ref, m_ref, l_ref, acc_ref):
        qi, ki = pl.program_id(1), pl.program_id(2)

        @pl.when(ki == 0)
        def _():
            m_ref[...] = jnp.full_like(m_ref, NEG)
            l_ref[...] = jnp.zeros_like(l_ref)
            acc_ref[...] = jnp.zeros_like(acc_ref)

        def step(diagonal):
            for h in range(hh):
                rows = slice(h * HEAD_DIM, (h + 1) * HEAD_DIM)
                s = jnp.dot(ka_ref[:, h * AUG:(h + 1) * AUG], qa_ref[h * AUG:(h + 1) * AUG, :],
                            preferred_element_type=F32)
                if diagonal:
                    s = jnp.where(_keys_visible(tq), s, NEG)
                m_prev = m_ref[h:h + 1, :]
                m_new = jnp.maximum(m_prev, jnp.max(s, axis=0, keepdims=True))
                alpha = jnp.exp(m_prev - m_new)
                p = jnp.exp(s - m_new)
                l_ref[h:h + 1, :] = alpha * l_ref[h:h + 1, :] + jnp.sum(p, axis=0, keepdims=True)
                m_ref[h:h + 1, :] = m_new
                hi = p.astype(BF16)
                lo = (p - hi.astype(F32)).astype(BF16)
                vt = vt_ref[rows, :]
                acc_ref[rows, :] = (alpha * acc_ref[rows, :] + jnp.dot(vt, hi, preferred_element_type=F32)
                                    + jnp.dot(vt, lo, preferred_element_type=F32))

        pl.when(ki < qi)(functools.partial(step, False))
        pl.when(ki == qi)(functools.partial(step, True))

        @pl.when(ki == qi)
        def _():
            for h in range(hh):
                rows = slice(h * HEAD_DIM, (h + 1) * HEAD_DIM)
                o_ref[rows, :] = acc_ref[rows, :] * (1.0 / l_ref[h:h + 1, :])
            lse_ref[...] = m_ref[...] + jnp.log(l_ref[...])

    def qcol(b, qi, ki):
        return (0, b * nq + qi)

    return pl.pallas_call(
        body, name=name, grid=(cfg.batch, nq, nq),
        in_specs=[pl.BlockSpec((hh * AUG, tq), qcol),
                  pl.BlockSpec((tq, hh * AUG), lambda b, qi, ki: (b * nq + jnp.minimum(ki, qi), 0)),
                  pl.BlockSpec((d, tq), lambda b, qi, ki: (0, b * nq + jnp.minimum(ki, qi)))],
        out_specs=[pl.BlockSpec((d, tq), qcol), pl.BlockSpec((hh, tq), qcol)],
        out_shape=[jax.ShapeDtypeStruct((d, t), F32), jax.ShapeDtypeStruct((hh, t), F32)],
        scratch_shapes=[pltpu.VMEM((hh, tq), F32), pltpu.VMEM((hh, tq), F32), pltpu.VMEM((d, tq), F32)],
        compiler_params=_params(("parallel", "parallel", "arbitrary")),
    )(qa_t, k_aug, v_t)


def _head_dot_t(a_t, b_t, cfg, *, name):
    t, d, hh = cfg.tokens, cfg.d_model, cfg.heads
    tc = _blk(t, 2 * ROW_BLOCK)

    def body(a_ref, b_ref, o_ref):
        for h in range(hh):
            rows = slice(h * HEAD_DIM, (h + 1) * HEAD_DIM)
            o_ref[h:h + 1, :] = jnp.sum(a_ref[rows, :].astype(F32) * b_ref[rows, :], axis=0, keepdims=True)

    return pl.pallas_call(
        body, name=name, grid=(t // tc,),
        in_specs=[pl.BlockSpec((d, tc), lambda i: (0, i)), pl.BlockSpec((d, tc), lambda i: (0, i))],
        out_specs=pl.BlockSpec((hh, tc), lambda i: (0, i)),
        out_shape=jax.ShapeDtypeStruct((hh, t), F32),
        compiler_params=_params(("parallel",)),
    )(a_t, b_t)


def _fox_bwd_t(qa_t, k_aug, k_t, v, do_t, do, lse, dsum, cfg, *, name):
    t, d, hh = cfg.tokens, cfg.d_model, cfg.heads
    tq = _blk(cfg.seq, FOX_BLOCK)
    nq = cfg.seq // tq

    def body(qa_ref, ka_ref, kt_ref, v_ref, dot_ref, do_ref, lse_ref, ds_ref, dq_hbm, dk_ref, dv_ref, dc_ref,
             dq_acc, sem):
        b, ki, qi = pl.program_id(0), pl.program_id(1), pl.program_id(2)
        qq = jnp.maximum(qi, ki)

        @pl.when((ki == 0) & (qi == 0))
        def _():
            dq_acc[...] = jnp.zeros_like(dq_acc)

        @pl.when(qi == 0)
        def _():
            dk_ref[...] = jnp.zeros_like(dk_ref)
            dv_ref[...] = jnp.zeros_like(dv_ref)
            dc_ref[...] = jnp.zeros_like(dc_ref)

        def step(diagonal):
            upper = lax.broadcasted_iota(jnp.int32, (LANES, tq), 0) < HEAD_DIM
            half = _half_mask((tq, LANES))
            for hp in range(hh // 2):
                pair = slice(hp * LANES, (hp + 1) * LANES)
                dvs = []
                for e in range(2):
                    h = 2 * hp + e
                    rows = slice(h * HEAD_DIM, (h + 1) * HEAD_DIM)
                    aug = slice(h * AUG, (h + 1) * AUG)
                    s = jnp.dot(ka_ref[:, aug], qa_ref[aug, :], preferred_element_type=F32)
                    if diagonal:
                        s = jnp.where(_keys_visible(tq), s, NEG)
                    p = jnp.exp(s - lse_ref[h:h + 1, :])
                    dot2 = dot_ref[pair, :]
                    dote = jnp.where(upper == (e == 0), dot2, jnp.zeros_like(dot2))
                    dp = jnp.dot(v_ref[:, pair], dote, preferred_element_type=F32)
                    dsf = p * (dp - ds_ref[h:h + 1, :])
                    dc_ref[:, h:h + 1] -= jnp.sum(dsf, axis=1, keepdims=True)
                    dsc = dsf.astype(BF16)
                    dvs.append(jnp.dot(p.astype(BF16), do_ref[:, pair], preferred_element_type=F32))
                    dk_ref[:, aug] += lax.dot_general(dsc, qa_ref[aug, :], _NT, preferred_element_type=F32)
                    dq_acc[qq, rows, :] += jnp.dot(kt_ref[rows, :], dsc, preferred_element_type=F32)
                dv_ref[:, pair] += jnp.where(half, dvs[0], dvs[1])

        pl.when(qi > ki)(functools.partial(step, False))
        pl.when(qi == ki)(functools.partial(step, True))

        @pl.when((ki == nq - 1) & (qi == nq - 1))
        def _():
            cp = pltpu.make_async_copy(dq_acc, dq_hbm.at[b], sem)
            cp.start()
            cp.wait()

    def qcol(b, ki, qi):
        return (0, b * nq + jnp.maximum(qi, ki))

    def krow(b, ki, qi):
        return (b * nq + ki, 0)

    return pl.pallas_call(
        body, name=name, grid=(cfg.batch, nq, nq),
        in_specs=[pl.BlockSpec((hh * AUG, tq), qcol),
                  pl.BlockSpec((tq, hh * AUG), krow),
                  pl.BlockSpec((d, tq), lambda b, ki, qi: (0, b * nq + ki)),
                  pl.BlockSpec((tq, d), krow),
                  pl.BlockSpec((d, tq), qcol),
                  pl.BlockSpec((tq, d), lambda b, ki, qi: (b * nq + jnp.maximum(qi, ki), 0)),
                  pl.BlockSpec((hh, tq), qcol), pl.BlockSpec((hh, tq), qcol)],
        out_specs=[pl.BlockSpec(memory_space=pl.ANY), pl.BlockSpec((tq, hh * AUG), krow),
                   pl.BlockSpec((tq, d), krow), pl.BlockSpec((tq, LANES), krow)],
        out_shape=[jax.ShapeDtypeStruct((cfg.batch, nq, d, tq), F32), jax.ShapeDtypeStruct((t, hh * AUG), F32),
                   jax.ShapeDtypeStruct((t, d), F32), jax.ShapeDtypeStruct((t, LANES), F32)],
        scratch_shapes=[pltpu.VMEM((nq, d, tq), F32), pltpu.SemaphoreType.DMA],
        compiler_params=_params(("arbitrary", "arbitrary", "arbitrary")),
    )(qa_t, k_aug, k_t, v, do_t, do, lse, dsum)


WIDE = 1536


def _fwd(a, w, *, name, res=None, scale=1.0):
    return _mm(a, w, form="F", out_dtype=F32, name=name, bn=WIDE, bk=WIDE, res=res, scale=scale)


def _bwd(dy, w, *, name, scale=1.0):
    return _mm(dy, w, form="B", out_dtype=F32, name=name, bn=WIDE, bk=WIDE, scale=scale)


def _wgrad(a, dy, w, *, name, scale=1.0):
    return _mm_grad(a, dy, w.shape[0], name=name, bm=WIDE, bn=WIDE, scale=scale)


def _ffn_fwd(h, g, w_in, w_out, tag):
    n = _rms_fwd(h, g, name=f"{tag}_norm")
    gate, up, a = _ffn_in_act(n, w_in, name=f"{tag}_in")
    return _fwd(a, w_out, name=f"{tag}_out", res=h, scale=0.5), (n, gate, up, a)


def _ffn_bwd(dh_out, h, g, w_in, w_out, saved, tag):
    n, gate, up, a = saved
    du = _ffn_out_dx_act(dh_out, w_out, gate, up, name=f"{tag}_out_dx", scale=0.5)
    dw_out = _wgrad(a, dh_out, w_out, name=f"{tag}_out_dw", scale=0.5)
    dn = _mm_back2(du, w_in, name=f"{tag}_in_dx", bn=WIDE, bk=WIDE)
    dw_in = _mm_grad(n, du, w_in.shape[0], name=f"{tag}_in_dw", bm=WIDE, bn=WIDE)
    dh, dg = _rms_bwd(h, g, dn, dh_out, name=f"{tag}_norm_bwd")
    return dh, dg, dw_in, dw_out


def _head_gain(g, heads, scale=1.0):
    return jnp.tile(g.astype(F32) * scale, heads)


def _local_step(cfg, x, positions, target, w, s):
    d, hh = cfg.d_model, cfg.heads
    cos, sin = _rope_tables(positions)
    ones = jnp.ones((d,), F32)

    h1, ffn0 = _ffn_fwd(x, s["ffn_norm"][0, 0], w["ffn_w_in"][0][0], w["ffn_w_out"][0][0], "ffn00")
    hn_a = _rms_fwd(h1, s["mix_norm"][0], name="a_norm")
    qkv = _fwd(hn_a, w["a_w_qkv"], name="a_qkv")
    kinds_a = ["rope", "rope", "cast"] * len(DILATIONS)
    gains_a = jnp.stack([z for g in range(len(DILATIONS)) for z in (
        _head_gain(s["a_q_norm"][g], hh, Q_SCALE), _head_gain(s["a_k_norm"][g], hh), ones)])
    qkvp = [_hn_fwd(qkv, gains_a[3 * g:3 * g + 3], kinds_a[:3], d, cos, sin, name=f"a_qk_norm{g}", col0=3 * g)
            for g in range(len(DILATIONS))]
    lay = [_band_layouts(qkvp[g], dil, cfg) for g, dil in enumerate(DILATIONS)]
    band = [_band_fwd_t(*lay[g], dil, cfg, name=f"a_band{g}") for g, dil in enumerate(DILATIONS)]
    mixed, lse_a = _mix_fwd([_from_classes_t(o) for o, _ in band],
                            [jnp.repeat(_from_classes_t(l), HEAD_DIM, axis=1) for _, l in band], name="a_mix")
    h2 = _fwd(mixed, w["a_w_o"], name="a_out", res=h1)
    h3, ffn1 = _ffn_fwd(h2, s["ffn_norm"][0, 1], w["ffn_w_in"][0][1], w["ffn_w_out"][0][1], "ffn01")

    kn = _rms_fwd(h3, s["kv_norm"], name="kv_norm")
    proj = _fwd(kn, w["kv_w"], name="kv_proj")
    kinds_kv = ["norm", "cast"]
    gains_kv = jnp.stack([_head_gain(s["kv_k_norm"], hh), ones])
    kvp = _hn_fwd(proj, gains_kv, kinds_kv, d, cos, sin, name="kv_k_norm")
    gate_col = 2 * d // LANES
    bias = jnp.pad(s["kv_b_f"].astype(F32), (0, LANES - hh))
    cum = _gate_fwd(proj, gate_col, bias, cfg, name="kv_gate")
    k_b, v_b = kvp[:, :d], kvp[:, d:]
    k_aug = _fox_aug_k(k_b, cum[:, :hh], cfg)

    h4, ffn2 = _ffn_fwd(h3, s["ffn_norm"][1, 0], w["ffn_w_in"][1][0], w["ffn_w_out"][1][0], "ffn10")
    hn_b = _rms_fwd(h4, s["mix_norm"][1], name="b_norm")
    qraw = _fwd(hn_b, w["b_w_q"], name="b_q")
    gains_b = _head_gain(s["b_q_norm"][0], hh, Q_SCALE)[None]
    qp = _hn_fwd(qraw, gains_b, ["norm"], d, cos, sin, name="b_q_norm")
    qa_t = _fox_aug_q(qp, cfg)
    o_t, lse_b = _fox_fwd_t(qa_t, k_aug, v_b.T, cfg, name="b_fox")
    o_b = o_t.T
    h5 = _fwd(o_b, w["b_w_o"], name="b_out", res=h4)
    h6, ffn3 = _ffn_fwd(h5, s["ffn_norm"][1, 1], w["ffn_w_in"][1][1], w["ffn_w_out"][1][1], "ffn11")

    loss, dh6 = _loss_fwd_bwd(h6, target, name="loss")

    dh5, dg11, dwi11, dwo11 = _ffn_bwd(dh6, h5, s["ffn_norm"][1, 1], w["ffn_w_in"][1][1], w["ffn_w_out"][1][1],
                                       ffn3, "ffn11")
    do_b = _bwd(dh5, w["b_w_o"], name="b_out_dx")
    dw_bo = _wgrad(o_b, dh5, w["b_w_o"], name="b_out_dw")
    do_bf = do_b.astype(BF16)
    do_t = do_bf.T
    dsum_b = _head_dot_t(do_t, o_t, cfg, name="b_dsum")
    dq4, dk_aug, dv_b, dcum = _fox_bwd_t(qa_t, k_aug, k_b.T, v_b, do_t, do_bf, lse_b, dsum_b, cfg,
                                         name="b_fox_bwd")
    dq_b = dq4.transpose(0, 1, 3, 2).reshape(cfg.tokens, d)
    dk_b = dk_aug.reshape(cfg.tokens, hh, AUG)[:, :, :HEAD_DIM].reshape(cfg.tokens, d)
    dqraw, dgq = _hn_bwd(qraw, [dq_b], gains_b, ["norm"], d, cos, sin, name="b_q_norm_bwd")
    dhn_b = _bwd(dqraw, w["b_w_q"], name="b_q_dx")
    dw_bq = _wgrad(hn_b, dqraw, w["b_w_q"], name="b_q_dw")
    dh4, dmix1 = _rms_bwd(h4, s["mix_norm"][1], dhn_b, dh5, name="b_norm_bwd")
    dh3, dg10, dwi10, dwo10 = _ffn_bwd(dh4, h3, s["ffn_norm"][1, 0], w["ffn_w_in"][1][0], w["ffn_w_out"][1][0],
                                       ffn2, "ffn10")

    dkvraw, dgk = _hn_bwd(proj, [dk_b, dv_b], gains_kv, kinds_kv, d, cos, sin,
                          name="kv_k_norm_bwd")
    dz, dbias = _gate_bwd(proj, gate_col, bias, dcum, cfg, name="kv_gate_bwd")
    pad_cols = w["kv_w"].shape[2] - 2 * d - LANES
    dproj = jnp.concatenate([dkvraw, dz.astype(BF16), jnp.zeros((cfg.tokens, pad_cols), BF16)], axis=1)
    dkn = _bwd(dproj, w["kv_w"], name="kv_proj_dx")
    dw_kv = _wgrad(kn, dproj, w["kv_w"], name="kv_proj_dw")
    dh3, dkvn = _rms_bwd(h3, s["kv_norm"], dkn, dh3, name="kv_norm_bwd")

    dh2, dg01, dwi01, dwo01 = _ffn_bwd(dh3, h2, s["ffn_norm"][0, 1], w["ffn_w_in"][0][1], w["ffn_w_out"][0][1],
                                       ffn1, "ffn01")
    dmixed = _bwd(dh2, w["a_w_o"], name="a_out_dx")
    dw_ao = _wgrad(mixed, dh2, w["a_w_o"], name="a_out_dw")
    dsum_a = _head_dot(dmixed, mixed, name="a_dsum")
    dqkvp = []
    dmixed_bf = dmixed.astype(BF16)
    lse_h, dsum_h = lse_a[:, ::HEAD_DIM], dsum_a[:, ::HEAD_DIM]
    for g, dil in enumerate(DILATIONS):
        dq_t, dk_a, dv_a = _band_bwd_t(
            *lay[g], _to_classes_t(dmixed_bf, dil, d), dmixed_bf.reshape(cfg.tokens // dil, dil * d),
            _to_classes_t(lse_h, dil, hh), _to_classes_t(dsum_h, dil, hh), dil, cfg, name=f"a_band{g}_bwd")
        dqkvp += [_from_classes_t(dq_t), dk_a.reshape(cfg.tokens, d), dv_a.reshape(cfg.tokens, d)]
    dqkv, dga = _hn_bwd(qkv, dqkvp, gains_a, kinds_a, d, cos, sin, name="a_qk_norm_bwd")
    dhn_a = _bwd(dqkv, w["a_w_qkv"], name="a_qkv_dx")
    dw_qkv = _wgrad(hn_a, dqkv, w["a_w_qkv"], name="a_qkv_dw")
    dh1, dmix0 = _rms_bwd(h1, s["mix_norm"][0], dhn_a, dh2, name="a_norm_bwd")
    dx, dg00, dwi00, dwo00 = _ffn_bwd(dh1, x, s["ffn_norm"][0, 0], w["ffn_w_in"][0][0], w["ffn_w_out"][0][0],
                                      ffn0, "ffn00")

    dw = {
        "ffn_w_in": [[dwi00, dwi01], [dwi10, dwi11]],
        "ffn_w_out": [[dwo00, dwo01], [dwo10, dwo11]],
        "a_w_qkv": dw_qkv, "a_w_o": dw_ao, "kv_w": dw_kv, "b_w_q": dw_bq, "b_w_o": dw_bo,
    }
    ds = {
        "ffn_norm": jnp.stack([jnp.stack([dg00, dg01]), jnp.stack([dg10, dg11])]),
        "mix_norm": jnp.stack([dmix0, dmix1]),
        "a_q_norm": jnp.stack([dga[3 * g] for g in range(len(DILATIONS))])[None] * Q_SCALE,
        "a_k_norm": jnp.stack([dga[3 * g + 1] for g in range(len(DILATIONS))])[None],
        "kv_norm": dkvn,
        "kv_b_f": dbias[:hh],
        "kv_k_norm": dgk[0],
        "b_q_norm": dgq * Q_SCALE,
    }
    return loss, dx, dw, ds


MESH_ID = pl.DeviceIdType.MESH
ANY = pl.BlockSpec(memory_space=pl.ANY)
PACK_COLS = 1024
PACK_ROW_ALIGN = 32


def _me():
    return lax.axis_index("x"), lax.axis_index("y"), lax.axis_index("c")


def _other_chips(x, y):
    return [(1 - x, y), (x, 1 - y), (1 - x, 1 - y)]


def _all_gather_small(v, *, name):
    r = v.shape[0]

    def body(v_ref, out_ref, send_sems, recv_sems):
        x, y, c = _me()
        me = 4 * x + 2 * y + c
        out_ref[me] = v_ref[...]
        copies = []
        for k in range(1, N_DEV):
            fx, fy, fc = (k >> 2) & 1, (k >> 1) & 1, k & 1
            peer = (1 - x if fx else x, 1 - y if fy else y, 1 - c if fc else c)
            copies.append(pltpu.make_async_remote_copy(
                src_ref=v_ref, dst_ref=out_ref.at[me], send_sem=send_sems.at[k - 1], recv_sem=recv_sems.at[k - 1],
                device_id=peer, device_id_type=MESH_ID))
        for cp in copies:
            cp.start()
        for cp in copies:
            cp.wait()

    return pl.pallas_call(
        body, name=name,
        in_specs=[pl.BlockSpec(memory_space=pltpu.VMEM)], out_specs=pl.BlockSpec(memory_space=pltpu.VMEM),
        out_shape=jax.ShapeDtypeStruct((N_DEV, r, LANES), v.dtype),
        scratch_shapes=[pltpu.SemaphoreType.DMA((N_DEV - 1,)), pltpu.SemaphoreType.DMA((N_DEV - 1,))],
    )(v)


def _all_gather_chips(v, *, name):
    rh = v.shape[0] // 2

    def body(v_ref, out_ref, send_sems, recv_sems):
        x, y, c = _me()
        j = 2 * x + y
        chips = _other_chips(x, y)

        def half(chip, core):
            return out_ref.at[chip, pl.ds(core * rh, rh)]

        first = [pltpu.make_async_remote_copy(
            src_ref=v_ref.at[pl.ds(c * rh, rh)], dst_ref=half(j, c), send_sem=send_sems.at[k],
            recv_sem=recv_sems.at[k], device_id=(px, py, c), device_id_type=MESH_ID)
            for k, (px, py) in enumerate(chips)]
        for cp in first:
            cp.start()
        passed = [pltpu.make_async_remote_copy(
            src_ref=half(2 * px + py, c), dst_ref=half(2 * px + py, c), send_sem=send_sems.at[3 + k],
            recv_sem=recv_sems.at[3 + k], device_id=(x, y, 1 - c), device_id_type=MESH_ID)
            for k, (px, py) in enumerate(chips)]
        for k in range(len(chips)):
            first[k].wait_recv()
            passed[k].start()
        for k, (px, py) in enumerate(chips):
            pltpu.make_async_remote_copy(
                src_ref=half(2 * px + py, 1 - c), dst_ref=half(2 * px + py, 1 - c), send_sem=send_sems.at[3 + k],
                recv_sem=recv_sems.at[3 + k], device_id=(x, y, 1 - c), device_id_type=MESH_ID).wait_recv()
        for cp in first + passed:
            cp.wait_send()

    return pl.pallas_call(
        body, name=name, in_specs=[ANY], out_specs=ANY,
        out_shape=jax.ShapeDtypeStruct((N_CHIPS,) + v.shape, v.dtype),
        scratch_shapes=[pltpu.SemaphoreType.DMA((2 * (N_CHIPS - 1),)), pltpu.SemaphoreType.DMA((2 * (N_CHIPS - 1),))],
    )(v)


def _swap_halves(g, *, name):
    n, r, cols = g.shape
    rh = r // 2

    def body(g_ref, out_ref, send_sem, recv_sem):
        x, y, c = _me()
        cp = pltpu.make_async_remote_copy(
            src_ref=g_ref.at[:, pl.ds((1 - c) * rh, rh)], dst_ref=out_ref, send_sem=send_sem, recv_sem=recv_sem,
            device_id=(x, y, 1 - c), device_id_type=MESH_ID)
        cp.start()
        cp.wait()

    return pl.pallas_call(
        body, name=name, in_specs=[ANY], out_specs=ANY,
        out_shape=jax.ShapeDtypeStruct((n, rh, cols), g.dtype),
        scratch_shapes=[pltpu.SemaphoreType.DMA, pltpu.SemaphoreType.DMA],
    )(g)


def _scatter_chips(v, *, name):
    def body(v_ref, out_ref, send_sems, recv_sems):
        x, y, c = _me()
        j = 2 * x + y
        copies = [pltpu.make_async_remote_copy(
            src_ref=v_ref.at[2 * px + py], dst_ref=out_ref.at[j], send_sem=send_sems.at[k], recv_sem=recv_sems.at[k],
            device_id=(px, py, c), device_id_type=MESH_ID) for k, (px, py) in enumerate(_other_chips(x, y))]
        for cp in copies:
            cp.start()
        for cp in copies:
            cp.wait()

    return pl.pallas_call(
        body, name=name, in_specs=[ANY], out_specs=ANY,
        out_shape=jax.ShapeDtypeStruct(v.shape, v.dtype),
        scratch_shapes=[pltpu.SemaphoreType.DMA((N_CHIPS - 1,)), pltpu.SemaphoreType.DMA((N_CHIPS - 1,))],
    )(v)


def _join_halves(v, *, name):
    def body(v_ref, out_ref, send_sem, recv_sem):
        x, y, c = _me()
        cp = pltpu.make_async_remote_copy(
            src_ref=v_ref, dst_ref=out_ref.at[c], send_sem=send_sem, recv_sem=recv_sem,
            device_id=(x, y, 1 - c), device_id_type=MESH_ID)
        cp.start()
        cp.wait()

    return pl.pallas_call(
        body, name=name, in_specs=[ANY], out_specs=ANY,
        out_shape=jax.ShapeDtypeStruct((2,) + v.shape, v.dtype),
        scratch_shapes=[pltpu.SemaphoreType.DMA, pltpu.SemaphoreType.DMA],
    )(v)


def _row_blk(rows, want):
    for b in range(min(rows, want) // SUBLANES * SUBLANES, 0, -SUBLANES):
        if rows % b == 0:
            return b
    return rows


def _add_own_half(g, got, *, name):
    n, r, cols = g.shape
    rh = r // 2
    tr = _row_blk(rh, 512)
    nb = rh // tr

    def body(c_ref, g_ref, got_ref, o_ref):
        del c_ref
        o_ref[...] = (g_ref[...] + got_ref[...]).astype(BF16)

    grid_spec = pltpu.PrefetchScalarGridSpec(
        num_scalar_prefetch=1, grid=(n, nb),
        in_specs=[pl.BlockSpec((None, tr, cols), lambda j, i, c: (j, c[0] * nb + i, 0)),
                  pl.BlockSpec((None, tr, cols), lambda j, i, c: (j, i, 0))],
        out_specs=pl.BlockSpec((None, tr, cols), lambda j, i, c: (j, i, 0)))
    return pl.pallas_call(
        body, name=name, grid_spec=grid_spec, out_shape=jax.ShapeDtypeStruct((n, rh, cols), BF16),
        compiler_params=_params(("parallel", "parallel")),
    )(lax.axis_index("c").astype(jnp.int32).reshape(1), g, got)


def _sum_parts(parts, *, name):
    n, r, cols = parts.shape
    tr = _row_blk(r, 512)

    def body(*refs):
        o_ref = refs[n]
        acc = refs[0][...].astype(F32)
        for p_ref in refs[1:n]:
            acc = acc + p_ref[...].astype(F32)
        o_ref[...] = acc

    return pl.pallas_call(
        body, name=name, grid=(r // tr,),
        in_specs=[pl.BlockSpec((None, tr, cols), functools.partial(lambda j, i: (j, i, 0), j)) for j in range(n)],
        out_specs=pl.BlockSpec((tr, cols), lambda i: (i, 0)),
        out_shape=jax.ShapeDtypeStruct((r, cols), F32),
        compiler_params=_params(("parallel",)),
    )(*([parts] * n))


def _adamw(w, m, v, g, *, name):
    shape = w.shape
    cols = shape[-1]
    w2, m2, v2, g2 = (z.reshape(-1, cols) for z in (w, m, v, g))
    rows = w2.shape[0]
    tr = _row_blk(rows, max(SUBLANES, (1 << 20) // (4 * cols)))

    def body(w_ref, m_ref, v_ref, g_ref, d_ref, nm_ref, nv_ref):
        gv = g_ref[...]
        nm = ADAM_B1 * m_ref[...] + (1.0 - ADAM_B1) * gv
        nv = ADAM_B2 * v_ref[...] + (1.0 - ADAM_B2) * jnp.square(gv)
        m_hat = nm / (1.0 - ADAM_B1 ** ADAM_STEP)
        v_hat = nv / (1.0 - ADAM_B2 ** ADAM_STEP)
        d_ref[...] = -ADAM_LR * (m_hat / (jnp.sqrt(v_hat) + ADAM_EPS) + ADAM_WD * w_ref[...])
        nm_ref[...] = nm
        nv_ref[...] = nv

    spec = pl.BlockSpec((tr, cols), lambda i: (i, 0))
    out = jax.ShapeDtypeStruct((rows, cols), F32)
    d, nm, nv = pl.pallas_call(
        body, name=name, grid=(rows // tr,), in_specs=[spec] * 4, out_specs=[spec] * 3, out_shape=[out] * 3,
        compiler_params=_params(("parallel",)),
    )(w2, m2, v2, g2)
    return d.reshape(shape), nm.reshape(shape), nv.reshape(shape)


def _pack_rows(size, cols, align):
    return -(-size // (cols * align)) * align


def _pack(arrs, lead, cols, align, total_align):
    lead_shape = arrs[0].shape[:lead]
    parts = []
    for a in arrs:
        flat = a.reshape(lead_shape + (-1,))
        size = flat.shape[-1]
        rows = _pack_rows(size, cols, align)
        flat = jnp.pad(flat, [(0, 0)] * lead + [(0, rows * cols - size)])
        parts.append(flat.reshape(lead_shape + (rows, cols)))
    total = sum(p.shape[lead] for p in parts)
    extra = -total % total_align
    if extra:
        parts.append(jnp.zeros(lead_shape + (extra, cols), parts[0].dtype))
    return jnp.concatenate(parts, axis=lead)


def _unpack(buf, shapes, lead, cols, align):
    lead_shape = buf.shape[:lead]
    out, row = [], 0
    for shp in shapes:
        size = 1
        for n in shp:
            size *= n
        rows = _pack_rows(size, cols, align)
        piece = lax.slice_in_dim(buf, row, row + rows, axis=lead).reshape(lead_shape + (-1,))
        out.append(piece[..., :size].reshape(lead_shape + tuple(shp)))
        row += rows
    return out


BIG = ("ffn_w_in", "ffn_w_out", "a_w_qkv", "a_w_o", "kv_w", "b_w_q", "b_w_o")
SMALL = ("ffn_norm", "mix_norm", "a_q_norm", "a_k_norm", "kv_norm", "kv_b_f", "kv_k_norm", "b_q_norm")
WEIGHTS = ("ffn_norm", "ffn_w_in", "ffn_w_out", "mix_norm", "a_w_qkv", "a_q_norm", "a_k_norm", "a_w_o",
           "kv_norm", "kv_w", "kv_b_f", "kv_k_norm", "b_w_q", "b_q_norm", "b_w_o")
GATE_PAD = 2 * LANES


def _stack_weights(sh, d):
    depth = sh["ffn_w_in"].shape[1]
    kv = sh["kv_w"].transpose(1, 0, 2).reshape(d, -1)
    kv = jnp.pad(kv, ((0, 0), (0, 2 * d + GATE_PAD - kv.shape[1])))
    return {
        "ffn_w_in": [[sh["ffn_w_in"][:, l, i] for i in range(2)] for l in range(depth)],
        "ffn_w_out": [[sh["ffn_w_out"][:, l, i].reshape(1, -1, d) for i in range(2)] for l in range(depth)],
        "a_w_qkv": sh["a_w_qkv"][:, 0],
        "a_w_o": sh["a_w_o"].reshape(1, d, d),
        "kv_w": kv[None],
        "b_w_q": sh["b_w_q"].reshape(1, d, d),
        "b_w_o": sh["b_w_o"].reshape(1, d, d),
    }


def _unstack_grads(dw, d, heads):
    def rows4(z):
        return z.reshape(N_CHIPS, -1, d)

    kv_cols = 2 * d + heads
    kv = dw["kv_w"][0][:, :kv_cols].reshape(d, N_CHIPS, kv_cols // N_CHIPS).transpose(1, 0, 2)
    return [
        jnp.stack([jnp.stack(row, axis=1) for row in dw["ffn_w_in"]], axis=1),
        jnp.stack([jnp.stack([rows4(z) for z in row], axis=1) for row in dw["ffn_w_out"]], axis=1),
        dw["a_w_qkv"][:, None],
        rows4(dw["a_w_o"])[:, None],
        kv,
        rows4(dw["b_w_q"])[:, None],
        rows4(dw["b_w_o"])[:, None],
    ]


def kernel(x, positions, ffn_norm, ffn_w_in, ffn_w_out, mix_norm, a_w_qkv, a_q_norm, a_k_norm, a_w_o, kv_norm, kv_w, kv_b_f, kv_k_norm, b_w_q, b_q_norm, b_w_o, loss_target, m_ffn_norm, m_ffn_w_in, m_ffn_w_out, m_mix_norm, m_a_w_qkv, m_a_q_norm, m_a_k_norm, m_a_w_o, m_kv_norm, m_kv_w, m_kv_b_f, m_kv_k_norm, m_b_w_q, m_b_q_norm, m_b_w_o, v_ffn_norm, v_ffn_w_in, v_ffn_w_out, v_mix_norm, v_a_w_qkv, v_a_q_norm, v_a_k_norm, v_a_w_o, v_kv_norm, v_kv_w, v_kv_b_f, v_kv_k_norm, v_b_w_q, v_b_q_norm, v_b_w_o):
    wts = dict(zip(WEIGHTS, (ffn_norm, ffn_w_in, ffn_w_out, mix_norm, a_w_qkv, a_q_norm, a_k_norm, a_w_o, kv_norm,
                             kv_w, kv_b_f, kv_k_norm, b_w_q, b_q_norm, b_w_o)))
    mom = dict(zip(WEIGHTS, (m_ffn_norm, m_ffn_w_in, m_ffn_w_out, m_mix_norm, m_a_w_qkv, m_a_q_norm, m_a_k_norm,
                             m_a_w_o, m_kv_norm, m_kv_w, m_kv_b_f, m_kv_k_norm, m_b_w_q, m_b_q_norm, m_b_w_o)))
    var = dict(zip(WEIGHTS, (v_ffn_norm, v_ffn_w_in, v_ffn_w_out, v_mix_norm, v_a_w_qkv, v_a_q_norm, v_a_k_norm,
                             v_a_w_o, v_kv_norm, v_kv_w, v_kv_b_f, v_kv_k_norm, v_b_w_q, v_b_q_norm, v_b_w_o)))
    batch, seq, d = x.shape
    cfg = Cfg(d_model=d, d_ff=ffn_w_out.shape[2] * N_CHIPS, seq=seq, batch=batch)
    chip = 2 * lax.axis_index("x") + lax.axis_index("y")
    big_shapes = [wts[n].shape for n in BIG]

    shard = _pack([wts[n].astype(BF16) for n in BIG], 0, PACK_COLS, PACK_ROW_ALIGN, PACK_COLS)
    gathered = _all_gather_chips(shard, name="gather_weights")
    gathered = lax.dynamic_update_slice_in_dim(gathered, shard[None], chip, axis=0)
    w = _stack_weights(dict(zip(BIG, _unpack(gathered, big_shapes, 1, PACK_COLS, PACK_ROW_ALIGN))), d)
    norm_shard = _pack([ffn_norm], 0, LANES, SUBLANES, SUBLANES)
    norms = _all_gather_small(norm_shard, name="gather_ffn_norm")[0::2]
    (norms,) = _unpack(norms, [ffn_norm.shape], 1, LANES, SUBLANES)
    small = {"ffn_norm": jnp.moveaxis(norms, 0, 2).reshape(ffn_norm.shape[:2] + (d,)),
             "mix_norm": mix_norm, "a_q_norm": a_q_norm[0], "a_k_norm": a_k_norm[0], "kv_norm": kv_norm,
             "kv_b_f": kv_b_f, "kv_k_norm": kv_k_norm, "b_q_norm": b_q_norm}

    loss, dx, dw, ds = _local_step(cfg, x.reshape(cfg.tokens, d), positions.reshape(cfg.tokens),
                                   loss_target.reshape(cfg.tokens, d), w, small)
    loss = lax.psum(loss, ("x", "y", "c"))

    g = _pack(_unstack_grads(dw, d, cfg.heads), 1, PACK_COLS, PACK_ROW_ALIGN, PACK_COLS)
    chip_half = _add_own_half(g, _swap_halves(g, name="swap_halves"), name="add_halves")
    parts = _scatter_chips(chip_half, name="scatter_chips")
    parts = lax.dynamic_update_slice_in_dim(parts, lax.dynamic_slice_in_dim(chip_half, chip, 1, axis=0), chip, axis=0)
    mine = _sum_parts(parts, name="sum_chips")
    both = _join_halves(mine, name="join_halves")
    g_big = lax.dynamic_update_slice_in_dim(both, mine[None], lax.axis_index("c"), axis=0).reshape(g.shape[1:])
    grads = dict(zip(BIG, _unpack(g_big, big_shapes, 0, PACK_COLS, PACK_ROW_ALIGN)))

    small_shapes = [ds[n].shape for n in SMALL]
    parts = _all_gather_small(_pack([ds[n] for n in SMALL], 0, LANES, SUBLANES, SUBLANES), name="gather_small")
    g_small = dict(zip(SMALL, _unpack(_sum_parts(parts, name="sum_small"), small_shapes, 0, LANES, SUBLANES)))
    quarter = d // N_CHIPS
    g_small["ffn_norm"] = lax.dynamic_slice_in_dim(g_small["ffn_norm"], chip * quarter, quarter, axis=2)
    grads.update(g_small)

    delta, new_m, new_v = {}, {}, {}
    for n in BIG:
        delta[n], new_m[n], new_v[n] = _adamw(wts[n], mom[n], var[n], grads[n], name=f"adamw_{n}")
    packed = [_pack([z[n] for n in SMALL], 0, LANES, SUBLANES, SUBLANES) for z in (wts, mom, var, grads)]
    small_out = _adamw(*packed, name="adamw_small")
    shard_shapes = [wts[n].shape for n in SMALL]
    for out, res in zip((delta, new_m, new_v), small_out):
        out.update(zip(SMALL, _unpack(res, shard_shapes, 0, LANES, SUBLANES)))

    return (loss, dx.reshape(x.shape), *[grads[n] for n in WEIGHTS], *[delta[n] for n in WEIGHTS],
            *[new_m[n] for n in WEIGHTS], *[new_v[n] for n in WEIGHTS])
```

```python
import functools
from typing import NamedTuple

import jax
import jax.numpy as jnp
from jax import lax
from jax.experimental import pallas as pl
from jax.experimental.pallas import tpu as pltpu

F32 = jnp.float32
BF16 = jnp.bfloat16

HEAD_DIM = 64
LANES = 128
SUBLANES = 8
ROT_DIM = HEAD_DIM // 4
ROPE_THETA = 500000.0
NORM_EPS = 1e-6
BAND = 128
DILATIONS = (1, 4, 16)
NEG = -1e30
Q_SCALE = HEAD_DIM ** -0.5
N_CHIPS = 4
N_DEV = 8
VMEM_LIMIT = 48 * 1024 * 1024

ADAM_LR = 0.001
ADAM_B1 = 0.9
ADAM_B2 = 0.999
ADAM_EPS = 1e-08
ADAM_WD = 0.01
ADAM_STEP = 10


class Cfg(NamedTuple):
    d_model: int
    d_ff: int
    seq: int
    batch: int

    @property
    def heads(self):
        return self.d_model // HEAD_DIM

    @property
    def tokens(self):
        return self.batch * self.seq

    @property
    def pairs(self):
        return self.d_model // LANES


def _params(sem):
    return pltpu.CompilerParams(dimension_semantics=sem, vmem_limit_bytes=VMEM_LIMIT)


def _blk(dim, want):
    if dim <= want:
        return dim
    for b in range(want // LANES * LANES, 0, -LANES):
        if dim % b == 0:
            return b
    b = want
    while dim % b:
        b //= 2
    return b


def _mm(a, b, *, form, out_dtype, name, bm=1024, bn=1024, bk=1024, res=None, scale=1.0):
    if form == "F":
        m, kdim = a.shape
        jn, _, ns = b.shape
        bm, bn, bk = _blk(m, bm), _blk(ns, bn), _blk(kdim, bk)
        npj = ns // bn
        grid = (m // bm, jn * npj, kdim // bk)
        a_spec = pl.BlockSpec((bm, bk), lambda i, n, k: (i, k))
        b_spec = pl.BlockSpec((None, bk, bn), lambda i, n, k: (n // npj, k, n % npj))
        o_spec = pl.BlockSpec((bm, bn), lambda i, n, k: (i, n))
        o_shape = jax.ShapeDtypeStruct((m, jn * ns), out_dtype)
        dims = (((1,), (0,)), ((), ()))
    elif form == "B":
        m = a.shape[0]
        jn, kdim, ns = b.shape
        bm, bn, bk = _blk(m, bm), _blk(kdim, bn), _blk(ns, bk)
        kpj = ns // bk
        grid = (m // bm, kdim // bn, jn * kpj)
        a_spec = pl.BlockSpec((bm, bk), lambda i, n, k: (i, k))
        b_spec = pl.BlockSpec((None, bn, bk), lambda i, n, k: (k // kpj, n, k % kpj))
        o_spec = pl.BlockSpec((bm, bn), lambda i, n, k: (i, n))
        o_shape = jax.ShapeDtypeStruct((m, kdim), out_dtype)
        dims = (((1,), (1,)), ((), ()))
    else:
        raise ValueError(form)
    nk = grid[2]

    def body(*refs):
        if res is None:
            a_ref, b_ref, o_ref, acc_ref = refs
            r_ref = None
        else:
            a_ref, b_ref, r_ref, o_ref, acc_ref = refs
        k = pl.program_id(2)

        @pl.when(k == 0)
        def _():
            acc_ref[...] = jnp.zeros_like(acc_ref)

        acc_ref[...] += lax.dot_general(a_ref[...].astype(BF16), b_ref[...].astype(BF16), dims,
                                        preferred_element_type=F32)

        @pl.when(k == nk - 1)
        def _():
            r = acc_ref[...]
            if scale != 1.0:
                r = r * scale
            if r_ref is not None:
                r = r_ref[...] + r
            o_ref[...] = r.astype(o_ref.dtype)

    in_specs = [a_spec, b_spec]
    args = [a, b]
    if res is not None:
        in_specs.append(pl.BlockSpec((bm, bn), lambda i, n, k: (i, n)))
        args.append(res)
    return pl.pallas_call(
        body, name=name, grid=grid, in_specs=in_specs, out_specs=o_spec, out_shape=o_shape,
        scratch_shapes=[pltpu.VMEM((bm, bn), F32)],
        compiler_params=_params(("parallel", "parallel", "arbitrary")),
    )(*args)


def _mm_grad(a, dy, jn, *, name, scale=1.0, bm=1024, bn=1024, bk=1024):
    halves = dy if isinstance(dy, (tuple, list)) else (dy,)
    t, kdim = a.shape
    ns = len(halves) * halves[0].shape[1] // jn
    bm, bn, bk = _blk(kdim, bm), _blk(ns, bn), _blk(t, bk)
    npj = ns // bn
    grid = (kdim // bm, jn * npj, t // bk)
    nk = grid[2]
    nhalf = jn * npj // len(halves)
    dims = (((0,), (0,)), ((), ()))

    def body(a_ref, *refs):
        b_refs, o_ref, acc_ref = refs[:len(halves)], refs[-2], refs[-1]
        n, k = pl.program_id(1), pl.program_id(2)

        @pl.when(k == 0)
        def _():
            acc_ref[...] = jnp.zeros_like(acc_ref)

        for which, b_ref in enumerate(b_refs):
            @pl.when(n // nhalf == which)
            def _(b_ref=b_ref):
                acc_ref[...] += lax.dot_general(a_ref[...].astype(BF16), b_ref[...].astype(BF16), dims,
                                                preferred_element_type=F32)

        @pl.when(k == nk - 1)
        def _():
            r = acc_ref[...]
            if scale != 1.0:
                r = r * scale
            o_ref[...] = r

    def half_spec(which):
        return pl.BlockSpec((bk, bn), lambda m, n, k: (jnp.where(n // nhalf == which, k, 0),
                                                        jnp.where(n // nhalf == which, n % nhalf, 0)))

    return pl.pallas_call(
        body, name=name, grid=grid,
        in_specs=[pl.BlockSpec((bk, bm), lambda m, n, k: (k, m))] + [half_spec(w) for w in range(len(halves))],
        out_specs=pl.BlockSpec((None, bm, bn), lambda m, n, k: (n // npj, m, n % npj)),
        out_shape=jax.ShapeDtypeStruct((jn, kdim, ns), F32),
        scratch_shapes=[pltpu.VMEM((bm, bn), F32)],
        compiler_params=_params(("parallel", "parallel", "arbitrary")),
    )(a, *halves)


def _mm_back2(dy_halves, w, *, name, bm=1024, bn=1024, bk=1024):
    m = dy_halves[0].shape[0]
    jn, kdim, ns = w.shape
    bm, bn, bk = _blk(m, bm), _blk(kdim, bn), _blk(ns, bk)
    kpj = ns // bk
    nk = jn * kpj
    khalf = nk // 2

    def body(a0_ref, a1_ref, b_ref, o_ref, acc_ref):
        k = pl.program_id(2)

        @pl.when(k == 0)
        def _():
            acc_ref[...] = jnp.zeros_like(acc_ref)

        for which, a_ref in enumerate((a0_ref, a1_ref)):
            @pl.when(k // khalf == which)
            def _(a_ref=a_ref):
                acc_ref[...] += lax.dot_general(a_ref[...], b_ref[...], _NT, preferred_element_type=F32)

        @pl.when(k == nk - 1)
        def _():
            o_ref[...] = acc_ref[...]

    def half_spec(which):
        return pl.BlockSpec((bm, bk), lambda i, n, k: (i, jnp.clip(k - which * khalf, 0, khalf - 1)))

    return pl.pallas_call(
        body, name=name, grid=(m // bm, kdim // bn, nk),
        in_specs=[half_spec(0), half_spec(1),
                  pl.BlockSpec((None, bn, bk), lambda i, n, k: (k // kpj, n, k % kpj))],
        out_specs=pl.BlockSpec((bm, bn), lambda i, n, k: (i, n)),
        out_shape=jax.ShapeDtypeStruct((m, kdim), F32),
        scratch_shapes=[pltpu.VMEM((bm, bn), F32)],
        compiler_params=_params(("parallel", "parallel", "arbitrary")),
    )(dy_halves[0], dy_halves[1], w)


def _ffn_in_act(n, w_in, *, name, bm=512):
    m, kdim = n.shape
    jn, _, ns = w_in.shape
    f = jn * ns // 2
    bm = _blk(m, bm)
    bn = _blk(ns, WIDE)
    npj = ns // bn
    nf = f // bn

    def body(n_ref, wg_ref, wu_ref, g_ref, u_ref, a_ref):
        nv = n_ref[...]
        g = jnp.dot(nv, wg_ref[...], preferred_element_type=F32)
        u = jnp.dot(nv, wu_ref[...], preferred_element_type=F32)
        g_ref[...] = g.astype(BF16)
        u_ref[...] = u.astype(BF16)
        a_ref[...] = (g * jax.nn.sigmoid(g) * u).astype(BF16)

    out = jax.ShapeDtypeStruct((m, f), BF16)
    ospec = pl.BlockSpec((bm, bn), lambda i, c: (i, c))
    return pl.pallas_call(
        body, name=name, grid=(m // bm, nf),
        in_specs=[pl.BlockSpec((bm, kdim), lambda i, c: (i, 0)),
                  pl.BlockSpec((None, kdim, bn), lambda i, c: (c // npj, 0, c % npj)),
                  pl.BlockSpec((None, kdim, bn), lambda i, c: ((c + nf) // npj, 0, (c + nf) % npj))],
        out_specs=[ospec, ospec, ospec], out_shape=[out, out, out],
        compiler_params=_params(("parallel", "parallel")),
    )(n, w_in, w_in)


def _ffn_out_dx_act(dh, w_out, gate, up, *, name, scale, bm=512):
    m, d = dh.shape
    f = w_out.shape[1]
    bm = _blk(m, bm)
    bn = _blk(f, WIDE)

    def body(dh_ref, w_ref, g_ref, u_ref, dg_ref, du_ref):
        da = lax.dot_general(dh_ref[...].astype(BF16), w_ref[...], _NT, preferred_element_type=F32) * scale
        g = g_ref[...].astype(F32)
        sg = jax.nn.sigmoid(g)
        silu = g * sg
        dg_ref[...] = (da * u_ref[...].astype(F32) * (sg + silu * (1.0 - sg))).astype(BF16)
        du_ref[...] = (da * silu).astype(BF16)

    out = jax.ShapeDtypeStruct((m, f), BF16)
    spec = pl.BlockSpec((bm, bn), lambda i, c: (i, c))
    return pl.pallas_call(
        body, name=name, grid=(m // bm, f // bn),
        in_specs=[pl.BlockSpec((bm, d), lambda i, c: (i, 0)), pl.BlockSpec((None, bn, d), lambda i, c: (0, c, 0)),
                  spec, spec],
        out_specs=[spec, spec], out_shape=[out, out],
        compiler_params=_params(("parallel", "parallel")),
    )(dh, w_out, gate, up)


ROW_BLOCK = 512


def _fold8(x):
    return jnp.sum(x.reshape(x.shape[0] // SUBLANES, SUBLANES, x.shape[1]), axis=0)


def _rms_fwd(x, g, *, name):
    t, d = x.shape
    tr = _blk(t, ROW_BLOCK)

    def body(x_ref, g_ref, o_ref):
        xv = x_ref[...]
        rstd = lax.rsqrt(jnp.mean(xv * xv, axis=-1, keepdims=True) + NORM_EPS)
        o_ref[...] = ((xv * rstd) * g_ref[...]).astype(BF16)

    return pl.pallas_call(
        body, name=name, grid=(t // tr,),
        in_specs=[pl.BlockSpec((tr, d), lambda i: (i, 0)), pl.BlockSpec((1, d), lambda i: (0, 0))],
        out_specs=pl.BlockSpec((tr, d), lambda i: (i, 0)),
        out_shape=jax.ShapeDtypeStruct((t, d), BF16),
        compiler_params=_params(("parallel",)),
    )(x, g.reshape(1, d))


def _rms_bwd(x, g, dy, dres, *, name):
    t, d = x.shape
    tr = _blk(t, ROW_BLOCK)

    def body(x_ref, g_ref, dy_ref, dres_ref, dx_ref, dg_ref):
        i = pl.program_id(0)
        xv = x_ref[...]
        rstd = lax.rsqrt(jnp.mean(xv * xv, axis=-1, keepdims=True) + NORM_EPS)
        xhat = xv * rstd
        dyv = dy_ref[...]
        dyg = dyv * g_ref[...]
        proj = jnp.mean(dyg * xhat, axis=-1, keepdims=True)
        dx_ref[...] = dres_ref[...] + rstd * (dyg - xhat * proj)

        @pl.when(i == 0)
        def _():
            dg_ref[...] = jnp.zeros_like(dg_ref)

        dg_ref[...] += _fold8(dyv * xhat)

    dx, dg = pl.pallas_call(
        body, name=name, grid=(t // tr,),
        in_specs=[pl.BlockSpec((tr, d), lambda i: (i, 0)), pl.BlockSpec((1, d), lambda i: (0, 0)),
                  pl.BlockSpec((tr, d), lambda i: (i, 0)), pl.BlockSpec((tr, d), lambda i: (i, 0))],
        out_specs=[pl.BlockSpec((tr, d), lambda i: (i, 0)), pl.BlockSpec((SUBLANES, d), lambda i: (0, 0))],
        out_shape=[jax.ShapeDtypeStruct((t, d), F32), jax.ShapeDtypeStruct((SUBLANES, d), F32)],
        compiler_params=_params(("arbitrary",)),
    )(x, g.reshape(1, d), dy, dres)
    return dx, jnp.sum(dg, axis=0)


def _swiglu_fwd(u, *, name):
    t, f2 = u.shape
    f = f2 // 2
    tr = _blk(t, 256)

    def body(g_ref, u_ref, o_ref):
        gv = g_ref[...]
        o_ref[...] = (gv * jax.nn.sigmoid(gv) * u_ref[...]).astype(BF16)

    return pl.pallas_call(
        body, name=name, grid=(t // tr,),
        in_specs=[pl.BlockSpec((tr, f), lambda i: (i, 0)), pl.BlockSpec((tr, f), lambda i: (i, 1))],
        out_specs=pl.BlockSpec((tr, f), lambda i: (i, 0)),
        out_shape=jax.ShapeDtypeStruct((t, f), BF16),
        compiler_params=_params(("parallel",)),
    )(u, u)


def _swiglu_bwd(u, da, *, name):
    t, f2 = u.shape
    f = f2 // 2
    tr = _blk(t, 256)

    def body(g_ref, u_ref, da_ref, o_ref):
        gv = g_ref[...]
        sg = jax.nn.sigmoid(gv)
        silu = gv * sg
        dav = da_ref[...]
        o_ref[:, :f] = (dav * u_ref[...] * (sg + silu * (1.0 - sg))).astype(BF16)
        o_ref[:, f:] = (dav * silu).astype(BF16)

    return pl.pallas_call(
        body, name=name, grid=(t // tr,),
        in_specs=[pl.BlockSpec((tr, f), lambda i: (i, 0)), pl.BlockSpec((tr, f), lambda i: (i, 1)),
                  pl.BlockSpec((tr, f), lambda i: (i, 0))],
        out_specs=pl.BlockSpec((tr, f2), lambda i: (i, 0)),
        out_shape=jax.ShapeDtypeStruct((t, f2), BF16),
        compiler_params=_params(("parallel",)),
    )(u, u, da)


def _loss_fwd_bwd(h, target, *, name):
    t, d = h.shape
    tr = _blk(t, ROW_BLOCK)

    def body(h_ref, t_ref, dh_ref, l_ref):
        i = pl.program_id(0)
        err = h_ref[...] - t_ref[...]
        dh_ref[...] = err * (1.0 / d)

        @pl.when(i == 0)
        def _():
            l_ref[...] = jnp.zeros_like(l_ref)

        l_ref[...] += _fold8(err * err)

    dh, part = pl.pallas_call(
        body, name=name, grid=(t // tr,),
        in_specs=[pl.BlockSpec((tr, d), lambda i: (i, 0)), pl.BlockSpec((tr, d), lambda i: (i, 0))],
        out_specs=[pl.BlockSpec((tr, d), lambda i: (i, 0)), pl.BlockSpec((SUBLANES, d), lambda i: (0, 0))],
        out_shape=[jax.ShapeDtypeStruct((t, d), F32), jax.ShapeDtypeStruct((SUBLANES, d), F32)],
        compiler_params=_params(("arbitrary",)),
    )(h, target)
    return jnp.sum(part) * (0.5 / d), dh


def _seg_matrix():
    r = lax.broadcasted_iota(jnp.int32, (LANES, LANES), 0) // HEAD_DIM
    c = lax.broadcasted_iota(jnp.int32, (LANES, LANES), 1) // HEAD_DIM
    return (r == c).astype(BF16)


def _head_sum(x, seg):
    hi = x.astype(BF16)
    r1 = x - hi.astype(F32)
    mid = r1.astype(BF16)
    lo = (r1 - mid.astype(F32)).astype(BF16)
    dot = functools.partial(jnp.dot, preferred_element_type=F32)
    return dot(hi, seg) + dot(mid, seg) + dot(lo, seg)


def _lane_in_head(shape):
    return lax.broadcasted_iota(jnp.int32, shape, 1) % HEAD_DIM


def _rot_partner(x):
    up = pltpu.roll(x, LANES - ROT_DIM // 2, 1)
    down = pltpu.roll(x, ROT_DIM // 2, 1)
    return jnp.where(_lane_in_head(x.shape) < ROT_DIM // 2, up, down)


def _rope_tables(positions):
    inv_freq = ROPE_THETA ** (-jnp.arange(0, ROT_DIM, 2, dtype=F32) / ROT_DIM)
    ang = positions.astype(F32)[:, None] * inv_freq
    t = ang.shape[0]
    rest = HEAD_DIM - ROT_DIM
    cos = jnp.concatenate([jnp.cos(ang), jnp.cos(ang), jnp.ones((t, rest), F32)], axis=1)
    sin = jnp.concatenate([-jnp.sin(ang), jnp.sin(ang), jnp.zeros((t, rest), F32)], axis=1)
    return jnp.tile(cos, (1, LANES // HEAD_DIM)), jnp.tile(sin, (1, LANES // HEAD_DIM))


def _kind_is(j, kinds, kind):
    hits = [j == jj for jj, k in enumerate(kinds) if k == kind]
    return functools.reduce(jnp.logical_or, hits) if hits else None


def _hn_fwd(x, gains, kinds, d, cos, sin, *, name, col0=0):
    t = x.shape[0]
    n = len(kinds)
    tr = _blk(t, ROW_BLOCK)
    seg = _seg_matrix()
    g8 = jnp.repeat(gains.astype(F32), SUBLANES, axis=0)

    def body(x_ref, g_ref, seg_ref, cos_ref, sin_ref, o_ref):
        j = pl.program_id(1)

        def normed(rope):
            for c in range(d // LANES):
                sl = slice(c * LANES, (c + 1) * LANES)
                xv = x_ref[:, sl]
                ms = _head_sum(xv * xv, seg_ref[...]) * (1.0 / HEAD_DIM)
                y = (xv * lax.rsqrt(ms + NORM_EPS)) * g_ref[0:1, sl]
                if rope:
                    y = y * cos_ref[...] + _rot_partner(y) * sin_ref[...]
                o_ref[:, sl] = y.astype(BF16)

        for kind in ("rope", "norm"):
            hit = _kind_is(j, kinds, kind)
            if hit is not None:
                pl.when(hit)(functools.partial(normed, kind == "rope"))
        hit = _kind_is(j, kinds, "cast")
        if hit is not None:
            @pl.when(hit)
            def _():
                o_ref[...] = x_ref[...].astype(BF16)

    return pl.pallas_call(
        body, name=name, grid=(t // tr, n),
        in_specs=[pl.BlockSpec((tr, d), lambda i, j: (i, col0 + j)), pl.BlockSpec((SUBLANES, d), lambda i, j: (j, 0)),
                  pl.BlockSpec((LANES, LANES), lambda i, j: (0, 0)),
                  pl.BlockSpec((tr, LANES), lambda i, j: (i, 0)), pl.BlockSpec((tr, LANES), lambda i, j: (i, 0))],
        out_specs=pl.BlockSpec((tr, d), lambda i, j: (i, j)),
        out_shape=jax.ShapeDtypeStruct((t, n * d), BF16),
        compiler_params=_params(("parallel", "parallel")),
    )(x, g8, seg, cos, sin)


def _hn_bwd(x, dys, gains, kinds, d, cos, sin, *, name, col0=0):
    t = x.shape[0]
    n = len(kinds)
    tr = _blk(t, ROW_BLOCK // 2)
    seg = _seg_matrix()
    g8 = jnp.repeat(gains.astype(F32), SUBLANES, axis=0)

    def body(x_ref, *refs):
        dy_refs = refs[:n]
        g_ref, seg_ref, cos_ref, sin_ref, dx_ref, dg_ref = refs[n:]
        j = pl.program_id(0)
        i = pl.program_id(1)

        @pl.when(i == 0)
        def _():
            dg_ref[...] = jnp.zeros_like(dg_ref)

        def normed(rope, dy_ref):
            for c in range(d // LANES):
                sl = slice(c * LANES, (c + 1) * LANES)
                xv = x_ref[:, sl]
                dyv = dy_ref[:, sl]
                if rope:
                    dyv = dyv * cos_ref[...] - _rot_partner(dyv) * sin_ref[...]
                ms = _head_sum(xv * xv, seg_ref[...]) * (1.0 / HEAD_DIM)
                rstd = lax.rsqrt(ms + NORM_EPS)
                xhat = xv * rstd
                dg_ref[:, sl] += _fold8(dyv * xhat)
                dyg = dyv * g_ref[0:1, sl]
                proj = _head_sum(dyg * xhat, seg_ref[...]) * (1.0 / HEAD_DIM)
                dx_ref[:, sl] = (rstd * (dyg - xhat * proj)).astype(BF16)

        def cast(dy_ref):
            dx_ref[...] = dy_ref[...].astype(BF16)

        for jj, kind in enumerate(kinds):
            if kind == "cast":
                pl.when(j == jj)(functools.partial(cast, dy_refs[jj]))
            else:
                pl.when(j == jj)(functools.partial(normed, kind == "rope", dy_refs[jj]))

    def dy_spec(jj):
        return pl.BlockSpec((tr, d), lambda j, i: (jnp.where(j == jj, i, 0), 0))

    dx, dg = pl.pallas_call(
        body, name=name, grid=(n, t // tr),
        in_specs=[pl.BlockSpec((tr, d), lambda j, i: (i, col0 + j))] + [dy_spec(jj) for jj in range(n)] + [
                  pl.BlockSpec((SUBLANES, d), lambda j, i: (j, 0)),
                  pl.BlockSpec((LANES, LANES), lambda j, i: (0, 0)),
                  pl.BlockSpec((tr, LANES), lambda j, i: (i, 0)), pl.BlockSpec((tr, LANES), lambda j, i: (i, 0))],
        out_specs=[pl.BlockSpec((tr, d), lambda j, i: (i, j)), pl.BlockSpec((SUBLANES, d), lambda j, i: (j, 0))],
        out_shape=[jax.ShapeDtypeStruct((t, n * d), BF16), jax.ShapeDtypeStruct((n * SUBLANES, d), F32)],
        compiler_params=_params(("arbitrary", "arbitrary")),
    )(x, *dys, g8, seg, cos, sin)
    dg = dg.reshape(n, SUBLANES, d // HEAD_DIM, HEAD_DIM).sum(axis=(1, 2))
    return dx, dg


def _head_dot(a, b, *, name):
    t, d = a.shape
    tr = _blk(t, ROW_BLOCK)
    seg = _seg_matrix()

    def body(a_ref, b_ref, seg_ref, o_ref):
        for c in range(d // LANES):
            sl = slice(c * LANES, (c + 1) * LANES)
            o_ref[:, sl] = _head_sum(a_ref[:, sl].astype(BF16).astype(F32) * b_ref[:, sl], seg_ref[...])

    return pl.pallas_call(
        body, name=name, grid=(t // tr,),
        in_specs=[pl.BlockSpec((tr, d), lambda i: (i, 0)), pl.BlockSpec((tr, d), lambda i: (i, 0)),
                  pl.BlockSpec((LANES, LANES), lambda i: (0, 0))],
        out_specs=pl.BlockSpec((tr, d), lambda i: (i, 0)),
        out_shape=jax.ShapeDtypeStruct((t, d), F32),
        compiler_params=_params(("parallel",)),
    )(a, b, seg)


def _half_mask(shape):
    return lax.broadcasted_iota(jnp.int32, shape, 1) < HEAD_DIM


def _band_valid(first):
    qi = lax.broadcasted_iota(jnp.int32, (BAND, 2 * BAND), 0)
    kj = lax.broadcasted_iota(jnp.int32, (BAND, 2 * BAND), 1)
    dist = qi + BAND - kj
    return (dist >= 0) & (dist <= BAND) & ((kj >= BAND) | jnp.logical_not(first))


_NT = (((1,), (1,)), ((), ()))
_TN = (((0,), (0,)), ((), ()))


def _dot2(p, v):
    hi = p.astype(BF16)
    lo = (p - hi.astype(F32)).astype(BF16)
    return jnp.dot(hi, v, preferred_element_type=F32) + jnp.dot(lo, v, preferred_element_type=F32)


def _band_fwd(qkv, dil, cfg, *, name):
    t, d = cfg.tokens, cfg.d_model
    w = 3 * d
    rows = t // dil
    nbt = rows // BAND
    nb = cfg.seq // (dil * BAND)
    view = qkv.reshape(rows, dil * w)
    ncol = w // d

    def body(q_ref, kp_ref, kc_ref, vp_ref, vc_ref, o_ref, lse_ref):
        i = pl.program_id(1)
        valid = _band_valid(i % nb == 0)
        half = _half_mask((BAND, LANES))
        for hp in range(d // LANES):
            sl = slice(hp * LANES, (hp + 1) * LANES)
            q2 = q_ref[:, sl]
            kk = jnp.concatenate([kp_ref[:, sl], kc_ref[:, sl]], axis=0)
            vv = jnp.concatenate([vp_ref[:, sl], vc_ref[:, sl]], axis=0)
            outs, lses = [], []
            for e in range(2):
                qe = jnp.where(half == (e == 0), q2, jnp.zeros_like(q2))
                s = lax.dot_general(qe, kk, _NT, preferred_element_type=F32)
                s = jnp.where(valid, s, NEG)
                m = jnp.max(s, axis=1, keepdims=True)
                p = jnp.exp(s - m)
                l = jnp.sum(p, axis=1, keepdims=True)
                outs.append(_dot2(p * (1.0 / l), vv))
                lses.append(m + jnp.log(l))
            o_ref[:, sl] = jnp.where(half, outs[0], outs[1])
            lse_ref[:, sl] = jnp.where(half, lses[0], lses[1])

    def col(which):
        return lambda r, i: (i, r * ncol + which)

    def col_prev(which):
        return lambda r, i: (jnp.maximum(i - 1, 0), r * ncol + which)

    blk = (BAND, d)
    o, lse = pl.pallas_call(
        body, name=name, grid=(dil, nbt),
        in_specs=[pl.BlockSpec(blk, col(0)), pl.BlockSpec(blk, col_prev(1)), pl.BlockSpec(blk, col(1)),
                  pl.BlockSpec(blk, col_prev(2)), pl.BlockSpec(blk, col(2))],
        out_specs=[pl.BlockSpec(blk, lambda r, i: (i, r)), pl.BlockSpec(blk, lambda r, i: (i, r))],
        out_shape=[jax.ShapeDtypeStruct((rows, dil * d), F32), jax.ShapeDtypeStruct((rows, dil * d), F32)],
        compiler_params=_params(("parallel", "arbitrary")),
    )(view, view, view, view, view)
    return o.reshape(t, d), lse.reshape(t, d)


def _band_bwd(qkv, dmixed, lse_all, dsum, dil, cfg, *, name):
    t, d = cfg.tokens, cfg.d_model
    w = 3 * d
    rows = t // dil
    nbt = rows // BAND
    nb = cfg.seq // (dil * BAND)
    view = qkv.reshape(rows, dil * w)
    ncol = w // d
    do_v, l_v, d_v = (z.reshape(rows, dil * d) for z in (dmixed, lse_all, dsum))

    def body(q_ref, kp_ref, kc_ref, vp_ref, vc_ref, do_ref, l_ref, ds_ref, dq_ref, dk_ref, dv_ref, ck_ref, cv_ref):
        i = pl.program_id(1)

        @pl.when(i < nbt)
        def _():
            valid = _band_valid(i % nb == 0)
            half = _half_mask((BAND, LANES))
            half2 = _half_mask((2 * BAND, LANES))
            for hp in range(d // LANES):
                sl = slice(hp * LANES, (hp + 1) * LANES)
                q2 = q_ref[:, sl]
                kk = jnp.concatenate([kp_ref[:, sl], kc_ref[:, sl]], axis=0)
                vv = jnp.concatenate([vp_ref[:, sl], vc_ref[:, sl]], axis=0)
                do2 = do_ref[:, sl].astype(BF16)
                dqs, dks, dvs = [], [], []
                for e in range(2):
                    lane0 = e * HEAD_DIM
                    keep = half == (e == 0)
                    qe = jnp.where(keep, q2, jnp.zeros_like(q2))
                    doe = jnp.where(keep, do2, jnp.zeros_like(do2))
                    s = lax.dot_general(qe, kk, _NT, preferred_element_type=F32)
                    s = jnp.where(valid, s, NEG)
                    p = jnp.exp(s - l_ref[:, hp * LANES + lane0:hp * LANES + lane0 + 1])
                    dp = lax.dot_general(doe, vv, _NT, preferred_element_type=F32)
                    dsc = (p * (dp - ds_ref[:, hp * LANES + lane0:hp * LANES + lane0 + 1])).astype(BF16)
                    dqs.append(jnp.dot(dsc, kk, preferred_element_type=F32))
                    dks.append(lax.dot_general(dsc, q2, _TN, preferred_element_type=F32))
                    dvs.append(lax.dot_general(p.astype(BF16), do2, _TN, preferred_element_type=F32))
                dq_ref[:, sl] = jnp.where(half, dqs[0], dqs[1])
                dkk = jnp.where(half2, dks[0], dks[1])
                dvv = jnp.where(half2, dvs[0], dvs[1])

                @pl.when(i > 0)
                def _():
                    dk_ref[:, sl] = ck_ref[:, sl] + dkk[:BAND]
                    dv_ref[:, sl] = cv_ref[:, sl] + dvv[:BAND]

                ck_ref[:, sl] = dkk[BAND:]
                cv_ref[:, sl] = dvv[BAND:]

        @pl.when(i == nbt)
        def _():
            dk_ref[...] = ck_ref[...]
            dv_ref[...] = cv_ref[...]

    def cur(i):
        return jnp.minimum(i, nbt - 1)

    def col(which):
        return lambda r, i: (cur(i), r * ncol + which)

    def col_prev(which):
        return lambda r, i: (jnp.maximum(cur(i) - 1, 0), r * ncol + which)

    blk = (BAND, d)
    here = pl.BlockSpec(blk, lambda r, i: (cur(i), r))
    behind = pl.BlockSpec(blk, lambda r, i: (jnp.maximum(i - 1, 0), r))
    shape = jax.ShapeDtypeStruct((rows, dil * d), F32)
    dq, dk, dv = pl.pallas_call(
        body, name=name, grid=(dil, nbt + 1),
        in_specs=[pl.BlockSpec(blk, col(0)), pl.BlockSpec(blk, col_prev(1)), pl.BlockSpec(blk, col(1)),
                  pl.BlockSpec(blk, col_prev(2)), pl.BlockSpec(blk, col(2)), here, here, here],
        out_specs=[here, behind, behind],
        out_shape=[shape, shape, shape],
        scratch_shapes=[pltpu.VMEM(blk, F32), pltpu.VMEM(blk, F32)],
        compiler_params=_params(("arbitrary", "arbitrary")),
    )(view, view, view, view, view, do_v, l_v, d_v)
    return dq.reshape(t, d), dk.reshape(t, d), dv.reshape(t, d)


def _band_valid_t(first):
    s = lax.broadcasted_iota(jnp.int32, (2 * BAND, BAND), 0)
    t = lax.broadcasted_iota(jnp.int32, (2 * BAND, BAND), 1)
    dist = t + BAND - s
    return (dist >= 0) & (dist <= BAND) & ((s >= BAND) | jnp.logical_not(first))


def _band_layouts(qkv, dil, cfg):
    rows = cfg.tokens // dil
    d = cfg.d_model
    return qkv.reshape(rows, dil * 3 * d), qkv.reshape(rows, dil, 3, d).transpose(1, 2, 3, 0)


def _to_classes_t(z, dil, width):
    return z.reshape(z.shape[0] // dil, dil, width).transpose(1, 2, 0)


def _from_classes_t(z):
    dil, width, rows = z.shape
    return z.transpose(2, 0, 1).reshape(rows * dil, width)


def _band_fwd_t(nat, tr, dil, cfg, *, name):
    d, hh = cfg.d_model, cfg.heads
    rows = cfg.tokens // dil
    nbt = rows // BAND
    nb = cfg.seq // (dil * BAND)

    def body(qt_ref, kp_ref, kc_ref, vtp_ref, vtc_ref, o_ref, lse_ref):
        i = pl.program_id(1)
        valid = _band_valid_t(i % nb == 0)
        upper = lax.broadcasted_iota(jnp.int32, (LANES, BAND), 0) < HEAD_DIM
        for hp in range(d // LANES):
            pair = slice(hp * LANES, (hp + 1) * LANES)
            qt2 = qt_ref[pair, :]
            kk = jnp.concatenate([kp_ref[:, pair], kc_ref[:, pair]], axis=0)
            for e in range(2):
                h = 2 * hp + e
                hrows = slice(h * HEAD_DIM, (h + 1) * HEAD_DIM)
                qte = jnp.where(upper == (e == 0), qt2, jnp.zeros_like(qt2))
                s = jnp.where(valid, jnp.dot(kk, qte, preferred_element_type=F32), NEG)
                m = jnp.max(s, axis=0, keepdims=True)
                p = jnp.exp(s - m)
                l = jnp.sum(p, axis=0, keepdims=True)
                hi = p.astype(BF16)
                lo = (p - hi.astype(F32)).astype(BF16)
                vvt = jnp.concatenate([vtp_ref[hrows, :], vtc_ref[hrows, :]], axis=1)
                o = jnp.dot(vvt, hi, preferred_element_type=F32) + jnp.dot(vvt, lo, preferred_element_type=F32)
                o_ref[hrows, :] = o * (1.0 / l)
                lse_ref[h:h + 1, :] = m + jnp.log(l)

    def prev(i):
        return jnp.maximum(i - 1, 0)

    tblk = (None, None, d, BAND)
    return pl.pallas_call(
        body, name=name, grid=(dil, nbt),
        in_specs=[pl.BlockSpec(tblk, lambda r, i: (r, 0, 0, i)),
                  pl.BlockSpec((BAND, d), lambda r, i: (prev(i), r * 3 + 1)),
                  pl.BlockSpec((BAND, d), lambda r, i: (i, r * 3 + 1)),
                  pl.BlockSpec(tblk, lambda r, i: (r, 2, 0, prev(i))),
                  pl.BlockSpec(tblk, lambda r, i: (r, 2, 0, i))],
        out_specs=[pl.BlockSpec((None, d, BAND), lambda r, i: (r, 0, i)),
                   pl.BlockSpec((None, hh, BAND), lambda r, i: (r, 0, i))],
        out_shape=[jax.ShapeDtypeStruct((dil, d, rows), F32), jax.ShapeDtypeStruct((dil, hh, rows), F32)],
        compiler_params=_params(("parallel", "arbitrary")),
    )(tr, nat, nat, tr, tr)


def _band_bwd_t(nat, tr, do_t, do_nat, lse_c, dsum_c, dil, cfg, *, name):
    d, hh = cfg.d_model, cfg.heads
    rows = cfg.tokens // dil
    nbt = rows // BAND
    nb = cfg.seq // (dil * BAND)

    def body(qt_ref, qn_ref, kp_ref, kc_ref, ktp_ref, ktc_ref, vp_ref, vc_ref, dot_ref, don_ref, l_ref, ds_ref,
             dq_ref, dk_ref, dv_ref, ck_ref, cv_ref):
        i = pl.program_id(1)

        @pl.when(i < nbt)
        def _():
            valid = _band_valid_t(i % nb == 0)
            upper = lax.broadcasted_iota(jnp.int32, (LANES, BAND), 0) < HEAD_DIM
            half2 = _half_mask((2 * BAND, LANES))
            for hp in range(d // LANES):
                pair = slice(hp * LANES, (hp + 1) * LANES)
                qt2, dot2 = qt_ref[pair, :], dot_ref[pair, :]
                qn2, don2 = qn_ref[:, pair], don_ref[:, pair]
                kk = jnp.concatenate([kp_ref[:, pair], kc_ref[:, pair]], axis=0)
                vv = jnp.concatenate([vp_ref[:, pair], vc_ref[:, pair]], axis=0)
                dks, dvs = [], []
                for e in range(2):
                    h = 2 * hp + e
                    hrows = slice(h * HEAD_DIM, (h + 1) * HEAD_DIM)
                    keep = upper == (e == 0)
                    qte = jnp.where(keep, qt2, jnp.zeros_like(qt2))
                    dote = jnp.where(keep, dot2, jnp.zeros_like(dot2))
                    s = jnp.where(valid, jnp.dot(kk, qte, preferred_element_type=F32), NEG)
                    p = jnp.exp(s - l_ref[h:h + 1, :])
                    dp = jnp.dot(vv, dote, preferred_element_type=F32)
                    dsb = (p * (dp - ds_ref[h:h + 1, :])).astype(BF16)
                    kkt = jnp.concatenate([ktp_ref[hrows, :], ktc_ref[hrows, :]], axis=1)
                    dq_ref[hrows, :] = jnp.dot(kkt, dsb, preferred_element_type=F32)
                    dks.append(jnp.dot(dsb, qn2, preferred_element_type=F32))
                    dvs.append(jnp.dot(p.astype(BF16), don2, preferred_element_type=F32))
                dkk = jnp.where(half2, dks[0], dks[1])
                dvv = jnp.where(half2, dvs[0], dvs[1])

                @pl.when(i > 0)
                def _():
                    dk_ref[:, pair] = ck_ref[:, pair] + dkk[:BAND]
                    dv_ref[:, pair] = cv_ref[:, pair] + dvv[:BAND]

                ck_ref[:, pair] = dkk[BAND:]
                cv_ref[:, pair] = dvv[BAND:]

        @pl.when(i == nbt)
        def _():
            dk_ref[...] = ck_ref[...]
            dv_ref[...] = cv_ref[...]

    def cur(i):
        return jnp.minimum(i, nbt - 1)

    def prev(i):
        return jnp.maximum(cur(i) - 1, 0)

    tblk = (None, None, d, BAND)
    cblk = (None, hh, BAND)
    blk = (BAND, d)
    behind = pl.BlockSpec(blk, lambda r, i: (jnp.maximum(i - 1, 0), r))
    shape = jax.ShapeDtypeStruct((rows, dil * d), F32)
    return pl.pallas_call(
        body, name=name, grid=(dil, nbt + 1),
        in_specs=[pl.BlockSpec(tblk, lambda r, i: (r, 0, 0, cur(i))),
                  pl.BlockSpec(blk, lambda r, i: (cur(i), r * 3)),
                  pl.BlockSpec(blk, lambda r, i: (prev(i), r * 3 + 1)),
                  pl.BlockSpec(blk, lambda r, i: (cur(i), r * 3 + 1)),
                  pl.BlockSpec(tblk, lambda r, i: (r, 1, 0, prev(i))),
                  pl.BlockSpec(tblk, lambda r, i: (r, 1, 0, cur(i))),
                  pl.BlockSpec(blk, lambda r, i: (prev(i), r * 3 + 2)),
                  pl.BlockSpec(blk, lambda r, i: (cur(i), r * 3 + 2)),
                  pl.BlockSpec((None, d, BAND), lambda r, i: (r, 0, cur(i))),
                  pl.BlockSpec(blk, lambda r, i: (cur(i), r)),
                  pl.BlockSpec(cblk, lambda r, i: (r, 0, cur(i))),
                  pl.BlockSpec(cblk, lambda r, i: (r, 0, cur(i)))],
        out_specs=[pl.BlockSpec((None, d, BAND), lambda r, i: (r, 0, cur(i))), behind, behind],
        out_shape=[jax.ShapeDtypeStruct((dil, d, rows), F32), shape, shape],
        scratch_shapes=[pltpu.VMEM(blk, F32), pltpu.VMEM(blk, F32)],
        compiler_params=_params(("arbitrary", "arbitrary")),
    )(tr, nat, nat, nat, tr, tr, nat, nat, do_t, do_nat, lse_c, dsum_c)


def _band_fwd_n(nat, dil, cfg, *, name):
    d, hh = cfg.d_model, cfg.heads
    rows = cfg.tokens // dil
    nbt = rows // BAND
    nb = cfg.seq // (dil * BAND)

    def body(q_ref, kp_ref, kc_ref, vp_ref, vc_ref, o_ref, lse_ref):
        i = pl.program_id(1)
        valid = _band_valid_t(i % nb == 0)
        upper = lax.broadcasted_iota(jnp.int32, (LANES, BAND), 0) < HEAD_DIM
        for hp in range(d // LANES):
            pair = slice(hp * LANES, (hp + 1) * LANES)
            qt2 = q_ref[:, pair].T
            kk = jnp.concatenate([kp_ref[:, pair], kc_ref[:, pair]], axis=0)
            vvt = jnp.concatenate([vp_ref[:, pair], vc_ref[:, pair]], axis=0).T
            outs = []
            for e in range(2):
                h = 2 * hp + e
                qte = jnp.where(upper == (e == 0), qt2, jnp.zeros_like(qt2))
                s = jnp.where(valid, jnp.dot(kk, qte, preferred_element_type=F32), NEG)
                m = jnp.max(s, axis=0, keepdims=True)
                p = jnp.exp(s - m)
                l = jnp.sum(p, axis=0, keepdims=True)
                hi = p.astype(BF16)
                lo = (p - hi.astype(F32)).astype(BF16)
                vt = vvt[e * HEAD_DIM:(e + 1) * HEAD_DIM]
                o = jnp.dot(vt, hi, preferred_element_type=F32) + jnp.dot(vt, lo, preferred_element_type=F32)
                outs.append(o * (1.0 / l))
                lse_ref[h:h + 1, :] = m + jnp.log(l)
            o_ref[:, pair] = jnp.concatenate(outs, axis=0).T

    def prev(i):
        return jnp.maximum(i - 1, 0)

    blk = (BAND, d)
    return pl.pallas_call(
        body, name=name, grid=(dil, nbt),
        in_specs=[pl.BlockSpec(blk, lambda r, i: (i, r * 3)),
                  pl.BlockSpec(blk, lambda r, i: (prev(i), r * 3 + 1)),
                  pl.BlockSpec(blk, lambda r, i: (i, r * 3 + 1)),
                  pl.BlockSpec(blk, lambda r, i: (prev(i), r * 3 + 2)),
                  pl.BlockSpec(blk, lambda r, i: (i, r * 3 + 2))],
        out_specs=[pl.BlockSpec(blk, lambda r, i: (i, r)),
                   pl.BlockSpec((None, hh, BAND), lambda r, i: (r, 0, i))],
        out_shape=[jax.ShapeDtypeStruct((rows, dil * d), F32), jax.ShapeDtypeStruct((dil, hh, rows), F32)],
        compiler_params=_params(("parallel", "arbitrary")),
    )(nat, nat, nat, nat, nat)


def _band_bwd_n(nat, do_nat, lse_c, dsum_c, dil, cfg, *, name):
    d, hh = cfg.d_model, cfg.heads
    rows = cfg.tokens // dil
    nbt = rows // BAND
    nb = cfg.seq // (dil * BAND)

    def body(q_ref, kp_ref, kc_ref, vp_ref, vc_ref, do_ref, l_ref, ds_ref, dq_ref, dk_ref, dv_ref, ck_ref, cv_ref):
        i = pl.program_id(1)

        @pl.when(i < nbt)
        def _():
            valid = _band_valid_t(i % nb == 0)
            upper = lax.broadcasted_iota(jnp.int32, (LANES, BAND), 0) < HEAD_DIM
            half2 = _half_mask((2 * BAND, LANES))
            for hp in range(d // LANES):
                pair = slice(hp * LANES, (hp + 1) * LANES)
                qn2, don2 = q_ref[:, pair], do_ref[:, pair]
                qt2, dot2 = qn2.T, don2.T
                kk = jnp.concatenate([kp_ref[:, pair], kc_ref[:, pair]], axis=0)
                vv = jnp.concatenate([vp_ref[:, pair], vc_ref[:, pair]], axis=0)
                kkt = kk.T
                dqs, dks, dvs = [], [], []
                for e in range(2):
                    h = 2 * hp + e
                    keep = upper == (e == 0)
                    qte = jnp.where(keep, qt2, jnp.zeros_like(qt2))
                    dote = jnp.where(keep, dot2, jnp.zeros_like(dot2))
                    s = jnp.where(valid, jnp.dot(kk, qte, preferred_element_type=F32), NEG)
                    p = jnp.exp(s - l_ref[h:h + 1, :])
                    dp = jnp.dot(vv, dote, preferred_element_type=F32)
                    dsb = (p * (dp - ds_ref[h:h + 1, :])).astype(BF16)
                    dqs.append(jnp.dot(kkt[e * HEAD_DIM:(e + 1) * HEAD_DIM], dsb, preferred_element_type=F32))
                    dks.append(jnp.dot(dsb, qn2, preferred_element_type=F32))
                    dvs.append(jnp.dot(p.astype(BF16), don2, preferred_element_type=F32))
                dq_ref[:, pair] = jnp.concatenate(dqs, axis=0).T
                dkk = jnp.where(half2, dks[0], dks[1])
                dvv = jnp.where(half2, dvs[0], dvs[1])

                @pl.when(i > 0)
                def _():
                    dk_ref[:, pair] = ck_ref[:, pair] + dkk[:BAND]
                    dv_ref[:, pair] = cv_ref[:, pair] + dvv[:BAND]

                ck_ref[:, pair] = dkk[BAND:]
                cv_ref[:, pair] = dvv[BAND:]

        @pl.when(i == nbt)
        def _():
            dk_ref[...] = ck_ref[...]
            dv_ref[...] = cv_ref[...]

    def cur(i):
        return jnp.minimum(i, nbt - 1)

    def prev(i):
        return jnp.maximum(cur(i) - 1, 0)

    cblk = (None, hh, BAND)
    blk = (BAND, d)
    here = pl.BlockSpec(blk, lambda r, i: (cur(i), r))
    behind = pl.BlockSpec(blk, lambda r, i: (jnp.maximum(i - 1, 0), r))
    shape = jax.ShapeDtypeStruct((rows, dil * d), F32)
    return pl.pallas_call(
        body, name=name, grid=(dil, nbt + 1),
        in_specs=[pl.BlockSpec(blk, lambda r, i: (cur(i), r * 3)),
                  pl.BlockSpec(blk, lambda r, i: (prev(i), r * 3 + 1)),
                  pl.BlockSpec(blk, lambda r, i: (cur(i), r * 3 + 1)),
                  pl.BlockSpec(blk, lambda r, i: (prev(i), r * 3 + 2)),
                  pl.BlockSpec(blk, lambda r, i: (cur(i), r * 3 + 2)),
                  here,
                  pl.BlockSpec(cblk, lambda r, i: (r, 0, cur(i))),
                  pl.BlockSpec(cblk, lambda r, i: (r, 0, cur(i)))],
        out_specs=[here, behind, behind],
        out_shape=[shape, shape, shape],
        scratch_shapes=[pltpu.VMEM(blk, F32), pltpu.VMEM(blk, F32)],
        compiler_params=_params(("arbitrary", "arbitrary")),
    )(nat, nat, nat, nat, nat, do_nat, lse_c, dsum_c)


def _mix_fwd(outs, lses, *, name):
    t, d = outs[0].shape
    tr = _blk(t, ROW_BLOCK)
    ng = len(outs)

    def body(*refs):
        o_refs, l_refs = refs[:ng], refs[ng:2 * ng]
        mixed_ref, lse_ref = refs[2 * ng:]
        ls = [r[...] for r in l_refs]
        m = functools.reduce(jnp.maximum, ls)
        es = [jnp.exp(l - m) for l in ls]
        tot = functools.reduce(jnp.add, es)
        inv = 1.0 / tot
        mixed_ref[...] = functools.reduce(jnp.add, [(e * inv) * r[...] for e, r in zip(es, o_refs)])
        lse_ref[...] = m + jnp.log(tot)

    spec = pl.BlockSpec((tr, d), lambda i: (i, 0))
    return pl.pallas_call(
        body, name=name, grid=(t // tr,),
        in_specs=[spec] * (2 * ng), out_specs=[spec, spec],
        out_shape=[jax.ShapeDtypeStruct((t, d), F32), jax.ShapeDtypeStruct((t, d), F32)],
        compiler_params=_params(("parallel",)),
    )(*outs, *lses)


GATE_BLOCK = 256


def _tri(n, upper):
    r = lax.broadcasted_iota(jnp.int32, (n, n), 0)
    c = lax.broadcasted_iota(jnp.int32, (n, n), 1)
    return ((c >= r) if upper else (c <= r)).astype(BF16)


def _tri_dot(tri, x):
    hi = x.astype(BF16)
    r1 = x - hi.astype(F32)
    mid = r1.astype(BF16)
    lo = (r1 - mid.astype(F32)).astype(BF16)
    dot = functools.partial(jnp.dot, preferred_element_type=F32)
    return dot(tri, hi) + dot(tri, mid) + dot(tri, lo)


def _log_sigmoid(z):
    return jnp.minimum(z, 0.0) - jnp.log(1.0 + jnp.exp(-jnp.abs(z)))


def _gate_fwd(proj, col_block, bias, cfg, *, name):
    tr = _blk(cfg.seq, GATE_BLOCK)
    nblk = cfg.seq // tr

    def body(z_ref, b_ref, tri_ref, o_ref, carry_ref):
        i = pl.program_id(1)

        @pl.when(i == 0)
        def _():
            carry_ref[...] = jnp.zeros_like(carry_ref)

        logf = _log_sigmoid(z_ref[...] + b_ref[0:1, :])
        cum = _tri_dot(tri_ref[...], logf) + carry_ref[0:1, :]
        o_ref[...] = cum
        carry_ref[...] = jnp.broadcast_to(cum[tr - 1:tr, :], carry_ref.shape)

    return pl.pallas_call(
        body, name=name, grid=(cfg.batch, nblk),
        in_specs=[pl.BlockSpec((tr, LANES), lambda b, i: (b * nblk + i, col_block)),
                  pl.BlockSpec((SUBLANES, LANES), lambda b, i: (0, 0)),
                  pl.BlockSpec((tr, tr), lambda b, i: (0, 0))],
        out_specs=pl.BlockSpec((tr, LANES), lambda b, i: (b * nblk + i, 0)),
        out_shape=jax.ShapeDtypeStruct((cfg.tokens, LANES), F32),
        scratch_shapes=[pltpu.VMEM((SUBLANES, LANES), F32)],
        compiler_params=_params(("arbitrary", "arbitrary")),
    )(proj, jnp.broadcast_to(bias, (SUBLANES, LANES)), _tri(tr, upper=False))


def _gate_bwd(proj, col_block, bias, dcum, cfg, *, name):
    tr = _blk(cfg.seq, GATE_BLOCK)
    nblk = cfg.seq // tr

    def body(z_ref, b_ref, tri_ref, dc_ref, dz_ref, db_ref, carry_ref):
        b = pl.program_id(0)
        i = pl.program_id(1)

        @pl.when(i == 0)
        def _():
            carry_ref[...] = jnp.zeros_like(carry_ref)

        @pl.when((i == 0) & (b == 0))
        def _():
            db_ref[...] = jnp.zeros_like(db_ref)

        dcv = dc_ref[...]
        dlogf = _tri_dot(tri_ref[...], dcv) + carry_ref[0:1, :]
        carry_ref[...] = jnp.broadcast_to(dlogf[0:1, :], carry_ref.shape)
        dz = dlogf * jax.nn.sigmoid(-(z_ref[...] + b_ref[0:1, :]))
        dz_ref[...] = dz
        db_ref[...] += _fold8(dz)

    def rev(b, i):
        return (b * nblk + nblk - 1 - i, 0)

    dz, db = pl.pallas_call(
        body, name=name, grid=(cfg.batch, nblk),
        in_specs=[pl.BlockSpec((tr, LANES), lambda b, i: (b * nblk + nblk - 1 - i, col_block)),
                  pl.BlockSpec((SUBLANES, LANES), lambda b, i: (0, 0)),
                  pl.BlockSpec((tr, tr), lambda b, i: (0, 0)),
                  pl.BlockSpec((tr, LANES), rev)],
        out_specs=[pl.BlockSpec((tr, LANES), rev), pl.BlockSpec((SUBLANES, LANES), lambda b, i: (0, 0))],
        out_shape=[jax.ShapeDtypeStruct((cfg.tokens, LANES), F32), jax.ShapeDtypeStruct((SUBLANES, LANES), F32)],
        scratch_shapes=[pltpu.VMEM((SUBLANES, LANES), F32)],
        compiler_params=_params(("arbitrary", "arbitrary")),
    )(proj, jnp.broadcast_to(bias, (SUBLANES, LANES)), _tri(tr, upper=True), dcum)
    return dz, jnp.sum(db, axis=0)


FOX_BLOCK = 256


def _fox_scores(q2, k2, e, half, mask, cref, ck_row):
    qe = jnp.where(half == (e == 0), q2, jnp.zeros_like(q2))
    s = lax.dot_general(qe, k2, _NT, preferred_element_type=F32)
    return jnp.where(mask, s + (cref - ck_row), NEG)


def _causal(qi, ki, tq):
    r = lax.broadcasted_iota(jnp.int32, (tq, tq), 0) + qi * tq
    c = lax.broadcasted_iota(jnp.int32, (tq, tq), 1) + ki * tq
    return r >= c


def _fox_fwd(q, kv, cum_t, cfg, *, name):
    t, d, hrows = cfg.tokens, cfg.d_model, cum_t.shape[0]
    tq = _blk(cfg.seq, FOX_BLOCK)
    nq = cfg.seq // tq

    def body(q_ref, k_ref, v_ref, cq_ref, ck_ref, o_ref, lse_ref, m_ref, l_ref, acc_ref):
        qi, ki = pl.program_id(1), pl.program_id(2)

        @pl.when(ki == 0)
        def _():
            m_ref[...] = jnp.full_like(m_ref, NEG)
            l_ref[...] = jnp.zeros_like(l_ref)
            acc_ref[...] = jnp.zeros_like(acc_ref)

        @pl.when(ki <= qi)
        def _():
            mask = _causal(qi, ki, tq)
            half = _half_mask((tq, LANES))
            for hp in range(d // LANES):
                sl = slice(hp * LANES, (hp + 1) * LANES)
                q2, k2, v2 = q_ref[:, sl], k_ref[:, sl], v_ref[:, sl]
                alphas, pvs = [], []
                for e in range(2):
                    h = 2 * hp + e
                    s = _fox_scores(q2, k2, e, half, mask, cq_ref[h:h + 1, 0:1], ck_ref[h:h + 1, :])
                    m_prev = m_ref[h]
                    m_new = jnp.maximum(m_prev, jnp.max(s, axis=1, keepdims=True))
                    alpha = jnp.exp(m_prev - m_new)
                    p = jnp.exp(s - m_new[:, 0:1])
                    l_ref[h] = alpha * l_ref[h] + jnp.sum(p, axis=1, keepdims=True)
                    m_ref[h] = m_new
                    alphas.append(alpha)
                    pvs.append(_dot2(p, v2))
                acc = acc_ref[:, sl]
                acc_ref[:, sl] = jnp.where(half, alphas[0] * acc + pvs[0], alphas[1] * acc + pvs[1])

        @pl.when(ki == qi)
        def _():
            half = _half_mask((tq, LANES))
            for hp in range(d // LANES):
                sl = slice(hp * LANES, (hp + 1) * LANES)
                h0, h1 = 2 * hp, 2 * hp + 1
                inv = jnp.where(half, 1.0 / l_ref[h0], 1.0 / l_ref[h1])
                o_ref[:, sl] = acc_ref[:, sl] * inv
                lse0 = m_ref[h0] + jnp.log(l_ref[h0]) - cq_ref[h0:h0 + 1, 0:1]
                lse1 = m_ref[h1] + jnp.log(l_ref[h1]) - cq_ref[h1:h1 + 1, 0:1]
                lse_ref[:, sl] = jnp.where(half, lse0, lse1)

    def qrow(b, qi, ki):
        return (b * nq + qi, 0)

    def krow(b, qi, ki):
        return (b * nq + jnp.minimum(ki, qi), 0)

    o, lse = pl.pallas_call(
        body, name=name, grid=(cfg.batch, nq, nq),
        in_specs=[pl.BlockSpec((tq, d), qrow),
                  pl.BlockSpec((tq, d), krow),
                  pl.BlockSpec((tq, d), lambda b, qi, ki: (b * nq + jnp.minimum(ki, qi), 1)),
                  pl.BlockSpec((hrows, tq), lambda b, qi, ki: (0, b * nq + qi)),
                  pl.BlockSpec((hrows, tq), lambda b, qi, ki: (0, b * nq + jnp.minimum(ki, qi)))],
        out_specs=[pl.BlockSpec((tq, d), qrow), pl.BlockSpec((tq, d), qrow)],
        out_shape=[jax.ShapeDtypeStruct((t, d), F32), jax.ShapeDtypeStruct((t, d), F32)],
        scratch_shapes=[pltpu.VMEM((cfg.heads, tq, LANES), F32), pltpu.VMEM((cfg.heads, tq, LANES), F32),
                        pltpu.VMEM((tq, d), F32)],
        compiler_params=_params(("parallel", "parallel", "arbitrary")),
    )(q, kv, kv, cum_t, cum_t)
    return o, lse


def _fox_bwd_q(q, kv, cum_t, do, lse, dsum, cfg, *, name):
    t, d, hrows = cfg.tokens, cfg.d_model, cum_t.shape[0]
    tq = _blk(cfg.seq, FOX_BLOCK)
    nq = cfg.seq // tq

    def body(q_ref, k_ref, v_ref, cq_ref, ck_ref, do_ref, l_ref, ds_ref, dq_ref, acc_ref):
        qi, ki = pl.program_id(1), pl.program_id(2)

        @pl.when(ki == 0)
        def _():
            acc_ref[...] = jnp.zeros_like(acc_ref)

        @pl.when(ki <= qi)
        def _():
            mask = _causal(qi, ki, tq)
            half = _half_mask((tq, LANES))
            for hp in range(d // LANES):
                sl = slice(hp * LANES, (hp + 1) * LANES)
                q2, k2, v2 = q_ref[:, sl], k_ref[:, sl], v_ref[:, sl]
                do2 = do_ref[:, sl].astype(BF16)
                dqs = []
                for e in range(2):
                    h = 2 * hp + e
                    lane0 = hp * LANES + e * HEAD_DIM
                    cref = cq_ref[h:h + 1, 0:1]
                    s = _fox_scores(q2, k2, e, half, mask, cref, ck_ref[h:h + 1, :])
                    p = jnp.exp(s - (l_ref[:, lane0:lane0 + 1] + cref))
                    doe = jnp.where(half == (e == 0), do2, jnp.zeros_like(do2))
                    dp = lax.dot_general(doe, v2, _NT, preferred_element_type=F32)
                    dsc = (p * (dp - ds_ref[:, lane0:lane0 + 1])).astype(BF16)
                    dqs.append(jnp.dot(dsc, k2, preferred_element_type=F32))
                acc_ref[:, sl] += jnp.where(half, dqs[0], dqs[1])

        @pl.when(ki == qi)
        def _():
            dq_ref[...] = acc_ref[...]

    def qrow(b, qi, ki):
        return (b * nq + qi, 0)

    return pl.pallas_call(
        body, name=name, grid=(cfg.batch, nq, nq),
        in_specs=[pl.BlockSpec((tq, d), qrow),
                  pl.BlockSpec((tq, d), lambda b, qi, ki: (b * nq + jnp.minimum(ki, qi), 0)),
                  pl.BlockSpec((tq, d), lambda b, qi, ki: (b * nq + jnp.minimum(ki, qi), 1)),
                  pl.BlockSpec((hrows, tq), lambda b, qi, ki: (0, b * nq + qi)),
                  pl.BlockSpec((hrows, tq), lambda b, qi, ki: (0, b * nq + jnp.minimum(ki, qi))),
                  pl.BlockSpec((tq, d), qrow), pl.BlockSpec((tq, d), qrow), pl.BlockSpec((tq, d), qrow)],
        out_specs=pl.BlockSpec((tq, d), qrow),
        out_shape=jax.ShapeDtypeStruct((t, d), F32),
        scratch_shapes=[pltpu.VMEM((tq, d), F32)],
        compiler_params=_params(("parallel", "parallel", "arbitrary")),
    )(q, kv, kv, cum_t, cum_t, do, lse, dsum)


def _fox_bwd_kv(q, kv, cum_t, do, lse, dsum, cfg, *, name):
    t, d, hrows = cfg.tokens, cfg.d_model, cum_t.shape[0]
    tq = _blk(cfg.seq, FOX_BLOCK)
    nq = cfg.seq // tq

    def body(q_ref, k_ref, v_ref, cq_ref, ck_ref, do_ref, l_ref, ds_ref, dk_ref, dv_ref, dc_ref,
             kacc_ref, vacc_ref, cacc_ref):
        ki, qi = pl.program_id(1), pl.program_id(2)

        @pl.when(qi == 0)
        def _():
            kacc_ref[...] = jnp.zeros_like(kacc_ref)
            vacc_ref[...] = jnp.zeros_like(vacc_ref)
            cacc_ref[...] = jnp.zeros_like(cacc_ref)

        @pl.when(qi >= ki)
        def _():
            mask = _causal(qi, ki, tq)
            half = _half_mask((tq, LANES))
            for hp in range(d // LANES):
                sl = slice(hp * LANES, (hp + 1) * LANES)
                q2, k2, v2 = q_ref[:, sl], k_ref[:, sl], v_ref[:, sl]
                do2 = do_ref[:, sl].astype(BF16)
                dks, dvs = [], []
                for e in range(2):
                    h = 2 * hp + e
                    lane0 = hp * LANES + e * HEAD_DIM
                    cref = cq_ref[h:h + 1, 0:1]
                    s = _fox_scores(q2, k2, e, half, mask, cref, ck_ref[h:h + 1, :])
                    p = jnp.exp(s - (l_ref[:, lane0:lane0 + 1] + cref))
                    doe = jnp.where(half == (e == 0), do2, jnp.zeros_like(do2))
                    dp = lax.dot_general(doe, v2, _NT, preferred_element_type=F32)
                    dsf = p * (dp - ds_ref[:, lane0:lane0 + 1])
                    cacc_ref[h:h + 1, :] -= jnp.sum(dsf, axis=0, keepdims=True)
                    dks.append(lax.dot_general(dsf.astype(BF16), q2, _TN, preferred_element_type=F32))
                    dvs.append(lax.dot_general(p.astype(BF16), do2, _TN, preferred_element_type=F32))
                kacc_ref[:, sl] += jnp.where(half, dks[0], dks[1])
                vacc_ref[:, sl] += jnp.where(half, dvs[0], dvs[1])

        @pl.when(qi == nq - 1)
        def _():
            dk_ref[...] = kacc_ref[...]
            dv_ref[...] = vacc_ref[...]
            dc_ref[...] = cacc_ref[...]

    def qrow(b, ki, qi):
        return (b * nq + jnp.maximum(qi, ki), 0)

    def krow(b, ki, qi):
        return (b * nq + ki, 0)

    return pl.pallas_call(
        body, name=name, grid=(cfg.batch, nq, nq),
        in_specs=[pl.BlockSpec((tq, d), qrow),
                  pl.BlockSpec((tq, d), krow),
                  pl.BlockSpec((tq, d), lambda b, ki, qi: (b * nq + ki, 1)),
                  pl.BlockSpec((hrows, tq), lambda b, ki, qi: (0, b * nq + jnp.maximum(qi, ki))),
                  pl.BlockSpec((hrows, tq), lambda b, ki, qi: (0, b * nq + ki)),
                  pl.BlockSpec((tq, d), qrow), pl.BlockSpec((tq, d), qrow), pl.BlockSpec((tq, d), qrow)],
        out_specs=[pl.BlockSpec((tq, d), krow), pl.BlockSpec((tq, d), krow),
                   pl.BlockSpec((hrows, tq), lambda b, ki, qi: (0, b * nq + ki))],
        out_shape=[jax.ShapeDtypeStruct((t, d), F32), jax.ShapeDtypeStruct((t, d), F32),
                   jax.ShapeDtypeStruct((hrows, t), F32)],
        scratch_shapes=[pltpu.VMEM((tq, d), F32), pltpu.VMEM((tq, d), F32), pltpu.VMEM((hrows, tq), F32)],
        compiler_params=_params(("parallel", "parallel", "arbitrary")),
    )(q, kv, kv, cum_t, cum_t, do, lse, dsum)


AUG = LANES
BIAS_TERMS = 3


def _fox_aug_q(qp, cfg):
    t, hh = cfg.tokens, cfg.heads
    q3 = qp.reshape(t, hh, HEAD_DIM)
    ones = jnp.ones((t, hh, BIAS_TERMS), BF16)
    zeros = jnp.zeros((t, hh, AUG - HEAD_DIM - BIAS_TERMS), BF16)
    return jnp.concatenate([q3, ones, zeros], axis=2).reshape(t, hh * AUG).T


def _fox_aug_k(k, cum, cfg):
    t, hh = cfg.tokens, cfg.heads
    c = -cum
    hi = lax.reduce_precision(c, 8, 7)
    mid = lax.reduce_precision(c - hi, 8, 7)
    lo = c - hi - mid
    zeros = jnp.zeros((t, hh, AUG - HEAD_DIM - BIAS_TERMS), BF16)
    parts = [k.reshape(t, hh, HEAD_DIM)] + [z.astype(BF16)[..., None] for z in (hi, mid, lo)] + [zeros]
    return jnp.concatenate(parts, axis=2).reshape(t, hh * AUG)


def _keys_visible(tq):
    s = lax.broadcasted_iota(jnp.int32, (tq, tq), 0)
    t = lax.broadcasted_iota(jnp.int32, (tq, tq), 1)
    return s <= t


def _fox_fwd_t(qa_t, k_aug, v_t, cfg, *, name):
    t, d, hh = cfg.tokens, cfg.d_model, cfg.heads
    tq = _blk(cfg.seq, FOX_BLOCK)
    nq = cfg.seq // tq

    def body(qa_ref, ka_ref, vt_ref, o_ref, lse_ref, m_ref, l_ref, acc_ref):
        qi, ki = pl.program_id(1), pl.program_id(2)

        @pl.when(ki == 0)
        def _():
            m_ref[...] = jnp.full_like(m_ref, NEG)
            l_ref[...] = jnp.zeros_like(l_ref)
            acc_ref[...] = jnp.zeros_like(acc_ref)

        def step(diagonal):
            for h in range(hh):
                rows = slice(h * HEAD_DIM, (h + 1) * HEAD_DIM)
                s = jnp.dot(ka_ref[:, h * AUG:(h + 1) * AUG], qa_ref[h * AUG:(h + 1) * AUG, :],
                            preferred_element_type=F32)
                if diagonal:
                    s = jnp.where(_keys_visible(tq), s, NEG)
                m_prev = m_ref[h:h + 1, :]
                m_new = jnp.maximum(m_prev, jnp.max(s, axis=0, keepdims=True))
                alpha = jnp.exp(m_prev - m_new)
                p = jnp.exp(s - m_new)
                l_ref[h:h + 1, :] = alpha * l_ref[h:h + 1, :] + jnp.sum(p, axis=0, keepdims=True)
                m_ref[h:h + 1, :] = m_new
                hi = p.astype(BF16)
                lo = (p - hi.astype(F32)).astype(BF16)
                vt = vt_ref[rows, :]
                acc_ref[rows, :] = (alpha * acc_ref[rows, :] + jnp.dot(vt, hi, preferred_element_type=F32)
                                    + jnp.dot(vt, lo, preferred_element_type=F32))

        pl.when(ki < qi)(functools.partial(step, False))
        pl.when(ki == qi)(functools.partial(step, True))

        @pl.when(ki == qi)
        def _():
            for h in range(hh):
                rows = slice(h * HEAD_DIM, (h + 1) * HEAD_DIM)
                o_ref[rows, :] = acc_ref[rows, :] * (1.0 / l_ref[h:h + 1, :])
            lse_ref[...] = m_ref[...] + jnp.log(l_ref[...])

    def qcol(b, qi, ki):
        return (0, b * nq + qi)

    return pl.pallas_call(
        body, name=name, grid=(cfg.batch, nq, nq),
        in_specs=[pl.BlockSpec((hh * AUG, tq), qcol),
                  pl.BlockSpec((tq, hh * AUG), lambda b, qi, ki: (b * nq + jnp.minimum(ki, qi), 0)),
                  pl.BlockSpec((d, tq), lambda b, qi, ki: (0, b * nq + jnp.minimum(ki, qi)))],
        out_specs=[pl.BlockSpec((d, tq), qcol), pl.BlockSpec((hh, tq), qcol)],
        out_shape=[jax.ShapeDtypeStruct((d, t), F32), jax.ShapeDtypeStruct((hh, t), F32)],
        scratch_shapes=[pltpu.VMEM((hh, tq), F32), pltpu.VMEM((hh, tq), F32), pltpu.VMEM((d, tq), F32)],
        compiler_params=_params(("parallel", "parallel", "arbitrary")),
    )(qa_t, k_aug, v_t)


def _aug_q_t(q2, e, tq):
    ones = (lax.broadcasted_iota(jnp.int32, (AUG - HEAD_DIM, tq), 0) < BIAS_TERMS).astype(q2.dtype)
    return jnp.concatenate([q2[e * HEAD_DIM:(e + 1) * HEAD_DIM], ones], axis=0)


def _fox_fwd_n(q, k_aug, v, cfg, *, name):
    t, d, hh = cfg.tokens, cfg.d_model, cfg.heads
    tq = _blk(cfg.seq, FOX_BLOCK)
    nq = cfg.seq // tq

    def body(q_ref, ka_ref, v_ref, o_ref, lse_ref, qa_ref, m_ref, l_ref, acc_ref):
        qi, ki = pl.program_id(1), pl.program_id(2)

        @pl.when(ki == 0)
        def _():
            m_ref[...] = jnp.full_like(m_ref, NEG)
            l_ref[...] = jnp.zeros_like(l_ref)
            acc_ref[...] = jnp.zeros_like(acc_ref)
            for hp in range(hh // 2):
                q2 = q_ref[:, hp * LANES:(hp + 1) * LANES].T
                for e in range(2):
                    h = 2 * hp + e
                    qa_ref[h * AUG:(h + 1) * AUG, :] = _aug_q_t(q2, e, tq)

        def step(diagonal):
            for hp in range(hh // 2):
                vt2 = v_ref[:, hp * LANES:(hp + 1) * LANES].T
                for e in range(2):
                    h = 2 * hp + e
                    rows = slice(h * HEAD_DIM, (h + 1) * HEAD_DIM)
                    s = jnp.dot(ka_ref[:, h * AUG:(h + 1) * AUG], qa_ref[h * AUG:(h + 1) * AUG, :],
                                preferred_element_type=F32)
                    if diagonal:
                        s = jnp.where(_keys_visible(tq), s, NEG)
                    m_prev = m_ref[h:h + 1, :]
                    m_new = jnp.maximum(m_prev, jnp.max(s, axis=0, keepdims=True))
                    alpha = jnp.exp(m_prev - m_new)
                    p = jnp.exp(s - m_new)
                    l_ref[h:h + 1, :] = alpha * l_ref[h:h + 1, :] + jnp.sum(p, axis=0, keepdims=True)
                    m_ref[h:h + 1, :] = m_new
                    hi = p.astype(BF16)
                    lo = (p - hi.astype(F32)).astype(BF16)
                    vt = vt2[e * HEAD_DIM:(e + 1) * HEAD_DIM]
                    acc_ref[rows, :] = (alpha * acc_ref[rows, :] + jnp.dot(vt, hi, preferred_element_type=F32)
                                        + jnp.dot(vt, lo, preferred_element_type=F32))

        pl.when(ki < qi)(functools.partial(step, False))
        pl.when(ki == qi)(functools.partial(step, True))

        @pl.when(ki == qi)
        def _():
            for hp in range(hh // 2):
                halves = [acc_ref[h * HEAD_DIM:(h + 1) * HEAD_DIM, :] * (1.0 / l_ref[h:h + 1, :])
                          for h in (2 * hp, 2 * hp + 1)]
                o_ref[:, hp * LANES:(hp + 1) * LANES] = jnp.concatenate(halves, axis=0).T
            lse_ref[...] = m_ref[...] + jnp.log(l_ref[...])

    def qrow(b, qi, ki):
        return (b * nq + qi, 0)

    def krow(b, qi, ki):
        return (b * nq + jnp.minimum(ki, qi), 0)

    return pl.pallas_call(
        body, name=name, grid=(cfg.batch, nq, nq),
        in_specs=[pl.BlockSpec((tq, d), qrow), pl.BlockSpec((tq, hh * AUG), krow), pl.BlockSpec((tq, d), krow)],
        out_specs=[pl.BlockSpec((tq, d), qrow), pl.BlockSpec((hh, tq), lambda b, qi, ki: (0, b * nq + qi))],
        out_shape=[jax.ShapeDtypeStruct((t, d), F32), jax.ShapeDtypeStruct((hh, t), F32)],
        scratch_shapes=[pltpu.VMEM((hh * AUG, tq), BF16), pltpu.VMEM((hh, tq), F32), pltpu.VMEM((hh, tq), F32),
                        pltpu.VMEM((d, tq), F32)],
        compiler_params=_params(("parallel", "parallel", "arbitrary")),
    )(q, k_aug, v)


def _head_dot_c(a, b, cfg, *, name):
    t, d, hh = cfg.tokens, cfg.d_model, cfg.heads
    tc = _blk(t, ROW_BLOCK)

    def body(a_ref, b_ref, o_ref):
        for hp in range(hh // 2):
            pair = slice(hp * LANES, (hp + 1) * LANES)
            prod = (a_ref[:, pair].astype(F32) * b_ref[:, pair]).T
            for e in range(2):
                h = 2 * hp + e
                o_ref[h:h + 1, :] = jnp.sum(prod[e * HEAD_DIM:(e + 1) * HEAD_DIM], axis=0, keepdims=True)

    return pl.pallas_call(
        body, name=name, grid=(t // tc,),
        in_specs=[pl.BlockSpec((tc, d), lambda i: (i, 0)), pl.BlockSpec((tc, d), lambda i: (i, 0))],
        out_specs=pl.BlockSpec((hh, tc), lambda i: (0, i)),
        out_shape=jax.ShapeDtypeStruct((hh, t), F32),
        compiler_params=_params(("parallel",)),
    )(a, b)


def _fox_bwd_n(q, k_aug, k_t, v, do, lse, dsum, cfg, *, name):
    t, d, hh = cfg.tokens, cfg.d_model, cfg.heads
    tq = _blk(cfg.seq, FOX_BLOCK)
    nq = cfg.seq // tq

    def body(q_ref, ka_ref, kt_ref, v_ref, do_ref, lse_ref, ds_ref, dq_hbm, dk_ref, dv_ref, dc_ref, dq_acc, sem):
        b, ki, qi = pl.program_id(0), pl.program_id(1), pl.program_id(2)
        qq = jnp.maximum(qi, ki)

        @pl.when((ki == 0) & (qi == 0))
        def _():
            dq_acc[...] = jnp.zeros_like(dq_acc)

        @pl.when(qi == 0)
        def _():
            dk_ref[...] = jnp.zeros_like(dk_ref)
            dv_ref[...] = jnp.zeros_like(dv_ref)
            dc_ref[...] = jnp.zeros_like(dc_ref)

        def step(diagonal):
            upper = lax.broadcasted_iota(jnp.int32, (LANES, tq), 0) < HEAD_DIM
            half = _half_mask((tq, LANES))
            for hp in range(hh // 2):
                pair = slice(hp * LANES, (hp + 1) * LANES)
                q2 = q_ref[:, pair].T
                don2 = do_ref[:, pair]
                dot2 = don2.T
                dvs = []
                for e in range(2):
                    h = 2 * hp + e
                    rows = slice(h * HEAD_DIM, (h + 1) * HEAD_DIM)
                    aug = slice(h * AUG, (h + 1) * AUG)
                    qa = _aug_q_t(q2, e, tq)
                    s = jnp.dot(ka_ref[:, aug], qa, preferred_element_type=F32)
                    if diagonal:
                        s = jnp.where(_keys_visible(tq), s, NEG)
                    p = jnp.exp(s - lse_ref[h:h + 1, :])
                    dote = jnp.where(upper == (e == 0), dot2, jnp.zeros_like(dot2))
                    dp = jnp.dot(v_ref[:, pair], dote, preferred_element_type=F32)
                    dsf = p * (dp - ds_ref[h:h + 1, :])
                    dc_ref[:, h:h + 1] -= jnp.sum(dsf, axis=1, keepdims=True)
                    dsc = dsf.astype(BF16)
                    dvs.append(jnp.dot(p.astype(BF16), don2, preferred_element_type=F32))
                    dk_ref[:, aug] += lax.dot_general(dsc, qa, _NT, preferred_element_type=F32)
                    dq_acc[qq, rows, :] += jnp.dot(kt_ref[rows, :], dsc, preferred_element_type=F32)
                dv_ref[:, pair] += jnp.where(half, dvs[0], dvs[1])

        pl.when(qi > ki)(functools.partial(step, False))
        pl.when(qi == ki)(functools.partial(step, True))

        @pl.when((ki == nq - 1) & (qi == nq - 1))
        def _():
            cp = pltpu.make_async_copy(dq_acc, dq_hbm.at[b], sem)
            cp.start()
            cp.wait()

    def qrow(b, ki, qi):
        return (b * nq + jnp.maximum(qi, ki), 0)

    def qcol(b, ki, qi):
        return (0, b * nq + jnp.maximum(qi, ki))

    def krow(b, ki, qi):
        return (b * nq + ki, 0)

    return pl.pallas_call(
        body, name=name, grid=(cfg.batch, nq, nq),
        in_specs=[pl.BlockSpec((tq, d), qrow),
                  pl.BlockSpec((tq, hh * AUG), krow),
                  pl.BlockSpec((d, tq), lambda b, ki, qi: (0, b * nq + ki)),
                  pl.BlockSpec((tq, d), krow),
                  pl.BlockSpec((tq, d), qrow),
                  pl.BlockSpec((hh, tq), qcol), pl.BlockSpec((hh, tq), qcol)],
        out_specs=[pl.BlockSpec(memory_space=pl.ANY), pl.BlockSpec((tq, hh * AUG), krow),
                   pl.BlockSpec((tq, d), krow), pl.BlockSpec((tq, LANES), krow)],
        out_shape=[jax.ShapeDtypeStruct((cfg.batch, nq, d, tq), F32), jax.ShapeDtypeStruct((t, hh * AUG), F32),
                   jax.ShapeDtypeStruct((t, d), F32), jax.ShapeDtypeStruct((t, LANES), F32)],
        scratch_shapes=[pltpu.VMEM((nq, d, tq), F32), pltpu.SemaphoreType.DMA],
        compiler_params=_params(("arbitrary", "arbitrary", "arbitrary")),
    )(q, k_aug, k_t, v, do, lse, dsum)


def _head_dot_t(a_t, b_t, cfg, *, name):
    t, d, hh = cfg.tokens, cfg.d_model, cfg.heads
    tc = _blk(t, 2 * ROW_BLOCK)

    def body(a_ref, b_ref, o_ref):
        for h in range(hh):
            rows = slice(h * HEAD_DIM, (h + 1) * HEAD_DIM)
            o_ref[h:h + 1, :] = jnp.sum(a_ref[rows, :].astype(F32) * b_ref[rows, :], axis=0, keepdims=True)

    return pl.pallas_call(
        body, name=name, grid=(t // tc,),
        in_specs=[pl.BlockSpec((d, tc), lambda i: (0, i)), pl.BlockSpec((d, tc), lambda i: (0, i))],
        out_specs=pl.BlockSpec((hh, tc), lambda i: (0, i)),
        out_shape=jax.ShapeDtypeStruct((hh, t), F32),
        compiler_params=_params(("parallel",)),
    )(a_t, b_t)


def _fox_bwd_t(qa_t, k_aug, k_t, v, do_t, do, lse, dsum, cfg, *, name):
    t, d, hh = cfg.tokens, cfg.d_model, cfg.heads
    tq = _blk(cfg.seq, FOX_BLOCK)
    nq = cfg.seq // tq

    def body(qa_ref, ka_ref, kt_ref, v_ref, dot_ref, do_ref, lse_ref, ds_ref, dq_hbm, dk_ref, dv_ref, dc_ref,
             dq_acc, sem):
        b, ki, qi = pl.program_id(0), pl.program_id(1), pl.program_id(2)
        qq = jnp.maximum(qi, ki)

        @pl.when((ki == 0) & (qi == 0))
        def _():
            dq_acc[...] = jnp.zeros_like(dq_acc)

        @pl.when(qi == 0)
        def _():
            dk_ref[...] = jnp.zeros_like(dk_ref)
            dv_ref[...] = jnp.zeros_like(dv_ref)
            dc_ref[...] = jnp.zeros_like(dc_ref)

        def step(diagonal):
            upper = lax.broadcasted_iota(jnp.int32, (LANES, tq), 0) < HEAD_DIM
            half = _half_mask((tq, LANES))
            for hp in range(hh // 2):
                pair = slice(hp * LANES, (hp + 1) * LANES)
                dvs = []
                for e in range(2):
                    h = 2 * hp + e
                    rows = slice(h * HEAD_DIM, (h + 1) * HEAD_DIM)
                    aug = slice(h * AUG, (h + 1) * AUG)
                    s = jnp.dot(ka_ref[:, aug], qa_ref[aug, :], preferred_element_type=F32)
                    if diagonal:
                        s = jnp.where(_keys_visible(tq), s, NEG)
                    p = jnp.exp(s - lse_ref[h:h + 1, :])
                    dot2 = dot_ref[pair, :]
                    dote = jnp.where(upper == (e == 0), dot2, jnp.zeros_like(dot2))
                    dp = jnp.dot(v_ref[:, pair], dote, preferred_element_type=F32)
                    dsf = p * (dp - ds_ref[h:h + 1, :])
                    dc_ref[:, h:h + 1] -= jnp.sum(dsf, axis=1, keepdims=True)
                    dsc = dsf.astype(BF16)
                    dvs.append(jnp.dot(p.astype(BF16), do_ref[:, pair], preferred_element_type=F32))
                    dk_ref[:, aug] += lax.dot_general(dsc, qa_ref[aug, :], _NT, preferred_element_type=F32)
                    dq_acc[qq, rows, :] += jnp.dot(kt_ref[rows, :], dsc, preferred_element_type=F32)
                dv_ref[:, pair] += jnp.where(half, dvs[0], dvs[1])

        pl.when(qi > ki)(functools.partial(step, False))
        pl.when(qi == ki)(functools.partial(step, True))

        @pl.when((ki == nq - 1) & (qi == nq - 1))
        def _():
            cp = pltpu.make_async_copy(dq_acc, dq_hbm.at[b], sem)
            cp.start()
            cp.wait()

    def qcol(b, ki, qi):
        return (0, b * nq + jnp.maximum(qi, ki))

    def krow(b, ki, qi):
        return (b * nq + ki, 0)

    return pl.pallas_call(
        body, name=name, grid=(cfg.batch, nq, nq),
        in_specs=[pl.BlockSpec((hh * AUG, tq), qcol),
                  pl.BlockSpec((tq, hh * AUG), krow),
                  pl.BlockSpec((d, tq), lambda b, ki, qi: (0, b * nq + ki)),
                  pl.BlockSpec((tq, d), krow),
                  pl.BlockSpec((d, tq), qcol),
                  pl.BlockSpec((tq, d), lambda b, ki, qi: (b * nq + jnp.maximum(qi, ki), 0)),
                  pl.BlockSpec((hh, tq), qcol), pl.BlockSpec((hh, tq), qcol)],
        out_specs=[pl.BlockSpec(memory_space=pl.ANY), pl.BlockSpec((tq, hh * AUG), krow),
                   pl.BlockSpec((tq, d), krow), pl.BlockSpec((tq, LANES), krow)],
        out_shape=[jax.ShapeDtypeStruct((cfg.batch, nq, d, tq), F32), jax.ShapeDtypeStruct((t, hh * AUG), F32),
                   jax.ShapeDtypeStruct((t, d), F32), jax.ShapeDtypeStruct((t, LANES), F32)],
        scratch_shapes=[pltpu.VMEM((nq, d, tq), F32), pltpu.SemaphoreType.DMA],
        compiler_params=_params(("arbitrary", "arbitrary", "arbitrary")),
    )(qa_t, k_aug, k_t, v, do_t, do, lse, dsum)


WIDE = 1536


def _fwd(a, w, *, name, res=None, scale=1.0):
    return _mm(a, w, form="F", out_dtype=F32, name=name, bn=WIDE, bk=WIDE, res=res, scale=scale)


def _bwd(dy, w, *, name, scale=1.0):
    return _mm(dy, w, form="B", out_dtype=F32, name=name, bn=WIDE, bk=WIDE, scale=scale)


def _wgrad(a, dy, w, *, name, scale=1.0):
    return _mm_grad(a, dy, w.shape[0], name=name, bm=WIDE, bn=WIDE, scale=scale)


def _ffn_fwd(h, g, w_in, w_out, tag):
    n = _rms_fwd(h, g, name=f"{tag}_norm")
    gate, up, a = _ffn_in_act(n, w_in, name=f"{tag}_in")
    return _fwd(a, w_out, name=f"{tag}_out", res=h, scale=0.5), (n, gate, up, a)


def _ffn_bwd(dh_out, h, g, w_in, w_out, saved, tag):
    n, gate, up, a = saved
    du = _ffn_out_dx_act(dh_out, w_out, gate, up, name=f"{tag}_out_dx", scale=0.5)
    dw_out = _wgrad(a, dh_out, w_out, name=f"{tag}_out_dw", scale=0.5)
    dn = _mm_back2(du, w_in, name=f"{tag}_in_dx", bn=WIDE, bk=WIDE)
    dw_in = _mm_grad(n, du, w_in.shape[0], name=f"{tag}_in_dw", bm=WIDE, bn=WIDE)
    dh, dg = _rms_bwd(h, g, dn, dh_out, name=f"{tag}_norm_bwd")
    return dh, dg, dw_in, dw_out


def _head_gain(g, heads, scale=1.0):
    return jnp.tile(g.astype(F32) * scale, heads)


def _local_step(cfg, x, positions, target, w, s):
    d, hh = cfg.d_model, cfg.heads
    cos, sin = _rope_tables(positions)
    ones = jnp.ones((d,), F32)

    h1, ffn0 = _ffn_fwd(x, s["ffn_norm"][0, 0], w["ffn_w_in"][0][0], w["ffn_w_out"][0][0], "ffn00")
    hn_a = _rms_fwd(h1, s["mix_norm"][0], name="a_norm")
    qkv = _fwd(hn_a, w["a_w_qkv"], name="a_qkv")
    kinds_a = ["rope", "rope", "cast"] * len(DILATIONS)
    gains_a = jnp.stack([z for g in range(len(DILATIONS)) for z in (
        _head_gain(s["a_q_norm"][g], hh, Q_SCALE), _head_gain(s["a_k_norm"][g], hh), ones)])
    qkvp = [_hn_fwd(qkv, gains_a[3 * g:3 * g + 3], kinds_a[:3], d, cos, sin, name=f"a_qk_norm{g}", col0=3 * g)
            for g in range(len(DILATIONS))]
    lay = [qkvp[g].reshape(cfg.tokens // dil, dil * 3 * d) for g, dil in enumerate(DILATIONS)]
    band = [_band_fwd_n(lay[g], dil, cfg, name=f"a_band{g}") for g, dil in enumerate(DILATIONS)]
    mixed, lse_a = _mix_fwd([o.reshape(cfg.tokens, d) for o, _ in band],
                            [jnp.repeat(_from_classes_t(l), HEAD_DIM, axis=1) for _, l in band], name="a_mix")
    h2 = _fwd(mixed, w["a_w_o"], name="a_out", res=h1)
    h3, ffn1 = _ffn_fwd(h2, s["ffn_norm"][0, 1], w["ffn_w_in"][0][1], w["ffn_w_out"][0][1], "ffn01")

    kn = _rms_fwd(h3, s["kv_norm"], name="kv_norm")
    proj = _fwd(kn, w["kv_w"], name="kv_proj")
    kinds_kv = ["norm", "cast"]
    gains_kv = jnp.stack([_head_gain(s["kv_k_norm"], hh), ones])
    kvp = _hn_fwd(proj, gains_kv, kinds_kv, d, cos, sin, name="kv_k_norm")
    gate_col = 2 * d // LANES
    bias = jnp.pad(s["kv_b_f"].astype(F32), (0, LANES - hh))
    cum = _gate_fwd(proj, gate_col, bias, cfg, name="kv_gate")
    k_b, v_b = kvp[:, :d], kvp[:, d:]
    k_aug = _fox_aug_k(k_b, cum[:, :hh], cfg)

    h4, ffn2 = _ffn_fwd(h3, s["ffn_norm"][1, 0], w["ffn_w_in"][1][0], w["ffn_w_out"][1][0], "ffn10")
    hn_b = _rms_fwd(h4, s["mix_norm"][1], name="b_norm")
    qraw = _fwd(hn_b, w["b_w_q"], name="b_q")
    gains_b = _head_gain(s["b_q_norm"][0], hh, Q_SCALE)[None]
    qp = _hn_fwd(qraw, gains_b, ["norm"], d, cos, sin, name="b_q_norm")
    o_b, lse_b = _fox_fwd_n(qp, k_aug, v_b, cfg, name="b_fox")
    h5 =_fwd(o_b, w["b_w_o"], name="b_out", res=h4)
    h6, ffn3 = _ffn_fwd(h5, s["ffn_norm"][1, 1], w["ffn_w_in"][1][1], w["ffn_w_out"][1][1], "ffn11")

    loss, dh6 = _loss_fwd_bwd(h6, target, name="loss")

    dh5, dg11, dwi11, dwo11 = _ffn_bwd(dh6, h5, s["ffn_norm"][1, 1], w["ffn_w_in"][1][1], w["ffn_w_out"][1][1],
                                       ffn3, "ffn11")
    do_b = _bwd(dh5, w["b_w_o"], name="b_out_dx")
    dw_bo = _wgrad(o_b, dh5, w["b_w_o"], name="b_out_dw")
    do_bf = do_b.astype(BF16)
    dsum_b = _head_dot_c(do_bf, o_b, cfg, name="b_dsum")
    dq4, dk_aug, dv_b, dcum = _fox_bwd_n(qp, k_aug, k_b.T, v_b, do_bf, lse_b, dsum_b, cfg, name="b_fox_bwd")
    dq_b = dq4.transpose(0, 1, 3, 2).reshape(cfg.tokens, d)
    dk_b = dk_aug.reshape(cfg.tokens, hh, AUG)[:, :, :HEAD_DIM].reshape(cfg.tokens, d)
    dqraw, dgq = _hn_bwd(qraw, [dq_b], gains_b, ["norm"], d, cos, sin, name="b_q_norm_bwd")
    dhn_b = _bwd(dqraw, w["b_w_q"], name="b_q_dx")
    dw_bq = _wgrad(hn_b, dqraw, w["b_w_q"], name="b_q_dw")
    dh4, dmix1 = _rms_bwd(h4, s["mix_norm"][1], dhn_b, dh5, name="b_norm_bwd")
    dh3, dg10, dwi10, dwo10 = _ffn_bwd(dh4, h3, s["ffn_norm"][1, 0], w["ffn_w_in"][1][0], w["ffn_w_out"][1][0],
                                       ffn2, "ffn10")

    dkvraw, dgk = _hn_bwd(proj, [dk_b, dv_b], gains_kv, kinds_kv, d, cos, sin,
                          name="kv_k_norm_bwd")
    dz, dbias = _gate_bwd(proj, gate_col, bias, dcum, cfg, name="kv_gate_bwd")
    pad_cols = w["kv_w"].shape[2] - 2 * d - LANES
    dproj = jnp.concatenate([dkvraw, dz.astype(BF16), jnp.zeros((cfg.tokens, pad_cols), BF16)], axis=1)
    dkn = _bwd(dproj, w["kv_w"], name="kv_proj_dx")
    dw_kv = _wgrad(kn, dproj, w["kv_w"], name="kv_proj_dw")
    dh3, dkvn = _rms_bwd(h3, s["kv_norm"], dkn, dh3, name="kv_norm_bwd")

    dh2, dg01, dwi01, dwo01 = _ffn_bwd(dh3, h2, s["ffn_norm"][0, 1], w["ffn_w_in"][0][1], w["ffn_w_out"][0][1],
                                       ffn1, "ffn01")
    dmixed = _bwd(dh2, w["a_w_o"], name="a_out_dx")
    dw_ao = _wgrad(mixed, dh2, w["a_w_o"], name="a_out_dw")
    dsum_a = _head_dot(dmixed, mixed, name="a_dsum")
    dqkvp = []
    dmixed_bf = dmixed.astype(BF16)
    lse_h, dsum_h = lse_a[:, ::HEAD_DIM], dsum_a[:, ::HEAD_DIM]
    for g, dil in enumerate(DILATIONS):
        grads = _band_bwd_n(lay[g], dmixed_bf.reshape(cfg.tokens // dil, dil * d), _to_classes_t(lse_h, dil, hh),
                            _to_classes_t(dsum_h, dil, hh), dil, cfg, name=f"a_band{g}_bwd")
        dqkvp += [z.reshape(cfg.tokens, d) for z in grads]
    dqkv, dga = _hn_bwd(qkv, dqkvp, gains_a, kinds_a, d, cos, sin, name="a_qk_norm_bwd")
    dhn_a = _bwd(dqkv, w["a_w_qkv"], name="a_qkv_dx")
    dw_qkv = _wgrad(hn_a, dqkv, w["a_w_qkv"], name="a_qkv_dw")
    dh1, dmix0 = _rms_bwd(h1, s["mix_norm"][0], dhn_a, dh2, name="a_norm_bwd")
    dx, dg00, dwi00, dwo00 = _ffn_bwd(dh1, x, s["ffn_norm"][0, 0], w["ffn_w_in"][0][0], w["ffn_w_out"][0][0],
                                      ffn0, "ffn00")

    dw = {
        "ffn_w_in": [[dwi00, dwi01], [dwi10, dwi11]],
        "ffn_w_out": [[dwo00, dwo01], [dwo10, dwo11]],
        "a_w_qkv": dw_qkv, "a_w_o": dw_ao, "kv_w": dw_kv, "b_w_q": dw_bq, "b_w_o": dw_bo,
    }
    ds = {
        "ffn_norm": jnp.stack([jnp.stack([dg00, dg01]), jnp.stack([dg10, dg11])]),
        "mix_norm": jnp.stack([dmix0, dmix1]),
        "a_q_norm": jnp.stack([dga[3 * g] for g in range(len(DILATIONS))])[None] * Q_SCALE,
        "a_k_norm": jnp.stack([dga[3 * g + 1] for g in range(len(DILATIONS))])[None],
        "kv_norm": dkvn,
        "kv_b_f": dbias[:hh],
        "kv_k_norm": dgk[0],
        "b_q_norm": dgq * Q_SCALE,
    }
    return loss, dx, dw, ds


MESH_ID = pl.DeviceIdType.MESH
ANY = pl.BlockSpec(memory_space=pl.ANY)
PACK_COLS = 1024
PACK_ROW_ALIGN = 32


def _me():
    return lax.axis_index("x"), lax.axis_index("y"), lax.axis_index("c")


def _other_chips(x, y):
    return [(1 - x, y), (x, 1 - y), (1 - x, 1 - y)]


def _all_gather_small(v, *, name):
    r = v.shape[0]

    def body(v_ref, out_ref, send_sems, recv_sems):
        x, y, c = _me()
        me = 4 * x + 2 * y + c
        out_ref[me] = v_ref[...]
        copies = []
        for k in range(1, N_DEV):
            fx, fy, fc = (k >> 2) & 1, (k >> 1) & 1, k & 1
            peer = (1 - x if fx else x, 1 - y if fy else y, 1 - c if fc else c)
            copies.append(pltpu.make_async_remote_copy(
                src_ref=v_ref, dst_ref=out_ref.at[me], send_sem=send_sems.at[k - 1], recv_sem=recv_sems.at[k - 1],
                device_id=peer, device_id_type=MESH_ID))
        for cp in copies:
            cp.start()
        for cp in copies:
            cp.wait()

    return pl.pallas_call(
        body, name=name,
        in_specs=[pl.BlockSpec(memory_space=pltpu.VMEM)], out_specs=pl.BlockSpec(memory_space=pltpu.VMEM),
        out_shape=jax.ShapeDtypeStruct((N_DEV, r, LANES), v.dtype),
        scratch_shapes=[pltpu.SemaphoreType.DMA((N_DEV - 1,)), pltpu.SemaphoreType.DMA((N_DEV - 1,))],
    )(v)


def _all_gather_chips(v, *, name):
    rh = v.shape[0] // 2

    def body(v_ref, out_ref, send_sems, recv_sems):
        x, y, c = _me()
        j = 2 * x + y
        chips = _other_chips(x, y)

        def half(chip, core):
            return out_ref.at[chip, pl.ds(core * rh, rh)]

        first = [pltpu.make_async_remote_copy(
            src_ref=v_ref.at[pl.ds(c * rh, rh)], dst_ref=half(j, c), send_sem=send_sems.at[k],
            recv_sem=recv_sems.at[k], device_id=(px, py, c), device_id_type=MESH_ID)
            for k, (px, py) in enumerate(chips)]
        for cp in first:
            cp.start()
        passed = [pltpu.make_async_remote_copy(
            src_ref=half(2 * px + py, c), dst_ref=half(2 * px + py, c), send_sem=send_sems.at[3 + k],
            recv_sem=recv_sems.at[3 + k], device_id=(x, y, 1 - c), device_id_type=MESH_ID)
            for k, (px, py) in enumerate(chips)]
        for k in range(len(chips)):
            first[k].wait_recv()
            passed[k].start()
        for k, (px, py) in enumerate(chips):
            pltpu.make_async_remote_copy(
                src_ref=half(2 * px + py, 1 - c), dst_ref=half(2 * px + py, 1 - c), send_sem=send_sems.at[3 + k],
                recv_sem=recv_sems.at[3 + k], device_id=(x, y, 1 - c), device_id_type=MESH_ID).wait_recv()
        for cp in first + passed:
            cp.wait_send()

    return pl.pallas_call(
        body, name=name, in_specs=[ANY], out_specs=ANY,
        out_shape=jax.ShapeDtypeStruct((N_CHIPS,) + v.shape, v.dtype),
        scratch_shapes=[pltpu.SemaphoreType.DMA((2 * (N_CHIPS - 1),)), pltpu.SemaphoreType.DMA((2 * (N_CHIPS - 1),))],
    )(v)


def _swap_halves(g, *, name):
    n, r, cols = g.shape
    rh = r // 2

    def body(g_ref, out_ref, send_sem, recv_sem):
        x, y, c = _me()
        cp = pltpu.make_async_remote_copy(
            src_ref=g_ref.at[:, pl.ds((1 - c) * rh, rh)], dst_ref=out_ref, send_sem=send_sem, recv_sem=recv_sem,
            device_id=(x, y, 1 - c), device_id_type=MESH_ID)
        cp.start()
        cp.wait()

    return pl.pallas_call(
        body, name=name, in_specs=[ANY], out_specs=ANY,
        out_shape=jax.ShapeDtypeStruct((n, rh, cols), g.dtype),
        scratch_shapes=[pltpu.SemaphoreType.DMA, pltpu.SemaphoreType.DMA],
    )(g)


def _scatter_chips(v, *, name):
    def body(v_ref, out_ref, send_sems, recv_sems):
        x, y, c = _me()
        j = 2 * x + y
        copies = [pltpu.make_async_remote_copy(
            src_ref=v_ref.at[2 * px + py], dst_ref=out_ref.at[j], send_sem=send_sems.at[k], recv_sem=recv_sems.at[k],
            device_id=(px, py, c), device_id_type=MESH_ID) for k, (px, py) in enumerate(_other_chips(x, y))]
        for cp in copies:
            cp.start()
        for cp in copies:
            cp.wait()

    return pl.pallas_call(
        body, name=name, in_specs=[ANY], out_specs=ANY,
        out_shape=jax.ShapeDtypeStruct(v.shape, v.dtype),
        scratch_shapes=[pltpu.SemaphoreType.DMA((N_CHIPS - 1,)), pltpu.SemaphoreType.DMA((N_CHIPS - 1,))],
    )(v)


def _join_halves(v, *, name):
    def body(v_ref, out_ref, send_sem, recv_sem):
        x, y, c = _me()
        cp = pltpu.make_async_remote_copy(
            src_ref=v_ref, dst_ref=out_ref.at[c], send_sem=send_sem, recv_sem=recv_sem,
            device_id=(x, y, 1 - c), device_id_type=MESH_ID)
        cp.start()
        cp.wait()

    return pl.pallas_call(
        body, name=name, in_specs=[ANY], out_specs=ANY,
        out_shape=jax.ShapeDtypeStruct((2,) + v.shape, v.dtype),
        scratch_shapes=[pltpu.SemaphoreType.DMA, pltpu.SemaphoreType.DMA],
    )(v)


def _row_blk(rows, want):
    for b in range(min(rows, want) // SUBLANES * SUBLANES, 0, -SUBLANES):
        if rows % b == 0:
            return b
    return rows


def _add_own_half(g, got, *, name):
    n, r, cols = g.shape
    rh = r // 2
    tr = _row_blk(rh, 512)
    nb = rh // tr

    def body(c_ref, g_ref, got_ref, o_ref):
        del c_ref
        o_ref[...] = (g_ref[...] + got_ref[...]).astype(BF16)

    grid_spec = pltpu.PrefetchScalarGridSpec(
        num_scalar_prefetch=1, grid=(n, nb),
        in_specs=[pl.BlockSpec((None, tr, cols), lambda j, i, c: (j, c[0] * nb + i, 0)),
                  pl.BlockSpec((None, tr, cols), lambda j, i, c: (j, i, 0))],
        out_specs=pl.BlockSpec((None, tr, cols), lambda j, i, c: (j, i, 0)))
    return pl.pallas_call(
        body, name=name, grid_spec=grid_spec, out_shape=jax.ShapeDtypeStruct((n, rh, cols), BF16),
        compiler_params=_params(("parallel", "parallel")),
    )(lax.axis_index("c").astype(jnp.int32).reshape(1), g, got)


def _sum_parts(parts, *, name):
    n, r, cols = parts.shape
    tr = _row_blk(r, 512)

    def body(*refs):
        o_ref = refs[n]
        acc = refs[0][...].astype(F32)
        for p_ref in refs[1:n]:
            acc = acc + p_ref[...].astype(F32)
        o_ref[...] = acc

    return pl.pallas_call(
        body, name=name, grid=(r // tr,),
        in_specs=[pl.BlockSpec((None, tr, cols), functools.partial(lambda j, i: (j, i, 0), j)) for j in range(n)],
        out_specs=pl.BlockSpec((tr, cols), lambda i: (i, 0)),
        out_shape=jax.ShapeDtypeStruct((r, cols), F32),
        compiler_params=_params(("parallel",)),
    )(*([parts] * n))


def _adamw(w, m, v, g, *, name):
    shape = w.shape
    cols = shape[-1]
    w2, m2, v2, g2 = (z.reshape(-1, cols) for z in (w, m, v, g))
    rows = w2.shape[0]
    tr = _row_blk(rows, max(SUBLANES, (1 << 20) // (4 * cols)))

    def body(w_ref, m_ref, v_ref, g_ref, d_ref, nm_ref, nv_ref):
        gv = g_ref[...]
        nm = ADAM_B1 * m_ref[...] + (1.0 - ADAM_B1) * gv
        nv = ADAM_B2 * v_ref[...] + (1.0 - ADAM_B2) * jnp.square(gv)
        m_hat = nm / (1.0 - ADAM_B1 ** ADAM_STEP)
        v_hat = nv / (1.0 - ADAM_B2 ** ADAM_STEP)
        d_ref[...] = -ADAM_LR * (m_hat / (jnp.sqrt(v_hat) + ADAM_EPS) + ADAM_WD * w_ref[...])
        nm_ref[...] = nm
        nv_ref[...] = nv

    spec = pl.BlockSpec((tr, cols), lambda i: (i, 0))
    out = jax.ShapeDtypeStruct((rows, cols), F32)
    d, nm, nv = pl.pallas_call(
        body, name=name, grid=(rows // tr,), in_specs=[spec] * 4, out_specs=[spec] * 3, out_shape=[out] * 3,
        compiler_params=_params(("parallel",)),
    )(w2, m2, v2, g2)
    return d.reshape(shape), nm.reshape(shape), nv.reshape(shape)


def _pack_rows(size, cols, align):
    return -(-size // (cols * align)) * align


def _pack(arrs, lead, cols, align, total_align):
    lead_shape = arrs[0].shape[:lead]
    parts = []
    for a in arrs:
        flat = a.reshape(lead_shape + (-1,))
        size = flat.shape[-1]
        rows = _pack_rows(size, cols, align)
        flat = jnp.pad(flat, [(0, 0)] * lead + [(0, rows * cols - size)])
        parts.append(flat.reshape(lead_shape + (rows, cols)))
    total = sum(p.shape[lead] for p in parts)
    extra = -total % total_align
    if extra:
        parts.append(jnp.zeros(lead_shape + (extra, cols), parts[0].dtype))
    return jnp.concatenate(parts, axis=lead)


def _unpack(buf, shapes, lead, cols, align):
    lead_shape = buf.shape[:lead]
    out, row = [], 0
    for shp in shapes:
        size = 1
        for n in shp:
            size *= n
        rows = _pack_rows(size, cols, align)
        piece = lax.slice_in_dim(buf, row, row + rows, axis=lead).reshape(lead_shape + (-1,))
        out.append(piece[..., :size].reshape(lead_shape + tuple(shp)))
        row += rows
    return out


BIG = ("ffn_w_in", "ffn_w_out", "a_w_qkv", "a_w_o", "kv_w", "b_w_q", "b_w_o")
SMALL = ("ffn_norm", "mix_norm", "a_q_norm", "a_k_norm", "kv_norm", "kv_b_f", "kv_k_norm", "b_q_norm")
WEIGHTS = ("ffn_norm", "ffn_w_in", "ffn_w_out", "mix_norm", "a_w_qkv", "a_q_norm", "a_k_norm", "a_w_o",
           "kv_norm", "kv_w", "kv_b_f", "kv_k_norm", "b_w_q", "b_q_norm", "b_w_o")
GATE_PAD = 2 * LANES


def _stack_weights(sh, d):
    depth = sh["ffn_w_in"].shape[1]
    kv = sh["kv_w"].transpose(1, 0, 2).reshape(d, -1)
    kv = jnp.pad(kv, ((0, 0), (0, 2 * d + GATE_PAD - kv.shape[1])))
    return {
        "ffn_w_in": [[sh["ffn_w_in"][:, l, i] for i in range(2)] for l in range(depth)],
        "ffn_w_out": [[sh["ffn_w_out"][:, l, i].reshape(1, -1, d) for i in range(2)] for l in range(depth)],
        "a_w_qkv": sh["a_w_qkv"][:, 0],
        "a_w_o": sh["a_w_o"].reshape(1, d, d),
        "kv_w": kv[None],
        "b_w_q": sh["b_w_q"].reshape(1, d, d),
        "b_w_o": sh["b_w_o"].reshape(1, d, d),
    }


def _unstack_grads(dw, d, heads):
    def rows4(z):
        return z.reshape(N_CHIPS, -1, d)

    kv_cols = 2 * d + heads
    kv = dw["kv_w"][0][:, :kv_cols].reshape(d, N_CHIPS, kv_cols // N_CHIPS).transpose(1, 0, 2)
    return [
        jnp.stack([jnp.stack(row, axis=1) for row in dw["ffn_w_in"]], axis=1),
        jnp.stack([jnp.stack([rows4(z) for z in row], axis=1) for row in dw["ffn_w_out"]], axis=1),
        dw["a_w_qkv"][:, None],
        rows4(dw["a_w_o"])[:, None],
        kv,
        rows4(dw["b_w_q"])[:, None],
        rows4(dw["b_w_o"])[:, None],
    ]


def kernel(x, positions, ffn_norm, ffn_w_in, ffn_w_out, mix_norm, a_w_qkv, a_q_norm, a_k_norm, a_w_o, kv_norm, kv_w, kv_b_f, kv_k_norm, b_w_q, b_q_norm, b_w_o, loss_target, m_ffn_norm, m_ffn_w_in, m_ffn_w_out, m_mix_norm, m_a_w_qkv, m_a_q_norm, m_a_k_norm, m_a_w_o, m_kv_norm, m_kv_w, m_kv_b_f, m_kv_k_norm, m_b_w_q, m_b_q_norm, m_b_w_o, v_ffn_norm, v_ffn_w_in, v_ffn_w_out, v_mix_norm, v_a_w_qkv, v_a_q_norm, v_a_k_norm, v_a_w_o, v_kv_norm, v_kv_w, v_kv_b_f, v_kv_k_norm, v_b_w_q, v_b_q_norm, v_b_w_o):
    wts = dict(zip(WEIGHTS, (ffn_norm, ffn_w_in, ffn_w_out, mix_norm, a_w_qkv, a_q_norm, a_k_norm, a_w_o, kv_norm,
                             kv_w, kv_b_f, kv_k_norm, b_w_q, b_q_norm, b_w_o)))
    mom = dict(zip(WEIGHTS, (m_ffn_norm, m_ffn_w_in, m_ffn_w_out, m_mix_norm, m_a_w_qkv, m_a_q_norm, m_a_k_norm,
                             m_a_w_o, m_kv_norm, m_kv_w, m_kv_b_f, m_kv_k_norm, m_b_w_q, m_b_q_norm, m_b_w_o)))
    var = dict(zip(WEIGHTS, (v_ffn_norm, v_ffn_w_in, v_ffn_w_out, v_mix_norm, v_a_w_qkv, v_a_q_norm, v_a_k_norm,
                             v_a_w_o, v_kv_norm, v_kv_w, v_kv_b_f, v_kv_k_norm, v_b_w_q, v_b_q_norm, v_b_w_o)))
    batch, seq, d = x.shape
    cfg = Cfg(d_model=d, d_ff=ffn_w_out.shape[2] * N_CHIPS, seq=seq, batch=batch)
    chip = 2 * lax.axis_index("x") + lax.axis_index("y")
    big_shapes = [wts[n].shape for n in BIG]

    shard = _pack([wts[n].astype(BF16) for n in BIG], 0, PACK_COLS, PACK_ROW_ALIGN, PACK_COLS)
    gathered = _all_gather_chips(shard, name="gather_weights")
    gathered = lax.dynamic_update_slice_in_dim(gathered, shard[None], chip, axis=0)
    w = _stack_weights(dict(zip(BIG, _unpack(gathered, big_shapes, 1, PACK_COLS, PACK_ROW_ALIGN))), d)
    norm_shard = _pack([ffn_norm], 0, LANES, SUBLANES, SUBLANES)
    norms = _all_gather_small(norm_shard, name="gather_ffn_norm")[0::2]
    (norms,) = _unpack(norms, [ffn_norm.shape], 1, LANES, SUBLANES)
    small = {"ffn_norm": jnp.moveaxis(norms, 0, 2).reshape(ffn_norm.shape[:2] + (d,)),
             "mix_norm": mix_norm, "a_q_norm": a_q_norm[0], "a_k_norm": a_k_norm[0], "kv_norm": kv_norm,
             "kv_b_f": kv_b_f, "kv_k_norm": kv_k_norm, "b_q_norm": b_q_norm}

    loss, dx, dw, ds = _local_step(cfg, x.reshape(cfg.tokens, d), positions.reshape(cfg.tokens),
                                   loss_target.reshape(cfg.tokens, d), w, small)
    loss = lax.psum(loss, ("x", "y", "c"))

    g = _pack(_unstack_grads(dw, d, cfg.heads), 1, PACK_COLS, PACK_ROW_ALIGN, PACK_COLS)
    chip_half = _add_own_half(g, _swap_halves(g, name="swap_halves"), name="add_halves")
    parts = _scatter_chips(chip_half, name="scatter_chips")
    parts = lax.dynamic_update_slice_in_dim(parts, lax.dynamic_slice_in_dim(chip_half, chip, 1, axis=0), chip, axis=0)
    mine = _sum_parts(parts, name="sum_chips")
    both = _join_halves(mine, name="join_halves")
    g_big = lax.dynamic_update_slice_in_dim(both, mine[None], lax.axis_index("c"), axis=0).reshape(g.shape[1:])
    grads = dict(zip(BIG, _unpack(g_big, big_shapes, 0, PACK_COLS, PACK_ROW_ALIGN)))

    small_shapes = [ds[n].shape for n in SMALL]
    parts = _all_gather_small(_pack([ds[n] for n in SMALL], 0, LANES, SUBLANES, SUBLANES), name="gather_small")
    g_small = dict(zip(SMALL, _unpack(_sum_parts(parts, name="sum_small"), small_shapes, 0, LANES, SUBLANES)))
    quarter = d // N_CHIPS
    g_small["ffn_norm"] = lax.dynamic_slice_in_dim(g_small["ffn_norm"], chip * quarter, quarter, axis=2)
    grads.update(g_small)

    delta, new_m, new_v = {}, {}, {}
    for n in BIG:
        delta[n], new_m[n], new_v[n] = _adamw(wts[n], mom[n], var[n], grads[n], name=f"adamw_{n}")
    packed = [_pack([z[n] for n in SMALL], 0, LANES, SUBLANES, SUBLANES) for z in (wts, mom, var, grads)]
    small_out = _adamw(*packed, name="adamw_small")
    shard_shapes = [wts[n].shape for n in SMALL]
    for out, res in zip((delta, new_m, new_v), small_out):
        out.update(zip(SMALL, _unpack(res, shard_shapes, 0, LANES, SUBLANES)))

    return (loss, dx.reshape(x.shape), *[grads[n] for n in WEIGHTS], *[delta[n] for n in WEIGHTS],
            *[new_m[n] for n in WEIGHTS], *[new_v[n] for n in WEIGHTS])
```

```python
import functools
from typing import NamedTuple

import jax
import jax.numpy as jnp
from jax import lax
from jax.experimental import pallas as pl
from jax.experimental.pallas import tpu as pltpu

F32 = jnp.float32
BF16 = jnp.bfloat16

HEAD_DIM = 64
LANES = 128
SUBLANES = 8
ROT_DIM = HEAD_DIM // 4
ROPE_THETA = 500000.0
NORM_EPS = 1e-6
BAND = 128
DILATIONS = (1, 4, 16)
NEG = -1e30
Q_SCALE = HEAD_DIM ** -0.5
N_CHIPS = 4
N_DEV = 8
VMEM_LIMIT = 48 * 1024 * 1024

ADAM_LR = 0.001
ADAM_B1 = 0.9
ADAM_B2 = 0.999
ADAM_EPS = 1e-08
ADAM_WD = 0.01
ADAM_STEP = 10


class Cfg(NamedTuple):
    d_model: int
    d_ff: int
    seq: int
    batch: int

    @property
    def heads(self):
        return self.d_model // HEAD_DIM

    @property
    def tokens(self):
        return self.batch * self.seq

    @property
    def pairs(self):
        return self.d_model // LANES


def _params(sem):
    return pltpu.CompilerParams(dimension_semantics=sem, vmem_limit_bytes=VMEM_LIMIT)


def _blk(dim, want):
    if dim <= want:
        return dim
    for b in range(want // LANES * LANES, 0, -LANES):
        if dim % b == 0:
            return b
    b = want
    while dim % b:
        b //= 2
    return b


def _mm(a, b, *, form, out_dtype, name, bm=1024, bn=1024, bk=1024, res=None, scale=1.0):
    if form == "F":
        m, kdim = a.shape
        jn, _, ns = b.shape
        bm, bn, bk = _blk(m, bm), _blk(ns, bn), _blk(kdim, bk)
        npj = ns // bn
        grid = (m // bm, jn * npj, kdim // bk)
        a_spec = pl.BlockSpec((bm, bk), lambda i, n, k: (i, k))
        b_spec = pl.BlockSpec((None, bk, bn), lambda i, n, k: (n // npj, k, n % npj))
        o_spec = pl.BlockSpec((bm, bn), lambda i, n, k: (i, n))
        o_shape = jax.ShapeDtypeStruct((m, jn * ns), out_dtype)
        dims = (((1,), (0,)), ((), ()))
    elif form == "B":
        m = a.shape[0]
        jn, kdim, ns = b.shape
        bm, bn, bk = _blk(m, bm), _blk(kdim, bn), _blk(ns, bk)
        kpj = ns // bk
        grid = (m // bm, kdim // bn, jn * kpj)
        a_spec = pl.BlockSpec((bm, bk), lambda i, n, k: (i, k))
        b_spec = pl.BlockSpec((None, bn, bk), lambda i, n, k: (k // kpj, n, k % kpj))
        o_spec = pl.BlockSpec((bm, bn), lambda i, n, k: (i, n))
        o_shape = jax.ShapeDtypeStruct((m, kdim), out_dtype)
        dims = (((1,), (1,)), ((), ()))
    else:
        raise ValueError(form)
    nk = grid[2]

    def body(*refs):
        if res is None:
            a_ref, b_ref, o_ref, acc_ref = refs
            r_ref = None
        else:
            a_ref, b_ref, r_ref, o_ref, acc_ref = refs
        k = pl.program_id(2)

        @pl.when(k == 0)
        def _():
            acc_ref[...] = jnp.zeros_like(acc_ref)

        acc_ref[...] += lax.dot_general(a_ref[...].astype(BF16), b_ref[...].astype(BF16), dims,
                                        preferred_element_type=F32)

        @pl.when(k == nk - 1)
        def _():
            r = acc_ref[...]
            if scale != 1.0:
                r = r * scale
            if r_ref is not None:
                r = r_ref[...] + r
            o_ref[...] = r.astype(o_ref.dtype)

    in_specs = [a_spec, b_spec]
    args = [a, b]
    if res is not None:
        in_specs.append(pl.BlockSpec((bm, bn), lambda i, n, k: (i, n)))
        args.append(res)
    return pl.pallas_call(
        body, name=name, grid=grid, in_specs=in_specs, out_specs=o_spec, out_shape=o_shape,
        scratch_shapes=[pltpu.VMEM((bm, bn), F32)],
        compiler_params=_params(("parallel", "parallel", "arbitrary")),
    )(*args)


def _mm_grad(a, dy, jn, *, name, scale=1.0, bm=1024, bn=1024, bk=1024):
    halves = dy if isinstance(dy, (tuple, list)) else (dy,)
    t, kdim = a.shape
    ns = len(halves) * halves[0].shape[1] // jn
    bm, bn, bk = _blk(kdim, bm), _blk(ns, bn), _blk(t, bk)
    npj = ns // bn
    grid = (kdim // bm, jn * npj, t // bk)
    nk = grid[2]
    nhalf = jn * npj // len(halves)
    dims = (((0,), (0,)), ((), ()))

    def body(a_ref, *refs):
        b_refs, o_ref, acc_ref = refs[:len(halves)], refs[-2], refs[-1]
        n, k = pl.program_id(1), pl.program_id(2)

        @pl.when(k == 0)
        def _():
            acc_ref[...] = jnp.zeros_like(acc_ref)

        for which, b_ref in enumerate(b_refs):
            @pl.when(n // nhalf == which)
            def _(b_ref=b_ref):
                acc_ref[...] += lax.dot_general(a_ref[...].astype(BF16), b_ref[...].astype(BF16), dims,
                                                preferred_element_type=F32)

        @pl.when(k == nk - 1)
        def _():
            r = acc_ref[...]
            if scale != 1.0:
                r = r * scale
            o_ref[...] = r

    def half_spec(which):
        return pl.BlockSpec((bk, bn), lambda m, n, k: (jnp.where(n // nhalf == which, k, 0),
                                                        jnp.where(n // nhalf == which, n % nhalf, 0)))

    return pl.pallas_call(
        body, name=name, grid=grid,
        in_specs=[pl.BlockSpec((bk, bm), lambda m, n, k: (k, m))] + [half_spec(w) for w in range(len(halves))],
        out_specs=pl.BlockSpec((None, bm, bn), lambda m, n, k: (n // npj, m, n % npj)),
        out_shape=jax.ShapeDtypeStruct((jn, kdim, ns), F32),
        scratch_shapes=[pltpu.VMEM((bm, bn), F32)],
        compiler_params=_params(("parallel", "parallel", "arbitrary")),
    )(a, *halves)


def _mm_back2(dy_halves, w, *, name, bm=1024, bn=1024, bk=1024):
    m = dy_halves[0].shape[0]
    jn, kdim, ns = w.shape
    bm, bn, bk = _blk(m, bm), _blk(kdim, bn), _blk(ns, bk)
    kpj = ns // bk
    nk = jn * kpj
    khalf = nk // 2

    def body(a0_ref, a1_ref, b_ref, o_ref, acc_ref):
        k = pl.program_id(2)

        @pl.when(k == 0)
        def _():
            acc_ref[...] = jnp.zeros_like(acc_ref)

        for which, a_ref in enumerate((a0_ref, a1_ref)):
            @pl.when(k // khalf == which)
            def _(a_ref=a_ref):
                acc_ref[...] += lax.dot_general(a_ref[...], b_ref[...], _NT, preferred_element_type=F32)

        @pl.when(k == nk - 1)
        def _():
            o_ref[...] = acc_ref[...]

    def half_spec(which):
        return pl.BlockSpec((bm, bk), lambda i, n, k: (i, jnp.clip(k - which * khalf, 0, khalf - 1)))

    return pl.pallas_call(
        body, name=name, grid=(m // bm, kdim // bn, nk),
        in_specs=[half_spec(0), half_spec(1),
                  pl.BlockSpec((None, bn, bk), lambda i, n, k: (k // kpj, n, k % kpj))],
        out_specs=pl.BlockSpec((bm, bn), lambda i, n, k: (i, n)),
        out_shape=jax.ShapeDtypeStruct((m, kdim), F32),
        scratch_shapes=[pltpu.VMEM((bm, bn), F32)],
        compiler_params=_params(("parallel", "parallel", "arbitrary")),
    )(dy_halves[0], dy_halves[1], w)


def _ffn_in_act(n, w_in, *, name, bm=512):
    m, kdim = n.shape
    jn, _, ns = w_in.shape
    f = jn * ns // 2
    bm = _blk(m, bm)
    bn = _blk(ns, WIDE)
    npj = ns // bn
    nf = f // bn

    def body(n_ref, wg_ref, wu_ref, g_ref, u_ref, a_ref):
        nv = n_ref[...]
        g = jnp.dot(nv, wg_ref[...], preferred_element_type=F32)
        u = jnp.dot(nv, wu_ref[...], preferred_element_type=F32)
        g_ref[...] = g.astype(BF16)
        u_ref[...] = u.astype(BF16)
        a_ref[...] = (g * jax.nn.sigmoid(g) * u).astype(BF16)

    out = jax.ShapeDtypeStruct((m, f), BF16)
    ospec = pl.BlockSpec((bm, bn), lambda i, c: (i, c))
    return pl.pallas_call(
        body, name=name, grid=(m // bm, nf),
        in_specs=[pl.BlockSpec((bm, kdim), lambda i, c: (i, 0)),
                  pl.BlockSpec((None, kdim, bn), lambda i, c: (c // npj, 0, c % npj)),
                  pl.BlockSpec((None, kdim, bn), lambda i, c: ((c + nf) // npj, 0, (c + nf) % npj))],
        out_specs=[ospec, ospec, ospec], out_shape=[out, out, out],
        compiler_params=_params(("parallel", "parallel")),
    )(n, w_in, w_in)


def _ffn_out_dx_act(dh, w_out, gate, up, *, name, scale, bm=512):
    m, d = dh.shape
    f = w_out.shape[1]
    bm = _blk(m, bm)
    bn = _blk(f, WIDE)

    def body(dh_ref, w_ref, g_ref, u_ref, dg_ref, du_ref):
        da = lax.dot_general(dh_ref[...].astype(BF16), w_ref[...], _NT, preferred_element_type=F32) * scale
        g = g_ref[...].astype(F32)
        sg = jax.nn.sigmoid(g)
        silu = g * sg
        dg_ref[...] = (da * u_ref[...].astype(F32) * (sg + silu * (1.0 - sg))).astype(BF16)
        du_ref[...] = (da * silu).astype(BF16)

    out = jax.ShapeDtypeStruct((m, f), BF16)
    spec = pl.BlockSpec((bm, bn), lambda i, c: (i, c))
    return pl.pallas_call(
        body, name=name, grid=(m // bm, f // bn),
        in_specs=[pl.BlockSpec((bm, d), lambda i, c: (i, 0)), pl.BlockSpec((None, bn, d), lambda i, c: (0, c, 0)),
                  spec, spec],
        out_specs=[spec, spec], out_shape=[out, out],
        compiler_params=_params(("parallel", "parallel")),
    )(dh, w_out, gate, up)


ROW_BLOCK = 512


def _fold8(x):
    return jnp.sum(x.reshape(x.shape[0] // SUBLANES, SUBLANES, x.shape[1]), axis=0)


def _rms_fwd(x, g, *, name):
    t, d = x.shape
    tr = _blk(t, ROW_BLOCK)

    def body(x_ref, g_ref, o_ref):
        xv = x_ref[...]
        rstd = lax.rsqrt(jnp.mean(xv * xv, axis=-1, keepdims=True) + NORM_EPS)
        o_ref[...] = ((xv * rstd) * g_ref[...]).astype(BF16)

    return pl.pallas_call(
        body, name=name, grid=(t // tr,),
        in_specs=[pl.BlockSpec((tr, d), lambda i: (i, 0)), pl.BlockSpec((1, d), lambda i: (0, 0))],
        out_specs=pl.BlockSpec((tr, d), lambda i: (i, 0)),
        out_shape=jax.ShapeDtypeStruct((t, d), BF16),
        compiler_params=_params(("parallel",)),
    )(x, g.reshape(1, d))


def _rms_bwd(x, g, dy, dres, *, name):
    t, d = x.shape
    tr = _blk(t, ROW_BLOCK)

    def body(x_ref, g_ref, dy_ref, dres_ref, dx_ref, dg_ref):
        i = pl.program_id(0)
        xv = x_ref[...]
        rstd = lax.rsqrt(jnp.mean(xv * xv, axis=-1, keepdims=True) + NORM_EPS)
        xhat = xv * rstd
        dyv = dy_ref[...]
        dyg = dyv * g_ref[...]
        proj = jnp.mean(dyg * xhat, axis=-1, keepdims=True)
        dx_ref[...] = dres_ref[...] + rstd * (dyg - xhat * proj)

        @pl.when(i == 0)
        def _():
            dg_ref[...] = jnp.zeros_like(dg_ref)

        dg_ref[...] += _fold8(dyv * xhat)

    dx, dg = pl.pallas_call(
        body, name=name, grid=(t // tr,),
        in_specs=[pl.BlockSpec((tr, d), lambda i: (i, 0)), pl.BlockSpec((1, d), lambda i: (0, 0)),
                  pl.BlockSpec((tr, d), lambda i: (i, 0)), pl.BlockSpec((tr, d), lambda i: (i, 0))],
        out_specs=[pl.BlockSpec((tr, d), lambda i: (i, 0)), pl.BlockSpec((SUBLANES, d), lambda i: (0, 0))],
        out_shape=[jax.ShapeDtypeStruct((t, d), F32), jax.ShapeDtypeStruct((SUBLANES, d), F32)],
        compiler_params=_params(("arbitrary",)),
    )(x, g.reshape(1, d), dy, dres)
    return dx, jnp.sum(dg, axis=0)


def _swiglu_fwd(u, *, name):
    t, f2 = u.shape
    f = f2 // 2
    tr = _blk(t, 256)

    def body(g_ref, u_ref, o_ref):
        gv = g_ref[...]
        o_ref[...] = (gv * jax.nn.sigmoid(gv) * u_ref[...]).astype(BF16)

    return pl.pallas_call(
        body, name=name, grid=(t // tr,),
        in_specs=[pl.BlockSpec((tr, f), lambda i: (i, 0)), pl.BlockSpec((tr, f), lambda i: (i, 1))],
        out_specs=pl.BlockSpec((tr, f), lambda i: (i, 0)),
        out_shape=jax.ShapeDtypeStruct((t, f), BF16),
        compiler_params=_params(("parallel",)),
    )(u, u)


def _swiglu_bwd(u, da, *, name):
    t, f2 = u.shape
    f = f2 // 2
    tr = _blk(t, 256)

    def body(g_ref, u_ref, da_ref, o_ref):
        gv = g_ref[...]
        sg = jax.nn.sigmoid(gv)
        silu = gv * sg
        dav = da_ref[...]
        o_ref[:, :f] = (dav * u_ref[...] * (sg + silu * (1.0 - sg))).astype(BF16)
        o_ref[:, f:] = (dav * silu).astype(BF16)

    return pl.pallas_call(
        body, name=name, grid=(t // tr,),
        in_specs=[pl.BlockSpec((tr, f), lambda i: (i, 0)), pl.BlockSpec((tr, f), lambda i: (i, 1)),
                  pl.BlockSpec((tr, f), lambda i: (i, 0))],
        out_specs=pl.BlockSpec((tr, f2), lambda i: (i, 0)),
        out_shape=jax.ShapeDtypeStruct((t, f2), BF16),
        compiler_params=_params(("parallel",)),
    )(u, u, da)


def _loss_fwd_bwd(h, target, *, name):
    t, d = h.shape
    tr = _blk(t, ROW_BLOCK)

    def body(h_ref, t_ref, dh_ref, l_ref):
        i = pl.program_id(0)
        err = h_ref[...] - t_ref[...]
        dh_ref[...] = err * (1.0 / d)

        @pl.when(i == 0)
        def _():
            l_ref[...] = jnp.zeros_like(l_ref)

        l_ref[...] += _fold8(err * err)

    dh, part = pl.pallas_call(
        body, name=name, grid=(t // tr,),
        in_specs=[pl.BlockSpec((tr, d), lambda i: (i, 0)), pl.BlockSpec((tr, d), lambda i: (i, 0))],
        out_specs=[pl.BlockSpec((tr, d), lambda i: (i, 0)), pl.BlockSpec((SUBLANES, d), lambda i: (0, 0))],
        out_shape=[jax.ShapeDtypeStruct((t, d), F32), jax.ShapeDtypeStruct((SUBLANES, d), F32)],
        compiler_params=_params(("arbitrary",)),
    )(h, target)
    return jnp.sum(part) * (0.5 / d), dh


def _seg_matrix():
    r = lax.broadcasted_iota(jnp.int32, (LANES, LANES), 0) // HEAD_DIM
    c = lax.broadcasted_iota(jnp.int32, (LANES, LANES), 1) // HEAD_DIM
    return (r == c).astype(BF16)


def _head_sum(x, seg, terms=3):
    hi = x.astype(BF16)
    r1 = x - hi.astype(F32)
    mid = r1.astype(BF16)
    dot = functools.partial(jnp.dot, preferred_element_type=F32)
    if terms == 2:
        return dot(hi, seg) + dot(mid, seg)
    lo = (r1 - mid.astype(F32)).astype(BF16)
    return dot(hi, seg) + dot(mid, seg) + dot(lo, seg)


def _lane_in_head(shape):
    return lax.broadcasted_iota(jnp.int32, shape, 1) % HEAD_DIM


def _rot_partner(x):
    up = pltpu.roll(x, LANES - ROT_DIM // 2, 1)
    down = pltpu.roll(x, ROT_DIM // 2, 1)
    return jnp.where(_lane_in_head(x.shape) < ROT_DIM // 2, up, down)


def _rope_tables(positions):
    inv_freq = ROPE_THETA ** (-jnp.arange(0, ROT_DIM, 2, dtype=F32) / ROT_DIM)
    ang = positions.astype(F32)[:, None] * inv_freq
    t = ang.shape[0]
    rest = HEAD_DIM - ROT_DIM
    cos = jnp.concatenate([jnp.cos(ang), jnp.cos(ang), jnp.ones((t, rest), F32)], axis=1)
    sin = jnp.concatenate([-jnp.sin(ang), jnp.sin(ang), jnp.zeros((t, rest), F32)], axis=1)
    return jnp.tile(cos, (1, LANES // HEAD_DIM)), jnp.tile(sin, (1, LANES // HEAD_DIM))


def _kind_is(j, kinds, kind):
    hits = [j == jj for jj, k in enumerate(kinds) if k == kind]
    return functools.reduce(jnp.logical_or, hits) if hits else None


def _hn_fwd(x, gains, kinds, d, cos, sin, *, name, col0=0):
    t = x.shape[0]
    n = len(kinds)
    tr = _blk(t, ROW_BLOCK)
    seg = _seg_matrix()
    g8 = jnp.repeat(gains.astype(F32), SUBLANES, axis=0)

    def body(x_ref, g_ref, seg_ref, cos_ref, sin_ref, o_ref):
        j = pl.program_id(1)

        def normed(rope):
            for c in range(d // LANES):
                sl = slice(c * LANES, (c + 1) * LANES)
                xv = x_ref[:, sl]
                ms = _head_sum(xv * xv, seg_ref[...], terms=2) * (1.0 / HEAD_DIM)
                y = (xv * lax.rsqrt(ms + NORM_EPS)) * g_ref[0:1, sl]
                if rope:
                    y = y * cos_ref[...] + _rot_partner(y) * sin_ref[...]
                o_ref[:, sl] = y.astype(BF16)

        for kind in ("rope", "norm"):
            hit = _kind_is(j, kinds, kind)
            if hit is not None:
                pl.when(hit)(functools.partial(normed, kind == "rope"))
        hit = _kind_is(j, kinds, "cast")
        if hit is not None:
            @pl.when(hit)
            def _():
                o_ref[...] = x_ref[...].astype(BF16)

    return pl.pallas_call(
        body, name=name, grid=(t // tr, n),
        in_specs=[pl.BlockSpec((tr, d), lambda i, j: (i, col0 + j)), pl.BlockSpec((SUBLANES, d), lambda i, j: (j, 0)),
                  pl.BlockSpec((LANES, LANES), lambda i, j: (0, 0)),
                  pl.BlockSpec((tr, LANES), lambda i, j: (i, 0)), pl.BlockSpec((tr, LANES), lambda i, j: (i, 0))],
        out_specs=pl.BlockSpec((tr, d), lambda i, j: (i, j)),
        out_shape=jax.ShapeDtypeStruct((t, n * d), BF16),
        compiler_params=_params(("parallel", "parallel")),
    )(x, g8, seg, cos, sin)


def _hn_bwd(x, dys, gains, kinds, d, cos, sin, *, name, col0=0):
    t = x.shape[0]
    n = len(kinds)
    tr = _blk(t, ROW_BLOCK // 2)
    seg = _seg_matrix()
    g8 = jnp.repeat(gains.astype(F32), SUBLANES, axis=0)

    def body(x_ref, *refs):
        dy_refs = refs[:n]
        g_ref, seg_ref, cos_ref, sin_ref, dx_ref, dg_ref = refs[n:]
        j = pl.program_id(0)
        i = pl.program_id(1)

        @pl.when(i == 0)
        def _():
            dg_ref[...] = jnp.zeros_like(dg_ref)

        def normed(rope, dy_ref):
            for c in range(d // LANES):
                sl = slice(c * LANES, (c + 1) * LANES)
                xv = x_ref[:, sl]
                dyv = dy_ref[:, sl]
                if rope:
                    dyv = dyv * cos_ref[...] - _rot_partner(dyv) * sin_ref[...]
                ms = _head_sum(xv * xv, seg_ref[...], terms=2) * (1.0 / HEAD_DIM)
                rstd = lax.rsqrt(ms + NORM_EPS)
                xhat = xv * rstd
                dg_ref[:, sl] += _fold8(dyv * xhat)
                dyg = dyv * g_ref[0:1, sl]
                proj = _head_sum(dyg * xhat, seg_ref[...], terms=2) * (1.0 / HEAD_DIM)
                dx_ref[:, sl] = (rstd * (dyg - xhat * proj)).astype(BF16)

        def cast(dy_ref):
            dx_ref[...] = dy_ref[...].astype(BF16)

        for jj, kind in enumerate(kinds):
            if kind == "cast":
                pl.when(j == jj)(functools.partial(cast, dy_refs[jj]))
            else:
                pl.when(j == jj)(functools.partial(normed, kind == "rope", dy_refs[jj]))

    def dy_spec(jj):
        return pl.BlockSpec((tr, d), lambda j, i: (jnp.where(j == jj, i, 0), 0))

    dx, dg = pl.pallas_call(
        body, name=name, grid=(n, t // tr),
        in_specs=[pl.BlockSpec((tr, d), lambda j, i: (i, col0 + j))] + [dy_spec(jj) for jj in range(n)] + [
                  pl.BlockSpec((SUBLANES, d), lambda j, i: (j, 0)),
                  pl.BlockSpec((LANES, LANES), lambda j, i: (0, 0)),
                  pl.BlockSpec((tr, LANES), lambda j, i: (i, 0)), pl.BlockSpec((tr, LANES), lambda j, i: (i, 0))],
        out_specs=[pl.BlockSpec((tr, d), lambda j, i: (i, j)), pl.BlockSpec((SUBLANES, d), lambda j, i: (j, 0))],
        out_shape=[jax.ShapeDtypeStruct((t, n * d), BF16), jax.ShapeDtypeStruct((n * SUBLANES, d), F32)],
        compiler_params=_params(("arbitrary", "arbitrary")),
    )(x, *dys, g8, seg, cos, sin)
    dg = dg.reshape(n, SUBLANES, d // HEAD_DIM, HEAD_DIM).sum(axis=(1, 2))
    return dx, dg


def _head_dot(a, b, *, name):
    t, d = a.shape
    tr = _blk(t, ROW_BLOCK)
    seg = _seg_matrix()

    def body(a_ref, b_ref, seg_ref, o_ref):
        for c in range(d // LANES):
            sl = slice(c * LANES, (c + 1) * LANES)
            o_ref[:, sl] = _head_sum(a_ref[:, sl].astype(BF16).astype(F32) * b_ref[:, sl], seg_ref[...])

    return pl.pallas_call(
        body, name=name, grid=(t // tr,),
        in_specs=[pl.BlockSpec((tr, d), lambda i: (i, 0)), pl.BlockSpec((tr, d), lambda i: (i, 0)),
                  pl.BlockSpec((LANES, LANES), lambda i: (0, 0))],
        out_specs=pl.BlockSpec((tr, d), lambda i: (i, 0)),
        out_shape=jax.ShapeDtypeStruct((t, d), F32),
        compiler_params=_params(("parallel",)),
    )(a, b, seg)


def _half_mask(shape):
    return lax.broadcasted_iota(jnp.int32, shape, 1) < HEAD_DIM


def _band_valid(first):
    qi = lax.broadcasted_iota(jnp.int32, (BAND, 2 * BAND), 0)
    kj = lax.broadcasted_iota(jnp.int32, (BAND, 2 * BAND), 1)
    dist = qi + BAND - kj
    return (dist >= 0) & (dist <= BAND) & ((kj >= BAND) | jnp.logical_not(first))


_NT = (((1,), (1,)), ((), ()))
_TN = (((0,), (0,)), ((), ()))


def _dot2(p, v):
    hi = p.astype(BF16)
    lo = (p - hi.astype(F32)).astype(BF16)
    return jnp.dot(hi, v, preferred_element_type=F32) + jnp.dot(lo, v, preferred_element_type=F32)


def _band_fwd(qkv, dil, cfg, *, name):
    t, d = cfg.tokens, cfg.d_model
    w = 3 * d
    rows = t // dil
    nbt = rows // BAND
    nb = cfg.seq // (dil * BAND)
    view = qkv.reshape(rows, dil * w)
    ncol = w // d

    def body(q_ref, kp_ref, kc_ref, vp_ref, vc_ref, o_ref, lse_ref):
        i = pl.program_id(1)
        valid = _band_valid(i % nb == 0)
        half = _half_mask((BAND, LANES))
        for hp in range(d // LANES):
            sl = slice(hp * LANES, (hp + 1) * LANES)
            q2 = q_ref[:, sl]
            kk = jnp.concatenate([kp_ref[:, sl], kc_ref[:, sl]], axis=0)
            vv = jnp.concatenate([vp_ref[:, sl], vc_ref[:, sl]], axis=0)
            outs, lses = [], []
            for e in range(2):
                qe = jnp.where(half == (e == 0), q2, jnp.zeros_like(q2))
                s = lax.dot_general(qe, kk, _NT, preferred_element_type=F32)
                s = jnp.where(valid, s, NEG)
                m = jnp.max(s, axis=1, keepdims=True)
                p = jnp.exp(s - m)
                l = jnp.sum(p, axis=1, keepdims=True)
                outs.append(_dot2(p * (1.0 / l), vv))
                lses.append(m + jnp.log(l))
            o_ref[:, sl] = jnp.where(half, outs[0], outs[1])
            lse_ref[:, sl] = jnp.where(half, lses[0], lses[1])

    def col(which):
        return lambda r, i: (i, r * ncol + which)

    def col_prev(which):
        return lambda r, i: (jnp.maximum(i - 1, 0), r * ncol + which)

    blk = (BAND, d)
    o, lse = pl.pallas_call(
        body, name=name, grid=(dil, nbt),
        in_specs=[pl.BlockSpec(blk, col(0)), pl.BlockSpec(blk, col_prev(1)), pl.BlockSpec(blk, col(1)),
                  pl.BlockSpec(blk, col_prev(2)), pl.BlockSpec(blk, col(2))],
        out_specs=[pl.BlockSpec(blk, lambda r, i: (i, r)), pl.BlockSpec(blk, lambda r, i: (i, r))],
        out_shape=[jax.ShapeDtypeStruct((rows, dil * d), F32), jax.ShapeDtypeStruct((rows, dil * d), F32)],
        compiler_params=_params(("parallel", "arbitrary")),
    )(view, view, view, view, view)
    return o.reshape(t, d), lse.reshape(t, d)


def _band_bwd(qkv, dmixed, lse_all, dsum, dil, cfg, *, name):
    t, d = cfg.tokens, cfg.d_model
    w = 3 * d
    rows = t // dil
    nbt = rows // BAND
    nb = cfg.seq // (dil * BAND)
    view = qkv.reshape(rows, dil * w)
    ncol = w // d
    do_v, l_v, d_v = (z.reshape(rows, dil * d) for z in (dmixed, lse_all, dsum))

    def body(q_ref, kp_ref, kc_ref, vp_ref, vc_ref, do_ref, l_ref, ds_ref, dq_ref, dk_ref, dv_ref, ck_ref, cv_ref):
        i = pl.program_id(1)

        @pl.when(i < nbt)
        def _():
            valid = _band_valid(i % nb == 0)
            half = _half_mask((BAND, LANES))
            half2 = _half_mask((2 * BAND, LANES))
            for hp in range(d // LANES):
                sl = slice(hp * LANES, (hp + 1) * LANES)
                q2 = q_ref[:, sl]
                kk = jnp.concatenate([kp_ref[:, sl], kc_ref[:, sl]], axis=0)
                vv = jnp.concatenate([vp_ref[:, sl], vc_ref[:, sl]], axis=0)
                do2 = do_ref[:, sl].astype(BF16)
                dqs, dks, dvs = [], [], []
                for e in range(2):
                    lane0 = e * HEAD_DIM
                    keep = half == (e == 0)
                    qe = jnp.where(keep, q2, jnp.zeros_like(q2))
                    doe = jnp.where(keep, do2, jnp.zeros_like(do2))
                    s = lax.dot_general(qe, kk, _NT, preferred_element_type=F32)
                    s = jnp.where(valid, s, NEG)
                    p = jnp.exp(s - l_ref[:, hp * LANES + lane0:hp * LANES + lane0 + 1])
                    dp = lax.dot_general(doe, vv, _NT, preferred_element_type=F32)
                    dsc = (p * (dp - ds_ref[:, hp * LANES + lane0:hp * LANES + lane0 + 1])).astype(BF16)
                    dqs.append(jnp.dot(dsc, kk, preferred_element_type=F32))
                    dks.append(lax.dot_general(dsc, q2, _TN, preferred_element_type=F32))
                    dvs.append(lax.dot_general(p.astype(BF16), do2, _TN, preferred_element_type=F32))
                dq_ref[:, sl] = jnp.where(half, dqs[0], dqs[1])
                dkk = jnp.where(half2, dks[0], dks[1])
                dvv = jnp.where(half2, dvs[0], dvs[1])

                @pl.when(i > 0)
                def _():
                    dk_ref[:, sl] = ck_ref[:, sl] + dkk[:BAND]
                    dv_ref[:, sl] = cv_ref[:, sl] + dvv[:BAND]

                ck_ref[:, sl] = dkk[BAND:]
                cv_ref[:, sl] = dvv[BAND:]

        @pl.when(i == nbt)
        def _():
            dk_ref[...] = ck_ref[...]
            dv_ref[...] = cv_ref[...]

    def cur(i):
        return jnp.minimum(i, nbt - 1)

    def col(which):
        return lambda r, i: (cur(i), r * ncol + which)

    def col_prev(which):
        return lambda r, i: (jnp.maximum(cur(i) - 1, 0), r * ncol + which)

    blk = (BAND, d)
    here = pl.BlockSpec(blk, lambda r, i: (cur(i), r))
    behind = pl.BlockSpec(blk, lambda r, i: (jnp.maximum(i - 1, 0), r))
    shape = jax.ShapeDtypeStruct((rows, dil * d), F32)
    dq, dk, dv = pl.pallas_call(
        body, name=name, grid=(dil, nbt + 1),
        in_specs=[pl.BlockSpec(blk, col(0)), pl.BlockSpec(blk, col_prev(1)), pl.BlockSpec(blk, col(1)),
                  pl.BlockSpec(blk, col_prev(2)), pl.BlockSpec(blk, col(2)), here, here, here],
        out_specs=[here, behind, behind],
        out_shape=[shape, shape, shape],
        scratch_shapes=[pltpu.VMEM(blk, F32), pltpu.VMEM(blk, F32)],
        compiler_params=_params(("arbitrary", "arbitrary")),
    )(view, view, view, view, view, do_v, l_v, d_v)
    return dq.reshape(t, d), dk.reshape(t, d), dv.reshape(t, d)


def _band_valid_t(first):
    s = lax.broadcasted_iota(jnp.int32, (2 * BAND, BAND), 0)
    t = lax.broadcasted_iota(jnp.int32, (2 * BAND, BAND), 1)
    dist = t + BAND - s
    return (dist >= 0) & (dist <= BAND) & ((s >= BAND) | jnp.logical_not(first))


def _band_layouts(qkv, dil, cfg):
    rows = cfg.tokens // dil
    d = cfg.d_model
    return qkv.reshape(rows, dil * 3 * d), qkv.reshape(rows, dil, 3, d).transpose(1, 2, 3, 0)


def _to_classes_t(z, dil, width):
    return z.reshape(z.shape[0] // dil, dil, width).transpose(1, 2, 0)


def _from_classes_t(z):
    dil, width, rows = z.shape
    return z.transpose(2, 0, 1).reshape(rows * dil, width)


def _band_fwd_t(nat, tr, dil, cfg, *, name):
    d, hh = cfg.d_model, cfg.heads
    rows = cfg.tokens // dil
    nbt = rows // BAND
    nb = cfg.seq // (dil * BAND)

    def body(qt_ref, kp_ref, kc_ref, vtp_ref, vtc_ref, o_ref, lse_ref):
        i = pl.program_id(1)
        valid = _band_valid_t(i % nb == 0)
        upper = lax.broadcasted_iota(jnp.int32, (LANES, BAND), 0) < HEAD_DIM
        for hp in range(d // LANES):
            pair = slice(hp * LANES, (hp + 1) * LANES)
            qt2 = qt_ref[pair, :]
            kk = jnp.concatenate([kp_ref[:, pair], kc_ref[:, pair]], axis=0)
            for e in range(2):
                h = 2 * hp + e
                hrows = slice(h * HEAD_DIM, (h + 1) * HEAD_DIM)
                qte = jnp.where(upper == (e == 0), qt2, jnp.zeros_like(qt2))
                s = jnp.where(valid, jnp.dot(kk, qte, preferred_element_type=F32), NEG)
                m = jnp.max(s, axis=0, keepdims=True)
                p = jnp.exp(s - m)
                l = jnp.sum(p, axis=0, keepdims=True)
                hi = p.astype(BF16)
                lo = (p - hi.astype(F32)).astype(BF16)
                vvt = jnp.concatenate([vtp_ref[hrows, :], vtc_ref[hrows, :]], axis=1)
                o = jnp.dot(vvt, hi, preferred_element_type=F32) + jnp.dot(vvt, lo, preferred_element_type=F32)
                o_ref[hrows, :] = o * (1.0 / l)
                lse_ref[h:h + 1, :] = m + jnp.log(l)

    def prev(i):
        return jnp.maximum(i - 1, 0)

    tblk = (None, None, d, BAND)
    return pl.pallas_call(
        body, name=name, grid=(dil, nbt),
        in_specs=[pl.BlockSpec(tblk, lambda r, i: (r, 0, 0, i)),
                  pl.BlockSpec((BAND, d), lambda r, i: (prev(i), r * 3 + 1)),
                  pl.BlockSpec((BAND, d), lambda r, i: (i, r * 3 + 1)),
                  pl.BlockSpec(tblk, lambda r, i: (r, 2, 0, prev(i))),
                  pl.BlockSpec(tblk, lambda r, i: (r, 2, 0, i))],
        out_specs=[pl.BlockSpec((None, d, BAND), lambda r, i: (r, 0, i)),
                   pl.BlockSpec((None, hh, BAND), lambda r, i: (r, 0, i))],
        out_shape=[jax.ShapeDtypeStruct((dil, d, rows), F32), jax.ShapeDtypeStruct((dil, hh, rows), F32)],
        compiler_params=_params(("parallel", "arbitrary")),
    )(tr, nat, nat, tr, tr)


def _band_bwd_t(nat, tr, do_t, do_nat, lse_c, dsum_c, dil, cfg, *, name):
    d, hh = cfg.d_model, cfg.heads
    rows = cfg.tokens // dil
    nbt = rows // BAND
    nb = cfg.seq // (dil * BAND)

    def body(qt_ref, qn_ref, kp_ref, kc_ref, ktp_ref, ktc_ref, vp_ref, vc_ref, dot_ref, don_ref, l_ref, ds_ref,
             dq_ref, dk_ref, dv_ref, ck_ref, cv_ref):
        i = pl.program_id(1)

        @pl.when(i < nbt)
        def _():
            valid = _band_valid_t(i % nb == 0)
            upper = lax.broadcasted_iota(jnp.int32, (LANES, BAND), 0) < HEAD_DIM
            half2 = _half_mask((2 * BAND, LANES))
            for hp in range(d // LANES):
                pair = slice(hp * LANES, (hp + 1) * LANES)
                qt2, dot2 = qt_ref[pair, :], dot_ref[pair, :]
                qn2, don2 = qn_ref[:, pair], don_ref[:, pair]
                kk = jnp.concatenate([kp_ref[:, pair], kc_ref[:, pair]], axis=0)
                vv = jnp.concatenate([vp_ref[:, pair], vc_ref[:, pair]], axis=0)
                dks, dvs = [], []
                for e in range(2):
                    h = 2 * hp + e
                    hrows = slice(h * HEAD_DIM, (h + 1) * HEAD_DIM)
                    keep = upper == (e == 0)
                    qte = jnp.where(keep, qt2, jnp.zeros_like(qt2))
                    dote = jnp.where(keep, dot2, jnp.zeros_like(dot2))
                    s = jnp.where(valid, jnp.dot(kk, qte, preferred_element_type=F32), NEG)
                    p = jnp.exp(s - l_ref[h:h + 1, :])
                    dp = jnp.dot(vv, dote, preferred_element_type=F32)
                    dsb = (p * (dp - ds_ref[h:h + 1, :])).astype(BF16)
                    kkt = jnp.concatenate([ktp_ref[hrows, :], ktc_ref[hrows, :]], axis=1)
                    dq_ref[hrows, :] = jnp.dot(kkt, dsb, preferred_element_type=F32)
                    dks.append(jnp.dot(dsb, qn2, preferred_element_type=F32))
                    dvs.append(jnp.dot(p.astype(BF16), don2, preferred_element_type=F32))
                dkk = jnp.where(half2, dks[0], dks[1])
                dvv = jnp.where(half2, dvs[0], dvs[1])

                @pl.when(i > 0)
                def _():
                    dk_ref[:, pair] = ck_ref[:, pair] + dkk[:BAND]
                    dv_ref[:, pair] = cv_ref[:, pair] + dvv[:BAND]

                ck_ref[:, pair] = dkk[BAND:]
                cv_ref[:, pair] = dvv[BAND:]

        @pl.when(i == nbt)
        def _():
            dk_ref[...] = ck_ref[...]
            dv_ref[...] = cv_ref[...]

    def cur(i):
        return jnp.minimum(i, nbt - 1)

    def prev(i):
        return jnp.maximum(cur(i) - 1, 0)

    tblk = (None, None, d, BAND)
    cblk = (None, hh, BAND)
    blk = (BAND, d)
    behind = pl.BlockSpec(blk, lambda r, i: (jnp.maximum(i - 1, 0), r))
    shape = jax.ShapeDtypeStruct((rows, dil * d), F32)
    return pl.pallas_call(
        body, name=name, grid=(dil, nbt + 1),
        in_specs=[pl.BlockSpec(tblk, lambda r, i: (r, 0, 0, cur(i))),
                  pl.BlockSpec(blk, lambda r, i: (cur(i), r * 3)),
                  pl.BlockSpec(blk, lambda r, i: (prev(i), r * 3 + 1)),
                  pl.BlockSpec(blk, lambda r, i: (cur(i), r * 3 + 1)),
                  pl.BlockSpec(tblk, lambda r, i: (r, 1, 0, prev(i))),
                  pl.BlockSpec(tblk, lambda r, i: (r, 1, 0, cur(i))),
                  pl.BlockSpec(blk, lambda r, i: (prev(i), r * 3 + 2)),
                  pl.BlockSpec(blk, lambda r, i: (cur(i), r * 3 + 2)),
                  pl.BlockSpec((None, d, BAND), lambda r, i: (r, 0, cur(i))),
                  pl.BlockSpec(blk, lambda r, i: (cur(i), r)),
                  pl.BlockSpec(cblk, lambda r, i: (r, 0, cur(i))),
                  pl.BlockSpec(cblk, lambda r, i: (r, 0, cur(i)))],
        out_specs=[pl.BlockSpec((None, d, BAND), lambda r, i: (r, 0, cur(i))), behind, behind],
        out_shape=[jax.ShapeDtypeStruct((dil, d, rows), F32), shape, shape],
        scratch_shapes=[pltpu.VMEM(blk, F32), pltpu.VMEM(blk, F32)],
        compiler_params=_params(("arbitrary", "arbitrary")),
    )(tr, nat, nat, nat, tr, tr, nat, nat, do_t, do_nat, lse_c, dsum_c)


def _band_fwd_n(nat, dil, cfg, *, name):
    d, hh = cfg.d_model, cfg.heads
    rows = cfg.tokens // dil
    nbt = rows // BAND
    nb = cfg.seq // (dil * BAND)

    def body(q_ref, kp_ref, kc_ref, vp_ref, vc_ref, o_ref, lse_ref):
        i = pl.program_id(1)
        valid = _band_valid_t(i % nb == 0)
        upper = lax.broadcasted_iota(jnp.int32, (LANES, BAND), 0) < HEAD_DIM
        for hp in range(d // LANES):
            pair = slice(hp * LANES, (hp + 1) * LANES)
            qt2 = q_ref[:, pair].T
            kk = jnp.concatenate([kp_ref[:, pair], kc_ref[:, pair]], axis=0)
            vvt = jnp.concatenate([vp_ref[:, pair], vc_ref[:, pair]], axis=0).T
            outs = []
            for e in range(2):
                h = 2 * hp + e
                qte = jnp.where(upper == (e == 0), qt2, jnp.zeros_like(qt2))
                s = jnp.where(valid, jnp.dot(kk, qte, preferred_element_type=F32), NEG)
                m = jnp.max(s, axis=0, keepdims=True)
                p = jnp.exp(s - m)
                l = jnp.sum(p, axis=0, keepdims=True)
                hi = p.astype(BF16)
                lo = (p - hi.astype(F32)).astype(BF16)
                vt = vvt[e * HEAD_DIM:(e + 1) * HEAD_DIM]
                o = jnp.dot(vt, hi, preferred_element_type=F32) + jnp.dot(vt, lo, preferred_element_type=F32)
                outs.append(o * (1.0 / l))
                lse_ref[h:h + 1, :] = m + jnp.log(l)
            o_ref[:, pair] = jnp.concatenate(outs, axis=0).T

    def prev(i):
        return jnp.maximum(i - 1, 0)

    blk = (BAND, d)
    return pl.pallas_call(
        body, name=name, grid=(dil, nbt),
        in_specs=[pl.BlockSpec(blk, lambda r, i: (i, r * 3)),
                  pl.BlockSpec(blk, lambda r, i: (prev(i), r * 3 + 1)),
                  pl.BlockSpec(blk, lambda r, i: (i, r * 3 + 1)),
                  pl.BlockSpec(blk, lambda r, i: (prev(i), r * 3 + 2)),
                  pl.BlockSpec(blk, lambda r, i: (i, r * 3 + 2))],
        out_specs=[pl.BlockSpec(blk, lambda r, i: (i, r)),
                   pl.BlockSpec((None, hh, BAND), lambda r, i: (r, 0, i))],
        out_shape=[jax.ShapeDtypeStruct((rows, dil * d), F32), jax.ShapeDtypeStruct((dil, hh, rows), F32)],
        compiler_params=_params(("parallel", "arbitrary")),
    )(nat, nat, nat, nat, nat)


def _band_bwd_n(nat, do_nat, lse_c, dsum_c, dil, cfg, *, name):
    d, hh = cfg.d_model, cfg.heads
    rows = cfg.tokens // dil
    nbt = rows // BAND
    nb = cfg.seq // (dil * BAND)

    def body(q_ref, kp_ref, kc_ref, vp_ref, vc_ref, do_ref, l_ref, ds_ref, dq_ref, dk_ref, dv_ref, ck_ref, cv_ref):
        i = pl.program_id(1)

        @pl.when(i < nbt)
        def _():
            valid1 = _band_valid_t(i % nb == 0)
            valid = jnp.concatenate([valid1, valid1], axis=1)
            upper = lax.broadcasted_iota(jnp.int32, (LANES, BAND), 0) < HEAD_DIM
            half2 = _half_mask((2 * BAND, LANES))

            def both(z):
                zero = jnp.zeros_like(z)
                return jnp.concatenate([jnp.where(upper, z, zero), jnp.where(upper, zero, z)], axis=1)

            def stack(z):
                return jnp.concatenate([z[:, :BAND], z[:, BAND:]], axis=0)

            for hp in range(d // LANES):
                pair = slice(hp * LANES, (hp + 1) * LANES)
                h0, h1 = 2 * hp, 2 * hp + 1
                qn2, don2 = q_ref[:, pair], do_ref[:, pair]
                kk = jnp.concatenate([kp_ref[:, pair], kc_ref[:, pair]], axis=0)
                vv = jnp.concatenate([vp_ref[:, pair], vc_ref[:, pair]], axis=0)
                lse2 = jnp.concatenate([l_ref[h0:h0 + 1, :], l_ref[h1:h1 + 1, :]], axis=1)
                dsum2 = jnp.concatenate([ds_ref[h0:h0 + 1, :], ds_ref[h1:h1 + 1, :]], axis=1)
                s = jnp.where(valid, jnp.dot(kk, both(qn2.T), preferred_element_type=F32), NEG)
                p = jnp.exp(s - lse2)
                dp = jnp.dot(vv, both(don2.T), preferred_element_type=F32)
                dsb = (p * (dp - dsum2)).astype(BF16)
                dq2 = jnp.dot(kk.T, dsb, preferred_element_type=F32)
                dq_ref[:, pair] = jnp.concatenate([dq2[:HEAD_DIM, :BAND], dq2[HEAD_DIM:, BAND:]], axis=0).T
                dk2 = jnp.dot(stack(dsb), qn2, preferred_element_type=F32)
                dv2 = jnp.dot(stack(p.astype(BF16)), don2, preferred_element_type=F32)
                dkk = jnp.where(half2, dk2[:2 * BAND], dk2[2 * BAND:])
                dvv = jnp.where(half2, dv2[:2 * BAND], dv2[2 * BAND:])

                @pl.when(i > 0)
                def _():
                    dk_ref[:, pair] = ck_ref[:, pair] + dkk[:BAND]
                    dv_ref[:, pair] = cv_ref[:, pair] + dvv[:BAND]

                ck_ref[:, pair] = dkk[BAND:]
                cv_ref[:, pair] = dvv[BAND:]

        @pl.when(i == nbt)
        def _():
            dk_ref[...] = ck_ref[...]
            dv_ref[...] = cv_ref[...]

    def cur(i):
        return jnp.minimum(i, nbt - 1)

    def prev(i):
        return jnp.maximum(cur(i) - 1, 0)

    cblk = (None, hh, BAND)
    blk = (BAND, d)
    here = pl.BlockSpec(blk, lambda r, i: (cur(i), r))
    behind = pl.BlockSpec(blk, lambda r, i: (jnp.maximum(i - 1, 0), r))
    shape = jax.ShapeDtypeStruct((rows, dil * d), F32)
    return pl.pallas_call(
        body, name=name, grid=(dil, nbt + 1),
        in_specs=[pl.BlockSpec(blk, lambda r, i: (cur(i), r * 3)),
                  pl.BlockSpec(blk, lambda r, i: (prev(i), r * 3 + 1)),
                  pl.BlockSpec(blk, lambda r, i: (cur(i), r * 3 + 1)),
                  pl.BlockSpec(blk, lambda r, i: (prev(i), r * 3 + 2)),
                  pl.BlockSpec(blk, lambda r, i: (cur(i), r * 3 + 2)),
                  here,
                  pl.BlockSpec(cblk, lambda r, i: (r, 0, cur(i))),
                  pl.BlockSpec(cblk, lambda r, i: (r, 0, cur(i)))],
        out_specs=[here, behind, behind],
        out_shape=[shape, shape, shape],
        scratch_shapes=[pltpu.VMEM(blk, F32), pltpu.VMEM(blk, F32)],
        compiler_params=_params(("arbitrary", "arbitrary")),
    )(nat, nat, nat, nat, nat, do_nat, lse_c, dsum_c)


def _mix_fwd(outs, lses, *, name):
    t, d = outs[0].shape
    tr = _blk(t, ROW_BLOCK)
    ng = len(outs)

    def body(*refs):
        o_refs, l_refs = refs[:ng], refs[ng:2 * ng]
        mixed_ref, lse_ref = refs[2 * ng:]
        ls = [r[...] for r in l_refs]
        m = functools.reduce(jnp.maximum, ls)
        es = [jnp.exp(l - m) for l in ls]
        tot = functools.reduce(jnp.add, es)
        inv = 1.0 / tot
        mixed_ref[...] = functools.reduce(jnp.add, [(e * inv) * r[...] for e, r in zip(es, o_refs)])
        lse_ref[...] = m + jnp.log(tot)

    spec = pl.BlockSpec((tr, d), lambda i: (i, 0))
    return pl.pallas_call(
        body, name=name, grid=(t // tr,),
        in_specs=[spec] * (2 * ng), out_specs=[spec, spec],
        out_shape=[jax.ShapeDtypeStruct((t, d), F32), jax.ShapeDtypeStruct((t, d), F32)],
        compiler_params=_params(("parallel",)),
    )(*outs, *lses)


GATE_BLOCK = 256


def _tri(n, upper):
    r = lax.broadcasted_iota(jnp.int32, (n, n), 0)
    c = lax.broadcasted_iota(jnp.int32, (n, n), 1)
    return ((c >= r) if upper else (c <= r)).astype(BF16)


def _tri_dot(tri, x):
    hi = x.astype(BF16)
    r1 = x - hi.astype(F32)
    mid = r1.astype(BF16)
    lo = (r1 - mid.astype(F32)).astype(BF16)
    dot = functools.partial(jnp.dot, preferred_element_type=F32)
    return dot(tri, hi) + dot(tri, mid) + dot(tri, lo)


def _log_sigmoid(z):
    return jnp.minimum(z, 0.0) - jnp.log(1.0 + jnp.exp(-jnp.abs(z)))


def _gate_fwd(proj, col_block, bias, cfg, *, name):
    tr = _blk(cfg.seq, GATE_BLOCK)
    nblk = cfg.seq // tr

    def body(z_ref, b_ref, tri_ref, o_ref, carry_ref):
        i = pl.program_id(1)

        @pl.when(i == 0)
        def _():
            carry_ref[...] = jnp.zeros_like(carry_ref)

        logf = _log_sigmoid(z_ref[...] + b_ref[0:1, :])
        cum = _tri_dot(tri_ref[...], logf) + carry_ref[0:1, :]
        o_ref[...] = cum
        carry_ref[...] = jnp.broadcast_to(cum[tr - 1:tr, :], carry_ref.shape)

    return pl.pallas_call(
        body, name=name, grid=(cfg.batch, nblk),
        in_specs=[pl.BlockSpec((tr, LANES), lambda b, i: (b * nblk + i, col_block)),
                  pl.BlockSpec((SUBLANES, LANES), lambda b, i: (0, 0)),
                  pl.BlockSpec((tr, tr), lambda b, i: (0, 0))],
        out_specs=pl.BlockSpec((tr, LANES), lambda b, i: (b * nblk + i, 0)),
        out_shape=jax.ShapeDtypeStruct((cfg.tokens, LANES), F32),
        scratch_shapes=[pltpu.VMEM((SUBLANES, LANES), F32)],
        compiler_params=_params(("arbitrary", "arbitrary")),
    )(proj, jnp.broadcast_to(bias, (SUBLANES, LANES)), _tri(tr, upper=False))


def _gate_bwd(proj, col_block, bias, dcum, cfg, *, name):
    tr = _blk(cfg.seq, GATE_BLOCK)
    nblk = cfg.seq // tr

    def body(z_ref, b_ref, tri_ref, dc_ref, dz_ref, db_ref, carry_ref):
        b = pl.program_id(0)
        i = pl.program_id(1)

        @pl.when(i == 0)
        def _():
            carry_ref[...] = jnp.zeros_like(carry_ref)

        @pl.when((i == 0) & (b == 0))
        def _():
            db_ref[...] = jnp.zeros_like(db_ref)

        dcv = dc_ref[...]
        dlogf = _tri_dot(tri_ref[...], dcv) + carry_ref[0:1, :]
        carry_ref[...] = jnp.broadcast_to(dlogf[0:1, :], carry_ref.shape)
        dz = dlogf * jax.nn.sigmoid(-(z_ref[...] + b_ref[0:1, :]))
        dz_ref[...] = dz
        db_ref[...] += _fold8(dz)

    def rev(b, i):
        return (b * nblk + nblk - 1 - i, 0)

    dz, db = pl.pallas_call(
        body, name=name, grid=(cfg.batch, nblk),
        in_specs=[pl.BlockSpec((tr, LANES), lambda b, i: (b * nblk + nblk - 1 - i, col_block)),
                  pl.BlockSpec((SUBLANES, LANES), lambda b, i: (0, 0)),
                  pl.BlockSpec((tr, tr), lambda b, i: (0, 0)),
                  pl.BlockSpec((tr, LANES), rev)],
        out_specs=[pl.BlockSpec((tr, LANES), rev), pl.BlockSpec((SUBLANES, LANES), lambda b, i: (0, 0))],
        out_shape=[jax.ShapeDtypeStruct((cfg.tokens, LANES), F32), jax.ShapeDtypeStruct((SUBLANES, LANES), F32)],
        scratch_shapes=[pltpu.VMEM((SUBLANES, LANES), F32)],
        compiler_params=_params(("arbitrary", "arbitrary")),
    )(proj, jnp.broadcast_to(bias, (SUBLANES, LANES)), _tri(tr, upper=True), dcum)
    return dz, jnp.sum(db, axis=0)


FOX_BLOCK = 256


def _fox_scores(q2, k2, e, half, mask, cref, ck_row):
    qe = jnp.where(half == (e == 0), q2, jnp.zeros_like(q2))
    s = lax.dot_general(qe, k2, _NT, preferred_element_type=F32)
    return jnp.where(mask, s + (cref - ck_row), NEG)


def _causal(qi, ki, tq):
    r = lax.broadcasted_iota(jnp.int32, (tq, tq), 0) + qi * tq
    c = lax.broadcasted_iota(jnp.int32, (tq, tq), 1) + ki * tq
    return r >= c


def _fox_fwd(q, kv, cum_t, cfg, *, name):
    t, d, hrows = cfg.tokens, cfg.d_model, cum_t.shape[0]
    tq = _blk(cfg.seq, FOX_BLOCK)
    nq = cfg.seq // tq

    def body(q_ref, k_ref, v_ref, cq_ref, ck_ref, o_ref, lse_ref, m_ref, l_ref, acc_ref):
        qi, ki = pl.program_id(1), pl.program_id(2)

        @pl.when(ki == 0)
        def _():
            m_ref[...] = jnp.full_like(m_ref, NEG)
            l_ref[...] = jnp.zeros_like(l_ref)
            acc_ref[...] = jnp.zeros_like(acc_ref)

        @pl.when(ki <= qi)
        def _():
            mask = _causal(qi, ki, tq)
            half = _half_mask((tq, LANES))
            for hp in range(d // LANES):
                sl = slice(hp * LANES, (hp + 1) * LANES)
                q2, k2, v2 = q_ref[:, sl], k_ref[:, sl], v_ref[:, sl]
                alphas, pvs = [], []
                for e in range(2):
                    h = 2 * hp + e
                    s = _fox_scores(q2, k2, e, half, mask, cq_ref[h:h + 1, 0:1], ck_ref[h:h + 1, :])
                    m_prev = m_ref[h]
                    m_new = jnp.maximum(m_prev, jnp.max(s, axis=1, keepdims=True))
                    alpha = jnp.exp(m_prev - m_new)
                    p = jnp.exp(s - m_new[:, 0:1])
                    l_ref[h] = alpha * l_ref[h] + jnp.sum(p, axis=1, keepdims=True)
                    m_ref[h] = m_new
                    alphas.append(alpha)
                    pvs.append(_dot2(p, v2))
                acc = acc_ref[:, sl]
                acc_ref[:, sl] = jnp.where(half, alphas[0] * acc + pvs[0], alphas[1] * acc + pvs[1])

        @pl.when(ki == qi)
        def _():
            half = _half_mask((tq, LANES))
            for hp in range(d // LANES):
                sl = slice(hp * LANES, (hp + 1) * LANES)
                h0, h1 = 2 * hp, 2 * hp + 1
                inv = jnp.where(half, 1.0 / l_ref[h0], 1.0 / l_ref[h1])
                o_ref[:, sl] = acc_ref[:, sl] * inv
                lse0 = m_ref[h0] + jnp.log(l_ref[h0]) - cq_ref[h0:h0 + 1, 0:1]
                lse1 = m_ref[h1] + jnp.log(l_ref[h1]) - cq_ref[h1:h1 + 1, 0:1]
                lse_ref[:, sl] = jnp.where(half, lse0, lse1)

    def qrow(b, qi, ki):
        return (b * nq + qi, 0)

    def krow(b, qi, ki):
        return (b * nq + jnp.minimum(ki, qi), 0)

    o, lse = pl.pallas_call(
        body, name=name, grid=(cfg.batch, nq, nq),
        in_specs=[pl.BlockSpec((tq, d), qrow),
                  pl.BlockSpec((tq, d), krow),
                  pl.BlockSpec((tq, d), lambda b, qi, ki: (b * nq + jnp.minimum(ki, qi), 1)),
                  pl.BlockSpec((hrows, tq), lambda b, qi, ki: (0, b * nq + qi)),
                  pl.BlockSpec((hrows, tq), lambda b, qi, ki: (0, b * nq + jnp.minimum(ki, qi)))],
        out_specs=[pl.BlockSpec((tq, d), qrow), pl.BlockSpec((tq, d), qrow)],
        out_shape=[jax.ShapeDtypeStruct((t, d), F32), jax.ShapeDtypeStruct((t, d), F32)],
        scratch_shapes=[pltpu.VMEM((cfg.heads, tq, LANES), F32), pltpu.VMEM((cfg.heads, tq, LANES), F32),
                        pltpu.VMEM((tq, d), F32)],
        compiler_params=_params(("parallel", "parallel", "arbitrary")),
    )(q, kv, kv, cum_t, cum_t)
    return o, lse


def _fox_bwd_q(q, kv, cum_t, do, lse, dsum, cfg, *, name):
    t, d, hrows = cfg.tokens, cfg.d_model, cum_t.shape[0]
    tq = _blk(cfg.seq, FOX_BLOCK)
    nq = cfg.seq // tq

    def body(q_ref, k_ref, v_ref, cq_ref, ck_ref, do_ref, l_ref, ds_ref, dq_ref, acc_ref):
        qi, ki = pl.program_id(1), pl.program_id(2)

        @pl.when(ki == 0)
        def _():
            acc_ref[...] = jnp.zeros_like(acc_ref)

        @pl.when(ki <= qi)
        def _():
            mask = _causal(qi, ki, tq)
            half = _half_mask((tq, LANES))
            for hp in range(d // LANES):
                sl = slice(hp * LANES, (hp + 1) * LANES)
                q2, k2, v2 = q_ref[:, sl], k_ref[:, sl], v_ref[:, sl]
                do2 = do_ref[:, sl].astype(BF16)
                dqs = []
                for e in range(2):
                    h = 2 * hp + e
                    lane0 = hp * LANES + e * HEAD_DIM
                    cref = cq_ref[h:h + 1, 0:1]
                    s = _fox_scores(q2, k2, e, half, mask, cref, ck_ref[h:h + 1, :])
                    p = jnp.exp(s - (l_ref[:, lane0:lane0 + 1] + cref))
                    doe = jnp.where(half == (e == 0), do2, jnp.zeros_like(do2))
                    dp = lax.dot_general(doe, v2, _NT, preferred_element_type=F32)
                    dsc = (p * (dp - ds_ref[:, lane0:lane0 + 1])).astype(BF16)
                    dqs.append(jnp.dot(dsc, k2, preferred_element_type=F32))
                acc_ref[:, sl] += jnp.where(half, dqs[0], dqs[1])

        @pl.when(ki == qi)
        def _():
            dq_ref[...] = acc_ref[...]

    def qrow(b, qi, ki):
        return (b * nq + qi, 0)

    return pl.pallas_call(
        body, name=name, grid=(cfg.batch, nq, nq),
        in_specs=[pl.BlockSpec((tq, d), qrow),
                  pl.BlockSpec((tq, d), lambda b, qi, ki: (b * nq + jnp.minimum(ki, qi), 0)),
                  pl.BlockSpec((tq, d), lambda b, qi, ki: (b * nq + jnp.minimum(ki, qi), 1)),
                  pl.BlockSpec((hrows, tq), lambda b, qi, ki: (0, b * nq + qi)),
                  pl.BlockSpec((hrows, tq), lambda b, qi, ki: (0, b * nq + jnp.minimum(ki, qi))),
                  pl.BlockSpec((tq, d), qrow), pl.BlockSpec((tq, d), qrow), pl.BlockSpec((tq, d), qrow)],
        out_specs=pl.BlockSpec((tq, d), qrow),
        out_shape=jax.ShapeDtypeStruct((t, d), F32),
        scratch_shapes=[pltpu.VMEM((tq, d), F32)],
        compiler_params=_params(("parallel", "parallel", "arbitrary")),
    )(q, kv, kv, cum_t, cum_t, do, lse, dsum)


def _fox_bwd_kv(q, kv, cum_t, do, lse, dsum, cfg, *, name):
    t, d, hrows = cfg.tokens, cfg.d_model, cum_t.shape[0]
    tq = _blk(cfg.seq, FOX_BLOCK)
    nq = cfg.seq // tq

    def body(q_ref, k_ref, v_ref, cq_ref, ck_ref, do_ref, l_ref, ds_ref, dk_ref, dv_ref, dc_ref,
             kacc_ref, vacc_ref, cacc_ref):
        ki, qi = pl.program_id(1), pl.program_id(2)

        @pl.when(qi == 0)
        def _():
            kacc_ref[...] = jnp.zeros_like(kacc_ref)
            vacc_ref[...] = jnp.zeros_like(vacc_ref)
            cacc_ref[...] = jnp.zeros_like(cacc_ref)

        @pl.when(qi >= ki)
        def _():
            mask = _causal(qi, ki, tq)
            half = _half_mask((tq, LANES))
            for hp in range(d // LANES):
                sl = slice(hp * LANES, (hp + 1) * LANES)
                q2, k2, v2 = q_ref[:, sl], k_ref[:, sl], v_ref[:, sl]
                do2 = do_ref[:, sl].astype(BF16)
                dks, dvs = [], []
                for e in range(2):
                    h = 2 * hp + e
                    lane0 = hp * LANES + e * HEAD_DIM
                    cref = cq_ref[h:h + 1, 0:1]
                    s = _fox_scores(q2, k2, e, half, mask, cref, ck_ref[h:h + 1, :])
                    p = jnp.exp(s - (l_ref[:, lane0:lane0 + 1] + cref))
                    doe = jnp.where(half == (e == 0), do2, jnp.zeros_like(do2))
                    dp = lax.dot_general(doe, v2, _NT, preferred_element_type=F32)
                    dsf = p * (dp - ds_ref[:, lane0:lane0 + 1])
                    cacc_ref[h:h + 1, :] -= jnp.sum(dsf, axis=0, keepdims=True)
                    dks.append(lax.dot_general(dsf.astype(BF16), q2, _TN, preferred_element_type=F32))
                    dvs.append(lax.dot_general(p.astype(BF16), do2, _TN, preferred_element_type=F32))
                kacc_ref[:, sl] += jnp.where(half, dks[0], dks[1])
                vacc_ref[:, sl] += jnp.where(half, dvs[0], dvs[1])

        @pl.when(qi == nq - 1)
        def _():
            dk_ref[...] = kacc_ref[...]
            dv_ref[...] = vacc_ref[...]
            dc_ref[...] = cacc_ref[...]

    def qrow(b, ki, qi):
        return (b * nq + jnp.maximum(qi, ki), 0)

    def krow(b, ki, qi):
        return (b * nq + ki, 0)

    return pl.pallas_call(
        body, name=name, grid=(cfg.batch, nq, nq),
        in_specs=[pl.BlockSpec((tq, d), qrow),
                  pl.BlockSpec((tq, d), krow),
                  pl.BlockSpec((tq, d), lambda b, ki, qi: (b * nq + ki, 1)),
                  pl.BlockSpec((hrows, tq), lambda b, ki, qi: (0, b * nq + jnp.maximum(qi, ki))),
                  pl.BlockSpec((hrows, tq), lambda b, ki, qi: (0, b * nq + ki)),
                  pl.BlockSpec((tq, d), qrow), pl.BlockSpec((tq, d), qrow), pl.BlockSpec((tq, d), qrow)],
        out_specs=[pl.BlockSpec((tq, d), krow), pl.BlockSpec((tq, d), krow),
                   pl.BlockSpec((hrows, tq), lambda b, ki, qi: (0, b * nq + ki))],
        out_shape=[jax.ShapeDtypeStruct((t, d), F32), jax.ShapeDtypeStruct((t, d), F32),
                   jax.ShapeDtypeStruct((hrows, t), F32)],
        scratch_shapes=[pltpu.VMEM((tq, d), F32), pltpu.VMEM((tq, d), F32), pltpu.VMEM((hrows, tq), F32)],
        compiler_params=_params(("parallel", "parallel", "arbitrary")),
    )(q, kv, kv, cum_t, cum_t, do, lse, dsum)


AUG = LANES
BIAS_TERMS = 3


def _fox_aug_q(qp, cfg):
    t, hh = cfg.tokens, cfg.heads
    q3 = qp.reshape(t, hh, HEAD_DIM)
    ones = jnp.ones((t, hh, BIAS_TERMS), BF16)
    zeros = jnp.zeros((t, hh, AUG - HEAD_DIM - BIAS_TERMS), BF16)
    return jnp.concatenate([q3, ones, zeros], axis=2).reshape(t, hh * AUG).T


def _fox_aug_k(k, cum, cfg):
    t, hh = cfg.tokens, cfg.heads
    c = -cum
    hi = lax.reduce_precision(c, 8, 7)
    mid = lax.reduce_precision(c - hi, 8, 7)
    lo = c - hi - mid
    zeros = jnp.zeros((t, hh, AUG - HEAD_DIM - BIAS_TERMS), BF16)
    parts = [k.reshape(t, hh, HEAD_DIM)] + [z.astype(BF16)[..., None] for z in (hi, mid, lo)] + [zeros]
    return jnp.concatenate(parts, axis=2).reshape(t, hh * AUG)


def _fox_aug_k_call(kv, cum, cfg, *, name):
    t, d, hh = cfg.tokens, cfg.d_model, cfg.heads
    tr = _blk(t, ROW_BLOCK)

    def body(k_ref, c_ref, o_ref):
        lane = lax.broadcasted_iota(jnp.int32, (tr, LANES), 1)
        for hp in range(hh // 2):
            k2 = k_ref[:, hp * LANES:(hp + 1) * LANES].astype(F32)
            for e in range(2):
                h = 2 * hp + e
                kh = k2 if e == 0 else pltpu.roll(k2, HEAD_DIM, 1)
                c = -c_ref[:, h:h + 1]
                hi = c.astype(BF16).astype(F32)
                mid = (c - hi).astype(BF16).astype(F32)
                lo = c - hi - mid
                bias = jnp.where(lane == HEAD_DIM, hi, jnp.where(lane == HEAD_DIM + 1, mid,
                                 jnp.where(lane == HEAD_DIM + 2, lo, 0.0)))
                o_ref[:, h * AUG:(h + 1) * AUG] = jnp.where(lane < HEAD_DIM, kh, bias).astype(BF16)

    return pl.pallas_call(
        body, name=name, grid=(t // tr,),
        in_specs=[pl.BlockSpec((tr, d), lambda i: (i, 0)), pl.BlockSpec((tr, LANES), lambda i: (i, 0))],
        out_specs=pl.BlockSpec((tr, hh * AUG), lambda i: (i, 0)),
        out_shape=jax.ShapeDtypeStruct((t, hh * AUG), BF16),
        compiler_params=_params(("parallel",)),
    )(kv, cum)


def _keys_visible(tq):
    s = lax.broadcasted_iota(jnp.int32, (tq, tq), 0)
    t = lax.broadcasted_iota(jnp.int32, (tq, tq), 1)
    return s <= t


def _fox_fwd_t(qa_t, k_aug, v_t, cfg, *, name):
    t, d, hh = cfg.tokens, cfg.d_model, cfg.heads
    tq = _blk(cfg.seq, FOX_BLOCK)
    nq = cfg.seq // tq

    def body(qa_ref, ka_ref, vt_ref, o_ref, lse_ref, m_ref, l_ref, acc_ref):
        qi, ki = pl.program_id(1), pl.program_id(2)

        @pl.when(ki == 0)
        def _():
            m_ref[...] = jnp.full_like(m_ref, NEG)
            l_ref[...] = jnp.zeros_like(l_ref)
            acc_ref[...] = jnp.zeros_like(acc_ref)

        def step(diagonal):
            for h in range(hh):
                rows = slice(h * HEAD_DIM, (h + 1) * HEAD_DIM)
                s = jnp.dot(ka_ref[:, h * AUG:(h + 1) * AUG], qa_ref[h * AUG:(h + 1) * AUG, :],
                            preferred_element_type=F32)
                if diagonal:
                    s = jnp.where(_keys_visible(tq), s, NEG)
                m_prev = m_ref[h:h + 1, :]
                m_new = jnp.maximum(m_prev, jnp.max(s, axis=0, keepdims=True))
                alpha = jnp.exp(m_prev - m_new)
                p = jnp.exp(s - m_new)
                l_ref[h:h + 1, :] = alpha * l_ref[h:h + 1, :] + jnp.sum(p, axis=0, keepdims=True)
                m_ref[h:h + 1, :] = m_new
                hi = p.astype(BF16)
                lo = (p - hi.astype(F32)).astype(BF16)
                vt = vt_ref[rows, :]
                acc_ref[rows, :] = (alpha * acc_ref[rows, :] + jnp.dot(vt, hi, preferred_element_type=F32)
                                    + jnp.dot(vt, lo, preferred_element_type=F32))

        pl.when(ki < qi)(functools.partial(step, False))
        pl.when(ki == qi)(functools.partial(step, True))

        @pl.when(ki == qi)
        def _():
            for h in range(hh):
                rows = slice(h * HEAD_DIM, (h + 1) * HEAD_DIM)
                o_ref[rows, :] = acc_ref[rows, :] * (1.0 / l_ref[h:h + 1, :])
            lse_ref[...] = m_ref[...] + jnp.log(l_ref[...])

    def qcol(b, qi, ki):
        return (0, b * nq + qi)

    return pl.pallas_call(
        body, name=name, grid=(cfg.batch, nq, nq),
        in_specs=[pl.BlockSpec((hh * AUG, tq), qcol),
                  pl.BlockSpec((tq, hh * AUG), lambda b, qi, ki: (b * nq + jnp.minimum(ki, qi), 0)),
                  pl.BlockSpec((d, tq), lambda b, qi, ki: (0, b * nq + jnp.minimum(ki, qi)))],
        out_specs=[pl.BlockSpec((d, tq), qcol), pl.BlockSpec((hh, tq), qcol)],
        out_shape=[jax.ShapeDtypeStruct((d, t), F32), jax.ShapeDtypeStruct((hh, t), F32)],
        scratch_shapes=[pltpu.VMEM((hh, tq), F32), pltpu.VMEM((hh, tq), F32), pltpu.VMEM((d, tq), F32)],
        compiler_params=_params(("parallel", "parallel", "arbitrary")),
    )(qa_t, k_aug, v_t)


def _aug_q_t(q2, e, tq):
    ones = (lax.broadcasted_iota(jnp.int32, (AUG - HEAD_DIM, tq), 0) < BIAS_TERMS).astype(q2.dtype)
    return jnp.concatenate([q2[e * HEAD_DIM:(e + 1) * HEAD_DIM], ones], axis=0)


def _fox_fwd_n(q, k_aug, v, cfg, *, name):
    t, d, hh = cfg.tokens, cfg.d_model, cfg.heads
    tq = _blk(cfg.seq, FOX_BLOCK)
    nq = cfg.seq // tq

    def body(q_ref, ka_ref, v_ref, o_ref, lse_ref, qa_ref, m_ref, l_ref, acc_ref):
        qi, ki = pl.program_id(1), pl.program_id(2)

        @pl.when(ki == 0)
        def _():
            m_ref[...] = jnp.full_like(m_ref, NEG)
            l_ref[...] = jnp.zeros_like(l_ref)
            acc_ref[...] = jnp.zeros_like(acc_ref)
            for hp in range(hh // 2):
                q2 = q_ref[:, hp * LANES:(hp + 1) * LANES].T
                for e in range(2):
                    h = 2 * hp + e
                    qa_ref[h * AUG:(h + 1) * AUG, :] = _aug_q_t(q2, e, tq)

        def step(diagonal):
            for hp in range(hh // 2):
                vt2 = v_ref[:, hp * LANES:(hp + 1) * LANES].T
                for e in range(2):
                    h = 2 * hp + e
                    rows = slice(h * HEAD_DIM, (h + 1) * HEAD_DIM)
                    s = jnp.dot(ka_ref[:, h * AUG:(h + 1) * AUG], qa_ref[h * AUG:(h + 1) * AUG, :],
                                preferred_element_type=F32)
                    if diagonal:
                        s = jnp.where(_keys_visible(tq), s, NEG)
                    m_prev = m_ref[h:h + 1, :]
                    m_new = jnp.maximum(m_prev, jnp.max(s, axis=0, keepdims=True))
                    alpha = jnp.exp(m_prev - m_new)
                    p = jnp.exp(s - m_new)
                    l_ref[h:h + 1, :] = alpha * l_ref[h:h + 1, :] + jnp.sum(p, axis=0, keepdims=True)
                    m_ref[h:h + 1, :] = m_new
                    hi = p.astype(BF16)
                    lo = (p - hi.astype(F32)).astype(BF16)
                    vt = vt2[e * HEAD_DIM:(e + 1) * HEAD_DIM]
                    acc_ref[rows, :] = (alpha * acc_ref[rows, :] + jnp.dot(vt, hi, preferred_element_type=F32)
                                        + jnp.dot(vt, lo, preferred_element_type=F32))

        pl.when(ki < qi)(functools.partial(step, False))
        pl.when(ki == qi)(functools.partial(step, True))

        @pl.when(ki == qi)
        def _():
            for hp in range(hh // 2):
                halves = [acc_ref[h * HEAD_DIM:(h + 1) * HEAD_DIM, :] * (1.0 / l_ref[h:h + 1, :])
                          for h in (2 * hp, 2 * hp + 1)]
                o_ref[:, hp * LANES:(hp + 1) * LANES] = jnp.concatenate(halves, axis=0).T
            lse_ref[...] = m_ref[...] + jnp.log(l_ref[...])

    def qrow(b, qi, ki):
        return (b * nq + qi, 0)

    def krow(b, qi, ki):
        return (b * nq + jnp.minimum(ki, qi), 0)

    return pl.pallas_call(
        body, name=name, grid=(cfg.batch, nq, nq),
        in_specs=[pl.BlockSpec((tq, d), qrow), pl.BlockSpec((tq, hh * AUG), krow), pl.BlockSpec((tq, d), krow)],
        out_specs=[pl.BlockSpec((tq, d), qrow), pl.BlockSpec((hh, tq), lambda b, qi, ki: (0, b * nq + qi))],
        out_shape=[jax.ShapeDtypeStruct((t, d), F32), jax.ShapeDtypeStruct((hh, t), F32)],
        scratch_shapes=[pltpu.VMEM((hh * AUG, tq), BF16), pltpu.VMEM((hh, tq), F32), pltpu.VMEM((hh, tq), F32),
                        pltpu.VMEM((d, tq), F32)],
        compiler_params=_params(("parallel", "parallel", "arbitrary")),
    )(q, k_aug, v)


def _head_dot_c(a, b, cfg, *, name):
    t, d, hh = cfg.tokens, cfg.d_model, cfg.heads
    tc = _blk(t, ROW_BLOCK)

    def body(a_ref, b_ref, o_ref):
        for hp in range(hh // 2):
            pair = slice(hp * LANES, (hp + 1) * LANES)
            prod = (a_ref[:, pair].astype(F32) * b_ref[:, pair]).T
            for e in range(2):
                h = 2 * hp + e
                o_ref[h:h + 1, :] = jnp.sum(prod[e * HEAD_DIM:(e + 1) * HEAD_DIM], axis=0, keepdims=True)

    return pl.pallas_call(
        body, name=name, grid=(t // tc,),
        in_specs=[pl.BlockSpec((tc, d), lambda i: (i, 0)), pl.BlockSpec((tc, d), lambda i: (i, 0))],
        out_specs=pl.BlockSpec((hh, tc), lambda i: (0, i)),
        out_shape=jax.ShapeDtypeStruct((hh, t), F32),
        compiler_params=_params(("parallel",)),
    )(a, b)


def _fox_bwd_n(q, k_aug, k_t, v, do, lse, dsum, cfg, *, name):
    t, d, hh = cfg.tokens, cfg.d_model, cfg.heads
    tq = _blk(cfg.seq, FOX_BLOCK)
    nq = cfg.seq // tq

    def body(q_ref, ka_ref, kt_ref, v_ref, do_ref, lse_ref, ds_ref, dq_hbm, dk_ref, dv_ref, dc_ref, dq_acc, sem):
        b, ki, qi = pl.program_id(0), pl.program_id(1), pl.program_id(2)
        qq = jnp.maximum(qi, ki)

        @pl.when((ki == 0) & (qi == 0))
        def _():
            dq_acc[...] = jnp.zeros_like(dq_acc)

        @pl.when(qi == 0)
        def _():
            dk_ref[...] = jnp.zeros_like(dk_ref)
            dv_ref[...] = jnp.zeros_like(dv_ref)
            dc_ref[...] = jnp.zeros_like(dc_ref)

        def step(diagonal):
            upper = lax.broadcasted_iota(jnp.int32, (LANES, tq), 0) < HEAD_DIM
            half = _half_mask((tq, LANES))
            for hp in range(hh // 2):
                pair = slice(hp * LANES, (hp + 1) * LANES)
                q2 = q_ref[:, pair].T
                don2 = do_ref[:, pair]
                dot2 = don2.T
                dvs = []
                for e in range(2):
                    h = 2 * hp + e
                    rows = slice(h * HEAD_DIM, (h + 1) * HEAD_DIM)
                    aug = slice(h * AUG, (h + 1) * AUG)
                    qa = _aug_q_t(q2, e, tq)
                    s = jnp.dot(ka_ref[:, aug], qa, preferred_element_type=F32)
                    if diagonal:
                        s = jnp.where(_keys_visible(tq), s, NEG)
                    p = jnp.exp(s - lse_ref[h:h + 1, :])
                    dote = jnp.where(upper == (e == 0), dot2, jnp.zeros_like(dot2))
                    dp = jnp.dot(v_ref[:, pair], dote, preferred_element_type=F32)
                    dsf = p * (dp - ds_ref[h:h + 1, :])
                    dc_ref[:, h:h + 1] -= jnp.sum(dsf, axis=1, keepdims=True)
                    dsc = dsf.astype(BF16)
                    dvs.append(jnp.dot(p.astype(BF16), don2, preferred_element_type=F32))
                    dk_ref[:, aug] += lax.dot_general(dsc, qa, _NT, preferred_element_type=F32)
                    dq_acc[qq, rows, :] += jnp.dot(kt_ref[rows, :], dsc, preferred_element_type=F32)
                dv_ref[:, pair] += jnp.where(half, dvs[0], dvs[1])

        pl.when(qi > ki)(functools.partial(step, False))
        pl.when(qi == ki)(functools.partial(step, True))

        @pl.when((ki == nq - 1) & (qi == nq - 1))
        def _():
            cp = pltpu.make_async_copy(dq_acc, dq_hbm.at[b], sem)
            cp.start()
            cp.wait()

    def qrow(b, ki, qi):
        return (b * nq + jnp.maximum(qi, ki), 0)

    def qcol(b, ki, qi):
        return (0, b * nq + jnp.maximum(qi, ki))

    def krow(b, ki, qi):
        return (b * nq + ki, 0)

    return pl.pallas_call(
        body, name=name, grid=(cfg.batch, nq, nq),
        in_specs=[pl.BlockSpec((tq, d), qrow),
                  pl.BlockSpec((tq, hh * AUG), krow),
                  pl.BlockSpec((d, tq), lambda b, ki, qi: (0, b * nq + ki)),
                  pl.BlockSpec((tq, d), krow),
                  pl.BlockSpec((tq, d), qrow),
                  pl.BlockSpec((hh, tq), qcol), pl.BlockSpec((hh, tq), qcol)],
        out_specs=[pl.BlockSpec(memory_space=pl.ANY), pl.BlockSpec((tq, hh * AUG), krow),
                   pl.BlockSpec((tq, d), krow), pl.BlockSpec((tq, LANES), krow)],
        out_shape=[jax.ShapeDtypeStruct((cfg.batch, nq, d, tq), F32), jax.ShapeDtypeStruct((t, hh * AUG), F32),
                   jax.ShapeDtypeStruct((t, d), F32), jax.ShapeDtypeStruct((t, LANES), F32)],
        scratch_shapes=[pltpu.VMEM((nq, d, tq), F32), pltpu.SemaphoreType.DMA],
        compiler_params=_params(("arbitrary", "arbitrary", "arbitrary")),
    )(q, k_aug, k_t, v, do, lse, dsum)


def _head_dot_t(a_t, b_t, cfg, *, name):
    t, d, hh = cfg.tokens, cfg.d_model, cfg.heads
    tc = _blk(t, 2 * ROW_BLOCK)

    def body(a_ref, b_ref, o_ref):
        for h in range(hh):
            rows = slice(h * HEAD_DIM, (h + 1) * HEAD_DIM)
            o_ref[h:h + 1, :] = jnp.sum(a_ref[rows, :].astype(F32) * b_ref[rows, :], axis=0, keepdims=True)

    return pl.pallas_call(
        body, name=name, grid=(t // tc,),
        in_specs=[pl.BlockSpec((d, tc), lambda i: (0, i)), pl.BlockSpec((d, tc), lambda i: (0, i))],
        out_specs=pl.BlockSpec((hh, tc), lambda i: (0, i)),
        out_shape=jax.ShapeDtypeStruct((hh, t), F32),
        compiler_params=_params(("parallel",)),
    )(a_t, b_t)


def _fox_bwd_t(qa_t, k_aug, k_t, v, do_t, do, lse, dsum, cfg, *, name):
    t, d, hh = cfg.tokens, cfg.d_model, cfg.heads
    tq = _blk(cfg.seq, FOX_BLOCK)
    nq = cfg.seq // tq

    def body(qa_ref, ka_ref, kt_ref, v_ref, dot_ref, do_ref, lse_ref, ds_ref, dq_hbm, dk_ref, dv_ref, dc_ref,
             dq_acc, sem):
        b, ki, qi = pl.program_id(0), pl.program_id(1), pl.program_id(2)
        qq = jnp.maximum(qi, ki)

        @pl.when((ki == 0) & (qi == 0))
        def _():
            dq_acc[...] = jnp.zeros_like(dq_acc)

        @pl.when(qi == 0)
        def _():
            dk_ref[...] = jnp.zeros_like(dk_ref)
            dv_ref[...] = jnp.zeros_like(dv_ref)
            dc_ref[...] = jnp.zeros_like(dc_ref)

        def step(diagonal):
            upper = lax.broadcasted_iota(jnp.int32, (LANES, tq), 0) < HEAD_DIM
            half = _half_mask((tq, LANES))
            for hp in range(hh // 2):
                pair = slice(hp * LANES, (hp + 1) * LANES)
                dvs = []
                for e in range(2):
                    h = 2 * hp + e
                    rows = slice(h * HEAD_DIM, (h + 1) * HEAD_DIM)
                    aug = slice(h * AUG, (h + 1) * AUG)
                    s = jnp.dot(ka_ref[:, aug], qa_ref[aug, :], preferred_element_type=F32)
                    if diagonal:
                        s = jnp.where(_keys_visible(tq), s, NEG)
                    p = jnp.exp(s - lse_ref[h:h + 1, :])
                    dot2 = dot_ref[pair, :]
                    dote = jnp.where(upper == (e == 0), dot2, jnp.zeros_like(dot2))
                    dp = jnp.dot(v_ref[:, pair], dote, preferred_element_type=F32)
                    dsf = p * (dp - ds_ref[h:h + 1, :])
                    dc_ref[:, h:h + 1] -= jnp.sum(dsf, axis=1, keepdims=True)
                    dsc = dsf.astype(BF16)
                    dvs.append(jnp.dot(p.astype(BF16), do_ref[:, pair], preferred_element_type=F32))
                    dk_ref[:, aug] += lax.dot_general(dsc, qa_ref[aug, :], _NT, preferred_element_type=F32)
                    dq_acc[qq, rows, :] += jnp.dot(kt_ref[rows, :], dsc, preferred_element_type=F32)
                dv_ref[:, pair] += jnp.where(half, dvs[0], dvs[1])

        pl.when(qi > ki)(functools.partial(step, False))
        pl.when(qi == ki)(functools.partial(step, True))

        @pl.when((ki == nq - 1) & (qi == nq - 1))
        def _():
            cp = pltpu.make_async_copy(dq_acc, dq_hbm.at[b], sem)
            cp.start()
            cp.wait()

    def qcol(b, ki, qi):
        return (0, b * nq + jnp.maximum(qi, ki))

    def krow(b, ki, qi):
        return (b * nq + ki, 0)

    return pl.pallas_call(
        body, name=name, grid=(cfg.batch, nq, nq),
        in_specs=[pl.BlockSpec((hh * AUG, tq), qcol),
                  pl.BlockSpec((tq, hh * AUG), krow),
                  pl.BlockSpec((d, tq), lambda b, ki, qi: (0, b * nq + ki)),
                  pl.BlockSpec((tq, d), krow),
                  pl.BlockSpec((d, tq), qcol),
                  pl.BlockSpec((tq, d), lambda b, ki, qi: (b * nq + jnp.maximum(qi, ki), 0)),
                  pl.BlockSpec((hh, tq), qcol), pl.BlockSpec((hh, tq), qcol)],
        out_specs=[pl.BlockSpec(memory_space=pl.ANY), pl.BlockSpec((tq, hh * AUG), krow),
                   pl.BlockSpec((tq, d), krow), pl.BlockSpec((tq, LANES), krow)],
        out_shape=[jax.ShapeDtypeStruct((cfg.batch, nq, d, tq), F32), jax.ShapeDtypeStruct((t, hh * AUG), F32),
                   jax.ShapeDtypeStruct((t, d), F32), jax.ShapeDtypeStruct((t, LANES), F32)],
        scratch_shapes=[pltpu.VMEM((nq, d, tq), F32), pltpu.SemaphoreType.DMA],
        compiler_params=_params(("arbitrary", "arbitrary", "arbitrary")),
    )(qa_t, k_aug, k_t, v, do_t, do, lse, dsum)


WIDE = 1536


def _fwd(a, w, *, name, res=None, scale=1.0):
    return _mm(a, w, form="F", out_dtype=F32, name=name, bn=WIDE, bk=WIDE, res=res, scale=scale)


def _bwd(dy, w, *, name, scale=1.0):
    return _mm(dy, w, form="B", out_dtype=F32, name=name, bn=WIDE, bk=WIDE, scale=scale)


def _wgrad(a, dy, w, *, name, scale=1.0):
    return _mm_grad(a, dy, w.shape[0], name=name, bm=WIDE, bn=WIDE, scale=scale)


def _ffn_fwd(h, g, w_in, w_out, tag):
    n = _rms_fwd(h, g, name=f"{tag}_norm")
    gate, up, a = _ffn_in_act(n, w_in, name=f"{tag}_in")
    return _fwd(a, w_out, name=f"{tag}_out", res=h, scale=0.5), (n, gate, up, a)


def _ffn_bwd(dh_out, h, g, w_in, w_out, saved, tag):
    n, gate, up, a = saved
    du = _ffn_out_dx_act(dh_out, w_out, gate, up, name=f"{tag}_out_dx", scale=0.5)
    dw_out = _wgrad(a, dh_out, w_out, name=f"{tag}_out_dw", scale=0.5)
    dn = _mm_back2(du, w_in, name=f"{tag}_in_dx", bn=WIDE, bk=WIDE)
    dw_in = _mm_grad(n, du, w_in.shape[0], name=f"{tag}_in_dw", bm=WIDE, bn=WIDE)
    dh, dg = _rms_bwd(h, g, dn, dh_out, name=f"{tag}_norm_bwd")
    return dh, dg, dw_in, dw_out


def _head_gain(g, heads, scale=1.0):
    return jnp.tile(g.astype(F32) * scale, heads)


def _local_step(cfg, x, positions, target, w, s):
    d, hh = cfg.d_model, cfg.heads
    cos, sin = _rope_tables(positions)
    ones = jnp.ones((d,), F32)

    h1, ffn0 = _ffn_fwd(x, s["ffn_norm"][0, 0], w["ffn_w_in"][0][0], w["ffn_w_out"][0][0], "ffn00")
    hn_a = _rms_fwd(h1, s["mix_norm"][0], name="a_norm")
    qkv = _fwd(hn_a, w["a_w_qkv"], name="a_qkv")
    kinds_a = ["rope", "rope", "cast"] * len(DILATIONS)
    gains_a = jnp.stack([z for g in range(len(DILATIONS)) for z in (
        _head_gain(s["a_q_norm"][g], hh, Q_SCALE), _head_gain(s["a_k_norm"][g], hh), ones)])
    qkvp = [_hn_fwd(qkv, gains_a[3 * g:3 * g + 3], kinds_a[:3], d, cos, sin, name=f"a_qk_norm{g}", col0=3 * g)
            for g in range(len(DILATIONS))]
    lay = [qkvp[g].reshape(cfg.tokens // dil, dil * 3 * d) for g, dil in enumerate(DILATIONS)]
    band = [_band_fwd_n(lay[g], dil, cfg, name=f"a_band{g}") for g, dil in enumerate(DILATIONS)]
    mixed, lse_a = _mix_fwd([o.reshape(cfg.tokens, d) for o, _ in band],
                            [jnp.repeat(_from_classes_t(l), HEAD_DIM, axis=1) for _, l in band], name="a_mix")
    h2 = _fwd(mixed, w["a_w_o"], name="a_out", res=h1)
    h3, ffn1 = _ffn_fwd(h2, s["ffn_norm"][0, 1], w["ffn_w_in"][0][1], w["ffn_w_out"][0][1], "ffn01")

    kn = _rms_fwd(h3, s["kv_norm"], name="kv_norm")
    proj = _fwd(kn, w["kv_w"], name="kv_proj")
    kinds_kv = ["norm", "cast"]
    gains_kv = jnp.stack([_head_gain(s["kv_k_norm"], hh), ones])
    kvp = _hn_fwd(proj, gains_kv, kinds_kv, d, cos, sin, name="kv_k_norm")
    gate_col = 2 * d // LANES
    bias = jnp.pad(s["kv_b_f"].astype(F32), (0, LANES - hh))
    cum = _gate_fwd(proj, gate_col, bias, cfg, name="kv_gate")
    k_b, v_b = kvp[:, :d], kvp[:, d:]
    k_aug = _fox_aug_k_call(kvp, cum, cfg, name="kv_aug")

    h4, ffn2 = _ffn_fwd(h3, s["ffn_norm"][1, 0], w["ffn_w_in"][1][0], w["ffn_w_out"][1][0], "ffn10")
    hn_b = _rms_fwd(h4, s["mix_norm"][1], name="b_norm")
    qraw = _fwd(hn_b, w["b_w_q"], name="b_q")
    gains_b = _head_gain(s["b_q_norm"][0], hh, Q_SCALE)[None]
    qp = _hn_fwd(qraw, gains_b, ["norm"], d, cos, sin, name="b_q_norm")
    o_b, lse_b = _fox_fwd_n(qp, k_aug, v_b, cfg, name="b_fox")
    h5 =_fwd(o_b, w["b_w_o"], name="b_out", res=h4)
    h6, ffn3 = _ffn_fwd(h5, s["ffn_norm"][1, 1], w["ffn_w_in"][1][1], w["ffn_w_out"][1][1], "ffn11")

    loss, dh6 = _loss_fwd_bwd(h6, target, name="loss")

    dh5, dg11, dwi11, dwo11 = _ffn_bwd(dh6, h5, s["ffn_norm"][1, 1], w["ffn_w_in"][1][1], w["ffn_w_out"][1][1],
                                       ffn3, "ffn11")
    do_b = _bwd(dh5, w["b_w_o"], name="b_out_dx")
    dw_bo = _wgrad(o_b, dh5, w["b_w_o"], name="b_out_dw")
    do_bf = do_b.astype(BF16)
    dsum_b = _head_dot_c(do_bf, o_b, cfg, name="b_dsum")
    dq4, dk_aug, dv_b, dcum = _fox_bwd_n(qp, k_aug, k_b.T, v_b, do_bf, lse_b, dsum_b, cfg, name="b_fox_bwd")
    dq_b = dq4.transpose(0, 1, 3, 2).reshape(cfg.tokens, d)
    dk_b = dk_aug.reshape(cfg.tokens, hh, AUG)[:, :, :HEAD_DIM].reshape(cfg.tokens, d)
    dqraw, dgq = _hn_bwd(qraw, [dq_b], gains_b, ["norm"], d, cos, sin, name="b_q_norm_bwd")
    dhn_b = _bwd(dqraw, w["b_w_q"], name="b_q_dx")
    dw_bq = _wgrad(hn_b, dqraw, w["b_w_q"], name="b_q_dw")
    dh4, dmix1 = _rms_bwd(h4, s["mix_norm"][1], dhn_b, dh5, name="b_norm_bwd")
    dh3, dg10, dwi10, dwo10 = _ffn_bwd(dh4, h3, s["ffn_norm"][1, 0], w["ffn_w_in"][1][0], w["ffn_w_out"][1][0],
                                       ffn2, "ffn10")

    dkvraw, dgk = _hn_bwd(proj, [dk_b, dv_b], gains_kv, kinds_kv, d, cos, sin,
                          name="kv_k_norm_bwd")
    dz, dbias = _gate_bwd(proj, gate_col, bias, dcum, cfg, name="kv_gate_bwd")
    pad_cols = w["kv_w"].shape[2] - 2 * d - LANES
    dproj = jnp.concatenate([dkvraw, dz.astype(BF16), jnp.zeros((cfg.tokens, pad_cols), BF16)], axis=1)
    dkn = _bwd(dproj, w["kv_w"], name="kv_proj_dx")
    dw_kv = _wgrad(kn, dproj, w["kv_w"], name="kv_proj_dw")
    dh3, dkvn = _rms_bwd(h3, s["kv_norm"], dkn, dh3, name="kv_norm_bwd")

    dh2, dg01, dwi01, dwo01 = _ffn_bwd(dh3, h2, s["ffn_norm"][0, 1], w["ffn_w_in"][0][1], w["ffn_w_out"][0][1],
                                       ffn1, "ffn01")
    dmixed = _bwd(dh2, w["a_w_o"], name="a_out_dx")
    dw_ao = _wgrad(mixed, dh2, w["a_w_o"], name="a_out_dw")
    dsum_a = _head_dot(dmixed, mixed, name="a_dsum")
    dqkvp = []
    dmixed_bf = dmixed.astype(BF16)
    lse_h, dsum_h = lse_a[:, ::HEAD_DIM], dsum_a[:, ::HEAD_DIM]
    for g, dil in enumerate(DILATIONS):
        grads = _band_bwd_n(lay[g], dmixed_bf.reshape(cfg.tokens // dil, dil * d), _to_classes_t(lse_h, dil, hh),
                            _to_classes_t(dsum_h, dil, hh), dil, cfg, name=f"a_band{g}_bwd")
        dqkvp += [z.reshape(cfg.tokens, d) for z in grads]
    dqkv, dga = _hn_bwd(qkv, dqkvp, gains_a, kinds_a, d, cos, sin, name="a_qk_norm_bwd")
    dhn_a = _bwd(dqkv, w["a_w_qkv"], name="a_qkv_dx")
    dw_qkv = _wgrad(hn_a, dqkv, w["a_w_qkv"], name="a_qkv_dw")
    dh1, dmix0 = _rms_bwd(h1, s["mix_norm"][0], dhn_a, dh2, name="a_norm_bwd")
    dx, dg00, dwi00, dwo00 = _ffn_bwd(dh1, x, s["ffn_norm"][0, 0], w["ffn_w_in"][0][0], w["ffn_w_out"][0][0],
                                      ffn0, "ffn00")

    dw = {
        "ffn_w_in": [[dwi00, dwi01], [dwi10, dwi11]],
        "ffn_w_out": [[dwo00, dwo01], [dwo10, dwo11]],
        "a_w_qkv": dw_qkv, "a_w_o": dw_ao, "kv_w": dw_kv, "b_w_q": dw_bq, "b_w_o": dw_bo,
    }
    ds = {
        "ffn_norm": jnp.stack([jnp.stack([dg00, dg01]), jnp.stack([dg10, dg11])]),
        "mix_norm": jnp.stack([dmix0, dmix1]),
        "a_q_norm": jnp.stack([dga[3 * g] for g in range(len(DILATIONS))])[None] * Q_SCALE,
        "a_k_norm": jnp.stack([dga[3 * g + 1] for g in range(len(DILATIONS))])[None],
        "kv_norm": dkvn,
        "kv_b_f": dbias[:hh],
        "kv_k_norm": dgk[0],
        "b_q_norm": dgq * Q_SCALE,
    }
    return loss, dx, dw, ds


MESH_ID = pl.DeviceIdType.MESH
ANY = pl.BlockSpec(memory_space=pl.ANY)
PACK_COLS = 1024
PACK_ROW_ALIGN = 32


def _me():
    return lax.axis_index("x"), lax.axis_index("y"), lax.axis_index("c")


def _other_chips(x, y):
    return [(1 - x, y), (x, 1 - y), (1 - x, 1 - y)]


def _all_gather_small(v, *, name):
    r = v.shape[0]

    def body(v_ref, out_ref, send_sems, recv_sems):
        x, y, c = _me()
        me = 4 * x + 2 * y + c
        out_ref[me] = v_ref[...]
        copies = []
        for k in range(1, N_DEV):
            fx, fy, fc = (k >> 2) & 1, (k >> 1) & 1, k & 1
            peer = (1 - x if fx else x, 1 - y if fy else y, 1 - c if fc else c)
            copies.append(pltpu.make_async_remote_copy(
                src_ref=v_ref, dst_ref=out_ref.at[me], send_sem=send_sems.at[k - 1], recv_sem=recv_sems.at[k - 1],
                device_id=peer, device_id_type=MESH_ID))
        for cp in copies:
            cp.start()
        for cp in copies:
            cp.wait()

    return pl.pallas_call(
        body, name=name,
        in_specs=[pl.BlockSpec(memory_space=pltpu.VMEM)], out_specs=pl.BlockSpec(memory_space=pltpu.VMEM),
        out_shape=jax.ShapeDtypeStruct((N_DEV, r, LANES), v.dtype),
        scratch_shapes=[pltpu.SemaphoreType.DMA((N_DEV - 1,)), pltpu.SemaphoreType.DMA((N_DEV - 1,))],
    )(v)


def _all_gather_chips(v, *, name):
    rh = v.shape[0] // 2

    def body(v_ref, out_ref, send_sems, recv_sems):
        x, y, c = _me()
        j = 2 * x + y
        chips = _other_chips(x, y)

        def half(chip, core):
            return out_ref.at[chip, pl.ds(core * rh, rh)]

        first = [pltpu.make_async_remote_copy(
            src_ref=v_ref.at[pl.ds(c * rh, rh)], dst_ref=half(j, c), send_sem=send_sems.at[k],
            recv_sem=recv_sems.at[k], device_id=(px, py, c), device_id_type=MESH_ID)
            for k, (px, py) in enumerate(chips)]
        for cp in first:
            cp.start()
        passed = [pltpu.make_async_remote_copy(
            src_ref=half(2 * px + py, c), dst_ref=half(2 * px + py, c), send_sem=send_sems.at[3 + k],
            recv_sem=recv_sems.at[3 + k], device_id=(x, y, 1 - c), device_id_type=MESH_ID)
            for k, (px, py) in enumerate(chips)]
        for k in range(len(chips)):
            first[k].wait_recv()
            passed[k].start()
        for k, (px, py) in enumerate(chips):
            pltpu.make_async_remote_copy(
                src_ref=half(2 * px + py, 1 - c), dst_ref=half(2 * px + py, 1 - c), send_sem=send_sems.at[3 + k],
                recv_sem=recv_sems.at[3 + k], device_id=(x, y, 1 - c), device_id_type=MESH_ID).wait_recv()
        for cp in first + passed:
            cp.wait_send()

    return pl.pallas_call(
        body, name=name, in_specs=[ANY], out_specs=ANY,
        out_shape=jax.ShapeDtypeStruct((N_CHIPS,) + v.shape, v.dtype),
        scratch_shapes=[pltpu.SemaphoreType.DMA((2 * (N_CHIPS - 1),)), pltpu.SemaphoreType.DMA((2 * (N_CHIPS - 1),))],
    )(v)


def _swap_halves(g, *, name):
    n, r, cols = g.shape
    rh = r // 2

    def body(g_ref, out_ref, send_sem, recv_sem):
        x, y, c = _me()
        cp = pltpu.make_async_remote_copy(
            src_ref=g_ref.at[:, pl.ds((1 - c) * rh, rh)], dst_ref=out_ref, send_sem=send_sem, recv_sem=recv_sem,
            device_id=(x, y, 1 - c), device_id_type=MESH_ID)
        cp.start()
        cp.wait()

    return pl.pallas_call(
        body, name=name, in_specs=[ANY], out_specs=ANY,
        out_shape=jax.ShapeDtypeStruct((n, rh, cols), g.dtype),
        scratch_shapes=[pltpu.SemaphoreType.DMA, pltpu.SemaphoreType.DMA],
    )(g)


def _scatter_chips(v, *, name):
    def body(v_ref, out_ref, send_sems, recv_sems):
        x, y, c = _me()
        j = 2 * x + y
        copies = [pltpu.make_async_remote_copy(
            src_ref=v_ref.at[2 * px + py], dst_ref=out_ref.at[j], send_sem=send_sems.at[k], recv_sem=recv_sems.at[k],
            device_id=(px, py, c), device_id_type=MESH_ID) for k, (px, py) in enumerate(_other_chips(x, y))]
        for cp in copies:
            cp.start()
        for cp in copies:
            cp.wait()

    return pl.pallas_call(
        body, name=name, in_specs=[ANY], out_specs=ANY,
        out_shape=jax.ShapeDtypeStruct(v.shape, v.dtype),
        scratch_shapes=[pltpu.SemaphoreType.DMA((N_CHIPS - 1,)), pltpu.SemaphoreType.DMA((N_CHIPS - 1,))],
    )(v)


def _join_halves(v, *, name):
    def body(v_ref, out_ref, send_sem, recv_sem):
        x, y, c = _me()
        cp = pltpu.make_async_remote_copy(
            src_ref=v_ref, dst_ref=out_ref.at[c], send_sem=send_sem, recv_sem=recv_sem,
            device_id=(x, y, 1 - c), device_id_type=MESH_ID)
        cp.start()
        cp.wait()

    return pl.pallas_call(
        body, name=name, in_specs=[ANY], out_specs=ANY,
        out_shape=jax.ShapeDtypeStruct((2,) + v.shape, v.dtype),
        scratch_shapes=[pltpu.SemaphoreType.DMA, pltpu.SemaphoreType.DMA],
    )(v)


def _row_blk(rows, want):
    for b in range(min(rows, want) // SUBLANES * SUBLANES, 0, -SUBLANES):
        if rows % b == 0:
            return b
    return rows


def _add_own_half(g, got, *, name):
    n, r, cols = g.shape
    rh = r // 2
    tr = _row_blk(rh, 512)
    nb = rh // tr

    def body(c_ref, g_ref, got_ref, o_ref):
        del c_ref
        o_ref[...] = (g_ref[...] + got_ref[...]).astype(BF16)

    grid_spec = pltpu.PrefetchScalarGridSpec(
        num_scalar_prefetch=1, grid=(n, nb),
        in_specs=[pl.BlockSpec((None, tr, cols), lambda j, i, c: (j, c[0] * nb + i, 0)),
                  pl.BlockSpec((None, tr, cols), lambda j, i, c: (j, i, 0))],
        out_specs=pl.BlockSpec((None, tr, cols), lambda j, i, c: (j, i, 0)))
    return pl.pallas_call(
        body, name=name, grid_spec=grid_spec, out_shape=jax.ShapeDtypeStruct((n, rh, cols), BF16),
        compiler_params=_params(("parallel", "parallel")),
    )(lax.axis_index("c").astype(jnp.int32).reshape(1), g, got)


def _sum_parts(parts, *, name):
    n, r, cols = parts.shape
    tr = _row_blk(r, 512)

    def body(*refs):
        o_ref = refs[n]
        acc = refs[0][...].astype(F32)
        for p_ref in refs[1:n]:
            acc = acc + p_ref[...].astype(F32)
        o_ref[...] = acc

    return pl.pallas_call(
        body, name=name, grid=(r // tr,),
        in_specs=[pl.BlockSpec((None, tr, cols), functools.partial(lambda j, i: (j, i, 0), j)) for j in range(n)],
        out_specs=pl.BlockSpec((tr, cols), lambda i: (i, 0)),
        out_shape=jax.ShapeDtypeStruct((r, cols), F32),
        compiler_params=_params(("parallel",)),
    )(*([parts] * n))


def _adamw(w, m, v, g, *, name):
    shape = w.shape
    cols = shape[-1]
    w2, m2, v2, g2 = (z.reshape(-1, cols) for z in (w, m, v, g))
    rows = w2.shape[0]
    tr = _row_blk(rows, max(SUBLANES, (1 << 20) // (4 * cols)))

    def body(w_ref, m_ref, v_ref, g_ref, d_ref, nm_ref, nv_ref):
        gv = g_ref[...]
        nm = ADAM_B1 * m_ref[...] + (1.0 - ADAM_B1) * gv
        nv = ADAM_B2 * v_ref[...] + (1.0 - ADAM_B2) * jnp.square(gv)
        m_hat = nm / (1.0 - ADAM_B1 ** ADAM_STEP)
        v_hat = nv / (1.0 - ADAM_B2 ** ADAM_STEP)
        d_ref[...] = -ADAM_LR * (m_hat / (jnp.sqrt(v_hat) + ADAM_EPS) + ADAM_WD * w_ref[...])
        nm_ref[...] = nm
        nv_ref[...] = nv

    spec = pl.BlockSpec((tr, cols), lambda i: (i, 0))
    out = jax.ShapeDtypeStruct((rows, cols), F32)
    d, nm, nv = pl.pallas_call(
        body, name=name, grid=(rows // tr,), in_specs=[spec] * 4, out_specs=[spec] * 3, out_shape=[out] * 3,
        compiler_params=_params(("parallel",)),
    )(w2, m2, v2, g2)
    return d.reshape(shape), nm.reshape(shape), nv.reshape(shape)


def _pack_rows(size, cols, align):
    return -(-size // (cols * align)) * align


def _pack(arrs, lead, cols, align, total_align):
    lead_shape = arrs[0].shape[:lead]
    parts = []
    for a in arrs:
        flat = a.reshape(lead_shape + (-1,))
        size = flat.shape[-1]
        rows = _pack_rows(size, cols, align)
        flat = jnp.pad(flat, [(0, 0)] * lead + [(0, rows * cols - size)])
        parts.append(flat.reshape(lead_shape + (rows, cols)))
    total = sum(p.shape[lead] for p in parts)
    extra = -total % total_align
    if extra:
        parts.append(jnp.zeros(lead_shape + (extra, cols), parts[0].dtype))
    return jnp.concatenate(parts, axis=lead)


def _unpack(buf, shapes, lead, cols, align):
    lead_shape = buf.shape[:lead]
    out, row = [], 0
    for shp in shapes:
        size = 1
        for n in shp:
            size *= n
        rows = _pack_rows(size, cols, align)
        piece = lax.slice_in_dim(buf, row, row + rows, axis=lead).reshape(lead_shape + (-1,))
        out.append(piece[..., :size].reshape(lead_shape + tuple(shp)))
        row += rows
    return out


BIG = ("ffn_w_in", "ffn_w_out", "a_w_qkv", "a_w_o", "kv_w", "b_w_q", "b_w_o")
SMALL = ("ffn_norm", "mix_norm", "a_q_norm", "a_k_norm", "kv_norm", "kv_b_f", "kv_k_norm", "b_q_norm")
WEIGHTS = ("ffn_norm", "ffn_w_in", "ffn_w_out", "mix_norm", "a_w_qkv", "a_q_norm", "a_k_norm", "a_w_o",
           "kv_norm", "kv_w", "kv_b_f", "kv_k_norm", "b_w_q", "b_q_norm", "b_w_o")
GATE_PAD = 2 * LANES


def _stack_weights(sh, d):
    depth = sh["ffn_w_in"].shape[1]
    kv = sh["kv_w"].transpose(1, 0, 2).reshape(d, -1)
    kv = jnp.pad(kv, ((0, 0), (0, 2 * d + GATE_PAD - kv.shape[1])))
    return {
        "ffn_w_in": [[sh["ffn_w_in"][:, l, i] for i in range(2)] for l in range(depth)],
        "ffn_w_out": [[sh["ffn_w_out"][:, l, i].reshape(1, -1, d) for i in range(2)] for l in range(depth)],
        "a_w_qkv": sh["a_w_qkv"][:, 0],
        "a_w_o": sh["a_w_o"].reshape(1, d, d),
        "kv_w": kv[None],
        "b_w_q": sh["b_w_q"].reshape(1, d, d),
        "b_w_o": sh["b_w_o"].reshape(1, d, d),
    }


def _unstack_grads(dw, d, heads):
    def rows4(z):
        return z.reshape(N_CHIPS, -1, d)

    kv_cols = 2 * d + heads
    kv = dw["kv_w"][0][:, :kv_cols].reshape(d, N_CHIPS, kv_cols // N_CHIPS).transpose(1, 0, 2)
    return [
        jnp.stack([jnp.stack(row, axis=1) for row in dw["ffn_w_in"]], axis=1),
        jnp.stack([jnp.stack([rows4(z) for z in row], axis=1) for row in dw["ffn_w_out"]], axis=1),
        dw["a_w_qkv"][:, None],
        rows4(dw["a_w_o"])[:, None],
        kv,
        rows4(dw["b_w_q"])[:, None],
        rows4(dw["b_w_o"])[:, None],
    ]


def kernel(x, positions, ffn_norm, ffn_w_in, ffn_w_out, mix_norm, a_w_qkv, a_q_norm, a_k_norm, a_w_o, kv_norm, kv_w, kv_b_f, kv_k_norm, b_w_q, b_q_norm, b_w_o, loss_target, m_ffn_norm, m_ffn_w_in, m_ffn_w_out, m_mix_norm, m_a_w_qkv, m_a_q_norm, m_a_k_norm, m_a_w_o, m_kv_norm, m_kv_w, m_kv_b_f, m_kv_k_norm, m_b_w_q, m_b_q_norm, m_b_w_o, v_ffn_norm, v_ffn_w_in, v_ffn_w_out, v_mix_norm, v_a_w_qkv, v_a_q_norm, v_a_k_norm, v_a_w_o, v_kv_norm, v_kv_w, v_kv_b_f, v_kv_k_norm, v_b_w_q, v_b_q_norm, v_b_w_o):
    wts = dict(zip(WEIGHTS, (ffn_norm, ffn_w_in, ffn_w_out, mix_norm, a_w_qkv, a_q_norm, a_k_norm, a_w_o, kv_norm,
                             kv_w, kv_b_f, kv_k_norm, b_w_q, b_q_norm, b_w_o)))
    mom = dict(zip(WEIGHTS, (m_ffn_norm, m_ffn_w_in, m_ffn_w_out, m_mix_norm, m_a_w_qkv, m_a_q_norm, m_a_k_norm,
                             m_a_w_o, m_kv_norm, m_kv_w, m_kv_b_f, m_kv_k_norm, m_b_w_q, m_b_q_norm, m_b_w_o)))
    var = dict(zip(WEIGHTS, (v_ffn_norm, v_ffn_w_in, v_ffn_w_out, v_mix_norm, v_a_w_qkv, v_a_q_norm, v_a_k_norm,
                             v_a_w_o, v_kv_norm, v_kv_w, v_kv_b_f, v_kv_k_norm, v_b_w_q, v_b_q_norm, v_b_w_o)))
    batch, seq, d = x.shape
    cfg = Cfg(d_model=d, d_ff=ffn_w_out.shape[2] * N_CHIPS, seq=seq, batch=batch)
    chip = 2 * lax.axis_index("x") + lax.axis_index("y")
    big_shapes = [wts[n].shape for n in BIG]

    shard = _pack([wts[n].astype(BF16) for n in BIG], 0, PACK_COLS, PACK_ROW_ALIGN, PACK_COLS)
    gathered = _all_gather_chips(shard, name="gather_weights")
    gathered = lax.dynamic_update_slice_in_dim(gathered, shard[None], chip, axis=0)
    w = _stack_weights(dict(zip(BIG, _unpack(gathered, big_shapes, 1, PACK_COLS, PACK_ROW_ALIGN))), d)
    norm_shard = _pack([ffn_norm], 0, LANES, SUBLANES, SUBLANES)
    norms = _all_gather_small(norm_shard, name="gather_ffn_norm")[0::2]
    (norms,) = _unpack(norms, [ffn_norm.shape], 1, LANES, SUBLANES)
    small = {"ffn_norm": jnp.moveaxis(norms, 0, 2).reshape(ffn_norm.shape[:2] + (d,)),
             "mix_norm": mix_norm, "a_q_norm": a_q_norm[0], "a_k_norm": a_k_norm[0], "kv_norm": kv_norm,
             "kv_b_f": kv_b_f, "kv_k_norm": kv_k_norm, "b_q_norm": b_q_norm}

    loss, dx, dw, ds = _local_step(cfg, x.reshape(cfg.tokens, d), positions.reshape(cfg.tokens),
                                   loss_target.reshape(cfg.tokens, d), w, small)
    loss = lax.psum(loss, ("x", "y", "c"))

    g = _pack(_unstack_grads(dw, d, cfg.heads), 1, PACK_COLS, PACK_ROW_ALIGN, PACK_COLS)
    chip_half = _add_own_half(g, _swap_halves(g, name="swap_halves"), name="add_halves")
    parts = _scatter_chips(chip_half, name="scatter_chips")
    parts = lax.dynamic_update_slice_in_dim(parts, lax.dynamic_slice_in_dim(chip_half, chip, 1, axis=0), chip, axis=0)
    mine = _sum_parts(parts, name="sum_chips")
    both = _join_halves(mine, name="join_halves")
    g_big = lax.dynamic_update_slice_in_dim(both, mine[None], lax.axis_index("c"), axis=0).reshape(g.shape[1:])
    grads = dict(zip(BIG, _unpack(g_big, big_shapes, 0, PACK_COLS, PACK_ROW_ALIGN)))

    small_shapes = [ds[n].shape for n in SMALL]
    parts = _all_gather_small(_pack([ds[n] for n in SMALL], 0, LANES, SUBLANES, SUBLANES), name="gather_small")
    g_small = dict(zip(SMALL, _unpack(_sum_parts(parts, name="sum_small"), small_shapes, 0, LANES, SUBLANES)))
    quarter = d // N_CHIPS
    g_small["ffn_norm"] = lax.dynamic_slice_in_dim(g_small["ffn_norm"], chip * quarter, quarter, axis=2)
    grads.update(g_small)

    delta, new_m, new_v = {}, {}, {}
    for n in BIG:
        delta[n], new_m[n], new_v[n] = _adamw(wts[n], mom[n], var[n], grads[n], name=f"adamw_{n}")
    packed = [_pack([z[n] for n in SMALL], 0, LANES, SUBLANES, SUBLANES) for z in (wts, mom, var, grads)]
    small_out = _adamw(*packed, name="adamw_small")
    shard_shapes = [wts[n].shape for n in SMALL]
    for out, res in zip((delta, new_m, new_v), small_out):
        out.update(zip(SMALL, _unpack(res, shard_shapes, 0, LANES, SUBLANES)))

    return (loss, dx.reshape(x.shape), *[grads[n] for n in WEIGHTS], *[delta[n] for n in WEIGHTS],
            *[new_m[n] for n in WEIGHTS], *[new_v[n] for n in WEIGHTS])
```

```python
import functools
from typing import NamedTuple

import jax
import jax.numpy as jnp
from jax import lax
from jax.experimental import pallas as pl
from jax.experimental.pallas import tpu as pltpu

F32 = jnp.float32
BF16 = jnp.bfloat16

HEAD_DIM = 64
LANES = 128
SUBLANES = 8
ROT_DIM = HEAD_DIM // 4
ROPE_THETA = 500000.0
NORM_EPS = 1e-6
BAND = 128
DILATIONS = (1, 4, 16)
NEG = -1e30
Q_SCALE = HEAD_DIM ** -0.5
N_CHIPS = 4
N_DEV = 8
VMEM_LIMIT = 48 * 1024 * 1024

ADAM_LR = 0.001
ADAM_B1 = 0.9
ADAM_B2 = 0.999
ADAM_EPS = 1e-08
ADAM_WD = 0.01
ADAM_STEP = 10


class Cfg(NamedTuple):
    d_model: int
    d_ff: int
    seq: int
    batch: int

    @property
    def heads(self):
        return self.d_model // HEAD_DIM

    @property
    def tokens(self):
        return self.batch * self.seq

    @property
    def pairs(self):
        return self.d_model // LANES


def _params(sem):
    return pltpu.CompilerParams(dimension_semantics=sem, vmem_limit_bytes=VMEM_LIMIT)


def _blk(dim, want):
    if dim <= want:
        return dim
    for b in range(want // LANES * LANES, 0, -LANES):
        if dim % b == 0:
            return b
    b = want
    while dim % b:
        b //= 2
    return b


def _mm(a, b, *, form, out_dtype, name, bm=1024, bn=1024, bk=1024, res=None, scale=1.0):
    if form == "F":
        m, kdim = a.shape
        jn, _, ns = b.shape
        bm, bn, bk = _blk(m, bm), _blk(ns, bn), _blk(kdim, bk)
        npj = ns // bn
        grid = (m // bm, jn * npj, kdim // bk)
        a_spec = pl.BlockSpec((bm, bk), lambda i, n, k: (i, k))
        b_spec = pl.BlockSpec((None, bk, bn), lambda i, n, k: (n // npj, k, n % npj))
        o_spec = pl.BlockSpec((bm, bn), lambda i, n, k: (i, n))
        o_shape = jax.ShapeDtypeStruct((m, jn * ns), out_dtype)
        dims = (((1,), (0,)), ((), ()))
    elif form == "B":
        m = a.shape[0]
        jn, kdim, ns = b.shape
        bm, bn, bk = _blk(m, bm), _blk(kdim, bn), _blk(ns, bk)
        kpj = ns // bk
        grid = (m // bm, kdim // bn, jn * kpj)
        a_spec = pl.BlockSpec((bm, bk), lambda i, n, k: (i, k))
        b_spec = pl.BlockSpec((None, bn, bk), lambda i, n, k: (k // kpj, n, k % kpj))
        o_spec = pl.BlockSpec((bm, bn), lambda i, n, k: (i, n))
        o_shape = jax.ShapeDtypeStruct((m, kdim), out_dtype)
        dims = (((1,), (1,)), ((), ()))
    else:
        raise ValueError(form)
    nk = grid[2]

    def body(*refs):
        if res is None:
            a_ref, b_ref, o_ref, acc_ref = refs
            r_ref = None
        else:
            a_ref, b_ref, r_ref, o_ref, acc_ref = refs
        k = pl.program_id(2)

        @pl.when(k == 0)
        def _():
            acc_ref[...] = jnp.zeros_like(acc_ref)

        acc_ref[...] += lax.dot_general(a_ref[...].astype(BF16), b_ref[...].astype(BF16), dims,
                                        preferred_element_type=F32)

        @pl.when(k == nk - 1)
        def _():
            r = acc_ref[...]
            if scale != 1.0:
                r = r * scale
            if r_ref is not None:
                r = r_ref[...] + r
            o_ref[...] = r.astype(o_ref.dtype)

    in_specs = [a_spec, b_spec]
    args = [a, b]
    if res is not None:
        in_specs.append(pl.BlockSpec((bm, bn), lambda i, n, k: (i, n)))
        args.append(res)
    return pl.pallas_call(
        body, name=name, grid=grid, in_specs=in_specs, out_specs=o_spec, out_shape=o_shape,
        scratch_shapes=[pltpu.VMEM((bm, bn), F32)],
        compiler_params=_params(("parallel", "parallel", "arbitrary")),
    )(*args)


def _mm_grad(a, dy, jn, *, name, scale=1.0, bm=1024, bn=1024, bk=1024):
    halves = dy if isinstance(dy, (tuple, list)) else (dy,)
    t, kdim = a.shape
    ns = len(halves) * halves[0].shape[1] // jn
    bm, bn, bk = _blk(kdim, bm), _blk(ns, bn), _blk(t, bk)
    npj = ns // bn
    grid = (kdim // bm, jn * npj, t // bk)
    nk = grid[2]
    nhalf = jn * npj // len(halves)
    dims = (((0,), (0,)), ((), ()))

    def body(a_ref, *refs):
        b_refs, o_ref, acc_ref = refs[:len(halves)], refs[-2], refs[-1]
        n, k = pl.program_id(1), pl.program_id(2)

        @pl.when(k == 0)
        def _():
            acc_ref[...] = jnp.zeros_like(acc_ref)

        for which, b_ref in enumerate(b_refs):
            @pl.when(n // nhalf == which)
            def _(b_ref=b_ref):
                acc_ref[...] += lax.dot_general(a_ref[...].astype(BF16), b_ref[...].astype(BF16), dims,
                                                preferred_element_type=F32)

        @pl.when(k == nk - 1)
        def _():
            r = acc_ref[...]
            if scale != 1.0:
                r = r * scale
            o_ref[...] = r

    def half_spec(which):
        return pl.BlockSpec((bk, bn), lambda m, n, k: (jnp.where(n // nhalf == which, k, 0),
                                                        jnp.where(n // nhalf == which, n % nhalf, 0)))

    return pl.pallas_call(
        body, name=name, grid=grid,
        in_specs=[pl.BlockSpec((bk, bm), lambda m, n, k: (k, m))] + [half_spec(w) for w in range(len(halves))],
        out_specs=pl.BlockSpec((None, bm, bn), lambda m, n, k: (n // npj, m, n % npj)),
        out_shape=jax.ShapeDtypeStruct((jn, kdim, ns), F32),
        scratch_shapes=[pltpu.VMEM((bm, bn), F32)],
        compiler_params=_params(("parallel", "parallel", "arbitrary")),
    )(a, *halves)


def _mm_back2(dy_halves, w, *, name, bm=1024, bn=1024, bk=1024):
    m = dy_halves[0].shape[0]
    jn, kdim, ns = w.shape
    bm, bn, bk = _blk(m, bm), _blk(kdim, bn), _blk(ns, bk)
    kpj = ns // bk
    nk = jn * kpj
    khalf = nk // 2

    def body(a0_ref, a1_ref, b_ref, o_ref, acc_ref):
        k = pl.program_id(2)

        @pl.when(k == 0)
        def _():
            acc_ref[...] = jnp.zeros_like(acc_ref)

        for which, a_ref in enumerate((a0_ref, a1_ref)):
            @pl.when(k // khalf == which)
            def _(a_ref=a_ref):
                acc_ref[...] += lax.dot_general(a_ref[...], b_ref[...], _NT, preferred_element_type=F32)

        @pl.when(k == nk - 1)
        def _():
            o_ref[...] = acc_ref[...]

    def half_spec(which):
        return pl.BlockSpec((bm, bk), lambda i, n, k: (i, jnp.clip(k - which * khalf, 0, khalf - 1)))

    return pl.pallas_call(
        body, name=name, grid=(m // bm, kdim // bn, nk),
        in_specs=[half_spec(0), half_spec(1),
                  pl.BlockSpec((None, bn, bk), lambda i, n, k: (k // kpj, n, k % kpj))],
        out_specs=pl.BlockSpec((bm, bn), lambda i, n, k: (i, n)),
        out_shape=jax.ShapeDtypeStruct((m, kdim), F32),
        scratch_shapes=[pltpu.VMEM((bm, bn), F32)],
        compiler_params=_params(("parallel", "parallel", "arbitrary")),
    )(dy_halves[0], dy_halves[1], w)


def _ffn_in_act(n, w_in, *, name, bm=512):
    m, kdim = n.shape
    jn, _, ns = w_in.shape
    f = jn * ns // 2
    bm = _blk(m, bm)
    bn = _blk(ns, WIDE)
    npj = ns // bn
    nf = f // bn

    def body(n_ref, wg_ref, wu_ref, g_ref, u_ref, a_ref):
        nv = n_ref[...]
        g = jnp.dot(nv, wg_ref[...], preferred_element_type=F32)
        u = jnp.dot(nv, wu_ref[...], preferred_element_type=F32)
        g_ref[...] = g.astype(BF16)
        u_ref[...] = u.astype(BF16)
        a_ref[...] = (g * jax.nn.sigmoid(g) * u).astype(BF16)

    out = jax.ShapeDtypeStruct((m, f), BF16)
    ospec = pl.BlockSpec((bm, bn), lambda c, i: (i, c))
    return pl.pallas_call(
        body, name=name, grid=(nf, m // bm),
        in_specs=[pl.BlockSpec((bm, kdim), lambda c, i: (i, 0)),
                  pl.BlockSpec((None, kdim, bn), lambda c, i: (c // npj, 0, c % npj)),
                  pl.BlockSpec((None, kdim, bn), lambda c, i: ((c + nf) // npj, 0, (c + nf) % npj))],
        out_specs=[ospec, ospec, ospec], out_shape=[out, out, out],
        compiler_params=_params(("parallel", "parallel")),
    )(n, w_in, w_in)


def _ffn_out_dx_act(dh, w_out, gate, up, *, name, scale, bm=512):
    m, d = dh.shape
    f = w_out.shape[1]
    bm = _blk(m, bm)
    bn = _blk(f, WIDE)

    def body(dh_ref, w_ref, g_ref, u_ref, dg_ref, du_ref):
        da = lax.dot_general(dh_ref[...].astype(BF16), w_ref[...], _NT, preferred_element_type=F32) * scale
        g = g_ref[...].astype(F32)
        sg = jax.nn.sigmoid(g)
        silu = g * sg
        dg_ref[...] = (da * u_ref[...].astype(F32) * (sg + silu * (1.0 - sg))).astype(BF16)
        du_ref[...] = (da * silu).astype(BF16)

    out = jax.ShapeDtypeStruct((m, f), BF16)
    spec = pl.BlockSpec((bm, bn), lambda i, c: (i, c))
    return pl.pallas_call(
        body, name=name, grid=(m // bm, f // bn),
        in_specs=[pl.BlockSpec((bm, d), lambda i, c: (i, 0)), pl.BlockSpec((None, bn, d), lambda i, c: (0, c, 0)),
                  spec, spec],
        out_specs=[spec, spec], out_shape=[out, out],
        compiler_params=_params(("parallel", "parallel")),
    )(dh, w_out, gate, up)


ROW_BLOCK = 512


def _fold8(x):
    return jnp.sum(x.reshape(x.shape[0] // SUBLANES, SUBLANES, x.shape[1]), axis=0)


def _rms_fwd(x, g, *, name):
    t, d = x.shape
    tr = _blk(t, ROW_BLOCK)

    def body(x_ref, g_ref, o_ref):
        xv = x_ref[...]
        rstd = lax.rsqrt(jnp.mean(xv * xv, axis=-1, keepdims=True) + NORM_EPS)
        o_ref[...] = ((xv * rstd) * g_ref[...]).astype(BF16)

    return pl.pallas_call(
        body, name=name, grid=(t // tr,),
        in_specs=[pl.BlockSpec((tr, d), lambda i: (i, 0)), pl.BlockSpec((1, d), lambda i: (0, 0))],
        out_specs=pl.BlockSpec((tr, d), lambda i: (i, 0)),
        out_shape=jax.ShapeDtypeStruct((t, d), BF16),
        compiler_params=_params(("parallel",)),
    )(x, g.reshape(1, d))


def _rms_bwd(x, g, dy, dres, *, name):
    t, d = x.shape
    tr = _blk(t, ROW_BLOCK)

    def body(x_ref, g_ref, dy_ref, dres_ref, dx_ref, dg_ref):
        i = pl.program_id(0)
        xv = x_ref[...]
        rstd = lax.rsqrt(jnp.mean(xv * xv, axis=-1, keepdims=True) + NORM_EPS)
        xhat = xv * rstd
        dyv = dy_ref[...]
        dyg = dyv * g_ref[...]
        proj = jnp.mean(dyg * xhat, axis=-1, keepdims=True)
        dx_ref[...] = dres_ref[...] + rstd * (dyg - xhat * proj)

        @pl.when(i == 0)
        def _():
            dg_ref[...] = jnp.zeros_like(dg_ref)

        dg_ref[...] += _fold8(dyv * xhat)

    dx, dg = pl.pallas_call(
        body, name=name, grid=(t // tr,),
        in_specs=[pl.BlockSpec((tr, d), lambda i: (i, 0)), pl.BlockSpec((1, d), lambda i: (0, 0)),
                  pl.BlockSpec((tr, d), lambda i: (i, 0)), pl.BlockSpec((tr, d), lambda i: (i, 0))],
        out_specs=[pl.BlockSpec((tr, d), lambda i: (i, 0)), pl.BlockSpec((SUBLANES, d), lambda i: (0, 0))],
        out_shape=[jax.ShapeDtypeStruct((t, d), F32), jax.ShapeDtypeStruct((SUBLANES, d), F32)],
        compiler_params=_params(("arbitrary",)),
    )(x, g.reshape(1, d), dy, dres)
    return dx, jnp.sum(dg, axis=0)


def _swiglu_fwd(u, *, name):
    t, f2 = u.shape
    f = f2 // 2
    tr = _blk(t, 256)

    def body(g_ref, u_ref, o_ref):
        gv = g_ref[...]
        o_ref[...] = (gv * jax.nn.sigmoid(gv) * u_ref[...]).astype(BF16)

    return pl.pallas_call(
        body, name=name, grid=(t // tr,),
        in_specs=[pl.BlockSpec((tr, f), lambda i: (i, 0)), pl.BlockSpec((tr, f), lambda i: (i, 1))],
        out_specs=pl.BlockSpec((tr, f), lambda i: (i, 0)),
        out_shape=jax.ShapeDtypeStruct((t, f), BF16),
        compiler_params=_params(("parallel",)),
    )(u, u)


def _swiglu_bwd(u, da, *, name):
    t, f2 = u.shape
    f = f2 // 2
    tr = _blk(t, 256)

    def body(g_ref, u_ref, da_ref, o_ref):
        gv = g_ref[...]
        sg = jax.nn.sigmoid(gv)
        silu = gv * sg
        dav = da_ref[...]
        o_ref[:, :f] = (dav * u_ref[...] * (sg + silu * (1.0 - sg))).astype(BF16)
        o_ref[:, f:] = (dav * silu).astype(BF16)

    return pl.pallas_call(
        body, name=name, grid=(t // tr,),
        in_specs=[pl.BlockSpec((tr, f), lambda i: (i, 0)), pl.BlockSpec((tr, f), lambda i: (i, 1)),
                  pl.BlockSpec((tr, f), lambda i: (i, 0))],
        out_specs=pl.BlockSpec((tr, f2), lambda i: (i, 0)),
        out_shape=jax.ShapeDtypeStruct((t, f2), BF16),
        compiler_params=_params(("parallel",)),
    )(u, u, da)


def _loss_fwd_bwd(h, target, *, name):
    t, d = h.shape
    tr = _blk(t, ROW_BLOCK)

    def body(h_ref, t_ref, dh_ref, l_ref):
        i = pl.program_id(0)
        err = h_ref[...] - t_ref[...]
        dh_ref[...] = err * (1.0 / d)

        @pl.when(i == 0)
        def _():
            l_ref[...] = jnp.zeros_like(l_ref)

        l_ref[...] += _fold8(err * err)

    dh, part = pl.pallas_call(
        body, name=name, grid=(t // tr,),
        in_specs=[pl.BlockSpec((tr, d), lambda i: (i, 0)), pl.BlockSpec((tr, d), lambda i: (i, 0))],
        out_specs=[pl.BlockSpec((tr, d), lambda i: (i, 0)), pl.BlockSpec((SUBLANES, d), lambda i: (0, 0))],
        out_shape=[jax.ShapeDtypeStruct((t, d), F32), jax.ShapeDtypeStruct((SUBLANES, d), F32)],
        compiler_params=_params(("arbitrary",)),
    )(h, target)
    return jnp.sum(part) * (0.5 / d), dh


def _seg_matrix():
    r = lax.broadcasted_iota(jnp.int32, (LANES, LANES), 0) // HEAD_DIM
    c = lax.broadcasted_iota(jnp.int32, (LANES, LANES), 1) // HEAD_DIM
    return (r == c).astype(BF16)


def _head_sum(x, seg, terms=3):
    hi = x.astype(BF16)
    r1 = x - hi.astype(F32)
    mid = r1.astype(BF16)
    dot = functools.partial(jnp.dot, preferred_element_type=F32)
    if terms == 2:
        return dot(hi, seg) + dot(mid, seg)
    lo = (r1 - mid.astype(F32)).astype(BF16)
    return dot(hi, seg) + dot(mid, seg) + dot(lo, seg)


def _lane_in_head(shape):
    return lax.broadcasted_iota(jnp.int32, shape, 1) % HEAD_DIM


def _rot_partner(x):
    up = pltpu.roll(x, LANES - ROT_DIM // 2, 1)
    down = pltpu.roll(x, ROT_DIM // 2, 1)
    return jnp.where(_lane_in_head(x.shape) < ROT_DIM // 2, up, down)


def _rope_tables(positions):
    inv_freq = ROPE_THETA ** (-jnp.arange(0, ROT_DIM, 2, dtype=F32) / ROT_DIM)
    ang = positions.astype(F32)[:, None] * inv_freq
    t = ang.shape[0]
    rest = HEAD_DIM - ROT_DIM
    cos = jnp.concatenate([jnp.cos(ang), jnp.cos(ang), jnp.ones((t, rest), F32)], axis=1)
    sin = jnp.concatenate([-jnp.sin(ang), jnp.sin(ang), jnp.zeros((t, rest), F32)], axis=1)
    return jnp.tile(cos, (1, LANES // HEAD_DIM)), jnp.tile(sin, (1, LANES // HEAD_DIM))


def _kind_is(j, kinds, kind):
    hits = [j == jj for jj, k in enumerate(kinds) if k == kind]
    return functools.reduce(jnp.logical_or, hits) if hits else None


def _hn_fwd(x, gains, kinds, d, cos, sin, *, name, col0=0):
    t = x.shape[0]
    n = len(kinds)
    tr = _blk(t, ROW_BLOCK)
    seg = _seg_matrix()
    g8 = jnp.repeat(gains.astype(F32), SUBLANES, axis=0)

    def body(x_ref, g_ref, seg_ref, cos_ref, sin_ref, o_ref):
        j = pl.program_id(1)

        def normed(rope):
            for c in range(d // LANES):
                sl = slice(c * LANES, (c + 1) * LANES)
                xv = x_ref[:, sl]
                ms = _head_sum(xv * xv, seg_ref[...], terms=2) * (1.0 / HEAD_DIM)
                y = (xv * lax.rsqrt(ms + NORM_EPS)) * g_ref[0:1, sl]
                if rope:
                    y = y * cos_ref[...] + _rot_partner(y) * sin_ref[...]
                o_ref[:, sl] = y.astype(BF16)

        for kind in ("rope", "norm"):
            hit = _kind_is(j, kinds, kind)
            if hit is not None:
                pl.when(hit)(functools.partial(normed, kind == "rope"))
        hit = _kind_is(j, kinds, "cast")
        if hit is not None:
            @pl.when(hit)
            def _():
                o_ref[...] = x_ref[...].astype(BF16)

    return pl.pallas_call(
        body, name=name, grid=(t // tr, n),
        in_specs=[pl.BlockSpec((tr, d), lambda i, j: (i, col0 + j)), pl.BlockSpec((SUBLANES, d), lambda i, j: (j, 0)),
                  pl.BlockSpec((LANES, LANES), lambda i, j: (0, 0)),
                  pl.BlockSpec((tr, LANES), lambda i, j: (i, 0)), pl.BlockSpec((tr, LANES), lambda i, j: (i, 0))],
        out_specs=pl.BlockSpec((tr, d), lambda i, j: (i, j)),
        out_shape=jax.ShapeDtypeStruct((t, n * d), BF16),
        compiler_params=_params(("parallel", "parallel")),
    )(x, g8, seg, cos, sin)


def _hn_bwd(x, dys, gains, kinds, d, cos, sin, *, name, col0=0):
    t = x.shape[0]
    n = len(kinds)
    tr = _blk(t, ROW_BLOCK // 2)
    seg = _seg_matrix()
    g8 = jnp.repeat(gains.astype(F32), SUBLANES, axis=0)

    def body(x_ref, *refs):
        dy_refs = refs[:n]
        g_ref, seg_ref, cos_ref, sin_ref, dx_ref, dg_ref = refs[n:]
        j = pl.program_id(0)
        i = pl.program_id(1)

        @pl.when(i == 0)
        def _():
            dg_ref[...] = jnp.zeros_like(dg_ref)

        def normed(rope, dy_ref):
            for c in range(d // LANES):
                sl = slice(c * LANES, (c + 1) * LANES)
                xv = x_ref[:, sl]
                dyv = dy_ref[:, sl]
                if rope:
                    dyv = dyv * cos_ref[...] - _rot_partner(dyv) * sin_ref[...]
                ms = _head_sum(xv * xv, seg_ref[...], terms=2) * (1.0 / HEAD_DIM)
                rstd = lax.rsqrt(ms + NORM_EPS)
                xhat = xv * rstd
                dg_ref[:, sl] += _fold8(dyv * xhat)
                dyg = dyv * g_ref[0:1, sl]
                proj = _head_sum(dyg * xhat, seg_ref[...], terms=2) * (1.0 / HEAD_DIM)
                dx_ref[:, sl] = (rstd * (dyg - xhat * proj)).astype(BF16)

        def cast(dy_ref):
            dx_ref[...] = dy_ref[...].astype(BF16)

        for jj, kind in enumerate(kinds):
            if kind == "cast":
                pl.when(j == jj)(functools.partial(cast, dy_refs[jj]))
            else:
                pl.when(j == jj)(functools.partial(normed, kind == "rope", dy_refs[jj]))

    def dy_spec(jj):
        return pl.BlockSpec((tr, d), lambda j, i: (jnp.where(j == jj, i, 0), 0))

    dx, dg = pl.pallas_call(
        body, name=name, grid=(n, t // tr),
        in_specs=[pl.BlockSpec((tr, d), lambda j, i: (i, col0 + j))] + [dy_spec(jj) for jj in range(n)] + [
                  pl.BlockSpec((SUBLANES, d), lambda j, i: (j, 0)),
                  pl.BlockSpec((LANES, LANES), lambda j, i: (0, 0)),
                  pl.BlockSpec((tr, LANES), lambda j, i: (i, 0)), pl.BlockSpec((tr, LANES), lambda j, i: (i, 0))],
        out_specs=[pl.BlockSpec((tr, d), lambda j, i: (i, j)), pl.BlockSpec((SUBLANES, d), lambda j, i: (j, 0))],
        out_shape=[jax.ShapeDtypeStruct((t, n * d), BF16), jax.ShapeDtypeStruct((n * SUBLANES, d), F32)],
        compiler_params=_params(("arbitrary", "arbitrary")),
    )(x, *dys, g8, seg, cos, sin)
    dg = dg.reshape(n, SUBLANES, d // HEAD_DIM, HEAD_DIM).sum(axis=(1, 2))
    return dx, dg


def _head_dot(a, b, *, name):
    t, d = a.shape
    tr = _blk(t, ROW_BLOCK)
    seg = _seg_matrix()

    def body(a_ref, b_ref, seg_ref, o_ref):
        for c in range(d // LANES):
            sl = slice(c * LANES, (c + 1) * LANES)
            o_ref[:, sl] = _head_sum(a_ref[:, sl].astype(BF16).astype(F32) * b_ref[:, sl], seg_ref[...])

    return pl.pallas_call(
        body, name=name, grid=(t // tr,),
        in_specs=[pl.BlockSpec((tr, d), lambda i: (i, 0)), pl.BlockSpec((tr, d), lambda i: (i, 0)),
                  pl.BlockSpec((LANES, LANES), lambda i: (0, 0))],
        out_specs=pl.BlockSpec((tr, d), lambda i: (i, 0)),
        out_shape=jax.ShapeDtypeStruct((t, d), F32),
        compiler_params=_params(("parallel",)),
    )(a, b, seg)


def _half_mask(shape):
    return lax.broadcasted_iota(jnp.int32, shape, 1) < HEAD_DIM


def _band_valid(first):
    qi = lax.broadcasted_iota(jnp.int32, (BAND, 2 * BAND), 0)
    kj = lax.broadcasted_iota(jnp.int32, (BAND, 2 * BAND), 1)
    dist = qi + BAND - kj
    return (dist >= 0) & (dist <= BAND) & ((kj >= BAND) | jnp.logical_not(first))


_NT = (((1,), (1,)), ((), ()))
_TN = (((0,), (0,)), ((), ()))


def _dot2(p, v):
    hi = p.astype(BF16)
    lo = (p - hi.astype(F32)).astype(BF16)
    return jnp.dot(hi, v, preferred_element_type=F32) + jnp.dot(lo, v, preferred_element_type=F32)


def _band_fwd(qkv, dil, cfg, *, name):
    t, d = cfg.tokens, cfg.d_model
    w = 3 * d
    rows = t // dil
    nbt = rows // BAND
    nb = cfg.seq // (dil * BAND)
    view = qkv.reshape(rows, dil * w)
    ncol = w // d

    def body(q_ref, kp_ref, kc_ref, vp_ref, vc_ref, o_ref, lse_ref):
        i = pl.program_id(1)
        valid = _band_valid(i % nb == 0)
        half = _half_mask((BAND, LANES))
        for hp in range(d // LANES):
            sl = slice(hp * LANES, (hp + 1) * LANES)
            q2 = q_ref[:, sl]
            kk = jnp.concatenate([kp_ref[:, sl], kc_ref[:, sl]], axis=0)
            vv = jnp.concatenate([vp_ref[:, sl], vc_ref[:, sl]], axis=0)
            outs, lses = [], []
            for e in range(2):
                qe = jnp.where(half == (e == 0), q2, jnp.zeros_like(q2))
                s = lax.dot_general(qe, kk, _NT, preferred_element_type=F32)
                s = jnp.where(valid, s, NEG)
                m = jnp.max(s, axis=1, keepdims=True)
                p = jnp.exp(s - m)
                l = jnp.sum(p, axis=1, keepdims=True)
                outs.append(_dot2(p * (1.0 / l), vv))
                lses.append(m + jnp.log(l))
            o_ref[:, sl] = jnp.where(half, outs[0], outs[1])
            lse_ref[:, sl] = jnp.where(half, lses[0], lses[1])

    def col(which):
        return lambda r, i: (i, r * ncol + which)

    def col_prev(which):
        return lambda r, i: (jnp.maximum(i - 1, 0), r * ncol + which)

    blk = (BAND, d)
    o, lse = pl.pallas_call(
        body, name=name, grid=(dil, nbt),
        in_specs=[pl.BlockSpec(blk, col(0)), pl.BlockSpec(blk, col_prev(1)), pl.BlockSpec(blk, col(1)),
                  pl.BlockSpec(blk, col_prev(2)), pl.BlockSpec(blk, col(2))],
        out_specs=[pl.BlockSpec(blk, lambda r, i: (i, r)), pl.BlockSpec(blk, lambda r, i: (i, r))],
        out_shape=[jax.ShapeDtypeStruct((rows, dil * d), F32), jax.ShapeDtypeStruct((rows, dil * d), F32)],
        compiler_params=_params(("parallel", "arbitrary")),
    )(view, view, view, view, view)
    return o.reshape(t, d), lse.reshape(t, d)


def _band_bwd(qkv, dmixed, lse_all, dsum, dil, cfg, *, name):
    t, d = cfg.tokens, cfg.d_model
    w = 3 * d
    rows = t // dil
    nbt = rows // BAND
    nb = cfg.seq // (dil * BAND)
    view = qkv.reshape(rows, dil * w)
    ncol = w // d
    do_v, l_v, d_v = (z.reshape(rows, dil * d) for z in (dmixed, lse_all, dsum))

    def body(q_ref, kp_ref, kc_ref, vp_ref, vc_ref, do_ref, l_ref, ds_ref, dq_ref, dk_ref, dv_ref, ck_ref, cv_ref):
        i = pl.program_id(1)

        @pl.when(i < nbt)
        def _():
            valid = _band_valid(i % nb == 0)
            half = _half_mask((BAND, LANES))
            half2 = _half_mask((2 * BAND, LANES))
            for hp in range(d // LANES):
                sl = slice(hp * LANES, (hp + 1) * LANES)
                q2 = q_ref[:, sl]
                kk = jnp.concatenate([kp_ref[:, sl], kc_ref[:, sl]], axis=0)
                vv = jnp.concatenate([vp_ref[:, sl], vc_ref[:, sl]], axis=0)
                do2 = do_ref[:, sl].astype(BF16)
                dqs, dks, dvs = [], [], []
                for e in range(2):
                    lane0 = e * HEAD_DIM
                    keep = half == (e == 0)
                    qe = jnp.where(keep, q2, jnp.zeros_like(q2))
                    doe = jnp.where(keep, do2, jnp.zeros_like(do2))
                    s = lax.dot_general(qe, kk, _NT, preferred_element_type=F32)
                    s = jnp.where(valid, s, NEG)
                    p = jnp.exp(s - l_ref[:, hp * LANES + lane0:hp * LANES + lane0 + 1])
                    dp = lax.dot_general(doe, vv, _NT, preferred_element_type=F32)
                    dsc = (p * (dp - ds_ref[:, hp * LANES + lane0:hp * LANES + lane0 + 1])).astype(BF16)
                    dqs.append(jnp.dot(dsc, kk, preferred_element_type=F32))
                    dks.append(lax.dot_general(dsc, q2, _TN, preferred_element_type=F32))
                    dvs.append(lax.dot_general(p.astype(BF16), do2, _TN, preferred_element_type=F32))
                dq_ref[:, sl] = jnp.where(half, dqs[0], dqs[1])
                dkk = jnp.where(half2, dks[0], dks[1])
                dvv = jnp.where(half2, dvs[0], dvs[1])

                @pl.when(i > 0)
                def _():
                    dk_ref[:, sl] = ck_ref[:, sl] + dkk[:BAND]
                    dv_ref[:, sl] = cv_ref[:, sl] + dvv[:BAND]

                ck_ref[:, sl] = dkk[BAND:]
                cv_ref[:, sl] = dvv[BAND:]

        @pl.when(i == nbt)
        def _():
            dk_ref[...] = ck_ref[...]
            dv_ref[...] = cv_ref[...]

    def cur(i):
        return jnp.minimum(i, nbt - 1)

    def col(which):
        return lambda r, i: (cur(i), r * ncol + which)

    def col_prev(which):
        return lambda r, i: (jnp.maximum(cur(i) - 1, 0), r * ncol + which)

    blk = (BAND, d)
    here = pl.BlockSpec(blk, lambda r, i: (cur(i), r))
    behind = pl.BlockSpec(blk, lambda r, i: (jnp.maximum(i - 1, 0), r))
    shape = jax.ShapeDtypeStruct((rows, dil * d), F32)
    dq, dk, dv = pl.pallas_call(
        body, name=name, grid=(dil, nbt + 1),
        in_specs=[pl.BlockSpec(blk, col(0)), pl.BlockSpec(blk, col_prev(1)), pl.BlockSpec(blk, col(1)),
                  pl.BlockSpec(blk, col_prev(2)), pl.BlockSpec(blk, col(2)), here, here, here],
        out_specs=[here, behind, behind],
        out_shape=[shape, shape, shape],
        scratch_shapes=[pltpu.VMEM(blk, F32), pltpu.VMEM(blk, F32)],
        compiler_params=_params(("arbitrary", "arbitrary")),
    )(view, view, view, view, view, do_v, l_v, d_v)
    return dq.reshape(t, d), dk.reshape(t, d), dv.reshape(t, d)


def _band_valid_t(first):
    s = lax.broadcasted_iota(jnp.int32, (2 * BAND, BAND), 0)
    t = lax.broadcasted_iota(jnp.int32, (2 * BAND, BAND), 1)
    dist = t + BAND - s
    return (dist >= 0) & (dist <= BAND) & ((s >= BAND) | jnp.logical_not(first))


def _band_layouts(qkv, dil, cfg):
    rows = cfg.tokens // dil
    d = cfg.d_model
    return qkv.reshape(rows, dil * 3 * d), qkv.reshape(rows, dil, 3, d).transpose(1, 2, 3, 0)


def _to_classes_t(z, dil, width):
    return z.reshape(z.shape[0] // dil, dil, width).transpose(1, 2, 0)


def _from_classes_t(z):
    dil, width, rows = z.shape
    return z.transpose(2, 0, 1).reshape(rows * dil, width)


def _band_fwd_t(nat, tr, dil, cfg, *, name):
    d, hh = cfg.d_model, cfg.heads
    rows = cfg.tokens // dil
    nbt = rows // BAND
    nb = cfg.seq // (dil * BAND)

    def body(qt_ref, kp_ref, kc_ref, vtp_ref, vtc_ref, o_ref, lse_ref):
        i = pl.program_id(1)
        valid = _band_valid_t(i % nb == 0)
        upper = lax.broadcasted_iota(jnp.int32, (LANES, BAND), 0) < HEAD_DIM
        for hp in range(d // LANES):
            pair = slice(hp * LANES, (hp + 1) * LANES)
            qt2 = qt_ref[pair, :]
            kk = jnp.concatenate([kp_ref[:, pair], kc_ref[:, pair]], axis=0)
            for e in range(2):
                h = 2 * hp + e
                hrows = slice(h * HEAD_DIM, (h + 1) * HEAD_DIM)
                qte = jnp.where(upper == (e == 0), qt2, jnp.zeros_like(qt2))
                s = jnp.where(valid, jnp.dot(kk, qte, preferred_element_type=F32), NEG)
                m = jnp.max(s, axis=0, keepdims=True)
                p = jnp.exp(s - m)
                l = jnp.sum(p, axis=0, keepdims=True)
                hi = p.astype(BF16)
                lo = (p - hi.astype(F32)).astype(BF16)
                vvt = jnp.concatenate([vtp_ref[hrows, :], vtc_ref[hrows, :]], axis=1)
                o = jnp.dot(vvt, hi, preferred_element_type=F32) + jnp.dot(vvt, lo, preferred_element_type=F32)
                o_ref[hrows, :] = o * (1.0 / l)
                lse_ref[h:h + 1, :] = m + jnp.log(l)

    def prev(i):
        return jnp.maximum(i - 1, 0)

    tblk = (None, None, d, BAND)
    return pl.pallas_call(
        body, name=name, grid=(dil, nbt),
        in_specs=[pl.BlockSpec(tblk, lambda r, i: (r, 0, 0, i)),
                  pl.BlockSpec((BAND, d), lambda r, i: (prev(i), r * 3 + 1)),
                  pl.BlockSpec((BAND, d), lambda r, i: (i, r * 3 + 1)),
                  pl.BlockSpec(tblk, lambda r, i: (r, 2, 0, prev(i))),
                  pl.BlockSpec(tblk, lambda r, i: (r, 2, 0, i))],
        out_specs=[pl.BlockSpec((None, d, BAND), lambda r, i: (r, 0, i)),
                   pl.BlockSpec((None, hh, BAND), lambda r, i: (r, 0, i))],
        out_shape=[jax.ShapeDtypeStruct((dil, d, rows), F32), jax.ShapeDtypeStruct((dil, hh, rows), F32)],
        compiler_params=_params(("parallel", "arbitrary")),
    )(tr, nat, nat, tr, tr)


def _band_bwd_t(nat, tr, do_t, do_nat, lse_c, dsum_c, dil, cfg, *, name):
    d, hh = cfg.d_model, cfg.heads
    rows = cfg.tokens // dil
    nbt = rows // BAND
    nb = cfg.seq // (dil * BAND)

    def body(qt_ref, qn_ref, kp_ref, kc_ref, ktp_ref, ktc_ref, vp_ref, vc_ref, dot_ref, don_ref, l_ref, ds_ref,
             dq_ref, dk_ref, dv_ref, ck_ref, cv_ref):
        i = pl.program_id(1)

        @pl.when(i < nbt)
        def _():
            valid = _band_valid_t(i % nb == 0)
            upper = lax.broadcasted_iota(jnp.int32, (LANES, BAND), 0) < HEAD_DIM
            half2 = _half_mask((2 * BAND, LANES))
            for hp in range(d // LANES):
                pair = slice(hp * LANES, (hp + 1) * LANES)
                qt2, dot2 = qt_ref[pair, :], dot_ref[pair, :]
                qn2, don2 = qn_ref[:, pair], don_ref[:, pair]
                kk = jnp.concatenate([kp_ref[:, pair], kc_ref[:, pair]], axis=0)
                vv = jnp.concatenate([vp_ref[:, pair], vc_ref[:, pair]], axis=0)
                dks, dvs = [], []
                for e in range(2):
                    h = 2 * hp + e
                    hrows = slice(h * HEAD_DIM, (h + 1) * HEAD_DIM)
                    keep = upper == (e == 0)
                    qte = jnp.where(keep, qt2, jnp.zeros_like(qt2))
                    dote = jnp.where(keep, dot2, jnp.zeros_like(dot2))
                    s = jnp.where(valid, jnp.dot(kk, qte, preferred_element_type=F32), NEG)
                    p = jnp.exp(s - l_ref[h:h + 1, :])
                    dp = jnp.dot(vv, dote, preferred_element_type=F32)
                    dsb = (p * (dp - ds_ref[h:h + 1, :])).astype(BF16)
                    kkt = jnp.concatenate([ktp_ref[hrows, :], ktc_ref[hrows, :]], axis=1)
                    dq_ref[hrows, :] = jnp.dot(kkt, dsb, preferred_element_type=F32)
                    dks.append(jnp.dot(dsb, qn2, preferred_element_type=F32))
                    dvs.append(jnp.dot(p.astype(BF16), don2, preferred_element_type=F32))
                dkk = jnp.where(half2, dks[0], dks[1])
                dvv = jnp.where(half2, dvs[0], dvs[1])

                @pl.when(i > 0)
                def _():
                    dk_ref[:, pair] = ck_ref[:, pair] + dkk[:BAND]
                    dv_ref[:, pair] = cv_ref[:, pair] + dvv[:BAND]

                ck_ref[:, pair] = dkk[BAND:]
                cv_ref[:, pair] = dvv[BAND:]

        @pl.when(i == nbt)
        def _():
            dk_ref[...] = ck_ref[...]
            dv_ref[...] = cv_ref[...]

    def cur(i):
        return jnp.minimum(i, nbt - 1)

    def prev(i):
        return jnp.maximum(cur(i) - 1, 0)

    tblk = (None, None, d, BAND)
    cblk = (None, hh, BAND)
    blk = (BAND, d)
    behind = pl.BlockSpec(blk, lambda r, i: (jnp.maximum(i - 1, 0), r))
    shape = jax.ShapeDtypeStruct((rows, dil * d), F32)
    return pl.pallas_call(
        body, name=name, grid=(dil, nbt + 1),
        in_specs=[pl.BlockSpec(tblk, lambda r, i: (r, 0, 0, cur(i))),
                  pl.BlockSpec(blk, lambda r, i: (cur(i), r * 3)),
                  pl.BlockSpec(blk, lambda r, i: (prev(i), r * 3 + 1)),
                  pl.BlockSpec(blk, lambda r, i: (cur(i), r * 3 + 1)),
                  pl.BlockSpec(tblk, lambda r, i: (r, 1, 0, prev(i))),
                  pl.BlockSpec(tblk, lambda r, i: (r, 1, 0, cur(i))),
                  pl.BlockSpec(blk, lambda r, i: (prev(i), r * 3 + 2)),
                  pl.BlockSpec(blk, lambda r, i: (cur(i), r * 3 + 2)),
                  pl.BlockSpec((None, d, BAND), lambda r, i: (r, 0, cur(i))),
                  pl.BlockSpec(blk, lambda r, i: (cur(i), r)),
                  pl.BlockSpec(cblk, lambda r, i: (r, 0, cur(i))),
                  pl.BlockSpec(cblk, lambda r, i: (r, 0, cur(i)))],
        out_specs=[pl.BlockSpec((None, d, BAND), lambda r, i: (r, 0, cur(i))), behind, behind],
        out_shape=[jax.ShapeDtypeStruct((dil, d, rows), F32), shape, shape],
        scratch_shapes=[pltpu.VMEM(blk, F32), pltpu.VMEM(blk, F32)],
        compiler_params=_params(("arbitrary", "arbitrary")),
    )(tr, nat, nat, nat, tr, tr, nat, nat, do_t, do_nat, lse_c, dsum_c)


def _band_fwd_n(nat, dil, cfg, *, name):
    d, hh = cfg.d_model, cfg.heads
    rows = cfg.tokens // dil
    nbt = rows // BAND
    nb = cfg.seq // (dil * BAND)

    def body(q_ref, kp_ref, kc_ref, vp_ref, vc_ref, o_ref, lse_ref):
        i = pl.program_id(1)
        valid = _band_valid_t(i % nb == 0)
        upper = lax.broadcasted_iota(jnp.int32, (LANES, BAND), 0) < HEAD_DIM
        for hp in range(d // LANES):
            pair = slice(hp * LANES, (hp + 1) * LANES)
            qt2 = q_ref[:, pair].T
            kk = jnp.concatenate([kp_ref[:, pair], kc_ref[:, pair]], axis=0)
            vvt = jnp.concatenate([vp_ref[:, pair], vc_ref[:, pair]], axis=0).T
            outs = []
            for e in range(2):
                h = 2 * hp + e
                qte = jnp.where(upper == (e == 0), qt2, jnp.zeros_like(qt2))
                s = jnp.where(valid, jnp.dot(kk, qte, preferred_element_type=F32), NEG)
                m = jnp.max(s, axis=0, keepdims=True)
                p = jnp.exp(s - m)
                l = jnp.sum(p, axis=0, keepdims=True)
                hi = p.astype(BF16)
                lo = (p - hi.astype(F32)).astype(BF16)
                vt = vvt[e * HEAD_DIM:(e + 1) * HEAD_DIM]
                o = jnp.dot(vt, hi, preferred_element_type=F32) + jnp.dot(vt, lo, preferred_element_type=F32)
                outs.append(o * (1.0 / l))
                lse_ref[h:h + 1, :] = m + jnp.log(l)
            o_ref[:, pair] = jnp.concatenate(outs, axis=0).T

    def prev(i):
        return jnp.maximum(i - 1, 0)

    blk = (BAND, d)
    return pl.pallas_call(
        body, name=name, grid=(dil, nbt),
        in_specs=[pl.BlockSpec(blk, lambda r, i: (i, r * 3)),
                  pl.BlockSpec(blk, lambda r, i: (prev(i), r * 3 + 1)),
                  pl.BlockSpec(blk, lambda r, i: (i, r * 3 + 1)),
                  pl.BlockSpec(blk, lambda r, i: (prev(i), r * 3 + 2)),
                  pl.BlockSpec(blk, lambda r, i: (i, r * 3 + 2))],
        out_specs=[pl.BlockSpec(blk, lambda r, i: (i, r)),
                   pl.BlockSpec((None, hh, BAND), lambda r, i: (r, 0, i))],
        out_shape=[jax.ShapeDtypeStruct((rows, dil * d), F32), jax.ShapeDtypeStruct((dil, hh, rows), F32)],
        compiler_params=_params(("parallel", "arbitrary")),
    )(nat, nat, nat, nat, nat)


def _band_bwd_n(nat, do_nat, lse_c, dsum_c, dil, cfg, *, name):
    d, hh = cfg.d_model, cfg.heads
    rows = cfg.tokens // dil
    nbt = rows // BAND
    nb = cfg.seq // (dil * BAND)

    def body(q_ref, kp_ref, kc_ref, vp_ref, vc_ref, do_ref, l_ref, ds_ref, dq_ref, dk_ref, dv_ref, ck_ref, cv_ref):
        i = pl.program_id(1)

        @pl.when(i < nbt)
        def _():
            @pl.when(i == 0)
            def _():
                ck_ref[...] = jnp.zeros_like(ck_ref)
                cv_ref[...] = jnp.zeros_like(cv_ref)

            valid1 = _band_valid_t(i % nb == 0)
            valid = jnp.concatenate([valid1, valid1], axis=1)
            upper = lax.broadcasted_iota(jnp.int32, (LANES, BAND), 0) < HEAD_DIM
            half2 = _half_mask((2 * BAND, LANES))

            def both(z):
                zero = jnp.zeros_like(z)
                return jnp.concatenate([jnp.where(upper, z, zero), jnp.where(upper, zero, z)], axis=1)

            def stack(z):
                return jnp.concatenate([z[:, :BAND], z[:, BAND:]], axis=0)

            for hp in range(d // LANES):
                pair = slice(hp * LANES, (hp + 1) * LANES)
                h0, h1 = 2 * hp, 2 * hp + 1
                qn2, don2 = q_ref[:, pair], do_ref[:, pair]
                kk = jnp.concatenate([kp_ref[:, pair], kc_ref[:, pair]], axis=0)
                vv = jnp.concatenate([vp_ref[:, pair], vc_ref[:, pair]], axis=0)
                lse2 = jnp.concatenate([l_ref[h0:h0 + 1, :], l_ref[h1:h1 + 1, :]], axis=1)
                dsum2 = jnp.concatenate([ds_ref[h0:h0 + 1, :], ds_ref[h1:h1 + 1, :]], axis=1)
                s = jnp.where(valid, jnp.dot(kk, both(qn2.T), preferred_element_type=F32), NEG)
                p = jnp.exp(s - lse2)
                dp = jnp.dot(vv, both(don2.T), preferred_element_type=F32)
                dsb = (p * (dp - dsum2)).astype(BF16)
                dq2 = jnp.dot(kk.T, dsb, preferred_element_type=F32)
                dq_ref[:, pair] = jnp.concatenate([dq2[:HEAD_DIM, :BAND], dq2[HEAD_DIM:, BAND:]], axis=0).T
                dk2 = jnp.dot(stack(dsb), qn2, preferred_element_type=F32)
                dv2 = jnp.dot(stack(p.astype(BF16)), don2, preferred_element_type=F32)
                dkk = jnp.where(half2, dk2[:2 * BAND], dk2[2 * BAND:])
                dvv = jnp.where(half2, dv2[:2 * BAND], dv2[2 * BAND:])
                dk_ref[:, pair] = ck_ref[:, pair] + dkk[:BAND]
                dv_ref[:, pair] = cv_ref[:, pair] + dvv[:BAND]
                ck_ref[:, pair] = dkk[BAND:]
                cv_ref[:, pair] = dvv[BAND:]

        @pl.when(i == nbt)
        def _():
            dk_ref[...] = ck_ref[...]
            dv_ref[...] = cv_ref[...]

    def cur(i):
        return jnp.minimum(i, nbt - 1)

    def prev(i):
        return jnp.maximum(cur(i) - 1, 0)

    cblk = (None, hh, BAND)
    blk = (BAND, d)
    here = pl.BlockSpec(blk, lambda r, i: (cur(i), r))
    behind = pl.BlockSpec(blk, lambda r, i: (jnp.maximum(i - 1, 0), r))
    shape = jax.ShapeDtypeStruct((rows, dil * d), F32)
    return pl.pallas_call(
        body, name=name, grid=(dil, nbt + 1),
        in_specs=[pl.BlockSpec(blk, lambda r, i: (cur(i), r * 3)),
                  pl.BlockSpec(blk, lambda r, i: (prev(i), r * 3 + 1)),
                  pl.BlockSpec(blk, lambda r, i: (cur(i), r * 3 + 1)),
                  pl.BlockSpec(blk, lambda r, i: (prev(i), r * 3 + 2)),
                  pl.BlockSpec(blk, lambda r, i: (cur(i), r * 3 + 2)),
                  here,
                  pl.BlockSpec(cblk, lambda r, i: (r, 0, cur(i))),
                  pl.BlockSpec(cblk, lambda r, i: (r, 0, cur(i)))],
        out_specs=[here, behind, behind],
        out_shape=[shape, shape, shape],
        scratch_shapes=[pltpu.VMEM(blk, F32), pltpu.VMEM(blk, F32)],
        compiler_params=_params(("arbitrary", "arbitrary")),
    )(nat, nat, nat, nat, nat, do_nat, lse_c, dsum_c)


def _mix_fwd(outs, lses, *, name):
    t, d = outs[0].shape
    tr = _blk(t, ROW_BLOCK)
    ng = len(outs)

    def body(*refs):
        o_refs, l_refs = refs[:ng], refs[ng:2 * ng]
        mixed_ref, lse_ref = refs[2 * ng:]
        ls = [r[...] for r in l_refs]
        m = functools.reduce(jnp.maximum, ls)
        es = [jnp.exp(l - m) for l in ls]
        tot = functools.reduce(jnp.add, es)
        inv = 1.0 / tot
        mixed_ref[...] = functools.reduce(jnp.add, [(e * inv) * r[...] for e, r in zip(es, o_refs)])
        lse_ref[...] = m + jnp.log(tot)

    spec = pl.BlockSpec((tr, d), lambda i: (i, 0))
    return pl.pallas_call(
        body, name=name, grid=(t // tr,),
        in_specs=[spec] * (2 * ng), out_specs=[spec, spec],
        out_shape=[jax.ShapeDtypeStruct((t, d), F32), jax.ShapeDtypeStruct((t, d), F32)],
        compiler_params=_params(("parallel",)),
    )(*outs, *lses)


GATE_BLOCK = 256


def _tri(n, upper):
    r = lax.broadcasted_iota(jnp.int32, (n, n), 0)
    c = lax.broadcasted_iota(jnp.int32, (n, n), 1)
    return ((c >= r) if upper else (c <= r)).astype(BF16)


def _tri_dot(tri, x):
    hi = x.astype(BF16)
    r1 = x - hi.astype(F32)
    mid = r1.astype(BF16)
    lo = (r1 - mid.astype(F32)).astype(BF16)
    dot = functools.partial(jnp.dot, preferred_element_type=F32)
    return dot(tri, hi) + dot(tri, mid) + dot(tri, lo)


def _log_sigmoid(z):
    return jnp.minimum(z, 0.0) - jnp.log(1.0 + jnp.exp(-jnp.abs(z)))


def _gate_fwd(proj, col_block, bias, cfg, *, name):
    tr = _blk(cfg.seq, GATE_BLOCK)
    nblk = cfg.seq // tr

    def body(z_ref, b_ref, tri_ref, o_ref, carry_ref):
        i = pl.program_id(1)

        @pl.when(i == 0)
        def _():
            carry_ref[...] = jnp.zeros_like(carry_ref)

        logf = _log_sigmoid(z_ref[...] + b_ref[0:1, :])
        cum = _tri_dot(tri_ref[...], logf) + carry_ref[0:1, :]
        o_ref[...] = cum
        carry_ref[...] = jnp.broadcast_to(cum[tr - 1:tr, :], carry_ref.shape)

    return pl.pallas_call(
        body, name=name, grid=(cfg.batch, nblk),
        in_specs=[pl.BlockSpec((tr, LANES), lambda b, i: (b * nblk + i, col_block)),
                  pl.BlockSpec((SUBLANES, LANES), lambda b, i: (0, 0)),
                  pl.BlockSpec((tr, tr), lambda b, i: (0, 0))],
        out_specs=pl.BlockSpec((tr, LANES), lambda b, i: (b * nblk + i, 0)),
        out_shape=jax.ShapeDtypeStruct((cfg.tokens, LANES), F32),
        scratch_shapes=[pltpu.VMEM((SUBLANES, LANES), F32)],
        compiler_params=_params(("arbitrary", "arbitrary")),
    )(proj, jnp.broadcast_to(bias, (SUBLANES, LANES)), _tri(tr, upper=False))


def _gate_bwd(proj, col_block, bias, dcum, cfg, *, name):
    tr = _blk(cfg.seq, GATE_BLOCK)
    nblk = cfg.seq // tr

    def body(z_ref, b_ref, tri_ref, dc_ref, dz_ref, db_ref, carry_ref):
        b = pl.program_id(0)
        i = pl.program_id(1)

        @pl.when(i == 0)
        def _():
            carry_ref[...] = jnp.zeros_like(carry_ref)

        @pl.when((i == 0) & (b == 0))
        def _():
            db_ref[...] = jnp.zeros_like(db_ref)

        dcv = dc_ref[...]
        dlogf = _tri_dot(tri_ref[...], dcv) + carry_ref[0:1, :]
        carry_ref[...] = jnp.broadcast_to(dlogf[0:1, :], carry_ref.shape)
        dz = dlogf * jax.nn.sigmoid(-(z_ref[...] + b_ref[0:1, :]))
        dz_ref[...] = dz
        db_ref[...] += _fold8(dz)

    def rev(b, i):
        return (b * nblk + nblk - 1 - i, 0)

    dz, db = pl.pallas_call(
        body, name=name, grid=(cfg.batch, nblk),
        in_specs=[pl.BlockSpec((tr, LANES), lambda b, i: (b * nblk + nblk - 1 - i, col_block)),
                  pl.BlockSpec((SUBLANES, LANES), lambda b, i: (0, 0)),
                  pl.BlockSpec((tr, tr), lambda b, i: (0, 0)),
                  pl.BlockSpec((tr, LANES), rev)],
        out_specs=[pl.BlockSpec((tr, LANES), rev), pl.BlockSpec((SUBLANES, LANES), lambda b, i: (0, 0))],
        out_shape=[jax.ShapeDtypeStruct((cfg.tokens, LANES), F32), jax.ShapeDtypeStruct((SUBLANES, LANES), F32)],
        scratch_shapes=[pltpu.VMEM((SUBLANES, LANES), F32)],
        compiler_params=_params(("arbitrary", "arbitrary")),
    )(proj, jnp.broadcast_to(bias, (SUBLANES, LANES)), _tri(tr, upper=True), dcum)
    return dz, jnp.sum(db, axis=0)


FOX_BLOCK = 256


def _fox_scores(q2, k2, e, half, mask, cref, ck_row):
    qe = jnp.where(half == (e == 0), q2, jnp.zeros_like(q2))
    s = lax.dot_general(qe, k2, _NT, preferred_element_type=F32)
    return jnp.where(mask, s + (cref - ck_row), NEG)


def _causal(qi, ki, tq):
    r = lax.broadcasted_iota(jnp.int32, (tq, tq), 0) + qi * tq
    c = lax.broadcasted_iota(jnp.int32, (tq, tq), 1) + ki * tq
    return r >= c


def _fox_fwd(q, kv, cum_t, cfg, *, name):
    t, d, hrows = cfg.tokens, cfg.d_model, cum_t.shape[0]
    tq = _blk(cfg.seq, FOX_BLOCK)
    nq = cfg.seq // tq

    def body(q_ref, k_ref, v_ref, cq_ref, ck_ref, o_ref, lse_ref, m_ref, l_ref, acc_ref):
        qi, ki = pl.program_id(1), pl.program_id(2)

        @pl.when(ki == 0)
        def _():
            m_ref[...] = jnp.full_like(m_ref, NEG)
            l_ref[...] = jnp.zeros_like(l_ref)
            acc_ref[...] = jnp.zeros_like(acc_ref)

        @pl.when(ki <= qi)
        def _():
            mask = _causal(qi, ki, tq)
            half = _half_mask((tq, LANES))
            for hp in range(d // LANES):
                sl = slice(hp * LANES, (hp + 1) * LANES)
                q2, k2, v2 = q_ref[:, sl], k_ref[:, sl], v_ref[:, sl]
                alphas, pvs = [], []
                for e in range(2):
                    h = 2 * hp + e
                    s = _fox_scores(q2, k2, e, half, mask, cq_ref[h:h + 1, 0:1], ck_ref[h:h + 1, :])
                    m_prev = m_ref[h]
                    m_new = jnp.maximum(m_prev, jnp.max(s, axis=1, keepdims=True))
                    alpha = jnp.exp(m_prev - m_new)
                    p = jnp.exp(s - m_new[:, 0:1])
                    l_ref[h] = alpha * l_ref[h] + jnp.sum(p, axis=1, keepdims=True)
                    m_ref[h] = m_new
                    alphas.append(alpha)
                    pvs.append(_dot2(p, v2))
                acc = acc_ref[:, sl]
                acc_ref[:, sl] = jnp.where(half, alphas[0] * acc + pvs[0], alphas[1] * acc + pvs[1])

        @pl.when(ki == qi)
        def _():
            half = _half_mask((tq, LANES))
            for hp in range(d // LANES):
                sl = slice(hp * LANES, (hp + 1) * LANES)
                h0, h1 = 2 * hp, 2 * hp + 1
                inv = jnp.where(half, 1.0 / l_ref[h0], 1.0 / l_ref[h1])
                o_ref[:, sl] = acc_ref[:, sl] * inv
                lse0 = m_ref[h0] + jnp.log(l_ref[h0]) - cq_ref[h0:h0 + 1, 0:1]
                lse1 = m_ref[h1] + jnp.log(l_ref[h1]) - cq_ref[h1:h1 + 1, 0:1]
                lse_ref[:, sl] = jnp.where(half, lse0, lse1)

    def qrow(b, qi, ki):
        return (b * nq + qi, 0)

    def krow(b, qi, ki):
        return (b * nq + jnp.minimum(ki, qi), 0)

    o, lse = pl.pallas_call(
        body, name=name, grid=(cfg.batch, nq, nq),
        in_specs=[pl.BlockSpec((tq, d), qrow),
                  pl.BlockSpec((tq, d), krow),
                  pl.BlockSpec((tq, d), lambda b, qi, ki: (b * nq + jnp.minimum(ki, qi), 1)),
                  pl.BlockSpec((hrows, tq), lambda b, qi, ki: (0, b * nq + qi)),
                  pl.BlockSpec((hrows, tq), lambda b, qi, ki: (0, b * nq + jnp.minimum(ki, qi)))],
        out_specs=[pl.BlockSpec((tq, d), qrow), pl.BlockSpec((tq, d), qrow)],
        out_shape=[jax.ShapeDtypeStruct((t, d), F32), jax.ShapeDtypeStruct((t, d), F32)],
        scratch_shapes=[pltpu.VMEM((cfg.heads, tq, LANES), F32), pltpu.VMEM((cfg.heads, tq, LANES), F32),
                        pltpu.VMEM((tq, d), F32)],
        compiler_params=_params(("parallel", "parallel", "arbitrary")),
    )(q, kv, kv, cum_t, cum_t)
    return o, lse


def _fox_bwd_q(q, kv, cum_t, do, lse, dsum, cfg, *, name):
    t, d, hrows = cfg.tokens, cfg.d_model, cum_t.shape[0]
    tq = _blk(cfg.seq, FOX_BLOCK)
    nq = cfg.seq // tq

    def body(q_ref, k_ref, v_ref, cq_ref, ck_ref, do_ref, l_ref, ds_ref, dq_ref, acc_ref):
        qi, ki = pl.program_id(1), pl.program_id(2)

        @pl.when(ki == 0)
        def _():
            acc_ref[...] = jnp.zeros_like(acc_ref)

        @pl.when(ki <= qi)
        def _():
            mask = _causal(qi, ki, tq)
            half = _half_mask((tq, LANES))
            for hp in range(d // LANES):
                sl = slice(hp * LANES, (hp + 1) * LANES)
                q2, k2, v2 = q_ref[:, sl], k_ref[:, sl], v_ref[:, sl]
                do2 = do_ref[:, sl].astype(BF16)
                dqs = []
                for e in range(2):
                    h = 2 * hp + e
                    lane0 = hp * LANES + e * HEAD_DIM
                    cref = cq_ref[h:h + 1, 0:1]
                    s = _fox_scores(q2, k2, e, half, mask, cref, ck_ref[h:h + 1, :])
                    p = jnp.exp(s - (l_ref[:, lane0:lane0 + 1] + cref))
                    doe = jnp.where(half == (e == 0), do2, jnp.zeros_like(do2))
                    dp = lax.dot_general(doe, v2, _NT, preferred_element_type=F32)
                    dsc = (p * (dp - ds_ref[:, lane0:lane0 + 1])).astype(BF16)
                    dqs.append(jnp.dot(dsc, k2, preferred_element_type=F32))
                acc_ref[:, sl] += jnp.where(half, dqs[0], dqs[1])

        @pl.when(ki == qi)
        def _():
            dq_ref[...] = acc_ref[...]

    def qrow(b, qi, ki):
        return (b * nq + qi, 0)

    return pl.pallas_call(
        body, name=name, grid=(cfg.batch, nq, nq),
        in_specs=[pl.BlockSpec((tq, d), qrow),
                  pl.BlockSpec((tq, d), lambda b, qi, ki: (b * nq + jnp.minimum(ki, qi), 0)),
                  pl.BlockSpec((tq, d), lambda b, qi, ki: (b * nq + jnp.minimum(ki, qi), 1)),
                  pl.BlockSpec((hrows, tq), lambda b, qi, ki: (0, b * nq + qi)),
                  pl.BlockSpec((hrows, tq), lambda b, qi, ki: (0, b * nq + jnp.minimum(ki, qi))),
                  pl.BlockSpec((tq, d), qrow), pl.BlockSpec((tq, d), qrow), pl.BlockSpec((tq, d), qrow)],
        out_specs=pl.BlockSpec((tq, d), qrow),
        out_shape=jax.ShapeDtypeStruct((t, d), F32),
        scratch_shapes=[pltpu.VMEM((tq, d), F32)],
        compiler_params=_params(("parallel", "parallel", "arbitrary")),
    )(q, kv, kv, cum_t, cum_t, do, lse, dsum)


def _fox_bwd_kv(q, kv, cum_t, do, lse, dsum, cfg, *, name):
    t, d, hrows = cfg.tokens, cfg.d_model, cum_t.shape[0]
    tq = _blk(cfg.seq, FOX_BLOCK)
    nq = cfg.seq // tq

    def body(q_ref, k_ref, v_ref, cq_ref, ck_ref, do_ref, l_ref, ds_ref, dk_ref, dv_ref, dc_ref,
             kacc_ref, vacc_ref, cacc_ref):
        ki, qi = pl.program_id(1), pl.program_id(2)

        @pl.when(qi == 0)
        def _():
            kacc_ref[...] = jnp.zeros_like(kacc_ref)
            vacc_ref[...] = jnp.zeros_like(vacc_ref)
            cacc_ref[...] = jnp.zeros_like(cacc_ref)

        @pl.when(qi >= ki)
        def _():
            mask = _causal(qi, ki, tq)
            half = _half_mask((tq, LANES))
            for hp in range(d // LANES):
                sl = slice(hp * LANES, (hp + 1) * LANES)
                q2, k2, v2 = q_ref[:, sl], k_ref[:, sl], v_ref[:, sl]
                do2 = do_ref[:, sl].astype(BF16)
                dks, dvs = [], []
                for e in range(2):
                    h = 2 * hp + e
                    lane0 = hp * LANES + e * HEAD_DIM
                    cref = cq_ref[h:h + 1, 0:1]
                    s = _fox_scores(q2, k2, e, half, mask, cref, ck_ref[h:h + 1, :])
                    p = jnp.exp(s - (l_ref[:, lane0:lane0 + 1] + cref))
                    doe = jnp.where(half == (e == 0), do2, jnp.zeros_like(do2))
                    dp = lax.dot_general(doe, v2, _NT, preferred_element_type=F32)
                    dsf = p * (dp - ds_ref[:, lane0:lane0 + 1])
                    cacc_ref[h:h + 1, :] -= jnp.sum(dsf, axis=0, keepdims=True)
                    dks.append(lax.dot_general(dsf.astype(BF16), q2, _TN, preferred_element_type=F32))
                    dvs.append(lax.dot_general(p.astype(BF16), do2, _TN, preferred_element_type=F32))
                kacc_ref[:, sl] += jnp.where(half, dks[0], dks[1])
                vacc_ref[:, sl] += jnp.where(half, dvs[0], dvs[1])

        @pl.when(qi == nq - 1)
        def _():
            dk_ref[...] = kacc_ref[...]
            dv_ref[...] = vacc_ref[...]
            dc_ref[...] = cacc_ref[...]

    def qrow(b, ki, qi):
        return (b * nq + jnp.maximum(qi, ki), 0)

    def krow(b, ki, qi):
        return (b * nq + ki, 0)

    return pl.pallas_call(
        body, name=name, grid=(cfg.batch, nq, nq),
        in_specs=[pl.BlockSpec((tq, d), qrow),
                  pl.BlockSpec((tq, d), krow),
                  pl.BlockSpec((tq, d), lambda b, ki, qi: (b * nq + ki, 1)),
                  pl.BlockSpec((hrows, tq), lambda b, ki, qi: (0, b * nq + jnp.maximum(qi, ki))),
                  pl.BlockSpec((hrows, tq), lambda b, ki, qi: (0, b * nq + ki)),
                  pl.BlockSpec((tq, d), qrow), pl.BlockSpec((tq, d), qrow), pl.BlockSpec((tq, d), qrow)],
        out_specs=[pl.BlockSpec((tq, d), krow), pl.BlockSpec((tq, d), krow),
                   pl.BlockSpec((hrows, tq), lambda b, ki, qi: (0, b * nq + ki))],
        out_shape=[jax.ShapeDtypeStruct((t, d), F32), jax.ShapeDtypeStruct((t, d), F32),
                   jax.ShapeDtypeStruct((hrows, t), F32)],
        scratch_shapes=[pltpu.VMEM((tq, d), F32), pltpu.VMEM((tq, d), F32), pltpu.VMEM((hrows, tq), F32)],
        compiler_params=_params(("parallel", "parallel", "arbitrary")),
    )(q, kv, kv, cum_t, cum_t, do, lse, dsum)


AUG = LANES
BIAS_TERMS = 3


def _fox_aug_q(qp, cfg):
    t, hh = cfg.tokens, cfg.heads
    q3 = qp.reshape(t, hh, HEAD_DIM)
    ones = jnp.ones((t, hh, BIAS_TERMS), BF16)
    zeros = jnp.zeros((t, hh, AUG - HEAD_DIM - BIAS_TERMS), BF16)
    return jnp.concatenate([q3, ones, zeros], axis=2).reshape(t, hh * AUG).T


def _fox_aug_k(k, cum, cfg):
    t, hh = cfg.tokens, cfg.heads
    c = -cum
    hi = lax.reduce_precision(c, 8, 7)
    mid = lax.reduce_precision(c - hi, 8, 7)
    lo = c - hi - mid
    zeros = jnp.zeros((t, hh, AUG - HEAD_DIM - BIAS_TERMS), BF16)
    parts = [k.reshape(t, hh, HEAD_DIM)] + [z.astype(BF16)[..., None] for z in (hi, mid, lo)] + [zeros]
    return jnp.concatenate(parts, axis=2).reshape(t, hh * AUG)


def _fox_aug_k_call(kv, cum, cfg, *, name):
    t, d, hh = cfg.tokens, cfg.d_model, cfg.heads
    tr = _blk(t, ROW_BLOCK)

    def body(k_ref, c_ref, o_ref):
        lane = lax.broadcasted_iota(jnp.int32, (tr, LANES), 1)
        for hp in range(hh // 2):
            k2 = k_ref[:, hp * LANES:(hp + 1) * LANES].astype(F32)
            for e in range(2):
                h = 2 * hp + e
                kh = k2 if e == 0 else pltpu.roll(k2, HEAD_DIM, 1)
                c = -c_ref[:, h:h + 1]
                hi = c.astype(BF16).astype(F32)
                mid = (c - hi).astype(BF16).astype(F32)
                lo = c - hi - mid
                bias = jnp.where(lane == HEAD_DIM, hi, jnp.where(lane == HEAD_DIM + 1, mid,
                                 jnp.where(lane == HEAD_DIM + 2, lo, 0.0)))
                o_ref[:, h * AUG:(h + 1) * AUG] = jnp.where(lane < HEAD_DIM, kh, bias).astype(BF16)

    return pl.pallas_call(
        body, name=name, grid=(t // tr,),
        in_specs=[pl.BlockSpec((tr, d), lambda i: (i, 0)), pl.BlockSpec((tr, LANES), lambda i: (i, 0))],
        out_specs=pl.BlockSpec((tr, hh * AUG), lambda i: (i, 0)),
        out_shape=jax.ShapeDtypeStruct((t, hh * AUG), BF16),
        compiler_params=_params(("parallel",)),
    )(kv, cum)


def _keys_visible(tq):
    s = lax.broadcasted_iota(jnp.int32, (tq, tq), 0)
    t = lax.broadcasted_iota(jnp.int32, (tq, tq), 1)
    return s <= t


def _fox_fwd_t(qa_t, k_aug, v_t, cfg, *, name):
    t, d, hh = cfg.tokens, cfg.d_model, cfg.heads
    tq = _blk(cfg.seq, FOX_BLOCK)
    nq = cfg.seq // tq

    def body(qa_ref, ka_ref, vt_ref, o_ref, lse_ref, m_ref, l_ref, acc_ref):
        qi, ki = pl.program_id(1), pl.program_id(2)

        @pl.when(ki == 0)
        def _():
            m_ref[...] = jnp.full_like(m_ref, NEG)
            l_ref[...] = jnp.zeros_like(l_ref)
            acc_ref[...] = jnp.zeros_like(acc_ref)

        def step(diagonal):
            for h in range(hh):
                rows = slice(h * HEAD_DIM, (h + 1) * HEAD_DIM)
                s = jnp.dot(ka_ref[:, h * AUG:(h + 1) * AUG], qa_ref[h * AUG:(h + 1) * AUG, :],
                            preferred_element_type=F32)
                if diagonal:
                    s = jnp.where(_keys_visible(tq), s, NEG)
                m_prev = m_ref[h:h + 1, :]
                m_new = jnp.maximum(m_prev, jnp.max(s, axis=0, keepdims=True))
                alpha = jnp.exp(m_prev - m_new)
                p = jnp.exp(s - m_new)
                l_ref[h:h + 1, :] = alpha * l_ref[h:h + 1, :] + jnp.sum(p, axis=0, keepdims=True)
                m_ref[h:h + 1, :] = m_new
                hi = p.astype(BF16)
                lo = (p - hi.astype(F32)).astype(BF16)
                vt = vt_ref[rows, :]
                acc_ref[rows, :] = (alpha * acc_ref[rows, :] + jnp.dot(vt, hi, preferred_element_type=F32)
                                    + jnp.dot(vt, lo, preferred_element_type=F32))

        pl.when(ki < qi)(functools.partial(step, False))
        pl.when(ki == qi)(functools.partial(step, True))

        @pl.when(ki == qi)
        def _():
            for h in range(hh):
                rows = slice(h * HEAD_DIM, (h + 1) * HEAD_DIM)
                o_ref[rows, :] = acc_ref[rows, :] * (1.0 / l_ref[h:h + 1, :])
            lse_ref[...] = m_ref[...] + jnp.log(l_ref[...])

    def qcol(b, qi, ki):
        return (0, b * nq + qi)

    return pl.pallas_call(
        body, name=name, grid=(cfg.batch, nq, nq),
        in_specs=[pl.BlockSpec((hh * AUG, tq), qcol),
                  pl.BlockSpec((tq, hh * AUG), lambda b, qi, ki: (b * nq + jnp.minimum(ki, qi), 0)),
                  pl.BlockSpec((d, tq), lambda b, qi, ki: (0, b * nq + jnp.minimum(ki, qi)))],
        out_specs=[pl.BlockSpec((d, tq), qcol), pl.BlockSpec((hh, tq), qcol)],
        out_shape=[jax.ShapeDtypeStruct((d, t), F32), jax.ShapeDtypeStruct((hh, t), F32)],
        scratch_shapes=[pltpu.VMEM((hh, tq), F32), pltpu.VMEM((hh, tq), F32), pltpu.VMEM((d, tq), F32)],
        compiler_params=_params(("parallel", "parallel", "arbitrary")),
    )(qa_t, k_aug, v_t)


def _aug_q_t(q2, e, tq):
    ones = (lax.broadcasted_iota(jnp.int32, (AUG - HEAD_DIM, tq), 0) < BIAS_TERMS).astype(q2.dtype)
    return jnp.concatenate([q2[e * HEAD_DIM:(e + 1) * HEAD_DIM], ones], axis=0)


def _fox_fwd_n(q, k_aug, v, cfg, *, name):
    t, d, hh = cfg.tokens, cfg.d_model, cfg.heads
    tq = _blk(cfg.seq, FOX_BLOCK)
    nq = cfg.seq // tq

    def body(q_ref, ka_ref, v_ref, o_ref, lse_ref, qa_ref, m_ref, l_ref, acc_ref):
        qi, ki = pl.program_id(1), pl.program_id(2)

        @pl.when(ki == 0)
        def _():
            m_ref[...] = jnp.full_like(m_ref, NEG)
            l_ref[...] = jnp.zeros_like(l_ref)
            acc_ref[...] = jnp.zeros_like(acc_ref)
            for hp in range(hh // 2):
                q2 = q_ref[:, hp * LANES:(hp + 1) * LANES].T
                for e in range(2):
                    h = 2 * hp + e
                    qa_ref[h * AUG:(h + 1) * AUG, :] = _aug_q_t(q2, e, tq)

        def step(diagonal):
            for hp in range(hh // 2):
                vt2 = v_ref[:, hp * LANES:(hp + 1) * LANES].T
                for e in range(2):
                    h = 2 * hp + e
                    rows = slice(h * HEAD_DIM, (h + 1) * HEAD_DIM)
                    s = jnp.dot(ka_ref[:, h * AUG:(h + 1) * AUG], qa_ref[h * AUG:(h + 1) * AUG, :],
                                preferred_element_type=F32)
                    if diagonal:
                        s = jnp.where(_keys_visible(tq), s, NEG)
                    m_prev = m_ref[h:h + 1, :]
                    m_new = jnp.maximum(m_prev, jnp.max(s, axis=0, keepdims=True))
                    alpha = jnp.exp(m_prev - m_new)
                    p = jnp.exp(s - m_new)
                    l_ref[h:h + 1, :] = alpha * l_ref[h:h + 1, :] + jnp.sum(p, axis=0, keepdims=True)
                    m_ref[h:h + 1, :] = m_new
                    hi = p.astype(BF16)
                    lo = (p - hi.astype(F32)).astype(BF16)
                    vt = vt2[e * HEAD_DIM:(e + 1) * HEAD_DIM]
                    acc_ref[rows, :] = (alpha * acc_ref[rows, :] + jnp.dot(vt, hi, preferred_element_type=F32)
                                        + jnp.dot(vt, lo, preferred_element_type=F32))

        pl.when(ki < qi)(functools.partial(step, False))
        pl.when(ki == qi)(functools.partial(step, True))

        @pl.when(ki == qi)
        def _():
            for hp in range(hh // 2):
                halves = [acc_ref[h * HEAD_DIM:(h + 1) * HEAD_DIM, :] * (1.0 / l_ref[h:h + 1, :])
                          for h in (2 * hp, 2 * hp + 1)]
                o_ref[:, hp * LANES:(hp + 1) * LANES] = jnp.concatenate(halves, axis=0).T
            lse_ref[...] = m_ref[...] + jnp.log(l_ref[...])

    def qrow(b, qi, ki):
        return (b * nq + qi, 0)

    def krow(b, qi, ki):
        return (b * nq + jnp.minimum(ki, qi), 0)

    return pl.pallas_call(
        body, name=name, grid=(cfg.batch, nq, nq),
        in_specs=[pl.BlockSpec((tq, d), qrow), pl.BlockSpec((tq, hh * AUG), krow), pl.BlockSpec((tq, d), krow)],
        out_specs=[pl.BlockSpec((tq, d), qrow), pl.BlockSpec((hh, tq), lambda b, qi, ki: (0, b * nq + qi))],
        out_shape=[jax.ShapeDtypeStruct((t, d), F32), jax.ShapeDtypeStruct((hh, t), F32)],
        scratch_shapes=[pltpu.VMEM((hh * AUG, tq), BF16), pltpu.VMEM((hh, tq), F32), pltpu.VMEM((hh, tq), F32),
                        pltpu.VMEM((d, tq), F32)],
        compiler_params=_params(("parallel", "parallel", "arbitrary")),
    )(q, k_aug, v)


def _head_dot_c(a, b, cfg, *, name):
    t, d, hh = cfg.tokens, cfg.d_model, cfg.heads
    tc = _blk(t, ROW_BLOCK)

    def body(a_ref, b_ref, o_ref):
        for hp in range(hh // 2):
            pair = slice(hp * LANES, (hp + 1) * LANES)
            prod = (a_ref[:, pair].astype(F32) * b_ref[:, pair]).T
            for e in range(2):
                h = 2 * hp + e
                o_ref[h:h + 1, :] = jnp.sum(prod[e * HEAD_DIM:(e + 1) * HEAD_DIM], axis=0, keepdims=True)

    return pl.pallas_call(
        body, name=name, grid=(t // tc,),
        in_specs=[pl.BlockSpec((tc, d), lambda i: (i, 0)), pl.BlockSpec((tc, d), lambda i: (i, 0))],
        out_specs=pl.BlockSpec((hh, tc), lambda i: (0, i)),
        out_shape=jax.ShapeDtypeStruct((hh, t), F32),
        compiler_params=_params(("parallel",)),
    )(a, b)


def _fox_bwd_n(q, k_aug, k_t, v, do, lse, dsum, cfg, *, name):
    t, d, hh = cfg.tokens, cfg.d_model, cfg.heads
    tq = _blk(cfg.seq, FOX_BLOCK)
    nq = cfg.seq // tq

    def body(q_ref, ka_ref, kt_ref, v_ref, do_ref, lse_ref, ds_ref, dq_hbm, dk_ref, dv_ref, dc_ref, dq_acc, sem):
        b, ki, qi = pl.program_id(0), pl.program_id(1), pl.program_id(2)
        qq = jnp.maximum(qi, ki)

        @pl.when((ki == 0) & (qi == 0))
        def _():
            dq_acc[...] = jnp.zeros_like(dq_acc)

        @pl.when(qi == 0)
        def _():
            dk_ref[...] = jnp.zeros_like(dk_ref)
            dv_ref[...] = jnp.zeros_like(dv_ref)
            dc_ref[...] = jnp.zeros_like(dc_ref)

        def step(diagonal):
            upper = lax.broadcasted_iota(jnp.int32, (LANES, tq), 0) < HEAD_DIM
            half = _half_mask((tq, LANES))
            for hp in range(hh // 2):
                pair = slice(hp * LANES, (hp + 1) * LANES)
                q2 = q_ref[:, pair].T
                don2 = do_ref[:, pair]
                dot2 = don2.T
                dvs = []
                for e in range(2):
                    h = 2 * hp + e
                    rows = slice(h * HEAD_DIM, (h + 1) * HEAD_DIM)
                    aug = slice(h * AUG, (h + 1) * AUG)
                    qa = _aug_q_t(q2, e, tq)
                    s = jnp.dot(ka_ref[:, aug], qa, preferred_element_type=F32)
                    if diagonal:
                        s = jnp.where(_keys_visible(tq), s, NEG)
                    p = jnp.exp(s - lse_ref[h:h + 1, :])
                    dote = jnp.where(upper == (e == 0), dot2, jnp.zeros_like(dot2))
                    dp = jnp.dot(v_ref[:, pair], dote, preferred_element_type=F32)
                    dsf = p * (dp - ds_ref[h:h + 1, :])
                    dc_ref[:, h:h + 1] -= jnp.sum(dsf, axis=1, keepdims=True)
                    dsc = dsf.astype(BF16)
                    dvs.append(jnp.dot(p.astype(BF16), don2, preferred_element_type=F32))
                    dk_ref[:, aug] += lax.dot_general(dsc, qa, _NT, preferred_element_type=F32)
                    dq_acc[qq, rows, :] += jnp.dot(kt_ref[rows, :], dsc, preferred_element_type=F32)
                dv_ref[:, pair] += jnp.where(half, dvs[0], dvs[1])

        pl.when(qi > ki)(functools.partial(step, False))
        pl.when(qi == ki)(functools.partial(step, True))

        @pl.when((ki == nq - 1) & (qi == nq - 1))
        def _():
            cp = pltpu.make_async_copy(dq_acc, dq_hbm.at[b], sem)
            cp.start()
            cp.wait()

    def qrow(b, ki, qi):
        return (b * nq + jnp.maximum(qi, ki), 0)

    def qcol(b, ki, qi):
        return (0, b * nq + jnp.maximum(qi, ki))

    def krow(b, ki, qi):
        return (b * nq + ki, 0)

    return pl.pallas_call(
        body, name=name, grid=(cfg.batch, nq, nq),
        in_specs=[pl.BlockSpec((tq, d), qrow),
                  pl.BlockSpec((tq, hh * AUG), krow),
                  pl.BlockSpec((d, tq), lambda b, ki, qi: (0, b * nq + ki)),
                  pl.BlockSpec((tq, d), krow),
                  pl.BlockSpec((tq, d), qrow),
                  pl.BlockSpec((hh, tq), qcol), pl.BlockSpec((hh, tq), qcol)],
        out_specs=[pl.BlockSpec(memory_space=pl.ANY), pl.BlockSpec((tq, hh * AUG), krow),
                   pl.BlockSpec((tq, d), krow), pl.BlockSpec((tq, LANES), krow)],
        out_shape=[jax.ShapeDtypeStruct((cfg.batch, nq, d, tq), F32), jax.ShapeDtypeStruct((t, hh * AUG), F32),
                   jax.ShapeDtypeStruct((t, d), F32), jax.ShapeDtypeStruct((t, LANES), F32)],
        scratch_shapes=[pltpu.VMEM((nq, d, tq), F32), pltpu.SemaphoreType.DMA],
        compiler_params=_params(("arbitrary", "arbitrary", "arbitrary")),
    )(q, k_aug, k_t, v, do, lse, dsum)


def _head_dot_t(a_t, b_t, cfg, *, name):
    t, d, hh = cfg.tokens, cfg.d_model, cfg.heads
    tc = _blk(t, 2 * ROW_BLOCK)

    def body(a_ref, b_ref, o_ref):
        for h in range(hh):
            rows = slice(h * HEAD_DIM, (h + 1) * HEAD_DIM)
            o_ref[h:h + 1, :] = jnp.sum(a_ref[rows, :].astype(F32) * b_ref[rows, :], axis=0, keepdims=True)

    return pl.pallas_call(
        body, name=name, grid=(t // tc,),
        in_specs=[pl.BlockSpec((d, tc), lambda i: (0, i)), pl.BlockSpec((d, tc), lambda i: (0, i))],
        out_specs=pl.BlockSpec((hh, tc), lambda i: (0, i)),
        out_shape=jax.ShapeDtypeStruct((hh, t), F32),
        compiler_params=_params(("parallel",)),
    )(a_t, b_t)


def _fox_bwd_t(qa_t, k_aug, k_t, v, do_t, do, lse, dsum, cfg, *, name):
    t, d, hh = cfg.tokens, cfg.d_model, cfg.heads
    tq = _blk(cfg.seq, FOX_BLOCK)
    nq = cfg.seq // tq

    def body(qa_ref, ka_ref, kt_ref, v_ref, dot_ref, do_ref, lse_ref, ds_ref, dq_hbm, dk_ref, dv_ref, dc_ref,
             dq_acc, sem):
        b, ki, qi = pl.program_id(0), pl.program_id(1), pl.program_id(2)
        qq = jnp.maximum(qi, ki)

        @pl.when((ki == 0) & (qi == 0))
        def _():
            dq_acc[...] = jnp.zeros_like(dq_acc)

        @pl.when(qi == 0)
        def _():
            dk_ref[...] = jnp.zeros_like(dk_ref)
            dv_ref[...] = jnp.zeros_like(dv_ref)
            dc_ref[...] = jnp.zeros_like(dc_ref)

        def step(diagonal):
            upper = lax.broadcasted_iota(jnp.int32, (LANES, tq), 0) < HEAD_DIM
            half = _half_mask((tq, LANES))
            for hp in range(hh // 2):
                pair = slice(hp * LANES, (hp + 1) * LANES)
                dvs = []
                for e in range(2):
                    h = 2 * hp + e
                    rows = slice(h * HEAD_DIM, (h + 1) * HEAD_DIM)
                    aug = slice(h * AUG, (h + 1) * AUG)
                    s = jnp.dot(ka_ref[:, aug], qa_ref[aug, :], preferred_element_type=F32)
                    if diagonal:
                        s = jnp.where(_keys_visible(tq), s, NEG)
                    p = jnp.exp(s - lse_ref[h:h + 1, :])
                    dot2 = dot_ref[pair, :]
                    dote = jnp.where(upper == (e == 0), dot2, jnp.zeros_like(dot2))
                    dp = jnp.dot(v_ref[:, pair], dote, preferred_element_type=F32)
                    dsf = p * (dp - ds_ref[h:h + 1, :])
                    dc_ref[:, h:h + 1] -= jnp.sum(dsf, axis=1, keepdims=True)
                    dsc = dsf.astype(BF16)
                    dvs.append(jnp.dot(p.astype(BF16), do_ref[:, pair], preferred_element_type=F32))
                    dk_ref[:, aug] += lax.dot_general(dsc, qa_ref[aug, :], _NT, preferred_element_type=F32)
                    dq_acc[qq, rows, :] += jnp.dot(kt_ref[rows, :], dsc, preferred_element_type=F32)
                dv_ref[:, pair] += jnp.where(half, dvs[0], dvs[1])

        pl.when(qi > ki)(functools.partial(step, False))
        pl.when(qi == ki)(functools.partial(step, True))

        @pl.when((ki == nq - 1) & (qi == nq - 1))
        def _():
            cp = pltpu.make_async_copy(dq_acc, dq_hbm.at[b], sem)
            cp.start()
            cp.wait()

    def qcol(b, ki, qi):
        return (0, b * nq + jnp.maximum(qi, ki))

    def krow(b, ki, qi):
        return (b * nq + ki, 0)

    return pl.pallas_call(
        body, name=name, grid=(cfg.batch, nq, nq),
        in_specs=[pl.BlockSpec((hh * AUG, tq), qcol),
                  pl.BlockSpec((tq, hh * AUG), krow),
                  pl.BlockSpec((d, tq), lambda b, ki, qi: (0, b * nq + ki)),
                  pl.BlockSpec((tq, d), krow),
                  pl.BlockSpec((d, tq), qcol),
                  pl.BlockSpec((tq, d), lambda b, ki, qi: (b * nq + jnp.maximum(qi, ki), 0)),
                  pl.BlockSpec((hh, tq), qcol), pl.BlockSpec((hh, tq), qcol)],
        out_specs=[pl.BlockSpec(memory_space=pl.ANY), pl.BlockSpec((tq, hh * AUG), krow),
                   pl.BlockSpec((tq, d), krow), pl.BlockSpec((tq, LANES), krow)],
        out_shape=[jax.ShapeDtypeStruct((cfg.batch, nq, d, tq), F32), jax.ShapeDtypeStruct((t, hh * AUG), F32),
                   jax.ShapeDtypeStruct((t, d), F32), jax.ShapeDtypeStruct((t, LANES), F32)],
        scratch_shapes=[pltpu.VMEM((nq, d, tq), F32), pltpu.SemaphoreType.DMA],
        compiler_params=_params(("arbitrary", "arbitrary", "arbitrary")),
    )(qa_t, k_aug, k_t, v, do_t, do, lse, dsum)


WIDE = 1536


def _fwd(a, w, *, name, res=None, scale=1.0):
    return _mm(a, w, form="F", out_dtype=F32, name=name, bn=WIDE, bk=WIDE, res=res, scale=scale)


def _bwd(dy, w, *, name, scale=1.0):
    return _mm(dy, w, form="B", out_dtype=F32, name=name, bn=WIDE, bk=WIDE, scale=scale)


def _wgrad(a, dy, w, *, name, scale=1.0):
    return _mm_grad(a, dy, w.shape[0], name=name, bm=WIDE, bn=WIDE, scale=scale)


def _ffn_fwd(h, g, w_in, w_out, tag):
    n = _rms_fwd(h, g, name=f"{tag}_norm")
    gate, up, a = _ffn_in_act(n, w_in, name=f"{tag}_in")
    return _fwd(a, w_out, name=f"{tag}_out", res=h, scale=0.5), (n, gate, up, a)


def _ffn_bwd(dh_out, h, g, w_in, w_out, saved, tag):
    n, gate, up, a = saved
    du = _ffn_out_dx_act(dh_out, w_out, gate, up, name=f"{tag}_out_dx", scale=0.5)
    dw_out = _wgrad(a, dh_out, w_out, name=f"{tag}_out_dw", scale=0.5)
    dn = _mm_back2(du, w_in, name=f"{tag}_in_dx", bn=WIDE, bk=WIDE)
    dw_in = _mm_grad(n, du, w_in.shape[0], name=f"{tag}_in_dw", bm=WIDE, bn=WIDE)
    dh, dg = _rms_bwd(h, g, dn, dh_out, name=f"{tag}_norm_bwd")
    return dh, dg, dw_in, dw_out


def _head_gain(g, heads, scale=1.0):
    return jnp.tile(g.astype(F32) * scale, heads)


def _local_step(cfg, x, positions, target, w, s):
    d, hh = cfg.d_model, cfg.heads
    cos, sin = _rope_tables(positions)
    ones = jnp.ones((d,), F32)

    h1, ffn0 = _ffn_fwd(x, s["ffn_norm"][0, 0], w["ffn_w_in"][0][0], w["ffn_w_out"][0][0], "ffn00")
    hn_a = _rms_fwd(h1, s["mix_norm"][0], name="a_norm")
    qkv = _fwd(hn_a, w["a_w_qkv"], name="a_qkv")
    kinds_a = ["rope", "rope", "cast"] * len(DILATIONS)
    gains_a = jnp.stack([z for g in range(len(DILATIONS)) for z in (
        _head_gain(s["a_q_norm"][g], hh, Q_SCALE), _head_gain(s["a_k_norm"][g], hh), ones)])
    qkvp = [_hn_fwd(qkv, gains_a[3 * g:3 * g + 3], kinds_a[:3], d, cos, sin, name=f"a_qk_norm{g}", col0=3 * g)
            for g in range(len(DILATIONS))]
    lay = [qkvp[g].reshape(cfg.tokens // dil, dil * 3 * d) for g, dil in enumerate(DILATIONS)]
    band = [_band_fwd_n(lay[g], dil, cfg, name=f"a_band{g}") for g, dil in enumerate(DILATIONS)]
    mixed, lse_a = _mix_fwd([o.reshape(cfg.tokens, d) for o, _ in band],
                            [jnp.repeat(_from_classes_t(l), HEAD_DIM, axis=1) for _, l in band], name="a_mix")
    h2 = _fwd(mixed, w["a_w_o"], name="a_out", res=h1)
    h3, ffn1 = _ffn_fwd(h2, s["ffn_norm"][0, 1], w["ffn_w_in"][0][1], w["ffn_w_out"][0][1], "ffn01")

    kn = _rms_fwd(h3, s["kv_norm"], name="kv_norm")
    proj = _fwd(kn, w["kv_w"], name="kv_proj")
    kinds_kv = ["norm", "cast"]
    gains_kv = jnp.stack([_head_gain(s["kv_k_norm"], hh), ones])
    kvp = _hn_fwd(proj, gains_kv, kinds_kv, d, cos, sin, name="kv_k_norm")
    gate_col = 2 * d // LANES
    bias = jnp.pad(s["kv_b_f"].astype(F32), (0, LANES - hh))
    cum = _gate_fwd(proj, gate_col, bias, cfg, name="kv_gate")
    k_b, v_b = kvp[:, :d], kvp[:, d:]
    k_aug = _fox_aug_k_call(kvp, cum, cfg, name="kv_aug")

    h4, ffn2 = _ffn_fwd(h3, s["ffn_norm"][1, 0], w["ffn_w_in"][1][0], w["ffn_w_out"][1][0], "ffn10")
    hn_b = _rms_fwd(h4, s["mix_norm"][1], name="b_norm")
    qraw = _fwd(hn_b, w["b_w_q"], name="b_q")
    gains_b = _head_gain(s["b_q_norm"][0], hh, Q_SCALE)[None]
    qp = _hn_fwd(qraw, gains_b, ["norm"], d, cos, sin, name="b_q_norm")
    o_b, lse_b = _fox_fwd_n(qp, k_aug, v_b, cfg, name="b_fox")
    h5 =_fwd(o_b, w["b_w_o"], name="b_out", res=h4)
    h6, ffn3 = _ffn_fwd(h5, s["ffn_norm"][1, 1], w["ffn_w_in"][1][1], w["ffn_w_out"][1][1], "ffn11")

    loss, dh6 = _loss_fwd_bwd(h6, target, name="loss")

    dh5, dg11, dwi11, dwo11 = _ffn_bwd(dh6, h5, s["ffn_norm"][1, 1], w["ffn_w_in"][1][1], w["ffn_w_out"][1][1],
                                       ffn3, "ffn11")
    do_b = _bwd(dh5, w["b_w_o"], name="b_out_dx")
    dw_bo = _wgrad(o_b, dh5, w["b_w_o"], name="b_out_dw")
    do_bf = do_b.astype(BF16)
    dsum_b = _head_dot_c(do_bf, o_b, cfg, name="b_dsum")
    dq4, dk_aug, dv_b, dcum = _fox_bwd_n(qp, k_aug, k_b.T, v_b, do_bf, lse_b, dsum_b, cfg, name="b_fox_bwd")
    dq_b = dq4.transpose(0, 1, 3, 2).reshape(cfg.tokens, d)
    dk_b = dk_aug.reshape(cfg.tokens, hh, AUG)[:, :, :HEAD_DIM].reshape(cfg.tokens, d)
    dqraw, dgq = _hn_bwd(qraw, [dq_b], gains_b, ["norm"], d, cos, sin, name="b_q_norm_bwd")
    dhn_b = _bwd(dqraw, w["b_w_q"], name="b_q_dx")
    dw_bq = _wgrad(hn_b, dqraw, w["b_w_q"], name="b_q_dw")
    dh4, dmix1 = _rms_bwd(h4, s["mix_norm"][1], dhn_b, dh5, name="b_norm_bwd")
    dh3, dg10, dwi10, dwo10 = _ffn_bwd(dh4, h3, s["ffn_norm"][1, 0], w["ffn_w_in"][1][0], w["ffn_w_out"][1][0],
                                       ffn2, "ffn10")

    dkvraw, dgk = _hn_bwd(proj, [dk_b, dv_b], gains_kv, kinds_kv, d, cos, sin,
                          name="kv_k_norm_bwd")
    dz, dbias = _gate_bwd(proj, gate_col, bias, dcum, cfg, name="kv_gate_bwd")
    pad_cols = w["kv_w"].shape[2] - 2 * d - LANES
    dproj = jnp.concatenate([dkvraw, dz.astype(BF16), jnp.zeros((cfg.tokens, pad_cols), BF16)], axis=1)
    dkn = _bwd(dproj, w["kv_w"], name="kv_proj_dx")
    dw_kv = _wgrad(kn, dproj, w["kv_w"], name="kv_proj_dw")
    dh3, dkvn = _rms_bwd(h3, s["kv_norm"], dkn, dh3, name="kv_norm_bwd")

    dh2, dg01, dwi01, dwo01 = _ffn_bwd(dh3, h2, s["ffn_norm"][0, 1], w["ffn_w_in"][0][1], w["ffn_w_out"][0][1],
                                       ffn1, "ffn01")
    dmixed = _bwd(dh2, w["a_w_o"], name="a_out_dx")
    dw_ao = _wgrad(mixed, dh2, w["a_w_o"], name="a_out_dw")
    dsum_a = _head_dot(dmixed, mixed, name="a_dsum")
    dqkvp = []
    dmixed_bf = dmixed.astype(BF16)
    lse_h, dsum_h = lse_a[:, ::HEAD_DIM], dsum_a[:, ::HEAD_DIM]
    for g, dil in enumerate(DILATIONS):
        grads = _band_bwd_n(lay[g], dmixed_bf.reshape(cfg.tokens // dil, dil * d), _to_classes_t(lse_h, dil, hh),
                            _to_classes_t(dsum_h, dil, hh), dil, cfg, name=f"a_band{g}_bwd")
        dqkvp += [z.reshape(cfg.tokens, d) for z in grads]
    dqkv, dga = _hn_bwd(qkv, dqkvp, gains_a, kinds_a, d, cos, sin, name="a_qk_norm_bwd")
    dhn_a = _bwd(dqkv, w["a_w_qkv"], name="a_qkv_dx")
    dw_qkv = _wgrad(hn_a, dqkv, w["a_w_qkv"], name="a_qkv_dw")
    dh1, dmix0 = _rms_bwd(h1, s["mix_norm"][0], dhn_a, dh2, name="a_norm_bwd")
    dx, dg00, dwi00, dwo00 = _ffn_bwd(dh1, x, s["ffn_norm"][0, 0], w["ffn_w_in"][0][0], w["ffn_w_out"][0][0],
                                      ffn0, "ffn00")

    dw = {
        "ffn_w_in": [[dwi00, dwi01], [dwi10, dwi11]],
        "ffn_w_out": [[dwo00, dwo01], [dwo10, dwo11]],
        "a_w_qkv": dw_qkv, "a_w_o": dw_ao, "kv_w": dw_kv, "b_w_q": dw_bq, "b_w_o": dw_bo,
    }
    ds = {
        "ffn_norm": jnp.stack([jnp.stack([dg00, dg01]), jnp.stack([dg10, dg11])]),
        "mix_norm": jnp.stack([dmix0, dmix1]),
        "a_q_norm": jnp.stack([dga[3 * g] for g in range(len(DILATIONS))])[None] * Q_SCALE,
        "a_k_norm": jnp.stack([dga[3 * g + 1] for g in range(len(DILATIONS))])[None],
        "kv_norm": dkvn,
        "kv_b_f": dbias[:hh],
        "kv_k_norm": dgk[0],
        "b_q_norm": dgq * Q_SCALE,
    }
    return loss, dx, dw, ds


MESH_ID = pl.DeviceIdType.MESH
ANY = pl.BlockSpec(memory_space=pl.ANY)
PACK_COLS = 1024
PACK_ROW_ALIGN = 32


def _me():
    return lax.axis_index("x"), lax.axis_index("y"), lax.axis_index("c")


def _other_chips(x, y):
    return [(1 - x, y), (x, 1 - y), (1 - x, 1 - y)]


def _all_gather_small(v, *, name):
    r = v.shape[0]

    def body(v_ref, out_ref, send_sems, recv_sems):
        x, y, c = _me()
        me = 4 * x + 2 * y + c
        out_ref[me] = v_ref[...]
        copies = []
        for k in range(1, N_DEV):
            fx, fy, fc = (k >> 2) & 1, (k >> 1) & 1, k & 1
            peer = (1 - x if fx else x, 1 - y if fy else y, 1 - c if fc else c)
            copies.append(pltpu.make_async_remote_copy(
                src_ref=v_ref, dst_ref=out_ref.at[me], send_sem=send_sems.at[k - 1], recv_sem=recv_sems.at[k - 1],
                device_id=peer, device_id_type=MESH_ID))
        for cp in copies:
            cp.start()
        for cp in copies:
            cp.wait()

    return pl.pallas_call(
        body, name=name,
        in_specs=[pl.BlockSpec(memory_space=pltpu.VMEM)], out_specs=pl.BlockSpec(memory_space=pltpu.VMEM),
        out_shape=jax.ShapeDtypeStruct((N_DEV, r, LANES), v.dtype),
        scratch_shapes=[pltpu.SemaphoreType.DMA((N_DEV - 1,)), pltpu.SemaphoreType.DMA((N_DEV - 1,))],
    )(v)


def _all_gather_chips(v, *, name):
    rh = v.shape[0] // 2

    def body(v_ref, out_ref, send_sems, recv_sems):
        x, y, c = _me()
        j = 2 * x + y
        chips = _other_chips(x, y)

        def half(chip, core):
            return out_ref.at[chip, pl.ds(core * rh, rh)]

        first = [pltpu.make_async_remote_copy(
            src_ref=v_ref.at[pl.ds(c * rh, rh)], dst_ref=half(j, c), send_sem=send_sems.at[k],
            recv_sem=recv_sems.at[k], device_id=(px, py, c), device_id_type=MESH_ID)
            for k, (px, py) in enumerate(chips)]
        for cp in first:
            cp.start()
        passed = [pltpu.make_async_remote_copy(
            src_ref=half(2 * px + py, c), dst_ref=half(2 * px + py, c), send_sem=send_sems.at[3 + k],
            recv_sem=recv_sems.at[3 + k], device_id=(x, y, 1 - c), device_id_type=MESH_ID)
            for k, (px, py) in enumerate(chips)]
        for k in range(len(chips)):
            first[k].wait_recv()
            passed[k].start()
        for k, (px, py) in enumerate(chips):
            pltpu.make_async_remote_copy(
                src_ref=half(2 * px + py, 1 - c), dst_ref=half(2 * px + py, 1 - c), send_sem=send_sems.at[3 + k],
                recv_sem=recv_sems.at[3 + k], device_id=(x, y, 1 - c), device_id_type=MESH_ID).wait_recv()
        for cp in first + passed:
            cp.wait_send()

    return pl.pallas_call(
        body, name=name, in_specs=[ANY], out_specs=ANY,
        out_shape=jax.ShapeDtypeStruct((N_CHIPS,) + v.shape, v.dtype),
        scratch_shapes=[pltpu.SemaphoreType.DMA((2 * (N_CHIPS - 1),)), pltpu.SemaphoreType.DMA((2 * (N_CHIPS - 1),))],
    )(v)


def _swap_halves(g, *, name):
    n, r, cols = g.shape
    rh = r // 2

    def body(g_ref, out_ref, send_sem, recv_sem):
        x, y, c = _me()
        cp = pltpu.make_async_remote_copy(
            src_ref=g_ref.at[:, pl.ds((1 - c) * rh, rh)], dst_ref=out_ref, send_sem=send_sem, recv_sem=recv_sem,
            device_id=(x, y, 1 - c), device_id_type=MESH_ID)
        cp.start()
        cp.wait()

    return pl.pallas_call(
        body, name=name, in_specs=[ANY], out_specs=ANY,
        out_shape=jax.ShapeDtypeStruct((n, rh, cols), g.dtype),
        scratch_shapes=[pltpu.SemaphoreType.DMA, pltpu.SemaphoreType.DMA],
    )(g)


def _scatter_chips(v, *, name):
    def body(v_ref, out_ref, send_sems, recv_sems):
        x, y, c = _me()
        j = 2 * x + y
        copies = [pltpu.make_async_remote_copy(
            src_ref=v_ref.at[2 * px + py], dst_ref=out_ref.at[j], send_sem=send_sems.at[k], recv_sem=recv_sems.at[k],
            device_id=(px, py, c), device_id_type=MESH_ID) for k, (px, py) in enumerate(_other_chips(x, y))]
        for cp in copies:
            cp.start()
        for cp in copies:
            cp.wait()

    return pl.pallas_call(
        body, name=name, in_specs=[ANY], out_specs=ANY,
        out_shape=jax.ShapeDtypeStruct(v.shape, v.dtype),
        scratch_shapes=[pltpu.SemaphoreType.DMA((N_CHIPS - 1,)), pltpu.SemaphoreType.DMA((N_CHIPS - 1,))],
    )(v)


def _join_halves(v, *, name):
    def body(v_ref, out_ref, send_sem, recv_sem):
        x, y, c = _me()
        cp = pltpu.make_async_remote_copy(
            src_ref=v_ref, dst_ref=out_ref.at[c], send_sem=send_sem, recv_sem=recv_sem,
            device_id=(x, y, 1 - c), device_id_type=MESH_ID)
        cp.start()
        cp.wait()

    return pl.pallas_call(
        body, name=name, in_specs=[ANY], out_specs=ANY,
        out_shape=jax.ShapeDtypeStruct((2,) + v.shape, v.dtype),
        scratch_shapes=[pltpu.SemaphoreType.DMA, pltpu.SemaphoreType.DMA],
    )(v)


def _row_blk(rows, want):
    for b in range(min(rows, want) // SUBLANES * SUBLANES, 0, -SUBLANES):
        if rows % b == 0:
            return b
    return rows


def _add_own_half(g, got, *, name):
    n, r, cols = g.shape
    rh = r // 2
    tr = _row_blk(rh, 512)
    nb = rh // tr

    def body(c_ref, g_ref, got_ref, o_ref):
        del c_ref
        o_ref[...] = (g_ref[...] + got_ref[...]).astype(BF16)

    grid_spec = pltpu.PrefetchScalarGridSpec(
        num_scalar_prefetch=1, grid=(n, nb),
        in_specs=[pl.BlockSpec((None, tr, cols), lambda j, i, c: (j, c[0] * nb + i, 0)),
                  pl.BlockSpec((None, tr, cols), lambda j, i, c: (j, i, 0))],
        out_specs=pl.BlockSpec((None, tr, cols), lambda j, i, c: (j, i, 0)))
    return pl.pallas_call(
        body, name=name, grid_spec=grid_spec, out_shape=jax.ShapeDtypeStruct((n, rh, cols), BF16),
        compiler_params=_params(("parallel", "parallel")),
    )(lax.axis_index("c").astype(jnp.int32).reshape(1), g, got)


def _sum_parts(parts, *, name):
    n, r, cols = parts.shape
    tr = _row_blk(r, 512)

    def body(*refs):
        o_ref = refs[n]
        acc = refs[0][...].astype(F32)
        for p_ref in refs[1:n]:
            acc = acc + p_ref[...].astype(F32)
        o_ref[...] = acc

    return pl.pallas_call(
        body, name=name, grid=(r // tr,),
        in_specs=[pl.BlockSpec((None, tr, cols), functools.partial(lambda j, i: (j, i, 0), j)) for j in range(n)],
        out_specs=pl.BlockSpec((tr, cols), lambda i: (i, 0)),
        out_shape=jax.ShapeDtypeStruct((r, cols), F32),
        compiler_params=_params(("parallel",)),
    )(*([parts] * n))


def _adamw(w, m, v, g, *, name):
    shape = w.shape
    cols = shape[-1]
    w2, m2, v2, g2 = (z.reshape(-1, cols) for z in (w, m, v, g))
    rows = w2.shape[0]
    tr = _row_blk(rows, max(SUBLANES, (1 << 20) // (4 * cols)))

    def body(w_ref, m_ref, v_ref, g_ref, d_ref, nm_ref, nv_ref):
        gv = g_ref[...]
        nm = ADAM_B1 * m_ref[...] + (1.0 - ADAM_B1) * gv
        nv = ADAM_B2 * v_ref[...] + (1.0 - ADAM_B2) * jnp.square(gv)
        m_hat = nm / (1.0 - ADAM_B1 ** ADAM_STEP)
        v_hat = nv / (1.0 - ADAM_B2 ** ADAM_STEP)
        d_ref[...] = -ADAM_LR * (m_hat / (jnp.sqrt(v_hat) + ADAM_EPS) + ADAM_WD * w_ref[...])
        nm_ref[...] = nm
        nv_ref[...] = nv

    spec = pl.BlockSpec((tr, cols), lambda i: (i, 0))
    out = jax.ShapeDtypeStruct((rows, cols), F32)
    d, nm, nv = pl.pallas_call(
        body, name=name, grid=(rows // tr,), in_specs=[spec] * 4, out_specs=[spec] * 3, out_shape=[out] * 3,
        compiler_params=_params(("parallel",)),
    )(w2, m2, v2, g2)
    return d.reshape(shape), nm.reshape(shape), nv.reshape(shape)


def _pack_rows(size, cols, align):
    return -(-size // (cols * align)) * align


def _pack(arrs, lead, cols, align, total_align):
    lead_shape = arrs[0].shape[:lead]
    parts = []
    for a in arrs:
        flat = a.reshape(lead_shape + (-1,))
        size = flat.shape[-1]
        rows = _pack_rows(size, cols, align)
        flat = jnp.pad(flat, [(0, 0)] * lead + [(0, rows * cols - size)])
        parts.append(flat.reshape(lead_shape + (rows, cols)))
    total = sum(p.shape[lead] for p in parts)
    extra = -total % total_align
    if extra:
        parts.append(jnp.zeros(lead_shape + (extra, cols), parts[0].dtype))
    return jnp.concatenate(parts, axis=lead)


def _unpack(buf, shapes, lead, cols, align):
    lead_shape = buf.shape[:lead]
    out, row = [], 0
    for shp in shapes:
        size = 1
        for n in shp:
            size *= n
        rows = _pack_rows(size, cols, align)
        piece = lax.slice_in_dim(buf, row, row + rows, axis=lead).reshape(lead_shape + (-1,))
        out.append(piece[..., :size].reshape(lead_shape + tuple(shp)))
        row += rows
    return out


BIG = ("ffn_w_in", "ffn_w_out", "a_w_qkv", "a_w_o", "kv_w", "b_w_q", "b_w_o")
SMALL = ("ffn_norm", "mix_norm", "a_q_norm", "a_k_norm", "kv_norm", "kv_b_f", "kv_k_norm", "b_q_norm")
WEIGHTS = ("ffn_norm", "ffn_w_in", "ffn_w_out", "mix_norm", "a_w_qkv", "a_q_norm", "a_k_norm", "a_w_o",
           "kv_norm", "kv_w", "kv_b_f", "kv_k_norm", "b_w_q", "b_q_norm", "b_w_o")
GATE_PAD = 2 * LANES


def _stack_weights(sh, d):
    depth = sh["ffn_w_in"].shape[1]
    kv = sh["kv_w"].transpose(1, 0, 2).reshape(d, -1)
    kv = jnp.pad(kv, ((0, 0), (0, 2 * d + GATE_PAD - kv.shape[1])))
    return {
        "ffn_w_in": [[sh["ffn_w_in"][:, l, i] for i in range(2)] for l in range(depth)],
        "ffn_w_out": [[sh["ffn_w_out"][:, l, i].reshape(1, -1, d) for i in range(2)] for l in range(depth)],
        "a_w_qkv": sh["a_w_qkv"][:, 0],
        "a_w_o": sh["a_w_o"].reshape(1, d, d),
        "kv_w": kv[None],
        "b_w_q": sh["b_w_q"].reshape(1, d, d),
        "b_w_o": sh["b_w_o"].reshape(1, d, d),
    }


def _unstack_grads(dw, d, heads):
    def rows4(z):
        return z.reshape(N_CHIPS, -1, d)

    kv_cols = 2 * d + heads
    kv = dw["kv_w"][0][:, :kv_cols].reshape(d, N_CHIPS, kv_cols // N_CHIPS).transpose(1, 0, 2)
    return [
        jnp.stack([jnp.stack(row, axis=1) for row in dw["ffn_w_in"]], axis=1),
        jnp.stack([jnp.stack([rows4(z) for z in row], axis=1) for row in dw["ffn_w_out"]], axis=1),
        dw["a_w_qkv"][:, None],
        rows4(dw["a_w_o"])[:, None],
        kv,
        rows4(dw["b_w_q"])[:, None],
        rows4(dw["b_w_o"])[:, None],
    ]


def kernel(x, positions, ffn_norm, ffn_w_in, ffn_w_out, mix_norm, a_w_qkv, a_q_norm, a_k_norm, a_w_o, kv_norm, kv_w, kv_b_f, kv_k_norm, b_w_q, b_q_norm, b_w_o, loss_target, m_ffn_norm, m_ffn_w_in, m_ffn_w_out, m_mix_norm, m_a_w_qkv, m_a_q_norm, m_a_k_norm, m_a_w_o, m_kv_norm, m_kv_w, m_kv_b_f, m_kv_k_norm, m_b_w_q, m_b_q_norm, m_b_w_o, v_ffn_norm, v_ffn_w_in, v_ffn_w_out, v_mix_norm, v_a_w_qkv, v_a_q_norm, v_a_k_norm, v_a_w_o, v_kv_norm, v_kv_w, v_kv_b_f, v_kv_k_norm, v_b_w_q, v_b_q_norm, v_b_w_o):
    wts = dict(zip(WEIGHTS, (ffn_norm, ffn_w_in, ffn_w_out, mix_norm, a_w_qkv, a_q_norm, a_k_norm, a_w_o, kv_norm,
                             kv_w, kv_b_f, kv_k_norm, b_w_q, b_q_norm, b_w_o)))
    mom = dict(zip(WEIGHTS, (m_ffn_norm, m_ffn_w_in, m_ffn_w_out, m_mix_norm, m_a_w_qkv, m_a_q_norm, m_a_k_norm,
                             m_a_w_o, m_kv_norm, m_kv_w, m_kv_b_f, m_kv_k_norm, m_b_w_q, m_b_q_norm, m_b_w_o)))
    var = dict(zip(WEIGHTS, (v_ffn_norm, v_ffn_w_in, v_ffn_w_out, v_mix_norm, v_a_w_qkv, v_a_q_norm, v_a_k_norm,
                             v_a_w_o, v_kv_norm, v_kv_w, v_kv_b_f, v_kv_k_norm, v_b_w_q, v_b_q_norm, v_b_w_o)))
    batch, seq, d = x.shape
    cfg = Cfg(d_model=d, d_ff=ffn_w_out.shape[2] * N_CHIPS, seq=seq, batch=batch)
    chip = 2 * lax.axis_index("x") + lax.axis_index("y")
    big_shapes = [wts[n].shape for n in BIG]

    shard = _pack([wts[n].astype(BF16) for n in BIG], 0, PACK_COLS, PACK_ROW_ALIGN, PACK_COLS)
    gathered = _all_gather_chips(shard, name="gather_weights")
    gathered = lax.dynamic_update_slice_in_dim(gathered, shard[None], chip, axis=0)
    w = _stack_weights(dict(zip(BIG, _unpack(gathered, big_shapes, 1, PACK_COLS, PACK_ROW_ALIGN))), d)
    norm_shard = _pack([ffn_norm], 0, LANES, SUBLANES, SUBLANES)
    norms = _all_gather_small(norm_shard, name="gather_ffn_norm")[0::2]
    (norms,) = _unpack(norms, [ffn_norm.shape], 1, LANES, SUBLANES)
    small = {"ffn_norm": jnp.moveaxis(norms, 0, 2).reshape(ffn_norm.shape[:2] + (d,)),
             "mix_norm": mix_norm, "a_q_norm": a_q_norm[0], "a_k_norm": a_k_norm[0], "kv_norm": kv_norm,
             "kv_b_f": kv_b_f, "kv_k_norm": kv_k_norm, "b_q_norm": b_q_norm}

    loss, dx, dw, ds = _local_step(cfg, x.reshape(cfg.tokens, d), positions.reshape(cfg.tokens),
                                   loss_target.reshape(cfg.tokens, d), w, small)
    loss = lax.psum(loss, ("x", "y", "c"))

    g = _pack(_unstack_grads(dw, d, cfg.heads), 1, PACK_COLS, PACK_ROW_ALIGN, PACK_COLS)
    chip_half = _add_own_half(g, _swap_halves(g, name="swap_halves"), name="add_halves")
    parts = _scatter_chips(chip_half, name="scatter_chips")
    parts = lax.dynamic_update_slice_in_dim(parts, lax.dynamic_slice_in_dim(chip_half, chip, 1, axis=0), chip, axis=0)
    mine = _sum_parts(parts, name="sum_chips")
    both = _join_halves(mine, name="join_halves")
    g_big = lax.dynamic_update_slice_in_dim(both, mine[None], lax.axis_index("c"), axis=0).reshape(g.shape[1:])
    grads = dict(zip(BIG, _unpack(g_big, big_shapes, 0, PACK_COLS, PACK_ROW_ALIGN)))

    small_shapes = [ds[n].shape for n in SMALL]
    parts = _all_gather_small(_pack([ds[n] for n in SMALL], 0, LANES, SUBLANES, SUBLANES), name="gather_small")
    g_small = dict(zip(SMALL, _unpack(_sum_parts(parts, name="sum_small"), small_shapes, 0, LANES, SUBLANES)))
    quarter = d // N_CHIPS
    g_small["ffn_norm"] = lax.dynamic_slice_in_dim(g_small["ffn_norm"], chip * quarter, quarter, axis=2)
    grads.update(g_small)

    delta, new_m, new_v = {}, {}, {}
    for n in BIG:
        delta[n], new_m[n], new_v[n] = _adamw(wts[n], mom[n], var[n], grads[n], name=f"adamw_{n}")
    packed = [_pack([z[n] for n in SMALL], 0, LANES, SUBLANES, SUBLANES) for z in (wts, mom, var, grads)]
    small_out = _adamw(*packed, name="adamw_small")
    shard_shapes = [wts[n].shape for n in SMALL]
    for out, res in zip((delta, new_m, new_v), small_out):
        out.update(zip(SMALL, _unpack(res, shard_shapes, 0, LANES, SUBLANES)))

    return (loss, dx.reshape(x.shape), *[grads[n] for n in WEIGHTS], *[delta[n] for n in WEIGHTS],
            *[new_m[n] for n in WEIGHTS], *[new_v[n] for n in WEIGHTS])
```

```python
import functools
from typing import NamedTuple

import jax
import jax.numpy as jnp
from jax import lax
from jax.experimental import pallas as pl
from jax.experimental.pallas import tpu as pltpu

F32 = jnp.float32
BF16 = jnp.bfloat16

HEAD_DIM = 64
LANES = 128
SUBLANES = 8
ROT_DIM = HEAD_DIM // 4
ROPE_THETA = 500000.0
NORM_EPS = 1e-6
BAND = 128
DILATIONS = (1, 4, 16)
NEG = -1e30
Q_SCALE = HEAD_DIM ** -0.5
N_CHIPS = 4
N_DEV = 8
VMEM_LIMIT = 48 * 1024 * 1024

ADAM_LR = 0.001
ADAM_B1 = 0.9
ADAM_B2 = 0.999
ADAM_EPS = 1e-08
ADAM_WD = 0.01
ADAM_STEP = 10


class Cfg(NamedTuple):
    d_model: int
    d_ff: int
    seq: int
    batch: int

    @property
    def heads(self):
        return self.d_model // HEAD_DIM

    @property
    def tokens(self):
        return self.batch * self.seq

    @property
    def pairs(self):
        return self.d_model // LANES


def _params(sem):
    return pltpu.CompilerParams(dimension_semantics=sem, vmem_limit_bytes=VMEM_LIMIT)


def _blk(dim, want):
    if dim <= want:
        return dim
    for b in range(want // LANES * LANES, 0, -LANES):
        if dim % b == 0:
            return b
    b = want
    while dim % b:
        b //= 2
    return b


def _mm(a, b, *, form, out_dtype, name, bm=1024, bn=1024, bk=1024, res=None, scale=1.0, norms=(), norm_bwd=None):
    if form == "F":
        m, kdim = a.shape
        jn, _, ns = b.shape
        bm, bn, bk = _blk(m, bm), _blk(ns, bn), _blk(kdim, bk)
        npj = ns // bn
        grid = (m // bm, jn * npj, kdim // bk)
        a_spec = pl.BlockSpec((bm, bk), lambda i, n, k: (i, k))
        b_spec = pl.BlockSpec((None, bk, bn), lambda i, n, k: (n // npj, k, n % npj))
        o_spec = pl.BlockSpec((bm, bn), lambda i, n, k: (i, n))
        o_shape = jax.ShapeDtypeStruct((m, jn * ns), out_dtype)
        dims = (((1,), (0,)), ((), ()))
    elif form == "B":
        m = a.shape[0]
        jn, kdim, ns = b.shape
        bm, bn, bk = _blk(m, bm), _blk(kdim, bn), _blk(ns, bk)
        kpj = ns // bk
        grid = (m // bm, kdim // bn, jn * kpj)
        a_spec = pl.BlockSpec((bm, bk), lambda i, n, k: (i, k))
        b_spec = pl.BlockSpec((None, bn, bk), lambda i, n, k: (k // kpj, n, k % kpj))
        o_spec = pl.BlockSpec((bm, bn), lambda i, n, k: (i, n))
        o_shape = jax.ShapeDtypeStruct((m, kdim), out_dtype)
        dims = (((1,), (1,)), ((), ()))
    else:
        raise ValueError(form)
    nk = grid[2]
    n_norms = len(norms)
    full_rows = grid[1] == 1
    assert full_rows or (not norms and norm_bwd is None)

    def body(*refs):
        a_ref, b_ref = refs[:2]
        pos = 2
        r_ref = None
        if res is not None:
            r_ref = refs[pos]
            pos += 1
        g_refs = refs[pos:pos + n_norms]
        pos += n_norms
        if norm_bwd is not None:
            x_ref, gb_ref, dres_ref = refs[pos:pos + 3]
            pos += 3
        o_ref = refs[pos]
        n_refs = refs[pos + 1:pos + 1 + n_norms]
        acc_ref = refs[-1]
        i, k = pl.program_id(0), pl.program_id(2)

        @pl.when(k == 0)
        def _():
            acc_ref[...] = jnp.zeros_like(acc_ref)

        acc_ref[...] += lax.dot_general(a_ref[...].astype(BF16), b_ref[...].astype(BF16), dims,
                                        preferred_element_type=F32)

        if norm_bwd is not None:
            dg_ref = refs[pos + 1 + n_norms]

            @pl.when((i == 0) & (k == 0))
            def _():
                dg_ref[...] = jnp.zeros_like(dg_ref)

        @pl.when(k == nk - 1)
        def _():
            r = acc_ref[...]
            if scale != 1.0:
                r = r * scale
            if r_ref is not None:
                r = r_ref[...] + r
            if norm_bwd is not None:
                dx, dg8 = _rms_bwd_tile(x_ref[...], gb_ref[...], r, dres_ref[...])
                o_ref[...] = dx
                dg_ref[...] += dg8
            else:
                o_ref[...] = r.astype(o_ref.dtype)
            if n_norms:
                rstd = lax.rsqrt(jnp.mean(r * r, axis=-1, keepdims=True) + NORM_EPS)
                for g_ref, n_ref in zip(g_refs, n_refs):
                    n_ref[...] = ((r * rstd) * g_ref[...]).astype(BF16)

    row = pl.BlockSpec((bm, bn), lambda i, n, k: (i, n))
    vec = pl.BlockSpec((1, bn), lambda i, n, k: (0, 0))
    in_specs = [a_spec, b_spec]
    args = [a, b]
    if res is not None:
        in_specs.append(row)
        args.append(res)
    for g in norms:
        in_specs.append(vec)
        args.append(g.reshape(1, -1))
    out_specs, out_shapes = [o_spec], [o_shape]
    for _ in norms:
        out_specs.append(row)
        out_shapes.append(jax.ShapeDtypeStruct(o_shape.shape, BF16))
    if norm_bwd is not None:
        x, g, dres = norm_bwd
        in_specs += [row, vec, row]
        args += [x, g.reshape(1, -1), dres]
        out_specs.append(pl.BlockSpec((SUBLANES, bn), lambda i, n, k: (0, 0)))
        out_shapes.append(jax.ShapeDtypeStruct((SUBLANES, o_shape.shape[1]), F32))
    sem = ("arbitrary",) * 3 if norm_bwd is not None else ("parallel", "parallel", "arbitrary")
    single = len(out_specs) == 1
    out = pl.pallas_call(
        body, name=name, grid=grid, in_specs=in_specs, out_specs=out_specs[0] if single else out_specs,
        out_shape=out_shapes[0] if single else out_shapes,
        scratch_shapes=[pltpu.VMEM((bm, bn), F32)],
        compiler_params=_params(sem),
    )(*args)
    if norm_bwd is not None:
        return out[0], jnp.sum(out[1], axis=0)
    return out


def _rms_bwd_tile(xv, g, dyv, dres):
    rstd = lax.rsqrt(jnp.mean(xv * xv, axis=-1, keepdims=True) + NORM_EPS)
    xhat = xv * rstd
    dyg = dyv * g
    proj = jnp.mean(dyg * xhat, axis=-1, keepdims=True)
    return dres + rstd * (dyg - xhat * proj), _fold8(dyv * xhat)


def _mm_grad(a, dy, jn, *, name, scale=1.0, bm=1024, bn=1024, bk=1024):
    halves = dy if isinstance(dy, (tuple, list)) else (dy,)
    t, kdim = a.shape
    ns = len(halves) * halves[0].shape[1] // jn
    bm, bn, bk = _blk(kdim, bm), _blk(ns, bn), _blk(t, bk)
    npj = ns // bn
    grid = (kdim // bm, jn * npj, t // bk)
    nk = grid[2]
    nhalf = jn * npj // len(halves)
    dims = (((0,), (0,)), ((), ()))

    def body(a_ref, *refs):
        b_refs, o_ref, acc_ref = refs[:len(halves)], refs[-2], refs[-1]
        n, k = pl.program_id(1), pl.program_id(2)

        @pl.when(k == 0)
        def _():
            acc_ref[...] = jnp.zeros_like(acc_ref)

        for which, b_ref in enumerate(b_refs):
            @pl.when(n // nhalf == which)
            def _(b_ref=b_ref):
                acc_ref[...] += lax.dot_general(a_ref[...].astype(BF16), b_ref[...].astype(BF16), dims,
                                                preferred_element_type=F32)

        @pl.when(k == nk - 1)
        def _():
            r = acc_ref[...]
            if scale != 1.0:
                r = r * scale
            o_ref[...] = r

    def half_spec(which):
        return pl.BlockSpec((bk, bn), lambda m, n, k: (jnp.where(n // nhalf == which, k, 0),
                                                        jnp.where(n // nhalf == which, n % nhalf, 0)))

    return pl.pallas_call(
        body, name=name, grid=grid,
        in_specs=[pl.BlockSpec((bk, bm), lambda m, n, k: (k, m))] + [half_spec(w) for w in range(len(halves))],
        out_specs=pl.BlockSpec((None, bm, bn), lambda m, n, k: (n // npj, m, n % npj)),
        out_shape=jax.ShapeDtypeStruct((jn, kdim, ns), F32),
        scratch_shapes=[pltpu.VMEM((bm, bn), F32)],
        compiler_params=_params(("parallel", "parallel", "arbitrary")),
    )(a, *halves)


def _mm_back2(dy_halves, w, norm_bwd, *, name, bm=512, bn=1024, bk=1024):
    x, g, dres = norm_bwd
    m = dy_halves[0].shape[0]
    jn, kdim, ns = w.shape
    bm, bk = _blk(m, bm), _blk(ns, bk)
    kpj = ns // bk
    nk = jn * kpj
    khalf = nk // 2

    def body(a0_ref, a1_ref, b_ref, x_ref, g_ref, dres_ref, o_ref, dg_ref, acc_ref):
        i, k = pl.program_id(0), pl.program_id(1)

        @pl.when(k == 0)
        def _():
            acc_ref[...] = jnp.zeros_like(acc_ref)

        @pl.when((i == 0) & (k == 0))
        def _():
            dg_ref[...] = jnp.zeros_like(dg_ref)

        for which, a_ref in enumerate((a0_ref, a1_ref)):
            @pl.when(k // khalf == which)
            def _(a_ref=a_ref):
                acc_ref[...] += lax.dot_general(a_ref[...], b_ref[...], _NT, preferred_element_type=F32)

        @pl.when(k == nk - 1)
        def _():
            dx, dg8 = _rms_bwd_tile(x_ref[...], g_ref[...], acc_ref[...], dres_ref[...])
            o_ref[...] = dx
            dg_ref[...] += dg8

    def half_spec(which):
        return pl.BlockSpec((bm, bk), lambda i, k: (i, jnp.clip(k - which * khalf, 0, khalf - 1)))

    row = pl.BlockSpec((bm, kdim), lambda i, k: (i, 0))
    dx, dg = pl.pallas_call(
        body, name=name, grid=(m // bm, nk),
        in_specs=[half_spec(0), half_spec(1),
                  pl.BlockSpec((None, kdim, bk), lambda i, k: (k // kpj, 0, k % kpj)),
                  row, pl.BlockSpec((1, kdim), lambda i, k: (0, 0)), row],
        out_specs=[row, pl.BlockSpec((SUBLANES, kdim), lambda i, k: (0, 0))],
        out_shape=[jax.ShapeDtypeStruct((m, kdim), F32), jax.ShapeDtypeStruct((SUBLANES, kdim), F32)],
        scratch_shapes=[pltpu.VMEM((bm, kdim), F32)],
        compiler_params=_params(("arbitrary", "arbitrary")),
    )(dy_halves[0], dy_halves[1], w, x, g.reshape(1, -1), dres)
    return dx, jnp.sum(dg, axis=0)


def _ffn_in_act(n, w_in, *, name, bm=512):
    m, kdim = n.shape
    jn, _, ns = w_in.shape
    f = jn * ns // 2
    bm = _blk(m, bm)
    bn = _blk(ns, WIDE)
    npj = ns // bn
    nf = f // bn

    def body(n_ref, wg_ref, wu_ref, g_ref, u_ref, a_ref):
        nv = n_ref[...]
        g = jnp.dot(nv, wg_ref[...], preferred_element_type=F32)
        u = jnp.dot(nv, wu_ref[...], preferred_element_type=F32)
        g_ref[...] = g.astype(BF16)
        u_ref[...] = u.astype(BF16)
        a_ref[...] = (g * jax.nn.sigmoid(g) * u).astype(BF16)

    out = jax.ShapeDtypeStruct((m, f), BF16)
    ospec = pl.BlockSpec((bm, bn), lambda c, i: (i, c))
    return pl.pallas_call(
        body, name=name, grid=(nf, m // bm),
        in_specs=[pl.BlockSpec((bm, kdim), lambda c, i: (i, 0)),
                  pl.BlockSpec((None, kdim, bn), lambda c, i: (c // npj, 0, c % npj)),
                  pl.BlockSpec((None, kdim, bn), lambda c, i: ((c + nf) // npj, 0, (c + nf) % npj))],
        out_specs=[ospec, ospec, ospec], out_shape=[out, out, out],
        compiler_params=_params(("parallel", "parallel")),
    )(n, w_in, w_in)


def _ffn_out_dx_act(dh, w_out, gate, up, *, name, scale, bm=512):
    m, d = dh.shape
    f = w_out.shape[1]
    bm = _blk(m, bm)
    bn = _blk(f, WIDE)

    def body(dh_ref, w_ref, g_ref, u_ref, dg_ref, du_ref):
        da = lax.dot_general(dh_ref[...].astype(BF16), w_ref[...], _NT, preferred_element_type=F32) * scale
        g = g_ref[...].astype(F32)
        sg = jax.nn.sigmoid(g)
        silu = g * sg
        dg_ref[...] = (da * u_ref[...].astype(F32) * (sg + silu * (1.0 - sg))).astype(BF16)
        du_ref[...] = (da * silu).astype(BF16)

    out = jax.ShapeDtypeStruct((m, f), BF16)
    spec = pl.BlockSpec((bm, bn), lambda i, c: (i, c))
    return pl.pallas_call(
        body, name=name, grid=(m // bm, f // bn),
        in_specs=[pl.BlockSpec((bm, d), lambda i, c: (i, 0)), pl.BlockSpec((None, bn, d), lambda i, c: (0, c, 0)),
                  spec, spec],
        out_specs=[spec, spec], out_shape=[out, out],
        compiler_params=_params(("parallel", "parallel")),
    )(dh, w_out, gate, up)


ROW_BLOCK = 512


def _fold8(x):
    return jnp.sum(x.reshape(x.shape[0] // SUBLANES, SUBLANES, x.shape[1]), axis=0)


def _rms_fwd(x, g, *, name):
    t, d = x.shape
    tr = _blk(t, ROW_BLOCK)

    def body(x_ref, g_ref, o_ref):
        xv = x_ref[...]
        rstd = lax.rsqrt(jnp.mean(xv * xv, axis=-1, keepdims=True) + NORM_EPS)
        o_ref[...] = ((xv * rstd) * g_ref[...]).astype(BF16)

    return pl.pallas_call(
        body, name=name, grid=(t // tr,),
        in_specs=[pl.BlockSpec((tr, d), lambda i: (i, 0)), pl.BlockSpec((1, d), lambda i: (0, 0))],
        out_specs=pl.BlockSpec((tr, d), lambda i: (i, 0)),
        out_shape=jax.ShapeDtypeStruct((t, d), BF16),
        compiler_params=_params(("parallel",)),
    )(x, g.reshape(1, d))


def _rms_bwd(x, g, dy, dres, *, name):
    t, d = x.shape
    tr = _blk(t, ROW_BLOCK)

    def body(x_ref, g_ref, dy_ref, dres_ref, dx_ref, dg_ref):
        i = pl.program_id(0)
        xv = x_ref[...]
        rstd = lax.rsqrt(jnp.mean(xv * xv, axis=-1, keepdims=True) + NORM_EPS)
        xhat = xv * rstd
        dyv = dy_ref[...]
        dyg = dyv * g_ref[...]
        proj = jnp.mean(dyg * xhat, axis=-1, keepdims=True)
        dx_ref[...] = dres_ref[...] + rstd * (dyg - xhat * proj)

        @pl.when(i == 0)
        def _():
            dg_ref[...] = jnp.zeros_like(dg_ref)

        dg_ref[...] += _fold8(dyv * xhat)

    dx, dg = pl.pallas_call(
        body, name=name, grid=(t // tr,),
        in_specs=[pl.BlockSpec((tr, d), lambda i: (i, 0)), pl.BlockSpec((1, d), lambda i: (0, 0)),
                  pl.BlockSpec((tr, d), lambda i: (i, 0)), pl.BlockSpec((tr, d), lambda i: (i, 0))],
        out_specs=[pl.BlockSpec((tr, d), lambda i: (i, 0)), pl.BlockSpec((SUBLANES, d), lambda i: (0, 0))],
        out_shape=[jax.ShapeDtypeStruct((t, d), F32), jax.ShapeDtypeStruct((SUBLANES, d), F32)],
        compiler_params=_params(("arbitrary",)),
    )(x, g.reshape(1, d), dy, dres)
    return dx, jnp.sum(dg, axis=0)


def _swiglu_fwd(u, *, name):
    t, f2 = u.shape
    f = f2 // 2
    tr = _blk(t, 256)

    def body(g_ref, u_ref, o_ref):
        gv = g_ref[...]
        o_ref[...] = (gv * jax.nn.sigmoid(gv) * u_ref[...]).astype(BF16)

    return pl.pallas_call(
        body, name=name, grid=(t // tr,),
        in_specs=[pl.BlockSpec((tr, f), lambda i: (i, 0)), pl.BlockSpec((tr, f), lambda i: (i, 1))],
        out_specs=pl.BlockSpec((tr, f), lambda i: (i, 0)),
        out_shape=jax.ShapeDtypeStruct((t, f), BF16),
        compiler_params=_params(("parallel",)),
    )(u, u)


def _swiglu_bwd(u, da, *, name):
    t, f2 = u.shape
    f = f2 // 2
    tr = _blk(t, 256)

    def body(g_ref, u_ref, da_ref, o_ref):
        gv = g_ref[...]
        sg = jax.nn.sigmoid(gv)
        silu = gv * sg
        dav = da_ref[...]
        o_ref[:, :f] = (dav * u_ref[...] * (sg + silu * (1.0 - sg))).astype(BF16)
        o_ref[:, f:] = (dav * silu).astype(BF16)

    return pl.pallas_call(
        body, name=name, grid=(t // tr,),
        in_specs=[pl.BlockSpec((tr, f), lambda i: (i, 0)), pl.BlockSpec((tr, f), lambda i: (i, 1)),
                  pl.BlockSpec((tr, f), lambda i: (i, 0))],
        out_specs=pl.BlockSpec((tr, f2), lambda i: (i, 0)),
        out_shape=jax.ShapeDtypeStruct((t, f2), BF16),
        compiler_params=_params(("parallel",)),
    )(u, u, da)


def _loss_fwd_bwd(h, target, *, name):
    t, d = h.shape
    tr = _blk(t, ROW_BLOCK)

    def body(h_ref, t_ref, dh_ref, l_ref):
        i = pl.program_id(0)
        err = h_ref[...] - t_ref[...]
        dh_ref[...] = err * (1.0 / d)

        @pl.when(i == 0)
        def _():
            l_ref[...] = jnp.zeros_like(l_ref)

        l_ref[...] += _fold8(err * err)

    dh, part = pl.pallas_call(
        body, name=name, grid=(t // tr,),
        in_specs=[pl.BlockSpec((tr, d), lambda i: (i, 0)), pl.BlockSpec((tr, d), lambda i: (i, 0))],
        out_specs=[pl.BlockSpec((tr, d), lambda i: (i, 0)), pl.BlockSpec((SUBLANES, d), lambda i: (0, 0))],
        out_shape=[jax.ShapeDtypeStruct((t, d), F32), jax.ShapeDtypeStruct((SUBLANES, d), F32)],
        compiler_params=_params(("arbitrary",)),
    )(h, target)
    return jnp.sum(part) * (0.5 / d), dh


def _seg_matrix():
    r = lax.broadcasted_iota(jnp.int32, (LANES, LANES), 0) // HEAD_DIM
    c = lax.broadcasted_iota(jnp.int32, (LANES, LANES), 1) // HEAD_DIM
    return (r == c).astype(BF16)


def _head_sum(x, seg, terms=3):
    hi = x.astype(BF16)
    r1 = x - hi.astype(F32)
    mid = r1.astype(BF16)
    dot = functools.partial(jnp.dot, preferred_element_type=F32)
    if terms == 2:
        return dot(hi, seg) + dot(mid, seg)
    lo = (r1 - mid.astype(F32)).astype(BF16)
    return dot(hi, seg) + dot(mid, seg) + dot(lo, seg)


def _lane_in_head(shape):
    return lax.broadcasted_iota(jnp.int32, shape, 1) % HEAD_DIM


def _rot_partner(x):
    up = pltpu.roll(x, LANES - ROT_DIM // 2, 1)
    down = pltpu.roll(x, ROT_DIM // 2, 1)
    return jnp.where(_lane_in_head(x.shape) < ROT_DIM // 2, up, down)


def _rope_tables(positions):
    inv_freq = ROPE_THETA ** (-jnp.arange(0, ROT_DIM, 2, dtype=F32) / ROT_DIM)
    ang = positions.astype(F32)[:, None] * inv_freq
    t = ang.shape[0]
    rest = HEAD_DIM - ROT_DIM
    cos = jnp.concatenate([jnp.cos(ang), jnp.cos(ang), jnp.ones((t, rest), F32)], axis=1)
    sin = jnp.concatenate([-jnp.sin(ang), jnp.sin(ang), jnp.zeros((t, rest), F32)], axis=1)
    return jnp.tile(cos, (1, LANES // HEAD_DIM)), jnp.tile(sin, (1, LANES // HEAD_DIM))


def _kind_is(j, kinds, kind):
    hits = [j == jj for jj, k in enumerate(kinds) if k == kind]
    return functools.reduce(jnp.logical_or, hits) if hits else None


def _hn_fwd(x, gains, kinds, d, cos, sin, *, name, col0=0):
    t = x.shape[0]
    n = len(kinds)
    tr = _blk(t, ROW_BLOCK)
    seg = _seg_matrix()
    g8 = jnp.repeat(gains.astype(F32), SUBLANES, axis=0)

    def body(x_ref, g_ref, seg_ref, cos_ref, sin_ref, o_ref):
        j = pl.program_id(1)

        def normed(rope):
            for c in range(d // LANES):
                sl = slice(c * LANES, (c + 1) * LANES)
                xv = x_ref[:, sl]
                ms = _head_sum(xv * xv, seg_ref[...], terms=2) * (1.0 / HEAD_DIM)
                y = (xv * lax.rsqrt(ms + NORM_EPS)) * g_ref[0:1, sl]
                if rope:
                    y = y * cos_ref[...] + _rot_partner(y) * sin_ref[...]
                o_ref[:, sl] = y.astype(BF16)

        for kind in ("rope", "norm"):
            hit = _kind_is(j, kinds, kind)
            if hit is not None:
                pl.when(hit)(functools.partial(normed, kind == "rope"))
        hit = _kind_is(j, kinds, "cast")
        if hit is not None:
            @pl.when(hit)
            def _():
                o_ref[...] = x_ref[...].astype(BF16)

    return pl.pallas_call(
        body, name=name, grid=(t // tr, n),
        in_specs=[pl.BlockSpec((tr, d), lambda i, j: (i, col0 + j)), pl.BlockSpec((SUBLANES, d), lambda i, j: (j, 0)),
                  pl.BlockSpec((LANES, LANES), lambda i, j: (0, 0)),
                  pl.BlockSpec((tr, LANES), lambda i, j: (i, 0)), pl.BlockSpec((tr, LANES), lambda i, j: (i, 0))],
        out_specs=pl.BlockSpec((tr, d), lambda i, j: (i, j)),
        out_shape=jax.ShapeDtypeStruct((t, n * d), BF16),
        compiler_params=_params(("parallel", "parallel")),
    )(x, g8, seg, cos, sin)


def _hn_bwd(x, dys, gains, kinds, d, cos, sin, *, name, col0=0):
    t = x.shape[0]
    n = len(kinds)
    tr = _blk(t, ROW_BLOCK // 2)
    seg = _seg_matrix()
    g8 = jnp.repeat(gains.astype(F32), SUBLANES, axis=0)

    def body(x_ref, *refs):
        dy_refs = refs[:n]
        g_ref, seg_ref, cos_ref, sin_ref, dx_ref, dg_ref = refs[n:]
        j = pl.program_id(0)
        i = pl.program_id(1)

        @pl.when(i == 0)
        def _():
            dg_ref[...] = jnp.zeros_like(dg_ref)

        def normed(rope, dy_ref):
            for c in range(d // LANES):
                sl = slice(c * LANES, (c + 1) * LANES)
                xv = x_ref[:, sl]
                dyv = dy_ref[:, sl]
                if rope:
                    dyv = dyv * cos_ref[...] - _rot_partner(dyv) * sin_ref[...]
                ms = _head_sum(xv * xv, seg_ref[...], terms=2) * (1.0 / HEAD_DIM)
                rstd = lax.rsqrt(ms + NORM_EPS)
                xhat = xv * rstd
                dg_ref[:, sl] += _fold8(dyv * xhat)
                dyg = dyv * g_ref[0:1, sl]
                proj = _head_sum(dyg * xhat, seg_ref[...], terms=2) * (1.0 / HEAD_DIM)
                dx_ref[:, sl] = (rstd * (dyg - xhat * proj)).astype(BF16)

        def cast(dy_ref):
            dx_ref[...] = dy_ref[...].astype(BF16)

        for jj, kind in enumerate(kinds):
            if kind == "cast":
                pl.when(j == jj)(functools.partial(cast, dy_refs[jj]))
            else:
                pl.when(j == jj)(functools.partial(normed, kind == "rope", dy_refs[jj]))

    def dy_spec(jj):
        return pl.BlockSpec((tr, d), lambda j, i: (jnp.where(j == jj, i, 0), 0))

    dx, dg = pl.pallas_call(
        body, name=name, grid=(n, t // tr),
        in_specs=[pl.BlockSpec((tr, d), lambda j, i: (i, col0 + j))] + [dy_spec(jj) for jj in range(n)] + [
                  pl.BlockSpec((SUBLANES, d), lambda j, i: (j, 0)),
                  pl.BlockSpec((LANES, LANES), lambda j, i: (0, 0)),
                  pl.BlockSpec((tr, LANES), lambda j, i: (i, 0)), pl.BlockSpec((tr, LANES), lambda j, i: (i, 0))],
        out_specs=[pl.BlockSpec((tr, d), lambda j, i: (i, j)), pl.BlockSpec((SUBLANES, d), lambda j, i: (j, 0))],
        out_shape=[jax.ShapeDtypeStruct((t, n * d), BF16), jax.ShapeDtypeStruct((n * SUBLANES, d), F32)],
        compiler_params=_params(("arbitrary", "arbitrary")),
    )(x, *dys, g8, seg, cos, sin)
    dg = dg.reshape(n, SUBLANES, d // HEAD_DIM, HEAD_DIM).sum(axis=(1, 2))
    return dx, dg


def _head_dot(a, b, *, name):
    t, d = a.shape
    tr = _blk(t, ROW_BLOCK)
    seg = _seg_matrix()

    def body(a_ref, b_ref, seg_ref, o_ref):
        for c in range(d // LANES):
            sl = slice(c * LANES, (c + 1) * LANES)
            o_ref[:, sl] = _head_sum(a_ref[:, sl].astype(BF16).astype(F32) * b_ref[:, sl], seg_ref[...])

    return pl.pallas_call(
        body, name=name, grid=(t // tr,),
        in_specs=[pl.BlockSpec((tr, d), lambda i: (i, 0)), pl.BlockSpec((tr, d), lambda i: (i, 0)),
                  pl.BlockSpec((LANES, LANES), lambda i: (0, 0))],
        out_specs=pl.BlockSpec((tr, d), lambda i: (i, 0)),
        out_shape=jax.ShapeDtypeStruct((t, d), F32),
        compiler_params=_params(("parallel",)),
    )(a, b, seg)


def _half_mask(shape):
    return lax.broadcasted_iota(jnp.int32, shape, 1) < HEAD_DIM


def _band_valid(first):
    qi = lax.broadcasted_iota(jnp.int32, (BAND, 2 * BAND), 0)
    kj = lax.broadcasted_iota(jnp.int32, (BAND, 2 * BAND), 1)
    dist = qi + BAND - kj
    return (dist >= 0) & (dist <= BAND) & ((kj >= BAND) | jnp.logical_not(first))


_NT = (((1,), (1,)), ((), ()))
_TN = (((0,), (0,)), ((), ()))


def _dot2(p, v):
    hi = p.astype(BF16)
    lo = (p - hi.astype(F32)).astype(BF16)
    return jnp.dot(hi, v, preferred_element_type=F32) + jnp.dot(lo, v, preferred_element_type=F32)


def _band_fwd(qkv, dil, cfg, *, name):
    t, d = cfg.tokens, cfg.d_model
    w = 3 * d
    rows = t // dil
    nbt = rows // BAND
    nb = cfg.seq // (dil * BAND)
    view = qkv.reshape(rows, dil * w)
    ncol = w // d

    def body(q_ref, kp_ref, kc_ref, vp_ref, vc_ref, o_ref, lse_ref):
        i = pl.program_id(1)
        valid = _band_valid(i % nb == 0)
        half = _half_mask((BAND, LANES))
        for hp in range(d // LANES):
            sl = slice(hp * LANES, (hp + 1) * LANES)
            q2 = q_ref[:, sl]
            kk = jnp.concatenate([kp_ref[:, sl], kc_ref[:, sl]], axis=0)
            vv = jnp.concatenate([vp_ref[:, sl], vc_ref[:, sl]], axis=0)
            outs, lses = [], []
            for e in range(2):
                qe = jnp.where(half == (e == 0), q2, jnp.zeros_like(q2))
                s = lax.dot_general(qe, kk, _NT, preferred_element_type=F32)
                s = jnp.where(valid, s, NEG)
                m = jnp.max(s, axis=1, keepdims=True)
                p = jnp.exp(s - m)
                l = jnp.sum(p, axis=1, keepdims=True)
                outs.append(_dot2(p * (1.0 / l), vv))
                lses.append(m + jnp.log(l))
            o_ref[:, sl] = jnp.where(half, outs[0], outs[1])
            lse_ref[:, sl] = jnp.where(half, lses[0], lses[1])

    def col(which):
        return lambda r, i: (i, r * ncol + which)

    def col_prev(which):
        return lambda r, i: (jnp.maximum(i - 1, 0), r * ncol + which)

    blk = (BAND, d)
    o, lse = pl.pallas_call(
        body, name=name, grid=(dil, nbt),
        in_specs=[pl.BlockSpec(blk, col(0)), pl.BlockSpec(blk, col_prev(1)), pl.BlockSpec(blk, col(1)),
                  pl.BlockSpec(blk, col_prev(2)), pl.BlockSpec(blk, col(2))],
        out_specs=[pl.BlockSpec(blk, lambda r, i: (i, r)), pl.BlockSpec(blk, lambda r, i: (i, r))],
        out_shape=[jax.ShapeDtypeStruct((rows, dil * d), F32), jax.ShapeDtypeStruct((rows, dil * d), F32)],
        compiler_params=_params(("parallel", "arbitrary")),
    )(view, view, view, view, view)
    return o.reshape(t, d), lse.reshape(t, d)


def _band_bwd(qkv, dmixed, lse_all, dsum, dil, cfg, *, name):
    t, d = cfg.tokens, cfg.d_model
    w = 3 * d
    rows = t // dil
    nbt = rows // BAND
    nb = cfg.seq // (dil * BAND)
    view = qkv.reshape(rows, dil * w)
    ncol = w // d
    do_v, l_v, d_v = (z.reshape(rows, dil * d) for z in (dmixed, lse_all, dsum))

    def body(q_ref, kp_ref, kc_ref, vp_ref, vc_ref, do_ref, l_ref, ds_ref, dq_ref, dk_ref, dv_ref, ck_ref, cv_ref):
        i = pl.program_id(1)

        @pl.when(i < nbt)
        def _():
            valid = _band_valid(i % nb == 0)
            half = _half_mask((BAND, LANES))
            half2 = _half_mask((2 * BAND, LANES))
            for hp in range(d // LANES):
                sl = slice(hp * LANES, (hp + 1) * LANES)
                q2 = q_ref[:, sl]
                kk = jnp.concatenate([kp_ref[:, sl], kc_ref[:, sl]], axis=0)
                vv = jnp.concatenate([vp_ref[:, sl], vc_ref[:, sl]], axis=0)
                do2 = do_ref[:, sl].astype(BF16)
                dqs, dks, dvs = [], [], []
                for e in range(2):
                    lane0 = e * HEAD_DIM
                    keep = half == (e == 0)
                    qe = jnp.where(keep, q2, jnp.zeros_like(q2))
                    doe = jnp.where(keep, do2, jnp.zeros_like(do2))
                    s = lax.dot_general(qe, kk, _NT, preferred_element_type=F32)
                    s = jnp.where(valid, s, NEG)
                    p = jnp.exp(s - l_ref[:, hp * LANES + lane0:hp * LANES + lane0 + 1])
                    dp = lax.dot_general(doe, vv, _NT, preferred_element_type=F32)
                    dsc = (p * (dp - ds_ref[:, hp * LANES + lane0:hp * LANES + lane0 + 1])).astype(BF16)
                    dqs.append(jnp.dot(dsc, kk, preferred_element_type=F32))
                    dks.append(lax.dot_general(dsc, q2, _TN, preferred_element_type=F32))
                    dvs.append(lax.dot_general(p.astype(BF16), do2, _TN, preferred_element_type=F32))
                dq_ref[:, sl] = jnp.where(half, dqs[0], dqs[1])
                dkk = jnp.where(half2, dks[0], dks[1])
                dvv = jnp.where(half2, dvs[0], dvs[1])

                @pl.when(i > 0)
                def _():
                    dk_ref[:, sl] = ck_ref[:, sl] + dkk[:BAND]
                    dv_ref[:, sl] = cv_ref[:, sl] + dvv[:BAND]

                ck_ref[:, sl] = dkk[BAND:]
                cv_ref[:, sl] = dvv[BAND:]

        @pl.when(i == nbt)
        def _():
            dk_ref[...] = ck_ref[...]
            dv_ref[...] = cv_ref[...]

    def cur(i):
        return jnp.minimum(i, nbt - 1)

    def col(which):
        return lambda r, i: (cur(i), r * ncol + which)

    def col_prev(which):
        return lambda r, i: (jnp.maximum(cur(i) - 1, 0), r * ncol + which)

    blk = (BAND, d)
    here = pl.BlockSpec(blk, lambda r, i: (cur(i), r))
    behind = pl.BlockSpec(blk, lambda r, i: (jnp.maximum(i - 1, 0), r))
    shape = jax.ShapeDtypeStruct((rows, dil * d), F32)
    dq, dk, dv = pl.pallas_call(
        body, name=name, grid=(dil, nbt + 1),
        in_specs=[pl.BlockSpec(blk, col(0)), pl.BlockSpec(blk, col_prev(1)), pl.BlockSpec(blk, col(1)),
                  pl.BlockSpec(blk, col_prev(2)), pl.BlockSpec(blk, col(2)), here, here, here],
        out_specs=[here, behind, behind],
        out_shape=[shape, shape, shape],
        scratch_shapes=[pltpu.VMEM(blk, F32), pltpu.VMEM(blk, F32)],
        compiler_params=_params(("arbitrary", "arbitrary")),
    )(view, view, view, view, view, do_v, l_v, d_v)
    return dq.reshape(t, d), dk.reshape(t, d), dv.reshape(t, d)


def _band_valid_t(first):
    s = lax.broadcasted_iota(jnp.int32, (2 * BAND, BAND), 0)
    t = lax.broadcasted_iota(jnp.int32, (2 * BAND, BAND), 1)
    dist = t + BAND - s
    return (dist >= 0) & (dist <= BAND) & ((s >= BAND) | jnp.logical_not(first))


def _band_layouts(qkv, dil, cfg):
    rows = cfg.tokens // dil
    d = cfg.d_model
    return qkv.reshape(rows, dil * 3 * d), qkv.reshape(rows, dil, 3, d).transpose(1, 2, 3, 0)


def _to_classes_t(z, dil, width):
    return z.reshape(z.shape[0] // dil, dil, width).transpose(1, 2, 0)


def _from_classes_t(z):
    dil, width, rows = z.shape
    return z.transpose(2, 0, 1).reshape(rows * dil, width)


def _band_fwd_t(nat, tr, dil, cfg, *, name):
    d, hh = cfg.d_model, cfg.heads
    rows = cfg.tokens // dil
    nbt = rows // BAND
    nb = cfg.seq // (dil * BAND)

    def body(qt_ref, kp_ref, kc_ref, vtp_ref, vtc_ref, o_ref, lse_ref):
        i = pl.program_id(1)
        valid = _band_valid_t(i % nb == 0)
        upper = lax.broadcasted_iota(jnp.int32, (LANES, BAND), 0) < HEAD_DIM
        for hp in range(d // LANES):
            pair = slice(hp * LANES, (hp + 1) * LANES)
            qt2 = qt_ref[pair, :]
            kk = jnp.concatenate([kp_ref[:, pair], kc_ref[:, pair]], axis=0)
            for e in range(2):
                h = 2 * hp + e
                hrows = slice(h * HEAD_DIM, (h + 1) * HEAD_DIM)
                qte = jnp.where(upper == (e == 0), qt2, jnp.zeros_like(qt2))
                s = jnp.where(valid, jnp.dot(kk, qte, preferred_element_type=F32), NEG)
                m = jnp.max(s, axis=0, keepdims=True)
                p = jnp.exp(s - m)
                l = jnp.sum(p, axis=0, keepdims=True)
                hi = p.astype(BF16)
                lo = (p - hi.astype(F32)).astype(BF16)
                vvt = jnp.concatenate([vtp_ref[hrows, :], vtc_ref[hrows, :]], axis=1)
                o = jnp.dot(vvt, hi, preferred_element_type=F32) + jnp.dot(vvt, lo, preferred_element_type=F32)
                o_ref[hrows, :] = o * (1.0 / l)
                lse_ref[h:h + 1, :] = m + jnp.log(l)

    def prev(i):
        return jnp.maximum(i - 1, 0)

    tblk = (None, None, d, BAND)
    return pl.pallas_call(
        body, name=name, grid=(dil, nbt),
        in_specs=[pl.BlockSpec(tblk, lambda r, i: (r, 0, 0, i)),
                  pl.BlockSpec((BAND, d), lambda r, i: (prev(i), r * 3 + 1)),
                  pl.BlockSpec((BAND, d), lambda r, i: (i, r * 3 + 1)),
                  pl.BlockSpec(tblk, lambda r, i: (r, 2, 0, prev(i))),
                  pl.BlockSpec(tblk, lambda r, i: (r, 2, 0, i))],
        out_specs=[pl.BlockSpec((None, d, BAND), lambda r, i: (r, 0, i)),
                   pl.BlockSpec((None, hh, BAND), lambda r, i: (r, 0, i))],
        out_shape=[jax.ShapeDtypeStruct((dil, d, rows), F32), jax.ShapeDtypeStruct((dil, hh, rows), F32)],
        compiler_params=_params(("parallel", "arbitrary")),
    )(tr, nat, nat, tr, tr)


def _band_bwd_t(nat, tr, do_t, do_nat, lse_c, dsum_c, dil, cfg, *, name):
    d, hh = cfg.d_model, cfg.heads
    rows = cfg.tokens // dil
    nbt = rows // BAND
    nb = cfg.seq // (dil * BAND)

    def body(qt_ref, qn_ref, kp_ref, kc_ref, ktp_ref, ktc_ref, vp_ref, vc_ref, dot_ref, don_ref, l_ref, ds_ref,
             dq_ref, dk_ref, dv_ref, ck_ref, cv_ref):
        i = pl.program_id(1)

        @pl.when(i < nbt)
        def _():
            valid = _band_valid_t(i % nb == 0)
            upper = lax.broadcasted_iota(jnp.int32, (LANES, BAND), 0) < HEAD_DIM
            half2 = _half_mask((2 * BAND, LANES))
            for hp in range(d // LANES):
                pair = slice(hp * LANES, (hp + 1) * LANES)
                qt2, dot2 = qt_ref[pair, :], dot_ref[pair, :]
                qn2, don2 = qn_ref[:, pair], don_ref[:, pair]
                kk = jnp.concatenate([kp_ref[:, pair], kc_ref[:, pair]], axis=0)
                vv = jnp.concatenate([vp_ref[:, pair], vc_ref[:, pair]], axis=0)
                dks, dvs = [], []
                for e in range(2):
                    h = 2 * hp + e
                    hrows = slice(h * HEAD_DIM, (h + 1) * HEAD_DIM)
                    keep = upper == (e == 0)
                    qte = jnp.where(keep, qt2, jnp.zeros_like(qt2))
                    dote = jnp.where(keep, dot2, jnp.zeros_like(dot2))
                    s = jnp.where(valid, jnp.dot(kk, qte, preferred_element_type=F32), NEG)
                    p = jnp.exp(s - l_ref[h:h + 1, :])
                    dp = jnp.dot(vv, dote, preferred_element_type=F32)
                    dsb = (p * (dp - ds_ref[h:h + 1, :])).astype(BF16)
                    kkt = jnp.concatenate([ktp_ref[hrows, :], ktc_ref[hrows, :]], axis=1)
                    dq_ref[hrows, :] = jnp.dot(kkt, dsb, preferred_element_type=F32)
                    dks.append(jnp.dot(dsb, qn2, preferred_element_type=F32))
                    dvs.append(jnp.dot(p.astype(BF16), don2, preferred_element_type=F32))
                dkk = jnp.where(half2, dks[0], dks[1])
                dvv = jnp.where(half2, dvs[0], dvs[1])

                @pl.when(i > 0)
                def _():
                    dk_ref[:, pair] = ck_ref[:, pair] + dkk[:BAND]
                    dv_ref[:, pair] = cv_ref[:, pair] + dvv[:BAND]

                ck_ref[:, pair] = dkk[BAND:]
                cv_ref[:, pair] = dvv[BAND:]

        @pl.when(i == nbt)
        def _():
            dk_ref[...] = ck_ref[...]
            dv_ref[...] = cv_ref[...]

    def cur(i):
        return jnp.minimum(i, nbt - 1)

    def prev(i):
        return jnp.maximum(cur(i) - 1, 0)

    tblk = (None, None, d, BAND)
    cblk = (None, hh, BAND)
    blk = (BAND, d)
    behind = pl.BlockSpec(blk, lambda r, i: (jnp.maximum(i - 1, 0), r))
    shape = jax.ShapeDtypeStruct((rows, dil * d), F32)
    return pl.pallas_call(
        body, name=name, grid=(dil, nbt + 1),
        in_specs=[pl.BlockSpec(tblk, lambda r, i: (r, 0, 0, cur(i))),
                  pl.BlockSpec(blk, lambda r, i: (cur(i), r * 3)),
                  pl.BlockSpec(blk, lambda r, i: (prev(i), r * 3 + 1)),
                  pl.BlockSpec(blk, lambda r, i: (cur(i), r * 3 + 1)),
                  pl.BlockSpec(tblk, lambda r, i: (r, 1, 0, prev(i))),
                  pl.BlockSpec(tblk, lambda r, i: (r, 1, 0, cur(i))),
                  pl.BlockSpec(blk, lambda r, i: (prev(i), r * 3 + 2)),
                  pl.BlockSpec(blk, lambda r, i: (cur(i), r * 3 + 2)),
                  pl.BlockSpec((None, d, BAND), lambda r, i: (r, 0, cur(i))),
                  pl.BlockSpec(blk, lambda r, i: (cur(i), r)),
                  pl.BlockSpec(cblk, lambda r, i: (r, 0, cur(i))),
                  pl.BlockSpec(cblk, lambda r, i: (r, 0, cur(i)))],
        out_specs=[pl.BlockSpec((None, d, BAND), lambda r, i: (r, 0, cur(i))), behind, behind],
        out_shape=[jax.ShapeDtypeStruct((dil, d, rows), F32), shape, shape],
        scratch_shapes=[pltpu.VMEM(blk, F32), pltpu.VMEM(blk, F32)],
        compiler_params=_params(("arbitrary", "arbitrary")),
    )(tr, nat, nat, nat, tr, tr, nat, nat, do_t, do_nat, lse_c, dsum_c)


def _band_fwd_n(nat, dil, cfg, *, name):
    d, hh = cfg.d_model, cfg.heads
    rows = cfg.tokens // dil
    nbt = rows // BAND
    nb = cfg.seq // (dil * BAND)

    def body(q_ref, kp_ref, kc_ref, vp_ref, vc_ref, o_ref, lse_ref):
        i = pl.program_id(1)
        valid = _band_valid_t(i % nb == 0)
        upper = lax.broadcasted_iota(jnp.int32, (LANES, BAND), 0) < HEAD_DIM
        for hp in range(d // LANES):
            pair = slice(hp * LANES, (hp + 1) * LANES)
            qt2 = q_ref[:, pair].T
            kk = jnp.concatenate([kp_ref[:, pair], kc_ref[:, pair]], axis=0)
            vvt = jnp.concatenate([vp_ref[:, pair], vc_ref[:, pair]], axis=0).T
            outs = []
            for e in range(2):
                h = 2 * hp + e
                qte = jnp.where(upper == (e == 0), qt2, jnp.zeros_like(qt2))
                s = jnp.where(valid, jnp.dot(kk, qte, preferred_element_type=F32), NEG)
                m = jnp.max(s, axis=0, keepdims=True)
                p = jnp.exp(s - m)
                l = jnp.sum(p, axis=0, keepdims=True)
                hi = p.astype(BF16)
                lo = (p - hi.astype(F32)).astype(BF16)
                vt = vvt[e * HEAD_DIM:(e + 1) * HEAD_DIM]
                o = jnp.dot(vt, hi, preferred_element_type=F32) + jnp.dot(vt, lo, preferred_element_type=F32)
                outs.append(o * (1.0 / l))
                lse_ref[h:h + 1, :] = m + jnp.log(l)
            o_ref[:, pair] = jnp.concatenate(outs, axis=0).T

    def prev(i):
        return jnp.maximum(i - 1, 0)

    blk = (BAND, d)
    return pl.pallas_call(
        body, name=name, grid=(dil, nbt),
        in_specs=[pl.BlockSpec(blk, lambda r, i: (i, r * 3)),
                  pl.BlockSpec(blk, lambda r, i: (prev(i), r * 3 + 1)),
                  pl.BlockSpec(blk, lambda r, i: (i, r * 3 + 1)),
                  pl.BlockSpec(blk, lambda r, i: (prev(i), r * 3 + 2)),
                  pl.BlockSpec(blk, lambda r, i: (i, r * 3 + 2))],
        out_specs=[pl.BlockSpec(blk, lambda r, i: (i, r)),
                   pl.BlockSpec((None, hh, BAND), lambda r, i: (r, 0, i))],
        out_shape=[jax.ShapeDtypeStruct((rows, dil * d), F32), jax.ShapeDtypeStruct((dil, hh, rows), F32)],
        compiler_params=_params(("parallel", "arbitrary")),
    )(nat, nat, nat, nat, nat)


def _band_bwd_n(nat, do_nat, lse_c, dsum_c, dil, cfg, *, name):
    d, hh = cfg.d_model, cfg.heads
    rows = cfg.tokens // dil
    nbt = rows // BAND
    nb = cfg.seq // (dil * BAND)

    def body(q_ref, kp_ref, kc_ref, vp_ref, vc_ref, do_ref, l_ref, ds_ref, dq_ref, dk_ref, dv_ref, ck_ref, cv_ref):
        i = pl.program_id(1)

        @pl.when(i < nbt)
        def _():
            @pl.when(i == 0)
            def _():
                ck_ref[...] = jnp.zeros_like(ck_ref)
                cv_ref[...] = jnp.zeros_like(cv_ref)

            valid1 = _band_valid_t(i % nb == 0)
            valid = jnp.concatenate([valid1, valid1], axis=1)
            upper = lax.broadcasted_iota(jnp.int32, (LANES, BAND), 0) < HEAD_DIM
            half2 = _half_mask((2 * BAND, LANES))

            def both(z):
                zero = jnp.zeros_like(z)
                return jnp.concatenate([jnp.where(upper, z, zero), jnp.where(upper, zero, z)], axis=1)

            def stack(z):
                return jnp.concatenate([z[:, :BAND], z[:, BAND:]], axis=0)

            for hp in range(d // LANES):
                pair = slice(hp * LANES, (hp + 1) * LANES)
                h0, h1 = 2 * hp, 2 * hp + 1
                qn2, don2 = q_ref[:, pair], do_ref[:, pair]
                kk = jnp.concatenate([kp_ref[:, pair], kc_ref[:, pair]], axis=0)
                vv = jnp.concatenate([vp_ref[:, pair], vc_ref[:, pair]], axis=0)
                lse2 = jnp.concatenate([l_ref[h0:h0 + 1, :], l_ref[h1:h1 + 1, :]], axis=1)
                dsum2 = jnp.concatenate([ds_ref[h0:h0 + 1, :], ds_ref[h1:h1 + 1, :]], axis=1)
                s = jnp.where(valid, jnp.dot(kk, both(qn2.T), preferred_element_type=F32), NEG)
                p = jnp.exp(s - lse2)
                dp = jnp.dot(vv, both(don2.T), preferred_element_type=F32)
                dsb = (p * (dp - dsum2)).astype(BF16)
                dq2 = jnp.dot(kk.T, dsb, preferred_element_type=F32)
                dq_ref[:, pair] = jnp.concatenate([dq2[:HEAD_DIM, :BAND], dq2[HEAD_DIM:, BAND:]], axis=0).T
                dk2 = jnp.dot(stack(dsb), qn2, preferred_element_type=F32)
                dv2 = jnp.dot(stack(p.astype(BF16)), don2, preferred_element_type=F32)
                dkk = jnp.where(half2, dk2[:2 * BAND], dk2[2 * BAND:])
                dvv = jnp.where(half2, dv2[:2 * BAND], dv2[2 * BAND:])
                dk_ref[:, pair] = ck_ref[:, pair] + dkk[:BAND]
                dv_ref[:, pair] = cv_ref[:, pair] + dvv[:BAND]
                ck_ref[:, pair] = dkk[BAND:]
                cv_ref[:, pair] = dvv[BAND:]

        @pl.when(i == nbt)
        def _():
            dk_ref[...] = ck_ref[...]
            dv_ref[...] = cv_ref[...]

    def cur(i):
        return jnp.minimum(i, nbt - 1)

    def prev(i):
        return jnp.maximum(cur(i) - 1, 0)

    cblk = (None, hh, BAND)
    blk = (BAND, d)
    here = pl.BlockSpec(blk, lambda r, i: (cur(i), r))
    behind = pl.BlockSpec(blk, lambda r, i: (jnp.maximum(i - 1, 0), r))
    shape = jax.ShapeDtypeStruct((rows, dil * d), F32)
    return pl.pallas_call(
        body, name=name, grid=(dil, nbt + 1),
        in_specs=[pl.BlockSpec(blk, lambda r, i: (cur(i), r * 3)),
                  pl.BlockSpec(blk, lambda r, i: (prev(i), r * 3 + 1)),
                  pl.BlockSpec(blk, lambda r, i: (cur(i), r * 3 + 1)),
                  pl.BlockSpec(blk, lambda r, i: (prev(i), r * 3 + 2)),
                  pl.BlockSpec(blk, lambda r, i: (cur(i), r * 3 + 2)),
                  here,
                  pl.BlockSpec(cblk, lambda r, i: (r, 0, cur(i))),
                  pl.BlockSpec(cblk, lambda r, i: (r, 0, cur(i)))],
        out_specs=[here, behind, behind],
        out_shape=[shape, shape, shape],
        scratch_shapes=[pltpu.VMEM(blk, F32), pltpu.VMEM(blk, F32)],
        compiler_params=_params(("arbitrary", "arbitrary")),
    )(nat, nat, nat, nat, nat, do_nat, lse_c, dsum_c)


def _mix_fwd(outs, lses, *, name):
    t, d = outs[0].shape
    tr = _blk(t, ROW_BLOCK)
    ng = len(outs)

    def body(*refs):
        o_refs, l_refs = refs[:ng], refs[ng:2 * ng]
        mixed_ref, lse_ref = refs[2 * ng:]
        ls = [r[...] for r in l_refs]
        m = functools.reduce(jnp.maximum, ls)
        es = [jnp.exp(l - m) for l in ls]
        tot = functools.reduce(jnp.add, es)
        inv = 1.0 / tot
        mixed_ref[...] = functools.reduce(jnp.add, [(e * inv) * r[...] for e, r in zip(es, o_refs)])
        lse_ref[...] = m + jnp.log(tot)

    spec = pl.BlockSpec((tr, d), lambda i: (i, 0))
    return pl.pallas_call(
        body, name=name, grid=(t // tr,),
        in_specs=[spec] * (2 * ng), out_specs=[spec, spec],
        out_shape=[jax.ShapeDtypeStruct((t, d), F32), jax.ShapeDtypeStruct((t, d), F32)],
        compiler_params=_params(("parallel",)),
    )(*outs, *lses)


GATE_BLOCK = 256


def _tri(n, upper):
    r = lax.broadcasted_iota(jnp.int32, (n, n), 0)
    c = lax.broadcasted_iota(jnp.int32, (n, n), 1)
    return ((c >= r) if upper else (c <= r)).astype(BF16)


def _tri_dot(tri, x):
    hi = x.astype(BF16)
    r1 = x - hi.astype(F32)
    mid = r1.astype(BF16)
    lo = (r1 - mid.astype(F32)).astype(BF16)
    dot = functools.partial(jnp.dot, preferred_element_type=F32)
    return dot(tri, hi) + dot(tri, mid) + dot(tri, lo)


def _log_sigmoid(z):
    return jnp.minimum(z, 0.0) - jnp.log(1.0 + jnp.exp(-jnp.abs(z)))


def _gate_fwd(proj, col_block, bias, cfg, *, name):
    tr = _blk(cfg.seq, GATE_BLOCK)
    nblk = cfg.seq // tr

    def body(z_ref, b_ref, tri_ref, o_ref, carry_ref):
        i = pl.program_id(1)

        @pl.when(i == 0)
        def _():
            carry_ref[...] = jnp.zeros_like(carry_ref)

        logf = _log_sigmoid(z_ref[...] + b_ref[0:1, :])
        cum = _tri_dot(tri_ref[...], logf) + carry_ref[0:1, :]
        o_ref[...] = cum
        carry_ref[...] = jnp.broadcast_to(cum[tr - 1:tr, :], carry_ref.shape)

    return pl.pallas_call(
        body, name=name, grid=(cfg.batch, nblk),
        in_specs=[pl.BlockSpec((tr, LANES), lambda b, i: (b * nblk + i, col_block)),
                  pl.BlockSpec((SUBLANES, LANES), lambda b, i: (0, 0)),
                  pl.BlockSpec((tr, tr), lambda b, i: (0, 0))],
        out_specs=pl.BlockSpec((tr, LANES), lambda b, i: (b * nblk + i, 0)),
        out_shape=jax.ShapeDtypeStruct((cfg.tokens, LANES), F32),
        scratch_shapes=[pltpu.VMEM((SUBLANES, LANES), F32)],
        compiler_params=_params(("arbitrary", "arbitrary")),
    )(proj, jnp.broadcast_to(bias, (SUBLANES, LANES)), _tri(tr, upper=False))


def _gate_bwd(proj, col_block, bias, dcum, cfg, *, name):
    tr = _blk(cfg.seq, GATE_BLOCK)
    nblk = cfg.seq // tr

    def body(z_ref, b_ref, tri_ref, dc_ref, dz_ref, db_ref, carry_ref):
        b = pl.program_id(0)
        i = pl.program_id(1)

        @pl.when(i == 0)
        def _():
            carry_ref[...] = jnp.zeros_like(carry_ref)

        @pl.when((i == 0) & (b == 0))
        def _():
            db_ref[...] = jnp.zeros_like(db_ref)

        dcv = dc_ref[...]
        dlogf = _tri_dot(tri_ref[...], dcv) + carry_ref[0:1, :]
        carry_ref[...] = jnp.broadcast_to(dlogf[0:1, :], carry_ref.shape)
        dz = dlogf * jax.nn.sigmoid(-(z_ref[...] + b_ref[0:1, :]))
        dz_ref[...] = dz
        db_ref[...] += _fold8(dz)

    def rev(b, i):
        return (b * nblk + nblk - 1 - i, 0)

    dz, db = pl.pallas_call(
        body, name=name, grid=(cfg.batch, nblk),
        in_specs=[pl.BlockSpec((tr, LANES), lambda b, i: (b * nblk + nblk - 1 - i, col_block)),
                  pl.BlockSpec((SUBLANES, LANES), lambda b, i: (0, 0)),
                  pl.BlockSpec((tr, tr), lambda b, i: (0, 0)),
                  pl.BlockSpec((tr, LANES), rev)],
        out_specs=[pl.BlockSpec((tr, LANES), rev), pl.BlockSpec((SUBLANES, LANES), lambda b, i: (0, 0))],
        out_shape=[jax.ShapeDtypeStruct((cfg.tokens, LANES), F32), jax.ShapeDtypeStruct((SUBLANES, LANES), F32)],
        scratch_shapes=[pltpu.VMEM((SUBLANES, LANES), F32)],
        compiler_params=_params(("arbitrary", "arbitrary")),
    )(proj, jnp.broadcast_to(bias, (SUBLANES, LANES)), _tri(tr, upper=True), dcum)
    return dz, jnp.sum(db, axis=0)


FOX_BLOCK = 256


def _fox_scores(q2, k2, e, half, mask, cref, ck_row):
    qe = jnp.where(half == (e == 0), q2, jnp.zeros_like(q2))
    s = lax.dot_general(qe, k2, _NT, preferred_element_type=F32)
    return jnp.where(mask, s + (cref - ck_row), NEG)


def _causal(qi, ki, tq):
    r = lax.broadcasted_iota(jnp.int32, (tq, tq), 0) + qi * tq
    c = lax.broadcasted_iota(jnp.int32, (tq, tq), 1) + ki * tq
    return r >= c


def _fox_fwd(q, kv, cum_t, cfg, *, name):
    t, d, hrows = cfg.tokens, cfg.d_model, cum_t.shape[0]
    tq = _blk(cfg.seq, FOX_BLOCK)
    nq = cfg.seq // tq

    def body(q_ref, k_ref, v_ref, cq_ref, ck_ref, o_ref, lse_ref, m_ref, l_ref, acc_ref):
        qi, ki = pl.program_id(1), pl.program_id(2)

        @pl.when(ki == 0)
        def _():
            m_ref[...] = jnp.full_like(m_ref, NEG)
            l_ref[...] = jnp.zeros_like(l_ref)
            acc_ref[...] = jnp.zeros_like(acc_ref)

        @pl.when(ki <= qi)
        def _():
            mask = _causal(qi, ki, tq)
            half = _half_mask((tq, LANES))
            for hp in range(d // LANES):
                sl = slice(hp * LANES, (hp + 1) * LANES)
                q2, k2, v2 = q_ref[:, sl], k_ref[:, sl], v_ref[:, sl]
                alphas, pvs = [], []
                for e in range(2):
                    h = 2 * hp + e
                    s = _fox_scores(q2, k2, e, half, mask, cq_ref[h:h + 1, 0:1], ck_ref[h:h + 1, :])
                    m_prev = m_ref[h]
                    m_new = jnp.maximum(m_prev, jnp.max(s, axis=1, keepdims=True))
                    alpha = jnp.exp(m_prev - m_new)
                    p = jnp.exp(s - m_new[:, 0:1])
                    l_ref[h] = alpha * l_ref[h] + jnp.sum(p, axis=1, keepdims=True)
                    m_ref[h] = m_new
                    alphas.append(alpha)
                    pvs.append(_dot2(p, v2))
                acc = acc_ref[:, sl]
                acc_ref[:, sl] = jnp.where(half, alphas[0] * acc + pvs[0], alphas[1] * acc + pvs[1])

        @pl.when(ki == qi)
        def _():
            half = _half_mask((tq, LANES))
            for hp in range(d // LANES):
                sl = slice(hp * LANES, (hp + 1) * LANES)
                h0, h1 = 2 * hp, 2 * hp + 1
                inv = jnp.where(half, 1.0 / l_ref[h0], 1.0 / l_ref[h1])
                o_ref[:, sl] = acc_ref[:, sl] * inv
                lse0 = m_ref[h0] + jnp.log(l_ref[h0]) - cq_ref[h0:h0 + 1, 0:1]
                lse1 = m_ref[h1] + jnp.log(l_ref[h1]) - cq_ref[h1:h1 + 1, 0:1]
                lse_ref[:, sl] = jnp.where(half, lse0, lse1)

    def qrow(b, qi, ki):
        return (b * nq + qi, 0)

    def krow(b, qi, ki):
        return (b * nq + jnp.minimum(ki, qi), 0)

    o, lse = pl.pallas_call(
        body, name=name, grid=(cfg.batch, nq, nq),
        in_specs=[pl.BlockSpec((tq, d), qrow),
                  pl.BlockSpec((tq, d), krow),
                  pl.BlockSpec((tq, d), lambda b, qi, ki: (b * nq + jnp.minimum(ki, qi), 1)),
                  pl.BlockSpec((hrows, tq), lambda b, qi, ki: (0, b * nq + qi)),
                  pl.BlockSpec((hrows, tq), lambda b, qi, ki: (0, b * nq + jnp.minimum(ki, qi)))],
        out_specs=[pl.BlockSpec((tq, d), qrow), pl.BlockSpec((tq, d), qrow)],
        out_shape=[jax.ShapeDtypeStruct((t, d), F32), jax.ShapeDtypeStruct((t, d), F32)],
        scratch_shapes=[pltpu.VMEM((cfg.heads, tq, LANES), F32), pltpu.VMEM((cfg.heads, tq, LANES), F32),
                        pltpu.VMEM((tq, d), F32)],
        compiler_params=_params(("parallel", "parallel", "arbitrary")),
    )(q, kv, kv, cum_t, cum_t)
    return o, lse


def _fox_bwd_q(q, kv, cum_t, do, lse, dsum, cfg, *, name):
    t, d, hrows = cfg.tokens, cfg.d_model, cum_t.shape[0]
    tq = _blk(cfg.seq, FOX_BLOCK)
    nq = cfg.seq // tq

    def body(q_ref, k_ref, v_ref, cq_ref, ck_ref, do_ref, l_ref, ds_ref, dq_ref, acc_ref):
        qi, ki = pl.program_id(1), pl.program_id(2)

        @pl.when(ki == 0)
        def _():
            acc_ref[...] = jnp.zeros_like(acc_ref)

        @pl.when(ki <= qi)
        def _():
            mask = _causal(qi, ki, tq)
            half = _half_mask((tq, LANES))
            for hp in range(d // LANES):
                sl = slice(hp * LANES, (hp + 1) * LANES)
                q2, k2, v2 = q_ref[:, sl], k_ref[:, sl], v_ref[:, sl]
                do2 = do_ref[:, sl].astype(BF16)
                dqs = []
                for e in range(2):
                    h = 2 * hp + e
                    lane0 = hp * LANES + e * HEAD_DIM
                    cref = cq_ref[h:h + 1, 0:1]
                    s = _fox_scores(q2, k2, e, half, mask, cref, ck_ref[h:h + 1, :])
                    p = jnp.exp(s - (l_ref[:, lane0:lane0 + 1] + cref))
                    doe = jnp.where(half == (e == 0), do2, jnp.zeros_like(do2))
                    dp = lax.dot_general(doe, v2, _NT, preferred_element_type=F32)
                    dsc = (p * (dp - ds_ref[:, lane0:lane0 + 1])).astype(BF16)
                    dqs.append(jnp.dot(dsc, k2, preferred_element_type=F32))
                acc_ref[:, sl] += jnp.where(half, dqs[0], dqs[1])

        @pl.when(ki == qi)
        def _():
            dq_ref[...] = acc_ref[...]

    def qrow(b, qi, ki):
        return (b * nq + qi, 0)

    return pl.pallas_call(
        body, name=name, grid=(cfg.batch, nq, nq),
        in_specs=[pl.BlockSpec((tq, d), qrow),
                  pl.BlockSpec((tq, d), lambda b, qi, ki: (b * nq + jnp.minimum(ki, qi), 0)),
                  pl.BlockSpec((tq, d), lambda b, qi, ki: (b * nq + jnp.minimum(ki, qi), 1)),
                  pl.BlockSpec((hrows, tq), lambda b, qi, ki: (0, b * nq + qi)),
                  pl.BlockSpec((hrows, tq), lambda b, qi, ki: (0, b * nq + jnp.minimum(ki, qi))),
                  pl.BlockSpec((tq, d), qrow), pl.BlockSpec((tq, d), qrow), pl.BlockSpec((tq, d), qrow)],
        out_specs=pl.BlockSpec((tq, d), qrow),
        out_shape=jax.ShapeDtypeStruct((t, d), F32),
        scratch_shapes=[pltpu.VMEM((tq, d), F32)],
        compiler_params=_params(("parallel", "parallel", "arbitrary")),
    )(q, kv, kv, cum_t, cum_t, do, lse, dsum)


def _fox_bwd_kv(q, kv, cum_t, do, lse, dsum, cfg, *, name):
    t, d, hrows = cfg.tokens, cfg.d_model, cum_t.shape[0]
    tq = _blk(cfg.seq, FOX_BLOCK)
    nq = cfg.seq // tq

    def body(q_ref, k_ref, v_ref, cq_ref, ck_ref, do_ref, l_ref, ds_ref, dk_ref, dv_ref, dc_ref,
             kacc_ref, vacc_ref, cacc_ref):
        ki, qi = pl.program_id(1), pl.program_id(2)

        @pl.when(qi == 0)
        def _():
            kacc_ref[...] = jnp.zeros_like(kacc_ref)
            vacc_ref[...] = jnp.zeros_like(vacc_ref)
            cacc_ref[...] = jnp.zeros_like(cacc_ref)

        @pl.when(qi >= ki)
        def _():
            mask = _causal(qi, ki, tq)
            half = _half_mask((tq, LANES))
            for hp in range(d // LANES):
                sl = slice(hp * LANES, (hp + 1) * LANES)
                q2, k2, v2 = q_ref[:, sl], k_ref[:, sl], v_ref[:, sl]
                do2 = do_ref[:, sl].astype(BF16)
                dks, dvs = [], []
                for e in range(2):
                    h = 2 * hp + e
                    lane0 = hp * LANES + e * HEAD_DIM
                    cref = cq_ref[h:h + 1, 0:1]
                    s = _fox_scores(q2, k2, e, half, mask, cref, ck_ref[h:h + 1, :])
                    p = jnp.exp(s - (l_ref[:, lane0:lane0 + 1] + cref))
                    doe = jnp.where(half == (e == 0), do2, jnp.zeros_like(do2))
                    dp = lax.dot_general(doe, v2, _NT, preferred_element_type=F32)
                    dsf = p * (dp - ds_ref[:, lane0:lane0 + 1])
                    cacc_ref[h:h + 1, :] -= jnp.sum(dsf, axis=0, keepdims=True)
                    dks.append(lax.dot_general(dsf.astype(BF16), q2, _TN, preferred_element_type=F32))
                    dvs.append(lax.dot_general(p.astype(BF16), do2, _TN, preferred_element_type=F32))
                kacc_ref[:, sl] += jnp.where(half, dks[0], dks[1])
                vacc_ref[:, sl] += jnp.where(half, dvs[0], dvs[1])

        @pl.when(qi == nq - 1)
        def _():
            dk_ref[...] = kacc_ref[...]
            dv_ref[...] = vacc_ref[...]
            dc_ref[...] = cacc_ref[...]

    def qrow(b, ki, qi):
        return (b * nq + jnp.maximum(qi, ki), 0)

    def krow(b, ki, qi):
        return (b * nq + ki, 0)

    return pl.pallas_call(
        body, name=name, grid=(cfg.batch, nq, nq),
        in_specs=[pl.BlockSpec((tq, d), qrow),
                  pl.BlockSpec((tq, d), krow),
                  pl.BlockSpec((tq, d), lambda b, ki, qi: (b * nq + ki, 1)),
                  pl.BlockSpec((hrows, tq), lambda b, ki, qi: (0, b * nq + jnp.maximum(qi, ki))),
                  pl.BlockSpec((hrows, tq), lambda b, ki, qi: (0, b * nq + ki)),
                  pl.BlockSpec((tq, d), qrow), pl.BlockSpec((tq, d), qrow), pl.BlockSpec((tq, d), qrow)],
        out_specs=[pl.BlockSpec((tq, d), krow), pl.BlockSpec((tq, d), krow),
                   pl.BlockSpec((hrows, tq), lambda b, ki, qi: (0, b * nq + ki))],
        out_shape=[jax.ShapeDtypeStruct((t, d), F32), jax.ShapeDtypeStruct((t, d), F32),
                   jax.ShapeDtypeStruct((hrows, t), F32)],
        scratch_shapes=[pltpu.VMEM((tq, d), F32), pltpu.VMEM((tq, d), F32), pltpu.VMEM((hrows, tq), F32)],
        compiler_params=_params(("parallel", "parallel", "arbitrary")),
    )(q, kv, kv, cum_t, cum_t, do, lse, dsum)


AUG = LANES
BIAS_TERMS = 3


def _fox_aug_q(qp, cfg):
    t, hh = cfg.tokens, cfg.heads
    q3 = qp.reshape(t, hh, HEAD_DIM)
    ones = jnp.ones((t, hh, BIAS_TERMS), BF16)
    zeros = jnp.zeros((t, hh, AUG - HEAD_DIM - BIAS_TERMS), BF16)
    return jnp.concatenate([q3, ones, zeros], axis=2).reshape(t, hh * AUG).T


def _fox_aug_k(k, cum, cfg):
    t, hh = cfg.tokens, cfg.heads
    c = -cum
    hi = lax.reduce_precision(c, 8, 7)
    mid = lax.reduce_precision(c - hi, 8, 7)
    lo = c - hi - mid
    zeros = jnp.zeros((t, hh, AUG - HEAD_DIM - BIAS_TERMS), BF16)
    parts = [k.reshape(t, hh, HEAD_DIM)] + [z.astype(BF16)[..., None] for z in (hi, mid, lo)] + [zeros]
    return jnp.concatenate(parts, axis=2).reshape(t, hh * AUG)


def _fox_aug_k_call(kv, cum, cfg, *, name):
    t, d, hh = cfg.tokens, cfg.d_model, cfg.heads
    tr = _blk(t, ROW_BLOCK)

    def body(k_ref, c_ref, o_ref):
        lane = lax.broadcasted_iota(jnp.int32, (tr, LANES), 1)
        for hp in range(hh // 2):
            k2 = k_ref[:, hp * LANES:(hp + 1) * LANES].astype(F32)
            for e in range(2):
                h = 2 * hp + e
                kh = k2 if e == 0 else pltpu.roll(k2, HEAD_DIM, 1)
                c = -c_ref[:, h:h + 1]
                hi = c.astype(BF16).astype(F32)
                mid = (c - hi).astype(BF16).astype(F32)
                lo = c - hi - mid
                bias = jnp.where(lane == HEAD_DIM, hi, jnp.where(lane == HEAD_DIM + 1, mid,
                                 jnp.where(lane == HEAD_DIM + 2, lo, 0.0)))
                o_ref[:, h * AUG:(h + 1) * AUG] = jnp.where(lane < HEAD_DIM, kh, bias).astype(BF16)

    return pl.pallas_call(
        body, name=name, grid=(t // tr,),
        in_specs=[pl.BlockSpec((tr, d), lambda i: (i, 0)), pl.BlockSpec((tr, LANES), lambda i: (i, 0))],
        out_specs=pl.BlockSpec((tr, hh * AUG), lambda i: (i, 0)),
        out_shape=jax.ShapeDtypeStruct((t, hh * AUG), BF16),
        compiler_params=_params(("parallel",)),
    )(kv, cum)


def _keys_visible(tq):
    s = lax.broadcasted_iota(jnp.int32, (tq, tq), 0)
    t = lax.broadcasted_iota(jnp.int32, (tq, tq), 1)
    return s <= t


def _fox_fwd_t(qa_t, k_aug, v_t, cfg, *, name):
    t, d, hh = cfg.tokens, cfg.d_model, cfg.heads
    tq = _blk(cfg.seq, FOX_BLOCK)
    nq = cfg.seq // tq

    def body(qa_ref, ka_ref, vt_ref, o_ref, lse_ref, m_ref, l_ref, acc_ref):
        qi, ki = pl.program_id(1), pl.program_id(2)

        @pl.when(ki == 0)
        def _():
            m_ref[...] = jnp.full_like(m_ref, NEG)
            l_ref[...] = jnp.zeros_like(l_ref)
            acc_ref[...] = jnp.zeros_like(acc_ref)

        def step(diagonal):
            for h in range(hh):
                rows = slice(h * HEAD_DIM, (h + 1) * HEAD_DIM)
                s = jnp.dot(ka_ref[:, h * AUG:(h + 1) * AUG], qa_ref[h * AUG:(h + 1) * AUG, :],
                            preferred_element_type=F32)
                if diagonal:
                    s = jnp.where(_keys_visible(tq), s, NEG)
                m_prev = m_ref[h:h + 1, :]
                m_new = jnp.maximum(m_prev, jnp.max(s, axis=0, keepdims=True))
                alpha = jnp.exp(m_prev - m_new)
                p = jnp.exp(s - m_new)
                l_ref[h:h + 1, :] = alpha * l_ref[h:h + 1, :] + jnp.sum(p, axis=0, keepdims=True)
                m_ref[h:h + 1, :] = m_new
                hi = p.astype(BF16)
                lo = (p - hi.astype(F32)).astype(BF16)
                vt = vt_ref[rows, :]
                acc_ref[rows, :] = (alpha * acc_ref[rows, :] + jnp.dot(vt, hi, preferred_element_type=F32)
                                    + jnp.dot(vt, lo, preferred_element_type=F32))

        pl.when(ki < qi)(functools.partial(step, False))
        pl.when(ki == qi)(functools.partial(step, True))

        @pl.when(ki == qi)
        def _():
            for h in range(hh):
                rows = slice(h * HEAD_DIM, (h + 1) * HEAD_DIM)
                o_ref[rows, :] = acc_ref[rows, :] * (1.0 / l_ref[h:h + 1, :])
            lse_ref[...] = m_ref[...] + jnp.log(l_ref[...])

    def qcol(b, qi, ki):
        return (0, b * nq + qi)

    return pl.pallas_call(
        body, name=name, grid=(cfg.batch, nq, nq),
        in_specs=[pl.BlockSpec((hh * AUG, tq), qcol),
                  pl.BlockSpec((tq, hh * AUG), lambda b, qi, ki: (b * nq + jnp.minimum(ki, qi), 0)),
                  pl.BlockSpec((d, tq), lambda b, qi, ki: (0, b * nq + jnp.minimum(ki, qi)))],
        out_specs=[pl.BlockSpec((d, tq), qcol), pl.BlockSpec((hh, tq), qcol)],
        out_shape=[jax.ShapeDtypeStruct((d, t), F32), jax.ShapeDtypeStruct((hh, t), F32)],
        scratch_shapes=[pltpu.VMEM((hh, tq), F32), pltpu.VMEM((hh, tq), F32), pltpu.VMEM((d, tq), F32)],
        compiler_params=_params(("parallel", "parallel", "arbitrary")),
    )(qa_t, k_aug, v_t)


def _aug_q_t(q2, e, tq):
    ones = (lax.broadcasted_iota(jnp.int32, (AUG - HEAD_DIM, tq), 0) < BIAS_TERMS).astype(q2.dtype)
    return jnp.concatenate([q2[e * HEAD_DIM:(e + 1) * HEAD_DIM], ones], axis=0)


def _fox_fwd_n(q, k_aug, v, cfg, *, name):
    t, d, hh = cfg.tokens, cfg.d_model, cfg.heads
    tq = _blk(cfg.seq, FOX_BLOCK)
    nq = cfg.seq // tq

    def body(q_ref, ka_ref, v_ref, o_ref, lse_ref, qa_ref, m_ref, l_ref, acc_ref):
        qi, ki = pl.program_id(1), pl.program_id(2)

        @pl.when(ki == 0)
        def _():
            m_ref[...] = jnp.full_like(m_ref, NEG)
            l_ref[...] = jnp.zeros_like(l_ref)
            acc_ref[...] = jnp.zeros_like(acc_ref)
            for hp in range(hh // 2):
                q2 = q_ref[:, hp * LANES:(hp + 1) * LANES].T
                for e in range(2):
                    h = 2 * hp + e
                    qa_ref[h * AUG:(h + 1) * AUG, :] = _aug_q_t(q2, e, tq)

        def step(diagonal):
            for hp in range(hh // 2):
                vt2 = v_ref[:, hp * LANES:(hp + 1) * LANES].T
                for e in range(2):
                    h = 2 * hp + e
                    rows = slice(h * HEAD_DIM, (h + 1) * HEAD_DIM)
                    s = jnp.dot(ka_ref[:, h * AUG:(h + 1) * AUG], qa_ref[h * AUG:(h + 1) * AUG, :],
                                preferred_element_type=F32)
                    if diagonal:
                        s = jnp.where(_keys_visible(tq), s, NEG)
                    m_prev = m_ref[h:h + 1, :]
                    m_new = jnp.maximum(m_prev, jnp.max(s, axis=0, keepdims=True))
                    alpha = jnp.exp(m_prev - m_new)
                    p = jnp.exp(s - m_new)
                    l_ref[h:h + 1, :] = alpha * l_ref[h:h + 1, :] + jnp.sum(p, axis=0, keepdims=True)
                    m_ref[h:h + 1, :] = m_new
                    hi = p.astype(BF16)
                    lo = (p - hi.astype(F32)).astype(BF16)
                    vt = vt2[e * HEAD_DIM:(e + 1) * HEAD_DIM]
                    acc_ref[rows, :] = (alpha * acc_ref[rows, :] + jnp.dot(vt, hi, preferred_element_type=F32)
                                        + jnp.dot(vt, lo, preferred_element_type=F32))

        pl.when(ki < qi)(functools.partial(step, False))
        pl.when(ki == qi)(functools.partial(step, True))

        @pl.when(ki == qi)
        def _():
            for hp in range(hh // 2):
                halves = [acc_ref[h * HEAD_DIM:(h + 1) * HEAD_DIM, :] * (1.0 / l_ref[h:h + 1, :])
                          for h in (2 * hp, 2 * hp + 1)]
                o_ref[:, hp * LANES:(hp + 1) * LANES] = jnp.concatenate(halves, axis=0).T
            lse_ref[...] = m_ref[...] + jnp.log(l_ref[...])

    def qrow(b, qi, ki):
        return (b * nq + qi, 0)

    def krow(b, qi, ki):
        return (b * nq + jnp.minimum(ki, qi), 0)

    return pl.pallas_call(
        body, name=name, grid=(cfg.batch, nq, nq),
        in_specs=[pl.BlockSpec((tq, d), qrow), pl.BlockSpec((tq, hh * AUG), krow), pl.BlockSpec((tq, d), krow)],
        out_specs=[pl.BlockSpec((tq, d), qrow), pl.BlockSpec((hh, tq), lambda b, qi, ki: (0, b * nq + qi))],
        out_shape=[jax.ShapeDtypeStruct((t, d), F32), jax.ShapeDtypeStruct((hh, t), F32)],
        scratch_shapes=[pltpu.VMEM((hh * AUG, tq), BF16), pltpu.VMEM((hh, tq), F32), pltpu.VMEM((hh, tq), F32),
                        pltpu.VMEM((d, tq), F32)],
        compiler_params=_params(("parallel", "parallel", "arbitrary")),
    )(q, k_aug, v)


def _head_dot_c(a, b, cfg, *, name):
    t, d, hh = cfg.tokens, cfg.d_model, cfg.heads
    tc = _blk(t, ROW_BLOCK)

    def body(a_ref, b_ref, o_ref):
        for hp in range(hh // 2):
            pair = slice(hp * LANES, (hp + 1) * LANES)
            prod = (a_ref[:, pair].astype(F32) * b_ref[:, pair]).T
            for e in range(2):
                h = 2 * hp + e
                o_ref[h:h + 1, :] = jnp.sum(prod[e * HEAD_DIM:(e + 1) * HEAD_DIM], axis=0, keepdims=True)

    return pl.pallas_call(
        body, name=name, grid=(t // tc,),
        in_specs=[pl.BlockSpec((tc, d), lambda i: (i, 0)), pl.BlockSpec((tc, d), lambda i: (i, 0))],
        out_specs=pl.BlockSpec((hh, tc), lambda i: (0, i)),
        out_shape=jax.ShapeDtypeStruct((hh, t), F32),
        compiler_params=_params(("parallel",)),
    )(a, b)


def _fox_bwd_n(q, k_aug, k_t, v, do, lse, dsum, cfg, *, name):
    t, d, hh = cfg.tokens, cfg.d_model, cfg.heads
    tq = _blk(cfg.seq, FOX_BLOCK)
    nq = cfg.seq // tq

    def body(q_ref, ka_ref, kt_ref, v_ref, do_ref, lse_ref, ds_ref, dq_hbm, dk_ref, dv_ref, dc_ref, dq_acc, sem):
        b, ki, qi = pl.program_id(0), pl.program_id(1), pl.program_id(2)
        qq = jnp.maximum(qi, ki)

        @pl.when((ki == 0) & (qi == 0))
        def _():
            dq_acc[...] = jnp.zeros_like(dq_acc)

        @pl.when(qi == 0)
        def _():
            dk_ref[...] = jnp.zeros_like(dk_ref)
            dv_ref[...] = jnp.zeros_like(dv_ref)
            dc_ref[...] = jnp.zeros_like(dc_ref)

        def step(diagonal):
            upper = lax.broadcasted_iota(jnp.int32, (LANES, tq), 0) < HEAD_DIM
            half = _half_mask((tq, LANES))
            for hp in range(hh // 2):
                pair = slice(hp * LANES, (hp + 1) * LANES)
                q2 = q_ref[:, pair].T
                don2 = do_ref[:, pair]
                dot2 = don2.T
                dvs = []
                for e in range(2):
                    h = 2 * hp + e
                    rows = slice(h * HEAD_DIM, (h + 1) * HEAD_DIM)
                    aug = slice(h * AUG, (h + 1) * AUG)
                    qa = _aug_q_t(q2, e, tq)
                    s = jnp.dot(ka_ref[:, aug], qa, preferred_element_type=F32)
                    if diagonal:
                        s = jnp.where(_keys_visible(tq), s, NEG)
                    p = jnp.exp(s - lse_ref[h:h + 1, :])
                    dote = jnp.where(upper == (e == 0), dot2, jnp.zeros_like(dot2))
                    dp = jnp.dot(v_ref[:, pair], dote, preferred_element_type=F32)
                    dsf = p * (dp - ds_ref[h:h + 1, :])
                    dc_ref[:, h:h + 1] -= jnp.sum(dsf, axis=1, keepdims=True)
                    dsc = dsf.astype(BF16)
                    dvs.append(jnp.dot(p.astype(BF16), don2, preferred_element_type=F32))
                    dk_ref[:, aug] += lax.dot_general(dsc, qa, _NT, preferred_element_type=F32)
                    dq_acc[qq, rows, :] += jnp.dot(kt_ref[rows, :], dsc, preferred_element_type=F32)
                dv_ref[:, pair] += jnp.where(half, dvs[0], dvs[1])

        pl.when(qi > ki)(functools.partial(step, False))
        pl.when(qi == ki)(functools.partial(step, True))

        @pl.when((ki == nq - 1) & (qi == nq - 1))
        def _():
            cp = pltpu.make_async_copy(dq_acc, dq_hbm.at[b], sem)
            cp.start()
            cp.wait()

    def qrow(b, ki, qi):
        return (b * nq + jnp.maximum(qi, ki), 0)

    def qcol(b, ki, qi):
        return (0, b * nq + jnp.maximum(qi, ki))

    def krow(b, ki, qi):
        return (b * nq + ki, 0)

    return pl.pallas_call(
        body, name=name, grid=(cfg.batch, nq, nq),
        in_specs=[pl.BlockSpec((tq, d), qrow),
                  pl.BlockSpec((tq, hh * AUG), krow),
                  pl.BlockSpec((d, tq), lambda b, ki, qi: (0, b * nq + ki)),
                  pl.BlockSpec((tq, d), krow),
                  pl.BlockSpec((tq, d), qrow),
                  pl.BlockSpec((hh, tq), qcol), pl.BlockSpec((hh, tq), qcol)],
        out_specs=[pl.BlockSpec(memory_space=pl.ANY), pl.BlockSpec((tq, hh * AUG), krow),
                   pl.BlockSpec((tq, d), krow), pl.BlockSpec((tq, LANES), krow)],
        out_shape=[jax.ShapeDtypeStruct((cfg.batch, nq, d, tq), F32), jax.ShapeDtypeStruct((t, hh * AUG), F32),
                   jax.ShapeDtypeStruct((t, d), F32), jax.ShapeDtypeStruct((t, LANES), F32)],
        scratch_shapes=[pltpu.VMEM((nq, d, tq), F32), pltpu.SemaphoreType.DMA],
        compiler_params=_params(("arbitrary", "arbitrary", "arbitrary")),
    )(q, k_aug, k_t, v, do, lse, dsum)


def _head_dot_t(a_t, b_t, cfg, *, name):
    t, d, hh = cfg.tokens, cfg.d_model, cfg.heads
    tc = _blk(t, 2 * ROW_BLOCK)

    def body(a_ref, b_ref, o_ref):
        for h in range(hh):
            rows = slice(h * HEAD_DIM, (h + 1) * HEAD_DIM)
            o_ref[h:h + 1, :] = jnp.sum(a_ref[rows, :].astype(F32) * b_ref[rows, :], axis=0, keepdims=True)

    return pl.pallas_call(
        body, name=name, grid=(t // tc,),
        in_specs=[pl.BlockSpec((d, tc), lambda i: (0, i)), pl.BlockSpec((d, tc), lambda i: (0, i))],
        out_specs=pl.BlockSpec((hh, tc), lambda i: (0, i)),
        out_shape=jax.ShapeDtypeStruct((hh, t), F32),
        compiler_params=_params(("parallel",)),
    )(a_t, b_t)


def _fox_bwd_t(qa_t, k_aug, k_t, v, do_t, do, lse, dsum, cfg, *, name):
    t, d, hh = cfg.tokens, cfg.d_model, cfg.heads
    tq = _blk(cfg.seq, FOX_BLOCK)
    nq = cfg.seq // tq

    def body(qa_ref, ka_ref, kt_ref, v_ref, dot_ref, do_ref, lse_ref, ds_ref, dq_hbm, dk_ref, dv_ref, dc_ref,
             dq_acc, sem):
        b, ki, qi = pl.program_id(0), pl.program_id(1), pl.program_id(2)
        qq = jnp.maximum(qi, ki)

        @pl.when((ki == 0) & (qi == 0))
        def _():
            dq_acc[...] = jnp.zeros_like(dq_acc)

        @pl.when(qi == 0)
        def _():
            dk_ref[...] = jnp.zeros_like(dk_ref)
            dv_ref[...] = jnp.zeros_like(dv_ref)
            dc_ref[...] = jnp.zeros_like(dc_ref)

        def step(diagonal):
            upper = lax.broadcasted_iota(jnp.int32, (LANES, tq), 0) < HEAD_DIM
            half = _half_mask((tq, LANES))
            for hp in range(hh // 2):
                pair = slice(hp * LANES, (hp + 1) * LANES)
                dvs = []
                for e in range(2):
                    h = 2 * hp + e
                    rows = slice(h * HEAD_DIM, (h + 1) * HEAD_DIM)
                    aug = slice(h * AUG, (h + 1) * AUG)
                    s = jnp.dot(ka_ref[:, aug], qa_ref[aug, :], preferred_element_type=F32)
                    if diagonal:
                        s = jnp.where(_keys_visible(tq), s, NEG)
                    p = jnp.exp(s - lse_ref[h:h + 1, :])
                    dot2 = dot_ref[pair, :]
                    dote = jnp.where(upper == (e == 0), dot2, jnp.zeros_like(dot2))
                    dp = jnp.dot(v_ref[:, pair], dote, preferred_element_type=F32)
                    dsf = p * (dp - ds_ref[h:h + 1, :])
                    dc_ref[:, h:h + 1] -= jnp.sum(dsf, axis=1, keepdims=True)
                    dsc = dsf.astype(BF16)
                    dvs.append(jnp.dot(p.astype(BF16), do_ref[:, pair], preferred_element_type=F32))
                    dk_ref[:, aug] += lax.dot_general(dsc, qa_ref[aug, :], _NT, preferred_element_type=F32)
                    dq_acc[qq, rows, :] += jnp.dot(kt_ref[rows, :], dsc, preferred_element_type=F32)
                dv_ref[:, pair] += jnp.where(half, dvs[0], dvs[1])

        pl.when(qi > ki)(functools.partial(step, False))
        pl.when(qi == ki)(functools.partial(step, True))

        @pl.when((ki == nq - 1) & (qi == nq - 1))
        def _():
            cp = pltpu.make_async_copy(dq_acc, dq_hbm.at[b], sem)
            cp.start()
            cp.wait()

    def qcol(b, ki, qi):
        return (0, b * nq + jnp.maximum(qi, ki))

    def krow(b, ki, qi):
        return (b * nq + ki, 0)

    return pl.pallas_call(
        body, name=name, grid=(cfg.batch, nq, nq),
        in_specs=[pl.BlockSpec((hh * AUG, tq), qcol),
                  pl.BlockSpec((tq, hh * AUG), krow),
                  pl.BlockSpec((d, tq), lambda b, ki, qi: (0, b * nq + ki)),
                  pl.BlockSpec((tq, d), krow),
                  pl.BlockSpec((d, tq), qcol),
                  pl.BlockSpec((tq, d), lambda b, ki, qi: (b * nq + jnp.maximum(qi, ki), 0)),
                  pl.BlockSpec((hh, tq), qcol), pl.BlockSpec((hh, tq), qcol)],
        out_specs=[pl.BlockSpec(memory_space=pl.ANY), pl.BlockSpec((tq, hh * AUG), krow),
                   pl.BlockSpec((tq, d), krow), pl.BlockSpec((tq, LANES), krow)],
        out_shape=[jax.ShapeDtypeStruct((cfg.batch, nq, d, tq), F32), jax.ShapeDtypeStruct((t, hh * AUG), F32),
                   jax.ShapeDtypeStruct((t, d), F32), jax.ShapeDtypeStruct((t, LANES), F32)],
        scratch_shapes=[pltpu.VMEM((nq, d, tq), F32), pltpu.SemaphoreType.DMA],
        compiler_params=_params(("arbitrary", "arbitrary", "arbitrary")),
    )(qa_t, k_aug, k_t, v, do_t, do, lse, dsum)


WIDE = 1536


FUSED_ROWS = 512


def _fwd(a, w, *, name, res=None, scale=1.0, norms=()):
    bm = FUSED_ROWS if norms else 1024
    out = _mm(a, w, form="F", out_dtype=F32, name=name, bm=bm, bn=WIDE, bk=WIDE, res=res, scale=scale, norms=norms)
    return (out[0], list(out[1:])) if norms else out


def _bwd(dy, w, *, name, scale=1.0, norm_bwd=None):
    bm = FUSED_ROWS if norm_bwd is not None else 1024
    return _mm(dy, w, form="B", out_dtype=F32, name=name, bm=bm, bn=WIDE, bk=WIDE, scale=scale, norm_bwd=norm_bwd)


def _wgrad(a, dy, w, *, name, scale=1.0):
    return _mm_grad(a, dy, w.shape[0], name=name, bm=WIDE, bn=WIDE, scale=scale)


def _ffn_fwd(h, n, w_in, w_out, tag, norms=()):
    gate, up, a = _ffn_in_act(n, w_in, name=f"{tag}_in")
    out = _fwd(a, w_out, name=f"{tag}_out", res=h, scale=0.5, norms=norms)
    h_out, normed = out if norms else (out, [])
    return h_out, normed, (n, gate, up, a)


def _ffn_bwd(dh_out, h, g, w_in, w_out, saved, tag):
    n, gate, up, a = saved
    du = _ffn_out_dx_act(dh_out, w_out, gate, up, name=f"{tag}_out_dx", scale=0.5)
    dw_out = _wgrad(a, dh_out, w_out, name=f"{tag}_out_dw", scale=0.5)
    dh, dg = _mm_back2(du, w_in, (h, g, dh_out), name=f"{tag}_in_dx", bk=WIDE)
    dw_in = _mm_grad(n, du, w_in.shape[0], name=f"{tag}_in_dw", bm=WIDE, bn=WIDE)
    return dh, dg, dw_in, dw_out


def _head_gain(g, heads, scale=1.0):
    return jnp.tile(g.astype(F32) * scale, heads)


def _local_step(cfg, x, positions, target, w, s):
    d, hh = cfg.d_model, cfg.heads
    cos, sin = _rope_tables(positions)
    ones = jnp.ones((d,), F32)

    n00 = _rms_fwd(x, s["ffn_norm"][0, 0], name="ffn00_norm")
    h1, (hn_a,), ffn0 = _ffn_fwd(x, n00, w["ffn_w_in"][0][0], w["ffn_w_out"][0][0], "ffn00", [s["mix_norm"][0]])
    qkv = _fwd(hn_a, w["a_w_qkv"], name="a_qkv")
    kinds_a = ["rope", "rope", "cast"] * len(DILATIONS)
    gains_a = jnp.stack([z for g in range(len(DILATIONS)) for z in (
        _head_gain(s["a_q_norm"][g], hh, Q_SCALE), _head_gain(s["a_k_norm"][g], hh), ones)])
    qkvp = [_hn_fwd(qkv, gains_a[3 * g:3 * g + 3], kinds_a[:3], d, cos, sin, name=f"a_qk_norm{g}", col0=3 * g)
            for g in range(len(DILATIONS))]
    lay = [qkvp[g].reshape(cfg.tokens // dil, dil * 3 * d) for g, dil in enumerate(DILATIONS)]
    band = [_band_fwd_n(lay[g], dil, cfg, name=f"a_band{g}") for g, dil in enumerate(DILATIONS)]
    mixed, lse_a = _mix_fwd([o.reshape(cfg.tokens, d) for o, _ in band],
                            [jnp.repeat(_from_classes_t(l), HEAD_DIM, axis=1) for _, l in band], name="a_mix")
    h2, (n01,) = _fwd(mixed, w["a_w_o"], name="a_out", res=h1, norms=[s["ffn_norm"][0, 1]])
    h3, (kn, n10), ffn1 = _ffn_fwd(h2, n01, w["ffn_w_in"][0][1], w["ffn_w_out"][0][1], "ffn01",
                                   [s["kv_norm"], s["ffn_norm"][1, 0]])

    proj = _fwd(kn, w["kv_w"], name="kv_proj")
    kinds_kv = ["norm", "cast"]
    gains_kv = jnp.stack([_head_gain(s["kv_k_norm"], hh), ones])
    kvp = _hn_fwd(proj, gains_kv, kinds_kv, d, cos, sin, name="kv_k_norm")
    gate_col = 2 * d // LANES
    bias = jnp.pad(s["kv_b_f"].astype(F32), (0, LANES - hh))
    cum = _gate_fwd(proj, gate_col, bias, cfg, name="kv_gate")
    k_b, v_b = kvp[:, :d], kvp[:, d:]
    k_aug = _fox_aug_k_call(kvp, cum, cfg, name="kv_aug")

    h4, (hn_b,), ffn2 = _ffn_fwd(h3, n10, w["ffn_w_in"][1][0], w["ffn_w_out"][1][0], "ffn10", [s["mix_norm"][1]])
    qraw = _fwd(hn_b, w["b_w_q"], name="b_q")
    gains_b = _head_gain(s["b_q_norm"][0], hh, Q_SCALE)[None]
    qp = _hn_fwd(qraw, gains_b, ["norm"], d, cos, sin, name="b_q_norm")
    o_b, lse_b = _fox_fwd_n(qp, k_aug, v_b, cfg, name="b_fox")
    h5, (n11,) = _fwd(o_b, w["b_w_o"], name="b_out", res=h4, norms=[s["ffn_norm"][1, 1]])
    h6, _, ffn3 = _ffn_fwd(h5, n11, w["ffn_w_in"][1][1], w["ffn_w_out"][1][1], "ffn11")

    loss, dh6 = _loss_fwd_bwd(h6, target, name="loss")

    dh5, dg11, dwi11, dwo11 = _ffn_bwd(dh6, h5, s["ffn_norm"][1, 1], w["ffn_w_in"][1][1], w["ffn_w_out"][1][1],
                                       ffn3, "ffn11")
    do_b = _bwd(dh5, w["b_w_o"], name="b_out_dx")
    dw_bo = _wgrad(o_b, dh5, w["b_w_o"], name="b_out_dw")
    do_bf = do_b.astype(BF16)
    dsum_b = _head_dot_c(do_bf, o_b, cfg, name="b_dsum")
    dq4, dk_aug, dv_b, dcum = _fox_bwd_n(qp, k_aug, k_b.T, v_b, do_bf, lse_b, dsum_b, cfg, name="b_fox_bwd")
    dq_b = dq4.transpose(0, 1, 3, 2).reshape(cfg.tokens, d)
    dk_b = dk_aug.reshape(cfg.tokens, hh, AUG)[:, :, :HEAD_DIM].reshape(cfg.tokens, d)
    dqraw, dgq = _hn_bwd(qraw, [dq_b], gains_b, ["norm"], d, cos, sin, name="b_q_norm_bwd")
    dh4, dmix1 = _bwd(dqraw, w["b_w_q"], name="b_q_dx", norm_bwd=(h4, s["mix_norm"][1], dh5))
    dw_bq = _wgrad(hn_b, dqraw, w["b_w_q"], name="b_q_dw")
    dh3, dg10, dwi10, dwo10 = _ffn_bwd(dh4, h3, s["ffn_norm"][1, 0], w["ffn_w_in"][1][0], w["ffn_w_out"][1][0],
                                       ffn2, "ffn10")

    dkvraw, dgk = _hn_bwd(proj, [dk_b, dv_b], gains_kv, kinds_kv, d, cos, sin,
                          name="kv_k_norm_bwd")
    dz, dbias = _gate_bwd(proj, gate_col, bias, dcum, cfg, name="kv_gate_bwd")
    pad_cols = w["kv_w"].shape[2] - 2 * d - LANES
    dproj = jnp.concatenate([dkvraw, dz.astype(BF16), jnp.zeros((cfg.tokens, pad_cols), BF16)], axis=1)
    dw_kv = _wgrad(kn, dproj, w["kv_w"], name="kv_proj_dw")
    dh3, dkvn = _bwd(dproj, w["kv_w"], name="kv_proj_dx", norm_bwd=(h3, s["kv_norm"], dh3))

    dh2, dg01, dwi01, dwo01 = _ffn_bwd(dh3, h2, s["ffn_norm"][0, 1], w["ffn_w_in"][0][1], w["ffn_w_out"][0][1],
                                       ffn1, "ffn01")
    dmixed = _bwd(dh2, w["a_w_o"], name="a_out_dx")
    dw_ao = _wgrad(mixed, dh2, w["a_w_o"], name="a_out_dw")
    dsum_a = _head_dot(dmixed, mixed, name="a_dsum")
    dqkvp = []
    dmixed_bf = dmixed.astype(BF16)
    lse_h, dsum_h = lse_a[:, ::HEAD_DIM], dsum_a[:, ::HEAD_DIM]
    for g, dil in enumerate(DILATIONS):
        grads = _band_bwd_n(lay[g], dmixed_bf.reshape(cfg.tokens // dil, dil * d), _to_classes_t(lse_h, dil, hh),
                            _to_classes_t(dsum_h, dil, hh), dil, cfg, name=f"a_band{g}_bwd")
        dqkvp += [z.reshape(cfg.tokens, d) for z in grads]
    dqkv, dga = _hn_bwd(qkv, dqkvp, gains_a, kinds_a, d, cos, sin, name="a_qk_norm_bwd")
    dh1, dmix0 = _bwd(dqkv, w["a_w_qkv"], name="a_qkv_dx", norm_bwd=(h1, s["mix_norm"][0], dh2))
    dw_qkv = _wgrad(hn_a, dqkv, w["a_w_qkv"], name="a_qkv_dw")
    dx, dg00, dwi00, dwo00 = _ffn_bwd(dh1, x, s["ffn_norm"][0, 0], w["ffn_w_in"][0][0], w["ffn_w_out"][0][0],
                                      ffn0, "ffn00")

    dw = {
        "ffn_w_in": [[dwi00, dwi01], [dwi10, dwi11]],
        "ffn_w_out": [[dwo00, dwo01], [dwo10, dwo11]],
        "a_w_qkv": dw_qkv, "a_w_o": dw_ao, "kv_w": dw_kv, "b_w_q": dw_bq, "b_w_o": dw_bo,
    }
    ds = {
        "ffn_norm": jnp.stack([jnp.stack([dg00, dg01]), jnp.stack([dg10, dg11])]),
        "mix_norm": jnp.stack([dmix0, dmix1]),
        "a_q_norm": jnp.stack([dga[3 * g] for g in range(len(DILATIONS))])[None] * Q_SCALE,
        "a_k_norm": jnp.stack([dga[3 * g + 1] for g in range(len(DILATIONS))])[None],
        "kv_norm": dkvn,
        "kv_b_f": dbias[:hh],
        "kv_k_norm": dgk[0],
        "b_q_norm": dgq * Q_SCALE,
    }
    return loss, dx, dw, ds


MESH_ID = pl.DeviceIdType.MESH
ANY = pl.BlockSpec(memory_space=pl.ANY)
PACK_COLS = 1024
PACK_ROW_ALIGN = 32


def _me():
    return lax.axis_index("x"), lax.axis_index("y"), lax.axis_index("c")


def _other_chips(x, y):
    return [(1 - x, y), (x, 1 - y), (1 - x, 1 - y)]


def _all_gather_small(v, *, name):
    r = v.shape[0]

    def body(v_ref, out_ref, send_sems, recv_sems):
        x, y, c = _me()
        me = 4 * x + 2 * y + c
        out_ref[me] = v_ref[...]
        copies = []
        for k in range(1, N_DEV):
            fx, fy, fc = (k >> 2) & 1, (k >> 1) & 1, k & 1
            peer = (1 - x if fx else x, 1 - y if fy else y, 1 - c if fc else c)
            copies.append(pltpu.make_async_remote_copy(
                src_ref=v_ref, dst_ref=out_ref.at[me], send_sem=send_sems.at[k - 1], recv_sem=recv_sems.at[k - 1],
                device_id=peer, device_id_type=MESH_ID))
        for cp in copies:
            cp.start()
        for cp in copies:
            cp.wait()

    return pl.pallas_call(
        body, name=name,
        in_specs=[pl.BlockSpec(memory_space=pltpu.VMEM)], out_specs=pl.BlockSpec(memory_space=pltpu.VMEM),
        out_shape=jax.ShapeDtypeStruct((N_DEV, r, LANES), v.dtype),
        scratch_shapes=[pltpu.SemaphoreType.DMA((N_DEV - 1,)), pltpu.SemaphoreType.DMA((N_DEV - 1,))],
    )(v)


def _all_gather_chips(v, *, name):
    rh = v.shape[0] // 2

    def body(v_ref, out_ref, send_sems, recv_sems):
        x, y, c = _me()
        j = 2 * x + y
        chips = _other_chips(x, y)

        def half(chip, core):
            return out_ref.at[chip, pl.ds(core * rh, rh)]

        first = [pltpu.make_async_remote_copy(
            src_ref=v_ref.at[pl.ds(c * rh, rh)], dst_ref=half(j, c), send_sem=send_sems.at[k],
            recv_sem=recv_sems.at[k], device_id=(px, py, c), device_id_type=MESH_ID)
            for k, (px, py) in enumerate(chips)]
        for cp in first:
            cp.start()
        passed = [pltpu.make_async_remote_copy(
            src_ref=half(2 * px + py, c), dst_ref=half(2 * px + py, c), send_sem=send_sems.at[3 + k],
            recv_sem=recv_sems.at[3 + k], device_id=(x, y, 1 - c), device_id_type=MESH_ID)
            for k, (px, py) in enumerate(chips)]
        for k in range(len(chips)):
            first[k].wait_recv()
            passed[k].start()
        for k, (px, py) in enumerate(chips):
            pltpu.make_async_remote_copy(
                src_ref=half(2 * px + py, 1 - c), dst_ref=half(2 * px + py, 1 - c), send_sem=send_sems.at[3 + k],
                recv_sem=recv_sems.at[3 + k], device_id=(x, y, 1 - c), device_id_type=MESH_ID).wait_recv()
        for cp in first + passed:
            cp.wait_send()

    return pl.pallas_call(
        body, name=name, in_specs=[ANY], out_specs=ANY,
        out_shape=jax.ShapeDtypeStruct((N_CHIPS,) + v.shape, v.dtype),
        scratch_shapes=[pltpu.SemaphoreType.DMA((2 * (N_CHIPS - 1),)), pltpu.SemaphoreType.DMA((2 * (N_CHIPS - 1),))],
    )(v)


def _swap_halves(g, *, name):
    n, r, cols = g.shape
    rh = r // 2

    def body(g_ref, out_ref, send_sem, recv_sem):
        x, y, c = _me()
        cp = pltpu.make_async_remote_copy(
            src_ref=g_ref.at[:, pl.ds((1 - c) * rh, rh)], dst_ref=out_ref, send_sem=send_sem, recv_sem=recv_sem,
            device_id=(x, y, 1 - c), device_id_type=MESH_ID)
        cp.start()
        cp.wait()

    return pl.pallas_call(
        body, name=name, in_specs=[ANY], out_specs=ANY,
        out_shape=jax.ShapeDtypeStruct((n, rh, cols), g.dtype),
        scratch_shapes=[pltpu.SemaphoreType.DMA, pltpu.SemaphoreType.DMA],
    )(g)


def _scatter_chips(v, *, name):
    def body(v_ref, out_ref, send_sems, recv_sems):
        x, y, c = _me()
        j = 2 * x + y
        copies = [pltpu.make_async_remote_copy(
            src_ref=v_ref.at[2 * px + py], dst_ref=out_ref.at[j], send_sem=send_sems.at[k], recv_sem=recv_sems.at[k],
            device_id=(px, py, c), device_id_type=MESH_ID) for k, (px, py) in enumerate(_other_chips(x, y))]
        for cp in copies:
            cp.start()
        for cp in copies:
            cp.wait()

    return pl.pallas_call(
        body, name=name, in_specs=[ANY], out_specs=ANY,
        out_shape=jax.ShapeDtypeStruct(v.shape, v.dtype),
        scratch_shapes=[pltpu.SemaphoreType.DMA((N_CHIPS - 1,)), pltpu.SemaphoreType.DMA((N_CHIPS - 1,))],
    )(v)


def _join_halves(v, *, name):
    def body(v_ref, out_ref, send_sem, recv_sem):
        x, y, c = _me()
        cp = pltpu.make_async_remote_copy(
            src_ref=v_ref, dst_ref=out_ref.at[c], send_sem=send_sem, recv_sem=recv_sem,
            device_id=(x, y, 1 - c), device_id_type=MESH_ID)
        cp.start()
        cp.wait()

    return pl.pallas_call(
        body, name=name, in_specs=[ANY], out_specs=ANY,
        out_shape=jax.ShapeDtypeStruct((2,) + v.shape, v.dtype),
        scratch_shapes=[pltpu.SemaphoreType.DMA, pltpu.SemaphoreType.DMA],
    )(v)


def _row_blk(rows, want):
    for b in range(min(rows, want) // SUBLANES * SUBLANES, 0, -SUBLANES):
        if rows % b == 0:
            return b
    return rows


def _add_own_half(g, got, *, name):
    n, r, cols = g.shape
    rh = r // 2
    tr = _row_blk(rh, 512)
    nb = rh // tr

    def body(c_ref, g_ref, got_ref, o_ref):
        del c_ref
        o_ref[...] = (g_ref[...] + got_ref[...]).astype(BF16)

    grid_spec = pltpu.PrefetchScalarGridSpec(
        num_scalar_prefetch=1, grid=(n, nb),
        in_specs=[pl.BlockSpec((None, tr, cols), lambda j, i, c: (j, c[0] * nb + i, 0)),
                  pl.BlockSpec((None, tr, cols), lambda j, i, c: (j, i, 0))],
        out_specs=pl.BlockSpec((None, tr, cols), lambda j, i, c: (j, i, 0)))
    return pl.pallas_call(
        body, name=name, grid_spec=grid_spec, out_shape=jax.ShapeDtypeStruct((n, rh, cols), BF16),
        compiler_params=_params(("parallel", "parallel")),
    )(lax.axis_index("c").astype(jnp.int32).reshape(1), g, got)


def _sum_parts(parts, *, name):
    n, r, cols = parts.shape
    tr = _row_blk(r, 512)

    def body(*refs):
        o_ref = refs[n]
        acc = refs[0][...].astype(F32)
        for p_ref in refs[1:n]:
            acc = acc + p_ref[...].astype(F32)
        o_ref[...] = acc

    return pl.pallas_call(
        body, name=name, grid=(r // tr,),
        in_specs=[pl.BlockSpec((None, tr, cols), functools.partial(lambda j, i: (j, i, 0), j)) for j in range(n)],
        out_specs=pl.BlockSpec((tr, cols), lambda i: (i, 0)),
        out_shape=jax.ShapeDtypeStruct((r, cols), F32),
        compiler_params=_params(("parallel",)),
    )(*([parts] * n))


def _adamw(w, m, v, g, *, name):
    shape = w.shape
    cols = shape[-1]
    w2, m2, v2, g2 = (z.reshape(-1, cols) for z in (w, m, v, g))
    rows = w2.shape[0]
    tr = _row_blk(rows, max(SUBLANES, (1 << 20) // (4 * cols)))

    def body(w_ref, m_ref, v_ref, g_ref, d_ref, nm_ref, nv_ref):
        gv = g_ref[...]
        nm = ADAM_B1 * m_ref[...] + (1.0 - ADAM_B1) * gv
        nv = ADAM_B2 * v_ref[...] + (1.0 - ADAM_B2) * jnp.square(gv)
        m_hat = nm / (1.0 - ADAM_B1 ** ADAM_STEP)
        v_hat = nv / (1.0 - ADAM_B2 ** ADAM_STEP)
        d_ref[...] = -ADAM_LR * (m_hat / (jnp.sqrt(v_hat) + ADAM_EPS) + ADAM_WD * w_ref[...])
        nm_ref[...] = nm
        nv_ref[...] = nv

    spec = pl.BlockSpec((tr, cols), lambda i: (i, 0))
    out = jax.ShapeDtypeStruct((rows, cols), F32)
    d, nm, nv = pl.pallas_call(
        body, name=name, grid=(rows // tr,), in_specs=[spec] * 4, out_specs=[spec] * 3, out_shape=[out] * 3,
        compiler_params=_params(("parallel",)),
    )(w2, m2, v2, g2)
    return d.reshape(shape), nm.reshape(shape), nv.reshape(shape)


def _pack_rows(size, cols, align):
    return -(-size // (cols * align)) * align


def _pack(arrs, lead, cols, align, total_align):
    lead_shape = arrs[0].shape[:lead]
    parts = []
    for a in arrs:
        flat = a.reshape(lead_shape + (-1,))
        size = flat.shape[-1]
        rows = _pack_rows(size, cols, align)
        flat = jnp.pad(flat, [(0, 0)] * lead + [(0, rows * cols - size)])
        parts.append(flat.reshape(lead_shape + (rows, cols)))
    total = sum(p.shape[lead] for p in parts)
    extra = -total % total_align
    if extra:
        parts.append(jnp.zeros(lead_shape + (extra, cols), parts[0].dtype))
    return jnp.concatenate(parts, axis=lead)


def _unpack(buf, shapes, lead, cols, align):
    lead_shape = buf.shape[:lead]
    out, row = [], 0
    for shp in shapes:
        size = 1
        for n in shp:
            size *= n
        rows = _pack_rows(size, cols, align)
        piece = lax.slice_in_dim(buf, row, row + rows, axis=lead).reshape(lead_shape + (-1,))
        out.append(piece[..., :size].reshape(lead_shape + tuple(shp)))
        row += rows
    return out


BIG = ("ffn_w_in", "ffn_w_out", "a_w_qkv", "a_w_o", "kv_w", "b_w_q", "b_w_o")
SMALL = ("ffn_norm", "mix_norm", "a_q_norm", "a_k_norm", "kv_norm", "kv_b_f", "kv_k_norm", "b_q_norm")
WEIGHTS = ("ffn_norm", "ffn_w_in", "ffn_w_out", "mix_norm", "a_w_qkv", "a_q_norm", "a_k_norm", "a_w_o",
           "kv_norm", "kv_w", "kv_b_f", "kv_k_norm", "b_w_q", "b_q_norm", "b_w_o")
GATE_PAD = 2 * LANES


def _stack_weights(sh, d):
    depth = sh["ffn_w_in"].shape[1]
    kv = sh["kv_w"].transpose(1, 0, 2).reshape(d, -1)
    kv = jnp.pad(kv, ((0, 0), (0, 2 * d + GATE_PAD - kv.shape[1])))
    return {
        "ffn_w_in": [[sh["ffn_w_in"][:, l, i] for i in range(2)] for l in range(depth)],
        "ffn_w_out": [[sh["ffn_w_out"][:, l, i].reshape(1, -1, d) for i in range(2)] for l in range(depth)],
        "a_w_qkv": sh["a_w_qkv"][:, 0],
        "a_w_o": sh["a_w_o"].reshape(1, d, d),
        "kv_w": kv[None],
        "b_w_q": sh["b_w_q"].reshape(1, d, d),
        "b_w_o": sh["b_w_o"].reshape(1, d, d),
    }


def _unstack_grads(dw, d, heads):
    def rows4(z):
        return z.reshape(N_CHIPS, -1, d)

    kv_cols = 2 * d + heads
    kv = dw["kv_w"][0][:, :kv_cols].reshape(d, N_CHIPS, kv_cols // N_CHIPS).transpose(1, 0, 2)
    return [
        jnp.stack([jnp.stack(row, axis=1) for row in dw["ffn_w_in"]], axis=1),
        jnp.stack([jnp.stack([rows4(z) for z in row], axis=1) for row in dw["ffn_w_out"]], axis=1),
        dw["a_w_qkv"][:, None],
        rows4(dw["a_w_o"])[:, None],
        kv,
        rows4(dw["b_w_q"])[:, None],
        rows4(dw["b_w_o"])[:, None],
    ]


def kernel(x, positions, ffn_norm, ffn_w_in, ffn_w_out, mix_norm, a_w_qkv, a_q_norm, a_k_norm, a_w_o, kv_norm, kv_w, kv_b_f, kv_k_norm, b_w_q, b_q_norm, b_w_o, loss_target, m_ffn_norm, m_ffn_w_in, m_ffn_w_out, m_mix_norm, m_a_w_qkv, m_a_q_norm, m_a_k_norm, m_a_w_o, m_kv_norm, m_kv_w, m_kv_b_f, m_kv_k_norm, m_b_w_q, m_b_q_norm, m_b_w_o, v_ffn_norm, v_ffn_w_in, v_ffn_w_out, v_mix_norm, v_a_w_qkv, v_a_q_norm, v_a_k_norm, v_a_w_o, v_kv_norm, v_kv_w, v_kv_b_f, v_kv_k_norm, v_b_w_q, v_b_q_norm, v_b_w_o):
    wts = dict(zip(WEIGHTS, (ffn_norm, ffn_w_in, ffn_w_out, mix_norm, a_w_qkv, a_q_norm, a_k_norm, a_w_o, kv_norm,
                             kv_w, kv_b_f, kv_k_norm, b_w_q, b_q_norm, b_w_o)))
    mom = dict(zip(WEIGHTS, (m_ffn_norm, m_ffn_w_in, m_ffn_w_out, m_mix_norm, m_a_w_qkv, m_a_q_norm, m_a_k_norm,
                             m_a_w_o, m_kv_norm, m_kv_w, m_kv_b_f, m_kv_k_norm, m_b_w_q, m_b_q_norm, m_b_w_o)))
    var = dict(zip(WEIGHTS, (v_ffn_norm, v_ffn_w_in, v_ffn_w_out, v_mix_norm, v_a_w_qkv, v_a_q_norm, v_a_k_norm,
                             v_a_w_o, v_kv_norm, v_kv_w, v_kv_b_f, v_kv_k_norm, v_b_w_q, v_b_q_norm, v_b_w_o)))
    batch, seq, d = x.shape
    cfg = Cfg(d_model=d, d_ff=ffn_w_out.shape[2] * N_CHIPS, seq=seq, batch=batch)
    chip = 2 * lax.axis_index("x") + lax.axis_index("y")
    big_shapes = [wts[n].shape for n in BIG]

    shard = _pack([wts[n].astype(BF16) for n in BIG], 0, PACK_COLS, PACK_ROW_ALIGN, PACK_COLS)
    gathered = _all_gather_chips(shard, name="gather_weights")
    gathered = lax.dynamic_update_slice_in_dim(gathered, shard[None], chip, axis=0)
    w = _stack_weights(dict(zip(BIG, _unpack(gathered, big_shapes, 1, PACK_COLS, PACK_ROW_ALIGN))), d)
    norm_shard = _pack([ffn_norm], 0, LANES, SUBLANES, SUBLANES)
    norms = _all_gather_small(norm_shard, name="gather_ffn_norm")[0::2]
    (norms,) = _unpack(norms, [ffn_norm.shape], 1, LANES, SUBLANES)
    small = {"ffn_norm": jnp.moveaxis(norms, 0, 2).reshape(ffn_norm.shape[:2] + (d,)),
             "mix_norm": mix_norm, "a_q_norm": a_q_norm[0], "a_k_norm": a_k_norm[0], "kv_norm": kv_norm,
             "kv_b_f": kv_b_f, "kv_k_norm": kv_k_norm, "b_q_norm": b_q_norm}

    loss, dx, dw, ds = _local_step(cfg, x.reshape(cfg.tokens, d), positions.reshape(cfg.tokens),
                                   loss_target.reshape(cfg.tokens, d), w, small)
    loss = lax.psum(loss, ("x", "y", "c"))

    g = _pack(_unstack_grads(dw, d, cfg.heads), 1, PACK_COLS, PACK_ROW_ALIGN, PACK_COLS)
    chip_half = _add_own_half(g, _swap_halves(g, name="swap_halves"), name="add_halves")
    parts = _scatter_chips(chip_half, name="scatter_chips")
    parts = lax.dynamic_update_slice_in_dim(parts, lax.dynamic_slice_in_dim(chip_half, chip, 1, axis=0), chip, axis=0)
    mine = _sum_parts(parts, name="sum_chips")
    both = _join_halves(mine, name="join_halves")
    g_big = lax.dynamic_update_slice_in_dim(both, mine[None], lax.axis_index("c"), axis=0).reshape(g.shape[1:])
    grads = dict(zip(BIG, _unpack(g_big, big_shapes, 0, PACK_COLS, PACK_ROW_ALIGN)))

    small_shapes = [ds[n].shape for n in SMALL]
    parts = _all_gather_small(_pack([ds[n] for n in SMALL], 0, LANES, SUBLANES, SUBLANES), name="gather_small")
    g_small = dict(zip(SMALL, _unpack(_sum_parts(parts, name="sum_small"), small_shapes, 0, LANES, SUBLANES)))
    quarter = d // N_CHIPS
    g_small["ffn_norm"] = lax.dynamic_slice_in_dim(g_small["ffn_norm"], chip * quarter, quarter, axis=2)
    grads.update(g_small)

    delta, new_m, new_v = {}, {}, {}
    for n in BIG:
        delta[n], new_m[n], new_v[n] = _adamw(wts[n], mom[n], var[n], grads[n], name=f"adamw_{n}")
    packed = [_pack([z[n] for n in SMALL], 0, LANES, SUBLANES, SUBLANES) for z in (wts, mom, var, grads)]
    small_out = _adamw(*packed, name="adamw_small")
    shard_shapes = [wts[n].shape for n in SMALL]
    for out, res in zip((delta, new_m, new_v), small_out):
        out.update(zip(SMALL, _unpack(res, shard_shapes, 0, LANES, SUBLANES)))

    return (loss, dx.reshape(x.shape), *[grads[n] for n in WEIGHTS], *[delta[n] for n in WEIGHTS],
            *[new_m[n] for n in WEIGHTS], *[new_v[n] for n in WEIGHTS])
```

```python
import functools
from typing import NamedTuple

import jax
import jax.numpy as jnp
from jax import lax
from jax.experimental import pallas as pl
from jax.experimental.pallas import tpu as pltpu

F32 = jnp.float32
BF16 = jnp.bfloat16

HEAD_DIM = 64
LANES = 128
SUBLANES = 8
ROT_DIM = HEAD_DIM // 4
ROPE_THETA = 500000.0
NORM_EPS = 1e-6
BAND = 128
DILATIONS = (1, 4, 16)
NEG = -1e30
Q_SCALE = HEAD_DIM ** -0.5
N_CHIPS = 4
N_DEV = 8
VMEM_LIMIT = 48 * 1024 * 1024
WIDE = 1536
ROW_BLOCK = 512
FUSED_ROWS = 512
_NT = (((1,), (1,)), ((), ()))

ADAM_LR = 0.001
ADAM_B1 = 0.9
ADAM_B2 = 0.999
ADAM_EPS = 1e-08
ADAM_WD = 0.01
ADAM_STEP = 10


class Cfg(NamedTuple):
    d_model: int
    d_ff: int
    seq: int
    batch: int

    @property
    def heads(self):
        return self.d_model // HEAD_DIM

    @property
    def tokens(self):
        return self.batch * self.seq


def _params(sem):
    return pltpu.CompilerParams(dimension_semantics=sem, vmem_limit_bytes=VMEM_LIMIT)


def _blk(dim, want):
    if dim <= want:
        return dim
    for b in range(want // LANES * LANES, 0, -LANES):
        if dim % b == 0:
            return b
    b = want
    while dim % b:
        b //= 2
    return b


def _fold8(x):
    return jnp.sum(x.reshape(x.shape[0] // SUBLANES, SUBLANES, x.shape[1]), axis=0)


def _rms_bwd_tile(xv, g, dyv, dres):
    rstd = lax.rsqrt(jnp.mean(xv * xv, axis=-1, keepdims=True) + NORM_EPS)
    xhat = xv * rstd
    dyg = dyv * g
    proj = jnp.mean(dyg * xhat, axis=-1, keepdims=True)
    return dres + rstd * (dyg - xhat * proj), _fold8(dyv * xhat)


def _mm(a, b, *, form, out_dtype, name, bm=1024, bn=1024, bk=1024, res=None, scale=1.0, norms=(), norm_bwd=None):
    if form == "F":
        m, kdim = a.shape
        jn, _, ns = b.shape
        bm, bn, bk = _blk(m, bm), _blk(ns, bn), _blk(kdim, bk)
        npj = ns // bn
        grid = (m // bm, jn * npj, kdim // bk)
        a_spec = pl.BlockSpec((bm, bk), lambda i, n, k: (i, k))
        b_spec = pl.BlockSpec((None, bk, bn), lambda i, n, k: (n // npj, k, n % npj))
        o_spec = pl.BlockSpec((bm, bn), lambda i, n, k: (i, n))
        o_shape = jax.ShapeDtypeStruct((m, jn * ns), out_dtype)
        dims = (((1,), (0,)), ((), ()))
    elif form == "B":
        m = a.shape[0]
        jn, kdim, ns = b.shape
        bm, bn, bk = _blk(m, bm), _blk(kdim, bn), _blk(ns, bk)
        kpj = ns // bk
        grid = (m // bm, kdim // bn, jn * kpj)
        a_spec = pl.BlockSpec((bm, bk), lambda i, n, k: (i, k))
        b_spec = pl.BlockSpec((None, bn, bk), lambda i, n, k: (k // kpj, n, k % kpj))
        o_spec = pl.BlockSpec((bm, bn), lambda i, n, k: (i, n))
        o_shape = jax.ShapeDtypeStruct((m, kdim), out_dtype)
        dims = _NT
    else:
        raise ValueError(form)
    nk = grid[2]
    n_norms = len(norms)
    full_rows = grid[1] == 1
    assert full_rows or (not norms and norm_bwd is None)

    def body(*refs):
        a_ref, b_ref = refs[:2]
        pos = 2
        r_ref = None
        if res is not None:
            r_ref = refs[pos]
            pos += 1
        g_refs = refs[pos:pos + n_norms]
        pos += n_norms
        if norm_bwd is not None:
            x_ref, gb_ref, dres_ref = refs[pos:pos + 3]
            pos += 3
        o_ref = refs[pos]
        n_refs = refs[pos + 1:pos + 1 + n_norms]
        acc_ref = refs[-1]
        i, k = pl.program_id(0), pl.program_id(2)

        @pl.when(k == 0)
        def _():
            acc_ref[...] = jnp.zeros_like(acc_ref)

        acc_ref[...] += lax.dot_general(a_ref[...].astype(BF16), b_ref[...].astype(BF16), dims,
                                        preferred_element_type=F32)

        if norm_bwd is not None:
            dg_ref = refs[pos + 1 + n_norms]

            @pl.when((i == 0) & (k == 0))
            def _():
                dg_ref[...] = jnp.zeros_like(dg_ref)

        @pl.when(k == nk - 1)
        def _():
            r = acc_ref[...]
            if scale != 1.0:
                r = r * scale
            if r_ref is not None:
                r = r_ref[...] + r
            if norm_bwd is not None:
                dx, dg8 = _rms_bwd_tile(x_ref[...], gb_ref[...], r, dres_ref[...])
                o_ref[...] = dx
                dg_ref[...] += dg8
            else:
                o_ref[...] = r.astype(o_ref.dtype)
            if n_norms:
                rstd = lax.rsqrt(jnp.mean(r * r, axis=-1, keepdims=True) + NORM_EPS)
                for g_ref, n_ref in zip(g_refs, n_refs):
                    n_ref[...] = ((r * rstd) * g_ref[...]).astype(BF16)

    row = pl.BlockSpec((bm, bn), lambda i, n, k: (i, n))
    vec = pl.BlockSpec((1, bn), lambda i, n, k: (0, 0))
    in_specs = [a_spec, b_spec]
    args = [a, b]
    if res is not None:
        in_specs.append(row)
        args.append(res)
    for g in norms:
        in_specs.append(vec)
        args.append(g.reshape(1, -1))
    out_specs, out_shapes = [o_spec], [o_shape]
    for _ in norms:
        out_specs.append(row)
        out_shapes.append(jax.ShapeDtypeStruct(o_shape.shape, BF16))
    if norm_bwd is not None:
        x, g, dres = norm_bwd
        in_specs += [row, vec, row]
        args += [x, g.reshape(1, -1), dres]
        out_specs.append(pl.BlockSpec((SUBLANES, bn), lambda i, n, k: (0, 0)))
        out_shapes.append(jax.ShapeDtypeStruct((SUBLANES, o_shape.shape[1]), F32))
    sem = ("arbitrary",) * 3 if norm_bwd is not None else ("parallel", "parallel", "arbitrary")
    single = len(out_specs) == 1
    out = pl.pallas_call(
        body, name=name, grid=grid, in_specs=in_specs, out_specs=out_specs[0] if single else out_specs,
        out_shape=out_shapes[0] if single else out_shapes,
        scratch_shapes=[pltpu.VMEM((bm, bn), F32)],
        compiler_params=_params(sem),
    )(*args)
    if norm_bwd is not None:
        return out[0], jnp.sum(out[1], axis=0)
    return out


def _mm_grad(a, dy, jn, *, name, scale=1.0, bm=1024, bn=1024, bk=1024):
    halves = dy if isinstance(dy, (tuple, list)) else (dy,)
    t, kdim = a.shape
    ns = len(halves) * halves[0].shape[1] // jn
    bm, bn, bk = _blk(kdim, bm), _blk(ns, bn), _blk(t, bk)
    npj = ns // bn
    grid = (kdim // bm, jn * npj, t // bk)
    nk = grid[2]
    nhalf = jn * npj // len(halves)
    dims = (((0,), (0,)), ((), ()))

    def body(a_ref, *refs):
        b_refs, o_ref, acc_ref = refs[:len(halves)], refs[-2], refs[-1]
        n, k = pl.program_id(1), pl.program_id(2)

        @pl.when(k == 0)
        def _():
            acc_ref[...] = jnp.zeros_like(acc_ref)

        for which, b_ref in enumerate(b_refs):
            @pl.when(n // nhalf == which)
            def _(b_ref=b_ref):
                acc_ref[...] += lax.dot_general(a_ref[...].astype(BF16), b_ref[...].astype(BF16), dims,
                                                preferred_element_type=F32)

        @pl.when(k == nk - 1)
        def _():
            r = acc_ref[...]
            if scale != 1.0:
                r = r * scale
            o_ref[...] = r

    def half_spec(which):
        return pl.BlockSpec((bk, bn), lambda m, n, k: (jnp.where(n // nhalf == which, k, 0),
                                                        jnp.where(n // nhalf == which, n % nhalf, 0)))

    return pl.pallas_call(
        body, name=name, grid=grid,
        in_specs=[pl.BlockSpec((bk, bm), lambda m, n, k: (k, m))] + [half_spec(w) for w in range(len(halves))],
        out_specs=pl.BlockSpec((None, bm, bn), lambda m, n, k: (n // npj, m, n % npj)),
        out_shape=jax.ShapeDtypeStruct((jn, kdim, ns), F32),
        scratch_shapes=[pltpu.VMEM((bm, bn), F32)],
        compiler_params=_params(("parallel", "parallel", "arbitrary")),
    )(a, *halves)


def _mm_back2(dy_halves, w, norm_bwd, *, name, bm=FUSED_ROWS, bk=1024):
    x, g, dres = norm_bwd
    m = dy_halves[0].shape[0]
    jn, kdim, ns = w.shape
    bm, bk = _blk(m, bm), _blk(ns, bk)
    kpj = ns // bk
    nk = jn * kpj
    khalf = nk // 2

    def body(a0_ref, a1_ref, b_ref, x_ref, g_ref, dres_ref, o_ref, dg_ref, acc_ref):
        i, k = pl.program_id(0), pl.program_id(1)

        @pl.when(k == 0)
        def _():
            acc_ref[...] = jnp.zeros_like(acc_ref)

        @pl.when((i == 0) & (k == 0))
        def _():
            dg_ref[...] = jnp.zeros_like(dg_ref)

        for which, a_ref in enumerate((a0_ref, a1_ref)):
            @pl.when(k // khalf == which)
            def _(a_ref=a_ref):
                acc_ref[...] += lax.dot_general(a_ref[...], b_ref[...], _NT, preferred_element_type=F32)

        @pl.when(k == nk - 1)
        def _():
            dx, dg8 = _rms_bwd_tile(x_ref[...], g_ref[...], acc_ref[...], dres_ref[...])
            o_ref[...] = dx
            dg_ref[...] += dg8

    def half_spec(which):
        return pl.BlockSpec((bm, bk), lambda i, k: (i, jnp.clip(k - which * khalf, 0, khalf - 1)))

    row = pl.BlockSpec((bm, kdim), lambda i, k: (i, 0))
    dx, dg = pl.pallas_call(
        body, name=name, grid=(m // bm, nk),
        in_specs=[half_spec(0), half_spec(1),
                  pl.BlockSpec((None, kdim, bk), lambda i, k: (k // kpj, 0, k % kpj)),
                  row, pl.BlockSpec((1, kdim), lambda i, k: (0, 0)), row],
        out_specs=[row, pl.BlockSpec((SUBLANES, kdim), lambda i, k: (0, 0))],
        out_shape=[jax.ShapeDtypeStruct((m, kdim), F32), jax.ShapeDtypeStruct((SUBLANES, kdim), F32)],
        scratch_shapes=[pltpu.VMEM((bm, kdim), F32)],
        compiler_params=_params(("arbitrary", "arbitrary")),
    )(dy_halves[0], dy_halves[1], w, x, g.reshape(1, -1), dres)
    return dx, jnp.sum(dg, axis=0)


def _ffn_in_act(n, w_in, *, name, bm=512):
    m, kdim = n.shape
    jn, _, ns = w_in.shape
    f = jn * ns // 2
    bm = _blk(m, bm)
    bn = _blk(ns, WIDE)
    npj = ns // bn
    nf = f // bn

    def body(n_ref, wg_ref, wu_ref, g_ref, u_ref, a_ref):
        nv = n_ref[...]
        g = jnp.dot(nv, wg_ref[...], preferred_element_type=F32)
        u = jnp.dot(nv, wu_ref[...], preferred_element_type=F32)
        g_ref[...] = g.astype(BF16)
        u_ref[...] = u.astype(BF16)
        a_ref[...] = (g * jax.nn.sigmoid(g) * u).astype(BF16)

    out = jax.ShapeDtypeStruct((m, f), BF16)
    ospec = pl.BlockSpec((bm, bn), lambda c, i: (i, c))
    return pl.pallas_call(
        body, name=name, grid=(nf, m // bm),
        in_specs=[pl.BlockSpec((bm, kdim), lambda c, i: (i, 0)),
                  pl.BlockSpec((None, kdim, bn), lambda c, i: (c // npj, 0, c % npj)),
                  pl.BlockSpec((None, kdim, bn), lambda c, i: ((c + nf) // npj, 0, (c + nf) % npj))],
        out_specs=[ospec, ospec, ospec], out_shape=[out, out, out],
        compiler_params=_params(("parallel", "parallel")),
    )(n, w_in, w_in)


def _ffn_out_dx_act(dh, w_out, gate, up, *, name, scale, bm=512):
    m, d = dh.shape
    f = w_out.shape[1]
    bm = _blk(m, bm)
    bn = _blk(f, WIDE)

    def body(dh_ref, w_ref, g_ref, u_ref, dg_ref, du_ref):
        da = lax.dot_general(dh_ref[...].astype(BF16), w_ref[...], _NT, preferred_element_type=F32) * scale
        g = g_ref[...].astype(F32)
        sg = jax.nn.sigmoid(g)
        silu = g * sg
        dg_ref[...] = (da * u_ref[...].astype(F32) * (sg + silu * (1.0 - sg))).astype(BF16)
        du_ref[...] = (da * silu).astype(BF16)

    out = jax.ShapeDtypeStruct((m, f), BF16)
    spec = pl.BlockSpec((bm, bn), lambda i, c: (i, c))
    return pl.pallas_call(
        body, name=name, grid=(m // bm, f // bn),
        in_specs=[pl.BlockSpec((bm, d), lambda i, c: (i, 0)), pl.BlockSpec((None, bn, d), lambda i, c: (0, c, 0)),
                  spec, spec],
        out_specs=[spec, spec], out_shape=[out, out],
        compiler_params=_params(("parallel", "parallel")),
    )(dh, w_out, gate, up)


def _rms_fwd(x, g, *, name):
    t, d = x.shape
    tr = _blk(t, ROW_BLOCK)

    def body(x_ref, g_ref, o_ref):
        xv = x_ref[...]
        rstd = lax.rsqrt(jnp.mean(xv * xv, axis=-1, keepdims=True) + NORM_EPS)
        o_ref[...] = ((xv * rstd) * g_ref[...]).astype(BF16)

    return pl.pallas_call(
        body, name=name, grid=(t // tr,),
        in_specs=[pl.BlockSpec((tr, d), lambda i: (i, 0)), pl.BlockSpec((1, d), lambda i: (0, 0))],
        out_specs=pl.BlockSpec((tr, d), lambda i: (i, 0)),
        out_shape=jax.ShapeDtypeStruct((t, d), BF16),
        compiler_params=_params(("parallel",)),
    )(x, g.reshape(1, d))


def _loss_fwd_bwd(h, target, *, name):
    t, d = h.shape
    tr = _blk(t, ROW_BLOCK)

    def body(h_ref, t_ref, dh_ref, l_ref):
        i = pl.program_id(0)
        err = h_ref[...] - t_ref[...]
        dh_ref[...] = err * (1.0 / d)

        @pl.when(i == 0)
        def _():
            l_ref[...] = jnp.zeros_like(l_ref)

        l_ref[...] += _fold8(err * err)

    dh, part = pl.pallas_call(
        body, name=name, grid=(t // tr,),
        in_specs=[pl.BlockSpec((tr, d), lambda i: (i, 0)), pl.BlockSpec((tr, d), lambda i: (i, 0))],
        out_specs=[pl.BlockSpec((tr, d), lambda i: (i, 0)), pl.BlockSpec((SUBLANES, d), lambda i: (0, 0))],
        out_shape=[jax.ShapeDtypeStruct((t, d), F32), jax.ShapeDtypeStruct((SUBLANES, d), F32)],
        compiler_params=_params(("arbitrary",)),
    )(h, target)
    return jnp.sum(part) * (0.5 / d), dh


def _seg_matrix():
    r = lax.broadcasted_iota(jnp.int32, (LANES, LANES), 0) // HEAD_DIM
    c = lax.broadcasted_iota(jnp.int32, (LANES, LANES), 1) // HEAD_DIM
    return (r == c).astype(BF16)


def _head_sum(x, seg, terms=3):
    hi = x.astype(BF16)
    r1 = x - hi.astype(F32)
    mid = r1.astype(BF16)
    dot = functools.partial(jnp.dot, preferred_element_type=F32)
    if terms == 2:
        return dot(hi, seg) + dot(mid, seg)
    lo = (r1 - mid.astype(F32)).astype(BF16)
    return dot(hi, seg) + dot(mid, seg) + dot(lo, seg)


def _lane_in_head(shape):
    return lax.broadcasted_iota(jnp.int32, shape, 1) % HEAD_DIM


def _half_mask(shape):
    return lax.broadcasted_iota(jnp.int32, shape, 1) < HEAD_DIM


def _rot_partner(x):
    up = pltpu.roll(x, LANES - ROT_DIM // 2, 1)
    down = pltpu.roll(x, ROT_DIM // 2, 1)
    return jnp.where(_lane_in_head(x.shape) < ROT_DIM // 2, up, down)


def _rope_tables(positions):
    inv_freq = ROPE_THETA ** (-jnp.arange(0, ROT_DIM, 2, dtype=F32) / ROT_DIM)
    ang = positions.astype(F32)[:, None] * inv_freq
    t = ang.shape[0]
    rest = HEAD_DIM - ROT_DIM
    cos = jnp.concatenate([jnp.cos(ang), jnp.cos(ang), jnp.ones((t, rest), F32)], axis=1)
    sin = jnp.concatenate([-jnp.sin(ang), jnp.sin(ang), jnp.zeros((t, rest), F32)], axis=1)
    return jnp.tile(cos, (1, LANES // HEAD_DIM)), jnp.tile(sin, (1, LANES // HEAD_DIM))


def _kind_is(j, kinds, kind):
    hits = [j == jj for jj, k in enumerate(kinds) if k == kind]
    return functools.reduce(jnp.logical_or, hits) if hits else None


def _hn_fwd(x, gains, kinds, d, cos, sin, *, name, col0=0):
    t = x.shape[0]
    n = len(kinds)
    tr = _blk(t, ROW_BLOCK)
    seg = _seg_matrix()
    g8 = jnp.repeat(gains.astype(F32), SUBLANES, axis=0)

    def body(x_ref, g_ref, seg_ref, cos_ref, sin_ref, o_ref):
        j = pl.program_id(1)

        def normed(rope):
            for c in range(d // LANES):
                sl = slice(c * LANES, (c + 1) * LANES)
                xv = x_ref[:, sl]
                ms = _head_sum(xv * xv, seg_ref[...], terms=2) * (1.0 / HEAD_DIM)
                y = (xv * lax.rsqrt(ms + NORM_EPS)) * g_ref[0:1, sl]
                if rope:
                    y = y * cos_ref[...] + _rot_partner(y) * sin_ref[...]
                o_ref[:, sl] = y.astype(BF16)

        for kind in ("rope", "norm"):
            hit = _kind_is(j, kinds, kind)
            if hit is not None:
                pl.when(hit)(functools.partial(normed, kind == "rope"))
        hit = _kind_is(j, kinds, "cast")
        if hit is not None:
            @pl.when(hit)
            def _():
                o_ref[...] = x_ref[...].astype(BF16)

    return pl.pallas_call(
        body, name=name, grid=(t // tr, n),
        in_specs=[pl.BlockSpec((tr, d), lambda i, j: (i, col0 + j)), pl.BlockSpec((SUBLANES, d), lambda i, j: (j, 0)),
                  pl.BlockSpec((LANES, LANES), lambda i, j: (0, 0)),
                  pl.BlockSpec((tr, LANES), lambda i, j: (i, 0)), pl.BlockSpec((tr, LANES), lambda i, j: (i, 0))],
        out_specs=pl.BlockSpec((tr, d), lambda i, j: (i, j)),
        out_shape=jax.ShapeDtypeStruct((t, n * d), BF16),
        compiler_params=_params(("parallel", "parallel")),
    )(x, g8, seg, cos, sin)


def _hn_bwd(x, dys, gains, kinds, d, cos, sin, *, name, col0=0):
    t = x.shape[0]
    n = len(kinds)
    tr = _blk(t, ROW_BLOCK // 2)
    seg = _seg_matrix()
    g8 = jnp.repeat(gains.astype(F32), SUBLANES, axis=0)

    def body(x_ref, *refs):
        dy_refs = refs[:n]
        g_ref, seg_ref, cos_ref, sin_ref, dx_ref, dg_ref = refs[n:]
        j = pl.program_id(0)
        i = pl.program_id(1)

        @pl.when(i == 0)
        def _():
            dg_ref[...] = jnp.zeros_like(dg_ref)

        def normed(rope, dy_ref):
            for c in range(d // LANES):
                sl = slice(c * LANES, (c + 1) * LANES)
                xv = x_ref[:, sl]
                dyv = dy_ref[:, sl]
                if rope:
                    dyv = dyv * cos_ref[...] - _rot_partner(dyv) * sin_ref[...]
                ms = _head_sum(xv * xv, seg_ref[...], terms=2) * (1.0 / HEAD_DIM)
                rstd = lax.rsqrt(ms + NORM_EPS)
                xhat = xv * rstd
                dg_ref[:, sl] += _fold8(dyv * xhat)
                dyg = dyv * g_ref[0:1, sl]
                proj = _head_sum(dyg * xhat, seg_ref[...], terms=2) * (1.0 / HEAD_DIM)
                dx_ref[:, sl] = (rstd * (dyg - xhat * proj)).astype(BF16)

        def cast(dy_ref):
            dx_ref[...] = dy_ref[...].astype(BF16)

        for jj, kind in enumerate(kinds):
            if kind == "cast":
                pl.when(j == jj)(functools.partial(cast, dy_refs[jj]))
            else:
                pl.when(j == jj)(functools.partial(normed, kind == "rope", dy_refs[jj]))

    def dy_spec(jj):
        return pl.BlockSpec((tr, d), lambda j, i: (jnp.where(j == jj, i, 0), 0))

    dx, dg = pl.pallas_call(
        body, name=name, grid=(n, t // tr),
        in_specs=[pl.BlockSpec((tr, d), lambda j, i: (i, col0 + j))] + [dy_spec(jj) for jj in range(n)] + [
                  pl.BlockSpec((SUBLANES, d), lambda j, i: (j, 0)),
                  pl.BlockSpec((LANES, LANES), lambda j, i: (0, 0)),
                  pl.BlockSpec((tr, LANES), lambda j, i: (i, 0)), pl.BlockSpec((tr, LANES), lambda j, i: (i, 0))],
        out_specs=[pl.BlockSpec((tr, d), lambda j, i: (i, j)), pl.BlockSpec((SUBLANES, d), lambda j, i: (j, 0))],
        out_shape=[jax.ShapeDtypeStruct((t, n * d), BF16), jax.ShapeDtypeStruct((n * SUBLANES, d), F32)],
        compiler_params=_params(("arbitrary", "arbitrary")),
    )(x, *dys, g8, seg, cos, sin)
    dg = dg.reshape(n, SUBLANES, d // HEAD_DIM, HEAD_DIM).sum(axis=(1, 2))
    return dx, dg


def _head_dot(a, b, *, name):
    t, d = a.shape
    tr = _blk(t, ROW_BLOCK)
    seg = _seg_matrix()

    def body(a_ref, b_ref, seg_ref, o_ref):
        for c in range(d // LANES):
            sl = slice(c * LANES, (c + 1) * LANES)
            o_ref[:, sl] = _head_sum(a_ref[:, sl].astype(BF16).astype(F32) * b_ref[:, sl], seg_ref[...])

    return pl.pallas_call(
        body, name=name, grid=(t // tr,),
        in_specs=[pl.BlockSpec((tr, d), lambda i: (i, 0)), pl.BlockSpec((tr, d), lambda i: (i, 0)),
                  pl.BlockSpec((LANES, LANES), lambda i: (0, 0))],
        out_specs=pl.BlockSpec((tr, d), lambda i: (i, 0)),
        out_shape=jax.ShapeDtypeStruct((t, d), F32),
        compiler_params=_params(("parallel",)),
    )(a, b, seg)


def _band_valid_t(first):
    s = lax.broadcasted_iota(jnp.int32, (2 * BAND, BAND), 0)
    t = lax.broadcasted_iota(jnp.int32, (2 * BAND, BAND), 1)
    dist = t + BAND - s
    return (dist >= 0) & (dist <= BAND) & ((s >= BAND) | jnp.logical_not(first))


def _to_classes_t(z, dil, width):
    return z.reshape(z.shape[0] // dil, dil, width).transpose(1, 2, 0)


def _from_classes_t(z):
    dil, width, rows = z.shape
    return z.transpose(2, 0, 1).reshape(rows * dil, width)


def _band_fwd_n(nat, dil, cfg, *, name):
    d, hh = cfg.d_model, cfg.heads
    rows = cfg.tokens // dil
    nbt = rows // BAND
    nb = cfg.seq // (dil * BAND)

    def body(q_ref, kp_ref, kc_ref, vp_ref, vc_ref, o_ref, lse_ref):
        i = pl.program_id(1)
        valid = _band_valid_t(i % nb == 0)
        upper = lax.broadcasted_iota(jnp.int32, (LANES, BAND), 0) < HEAD_DIM
        for hp in range(d // LANES):
            pair = slice(hp * LANES, (hp + 1) * LANES)
            qt2 = q_ref[:, pair].T
            kk = jnp.concatenate([kp_ref[:, pair], kc_ref[:, pair]], axis=0)
            vvt = jnp.concatenate([vp_ref[:, pair], vc_ref[:, pair]], axis=0).T
            outs = []
            for e in range(2):
                h = 2 * hp + e
                qte = jnp.where(upper == (e == 0), qt2, jnp.zeros_like(qt2))
                s = jnp.where(valid, jnp.dot(kk, qte, preferred_element_type=F32), NEG)
                m = jnp.max(s, axis=0, keepdims=True)
                p = jnp.exp(s - m)
                l = jnp.sum(p, axis=0, keepdims=True)
                hi = p.astype(BF16)
                lo = (p - hi.astype(F32)).astype(BF16)
                vt = vvt[e * HEAD_DIM:(e + 1) * HEAD_DIM]
                o = jnp.dot(vt, hi, preferred_element_type=F32) + jnp.dot(vt, lo, preferred_element_type=F32)
                outs.append(o * (1.0 / l))
                lse_ref[h:h + 1, :] = m + jnp.log(l)
            o_ref[:, pair] = jnp.concatenate(outs, axis=0).T

    def prev(i):
        return jnp.maximum(i - 1, 0)

    blk = (BAND, d)
    return pl.pallas_call(
        body, name=name, grid=(dil, nbt),
        in_specs=[pl.BlockSpec(blk, lambda r, i: (i, r * 3)),
                  pl.BlockSpec(blk, lambda r, i: (prev(i), r * 3 + 1)),
                  pl.BlockSpec(blk, lambda r, i: (i, r * 3 + 1)),
                  pl.BlockSpec(blk, lambda r, i: (prev(i), r * 3 + 2)),
                  pl.BlockSpec(blk, lambda r, i: (i, r * 3 + 2))],
        out_specs=[pl.BlockSpec(blk, lambda r, i: (i, r)),
                   pl.BlockSpec((None, hh, BAND), lambda r, i: (r, 0, i))],
        out_shape=[jax.ShapeDtypeStruct((rows, dil * d), F32), jax.ShapeDtypeStruct((dil, hh, rows), F32)],
        compiler_params=_params(("parallel", "arbitrary")),
    )(nat, nat, nat, nat, nat)


def _band_bwd_n(nat, do_nat, lse_c, dsum_c, dil, cfg, *, name):
    d, hh = cfg.d_model, cfg.heads
    rows = cfg.tokens // dil
    nbt = rows // BAND
    nb = cfg.seq // (dil * BAND)

    def body(q_ref, kp_ref, kc_ref, vp_ref, vc_ref, do_ref, l_ref, ds_ref, dq_ref, dk_ref, dv_ref, ck_ref, cv_ref):
        i = pl.program_id(1)

        @pl.when(i < nbt)
        def _():
            @pl.when(i == 0)
            def _():
                ck_ref[...] = jnp.zeros_like(ck_ref)
                cv_ref[...] = jnp.zeros_like(cv_ref)

            valid1 = _band_valid_t(i % nb == 0)
            valid = jnp.concatenate([valid1, valid1], axis=1)
            upper = lax.broadcasted_iota(jnp.int32, (LANES, BAND), 0) < HEAD_DIM
            half2 = _half_mask((2 * BAND, LANES))

            def both(z):
                zero = jnp.zeros_like(z)
                return jnp.concatenate([jnp.where(upper, z, zero), jnp.where(upper, zero, z)], axis=1)

            def stack(z):
                return jnp.concatenate([z[:, :BAND], z[:, BAND:]], axis=0)

            for hp in range(d // LANES):
                pair = slice(hp * LANES, (hp + 1) * LANES)
                h0, h1 = 2 * hp, 2 * hp + 1
                qn2, don2 = q_ref[:, pair], do_ref[:, pair]
                kk = jnp.concatenate([kp_ref[:, pair], kc_ref[:, pair]], axis=0)
                vv = jnp.concatenate([vp_ref[:, pair], vc_ref[:, pair]], axis=0)
                lse2 = jnp.concatenate([l_ref[h0:h0 + 1, :], l_ref[h1:h1 + 1, :]], axis=1)
                dsum2 = jnp.concatenate([ds_ref[h0:h0 + 1, :], ds_ref[h1:h1 + 1, :]], axis=1)
                s = jnp.where(valid, jnp.dot(kk, both(qn2.T), preferred_element_type=F32), NEG)
                p = jnp.exp(s - lse2)
                dp = jnp.dot(vv, both(don2.T), preferred_element_type=F32)
                dsb = (p * (dp - dsum2)).astype(BF16)
                dq2 = jnp.dot(kk.T, dsb, preferred_element_type=F32)
                dq_ref[:, pair] = jnp.concatenate([dq2[:HEAD_DIM, :BAND], dq2[HEAD_DIM:, BAND:]], axis=0).T
                dk2 = jnp.dot(stack(dsb), qn2, preferred_element_type=F32)
                dv2 = jnp.dot(stack(p.astype(BF16)), don2, preferred_element_type=F32)
                dkk = jnp.where(half2, dk2[:2 * BAND], dk2[2 * BAND:])
                dvv = jnp.where(half2, dv2[:2 * BAND], dv2[2 * BAND:])
                dk_ref[:, pair] = ck_ref[:, pair] + dkk[:BAND]
                dv_ref[:, pair] = cv_ref[:, pair] + dvv[:BAND]
                ck_ref[:, pair] = dkk[BAND:]
                cv_ref[:, pair] = dvv[BAND:]

        @pl.when(i == nbt)
        def _():
            dk_ref[...] = ck_ref[...]
            dv_ref[...] = cv_ref[...]

    def cur(i):
        return jnp.minimum(i, nbt - 1)

    def prev(i):
        return jnp.maximum(cur(i) - 1, 0)

    cblk = (None, hh, BAND)
    blk = (BAND, d)
    here = pl.BlockSpec(blk, lambda r, i: (cur(i), r))
    behind = pl.BlockSpec(blk, lambda r, i: (jnp.maximum(i - 1, 0), r))
    shape = jax.ShapeDtypeStruct((rows, dil * d), F32)
    return pl.pallas_call(
        body, name=name, grid=(dil, nbt + 1),
        in_specs=[pl.BlockSpec(blk, lambda r, i: (cur(i), r * 3)),
                  pl.BlockSpec(blk, lambda r, i: (prev(i), r * 3 + 1)),
                  pl.BlockSpec(blk, lambda r, i: (cur(i), r * 3 + 1)),
                  pl.BlockSpec(blk, lambda r, i: (prev(i), r * 3 + 2)),
                  pl.BlockSpec(blk, lambda r, i: (cur(i), r * 3 + 2)),
                  here,
                  pl.BlockSpec(cblk, lambda r, i: (r, 0, cur(i))),
                  pl.BlockSpec(cblk, lambda r, i: (r, 0, cur(i)))],
        out_specs=[here, behind, behind],
        out_shape=[shape, shape, shape],
        scratch_shapes=[pltpu.VMEM(blk, F32), pltpu.VMEM(blk, F32)],
        compiler_params=_params(("arbitrary", "arbitrary")),
    )(nat, nat, nat, nat, nat, do_nat, lse_c, dsum_c)


def _mix_fwd(outs, lses, *, name):
    t, d = outs[0].shape
    tr = _blk(t, ROW_BLOCK)
    ng = len(outs)

    def body(*refs):
        o_refs, l_refs = refs[:ng], refs[ng:2 * ng]
        mixed_ref, lse_ref = refs[2 * ng:]
        ls = [r[...] for r in l_refs]
        m = functools.reduce(jnp.maximum, ls)
        es = [jnp.exp(l - m) for l in ls]
        tot = functools.reduce(jnp.add, es)
        inv = 1.0 / tot
        mixed_ref[...] = functools.reduce(jnp.add, [(e * inv) * r[...] for e, r in zip(es, o_refs)])
        lse_ref[...] = m + jnp.log(tot)

    spec = pl.BlockSpec((tr, d), lambda i: (i, 0))
    return pl.pallas_call(
        body, name=name, grid=(t // tr,),
        in_specs=[spec] * (2 * ng), out_specs=[spec, spec],
        out_shape=[jax.ShapeDtypeStruct((t, d), F32), jax.ShapeDtypeStruct((t, d), F32)],
        compiler_params=_params(("parallel",)),
    )(*outs, *lses)


GATE_BLOCK = 256


def _tri(n, upper):
    r = lax.broadcasted_iota(jnp.int32, (n, n), 0)
    c = lax.broadcasted_iota(jnp.int32, (n, n), 1)
    return ((c >= r) if upper else (c <= r)).astype(BF16)


def _tri_dot(tri, x):
    hi = x.astype(BF16)
    r1 = x - hi.astype(F32)
    mid = r1.astype(BF16)
    lo = (r1 - mid.astype(F32)).astype(BF16)
    dot = functools.partial(jnp.dot, preferred_element_type=F32)
    return dot(tri, hi) + dot(tri, mid) + dot(tri, lo)


def _log_sigmoid(z):
    return jnp.minimum(z, 0.0) - jnp.log(1.0 + jnp.exp(-jnp.abs(z)))


def _gate_fwd(proj, col_block, bias, cfg, *, name):
    tr = _blk(cfg.seq, GATE_BLOCK)
    nblk = cfg.seq // tr

    def body(z_ref, b_ref, tri_ref, o_ref, carry_ref):
        i = pl.program_id(1)

        @pl.when(i == 0)
        def _():
            carry_ref[...] = jnp.zeros_like(carry_ref)

        logf = _log_sigmoid(z_ref[...] + b_ref[0:1, :])
        cum = _tri_dot(tri_ref[...], logf) + carry_ref[0:1, :]
        o_ref[...] = cum
        carry_ref[...] = jnp.broadcast_to(cum[tr - 1:tr, :], carry_ref.shape)

    return pl.pallas_call(
        body, name=name, grid=(cfg.batch, nblk),
        in_specs=[pl.BlockSpec((tr, LANES), lambda b, i: (b * nblk + i, col_block)),
                  pl.BlockSpec((SUBLANES, LANES), lambda b, i: (0, 0)),
                  pl.BlockSpec((tr, tr), lambda b, i: (0, 0))],
        out_specs=pl.BlockSpec((tr, LANES), lambda b, i: (b * nblk + i, 0)),
        out_shape=jax.ShapeDtypeStruct((cfg.tokens, LANES), F32),
        scratch_shapes=[pltpu.VMEM((SUBLANES, LANES), F32)],
        compiler_params=_params(("arbitrary", "arbitrary")),
    )(proj, jnp.broadcast_to(bias, (SUBLANES, LANES)), _tri(tr, upper=False))


def _gate_bwd(proj, col_block, bias, dcum, cfg, *, name):
    tr = _blk(cfg.seq, GATE_BLOCK)
    nblk = cfg.seq // tr

    def body(z_ref, b_ref, tri_ref, dc_ref, dz_ref, db_ref, carry_ref):
        b = pl.program_id(0)
        i = pl.program_id(1)

        @pl.when(i == 0)
        def _():
            carry_ref[...] = jnp.zeros_like(carry_ref)

        @pl.when((i == 0) & (b == 0))
        def _():
            db_ref[...] = jnp.zeros_like(db_ref)

        dcv = dc_ref[...]
        dlogf = _tri_dot(tri_ref[...], dcv) + carry_ref[0:1, :]
        carry_ref[...] = jnp.broadcast_to(dlogf[0:1, :], carry_ref.shape)
        dz = dlogf * jax.nn.sigmoid(-(z_ref[...] + b_ref[0:1, :]))
        dz_ref[...] = dz
        db_ref[...] += _fold8(dz)

    def rev(b, i):
        return (b * nblk + nblk - 1 - i, 0)

    dz, db = pl.pallas_call(
        body, name=name, grid=(cfg.batch, nblk),
        in_specs=[pl.BlockSpec((tr, LANES), lambda b, i: (b * nblk + nblk - 1 - i, col_block)),
                  pl.BlockSpec((SUBLANES, LANES), lambda b, i: (0, 0)),
                  pl.BlockSpec((tr, tr), lambda b, i: (0, 0)),
                  pl.BlockSpec((tr, LANES), rev)],
        out_specs=[pl.BlockSpec((tr, LANES), rev), pl.BlockSpec((SUBLANES, LANES), lambda b, i: (0, 0))],
        out_shape=[jax.ShapeDtypeStruct((cfg.tokens, LANES), F32), jax.ShapeDtypeStruct((SUBLANES, LANES), F32)],
        scratch_shapes=[pltpu.VMEM((SUBLANES, LANES), F32)],
        compiler_params=_params(("arbitrary", "arbitrary")),
    )(proj, jnp.broadcast_to(bias, (SUBLANES, LANES)), _tri(tr, upper=True), dcum)
    return dz, jnp.sum(db, axis=0)


FOX_BLOCK = 256
AUG = LANES
BIAS_TERMS = 3


def _fox_aug_k_call(kv, cum, cfg, *, name):
    t, d, hh = cfg.tokens, cfg.d_model, cfg.heads
    tr = _blk(t, ROW_BLOCK)

    def body(k_ref, c_ref, o_ref):
        lane = lax.broadcasted_iota(jnp.int32, (tr, LANES), 1)
        for hp in range(hh // 2):
            k2 = k_ref[:, hp * LANES:(hp + 1) * LANES].astype(F32)
            for e in range(2):
                h = 2 * hp + e
                kh = k2 if e == 0 else pltpu.roll(k2, HEAD_DIM, 1)
                c = -c_ref[:, h:h + 1]
                hi = c.astype(BF16).astype(F32)
                mid = (c - hi).astype(BF16).astype(F32)
                lo = c - hi - mid
                bias = jnp.where(lane == HEAD_DIM, hi, jnp.where(lane == HEAD_DIM + 1, mid,
                                 jnp.where(lane == HEAD_DIM + 2, lo, 0.0)))
                o_ref[:, h * AUG:(h + 1) * AUG] = jnp.where(lane < HEAD_DIM, kh, bias).astype(BF16)

    return pl.pallas_call(
        body, name=name, grid=(t // tr,),
        in_specs=[pl.BlockSpec((tr, d), lambda i: (i, 0)), pl.BlockSpec((tr, LANES), lambda i: (i, 0))],
        out_specs=pl.BlockSpec((tr, hh * AUG), lambda i: (i, 0)),
        out_shape=jax.ShapeDtypeStruct((t, hh * AUG), BF16),
        compiler_params=_params(("parallel",)),
    )(kv, cum)


def _keys_visible(tq):
    s = lax.broadcasted_iota(jnp.int32, (tq, tq), 0)
    t = lax.broadcasted_iota(jnp.int32, (tq, tq), 1)
    return s <= t


def _aug_q_t(q2, e, tq):
    ones = (lax.broadcasted_iota(jnp.int32, (AUG - HEAD_DIM, tq), 0) < BIAS_TERMS).astype(q2.dtype)
    return jnp.concatenate([q2[e * HEAD_DIM:(e + 1) * HEAD_DIM], ones], axis=0)


def _fox_fwd_n(q, k_aug, kv, cfg, *, name):
    t, d, hh = cfg.tokens, cfg.d_model, cfg.heads
    tq = _blk(cfg.seq, FOX_BLOCK)
    nq = cfg.seq // tq

    def body(q_ref, ka_ref, v_ref, o_ref, lse_ref, qa_ref, m_ref, l_ref, acc_ref):
        qi, ki = pl.program_id(1), pl.program_id(2)

        @pl.when(ki == 0)
        def _():
            m_ref[...] = jnp.full_like(m_ref, NEG)
            l_ref[...] = jnp.zeros_like(l_ref)
            acc_ref[...] = jnp.zeros_like(acc_ref)
            for hp in range(hh // 2):
                q2 = q_ref[:, hp * LANES:(hp + 1) * LANES].T
                for e in range(2):
                    h = 2 * hp + e
                    qa_ref[h * AUG:(h + 1) * AUG, :] = _aug_q_t(q2, e, tq)

        def step(diagonal):
            for hp in range(hh // 2):
                vt2 = v_ref[:, hp * LANES:(hp + 1) * LANES].T
                for e in range(2):
                    h = 2 * hp + e
                    rows = slice(h * HEAD_DIM, (h + 1) * HEAD_DIM)
                    s = jnp.dot(ka_ref[:, h * AUG:(h + 1) * AUG], qa_ref[h * AUG:(h + 1) * AUG, :],
                                preferred_element_type=F32)
                    if diagonal:
                        s = jnp.where(_keys_visible(tq), s, NEG)
                    m_prev = m_ref[h:h + 1, :]
                    m_new = jnp.maximum(m_prev, jnp.max(s, axis=0, keepdims=True))
                    alpha = jnp.exp(m_prev - m_new)
                    p = jnp.exp(s - m_new)
                    l_ref[h:h + 1, :] = alpha * l_ref[h:h + 1, :] + jnp.sum(p, axis=0, keepdims=True)
                    m_ref[h:h + 1, :] = m_new
                    hi = p.astype(BF16)
                    lo = (p - hi.astype(F32)).astype(BF16)
                    vt = vt2[e * HEAD_DIM:(e + 1) * HEAD_DIM]
                    acc_ref[rows, :] = (alpha * acc_ref[rows, :] + jnp.dot(vt, hi, preferred_element_type=F32)
                                        + jnp.dot(vt, lo, preferred_element_type=F32))

        pl.when(ki < qi)(functools.partial(step, False))
        pl.when(ki == qi)(functools.partial(step, True))

        @pl.when(ki == qi)
        def _():
            for hp in range(hh // 2):
                halves = [acc_ref[h * HEAD_DIM:(h + 1) * HEAD_DIM, :] * (1.0 / l_ref[h:h + 1, :])
                          for h in (2 * hp, 2 * hp + 1)]
                o_ref[:, hp * LANES:(hp + 1) * LANES] = jnp.concatenate(halves, axis=0).T
            lse_ref[...] = m_ref[...] + jnp.log(l_ref[...])

    def qrow(b, qi, ki):
        return (b * nq + qi, 0)

    return pl.pallas_call(
        body, name=name, grid=(cfg.batch, nq, nq),
        in_specs=[pl.BlockSpec((tq, d), qrow),
                  pl.BlockSpec((tq, hh * AUG), lambda b, qi, ki: (b * nq + jnp.minimum(ki, qi), 0)),
                  pl.BlockSpec((tq, d), lambda b, qi, ki: (b * nq + jnp.minimum(ki, qi), 1))],
        out_specs=[pl.BlockSpec((tq, d), qrow), pl.BlockSpec((hh, tq), lambda b, qi, ki: (0, b * nq + qi))],
        out_shape=[jax.ShapeDtypeStruct((t, d), F32), jax.ShapeDtypeStruct((hh, t), F32)],
        scratch_shapes=[pltpu.VMEM((hh * AUG, tq), BF16), pltpu.VMEM((hh, tq), F32), pltpu.VMEM((hh, tq), F32),
                        pltpu.VMEM((d, tq), F32)],
        compiler_params=_params(("parallel", "parallel", "arbitrary")),
    )(q, k_aug, kv)


def _head_dot_c(a, b, cfg, *, name):
    t, d, hh = cfg.tokens, cfg.d_model, cfg.heads
    tc = _blk(t, ROW_BLOCK)

    def body(a_ref, b_ref, o_ref):
        for hp in range(hh // 2):
            pair = slice(hp * LANES, (hp + 1) * LANES)
            prod = (a_ref[:, pair].astype(F32) * b_ref[:, pair]).T
            for e in range(2):
                h = 2 * hp + e
                o_ref[h:h + 1, :] = jnp.sum(prod[e * HEAD_DIM:(e + 1) * HEAD_DIM], axis=0, keepdims=True)

    return pl.pallas_call(
        body, name=name, grid=(t // tc,),
        in_specs=[pl.BlockSpec((tc, d), lambda i: (i, 0)), pl.BlockSpec((tc, d), lambda i: (i, 0))],
        out_specs=pl.BlockSpec((hh, tc), lambda i: (0, i)),
        out_shape=jax.ShapeDtypeStruct((hh, t), F32),
        compiler_params=_params(("parallel",)),
    )(a, b)


def _fox_bwd_n(q, k_aug, k_t, kv, do, lse, dsum, cfg, *, name):
    t, d, hh = cfg.tokens, cfg.d_model, cfg.heads
    tq = _blk(cfg.seq, FOX_BLOCK)
    nq = cfg.seq // tq

    def body(q_ref, ka_ref, kt_ref, v_ref, do_ref, lse_ref, ds_ref, dq_hbm, dk_ref, dv_ref, dc_ref, dq_acc, sem):
        b, ki, qi = pl.program_id(0), pl.program_id(1), pl.program_id(2)
        qq = jnp.maximum(qi, ki)

        @pl.when((ki == 0) & (qi == 0))
        def _():
            dq_acc[...] = jnp.zeros_like(dq_acc)

        @pl.when(qi == 0)
        def _():
            dk_ref[...] = jnp.zeros_like(dk_ref)
            dv_ref[...] = jnp.zeros_like(dv_ref)
            dc_ref[...] = jnp.zeros_like(dc_ref)

        def step(diagonal):
            upper = lax.broadcasted_iota(jnp.int32, (LANES, tq), 0) < HEAD_DIM
            half = _half_mask((tq, LANES))
            for hp in range(hh // 2):
                pair = slice(hp * LANES, (hp + 1) * LANES)
                q2 = q_ref[:, pair].T
                don2 = do_ref[:, pair]
                dot2 = don2.T
                dvs = []
                for e in range(2):
                    h = 2 * hp + e
                    rows = slice(h * HEAD_DIM, (h + 1) * HEAD_DIM)
                    aug = slice(h * AUG, (h + 1) * AUG)
                    qa = _aug_q_t(q2, e, tq)
                    s = jnp.dot(ka_ref[:, aug], qa, preferred_element_type=F32)
                    if diagonal:
                        s = jnp.where(_keys_visible(tq), s, NEG)
                    p = jnp.exp(s - lse_ref[h:h + 1, :])
                    dote = jnp.where(upper == (e == 0), dot2, jnp.zeros_like(dot2))
                    dp = jnp.dot(v_ref[:, pair], dote, preferred_element_type=F32)
                    dsf = p * (dp - ds_ref[h:h + 1, :])
                    dc_ref[:, h:h + 1] -= jnp.sum(dsf, axis=1, keepdims=True)
                    dsc = dsf.astype(BF16)
                    dvs.append(jnp.dot(p.astype(BF16), don2, preferred_element_type=F32))
                    dk_ref[:, aug] += lax.dot_general(dsc, qa, _NT, preferred_element_type=F32)
                    dq_acc[qq, rows, :] += jnp.dot(kt_ref[rows, :], dsc, preferred_element_type=F32)
                dv_ref[:, pair] += jnp.where(half, dvs[0], dvs[1])

        pl.when(qi > ki)(functools.partial(step, False))
        pl.when(qi == ki)(functools.partial(step, True))

        @pl.when((ki == nq - 1) & (qi == nq - 1))
        def _():
            cp = pltpu.make_async_copy(dq_acc, dq_hbm.at[b], sem)
            cp.start()
            cp.wait()

    def qrow(b, ki, qi):
        return (b * nq + jnp.maximum(qi, ki), 0)

    def qcol(b, ki, qi):
        return (0, b * nq + jnp.maximum(qi, ki))

    def krow(b, ki, qi):
        return (b * nq + ki, 0)

    return pl.pallas_call(
        body, name=name, grid=(cfg.batch, nq, nq),
        in_specs=[pl.BlockSpec((tq, d), qrow),
                  pl.BlockSpec((tq, hh * AUG), krow),
                  pl.BlockSpec((d, tq), lambda b, ki, qi: (0, b * nq + ki)),
                  pl.BlockSpec((tq, d), lambda b, ki, qi: (b * nq + ki, 1)),
                  pl.BlockSpec((tq, d), qrow),
                  pl.BlockSpec((hh, tq), qcol), pl.BlockSpec((hh, tq), qcol)],
        out_specs=[pl.BlockSpec(memory_space=pl.ANY), pl.BlockSpec((tq, hh * AUG), krow),
                   pl.BlockSpec((tq, d), krow), pl.BlockSpec((tq, LANES), krow)],
        out_shape=[jax.ShapeDtypeStruct((cfg.batch, nq, d, tq), F32), jax.ShapeDtypeStruct((t, hh * AUG), F32),
                   jax.ShapeDtypeStruct((t, d), F32), jax.ShapeDtypeStruct((t, LANES), F32)],
        scratch_shapes=[pltpu.VMEM((nq, d, tq), F32), pltpu.SemaphoreType.DMA],
        compiler_params=_params(("arbitrary", "arbitrary", "arbitrary")),
    )(q, k_aug, k_t, kv, do, lse, dsum)


def _fwd(a, w, *, name, res=None, scale=1.0, norms=()):
    bm = FUSED_ROWS if norms else 1024
    out = _mm(a, w, form="F", out_dtype=F32, name=name, bm=bm, bn=WIDE, bk=WIDE, res=res, scale=scale, norms=norms)
    return (out[0], list(out[1:])) if norms else out


def _bwd(dy, w, *, name, scale=1.0, norm_bwd=None):
    bm = FUSED_ROWS if norm_bwd is not None else 1024
    return _mm(dy, w, form="B", out_dtype=F32, name=name, bm=bm, bn=WIDE, bk=WIDE, scale=scale, norm_bwd=norm_bwd)


def _wgrad(a, dy, w, *, name, scale=1.0):
    return _mm_grad(a, dy, w.shape[0], name=name, bm=WIDE, bn=WIDE, scale=scale)


def _ffn_fwd(h, n, w_in, w_out, tag, norms=()):
    gate, up, a = _ffn_in_act(n, w_in, name=f"{tag}_in")
    out = _fwd(a, w_out, name=f"{tag}_out", res=h, scale=0.5, norms=norms)
    h_out, normed = out if norms else (out, [])
    return h_out, normed, (n, gate, up, a)


def _ffn_bwd(dh_out, h, g, w_in, w_out, saved, tag):
    n, gate, up, a = saved
    du = _ffn_out_dx_act(dh_out, w_out, gate, up, name=f"{tag}_out_dx", scale=0.5)
    dw_out = _wgrad(a, dh_out, w_out, name=f"{tag}_out_dw", scale=0.5)
    dh, dg = _mm_back2(du, w_in, (h, g, dh_out), name=f"{tag}_in_dx", bk=WIDE)
    dw_in = _mm_grad(n, du, w_in.shape[0], name=f"{tag}_in_dw", bm=WIDE, bn=WIDE)
    return dh, dg, dw_in, dw_out


def _head_gain(g, heads, scale=1.0):
    return jnp.tile(g.astype(F32) * scale, heads)


def _local_step(cfg, x, positions, target, w, s):
    d, hh = cfg.d_model, cfg.heads
    cos, sin = _rope_tables(positions)
    ones = jnp.ones((d,), F32)

    n00 = _rms_fwd(x, s["ffn_norm"][0, 0], name="ffn00_norm")
    h1, (hn_a,), ffn0 = _ffn_fwd(x, n00, w["ffn_w_in"][0][0], w["ffn_w_out"][0][0], "ffn00", [s["mix_norm"][0]])
    qkv = _fwd(hn_a, w["a_w_qkv"], name="a_qkv")
    kinds_a = ["rope", "rope", "cast"] * len(DILATIONS)
    gains_a = jnp.stack([z for g in range(len(DILATIONS)) for z in (
        _head_gain(s["a_q_norm"][g], hh, Q_SCALE), _head_gain(s["a_k_norm"][g], hh), ones)])
    qkvp = [_hn_fwd(qkv, gains_a[3 * g:3 * g + 3], kinds_a[:3], d, cos, sin, name=f"a_qk_norm{g}", col0=3 * g)
            for g in range(len(DILATIONS))]
    lay = [qkvp[g].reshape(cfg.tokens // dil, dil * 3 * d) for g, dil in enumerate(DILATIONS)]
    band = [_band_fwd_n(lay[g], dil, cfg, name=f"a_band{g}") for g, dil in enumerate(DILATIONS)]
    mixed, lse_a = _mix_fwd([o.reshape(cfg.tokens, d) for o, _ in band],
                            [jnp.repeat(_from_classes_t(l), HEAD_DIM, axis=1) for _, l in band], name="a_mix")
    h2, (n01,) = _fwd(mixed, w["a_w_o"], name="a_out", res=h1, norms=[s["ffn_norm"][0, 1]])
    h3, (kn, n10), ffn1 = _ffn_fwd(h2, n01, w["ffn_w_in"][0][1], w["ffn_w_out"][0][1], "ffn01",
                                   [s["kv_norm"], s["ffn_norm"][1, 0]])

    proj = _fwd(kn, w["kv_w"], name="kv_proj")
    kinds_kv = ["norm", "cast"]
    gains_kv = jnp.stack([_head_gain(s["kv_k_norm"], hh), ones])
    kvp = _hn_fwd(proj, gains_kv, kinds_kv, d, cos, sin, name="kv_k_norm")
    gate_col = 2 * d // LANES
    bias = jnp.pad(s["kv_b_f"].astype(F32), (0, LANES - hh))
    cum = _gate_fwd(proj, gate_col, bias, cfg, name="kv_gate")
    k_aug = _fox_aug_k_call(kvp, cum, cfg, name="kv_aug")

    h4, (hn_b,), ffn2 = _ffn_fwd(h3, n10, w["ffn_w_in"][1][0], w["ffn_w_out"][1][0], "ffn10", [s["mix_norm"][1]])
    qraw = _fwd(hn_b, w["b_w_q"], name="b_q")
    gains_b = _head_gain(s["b_q_norm"][0], hh, Q_SCALE)[None]
    qp = _hn_fwd(qraw, gains_b, ["norm"], d, cos, sin, name="b_q_norm")
    o_b, lse_b = _fox_fwd_n(qp, k_aug, kvp, cfg, name="b_fox")
    h5, (n11,) = _fwd(o_b, w["b_w_o"], name="b_out", res=h4, norms=[s["ffn_norm"][1, 1]])
    h6, _, ffn3 = _ffn_fwd(h5, n11, w["ffn_w_in"][1][1], w["ffn_w_out"][1][1], "ffn11")

    loss, dh6 = _loss_fwd_bwd(h6, target, name="loss")

    dh5, dg11, dwi11, dwo11 = _ffn_bwd(dh6, h5, s["ffn_norm"][1, 1], w["ffn_w_in"][1][1], w["ffn_w_out"][1][1],
                                       ffn3, "ffn11")
    do_b = _bwd(dh5, w["b_w_o"], name="b_out_dx")
    dw_bo = _wgrad(o_b, dh5, w["b_w_o"], name="b_out_dw")
    do_bf = do_b.astype(BF16)
    dsum_b = _head_dot_c(do_bf, o_b, cfg, name="b_dsum")
    dq4, dk_aug, dv_b, dcum = _fox_bwd_n(qp, k_aug, kvp[:, :d].T, kvp, do_bf, lse_b, dsum_b, cfg, name="b_fox_bwd")
    dq_b = dq4.transpose(0, 1, 3, 2).reshape(cfg.tokens, d)
    dk_b = dk_aug.reshape(cfg.tokens, hh, AUG)[:, :, :HEAD_DIM].reshape(cfg.tokens, d)
    dqraw, dgq = _hn_bwd(qraw, [dq_b], gains_b, ["norm"], d, cos, sin, name="b_q_norm_bwd")
    dh4, dmix1 = _bwd(dqraw, w["b_w_q"], name="b_q_dx", norm_bwd=(h4, s["mix_norm"][1], dh5))
    dw_bq = _wgrad(hn_b, dqraw, w["b_w_q"], name="b_q_dw")
    dh3, dg10, dwi10, dwo10 = _ffn_bwd(dh4, h3, s["ffn_norm"][1, 0], w["ffn_w_in"][1][0], w["ffn_w_out"][1][0],
                                       ffn2, "ffn10")

    dkvraw, dgk = _hn_bwd(proj, [dk_b, dv_b], gains_kv, kinds_kv, d, cos, sin, name="kv_k_norm_bwd")
    dz, dbias = _gate_bwd(proj, gate_col, bias, dcum, cfg, name="kv_gate_bwd")
    pad_cols = w["kv_w"].shape[2] - 2 * d - LANES
    dproj = jnp.concatenate([dkvraw, dz.astype(BF16), jnp.zeros((cfg.tokens, pad_cols), BF16)], axis=1)
    dw_kv = _wgrad(kn, dproj, w["kv_w"], name="kv_proj_dw")
    dh3, dkvn = _bwd(dproj, w["kv_w"], name="kv_proj_dx", norm_bwd=(h3, s["kv_norm"], dh3))

    dh2, dg01, dwi01, dwo01 = _ffn_bwd(dh3, h2, s["ffn_norm"][0, 1], w["ffn_w_in"][0][1], w["ffn_w_out"][0][1],
                                       ffn1, "ffn01")
    dmixed = _bwd(dh2, w["a_w_o"], name="a_out_dx")
    dw_ao = _wgrad(mixed, dh2, w["a_w_o"], name="a_out_dw")
    dsum_a = _head_dot(dmixed, mixed, name="a_dsum")
    dqkvp = []
    dmixed_bf = dmixed.astype(BF16)
    lse_h, dsum_h = lse_a[:, ::HEAD_DIM], dsum_a[:, ::HEAD_DIM]
    for g, dil in enumerate(DILATIONS):
        grads = _band_bwd_n(lay[g], dmixed_bf.reshape(cfg.tokens // dil, dil * d), _to_classes_t(lse_h, dil, hh),
                            _to_classes_t(dsum_h, dil, hh), dil, cfg, name=f"a_band{g}_bwd")
        dqkvp += [z.reshape(cfg.tokens, d) for z in grads]
    dqkv, dga = _hn_bwd(qkv, dqkvp, gains_a, kinds_a, d, cos, sin, name="a_qk_norm_bwd")
    dh1, dmix0 = _bwd(dqkv, w["a_w_qkv"], name="a_qkv_dx", norm_bwd=(h1, s["mix_norm"][0], dh2))
    dw_qkv = _wgrad(hn_a, dqkv, w["a_w_qkv"], name="a_qkv_dw")
    dx, dg00, dwi00, dwo00 = _ffn_bwd(dh1, x, s["ffn_norm"][0, 0], w["ffn_w_in"][0][0], w["ffn_w_out"][0][0],
                                      ffn0, "ffn00")

    dw = {
        "ffn_w_in": [[dwi00, dwi01], [dwi10, dwi11]],
        "ffn_w_out": [[dwo00, dwo01], [dwo10, dwo11]],
        "a_w_qkv": dw_qkv, "a_w_o": dw_ao, "kv_w": dw_kv, "b_w_q": dw_bq, "b_w_o": dw_bo,
    }
    ds = {
        "ffn_norm": jnp.stack([jnp.stack([dg00, dg01]), jnp.stack([dg10, dg11])]),
        "mix_norm": jnp.stack([dmix0, dmix1]),
        "a_q_norm": jnp.stack([dga[3 * g] for g in range(len(DILATIONS))])[None] * Q_SCALE,
        "a_k_norm": jnp.stack([dga[3 * g + 1] for g in range(len(DILATIONS))])[None],
        "kv_norm": dkvn,
        "kv_b_f": dbias[:hh],
        "kv_k_norm": dgk[0],
        "b_q_norm": dgq * Q_SCALE,
    }
    return loss, dx, dw, ds


MESH_ID = pl.DeviceIdType.MESH
ANY = pl.BlockSpec(memory_space=pl.ANY)
PACK_COLS = 1024
PACK_ROW_ALIGN = 32


def _me():
    return lax.axis_index("x"), lax.axis_index("y"), lax.axis_index("c")


def _other_chips(x, y):
    return [(1 - x, y), (x, 1 - y), (1 - x, 1 - y)]


def _all_gather_small(v, *, name):
    r = v.shape[0]

    def body(v_ref, out_ref, send_sems, recv_sems):
        x, y, c = _me()
        me = 4 * x + 2 * y + c
        out_ref[me] = v_ref[...]
        copies = []
        for k in range(1, N_DEV):
            fx, fy, fc = (k >> 2) & 1, (k >> 1) & 1, k & 1
            peer = (1 - x if fx else x, 1 - y if fy else y, 1 - c if fc else c)
            copies.append(pltpu.make_async_remote_copy(
                src_ref=v_ref, dst_ref=out_ref.at[me], send_sem=send_sems.at[k - 1], recv_sem=recv_sems.at[k - 1],
                device_id=peer, device_id_type=MESH_ID))
        for cp in copies:
            cp.start()
        for cp in copies:
            cp.wait()

    return pl.pallas_call(
        body, name=name,
        in_specs=[pl.BlockSpec(memory_space=pltpu.VMEM)], out_specs=pl.BlockSpec(memory_space=pltpu.VMEM),
        out_shape=jax.ShapeDtypeStruct((N_DEV, r, LANES), v.dtype),
        scratch_shapes=[pltpu.SemaphoreType.DMA((N_DEV - 1,)), pltpu.SemaphoreType.DMA((N_DEV - 1,))],
    )(v)


def _all_gather_chips(v, *, name):
    rh = v.shape[0] // 2

    def body(v_ref, out_ref, send_sems, recv_sems):
        x, y, c = _me()
        j = 2 * x + y
        chips = _other_chips(x, y)

        def half(chip, core):
            return out_ref.at[chip, pl.ds(core * rh, rh)]

        first = [pltpu.make_async_remote_copy(
            src_ref=v_ref.at[pl.ds(c * rh, rh)], dst_ref=half(j, c), send_sem=send_sems.at[k],
            recv_sem=recv_sems.at[k], device_id=(px, py, c), device_id_type=MESH_ID)
            for k, (px, py) in enumerate(chips)]
        for cp in first:
            cp.start()
        passed = [pltpu.make_async_remote_copy(
            src_ref=half(2 * px + py, c), dst_ref=half(2 * px + py, c), send_sem=send_sems.at[3 + k],
            recv_sem=recv_sems.at[3 + k], device_id=(x, y, 1 - c), device_id_type=MESH_ID)
            for k, (px, py) in enumerate(chips)]
        for k in range(len(chips)):
            first[k].wait_recv()
            passed[k].start()
        for k, (px, py) in enumerate(chips):
            pltpu.make_async_remote_copy(
                src_ref=half(2 * px + py, 1 - c), dst_ref=half(2 * px + py, 1 - c), send_sem=send_sems.at[3 + k],
                recv_sem=recv_sems.at[3 + k], device_id=(x, y, 1 - c), device_id_type=MESH_ID).wait_recv()
        for cp in first + passed:
            cp.wait_send()

    return pl.pallas_call(
        body, name=name, in_specs=[ANY], out_specs=ANY,
        out_shape=jax.ShapeDtypeStruct((N_CHIPS,) + v.shape, v.dtype),
        scratch_shapes=[pltpu.SemaphoreType.DMA((2 * (N_CHIPS - 1),)), pltpu.SemaphoreType.DMA((2 * (N_CHIPS - 1),))],
    )(v)


def _swap_halves(g, *, name):
    n, r, cols = g.shape
    rh = r // 2

    def body(g_ref, out_ref, send_sem, recv_sem):
        x, y, c = _me()
        cp = pltpu.make_async_remote_copy(
            src_ref=g_ref.at[:, pl.ds((1 - c) * rh, rh)], dst_ref=out_ref, send_sem=send_sem, recv_sem=recv_sem,
            device_id=(x, y, 1 - c), device_id_type=MESH_ID)
        cp.start()
        cp.wait()

    return pl.pallas_call(
        body, name=name, in_specs=[ANY], out_specs=ANY,
        out_shape=jax.ShapeDtypeStruct((n, rh, cols), g.dtype),
        scratch_shapes=[pltpu.SemaphoreType.DMA, pltpu.SemaphoreType.DMA],
    )(g)


def _scatter_chips(v, *, name):
    def body(v_ref, out_ref, send_sems, recv_sems):
        x, y, c = _me()
        j = 2 * x + y
        copies = [pltpu.make_async_remote_copy(
            src_ref=v_ref.at[2 * px + py], dst_ref=out_ref.at[j], send_sem=send_sems.at[k], recv_sem=recv_sems.at[k],
            device_id=(px, py, c), device_id_type=MESH_ID) for k, (px, py) in enumerate(_other_chips(x, y))]
        for cp in copies:
            cp.start()
        for cp in copies:
            cp.wait()

    return pl.pallas_call(
        body, name=name, in_specs=[ANY], out_specs=ANY,
        out_shape=jax.ShapeDtypeStruct(v.shape, v.dtype),
        scratch_shapes=[pltpu.SemaphoreType.DMA((N_CHIPS - 1,)), pltpu.SemaphoreType.DMA((N_CHIPS - 1,))],
    )(v)


def _join_halves(v, *, name):
    def body(v_ref, out_ref, send_sem, recv_sem):
        x, y, c = _me()
        cp = pltpu.make_async_remote_copy(
            src_ref=v_ref, dst_ref=out_ref.at[c], send_sem=send_sem, recv_sem=recv_sem,
            device_id=(x, y, 1 - c), device_id_type=MESH_ID)
        cp.start()
        cp.wait()

    return pl.pallas_call(
        body, name=name, in_specs=[ANY], out_specs=ANY,
        out_shape=jax.ShapeDtypeStruct((2,) + v.shape, v.dtype),
        scratch_shapes=[pltpu.SemaphoreType.DMA, pltpu.SemaphoreType.DMA],
    )(v)


def _row_blk(rows, want):
    for b in range(min(rows, want) // SUBLANES * SUBLANES, 0, -SUBLANES):
        if rows % b == 0:
            return b
    return rows


def _add_own_half(g, got, *, name):
    n, r, cols = g.shape
    rh = r // 2
    tr = _row_blk(rh, 512)
    nb = rh // tr

    def body(c_ref, g_ref, got_ref, o_ref):
        del c_ref
        o_ref[...] = (g_ref[...] + got_ref[...]).astype(BF16)

    grid_spec = pltpu.PrefetchScalarGridSpec(
        num_scalar_prefetch=1, grid=(n, nb),
        in_specs=[pl.BlockSpec((None, tr, cols), lambda j, i, c: (j, c[0] * nb + i, 0)),
                  pl.BlockSpec((None, tr, cols), lambda j, i, c: (j, i, 0))],
        out_specs=pl.BlockSpec((None, tr, cols), lambda j, i, c: (j, i, 0)))
    return pl.pallas_call(
        body, name=name, grid_spec=grid_spec, out_shape=jax.ShapeDtypeStruct((n, rh, cols), BF16),
        compiler_params=_params(("parallel", "parallel")),
    )(lax.axis_index("c").astype(jnp.int32).reshape(1), g, got)


def _sum_parts(parts, *, name):
    n, r, cols = parts.shape
    tr = _row_blk(r, 512)

    def body(*refs):
        o_ref = refs[n]
        acc = refs[0][...].astype(F32)
        for p_ref in refs[1:n]:
            acc = acc + p_ref[...].astype(F32)
        o_ref[...] = acc

    return pl.pallas_call(
        body, name=name, grid=(r // tr,),
        in_specs=[pl.BlockSpec((None, tr, cols), functools.partial(lambda j, i: (j, i, 0), j)) for j in range(n)],
        out_specs=pl.BlockSpec((tr, cols), lambda i: (i, 0)),
        out_shape=jax.ShapeDtypeStruct((r, cols), F32),
        compiler_params=_params(("parallel",)),
    )(*([parts] * n))


def _adamw(w, m, v, g, *, name):
    shape = w.shape
    cols = shape[-1]
    w2, m2, v2, g2 = (z.reshape(-1, cols) for z in (w, m, v, g))
    rows = w2.shape[0]
    tr = _row_blk(rows, max(SUBLANES, (1 << 20) // (4 * cols)))

    def body(w_ref, m_ref, v_ref, g_ref, d_ref, nm_ref, nv_ref):
        gv = g_ref[...]
        nm = ADAM_B1 * m_ref[...] + (1.0 - ADAM_B1) * gv
        nv = ADAM_B2 * v_ref[...] + (1.0 - ADAM_B2) * jnp.square(gv)
        m_hat = nm / (1.0 - ADAM_B1 ** ADAM_STEP)
        v_hat = nv / (1.0 - ADAM_B2 ** ADAM_STEP)
        d_ref[...] = -ADAM_LR * (m_hat / (jnp.sqrt(v_hat) + ADAM_EPS) + ADAM_WD * w_ref[...])
        nm_ref[...] = nm
        nv_ref[...] = nv

    spec = pl.BlockSpec((tr, cols), lambda i: (i, 0))
    out = jax.ShapeDtypeStruct((rows, cols), F32)
    d, nm, nv = pl.pallas_call(
        body, name=name, grid=(rows // tr,), in_specs=[spec] * 4, out_specs=[spec] * 3, out_shape=[out] * 3,
        compiler_params=_params(("parallel",)),
    )(w2, m2, v2, g2)
    return d.reshape(shape), nm.reshape(shape), nv.reshape(shape)


def _pack_rows(size, cols, align):
    return -(-size // (cols * align)) * align


def _pack(arrs, lead, cols, align, total_align):
    lead_shape = arrs[0].shape[:lead]
    parts = []
    for a in arrs:
        flat = a.reshape(lead_shape + (-1,))
        size = flat.shape[-1]
        rows = _pack_rows(size, cols, align)
        flat = jnp.pad(flat, [(0, 0)] * lead + [(0, rows * cols - size)])
        parts.append(flat.reshape(lead_shape + (rows, cols)))
    total = sum(p.shape[lead] for p in parts)
    extra = -total % total_align
    if extra:
        parts.append(jnp.zeros(lead_shape + (extra, cols), parts[0].dtype))
    return jnp.concatenate(parts, axis=lead)


def _unpack(buf, shapes, lead, cols, align):
    lead_shape = buf.shape[:lead]
    out, row = [], 0
    for shp in shapes:
        size = 1
        for n in shp:
            size *= n
        rows = _pack_rows(size, cols, align)
        piece = lax.slice_in_dim(buf, row, row + rows, axis=lead).reshape(lead_shape + (-1,))
        out.append(piece[..., :size].reshape(lead_shape + tuple(shp)))
        row += rows
    return out


BIG = ("ffn_w_in", "ffn_w_out", "a_w_qkv", "a_w_o", "kv_w", "b_w_q", "b_w_o")
SMALL = ("ffn_norm", "mix_norm", "a_q_norm", "a_k_norm", "kv_norm", "kv_b_f", "kv_k_norm", "b_q_norm")
WEIGHTS = ("ffn_norm", "ffn_w_in", "ffn_w_out", "mix_norm", "a_w_qkv", "a_q_norm", "a_k_norm", "a_w_o",
           "kv_norm", "kv_w", "kv_b_f", "kv_k_norm", "b_w_q", "b_q_norm", "b_w_o")
GATE_PAD = 2 * LANES


def _stack_weights(sh, d):
    depth = sh["ffn_w_in"].shape[1]
    kv = sh["kv_w"].transpose(1, 0, 2).reshape(d, -1)
    kv = jnp.pad(kv, ((0, 0), (0, 2 * d + GATE_PAD - kv.shape[1])))
    return {
        "ffn_w_in": [[sh["ffn_w_in"][:, l, i] for i in range(2)] for l in range(depth)],
        "ffn_w_out": [[sh["ffn_w_out"][:, l, i].reshape(1, -1, d) for i in range(2)] for l in range(depth)],
        "a_w_qkv": sh["a_w_qkv"][:, 0],
        "a_w_o": sh["a_w_o"].reshape(1, d, d),
        "kv_w": kv[None],
        "b_w_q": sh["b_w_q"].reshape(1, d, d),
        "b_w_o": sh["b_w_o"].reshape(1, d, d),
    }


def _unstack_grads(dw, d, heads):
    def rows4(z):
        return z.reshape(N_CHIPS, -1, d)

    kv_cols = 2 * d + heads
    kv = dw["kv_w"][0][:, :kv_cols].reshape(d, N_CHIPS, kv_cols // N_CHIPS).transpose(1, 0, 2)
    return [
        jnp.stack([jnp.stack(row, axis=1) for row in dw["ffn_w_in"]], axis=1),
        jnp.stack([jnp.stack([rows4(z) for z in row], axis=1) for row in dw["ffn_w_out"]], axis=1),
        dw["a_w_qkv"][:, None],
        rows4(dw["a_w_o"])[:, None],
        kv,
        rows4(dw["b_w_q"])[:, None],
        rows4(dw["b_w_o"])[:, None],
    ]


def kernel(x, positions, ffn_norm, ffn_w_in, ffn_w_out, mix_norm, a_w_qkv, a_q_norm, a_k_norm, a_w_o, kv_norm, kv_w, kv_b_f, kv_k_norm, b_w_q, b_q_norm, b_w_o, loss_target, m_ffn_norm, m_ffn_w_in, m_ffn_w_out, m_mix_norm, m_a_w_qkv, m_a_q_norm, m_a_k_norm, m_a_w_o, m_kv_norm, m_kv_w, m_kv_b_f, m_kv_k_norm, m_b_w_q, m_b_q_norm, m_b_w_o, v_ffn_norm, v_ffn_w_in, v_ffn_w_out, v_mix_norm, v_a_w_qkv, v_a_q_norm, v_a_k_norm, v_a_w_o, v_kv_norm, v_kv_w, v_kv_b_f, v_kv_k_norm, v_b_w_q, v_b_q_norm, v_b_w_o):
    wts = dict(zip(WEIGHTS, (ffn_norm, ffn_w_in, ffn_w_out, mix_norm, a_w_qkv, a_q_norm, a_k_norm, a_w_o, kv_norm,
                             kv_w, kv_b_f, kv_k_norm, b_w_q, b_q_norm, b_w_o)))
    mom = dict(zip(WEIGHTS, (m_ffn_norm, m_ffn_w_in, m_ffn_w_out, m_mix_norm, m_a_w_qkv, m_a_q_norm, m_a_k_norm,
                             m_a_w_o, m_kv_norm, m_kv_w, m_kv_b_f, m_kv_k_norm, m_b_w_q, m_b_q_norm, m_b_w_o)))
    var = dict(zip(WEIGHTS, (v_ffn_norm, v_ffn_w_in, v_ffn_w_out, v_mix_norm, v_a_w_qkv, v_a_q_norm, v_a_k_norm,
                             v_a_w_o, v_kv_norm, v_kv_w, v_kv_b_f, v_kv_k_norm, v_b_w_q, v_b_q_norm, v_b_w_o)))
    batch, seq, d = x.shape
    cfg = Cfg(d_model=d, d_ff=ffn_w_out.shape[2] * N_CHIPS, seq=seq, batch=batch)
    chip = 2 * lax.axis_index("x") + lax.axis_index("y")
    big_shapes = [wts[n].shape for n in BIG]

    shard = _pack([wts[n].astype(BF16) for n in BIG], 0, PACK_COLS, PACK_ROW_ALIGN, PACK_COLS)
    gathered = _all_gather_chips(shard, name="gather_weights")
    gathered = lax.dynamic_update_slice_in_dim(gathered, shard[None], chip, axis=0)
    w = _stack_weights(dict(zip(BIG, _unpack(gathered, big_shapes, 1, PACK_COLS, PACK_ROW_ALIGN))), d)
    norm_shard = _pack([ffn_norm], 0, LANES, SUBLANES, SUBLANES)
    norms = _all_gather_small(norm_shard, name="gather_ffn_norm")[0::2]
    (norms,) = _unpack(norms, [ffn_norm.shape], 1, LANES, SUBLANES)
    small = {"ffn_norm": jnp.moveaxis(norms, 0, 2).reshape(ffn_norm.shape[:2] + (d,)),
             "mix_norm": mix_norm, "a_q_norm": a_q_norm[0], "a_k_norm": a_k_norm[0], "kv_norm": kv_norm,
             "kv_b_f": kv_b_f, "kv_k_norm": kv_k_norm, "b_q_norm": b_q_norm}

    loss, dx, dw, ds = _local_step(cfg, x.reshape(cfg.tokens, d), positions.reshape(cfg.tokens),
                                   loss_target.reshape(cfg.tokens, d), w, small)
    loss = lax.psum(loss, ("x", "y", "c"))

    g = _pack(_unstack_grads(dw, d, cfg.heads), 1, PACK_COLS, PACK_ROW_ALIGN, PACK_COLS)
    chip_half = _add_own_half(g, _swap_halves(g, name="swap_halves"), name="add_halves")
    parts = _scatter_chips(chip_half, name="scatter_chips")
    parts = lax.dynamic_update_slice_in_dim(parts, lax.dynamic_slice_in_dim(chip_half, chip, 1, axis=0), chip, axis=0)
    mine = _sum_parts(parts, name="sum_chips")
    both = _join_halves(mine, name="join_halves")
    g_big = lax.dynamic_update_slice_in_dim(both, mine[None], lax.axis_index("c"), axis=0).reshape(g.shape[1:])
    grads = dict(zip(BIG, _unpack(g_big, big_shapes, 0, PACK_COLS, PACK_ROW_ALIGN)))

    small_shapes = [ds[n].shape for n in SMALL]
    parts = _all_gather_small(_pack([ds[n] for n in SMALL], 0, LANES, SUBLANES, SUBLANES), name="gather_small")
    g_small = dict(zip(SMALL, _unpack(_sum_parts(parts, name="sum_small"), small_shapes, 0, LANES, SUBLANES)))
    quarter = d // N_CHIPS
    g_small["ffn_norm"] = lax.dynamic_slice_in_dim(g_small["ffn_norm"], chip * quarter, quarter, axis=2)
    grads.update(g_small)

    delta, new_m, new_v = {}, {}, {}
    for n in BIG:
        delta[n], new_m[n], new_v[n] = _adamw(wts[n], mom[n], var[n], grads[n], name=f"adamw_{n}")
    packed = [_pack([z[n] for n in SMALL], 0, LANES, SUBLANES, SUBLANES) for z in (wts, mom, var, grads)]
    small_out = _adamw(*packed, name="adamw_small")
    shard_shapes = [wts[n].shape for n in SMALL]
    for out, res in zip((delta, new_m, new_v), small_out):
        out.update(zip(SMALL, _unpack(res, shard_shapes, 0, LANES, SUBLANES)))

    return (loss, dx.reshape(x.shape), *[grads[n] for n in WEIGHTS], *[delta[n] for n in WEIGHTS],
            *[new_m[n] for n in WEIGHTS], *[new_v[n] for n in WEIGHTS])
```

```python
import functools
from typing import NamedTuple

import jax
import jax.numpy as jnp
from jax import lax
from jax.experimental import pallas as pl
from jax.experimental.pallas import tpu as pltpu

F32 = jnp.float32
BF16 = jnp.bfloat16

HEAD_DIM = 64
LANES = 128
SUBLANES = 8
ROT_DIM = HEAD_DIM // 4
ROPE_THETA = 500000.0
NORM_EPS = 1e-6
BAND = 128
DILATIONS = (1, 4, 16)
NEG = -1e30
Q_SCALE = HEAD_DIM ** -0.5
N_CHIPS = 4
N_DEV = 8
VMEM_LIMIT = 48 * 1024 * 1024
WIDE = 1536
ROW_BLOCK = 512
FUSED_ROWS = 512
_NT = (((1,), (1,)), ((), ()))

ADAM_LR = 0.001
ADAM_B1 = 0.9
ADAM_B2 = 0.999
ADAM_EPS = 1e-08
ADAM_WD = 0.01
ADAM_STEP = 10


class Cfg(NamedTuple):
    d_model: int
    d_ff: int
    seq: int
    batch: int

    @property
    def heads(self):
        return self.d_model // HEAD_DIM

    @property
    def tokens(self):
        return self.batch * self.seq


def _params(sem):
    return pltpu.CompilerParams(dimension_semantics=sem, vmem_limit_bytes=VMEM_LIMIT)


def _blk(dim, want):
    if dim <= want:
        return dim
    for b in range(want // LANES * LANES, 0, -LANES):
        if dim % b == 0:
            return b
    b = want
    while dim % b:
        b //= 2
    return b


def _fold8(x):
    return jnp.sum(x.reshape(x.shape[0] // SUBLANES, SUBLANES, x.shape[1]), axis=0)


def _rms_bwd_tile(xv, g, dyv, dres):
    rstd = lax.rsqrt(jnp.mean(xv * xv, axis=-1, keepdims=True) + NORM_EPS)
    xhat = xv * rstd
    dyg = dyv * g
    proj = jnp.mean(dyg * xhat, axis=-1, keepdims=True)
    return dres + rstd * (dyg - xhat * proj), _fold8(dyv * xhat)


def _mm(a, b, *, form, out_dtype, name, bm=1024, bn=1024, bk=1024, res=None, scale=1.0, norms=(), norm_bwd=None):
    if form == "F":
        m, kdim = a.shape
        jn, _, ns = b.shape
        bm, bn, bk = _blk(m, bm), _blk(ns, bn), _blk(kdim, bk)
        npj = ns // bn
        grid = (m // bm, jn * npj, kdim // bk)
        a_spec = pl.BlockSpec((bm, bk), lambda i, n, k: (i, k))
        b_spec = pl.BlockSpec((None, bk, bn), lambda i, n, k: (n // npj, k, n % npj))
        o_spec = pl.BlockSpec((bm, bn), lambda i, n, k: (i, n))
        o_shape = jax.ShapeDtypeStruct((m, jn * ns), out_dtype)
        dims = (((1,), (0,)), ((), ()))
    elif form == "B":
        m = a.shape[0]
        jn, kdim, ns = b.shape
        bm, bn, bk = _blk(m, bm), _blk(kdim, bn), _blk(ns, bk)
        kpj = ns // bk
        grid = (m // bm, kdim // bn, jn * kpj)
        a_spec = pl.BlockSpec((bm, bk), lambda i, n, k: (i, k))
        b_spec = pl.BlockSpec((None, bn, bk), lambda i, n, k: (k // kpj, n, k % kpj))
        o_spec = pl.BlockSpec((bm, bn), lambda i, n, k: (i, n))
        o_shape = jax.ShapeDtypeStruct((m, kdim), out_dtype)
        dims = _NT
    else:
        raise ValueError(form)
    nk = grid[2]
    n_norms = len(norms)
    full_rows = grid[1] == 1
    assert full_rows or (not norms and norm_bwd is None)

    def body(*refs):
        a_ref, b_ref = refs[:2]
        pos = 2
        r_ref = None
        if res is not None:
            r_ref = refs[pos]
            pos += 1
        g_refs = refs[pos:pos + n_norms]
        pos += n_norms
        if norm_bwd is not None:
            x_ref, gb_ref, dres_ref = refs[pos:pos + 3]
            pos += 3
        o_ref = refs[pos]
        n_refs = refs[pos + 1:pos + 1 + n_norms]
        acc_ref = refs[-1]
        i, k = pl.program_id(0), pl.program_id(2)

        @pl.when(k == 0)
        def _():
            acc_ref[...] = jnp.zeros_like(acc_ref)

        acc_ref[...] += lax.dot_general(a_ref[...].astype(BF16), b_ref[...].astype(BF16), dims,
                                        preferred_element_type=F32)

        if norm_bwd is not None:
            dg_ref = refs[pos + 1 + n_norms]

            @pl.when((i == 0) & (k == 0))
            def _():
                dg_ref[...] = jnp.zeros_like(dg_ref)

        @pl.when(k == nk - 1)
        def _():
            r = acc_ref[...]
            if scale != 1.0:
                r = r * scale
            if r_ref is not None:
                r = r_ref[...] + r
            if norm_bwd is not None:
                dx, dg8 = _rms_bwd_tile(x_ref[...], gb_ref[...], r, dres_ref[...])
                o_ref[...] = dx
                dg_ref[...] += dg8
            else:
                o_ref[...] = r.astype(o_ref.dtype)
            if n_norms:
                rstd = lax.rsqrt(jnp.mean(r * r, axis=-1, keepdims=True) + NORM_EPS)
                for g_ref, n_ref in zip(g_refs, n_refs):
                    n_ref[...] = ((r * rstd) * g_ref[...]).astype(BF16)

    row = pl.BlockSpec((bm, bn), lambda i, n, k: (i, n))
    vec = pl.BlockSpec((1, bn), lambda i, n, k: (0, 0))
    in_specs = [a_spec, b_spec]
    args = [a, b]
    if res is not None:
        in_specs.append(row)
        args.append(res)
    for g in norms:
        in_specs.append(vec)
        args.append(g.reshape(1, -1))
    out_specs, out_shapes = [o_spec], [o_shape]
    for _ in norms:
        out_specs.append(row)
        out_shapes.append(jax.ShapeDtypeStruct(o_shape.shape, BF16))
    if norm_bwd is not None:
        x, g, dres = norm_bwd
        in_specs += [row, vec, row]
        args += [x, g.reshape(1, -1), dres]
        out_specs.append(pl.BlockSpec((SUBLANES, bn), lambda i, n, k: (0, 0)))
        out_shapes.append(jax.ShapeDtypeStruct((SUBLANES, o_shape.shape[1]), F32))
    sem = ("arbitrary",) * 3 if norm_bwd is not None else ("parallel", "parallel", "arbitrary")
    single = len(out_specs) == 1
    out = pl.pallas_call(
        body, name=name, grid=grid, in_specs=in_specs, out_specs=out_specs[0] if single else out_specs,
        out_shape=out_shapes[0] if single else out_shapes,
        scratch_shapes=[pltpu.VMEM((bm, bn), F32)],
        compiler_params=_params(sem),
    )(*args)
    if norm_bwd is not None:
        return out[0], jnp.sum(out[1], axis=0)
    return out


def _mm_grad(a, dy, jn, *, name, scale=1.0, bm=1024, bn=1024, bk=1024):
    halves = dy if isinstance(dy, (tuple, list)) else (dy,)
    t, kdim = a.shape
    ns = len(halves) * halves[0].shape[1] // jn
    bm, bn, bk = _blk(kdim, bm), _blk(ns, bn), _blk(t, bk)
    npj = ns // bn
    grid = (kdim // bm, jn * npj, t // bk)
    nk = grid[2]
    nhalf = jn * npj // len(halves)
    dims = (((0,), (0,)), ((), ()))

    def body(a_ref, *refs):
        b_refs, o_ref, acc_ref = refs[:len(halves)], refs[-2], refs[-1]
        n, k = pl.program_id(1), pl.program_id(2)

        @pl.when(k == 0)
        def _():
            acc_ref[...] = jnp.zeros_like(acc_ref)

        for which, b_ref in enumerate(b_refs):
            @pl.when(n // nhalf == which)
            def _(b_ref=b_ref):
                acc_ref[...] += lax.dot_general(a_ref[...].astype(BF16), b_ref[...].astype(BF16), dims,
                                                preferred_element_type=F32)

        @pl.when(k == nk - 1)
        def _():
            r = acc_ref[...]
            if scale != 1.0:
                r = r * scale
            o_ref[...] = r

    def half_spec(which):
        return pl.BlockSpec((bk, bn), lambda m, n, k: (jnp.where(n // nhalf == which, k, 0),
                                                        jnp.where(n // nhalf == which, n % nhalf, 0)))

    return pl.pallas_call(
        body, name=name, grid=grid,
        in_specs=[pl.BlockSpec((bk, bm), lambda m, n, k: (k, m))] + [half_spec(w) for w in range(len(halves))],
        out_specs=pl.BlockSpec((None, bm, bn), lambda m, n, k: (n // npj, m, n % npj)),
        out_shape=jax.ShapeDtypeStruct((jn, kdim, ns), F32),
        scratch_shapes=[pltpu.VMEM((bm, bn), F32)],
        compiler_params=_params(("parallel", "parallel", "arbitrary")),
    )(a, *halves)


def _mm_back2(dy_halves, w, norm_bwd, *, name, bm=FUSED_ROWS, bk=1024):
    x, g, dres = norm_bwd
    m = dy_halves[0].shape[0]
    jn, kdim, ns = w.shape
    bm, bk = _blk(m, bm), _blk(ns, bk)
    kpj = ns // bk
    nk = jn * kpj
    khalf = nk // 2

    def body(a0_ref, a1_ref, b_ref, x_ref, g_ref, dres_ref, o_ref, dg_ref, acc_ref):
        i, k = pl.program_id(0), pl.program_id(1)

        @pl.when(k == 0)
        def _():
            acc_ref[...] = jnp.zeros_like(acc_ref)

        @pl.when((i == 0) & (k == 0))
        def _():
            dg_ref[...] = jnp.zeros_like(dg_ref)

        for which, a_ref in enumerate((a0_ref, a1_ref)):
            @pl.when(k // khalf == which)
            def _(a_ref=a_ref):
                acc_ref[...] += lax.dot_general(a_ref[...], b_ref[...], _NT, preferred_element_type=F32)

        @pl.when(k == nk - 1)
        def _():
            dx, dg8 = _rms_bwd_tile(x_ref[...], g_ref[...], acc_ref[...], dres_ref[...])
            o_ref[...] = dx
            dg_ref[...] += dg8

    def half_spec(which):
        return pl.BlockSpec((bm, bk), lambda i, k: (i, jnp.clip(k - which * khalf, 0, khalf - 1)))

    row = pl.BlockSpec((bm, kdim), lambda i, k: (i, 0))
    dx, dg = pl.pallas_call(
        body, name=name, grid=(m // bm, nk),
        in_specs=[half_spec(0), half_spec(1),
                  pl.BlockSpec((None, kdim, bk), lambda i, k: (k // kpj, 0, k % kpj)),
                  row, pl.BlockSpec((1, kdim), lambda i, k: (0, 0)), row],
        out_specs=[row, pl.BlockSpec((SUBLANES, kdim), lambda i, k: (0, 0))],
        out_shape=[jax.ShapeDtypeStruct((m, kdim), F32), jax.ShapeDtypeStruct((SUBLANES, kdim), F32)],
        scratch_shapes=[pltpu.VMEM((bm, kdim), F32)],
        compiler_params=_params(("arbitrary", "arbitrary")),
    )(dy_halves[0], dy_halves[1], w, x, g.reshape(1, -1), dres)
    return dx, jnp.sum(dg, axis=0)


def _ffn_in_act(n, w_in, *, name, bm=512):
    m, kdim = n.shape
    jn, _, ns = w_in.shape
    f = jn * ns // 2
    bm = _blk(m, bm)
    bn = _blk(ns, WIDE)
    npj = ns // bn
    nf = f // bn

    def body(n_ref, wg_ref, wu_ref, g_ref, u_ref, a_ref):
        nv = n_ref[...]
        g = jnp.dot(nv, wg_ref[...], preferred_element_type=F32)
        u = jnp.dot(nv, wu_ref[...], preferred_element_type=F32)
        g_ref[...] = g.astype(BF16)
        u_ref[...] = u.astype(BF16)
        a_ref[...] = (g * jax.nn.sigmoid(g) * u).astype(BF16)

    out = jax.ShapeDtypeStruct((m, f), BF16)
    ospec = pl.BlockSpec((bm, bn), lambda c, i: (i, c))
    return pl.pallas_call(
        body, name=name, grid=(nf, m // bm),
        in_specs=[pl.BlockSpec((bm, kdim), lambda c, i: (i, 0)),
                  pl.BlockSpec((None, kdim, bn), lambda c, i: (c // npj, 0, c % npj)),
                  pl.BlockSpec((None, kdim, bn), lambda c, i: ((c + nf) // npj, 0, (c + nf) % npj))],
        out_specs=[ospec, ospec, ospec], out_shape=[out, out, out],
        compiler_params=_params(("parallel", "parallel")),
    )(n, w_in, w_in)


def _ffn_out_dx_act(dh, w_out, gate, up, *, name, scale, bm=512):
    m, d = dh.shape
    f = w_out.shape[1]
    bm = _blk(m, bm)
    bn = _blk(f, WIDE)

    def body(dh_ref, w_ref, g_ref, u_ref, dg_ref, du_ref):
        da = lax.dot_general(dh_ref[...].astype(BF16), w_ref[...], _NT, preferred_element_type=F32) * scale
        g = g_ref[...].astype(F32)
        sg = jax.nn.sigmoid(g)
        silu = g * sg
        dg_ref[...] = (da * u_ref[...].astype(F32) * (sg + silu * (1.0 - sg))).astype(BF16)
        du_ref[...] = (da * silu).astype(BF16)

    out = jax.ShapeDtypeStruct((m, f), BF16)
    spec = pl.BlockSpec((bm, bn), lambda i, c: (i, c))
    return pl.pallas_call(
        body, name=name, grid=(m // bm, f // bn),
        in_specs=[pl.BlockSpec((bm, d), lambda i, c: (i, 0)), pl.BlockSpec((None, bn, d), lambda i, c: (0, c, 0)),
                  spec, spec],
        out_specs=[spec, spec], out_shape=[out, out],
        compiler_params=_params(("parallel", "parallel")),
    )(dh, w_out, gate, up)


def _rms_fwd(x, g, *, name):
    t, d = x.shape
    tr = _blk(t, ROW_BLOCK)

    def body(x_ref, g_ref, o_ref):
        xv = x_ref[...]
        rstd = lax.rsqrt(jnp.mean(xv * xv, axis=-1, keepdims=True) + NORM_EPS)
        o_ref[...] = ((xv * rstd) * g_ref[...]).astype(BF16)

    return pl.pallas_call(
        body, name=name, grid=(t // tr,),
        in_specs=[pl.BlockSpec((tr, d), lambda i: (i, 0)), pl.BlockSpec((1, d), lambda i: (0, 0))],
        out_specs=pl.BlockSpec((tr, d), lambda i: (i, 0)),
        out_shape=jax.ShapeDtypeStruct((t, d), BF16),
        compiler_params=_params(("parallel",)),
    )(x, g.reshape(1, d))


def _loss_fwd_bwd(h, target, *, name):
    t, d = h.shape
    tr = _blk(t, ROW_BLOCK)

    def body(h_ref, t_ref, dh_ref, l_ref):
        i = pl.program_id(0)
        err = h_ref[...] - t_ref[...]
        dh_ref[...] = err * (1.0 / d)

        @pl.when(i == 0)
        def _():
            l_ref[...] = jnp.zeros_like(l_ref)

        l_ref[...] += _fold8(err * err)

    dh, part = pl.pallas_call(
        body, name=name, grid=(t // tr,),
        in_specs=[pl.BlockSpec((tr, d), lambda i: (i, 0)), pl.BlockSpec((tr, d), lambda i: (i, 0))],
        out_specs=[pl.BlockSpec((tr, d), lambda i: (i, 0)), pl.BlockSpec((SUBLANES, d), lambda i: (0, 0))],
        out_shape=[jax.ShapeDtypeStruct((t, d), F32), jax.ShapeDtypeStruct((SUBLANES, d), F32)],
        compiler_params=_params(("arbitrary",)),
    )(h, target)
    return jnp.sum(part) * (0.5 / d), dh


def _seg_matrix():
    r = lax.broadcasted_iota(jnp.int32, (LANES, LANES), 0) // HEAD_DIM
    c = lax.broadcasted_iota(jnp.int32, (LANES, LANES), 1) // HEAD_DIM
    return (r == c).astype(BF16)


def _head_sum(x, seg, terms=3):
    hi = x.astype(BF16)
    r1 = x - hi.astype(F32)
    mid = r1.astype(BF16)
    dot = functools.partial(jnp.dot, preferred_element_type=F32)
    if terms == 2:
        return dot(hi, seg) + dot(mid, seg)
    lo = (r1 - mid.astype(F32)).astype(BF16)
    return dot(hi, seg) + dot(mid, seg) + dot(lo, seg)


def _lane_in_head(shape):
    return lax.broadcasted_iota(jnp.int32, shape, 1) % HEAD_DIM


def _half_mask(shape):
    return lax.broadcasted_iota(jnp.int32, shape, 1) < HEAD_DIM


def _rot_partner(x):
    up = pltpu.roll(x, LANES - ROT_DIM // 2, 1)
    down = pltpu.roll(x, ROT_DIM // 2, 1)
    return jnp.where(_lane_in_head(x.shape) < ROT_DIM // 2, up, down)


def _rope_tables(positions):
    inv_freq = ROPE_THETA ** (-jnp.arange(0, ROT_DIM, 2, dtype=F32) / ROT_DIM)
    ang = positions.astype(F32)[:, None] * inv_freq
    t = ang.shape[0]
    rest = HEAD_DIM - ROT_DIM
    cos = jnp.concatenate([jnp.cos(ang), jnp.cos(ang), jnp.ones((t, rest), F32)], axis=1)
    sin = jnp.concatenate([-jnp.sin(ang), jnp.sin(ang), jnp.zeros((t, rest), F32)], axis=1)
    return jnp.tile(cos, (1, LANES // HEAD_DIM)), jnp.tile(sin, (1, LANES // HEAD_DIM))


def _kind_is(j, kinds, kind):
    hits = [j == jj for jj, k in enumerate(kinds) if k == kind]
    return functools.reduce(jnp.logical_or, hits) if hits else None


def _hn_fwd(x, gains, kinds, d, cos, sin, *, name, col0=0):
    t = x.shape[0]
    n = len(kinds)
    tr = _blk(t, ROW_BLOCK)
    seg = _seg_matrix()
    g8 = jnp.repeat(gains.astype(F32), SUBLANES, axis=0)

    def body(x_ref, g_ref, seg_ref, cos_ref, sin_ref, o_ref):
        j = pl.program_id(1)

        def normed(rope):
            for c in range(d // LANES):
                sl = slice(c * LANES, (c + 1) * LANES)
                xv = x_ref[:, sl]
                ms = _head_sum(xv * xv, seg_ref[...], terms=2) * (1.0 / HEAD_DIM)
                y = (xv * lax.rsqrt(ms + NORM_EPS)) * g_ref[0:1, sl]
                if rope:
                    y = y * cos_ref[...] + _rot_partner(y) * sin_ref[...]
                o_ref[:, sl] = y.astype(BF16)

        for kind in ("rope", "norm"):
            hit = _kind_is(j, kinds, kind)
            if hit is not None:
                pl.when(hit)(functools.partial(normed, kind == "rope"))
        hit = _kind_is(j, kinds, "cast")
        if hit is not None:
            @pl.when(hit)
            def _():
                o_ref[...] = x_ref[...].astype(BF16)

    return pl.pallas_call(
        body, name=name, grid=(t // tr, n),
        in_specs=[pl.BlockSpec((tr, d), lambda i, j: (i, col0 + j)), pl.BlockSpec((SUBLANES, d), lambda i, j: (j, 0)),
                  pl.BlockSpec((LANES, LANES), lambda i, j: (0, 0)),
                  pl.BlockSpec((tr, LANES), lambda i, j: (i, 0)), pl.BlockSpec((tr, LANES), lambda i, j: (i, 0))],
        out_specs=pl.BlockSpec((tr, d), lambda i, j: (i, j)),
        out_shape=jax.ShapeDtypeStruct((t, n * d), BF16),
        compiler_params=_params(("parallel", "parallel")),
    )(x, g8, seg, cos, sin)


def _hn_bwd(x, dys, gains, kinds, d, cos, sin, *, name, col0=0):
    t = x.shape[0]
    n = len(kinds)
    tr = _blk(t, ROW_BLOCK // 2)
    seg = _seg_matrix()
    g8 = jnp.repeat(gains.astype(F32), SUBLANES, axis=0)

    def body(x_ref, *refs):
        dy_refs = refs[:n]
        g_ref, seg_ref, cos_ref, sin_ref, dx_ref, dg_ref = refs[n:]
        j = pl.program_id(0)
        i = pl.program_id(1)

        @pl.when(i == 0)
        def _():
            dg_ref[...] = jnp.zeros_like(dg_ref)

        def normed(rope, dy_ref):
            for c in range(d // LANES):
                sl = slice(c * LANES, (c + 1) * LANES)
                xv = x_ref[:, sl]
                dyv = dy_ref[:, sl]
                if rope:
                    dyv = dyv * cos_ref[...] - _rot_partner(dyv) * sin_ref[...]
                ms = _head_sum(xv * xv, seg_ref[...], terms=2) * (1.0 / HEAD_DIM)
                rstd = lax.rsqrt(ms + NORM_EPS)
                xhat = xv * rstd
                dg_ref[:, sl] += _fold8(dyv * xhat)
                dyg = dyv * g_ref[0:1, sl]
                proj = _head_sum(dyg * xhat, seg_ref[...], terms=2) * (1.0 / HEAD_DIM)
                dx_ref[:, sl] = (rstd * (dyg - xhat * proj)).astype(BF16)

        def cast(dy_ref):
            dx_ref[...] = dy_ref[...].astype(BF16)

        for jj, kind in enumerate(kinds):
            if kind == "cast":
                pl.when(j == jj)(functools.partial(cast, dy_refs[jj]))
            else:
                pl.when(j == jj)(functools.partial(normed, kind == "rope", dy_refs[jj]))

    def dy_spec(jj):
        return pl.BlockSpec((tr, d), lambda j, i: (jnp.where(j == jj, i, 0), 0))

    dx, dg = pl.pallas_call(
        body, name=name, grid=(n, t // tr),
        in_specs=[pl.BlockSpec((tr, d), lambda j, i: (i, col0 + j))] + [dy_spec(jj) for jj in range(n)] + [
                  pl.BlockSpec((SUBLANES, d), lambda j, i: (j, 0)),
                  pl.BlockSpec((LANES, LANES), lambda j, i: (0, 0)),
                  pl.BlockSpec((tr, LANES), lambda j, i: (i, 0)), pl.BlockSpec((tr, LANES), lambda j, i: (i, 0))],
        out_specs=[pl.BlockSpec((tr, d), lambda j, i: (i, j)), pl.BlockSpec((SUBLANES, d), lambda j, i: (j, 0))],
        out_shape=[jax.ShapeDtypeStruct((t, n * d), BF16), jax.ShapeDtypeStruct((n * SUBLANES, d), F32)],
        compiler_params=_params(("arbitrary", "arbitrary")),
    )(x, *dys, g8, seg, cos, sin)
    dg = dg.reshape(n, SUBLANES, d // HEAD_DIM, HEAD_DIM).sum(axis=(1, 2))
    return dx, dg


def _band_valid_t(first):
    s = lax.broadcasted_iota(jnp.int32, (2 * BAND, BAND), 0)
    t = lax.broadcasted_iota(jnp.int32, (2 * BAND, BAND), 1)
    dist = t + BAND - s
    return (dist >= 0) & (dist <= BAND) & ((s >= BAND) | jnp.logical_not(first))


def _to_classes_t(z, dil, width):
    return z.reshape(z.shape[0] // dil, dil, width).transpose(1, 2, 0)


def _from_classes_t(z):
    dil, width, rows = z.shape
    return z.transpose(2, 0, 1).reshape(rows * dil, width)


def _band_fwd_n(nat, dil, cfg, *, name):
    d, hh = cfg.d_model, cfg.heads
    rows = cfg.tokens // dil
    nbt = rows // BAND
    nb = cfg.seq // (dil * BAND)

    def body(q_ref, kp_ref, kc_ref, vp_ref, vc_ref, o_ref, lse_ref):
        i = pl.program_id(1)
        valid = _band_valid_t(i % nb == 0)
        upper = lax.broadcasted_iota(jnp.int32, (LANES, BAND), 0) < HEAD_DIM
        for hp in range(d // LANES):
            pair = slice(hp * LANES, (hp + 1) * LANES)
            qt2 = q_ref[:, pair].T
            kk = jnp.concatenate([kp_ref[:, pair], kc_ref[:, pair]], axis=0)
            vvt = jnp.concatenate([vp_ref[:, pair], vc_ref[:, pair]], axis=0).T
            outs = []
            for e in range(2):
                h = 2 * hp + e
                qte = jnp.where(upper == (e == 0), qt2, jnp.zeros_like(qt2))
                s = jnp.where(valid, jnp.dot(kk, qte, preferred_element_type=F32), NEG)
                m = jnp.max(s, axis=0, keepdims=True)
                p = jnp.exp(s - m)
                l = jnp.sum(p, axis=0, keepdims=True)
                hi = p.astype(BF16)
                lo = (p - hi.astype(F32)).astype(BF16)
                vt = vvt[e * HEAD_DIM:(e + 1) * HEAD_DIM]
                o = jnp.dot(vt, hi, preferred_element_type=F32) + jnp.dot(vt, lo, preferred_element_type=F32)
                outs.append(o * (1.0 / l))
                lse_ref[h:h + 1, :] = m + jnp.log(l)
            o_ref[:, pair] = jnp.concatenate(outs, axis=0).T

    def prev(i):
        return jnp.maximum(i - 1, 0)

    blk = (BAND, d)
    return pl.pallas_call(
        body, name=name, grid=(dil, nbt),
        in_specs=[pl.BlockSpec(blk, lambda r, i: (i, r * 3)),
                  pl.BlockSpec(blk, lambda r, i: (prev(i), r * 3 + 1)),
                  pl.BlockSpec(blk, lambda r, i: (i, r * 3 + 1)),
                  pl.BlockSpec(blk, lambda r, i: (prev(i), r * 3 + 2)),
                  pl.BlockSpec(blk, lambda r, i: (i, r * 3 + 2))],
        out_specs=[pl.BlockSpec(blk, lambda r, i: (i, r)),
                   pl.BlockSpec((None, hh, BAND), lambda r, i: (r, 0, i))],
        out_shape=[jax.ShapeDtypeStruct((rows, dil * d), F32), jax.ShapeDtypeStruct((dil, hh, rows), F32)],
        compiler_params=_params(("parallel", "arbitrary")),
    )(nat, nat, nat, nat, nat)


def _band_bwd_n(nat, do_nat, lse_c, dsum_c, dil, cfg, *, name):
    d, hh = cfg.d_model, cfg.heads
    rows = cfg.tokens // dil
    nbt = rows // BAND
    nb = cfg.seq // (dil * BAND)

    def body(q_ref, kp_ref, kc_ref, vp_ref, vc_ref, do_ref, l_ref, ds_ref, dq_ref, dk_ref, dv_ref, ck_ref, cv_ref):
        i = pl.program_id(1)

        @pl.when(i < nbt)
        def _():
            @pl.when(i == 0)
            def _():
                ck_ref[...] = jnp.zeros_like(ck_ref)
                cv_ref[...] = jnp.zeros_like(cv_ref)

            valid1 = _band_valid_t(i % nb == 0)
            valid = jnp.concatenate([valid1, valid1], axis=1)
            upper = lax.broadcasted_iota(jnp.int32, (LANES, BAND), 0) < HEAD_DIM
            half2 = _half_mask((2 * BAND, LANES))

            def both(z):
                zero = jnp.zeros_like(z)
                return jnp.concatenate([jnp.where(upper, z, zero), jnp.where(upper, zero, z)], axis=1)

            def stack(z):
                return jnp.concatenate([z[:, :BAND], z[:, BAND:]], axis=0)

            for hp in range(d // LANES):
                pair = slice(hp * LANES, (hp + 1) * LANES)
                h0, h1 = 2 * hp, 2 * hp + 1
                qn2, don2 = q_ref[:, pair], do_ref[:, pair]
                kk = jnp.concatenate([kp_ref[:, pair], kc_ref[:, pair]], axis=0)
                vv = jnp.concatenate([vp_ref[:, pair], vc_ref[:, pair]], axis=0)
                lse2 = jnp.concatenate([l_ref[h0:h0 + 1, :], l_ref[h1:h1 + 1, :]], axis=1)
                dsum2 = jnp.concatenate([ds_ref[h0:h0 + 1, :], ds_ref[h1:h1 + 1, :]], axis=1)
                s = jnp.where(valid, jnp.dot(kk, both(qn2.T), preferred_element_type=F32), NEG)
                p = jnp.exp(s - lse2)
                dp = jnp.dot(vv, both(don2.T), preferred_element_type=F32)
                dsb = (p * (dp - dsum2)).astype(BF16)
                dq2 = jnp.dot(kk.T, dsb, preferred_element_type=F32)
                dq_ref[:, pair] = jnp.concatenate([dq2[:HEAD_DIM, :BAND], dq2[HEAD_DIM:, BAND:]], axis=0).T
                dk2 = jnp.dot(stack(dsb), qn2, preferred_element_type=F32)
                dv2 = jnp.dot(stack(p.astype(BF16)), don2, preferred_element_type=F32)
                dkk = jnp.where(half2, dk2[:2 * BAND], dk2[2 * BAND:])
                dvv = jnp.where(half2, dv2[:2 * BAND], dv2[2 * BAND:])
                dk_ref[:, pair] = ck_ref[:, pair] + dkk[:BAND]
                dv_ref[:, pair] = cv_ref[:, pair] + dvv[:BAND]
                ck_ref[:, pair] = dkk[BAND:]
                cv_ref[:, pair] = dvv[BAND:]

        @pl.when(i == nbt)
        def _():
            dk_ref[...] = ck_ref[...]
            dv_ref[...] = cv_ref[...]

    def cur(i):
        return jnp.minimum(i, nbt - 1)

    def prev(i):
        return jnp.maximum(cur(i) - 1, 0)

    cblk = (None, hh, BAND)
    blk = (BAND, d)
    here = pl.BlockSpec(blk, lambda r, i: (cur(i), r))
    behind = pl.BlockSpec(blk, lambda r, i: (jnp.maximum(i - 1, 0), r))
    shape = jax.ShapeDtypeStruct((rows, dil * d), F32)
    return pl.pallas_call(
        body, name=name, grid=(dil, nbt + 1),
        in_specs=[pl.BlockSpec(blk, lambda r, i: (cur(i), r * 3)),
                  pl.BlockSpec(blk, lambda r, i: (prev(i), r * 3 + 1)),
                  pl.BlockSpec(blk, lambda r, i: (cur(i), r * 3 + 1)),
                  pl.BlockSpec(blk, lambda r, i: (prev(i), r * 3 + 2)),
                  pl.BlockSpec(blk, lambda r, i: (cur(i), r * 3 + 2)),
                  here,
                  pl.BlockSpec(cblk, lambda r, i: (r, 0, cur(i))),
                  pl.BlockSpec(cblk, lambda r, i: (r, 0, cur(i)))],
        out_specs=[here, behind, behind],
        out_shape=[shape, shape, shape],
        scratch_shapes=[pltpu.VMEM(blk, F32), pltpu.VMEM(blk, F32)],
        compiler_params=_params(("arbitrary", "arbitrary")),
    )(nat, nat, nat, nat, nat, do_nat, lse_c, dsum_c)


def _mix_fwd(outs, lses, *, name):
    t, d = outs[0].shape
    hh = lses[0].shape[1]
    tr = _blk(t, ROW_BLOCK)
    ng = len(outs)
    spread = (lax.broadcasted_iota(jnp.int32, (hh, d), 0)
              == lax.broadcasted_iota(jnp.int32, (hh, d), 1) // HEAD_DIM).astype(BF16)

    def body(*refs):
        o_refs, l_refs = refs[:ng], refs[ng:2 * ng]
        spread_ref, mixed_ref, lse_ref = refs[2 * ng:]
        ls = [r[...] for r in l_refs]
        m = functools.reduce(jnp.maximum, ls)
        es = [jnp.exp(l - m) for l in ls]
        tot = functools.reduce(jnp.add, es)
        inv = 1.0 / tot
        mixed_ref[...] = functools.reduce(
            jnp.add, [_head_sum(e * inv, spread_ref[...]) * r[...] for e, r in zip(es, o_refs)])
        lse_ref[...] = m + jnp.log(tot)

    spec = pl.BlockSpec((tr, d), lambda i: (i, 0))
    cspec = pl.BlockSpec((tr, hh), lambda i: (i, 0))
    return pl.pallas_call(
        body, name=name, grid=(t // tr,),
        in_specs=[spec] * ng + [cspec] * ng + [pl.BlockSpec((hh, d), lambda i: (0, 0))], out_specs=[spec, cspec],
        out_shape=[jax.ShapeDtypeStruct((t, d), F32), jax.ShapeDtypeStruct((t, hh), F32)],
        compiler_params=_params(("parallel",)),
    )(*outs, *lses, spread)


GATE_BLOCK = 256


def _tri(n, upper):
    r = lax.broadcasted_iota(jnp.int32, (n, n), 0)
    c = lax.broadcasted_iota(jnp.int32, (n, n), 1)
    return ((c >= r) if upper else (c <= r)).astype(BF16)


def _tri_dot(tri, x):
    hi = x.astype(BF16)
    r1 = x - hi.astype(F32)
    mid = r1.astype(BF16)
    lo = (r1 - mid.astype(F32)).astype(BF16)
    dot = functools.partial(jnp.dot, preferred_element_type=F32)
    return dot(tri, hi) + dot(tri, mid) + dot(tri, lo)


def _log_sigmoid(z):
    return jnp.minimum(z, 0.0) - jnp.log(1.0 + jnp.exp(-jnp.abs(z)))


def _gate_fwd(proj, col_block, bias, cfg, *, name):
    tr = _blk(cfg.seq, GATE_BLOCK)
    nblk = cfg.seq // tr

    def body(z_ref, b_ref, tri_ref, o_ref, carry_ref):
        i = pl.program_id(1)

        @pl.when(i == 0)
        def _():
            carry_ref[...] = jnp.zeros_like(carry_ref)

        logf = _log_sigmoid(z_ref[...] + b_ref[0:1, :])
        cum = _tri_dot(tri_ref[...], logf) + carry_ref[0:1, :]
        o_ref[...] = cum
        carry_ref[...] = jnp.broadcast_to(cum[tr - 1:tr, :], carry_ref.shape)

    return pl.pallas_call(
        body, name=name, grid=(cfg.batch, nblk),
        in_specs=[pl.BlockSpec((tr, LANES), lambda b, i: (b * nblk + i, col_block)),
                  pl.BlockSpec((SUBLANES, LANES), lambda b, i: (0, 0)),
                  pl.BlockSpec((tr, tr), lambda b, i: (0, 0))],
        out_specs=pl.BlockSpec((tr, LANES), lambda b, i: (b * nblk + i, 0)),
        out_shape=jax.ShapeDtypeStruct((cfg.tokens, LANES), F32),
        scratch_shapes=[pltpu.VMEM((SUBLANES, LANES), F32)],
        compiler_params=_params(("arbitrary", "arbitrary")),
    )(proj, jnp.broadcast_to(bias, (SUBLANES, LANES)), _tri(tr, upper=False))


def _gate_bwd(proj, col_block, bias, dcum, cfg, *, name):
    tr = _blk(cfg.seq, GATE_BLOCK)
    nblk = cfg.seq // tr

    def body(z_ref, b_ref, tri_ref, dc_ref, dz_ref, db_ref, carry_ref):
        b = pl.program_id(0)
        i = pl.program_id(1)

        @pl.when(i == 0)
        def _():
            carry_ref[...] = jnp.zeros_like(carry_ref)

        @pl.when((i == 0) & (b == 0))
        def _():
            db_ref[...] = jnp.zeros_like(db_ref)

        dcv = dc_ref[...]
        dlogf = _tri_dot(tri_ref[...], dcv) + carry_ref[0:1, :]
        carry_ref[...] = jnp.broadcast_to(dlogf[0:1, :], carry_ref.shape)
        dz = dlogf * jax.nn.sigmoid(-(z_ref[...] + b_ref[0:1, :]))
        dz_ref[...] = dz
        db_ref[...] += _fold8(dz)

    def rev(b, i):
        return (b * nblk + nblk - 1 - i, 0)

    dz, db = pl.pallas_call(
        body, name=name, grid=(cfg.batch, nblk),
        in_specs=[pl.BlockSpec((tr, LANES), lambda b, i: (b * nblk + nblk - 1 - i, col_block)),
                  pl.BlockSpec((SUBLANES, LANES), lambda b, i: (0, 0)),
                  pl.BlockSpec((tr, tr), lambda b, i: (0, 0)),
                  pl.BlockSpec((tr, LANES), rev)],
        out_specs=[pl.BlockSpec((tr, LANES), rev), pl.BlockSpec((SUBLANES, LANES), lambda b, i: (0, 0))],
        out_shape=[jax.ShapeDtypeStruct((cfg.tokens, LANES), F32), jax.ShapeDtypeStruct((SUBLANES, LANES), F32)],
        scratch_shapes=[pltpu.VMEM((SUBLANES, LANES), F32)],
        compiler_params=_params(("arbitrary", "arbitrary")),
    )(proj, jnp.broadcast_to(bias, (SUBLANES, LANES)), _tri(tr, upper=True), dcum)
    return dz, jnp.sum(db, axis=0)


FOX_BLOCK = 256
AUG = LANES
BIAS_TERMS = 3


def _fox_aug_k_call(kv, cum, cfg, *, name):
    t, d, hh = cfg.tokens, cfg.d_model, cfg.heads
    tr = _blk(t, ROW_BLOCK)

    def body(k_ref, c_ref, o_ref):
        lane = lax.broadcasted_iota(jnp.int32, (tr, LANES), 1)
        for hp in range(hh // 2):
            k2 = k_ref[:, hp * LANES:(hp + 1) * LANES].astype(F32)
            for e in range(2):
                h = 2 * hp + e
                kh = k2 if e == 0 else pltpu.roll(k2, HEAD_DIM, 1)
                c = -c_ref[:, h:h + 1]
                hi = c.astype(BF16).astype(F32)
                mid = (c - hi).astype(BF16).astype(F32)
                lo = c - hi - mid
                bias = jnp.where(lane == HEAD_DIM, hi, jnp.where(lane == HEAD_DIM + 1, mid,
                                 jnp.where(lane == HEAD_DIM + 2, lo, 0.0)))
                o_ref[:, h * AUG:(h + 1) * AUG] = jnp.where(lane < HEAD_DIM, kh, bias).astype(BF16)

    return pl.pallas_call(
        body, name=name, grid=(t // tr,),
        in_specs=[pl.BlockSpec((tr, d), lambda i: (i, 0)), pl.BlockSpec((tr, LANES), lambda i: (i, 0))],
        out_specs=pl.BlockSpec((tr, hh * AUG), lambda i: (i, 0)),
        out_shape=jax.ShapeDtypeStruct((t, hh * AUG), BF16),
        compiler_params=_params(("parallel",)),
    )(kv, cum)


def _keys_visible(tq):
    s = lax.broadcasted_iota(jnp.int32, (tq, tq), 0)
    t = lax.broadcasted_iota(jnp.int32, (tq, tq), 1)
    return s <= t


def _aug_q_t(q2, e, tq):
    ones = (lax.broadcasted_iota(jnp.int32, (AUG - HEAD_DIM, tq), 0) < BIAS_TERMS).astype(q2.dtype)
    return jnp.concatenate([q2[e * HEAD_DIM:(e + 1) * HEAD_DIM], ones], axis=0)


def _fox_fwd_n(q, k_aug, kv, cfg, *, name):
    t, d, hh = cfg.tokens, cfg.d_model, cfg.heads
    tq = _blk(cfg.seq, FOX_BLOCK)
    nq = cfg.seq // tq

    def body(q_ref, ka_ref, v_ref, o_ref, lse_ref, qa_ref, m_ref, l_ref, acc_ref):
        qi, ki = pl.program_id(1), pl.program_id(2)

        @pl.when(ki == 0)
        def _():
            m_ref[...] = jnp.full_like(m_ref, NEG)
            l_ref[...] = jnp.zeros_like(l_ref)
            acc_ref[...] = jnp.zeros_like(acc_ref)
            for hp in range(hh // 2):
                q2 = q_ref[:, hp * LANES:(hp + 1) * LANES].T
                for e in range(2):
                    h = 2 * hp + e
                    qa_ref[h * AUG:(h + 1) * AUG, :] = _aug_q_t(q2, e, tq)

        def step(diagonal):
            for hp in range(hh // 2):
                vt2 = v_ref[:, hp * LANES:(hp + 1) * LANES].T
                for e in range(2):
                    h = 2 * hp + e
                    rows = slice(h * HEAD_DIM, (h + 1) * HEAD_DIM)
                    s = jnp.dot(ka_ref[:, h * AUG:(h + 1) * AUG], qa_ref[h * AUG:(h + 1) * AUG, :],
                                preferred_element_type=F32)
                    if diagonal:
                        s = jnp.where(_keys_visible(tq), s, NEG)
                    m_prev = m_ref[h:h + 1, :]
                    m_new = jnp.maximum(m_prev, jnp.max(s, axis=0, keepdims=True))
                    alpha = jnp.exp(m_prev - m_new)
                    p = jnp.exp(s - m_new)
                    l_ref[h:h + 1, :] = alpha * l_ref[h:h + 1, :] + jnp.sum(p, axis=0, keepdims=True)
                    m_ref[h:h + 1, :] = m_new
                    hi = p.astype(BF16)
                    lo = (p - hi.astype(F32)).astype(BF16)
                    vt = vt2[e * HEAD_DIM:(e + 1) * HEAD_DIM]
                    acc_ref[rows, :] = (alpha * acc_ref[rows, :] + jnp.dot(vt, hi, preferred_element_type=F32)
                                        + jnp.dot(vt, lo, preferred_element_type=F32))

        pl.when(ki < qi)(functools.partial(step, False))
        pl.when(ki == qi)(functools.partial(step, True))

        @pl.when(ki == qi)
        def _():
            for hp in range(hh // 2):
                halves = [acc_ref[h * HEAD_DIM:(h + 1) * HEAD_DIM, :] * (1.0 / l_ref[h:h + 1, :])
                          for h in (2 * hp, 2 * hp + 1)]
                o_ref[:, hp * LANES:(hp + 1) * LANES] = jnp.concatenate(halves, axis=0).T
            lse_ref[...] = m_ref[...] + jnp.log(l_ref[...])

    def qrow(b, qi, ki):
        return (b * nq + qi, 0)

    return pl.pallas_call(
        body, name=name, grid=(cfg.batch, nq, nq),
        in_specs=[pl.BlockSpec((tq, d), qrow),
                  pl.BlockSpec((tq, hh * AUG), lambda b, qi, ki: (b * nq + jnp.minimum(ki, qi), 0)),
                  pl.BlockSpec((tq, d), lambda b, qi, ki: (b * nq + jnp.minimum(ki, qi), 1))],
        out_specs=[pl.BlockSpec((tq, d), qrow), pl.BlockSpec((hh, tq), lambda b, qi, ki: (0, b * nq + qi))],
        out_shape=[jax.ShapeDtypeStruct((t, d), F32), jax.ShapeDtypeStruct((hh, t), F32)],
        scratch_shapes=[pltpu.VMEM((hh * AUG, tq), BF16), pltpu.VMEM((hh, tq), F32), pltpu.VMEM((hh, tq), F32),
                        pltpu.VMEM((d, tq), F32)],
        compiler_params=_params(("parallel", "parallel", "arbitrary")),
    )(q, k_aug, kv)


def _head_dot_c(a, b, cfg, *, name):
    t, d, hh = cfg.tokens, cfg.d_model, cfg.heads
    tc = _blk(t, ROW_BLOCK)

    def body(a_ref, b_ref, o_ref):
        for hp in range(hh // 2):
            pair = slice(hp * LANES, (hp + 1) * LANES)
            prod = (a_ref[:, pair].astype(F32) * b_ref[:, pair]).T
            for e in range(2):
                h = 2 * hp + e
                o_ref[h:h + 1, :] = jnp.sum(prod[e * HEAD_DIM:(e + 1) * HEAD_DIM], axis=0, keepdims=True)

    return pl.pallas_call(
        body, name=name, grid=(t // tc,),
        in_specs=[pl.BlockSpec((tc, d), lambda i: (i, 0)), pl.BlockSpec((tc, d), lambda i: (i, 0))],
        out_specs=pl.BlockSpec((hh, tc), lambda i: (0, i)),
        out_shape=jax.ShapeDtypeStruct((hh, t), F32),
        compiler_params=_params(("parallel",)),
    )(a, b)


def _fox_bwd_n(q, k_aug, k_t, kv, do, lse, dsum, cfg, *, name):
    t, d, hh = cfg.tokens, cfg.d_model, cfg.heads
    tq = _blk(cfg.seq, FOX_BLOCK)
    nq = cfg.seq // tq

    def body(q_ref, ka_ref, kt_ref, v_ref, do_ref, lse_ref, ds_ref, dq_hbm, dk_ref, dv_ref, dc_ref, dq_acc, sem):
        b, ki, qi = pl.program_id(0), pl.program_id(1), pl.program_id(2)
        qq = jnp.maximum(qi, ki)

        @pl.when((ki == 0) & (qi == 0))
        def _():
            dq_acc[...] = jnp.zeros_like(dq_acc)

        @pl.when(qi == 0)
        def _():
            dk_ref[...] = jnp.zeros_like(dk_ref)
            dv_ref[...] = jnp.zeros_like(dv_ref)
            dc_ref[...] = jnp.zeros_like(dc_ref)

        def step(diagonal):
            upper = lax.broadcasted_iota(jnp.int32, (LANES, tq), 0) < HEAD_DIM
            half = _half_mask((tq, LANES))
            for hp in range(hh // 2):
                pair = slice(hp * LANES, (hp + 1) * LANES)
                q2 = q_ref[:, pair].T
                don2 = do_ref[:, pair]
                dot2 = don2.T
                dvs = []
                for e in range(2):
                    h = 2 * hp + e
                    rows = slice(h * HEAD_DIM, (h + 1) * HEAD_DIM)
                    aug = slice(h * AUG, (h + 1) * AUG)
                    qa = _aug_q_t(q2, e, tq)
                    s = jnp.dot(ka_ref[:, aug], qa, preferred_element_type=F32)
                    if diagonal:
                        s = jnp.where(_keys_visible(tq), s, NEG)
                    p = jnp.exp(s - lse_ref[h:h + 1, :])
                    dote = jnp.where(upper == (e == 0), dot2, jnp.zeros_like(dot2))
                    dp = jnp.dot(v_ref[:, pair], dote, preferred_element_type=F32)
                    dsf = p * (dp - ds_ref[h:h + 1, :])
                    dc_ref[:, h:h + 1] -= jnp.sum(dsf, axis=1, keepdims=True)
                    dsc = dsf.astype(BF16)
                    dvs.append(jnp.dot(p.astype(BF16), don2, preferred_element_type=F32))
                    dk_ref[:, aug] += lax.dot_general(dsc, qa, _NT, preferred_element_type=F32)
                    dq_acc[qq, rows, :] += jnp.dot(kt_ref[rows, :], dsc, preferred_element_type=F32)
                dv_ref[:, pair] += jnp.where(half, dvs[0], dvs[1])

        pl.when(qi > ki)(functools.partial(step, False))
        pl.when(qi == ki)(functools.partial(step, True))

        @pl.when((ki == nq - 1) & (qi == nq - 1))
        def _():
            cp = pltpu.make_async_copy(dq_acc, dq_hbm.at[b], sem)
            cp.start()
            cp.wait()

    def qrow(b, ki, qi):
        return (b * nq + jnp.maximum(qi, ki), 0)

    def qcol(b, ki, qi):
        return (0, b * nq + jnp.maximum(qi, ki))

    def krow(b, ki, qi):
        return (b * nq + ki, 0)

    return pl.pallas_call(
        body, name=name, grid=(cfg.batch, nq, nq),
        in_specs=[pl.BlockSpec((tq, d), qrow),
                  pl.BlockSpec((tq, hh * AUG), krow),
                  pl.BlockSpec((d, tq), lambda b, ki, qi: (0, b * nq + ki)),
                  pl.BlockSpec((tq, d), lambda b, ki, qi: (b * nq + ki, 1)),
                  pl.BlockSpec((tq, d), qrow),
                  pl.BlockSpec((hh, tq), qcol), pl.BlockSpec((hh, tq), qcol)],
        out_specs=[pl.BlockSpec(memory_space=pl.ANY), pl.BlockSpec((tq, hh * AUG), krow),
                   pl.BlockSpec((tq, d), krow), pl.BlockSpec((tq, LANES), krow)],
        out_shape=[jax.ShapeDtypeStruct((cfg.batch, nq, d, tq), F32), jax.ShapeDtypeStruct((t, hh * AUG), F32),
                   jax.ShapeDtypeStruct((t, d), F32), jax.ShapeDtypeStruct((t, LANES), F32)],
        scratch_shapes=[pltpu.VMEM((nq, d, tq), F32), pltpu.SemaphoreType.DMA],
        compiler_params=_params(("arbitrary", "arbitrary", "arbitrary")),
    )(q, k_aug, k_t, kv, do, lse, dsum)


def _fwd(a, w, *, name, res=None, scale=1.0, norms=()):
    bm = FUSED_ROWS if norms else 1024
    out = _mm(a, w, form="F", out_dtype=F32, name=name, bm=bm, bn=WIDE, bk=WIDE, res=res, scale=scale, norms=norms)
    return (out[0], list(out[1:])) if norms else out


def _bwd(dy, w, *, name, scale=1.0, norm_bwd=None):
    bm = FUSED_ROWS if norm_bwd is not None else 1024
    return _mm(dy, w, form="B", out_dtype=F32, name=name, bm=bm, bn=WIDE, bk=WIDE, scale=scale, norm_bwd=norm_bwd)


def _wgrad(a, dy, w, *, name, scale=1.0):
    return _mm_grad(a, dy, w.shape[0], name=name, bm=WIDE, bn=WIDE, scale=scale)


def _ffn_fwd(h, n, w_in, w_out, tag, norms=()):
    gate, up, a = _ffn_in_act(n, w_in, name=f"{tag}_in")
    out = _fwd(a, w_out, name=f"{tag}_out", res=h, scale=0.5, norms=norms)
    h_out, normed = out if norms else (out, [])
    return h_out, normed, (n, gate, up, a)


def _ffn_bwd(dh_out, h, g, w_in, w_out, saved, tag):
    n, gate, up, a = saved
    du = _ffn_out_dx_act(dh_out, w_out, gate, up, name=f"{tag}_out_dx", scale=0.5)
    dw_out = _wgrad(a, dh_out, w_out, name=f"{tag}_out_dw", scale=0.5)
    dh, dg = _mm_back2(du, w_in, (h, g, dh_out), name=f"{tag}_in_dx", bk=WIDE)
    dw_in = _mm_grad(n, du, w_in.shape[0], name=f"{tag}_in_dw", bm=WIDE, bn=WIDE)
    return dh, dg, dw_in, dw_out


def _head_gain(g, heads, scale=1.0):
    return jnp.tile(g.astype(F32) * scale, heads)


def _local_step(cfg, x, positions, target, w, s):
    d, hh = cfg.d_model, cfg.heads
    cos, sin = _rope_tables(positions)
    ones = jnp.ones((d,), F32)

    n00 = _rms_fwd(x, s["ffn_norm"][0, 0], name="ffn00_norm")
    h1, (hn_a,), ffn0 = _ffn_fwd(x, n00, w["ffn_w_in"][0][0], w["ffn_w_out"][0][0], "ffn00", [s["mix_norm"][0]])
    qkv = _fwd(hn_a, w["a_w_qkv"], name="a_qkv")
    kinds_a = ["rope", "rope", "cast"] * len(DILATIONS)
    gains_a = jnp.stack([z for g in range(len(DILATIONS)) for z in (
        _head_gain(s["a_q_norm"][g], hh, Q_SCALE), _head_gain(s["a_k_norm"][g], hh), ones)])
    qkvp = [_hn_fwd(qkv, gains_a[3 * g:3 * g + 3], kinds_a[:3], d, cos, sin, name=f"a_qk_norm{g}", col0=3 * g)
            for g in range(len(DILATIONS))]
    lay = [qkvp[g].reshape(cfg.tokens // dil, dil * 3 * d) for g, dil in enumerate(DILATIONS)]
    band = [_band_fwd_n(lay[g], dil, cfg, name=f"a_band{g}") for g, dil in enumerate(DILATIONS)]
    mixed, lse_a = _mix_fwd([o.reshape(cfg.tokens, d) for o, _ in band], [_from_classes_t(l) for _, l in band],
                            name="a_mix")
    h2, (n01,) = _fwd(mixed, w["a_w_o"], name="a_out", res=h1, norms=[s["ffn_norm"][0, 1]])
    h3, (kn, n10), ffn1 = _ffn_fwd(h2, n01, w["ffn_w_in"][0][1], w["ffn_w_out"][0][1], "ffn01",
                                   [s["kv_norm"], s["ffn_norm"][1, 0]])

    proj = _fwd(kn, w["kv_w"], name="kv_proj")
    kinds_kv = ["norm", "cast"]
    gains_kv = jnp.stack([_head_gain(s["kv_k_norm"], hh), ones])
    kvp = _hn_fwd(proj, gains_kv, kinds_kv, d, cos, sin, name="kv_k_norm")
    gate_col = 2 * d // LANES
    bias = jnp.pad(s["kv_b_f"].astype(F32), (0, LANES - hh))
    cum = _gate_fwd(proj, gate_col, bias, cfg, name="kv_gate")
    k_aug = _fox_aug_k_call(kvp, cum, cfg, name="kv_aug")

    h4, (hn_b,), ffn2 = _ffn_fwd(h3, n10, w["ffn_w_in"][1][0], w["ffn_w_out"][1][0], "ffn10", [s["mix_norm"][1]])
    qraw = _fwd(hn_b, w["b_w_q"], name="b_q")
    gains_b = _head_gain(s["b_q_norm"][0], hh, Q_SCALE)[None]
    qp = _hn_fwd(qraw, gains_b, ["norm"], d, cos, sin, name="b_q_norm")
    o_b, lse_b = _fox_fwd_n(qp, k_aug, kvp, cfg, name="b_fox")
    h5, (n11,) = _fwd(o_b, w["b_w_o"], name="b_out", res=h4, norms=[s["ffn_norm"][1, 1]])
    h6, _, ffn3 = _ffn_fwd(h5, n11, w["ffn_w_in"][1][1], w["ffn_w_out"][1][1], "ffn11")

    loss, dh6 = _loss_fwd_bwd(h6, target, name="loss")

    dh5, dg11, dwi11, dwo11 = _ffn_bwd(dh6, h5, s["ffn_norm"][1, 1], w["ffn_w_in"][1][1], w["ffn_w_out"][1][1],
                                       ffn3, "ffn11")
    do_b = _bwd(dh5, w["b_w_o"], name="b_out_dx")
    dw_bo = _wgrad(o_b, dh5, w["b_w_o"], name="b_out_dw")
    do_bf = do_b.astype(BF16)
    dsum_b = _head_dot_c(do_bf, o_b, cfg, name="b_dsum")
    dq4, dk_aug, dv_b, dcum = _fox_bwd_n(qp, k_aug, kvp[:, :d].T, kvp, do_bf, lse_b, dsum_b, cfg, name="b_fox_bwd")
    dq_b = dq4.transpose(0, 1, 3, 2).reshape(cfg.tokens, d)
    dk_b = dk_aug.reshape(cfg.tokens, hh, AUG)[:, :, :HEAD_DIM].reshape(cfg.tokens, d)
    dqraw, dgq = _hn_bwd(qraw, [dq_b], gains_b, ["norm"], d, cos, sin, name="b_q_norm_bwd")
    dh4, dmix1 = _bwd(dqraw, w["b_w_q"], name="b_q_dx", norm_bwd=(h4, s["mix_norm"][1], dh5))
    dw_bq = _wgrad(hn_b, dqraw, w["b_w_q"], name="b_q_dw")
    dh3, dg10, dwi10, dwo10 = _ffn_bwd(dh4, h3, s["ffn_norm"][1, 0], w["ffn_w_in"][1][0], w["ffn_w_out"][1][0],
                                       ffn2, "ffn10")

    dkvraw, dgk = _hn_bwd(proj, [dk_b, dv_b], gains_kv, kinds_kv, d, cos, sin, name="kv_k_norm_bwd")
    dz, dbias = _gate_bwd(proj, gate_col, bias, dcum, cfg, name="kv_gate_bwd")
    pad_cols = w["kv_w"].shape[2] - 2 * d - LANES
    dproj = jnp.concatenate([dkvraw, dz.astype(BF16), jnp.zeros((cfg.tokens, pad_cols), BF16)], axis=1)
    dw_kv = _wgrad(kn, dproj, w["kv_w"], name="kv_proj_dw")
    dh3, dkvn = _bwd(dproj, w["kv_w"], name="kv_proj_dx", norm_bwd=(h3, s["kv_norm"], dh3))

    dh2, dg01, dwi01, dwo01 = _ffn_bwd(dh3, h2, s["ffn_norm"][0, 1], w["ffn_w_in"][0][1], w["ffn_w_out"][0][1],
                                       ffn1, "ffn01")
    dmixed = _bwd(dh2, w["a_w_o"], name="a_out_dx")
    dw_ao = _wgrad(mixed, dh2, w["a_w_o"], name="a_out_dw")
    dmixed_bf = dmixed.astype(BF16)
    dsum_a = _head_dot_c(dmixed_bf, mixed, cfg, name="a_dsum")
    dqkvp = []
    for g, dil in enumerate(DILATIONS):
        dsum_c = dsum_a.reshape(hh, cfg.tokens // dil, dil).transpose(2, 0, 1)
        grads = _band_bwd_n(lay[g], dmixed_bf.reshape(cfg.tokens // dil, dil * d), _to_classes_t(lse_a, dil, hh),
                            dsum_c, dil, cfg, name=f"a_band{g}_bwd")
        dqkvp += [z.reshape(cfg.tokens, d) for z in grads]
    dqkv, dga = _hn_bwd(qkv, dqkvp, gains_a, kinds_a, d, cos, sin, name="a_qk_norm_bwd")
    dh1, dmix0 = _bwd(dqkv, w["a_w_qkv"], name="a_qkv_dx", norm_bwd=(h1, s["mix_norm"][0], dh2))
    dw_qkv = _wgrad(hn_a, dqkv, w["a_w_qkv"], name="a_qkv_dw")
    dx, dg00, dwi00, dwo00 = _ffn_bwd(dh1, x, s["ffn_norm"][0, 0], w["ffn_w_in"][0][0], w["ffn_w_out"][0][0],
                                      ffn0, "ffn00")

    dw = {
        "ffn_w_in": [[dwi00, dwi01], [dwi10, dwi11]],
        "ffn_w_out": [[dwo00, dwo01], [dwo10, dwo11]],
        "a_w_qkv": dw_qkv, "a_w_o": dw_ao, "kv_w": dw_kv, "b_w_q": dw_bq, "b_w_o": dw_bo,
    }
    ds = {
        "ffn_norm": jnp.stack([jnp.stack([dg00, dg01]), jnp.stack([dg10, dg11])]),
        "mix_norm": jnp.stack([dmix0, dmix1]),
        "a_q_norm": jnp.stack([dga[3 * g] for g in range(len(DILATIONS))])[None] * Q_SCALE,
        "a_k_norm": jnp.stack([dga[3 * g + 1] for g in range(len(DILATIONS))])[None],
        "kv_norm": dkvn,
        "kv_b_f": dbias[:hh],
        "kv_k_norm": dgk[0],
        "b_q_norm": dgq * Q_SCALE,
    }
    return loss, dx, dw, ds


MESH_ID = pl.DeviceIdType.MESH
ANY = pl.BlockSpec(memory_space=pl.ANY)
PACK_COLS = 1024
PACK_ROW_ALIGN = 32


def _me():
    return lax.axis_index("x"), lax.axis_index("y"), lax.axis_index("c")


def _other_chips(x, y):
    return [(1 - x, y), (x, 1 - y), (1 - x, 1 - y)]


def _all_gather_small(v, *, name):
    r = v.shape[0]

    def body(v_ref, out_ref, send_sems, recv_sems):
        x, y, c = _me()
        me = 4 * x + 2 * y + c
        out_ref[me] = v_ref[...]
        copies = []
        for k in range(1, N_DEV):
            fx, fy, fc = (k >> 2) & 1, (k >> 1) & 1, k & 1
            peer = (1 - x if fx else x, 1 - y if fy else y, 1 - c if fc else c)
            copies.append(pltpu.make_async_remote_copy(
                src_ref=v_ref, dst_ref=out_ref.at[me], send_sem=send_sems.at[k - 1], recv_sem=recv_sems.at[k - 1],
                device_id=peer, device_id_type=MESH_ID))
        for cp in copies:
            cp.start()
        for cp in copies:
            cp.wait()

    return pl.pallas_call(
        body, name=name,
        in_specs=[pl.BlockSpec(memory_space=pltpu.VMEM)], out_specs=pl.BlockSpec(memory_space=pltpu.VMEM),
        out_shape=jax.ShapeDtypeStruct((N_DEV, r, LANES), v.dtype),
        scratch_shapes=[pltpu.SemaphoreType.DMA((N_DEV - 1,)), pltpu.SemaphoreType.DMA((N_DEV - 1,))],
    )(v)


def _all_gather_chips(v, *, name):
    rh = v.shape[0] // 2

    def body(v_ref, out_ref, send_sems, recv_sems):
        x, y, c = _me()
        j = 2 * x + y
        chips = _other_chips(x, y)

        def half(chip, core):
            return out_ref.at[chip, pl.ds(core * rh, rh)]

        first = [pltpu.make_async_remote_copy(
            src_ref=v_ref.at[pl.ds(c * rh, rh)], dst_ref=half(j, c), send_sem=send_sems.at[k],
            recv_sem=recv_sems.at[k], device_id=(px, py, c), device_id_type=MESH_ID)
            for k, (px, py) in enumerate(chips)]
        for cp in first:
            cp.start()
        passed = [pltpu.make_async_remote_copy(
            src_ref=half(2 * px + py, c), dst_ref=half(2 * px + py, c), send_sem=send_sems.at[3 + k],
            recv_sem=recv_sems.at[3 + k], device_id=(x, y, 1 - c), device_id_type=MESH_ID)
            for k, (px, py) in enumerate(chips)]
        for k in range(len(chips)):
            first[k].wait_recv()
            passed[k].start()
        for k, (px, py) in enumerate(chips):
            pltpu.make_async_remote_copy(
                src_ref=half(2 * px + py, 1 - c), dst_ref=half(2 * px + py, 1 - c), send_sem=send_sems.at[3 + k],
                recv_sem=recv_sems.at[3 + k], device_id=(x, y, 1 - c), device_id_type=MESH_ID).wait_recv()
        for cp in first + passed:
            cp.wait_send()

    return pl.pallas_call(
        body, name=name, in_specs=[ANY], out_specs=ANY,
        out_shape=jax.ShapeDtypeStruct((N_CHIPS,) + v.shape, v.dtype),
        scratch_shapes=[pltpu.SemaphoreType.DMA((2 * (N_CHIPS - 1),)), pltpu.SemaphoreType.DMA((2 * (N_CHIPS - 1),))],
    )(v)


def _swap_halves(g, *, name):
    n, r, cols = g.shape
    rh = r // 2

    def body(g_ref, out_ref, send_sem, recv_sem):
        x, y, c = _me()
        cp = pltpu.make_async_remote_copy(
            src_ref=g_ref.at[:, pl.ds((1 - c) * rh, rh)], dst_ref=out_ref, send_sem=send_sem, recv_sem=recv_sem,
            device_id=(x, y, 1 - c), device_id_type=MESH_ID)
        cp.start()
        cp.wait()

    return pl.pallas_call(
        body, name=name, in_specs=[ANY], out_specs=ANY,
        out_shape=jax.ShapeDtypeStruct((n, rh, cols), g.dtype),
        scratch_shapes=[pltpu.SemaphoreType.DMA, pltpu.SemaphoreType.DMA],
    )(g)


def _scatter_chips(v, *, name):
    def body(v_ref, out_ref, send_sems, recv_sems):
        x, y, c = _me()
        j = 2 * x + y
        copies = [pltpu.make_async_remote_copy(
            src_ref=v_ref.at[2 * px + py], dst_ref=out_ref.at[j], send_sem=send_sems.at[k], recv_sem=recv_sems.at[k],
            device_id=(px, py, c), device_id_type=MESH_ID) for k, (px, py) in enumerate(_other_chips(x, y))]
        for cp in copies:
            cp.start()
        for cp in copies:
            cp.wait()

    return pl.pallas_call(
        body, name=name, in_specs=[ANY], out_specs=ANY,
        out_shape=jax.ShapeDtypeStruct(v.shape, v.dtype),
        scratch_shapes=[pltpu.SemaphoreType.DMA((N_CHIPS - 1,)), pltpu.SemaphoreType.DMA((N_CHIPS - 1,))],
    )(v)


def _join_halves(v, *, name):
    def body(v_ref, out_ref, send_sem, recv_sem):
        x, y, c = _me()
        cp = pltpu.make_async_remote_copy(
            src_ref=v_ref, dst_ref=out_ref.at[c], send_sem=send_sem, recv_sem=recv_sem,
            device_id=(x, y, 1 - c), device_id_type=MESH_ID)
        cp.start()
        cp.wait()

    return pl.pallas_call(
        body, name=name, in_specs=[ANY], out_specs=ANY,
        out_shape=jax.ShapeDtypeStruct((2,) + v.shape, v.dtype),
        scratch_shapes=[pltpu.SemaphoreType.DMA, pltpu.SemaphoreType.DMA],
    )(v)


def _row_blk(rows, want):
    for b in range(min(rows, want) // SUBLANES * SUBLANES, 0, -SUBLANES):
        if rows % b == 0:
            return b
    return rows


def _add_own_half(g, got, *, name):
    n, r, cols = g.shape
    rh = r // 2
    tr = _row_blk(rh, 512)
    nb = rh // tr

    def body(c_ref, g_ref, got_ref, o_ref):
        del c_ref
        o_ref[...] = (g_ref[...] + got_ref[...]).astype(BF16)

    grid_spec = pltpu.PrefetchScalarGridSpec(
        num_scalar_prefetch=1, grid=(n, nb),
        in_specs=[pl.BlockSpec((None, tr, cols), lambda j, i, c: (j, c[0] * nb + i, 0)),
                  pl.BlockSpec((None, tr, cols), lambda j, i, c: (j, i, 0))],
        out_specs=pl.BlockSpec((None, tr, cols), lambda j, i, c: (j, i, 0)))
    return pl.pallas_call(
        body, name=name, grid_spec=grid_spec, out_shape=jax.ShapeDtypeStruct((n, rh, cols), BF16),
        compiler_params=_params(("parallel", "parallel")),
    )(lax.axis_index("c").astype(jnp.int32).reshape(1), g, got)


def _sum_parts(parts, *, name):
    n, r, cols = parts.shape
    tr = _row_blk(r, 512)

    def body(*refs):
        o_ref = refs[n]
        acc = refs[0][...].astype(F32)
        for p_ref in refs[1:n]:
            acc = acc + p_ref[...].astype(F32)
        o_ref[...] = acc

    return pl.pallas_call(
        body, name=name, grid=(r // tr,),
        in_specs=[pl.BlockSpec((None, tr, cols), functools.partial(lambda j, i: (j, i, 0), j)) for j in range(n)],
        out_specs=pl.BlockSpec((tr, cols), lambda i: (i, 0)),
        out_shape=jax.ShapeDtypeStruct((r, cols), F32),
        compiler_params=_params(("parallel",)),
    )(*([parts] * n))


def _adamw(w, m, v, g, *, name):
    shape = w.shape
    cols = shape[-1]
    w2, m2, v2, g2 = (z.reshape(-1, cols) for z in (w, m, v, g))
    rows = w2.shape[0]
    tr = _row_blk(rows, max(SUBLANES, (1 << 20) // (4 * cols)))

    def body(w_ref, m_ref, v_ref, g_ref, d_ref, nm_ref, nv_ref):
        gv = g_ref[...]
        nm = ADAM_B1 * m_ref[...] + (1.0 - ADAM_B1) * gv
        nv = ADAM_B2 * v_ref[...] + (1.0 - ADAM_B2) * jnp.square(gv)
        m_hat = nm / (1.0 - ADAM_B1 ** ADAM_STEP)
        v_hat = nv / (1.0 - ADAM_B2 ** ADAM_STEP)
        d_ref[...] = -ADAM_LR * (m_hat / (jnp.sqrt(v_hat) + ADAM_EPS) + ADAM_WD * w_ref[...])
        nm_ref[...] = nm
        nv_ref[...] = nv

    spec = pl.BlockSpec((tr, cols), lambda i: (i, 0))
    out = jax.ShapeDtypeStruct((rows, cols), F32)
    d, nm, nv = pl.pallas_call(
        body, name=name, grid=(rows // tr,), in_specs=[spec] * 4, out_specs=[spec] * 3, out_shape=[out] * 3,
        compiler_params=_params(("parallel",)),
    )(w2, m2, v2, g2)
    return d.reshape(shape), nm.reshape(shape), nv.reshape(shape)


def _pack_rows(size, cols, align):
    return -(-size // (cols * align)) * align


def _pack(arrs, lead, cols, align, total_align):
    lead_shape = arrs[0].shape[:lead]
    parts = []
    for a in arrs:
        flat = a.reshape(lead_shape + (-1,))
        size = flat.shape[-1]
        rows = _pack_rows(size, cols, align)
        flat = jnp.pad(flat, [(0, 0)] * lead + [(0, rows * cols - size)])
        parts.append(flat.reshape(lead_shape + (rows, cols)))
    total = sum(p.shape[lead] for p in parts)
    extra = -total % total_align
    if extra:
        parts.append(jnp.zeros(lead_shape + (extra, cols), parts[0].dtype))
    return jnp.concatenate(parts, axis=lead)


def _unpack(buf, shapes, lead, cols, align):
    lead_shape = buf.shape[:lead]
    out, row = [], 0
    for shp in shapes:
        size = 1
        for n in shp:
            size *= n
        rows = _pack_rows(size, cols, align)
        piece = lax.slice_in_dim(buf, row, row + rows, axis=lead).reshape(lead_shape + (-1,))
        out.append(piece[..., :size].reshape(lead_shape + tuple(shp)))
        row += rows
    return out


BIG = ("ffn_w_in", "ffn_w_out", "a_w_qkv", "a_w_o", "kv_w", "b_w_q", "b_w_o")
SMALL = ("ffn_norm", "mix_norm", "a_q_norm", "a_k_norm", "kv_norm", "kv_b_f", "kv_k_norm", "b_q_norm")
WEIGHTS = ("ffn_norm", "ffn_w_in", "ffn_w_out", "mix_norm", "a_w_qkv", "a_q_norm", "a_k_norm", "a_w_o",
           "kv_norm", "kv_w", "kv_b_f", "kv_k_norm", "b_w_q", "b_q_norm", "b_w_o")
GATE_PAD = 2 * LANES


def _stack_weights(sh, d):
    depth = sh["ffn_w_in"].shape[1]
    kv = sh["kv_w"].transpose(1, 0, 2).reshape(d, -1)
    kv = jnp.pad(kv, ((0, 0), (0, 2 * d + GATE_PAD - kv.shape[1])))
    return {
        "ffn_w_in": [[sh["ffn_w_in"][:, l, i] for i in range(2)] for l in range(depth)],
        "ffn_w_out": [[sh["ffn_w_out"][:, l, i].reshape(1, -1, d) for i in range(2)] for l in range(depth)],
        "a_w_qkv": sh["a_w_qkv"][:, 0],
        "a_w_o": sh["a_w_o"].reshape(1, d, d),
        "kv_w": kv[None],
        "b_w_q": sh["b_w_q"].reshape(1, d, d),
        "b_w_o": sh["b_w_o"].reshape(1, d, d),
    }


def _unstack_grads(dw, d, heads):
    def rows4(z):
        return z.reshape(N_CHIPS, -1, d)

    kv_cols = 2 * d + heads
    kv = dw["kv_w"][0][:, :kv_cols].reshape(d, N_CHIPS, kv_cols // N_CHIPS).transpose(1, 0, 2)
    return [
        jnp.stack([jnp.stack(row, axis=1) for row in dw["ffn_w_in"]], axis=1),
        jnp.stack([jnp.stack([rows4(z) for z in row], axis=1) for row in dw["ffn_w_out"]], axis=1),
        dw["a_w_qkv"][:, None],
        rows4(dw["a_w_o"])[:, None],
        kv,
        rows4(dw["b_w_q"])[:, None],
        rows4(dw["b_w_o"])[:, None],
    ]


def kernel(x, positions, ffn_norm, ffn_w_in, ffn_w_out, mix_norm, a_w_qkv, a_q_norm, a_k_norm, a_w_o, kv_norm, kv_w, kv_b_f, kv_k_norm, b_w_q, b_q_norm, b_w_o, loss_target, m_ffn_norm, m_ffn_w_in, m_ffn_w_out, m_mix_norm, m_a_w_qkv, m_a_q_norm, m_a_k_norm, m_a_w_o, m_kv_norm, m_kv_w, m_kv_b_f, m_kv_k_norm, m_b_w_q, m_b_q_norm, m_b_w_o, v_ffn_norm, v_ffn_w_in, v_ffn_w_out, v_mix_norm, v_a_w_qkv, v_a_q_norm, v_a_k_norm, v_a_w_o, v_kv_norm, v_kv_w, v_kv_b_f, v_kv_k_norm, v_b_w_q, v_b_q_norm, v_b_w_o):
    wts = dict(zip(WEIGHTS, (ffn_norm, ffn_w_in, ffn_w_out, mix_norm, a_w_qkv, a_q_norm, a_k_norm, a_w_o, kv_norm,
                             kv_w, kv_b_f, kv_k_norm, b_w_q, b_q_norm, b_w_o)))
    mom = dict(zip(WEIGHTS, (m_ffn_norm, m_ffn_w_in, m_ffn_w_out, m_mix_norm, m_a_w_qkv, m_a_q_norm, m_a_k_norm,
                             m_a_w_o, m_kv_norm, m_kv_w, m_kv_b_f, m_kv_k_norm, m_b_w_q, m_b_q_norm, m_b_w_o)))
    var = dict(zip(WEIGHTS, (v_ffn_norm, v_ffn_w_in, v_ffn_w_out, v_mix_norm, v_a_w_qkv, v_a_q_norm, v_a_k_norm,
                             v_a_w_o, v_kv_norm, v_kv_w, v_kv_b_f, v_kv_k_norm, v_b_w_q, v_b_q_norm, v_b_w_o)))
    batch, seq, d = x.shape
    cfg = Cfg(d_model=d, d_ff=ffn_w_out.shape[2] * N_CHIPS, seq=seq, batch=batch)
    chip = 2 * lax.axis_index("x") + lax.axis_index("y")
    big_shapes = [wts[n].shape for n in BIG]

    shard = _pack([wts[n].astype(BF16) for n in BIG], 0, PACK_COLS, PACK_ROW_ALIGN, PACK_COLS)
    gathered = _all_gather_chips(shard, name="gather_weights")
    gathered = lax.dynamic_update_slice_in_dim(gathered, shard[None], chip, axis=0)
    w = _stack_weights(dict(zip(BIG, _unpack(gathered, big_shapes, 1, PACK_COLS, PACK_ROW_ALIGN))), d)
    norm_shard = _pack([ffn_norm], 0, LANES, SUBLANES, SUBLANES)
    norms = _all_gather_small(norm_shard, name="gather_ffn_norm")[0::2]
    (norms,) = _unpack(norms, [ffn_norm.shape], 1, LANES, SUBLANES)
    small = {"ffn_norm": jnp.moveaxis(norms, 0, 2).reshape(ffn_norm.shape[:2] + (d,)),
             "mix_norm": mix_norm, "a_q_norm": a_q_norm[0], "a_k_norm": a_k_norm[0], "kv_norm": kv_norm,
             "kv_b_f": kv_b_f, "kv_k_norm": kv_k_norm, "b_q_norm": b_q_norm}

    loss, dx, dw, ds = _local_step(cfg, x.reshape(cfg.tokens, d), positions.reshape(cfg.tokens),
                                   loss_target.reshape(cfg.tokens, d), w, small)
    loss = lax.psum(loss, ("x", "y", "c"))

    g = _pack(_unstack_grads(dw, d, cfg.heads), 1, PACK_COLS, PACK_ROW_ALIGN, PACK_COLS)
    chip_half = _add_own_half(g, _swap_halves(g, name="swap_halves"), name="add_halves")
    parts = _scatter_chips(chip_half, name="scatter_chips")
    parts = lax.dynamic_update_slice_in_dim(parts, lax.dynamic_slice_in_dim(chip_half, chip, 1, axis=0), chip, axis=0)
    mine = _sum_parts(parts, name="sum_chips")
    both = _join_halves(mine, name="join_halves")
    g_big = lax.dynamic_update_slice_in_dim(both, mine[None], lax.axis_index("c"), axis=0).reshape(g.shape[1:])
    grads = dict(zip(BIG, _unpack(g_big, big_shapes, 0, PACK_COLS, PACK_ROW_ALIGN)))

    small_shapes = [ds[n].shape for n in SMALL]
    parts = _all_gather_small(_pack([ds[n] for n in SMALL], 0, LANES, SUBLANES, SUBLANES), name="gather_small")
    g_small = dict(zip(SMALL, _unpack(_sum_parts(parts, name="sum_small"), small_shapes, 0, LANES, SUBLANES)))
    quarter = d // N_CHIPS
    g_small["ffn_norm"] = lax.dynamic_slice_in_dim(g_small["ffn_norm"], chip * quarter, quarter, axis=2)
    grads.update(g_small)

    delta, new_m, new_v = {}, {}, {}
    for n in BIG:
        delta[n], new_m[n], new_v[n] = _adamw(wts[n], mom[n], var[n], grads[n], name=f"adamw_{n}")
    packed = [_pack([z[n] for n in SMALL], 0, LANES, SUBLANES, SUBLANES) for z in (wts, mom, var, grads)]
    small_out = _adamw(*packed, name="adamw_small")
    shard_shapes = [wts[n].shape for n in SMALL]
    for out, res in zip((delta, new_m, new_v), small_out):
        out.update(zip(SMALL, _unpack(res, shard_shapes, 0, LANES, SUBLANES)))

    return (loss, dx.reshape(x.shape), *[grads[n] for n in WEIGHTS], *[delta[n] for n in WEIGHTS],
            *[new_m[n] for n in WEIGHTS], *[new_v[n] for n in WEIGHTS])
```

```python
import functools
from typing import NamedTuple

import jax
import jax.numpy as jnp
from jax import lax
from jax.experimental import pallas as pl
from jax.experimental.pallas import tpu as pltpu

F32 = jnp.float32
BF16 = jnp.bfloat16

HEAD_DIM = 64
LANES = 128
SUBLANES = 8
ROT_DIM = HEAD_DIM // 4
ROPE_THETA = 500000.0
NORM_EPS = 1e-6
BAND = 128
DILATIONS = (1, 4, 16)
NEG = -1e30
Q_SCALE = HEAD_DIM ** -0.5
N_CHIPS = 4
N_DEV = 8
VMEM_LIMIT = 48 * 1024 * 1024
WIDE = 1536
ROW_BLOCK = 512
FUSED_ROWS = 512
_NT = (((1,), (1,)), ((), ()))

ADAM_LR = 0.001
ADAM_B1 = 0.9
ADAM_B2 = 0.999
ADAM_EPS = 1e-08
ADAM_WD = 0.01
ADAM_STEP = 10


class Cfg(NamedTuple):
    d_model: int
    d_ff: int
    seq: int
    batch: int

    @property
    def heads(self):
        return self.d_model // HEAD_DIM

    @property
    def tokens(self):
        return self.batch * self.seq


def _params(sem):
    return pltpu.CompilerParams(dimension_semantics=sem, vmem_limit_bytes=VMEM_LIMIT)


def _blk(dim, want):
    if dim <= want:
        return dim
    for b in range(want // LANES * LANES, 0, -LANES):
        if dim % b == 0:
            return b
    b = want
    while dim % b:
        b //= 2
    return b


def _fold8(x):
    return jnp.sum(x.reshape(x.shape[0] // SUBLANES, SUBLANES, x.shape[1]), axis=0)


def _rms_bwd_tile(xv, g, dyv, dres):
    rstd = lax.rsqrt(jnp.mean(xv * xv, axis=-1, keepdims=True) + NORM_EPS)
    xhat = xv * rstd
    dyg = dyv * g
    proj = jnp.mean(dyg * xhat, axis=-1, keepdims=True)
    return dres + rstd * (dyg - xhat * proj), _fold8(dyv * xhat)


def _mm(a, b, *, form, out_dtype, name, bm=1024, bn=1024, bk=1024, res=None, scale=1.0, norms=(), norm_bwd=None):
    if form == "F":
        m, kdim = a.shape
        jn, _, ns = b.shape
        bm, bn, bk = _blk(m, bm), _blk(ns, bn), _blk(kdim, bk)
        npj = ns // bn
        grid = (m // bm, jn * npj, kdim // bk)
        a_spec = pl.BlockSpec((bm, bk), lambda i, n, k: (i, k))
        b_spec = pl.BlockSpec((None, bk, bn), lambda i, n, k: (n // npj, k, n % npj))
        o_spec = pl.BlockSpec((bm, bn), lambda i, n, k: (i, n))
        o_shape = jax.ShapeDtypeStruct((m, jn * ns), out_dtype)
        dims = (((1,), (0,)), ((), ()))
    elif form == "B":
        m = a.shape[0]
        jn, kdim, ns = b.shape
        bm, bn, bk = _blk(m, bm), _blk(kdim, bn), _blk(ns, bk)
        kpj = ns // bk
        grid = (m // bm, kdim // bn, jn * kpj)
        a_spec = pl.BlockSpec((bm, bk), lambda i, n, k: (i, k))
        b_spec = pl.BlockSpec((None, bn, bk), lambda i, n, k: (k // kpj, n, k % kpj))
        o_spec = pl.BlockSpec((bm, bn), lambda i, n, k: (i, n))
        o_shape = jax.ShapeDtypeStruct((m, kdim), out_dtype)
        dims = _NT
    else:
        raise ValueError(form)
    nk = grid[2]
    n_norms = len(norms)
    full_rows = grid[1] == 1
    assert full_rows or (not norms and norm_bwd is None)

    def body(*refs):
        a_ref, b_ref = refs[:2]
        pos = 2
        r_ref = None
        if res is not None:
            r_ref = refs[pos]
            pos += 1
        g_refs = refs[pos:pos + n_norms]
        pos += n_norms
        if norm_bwd is not None:
            x_ref, gb_ref, dres_ref = refs[pos:pos + 3]
            pos += 3
        o_ref = refs[pos]
        n_refs = refs[pos + 1:pos + 1 + n_norms]
        acc_ref = refs[-1]
        i, k = pl.program_id(0), pl.program_id(2)

        @pl.when(k == 0)
        def _():
            acc_ref[...] = jnp.zeros_like(acc_ref)

        acc_ref[...] += lax.dot_general(a_ref[...].astype(BF16), b_ref[...].astype(BF16), dims,
                                        preferred_element_type=F32)

        if norm_bwd is not None:
            dg_ref = refs[pos + 1 + n_norms]

            @pl.when((i == 0) & (k == 0))
            def _():
                dg_ref[...] = jnp.zeros_like(dg_ref)

        @pl.when(k == nk - 1)
        def _():
            r = acc_ref[...]
            if scale != 1.0:
                r = r * scale
            if r_ref is not None:
                r = r_ref[...] + r
            if norm_bwd is not None:
                dx, dg8 = _rms_bwd_tile(x_ref[...], gb_ref[...], r, dres_ref[...])
                o_ref[...] = dx
                dg_ref[...] += dg8
            else:
                o_ref[...] = r.astype(o_ref.dtype)
            if n_norms:
                rstd = lax.rsqrt(jnp.mean(r * r, axis=-1, keepdims=True) + NORM_EPS)
                for g_ref, n_ref in zip(g_refs, n_refs):
                    n_ref[...] = ((r * rstd) * g_ref[...]).astype(BF16)

    row = pl.BlockSpec((bm, bn), lambda i, n, k: (i, n))
    vec = pl.BlockSpec((1, bn), lambda i, n, k: (0, 0))
    in_specs = [a_spec, b_spec]
    args = [a, b]
    if res is not None:
        in_specs.append(row)
        args.append(res)
    for g in norms:
        in_specs.append(vec)
        args.append(g.reshape(1, -1))
    out_specs, out_shapes = [o_spec], [o_shape]
    for _ in norms:
        out_specs.append(row)
        out_shapes.append(jax.ShapeDtypeStruct(o_shape.shape, BF16))
    if norm_bwd is not None:
        x, g, dres = norm_bwd
        in_specs += [row, vec, row]
        args += [x, g.reshape(1, -1), dres]
        out_specs.append(pl.BlockSpec((SUBLANES, bn), lambda i, n, k: (0, 0)))
        out_shapes.append(jax.ShapeDtypeStruct((SUBLANES, o_shape.shape[1]), F32))
    sem = ("arbitrary",) * 3 if norm_bwd is not None else ("parallel", "parallel", "arbitrary")
    single = len(out_specs) == 1
    out = pl.pallas_call(
        body, name=name, grid=grid, in_specs=in_specs, out_specs=out_specs[0] if single else out_specs,
        out_shape=out_shapes[0] if single else out_shapes,
        scratch_shapes=[pltpu.VMEM((bm, bn), F32)],
        compiler_params=_params(sem),
    )(*args)
    if norm_bwd is not None:
        return out[0], jnp.sum(out[1], axis=0)
    return out


def _mm_grad(a, dy, jn, *, name, scale=1.0, bm=1024, bn=1024, bk=1024):
    halves = dy if isinstance(dy, (tuple, list)) else (dy,)
    t, kdim = a.shape
    ns = len(halves) * halves[0].shape[1] // jn
    bm, bn, bk = _blk(kdim, bm), _blk(ns, bn), _blk(t, bk)
    npj = ns // bn
    grid = (kdim // bm, jn * npj, t // bk)
    nk = grid[2]
    nhalf = jn * npj // len(halves)
    dims = (((0,), (0,)), ((), ()))

    def body(a_ref, *refs):
        b_refs, o_ref, acc_ref = refs[:len(halves)], refs[-2], refs[-1]
        n, k = pl.program_id(1), pl.program_id(2)

        @pl.when(k == 0)
        def _():
            acc_ref[...] = jnp.zeros_like(acc_ref)

        for which, b_ref in enumerate(b_refs):
            @pl.when(n // nhalf == which)
            def _(b_ref=b_ref):
                acc_ref[...] += lax.dot_general(a_ref[...].astype(BF16), b_ref[...].astype(BF16), dims,
                                                preferred_element_type=F32)

        @pl.when(k == nk - 1)
        def _():
            r = acc_ref[...]
            if scale != 1.0:
                r = r * scale
            o_ref[...] = r

    def half_spec(which):
        return pl.BlockSpec((bk, bn), lambda m, n, k: (jnp.where(n // nhalf == which, k, 0),
                                                        jnp.where(n // nhalf == which, n % nhalf, 0)))

    return pl.pallas_call(
        body, name=name, grid=grid,
        in_specs=[pl.BlockSpec((bk, bm), lambda m, n, k: (k, m))] + [half_spec(w) for w in range(len(halves))],
        out_specs=pl.BlockSpec((None, bm, bn), lambda m, n, k: (n // npj, m, n % npj)),
        out_shape=jax.ShapeDtypeStruct((jn, kdim, ns), F32),
        scratch_shapes=[pltpu.VMEM((bm, bn), F32)],
        compiler_params=_params(("parallel", "parallel", "arbitrary")),
    )(a, *halves)


def _mm_back2(dy_halves, w, norm_bwd, *, name, bm=FUSED_ROWS, bk=1024):
    x, g, dres = norm_bwd
    m = dy_halves[0].shape[0]
    jn, kdim, ns = w.shape
    bm, bk = _blk(m, bm), _blk(ns, bk)
    kpj = ns // bk
    nk = jn * kpj
    khalf = nk // 2

    def body(a0_ref, a1_ref, b_ref, x_ref, g_ref, dres_ref, o_ref, dg_ref, acc_ref):
        i, k = pl.program_id(0), pl.program_id(1)

        @pl.when(k == 0)
        def _():
            acc_ref[...] = jnp.zeros_like(acc_ref)

        @pl.when((i == 0) & (k == 0))
        def _():
            dg_ref[...] = jnp.zeros_like(dg_ref)

        for which, a_ref in enumerate((a0_ref, a1_ref)):
            @pl.when(k // khalf == which)
            def _(a_ref=a_ref):
                acc_ref[...] += lax.dot_general(a_ref[...], b_ref[...], _NT, preferred_element_type=F32)

        @pl.when(k == nk - 1)
        def _():
            dx, dg8 = _rms_bwd_tile(x_ref[...], g_ref[...], acc_ref[...], dres_ref[...])
            o_ref[...] = dx
            dg_ref[...] += dg8

    def half_spec(which):
        return pl.BlockSpec((bm, bk), lambda i, k: (i, jnp.clip(k - which * khalf, 0, khalf - 1)))

    row = pl.BlockSpec((bm, kdim), lambda i, k: (i, 0))
    dx, dg = pl.pallas_call(
        body, name=name, grid=(m // bm, nk),
        in_specs=[half_spec(0), half_spec(1),
                  pl.BlockSpec((None, kdim, bk), lambda i, k: (k // kpj, 0, k % kpj)),
                  row, pl.BlockSpec((1, kdim), lambda i, k: (0, 0)), row],
        out_specs=[row, pl.BlockSpec((SUBLANES, kdim), lambda i, k: (0, 0))],
        out_shape=[jax.ShapeDtypeStruct((m, kdim), F32), jax.ShapeDtypeStruct((SUBLANES, kdim), F32)],
        scratch_shapes=[pltpu.VMEM((bm, kdim), F32)],
        compiler_params=_params(("arbitrary", "arbitrary")),
    )(dy_halves[0], dy_halves[1], w, x, g.reshape(1, -1), dres)
    return dx, jnp.sum(dg, axis=0)


def _ffn_in_act(n, w_in, *, name, bm=512):
    m, kdim = n.shape
    jn, _, ns = w_in.shape
    f = jn * ns // 2
    bm = _blk(m, bm)
    bn = _blk(ns, WIDE)
    npj = ns // bn
    nf = f // bn

    def body(n_ref, wg_ref, wu_ref, g_ref, u_ref, a_ref):
        nv = n_ref[...]
        g = jnp.dot(nv, wg_ref[...], preferred_element_type=F32)
        u = jnp.dot(nv, wu_ref[...], preferred_element_type=F32)
        g_ref[...] = g.astype(BF16)
        u_ref[...] = u.astype(BF16)
        a_ref[...] = (g * jax.nn.sigmoid(g) * u).astype(BF16)

    out = jax.ShapeDtypeStruct((m, f), BF16)
    ospec = pl.BlockSpec((bm, bn), lambda c, i: (i, c))
    return pl.pallas_call(
        body, name=name, grid=(nf, m // bm),
        in_specs=[pl.BlockSpec((bm, kdim), lambda c, i: (i, 0)),
                  pl.BlockSpec((None, kdim, bn), lambda c, i: (c // npj, 0, c % npj)),
                  pl.BlockSpec((None, kdim, bn), lambda c, i: ((c + nf) // npj, 0, (c + nf) % npj))],
        out_specs=[ospec, ospec, ospec], out_shape=[out, out, out],
        compiler_params=_params(("parallel", "parallel")),
    )(n, w_in, w_in)


def _ffn_out_dx_act(dh, w_out, gate, up, *, name, scale, bm=512):
    m, d = dh.shape
    f = w_out.shape[1]
    bm = _blk(m, bm)
    bn = _blk(f, WIDE)

    def body(dh_ref, w_ref, g_ref, u_ref, dg_ref, du_ref):
        da = lax.dot_general(dh_ref[...].astype(BF16), w_ref[...], _NT, preferred_element_type=F32) * scale
        g = g_ref[...].astype(F32)
        sg = jax.nn.sigmoid(g)
        silu = g * sg
        dg_ref[...] = (da * u_ref[...].astype(F32) * (sg + silu * (1.0 - sg))).astype(BF16)
        du_ref[...] = (da * silu).astype(BF16)

    out = jax.ShapeDtypeStruct((m, f), BF16)
    spec = pl.BlockSpec((bm, bn), lambda i, c: (i, c))
    return pl.pallas_call(
        body, name=name, grid=(m // bm, f // bn),
        in_specs=[pl.BlockSpec((bm, d), lambda i, c: (i, 0)), pl.BlockSpec((None, bn, d), lambda i, c: (0, c, 0)),
                  spec, spec],
        out_specs=[spec, spec], out_shape=[out, out],
        compiler_params=_params(("parallel", "parallel")),
    )(dh, w_out, gate, up)


def _rms_fwd(x, g, *, name):
    t, d = x.shape
    tr = _blk(t, ROW_BLOCK)

    def body(x_ref, g_ref, o_ref):
        xv = x_ref[...]
        rstd = lax.rsqrt(jnp.mean(xv * xv, axis=-1, keepdims=True) + NORM_EPS)
        o_ref[...] = ((xv * rstd) * g_ref[...]).astype(BF16)

    return pl.pallas_call(
        body, name=name, grid=(t // tr,),
        in_specs=[pl.BlockSpec((tr, d), lambda i: (i, 0)), pl.BlockSpec((1, d), lambda i: (0, 0))],
        out_specs=pl.BlockSpec((tr, d), lambda i: (i, 0)),
        out_shape=jax.ShapeDtypeStruct((t, d), BF16),
        compiler_params=_params(("parallel",)),
    )(x, g.reshape(1, d))


def _loss_fwd_bwd(h, target, *, name):
    t, d = h.shape
    tr = _blk(t, ROW_BLOCK)

    def body(h_ref, t_ref, dh_ref, l_ref):
        i = pl.program_id(0)
        err = h_ref[...] - t_ref[...]
        dh_ref[...] = err * (1.0 / d)

        @pl.when(i == 0)
        def _():
            l_ref[...] = jnp.zeros_like(l_ref)

        l_ref[...] += _fold8(err * err)

    dh, part = pl.pallas_call(
        body, name=name, grid=(t // tr,),
        in_specs=[pl.BlockSpec((tr, d), lambda i: (i, 0)), pl.BlockSpec((tr, d), lambda i: (i, 0))],
        out_specs=[pl.BlockSpec((tr, d), lambda i: (i, 0)), pl.BlockSpec((SUBLANES, d), lambda i: (0, 0))],
        out_shape=[jax.ShapeDtypeStruct((t, d), F32), jax.ShapeDtypeStruct((SUBLANES, d), F32)],
        compiler_params=_params(("arbitrary",)),
    )(h, target)
    return jnp.sum(part) * (0.5 / d), dh


def _seg_matrix():
    r = lax.broadcasted_iota(jnp.int32, (LANES, LANES), 0) // HEAD_DIM
    c = lax.broadcasted_iota(jnp.int32, (LANES, LANES), 1) // HEAD_DIM
    return (r == c).astype(BF16)


def _head_sum(x, seg, terms=3):
    hi = x.astype(BF16)
    r1 = x - hi.astype(F32)
    mid = r1.astype(BF16)
    dot = functools.partial(jnp.dot, preferred_element_type=F32)
    if terms == 2:
        return dot(hi, seg) + dot(mid, seg)
    lo = (r1 - mid.astype(F32)).astype(BF16)
    return dot(hi, seg) + dot(mid, seg) + dot(lo, seg)


def _lane_in_head(shape):
    return lax.broadcasted_iota(jnp.int32, shape, 1) % HEAD_DIM


def _half_mask(shape):
    return lax.broadcasted_iota(jnp.int32, shape, 1) < HEAD_DIM


def _rot_partner(x):
    up = pltpu.roll(x, LANES - ROT_DIM // 2, 1)
    down = pltpu.roll(x, ROT_DIM // 2, 1)
    return jnp.where(_lane_in_head(x.shape) < ROT_DIM // 2, up, down)


def _rope_tables(positions):
    inv_freq = ROPE_THETA ** (-jnp.arange(0, ROT_DIM, 2, dtype=F32) / ROT_DIM)
    ang = positions.astype(F32)[:, None] * inv_freq
    t = ang.shape[0]
    rest = HEAD_DIM - ROT_DIM
    cos = jnp.concatenate([jnp.cos(ang), jnp.cos(ang), jnp.ones((t, rest), F32)], axis=1)
    sin = jnp.concatenate([-jnp.sin(ang), jnp.sin(ang), jnp.zeros((t, rest), F32)], axis=1)
    return jnp.tile(cos, (1, LANES // HEAD_DIM)), jnp.tile(sin, (1, LANES // HEAD_DIM))


def _kind_is(j, kinds, kind):
    hits = [j == jj for jj, k in enumerate(kinds) if k == kind]
    return functools.reduce(jnp.logical_or, hits) if hits else None


def _hn_fwd(x, gains, kinds, d, cos, sin, *, name, col0=0):
    t = x.shape[0]
    n = len(kinds)
    tr = _blk(t, ROW_BLOCK)
    seg = _seg_matrix()
    g8 = jnp.repeat(gains.astype(F32), SUBLANES, axis=0)

    def body(x_ref, g_ref, seg_ref, cos_ref, sin_ref, o_ref):
        j = pl.program_id(1)

        def normed(rope):
            for c in range(d // LANES):
                sl = slice(c * LANES, (c + 1) * LANES)
                xv = x_ref[:, sl]
                ms = _head_sum(xv * xv, seg_ref[...], terms=2) * (1.0 / HEAD_DIM)
                y = (xv * lax.rsqrt(ms + NORM_EPS)) * g_ref[0:1, sl]
                if rope:
                    y = y * cos_ref[...] + _rot_partner(y) * sin_ref[...]
                o_ref[:, sl] = y.astype(BF16)

        for kind in ("rope", "norm"):
            hit = _kind_is(j, kinds, kind)
            if hit is not None:
                pl.when(hit)(functools.partial(normed, kind == "rope"))
        hit = _kind_is(j, kinds, "cast")
        if hit is not None:
            @pl.when(hit)
            def _():
                o_ref[...] = x_ref[...].astype(BF16)

    return pl.pallas_call(
        body, name=name, grid=(t // tr, n),
        in_specs=[pl.BlockSpec((tr, d), lambda i, j: (i, col0 + j)), pl.BlockSpec((SUBLANES, d), lambda i, j: (j, 0)),
                  pl.BlockSpec((LANES, LANES), lambda i, j: (0, 0)),
                  pl.BlockSpec((tr, LANES), lambda i, j: (i, 0)), pl.BlockSpec((tr, LANES), lambda i, j: (i, 0))],
        out_specs=pl.BlockSpec((tr, d), lambda i, j: (i, j)),
        out_shape=jax.ShapeDtypeStruct((t, n * d), BF16),
        compiler_params=_params(("parallel", "parallel")),
    )(x, g8, seg, cos, sin)


def _hn_bwd(x, dys, gains, kinds, d, cos, sin, *, name, col0=0):
    t = x.shape[0]
    n = len(kinds)
    tr = _blk(t, ROW_BLOCK // 2)
    seg = _seg_matrix()
    g8 = jnp.repeat(gains.astype(F32), SUBLANES, axis=0)

    def body(x_ref, *refs):
        dy_refs = refs[:n]
        g_ref, seg_ref, cos_ref, sin_ref, dx_ref, dg_ref = refs[n:]
        j = pl.program_id(0)
        i = pl.program_id(1)

        @pl.when(i == 0)
        def _():
            dg_ref[...] = jnp.zeros_like(dg_ref)

        def normed(rope, dy_ref):
            for c in range(d // LANES):
                sl = slice(c * LANES, (c + 1) * LANES)
                xv = x_ref[:, sl]
                dyv = dy_ref[:, sl]
                if rope:
                    dyv = dyv * cos_ref[...] - _rot_partner(dyv) * sin_ref[...]
                ms = _head_sum(xv * xv, seg_ref[...], terms=2) * (1.0 / HEAD_DIM)
                rstd = lax.rsqrt(ms + NORM_EPS)
                xhat = xv * rstd
                dg_ref[:, sl] += _fold8(dyv * xhat)
                dyg = dyv * g_ref[0:1, sl]
                proj = _head_sum(dyg * xhat, seg_ref[...], terms=2) * (1.0 / HEAD_DIM)
                dx_ref[:, sl] = (rstd * (dyg - xhat * proj)).astype(BF16)

        def cast(dy_ref):
            dx_ref[...] = dy_ref[...].astype(BF16)

        for jj, kind in enumerate(kinds):
            if kind == "cast":
                pl.when(j == jj)(functools.partial(cast, dy_refs[jj]))
            else:
                pl.when(j == jj)(functools.partial(normed, kind == "rope", dy_refs[jj]))

    def dy_spec(jj):
        return pl.BlockSpec((tr, d), lambda j, i: (jnp.where(j == jj, i, 0), 0))

    dx, dg = pl.pallas_call(
        body, name=name, grid=(n, t // tr),
        in_specs=[pl.BlockSpec((tr, d), lambda j, i: (i, col0 + j))] + [dy_spec(jj) for jj in range(n)] + [
                  pl.BlockSpec((SUBLANES, d), lambda j, i: (j, 0)),
                  pl.BlockSpec((LANES, LANES), lambda j, i: (0, 0)),
                  pl.BlockSpec((tr, LANES), lambda j, i: (i, 0)), pl.BlockSpec((tr, LANES), lambda j, i: (i, 0))],
        out_specs=[pl.BlockSpec((tr, d), lambda j, i: (i, j)), pl.BlockSpec((SUBLANES, d), lambda j, i: (j, 0))],
        out_shape=[jax.ShapeDtypeStruct((t, n * d), BF16), jax.ShapeDtypeStruct((n * SUBLANES, d), F32)],
        compiler_params=_params(("arbitrary", "arbitrary")),
    )(x, *dys, g8, seg, cos, sin)
    dg = dg.reshape(n, SUBLANES, d // HEAD_DIM, HEAD_DIM).sum(axis=(1, 2))
    return dx, dg


def _band_valid_t(first):
    s = lax.broadcasted_iota(jnp.int32, (2 * BAND, BAND), 0)
    t = lax.broadcasted_iota(jnp.int32, (2 * BAND, BAND), 1)
    dist = t + BAND - s
    return (dist >= 0) & (dist <= BAND) & ((s >= BAND) | jnp.logical_not(first))


def _to_classes_t(z, dil, width):
    return z.reshape(z.shape[0] // dil, dil, width).transpose(1, 2, 0)


def _from_classes_t(z):
    dil, width, rows = z.shape
    return z.transpose(2, 0, 1).reshape(rows * dil, width)


def _band_fwd_n(nat, dil, cfg, *, name):
    d, hh = cfg.d_model, cfg.heads
    rows = cfg.tokens // dil
    nbt = rows // BAND
    nb = cfg.seq // (dil * BAND)

    def body(q_ref, kp_ref, kc_ref, vp_ref, vc_ref, o_ref, lse_ref):
        i = pl.program_id(1)
        valid = _band_valid_t(i % nb == 0)
        upper = lax.broadcasted_iota(jnp.int32, (LANES, BAND), 0) < HEAD_DIM
        for hp in range(d // LANES):
            pair = slice(hp * LANES, (hp + 1) * LANES)
            qt2 = q_ref[:, pair].T
            kk = jnp.concatenate([kp_ref[:, pair], kc_ref[:, pair]], axis=0)
            vvt = jnp.concatenate([vp_ref[:, pair], vc_ref[:, pair]], axis=0).T
            outs = []
            for e in range(2):
                h = 2 * hp + e
                qte = jnp.where(upper == (e == 0), qt2, jnp.zeros_like(qt2))
                s = jnp.where(valid, jnp.dot(kk, qte, preferred_element_type=F32), NEG)
                m = jnp.max(s, axis=0, keepdims=True)
                p = jnp.exp(s - m)
                l = jnp.sum(p, axis=0, keepdims=True)
                hi = p.astype(BF16)
                lo = (p - hi.astype(F32)).astype(BF16)
                vt = vvt[e * HEAD_DIM:(e + 1) * HEAD_DIM]
                o = jnp.dot(vt, hi, preferred_element_type=F32) + jnp.dot(vt, lo, preferred_element_type=F32)
                outs.append(o * (1.0 / l))
                lse_ref[h:h + 1, :] = m + jnp.log(l)
            o_ref[:, pair] = jnp.concatenate(outs, axis=0).T

    def prev(i):
        return jnp.maximum(i - 1, 0)

    blk = (BAND, d)
    return pl.pallas_call(
        body, name=name, grid=(dil, nbt),
        in_specs=[pl.BlockSpec(blk, lambda r, i: (i, r * 3)),
                  pl.BlockSpec(blk, lambda r, i: (prev(i), r * 3 + 1)),
                  pl.BlockSpec(blk, lambda r, i: (i, r * 3 + 1)),
                  pl.BlockSpec(blk, lambda r, i: (prev(i), r * 3 + 2)),
                  pl.BlockSpec(blk, lambda r, i: (i, r * 3 + 2))],
        out_specs=[pl.BlockSpec(blk, lambda r, i: (i, r)),
                   pl.BlockSpec((None, hh, BAND), lambda r, i: (r, 0, i))],
        out_shape=[jax.ShapeDtypeStruct((rows, dil * d), F32), jax.ShapeDtypeStruct((dil, hh, rows), F32)],
        compiler_params=_params(("parallel", "arbitrary")),
    )(nat, nat, nat, nat, nat)


def _band_bwd_n(nat, do_nat, lse_c, dsum_c, dil, cfg, *, name):
    d, hh = cfg.d_model, cfg.heads
    rows = cfg.tokens // dil
    nbt = rows // BAND
    nb = cfg.seq // (dil * BAND)

    def body(q_ref, kp_ref, kc_ref, vp_ref, vc_ref, do_ref, l_ref, ds_ref, dq_ref, dk_ref, dv_ref, ck_ref, cv_ref):
        i = pl.program_id(1)

        @pl.when(i < nbt)
        def _():
            @pl.when(i == 0)
            def _():
                ck_ref[...] = jnp.zeros_like(ck_ref)
                cv_ref[...] = jnp.zeros_like(cv_ref)

            valid1 = _band_valid_t(i % nb == 0)
            valid = jnp.concatenate([valid1, valid1], axis=1)
            upper = lax.broadcasted_iota(jnp.int32, (LANES, BAND), 0) < HEAD_DIM
            half2 = _half_mask((2 * BAND, LANES))

            def both(z):
                zero = jnp.zeros_like(z)
                return jnp.concatenate([jnp.where(upper, z, zero), jnp.where(upper, zero, z)], axis=1)

            def stack(z):
                return jnp.concatenate([z[:, :BAND], z[:, BAND:]], axis=0)

            for hp in range(d // LANES):
                pair = slice(hp * LANES, (hp + 1) * LANES)
                h0, h1 = 2 * hp, 2 * hp + 1
                qn2, don2 = q_ref[:, pair], do_ref[:, pair]
                kk = jnp.concatenate([kp_ref[:, pair], kc_ref[:, pair]], axis=0)
                vv = jnp.concatenate([vp_ref[:, pair], vc_ref[:, pair]], axis=0)
                lse2 = jnp.concatenate([l_ref[h0:h0 + 1, :], l_ref[h1:h1 + 1, :]], axis=1)
                dsum2 = jnp.concatenate([ds_ref[h0:h0 + 1, :], ds_ref[h1:h1 + 1, :]], axis=1)
                s = jnp.where(valid, jnp.dot(kk, both(qn2.T), preferred_element_type=F32), NEG)
                p = jnp.exp(s - lse2)
                dp = jnp.dot(vv, both(don2.T), preferred_element_type=F32)
                dsb = (p * (dp - dsum2)).astype(BF16)
                dq2 = jnp.dot(kk.T, dsb, preferred_element_type=F32)
                dq_ref[:, pair] = jnp.concatenate([dq2[:HEAD_DIM, :BAND], dq2[HEAD_DIM:, BAND:]], axis=0).T
                dk2 = jnp.dot(stack(dsb), qn2, preferred_element_type=F32)
                dv2 = jnp.dot(stack(p.astype(BF16)), don2, preferred_element_type=F32)
                dkk = jnp.where(half2, dk2[:2 * BAND], dk2[2 * BAND:])
                dvv = jnp.where(half2, dv2[:2 * BAND], dv2[2 * BAND:])
                dk_ref[:, pair] = ck_ref[:, pair] + dkk[:BAND]
                dv_ref[:, pair] = cv_ref[:, pair] + dvv[:BAND]
                ck_ref[:, pair] = dkk[BAND:]
                cv_ref[:, pair] = dvv[BAND:]

        @pl.when(i == nbt)
        def _():
            dk_ref[...] = ck_ref[...]
            dv_ref[...] = cv_ref[...]

    def cur(i):
        return jnp.minimum(i, nbt - 1)

    def prev(i):
        return jnp.maximum(cur(i) - 1, 0)

    cblk = (None, hh, BAND)
    blk = (BAND, d)
    here = pl.BlockSpec(blk, lambda r, i: (cur(i), r))
    behind = pl.BlockSpec(blk, lambda r, i: (jnp.maximum(i - 1, 0), r))
    shape = jax.ShapeDtypeStruct((rows, dil * d), F32)
    return pl.pallas_call(
        body, name=name, grid=(dil, nbt + 1),
        in_specs=[pl.BlockSpec(blk, lambda r, i: (cur(i), r * 3)),
                  pl.BlockSpec(blk, lambda r, i: (prev(i), r * 3 + 1)),
                  pl.BlockSpec(blk, lambda r, i: (cur(i), r * 3 + 1)),
                  pl.BlockSpec(blk, lambda r, i: (prev(i), r * 3 + 2)),
                  pl.BlockSpec(blk, lambda r, i: (cur(i), r * 3 + 2)),
                  here,
                  pl.BlockSpec(cblk, lambda r, i: (r, 0, cur(i))),
                  pl.BlockSpec(cblk, lambda r, i: (r, 0, cur(i)))],
        out_specs=[here, behind, behind],
        out_shape=[shape, shape, shape],
        scratch_shapes=[pltpu.VMEM(blk, F32), pltpu.VMEM(blk, F32)],
        compiler_params=_params(("arbitrary", "arbitrary")),
    )(nat, nat, nat, nat, nat, do_nat, lse_c, dsum_c)


def _mix_fwd(outs, lses, *, name):
    t, d = outs[0].shape
    hh = lses[0].shape[1]
    tr = _blk(t, ROW_BLOCK)
    ng = len(outs)
    spread = (lax.broadcasted_iota(jnp.int32, (hh, d), 0)
              == lax.broadcasted_iota(jnp.int32, (hh, d), 1) // HEAD_DIM).astype(BF16)

    def body(*refs):
        o_refs, l_refs = refs[:ng], refs[ng:2 * ng]
        spread_ref, mixed_ref, lse_ref = refs[2 * ng:]
        ls = [r[...] for r in l_refs]
        m = functools.reduce(jnp.maximum, ls)
        es = [jnp.exp(l - m) for l in ls]
        tot = functools.reduce(jnp.add, es)
        inv = 1.0 / tot
        mixed_ref[...] = functools.reduce(
            jnp.add, [_head_sum(e * inv, spread_ref[...]) * r[...] for e, r in zip(es, o_refs)])
        lse_ref[...] = m + jnp.log(tot)

    spec = pl.BlockSpec((tr, d), lambda i: (i, 0))
    cspec = pl.BlockSpec((tr, hh), lambda i: (i, 0))
    return pl.pallas_call(
        body, name=name, grid=(t // tr,),
        in_specs=[spec] * ng + [cspec] * ng + [pl.BlockSpec((hh, d), lambda i: (0, 0))], out_specs=[spec, cspec],
        out_shape=[jax.ShapeDtypeStruct((t, d), F32), jax.ShapeDtypeStruct((t, hh), F32)],
        compiler_params=_params(("parallel",)),
    )(*outs, *lses, spread)


GATE_BLOCK = 256


def _tri(n, upper):
    r = lax.broadcasted_iota(jnp.int32, (n, n), 0)
    c = lax.broadcasted_iota(jnp.int32, (n, n), 1)
    return ((c >= r) if upper else (c <= r)).astype(BF16)


def _tri_dot(tri, x):
    hi = x.astype(BF16)
    r1 = x - hi.astype(F32)
    mid = r1.astype(BF16)
    lo = (r1 - mid.astype(F32)).astype(BF16)
    dot = functools.partial(jnp.dot, preferred_element_type=F32)
    return dot(tri, hi) + dot(tri, mid) + dot(tri, lo)


def _log_sigmoid(z):
    return jnp.minimum(z, 0.0) - jnp.log(1.0 + jnp.exp(-jnp.abs(z)))


def _gate_fwd(proj, col_block, bias, cfg, *, name):
    tr = _blk(cfg.seq, GATE_BLOCK)
    nblk = cfg.seq // tr

    def body(z_ref, b_ref, tri_ref, o_ref, carry_ref):
        i = pl.program_id(1)

        @pl.when(i == 0)
        def _():
            carry_ref[...] = jnp.zeros_like(carry_ref)

        logf = _log_sigmoid(z_ref[...] + b_ref[0:1, :])
        cum = _tri_dot(tri_ref[...], logf) + carry_ref[0:1, :]
        o_ref[...] = cum
        carry_ref[...] = jnp.broadcast_to(cum[tr - 1:tr, :], carry_ref.shape)

    return pl.pallas_call(
        body, name=name, grid=(cfg.batch, nblk),
        in_specs=[pl.BlockSpec((tr, LANES), lambda b, i: (b * nblk + i, col_block)),
                  pl.BlockSpec((SUBLANES, LANES), lambda b, i: (0, 0)),
                  pl.BlockSpec((tr, tr), lambda b, i: (0, 0))],
        out_specs=pl.BlockSpec((tr, LANES), lambda b, i: (b * nblk + i, 0)),
        out_shape=jax.ShapeDtypeStruct((cfg.tokens, LANES), F32),
        scratch_shapes=[pltpu.VMEM((SUBLANES, LANES), F32)],
        compiler_params=_params(("arbitrary", "arbitrary")),
    )(proj, jnp.broadcast_to(bias, (SUBLANES, LANES)), _tri(tr, upper=False))


def _gate_bwd(proj, col_block, bias, dcum, cfg, *, name):
    tr = _blk(cfg.seq, GATE_BLOCK)
    nblk = cfg.seq // tr

    def body(z_ref, b_ref, tri_ref, dc_ref, dz_ref, db_ref, carry_ref):
        b = pl.program_id(0)
        i = pl.program_id(1)

        @pl.when(i == 0)
        def _():
            carry_ref[...] = jnp.zeros_like(carry_ref)

        @pl.when((i == 0) & (b == 0))
        def _():
            db_ref[...] = jnp.zeros_like(db_ref)

        dcv = dc_ref[...]
        dlogf = _tri_dot(tri_ref[...], dcv) + carry_ref[0:1, :]
        carry_ref[...] = jnp.broadcast_to(dlogf[0:1, :], carry_ref.shape)
        dz = dlogf * jax.nn.sigmoid(-(z_ref[...] + b_ref[0:1, :]))
        dz_ref[...] = dz
        db_ref[...] += _fold8(dz)

    def rev(b, i):
        return (b * nblk + nblk - 1 - i, 0)

    dz, db = pl.pallas_call(
        body, name=name, grid=(cfg.batch, nblk),
        in_specs=[pl.BlockSpec((tr, LANES), lambda b, i: (b * nblk + nblk - 1 - i, col_block)),
                  pl.BlockSpec((SUBLANES, LANES), lambda b, i: (0, 0)),
                  pl.BlockSpec((tr, tr), lambda b, i: (0, 0)),
                  pl.BlockSpec((tr, LANES), rev)],
        out_specs=[pl.BlockSpec((tr, LANES), rev), pl.BlockSpec((SUBLANES, LANES), lambda b, i: (0, 0))],
        out_shape=[jax.ShapeDtypeStruct((cfg.tokens, LANES), F32), jax.ShapeDtypeStruct((SUBLANES, LANES), F32)],
        scratch_shapes=[pltpu.VMEM((SUBLANES, LANES), F32)],
        compiler_params=_params(("arbitrary", "arbitrary")),
    )(proj, jnp.broadcast_to(bias, (SUBLANES, LANES)), _tri(tr, upper=True), dcum)
    return dz, jnp.sum(db, axis=0)


FOX_BLOCK = 256
AUG = LANES
BIAS_TERMS = 3


def _fox_aug_k_call(kv, cum, cfg, *, name):
    t, d, hh = cfg.tokens, cfg.d_model, cfg.heads
    tr = _blk(t, ROW_BLOCK)

    def body(k_ref, c_ref, o_ref):
        lane = lax.broadcasted_iota(jnp.int32, (tr, LANES), 1)
        for hp in range(hh // 2):
            k2 = k_ref[:, hp * LANES:(hp + 1) * LANES].astype(F32)
            for e in range(2):
                h = 2 * hp + e
                kh = k2 if e == 0 else pltpu.roll(k2, HEAD_DIM, 1)
                c = -c_ref[:, h:h + 1]
                hi = c.astype(BF16).astype(F32)
                mid = (c - hi).astype(BF16).astype(F32)
                lo = c - hi - mid
                bias = jnp.where(lane == HEAD_DIM, hi, jnp.where(lane == HEAD_DIM + 1, mid,
                                 jnp.where(lane == HEAD_DIM + 2, lo, 0.0)))
                o_ref[:, h * AUG:(h + 1) * AUG] = jnp.where(lane < HEAD_DIM, kh, bias).astype(BF16)

    return pl.pallas_call(
        body, name=name, grid=(t // tr,),
        in_specs=[pl.BlockSpec((tr, d), lambda i: (i, 0)), pl.BlockSpec((tr, LANES), lambda i: (i, 0))],
        out_specs=pl.BlockSpec((tr, hh * AUG), lambda i: (i, 0)),
        out_shape=jax.ShapeDtypeStruct((t, hh * AUG), BF16),
        compiler_params=_params(("parallel",)),
    )(kv, cum)


def _keys_visible(tq):
    s = lax.broadcasted_iota(jnp.int32, (tq, tq), 0)
    t = lax.broadcasted_iota(jnp.int32, (tq, tq), 1)
    return s <= t


def _aug_q_t(q2, e, tq):
    ones = (lax.broadcasted_iota(jnp.int32, (AUG - HEAD_DIM, tq), 0) < BIAS_TERMS).astype(q2.dtype)
    return jnp.concatenate([q2[e * HEAD_DIM:(e + 1) * HEAD_DIM], ones], axis=0)


def _fox_fwd_n(q, k_aug, kv, cfg, *, name):
    t, d, hh = cfg.tokens, cfg.d_model, cfg.heads
    tq = _blk(cfg.seq, FOX_BLOCK)
    nq = cfg.seq // tq

    def body(q_ref, ka_ref, v_ref, o_ref, lse_ref, qa_ref, m_ref, l_ref, acc_ref):
        qi, ki = pl.program_id(1), pl.program_id(2)

        @pl.when(ki == 0)
        def _():
            m_ref[...] = jnp.full_like(m_ref, NEG)
            l_ref[...] = jnp.zeros_like(l_ref)
            acc_ref[...] = jnp.zeros_like(acc_ref)
            for hp in range(hh // 2):
                q2 = q_ref[:, hp * LANES:(hp + 1) * LANES].T
                for e in range(2):
                    h = 2 * hp + e
                    qa_ref[h * AUG:(h + 1) * AUG, :] = _aug_q_t(q2, e, tq)

        def step(diagonal):
            for hp in range(hh // 2):
                vt2 = v_ref[:, hp * LANES:(hp + 1) * LANES].T
                for e in range(2):
                    h = 2 * hp + e
                    rows = slice(h * HEAD_DIM, (h + 1) * HEAD_DIM)
                    s = jnp.dot(ka_ref[:, h * AUG:(h + 1) * AUG], qa_ref[h * AUG:(h + 1) * AUG, :],
                                preferred_element_type=F32)
                    if diagonal:
                        s = jnp.where(_keys_visible(tq), s, NEG)
                    m_prev = m_ref[h:h + 1, :]
                    m_new = jnp.maximum(m_prev, jnp.max(s, axis=0, keepdims=True))
                    alpha = jnp.exp(m_prev - m_new)
                    p = jnp.exp(s - m_new)
                    l_ref[h:h + 1, :] = alpha * l_ref[h:h + 1, :] + jnp.sum(p, axis=0, keepdims=True)
                    m_ref[h:h + 1, :] = m_new
                    hi = p.astype(BF16)
                    lo = (p - hi.astype(F32)).astype(BF16)
                    vt = vt2[e * HEAD_DIM:(e + 1) * HEAD_DIM]
                    acc_ref[rows, :] = (alpha * acc_ref[rows, :] + jnp.dot(vt, hi, preferred_element_type=F32)
                                        + jnp.dot(vt, lo, preferred_element_type=F32))

        pl.when(ki < qi)(functools.partial(step, False))
        pl.when(ki == qi)(functools.partial(step, True))

        @pl.when(ki == qi)
        def _():
            for hp in range(hh // 2):
                halves = [acc_ref[h * HEAD_DIM:(h + 1) * HEAD_DIM, :] * (1.0 / l_ref[h:h + 1, :])
                          for h in (2 * hp, 2 * hp + 1)]
                o_ref[:, hp * LANES:(hp + 1) * LANES] = jnp.concatenate(halves, axis=0).T
            lse_ref[...] = m_ref[...] + jnp.log(l_ref[...])

    def qrow(b, qi, ki):
        return (b * nq + qi, 0)

    return pl.pallas_call(
        body, name=name, grid=(cfg.batch, nq, nq),
        in_specs=[pl.BlockSpec((tq, d), qrow),
                  pl.BlockSpec((tq, hh * AUG), lambda b, qi, ki: (b * nq + jnp.minimum(ki, qi), 0)),
                  pl.BlockSpec((tq, d), lambda b, qi, ki: (b * nq + jnp.minimum(ki, qi), 1))],
        out_specs=[pl.BlockSpec((tq, d), qrow), pl.BlockSpec((hh, tq), lambda b, qi, ki: (0, b * nq + qi))],
        out_shape=[jax.ShapeDtypeStruct((t, d), F32), jax.ShapeDtypeStruct((hh, t), F32)],
        scratch_shapes=[pltpu.VMEM((hh * AUG, tq), BF16), pltpu.VMEM((hh, tq), F32), pltpu.VMEM((hh, tq), F32),
                        pltpu.VMEM((d, tq), F32)],
        compiler_params=_params(("parallel", "parallel", "arbitrary")),
    )(q, k_aug, kv)


def _head_dot_c(a, b, cfg, *, name):
    t, d, hh = cfg.tokens, cfg.d_model, cfg.heads
    tc = _blk(t, ROW_BLOCK)

    def body(a_ref, b_ref, o_ref):
        for hp in range(hh // 2):
            pair = slice(hp * LANES, (hp + 1) * LANES)
            prod = (a_ref[:, pair].astype(F32) * b_ref[:, pair]).T
            for e in range(2):
                h = 2 * hp + e
                o_ref[h:h + 1, :] = jnp.sum(prod[e * HEAD_DIM:(e + 1) * HEAD_DIM], axis=0, keepdims=True)

    return pl.pallas_call(
        body, name=name, grid=(t // tc,),
        in_specs=[pl.BlockSpec((tc, d), lambda i: (i, 0)), pl.BlockSpec((tc, d), lambda i: (i, 0))],
        out_specs=pl.BlockSpec((hh, tc), lambda i: (0, i)),
        out_shape=jax.ShapeDtypeStruct((hh, t), F32),
        compiler_params=_params(("parallel",)),
    )(a, b)


def _fox_bwd_n(q, k_aug, kv, do, lse, dsum, cfg, *, name):
    t, d, hh = cfg.tokens, cfg.d_model, cfg.heads
    tq = _blk(cfg.seq, FOX_BLOCK)
    nq = cfg.seq // tq

    def body(q_ref, ka_ref, v_ref, do_ref, lse_ref, ds_ref, dq_hbm, dk_ref, dv_ref, dc_ref, dq_acc, sem):
        b, ki, qi = pl.program_id(0), pl.program_id(1), pl.program_id(2)
        qq = jnp.maximum(qi, ki)

        @pl.when((ki == 0) & (qi == 0))
        def _():
            dq_acc[...] = jnp.zeros_like(dq_acc)

        @pl.when(qi == 0)
        def _():
            dk_ref[...] = jnp.zeros_like(dk_ref)
            dv_ref[...] = jnp.zeros_like(dv_ref)
            dc_ref[...] = jnp.zeros_like(dc_ref)

        def step(diagonal):
            upper = lax.broadcasted_iota(jnp.int32, (LANES, tq), 0) < HEAD_DIM
            half = _half_mask((tq, LANES))
            for hp in range(hh // 2):
                pair = slice(hp * LANES, (hp + 1) * LANES)
                q2 = q_ref[:, pair].T
                don2 = do_ref[:, pair]
                dot2 = don2.T
                dvs, dks = [], []
                for e in range(2):
                    h = 2 * hp + e
                    rows = slice(h * HEAD_DIM, (h + 1) * HEAD_DIM)
                    ka = ka_ref[:, h * AUG:(h + 1) * AUG]
                    qa = _aug_q_t(q2, e, tq)
                    s = jnp.dot(ka, qa, preferred_element_type=F32)
                    if diagonal:
                        s = jnp.where(_keys_visible(tq), s, NEG)
                    p = jnp.exp(s - lse_ref[h:h + 1, :])
                    dote = jnp.where(upper == (e == 0), dot2, jnp.zeros_like(dot2))
                    dp = jnp.dot(v_ref[:, pair], dote, preferred_element_type=F32)
                    dsf = p * (dp - ds_ref[h:h + 1, :])
                    dc_ref[:, h:h + 1] -= jnp.sum(dsf, axis=1, keepdims=True)
                    dsc = dsf.astype(BF16)
                    dvs.append(jnp.dot(p.astype(BF16), don2, preferred_element_type=F32))
                    dks.append(lax.dot_general(dsc, qa, _NT, preferred_element_type=F32))
                    dq_acc[qq, rows, :] += jnp.dot(ka.T[:HEAD_DIM], dsc, preferred_element_type=F32)
                dv_ref[:, pair] += jnp.where(half, dvs[0], dvs[1])
                dk_ref[:, pair] += jnp.where(half, dks[0], pltpu.roll(dks[1], HEAD_DIM, 1))

        pl.when(qi > ki)(functools.partial(step, False))
        pl.when(qi == ki)(functools.partial(step, True))

        @pl.when((ki == nq - 1) & (qi == nq - 1))
        def _():
            cp = pltpu.make_async_copy(dq_acc, dq_hbm.at[b], sem)
            cp.start()
            cp.wait()

    def qrow(b, ki, qi):
        return (b * nq + jnp.maximum(qi, ki), 0)

    def qcol(b, ki, qi):
        return (0, b * nq + jnp.maximum(qi, ki))

    def krow(b, ki, qi):
        return (b * nq + ki, 0)

    return pl.pallas_call(
        body, name=name, grid=(cfg.batch, nq, nq),
        in_specs=[pl.BlockSpec((tq, d), qrow),
                  pl.BlockSpec((tq, hh * AUG), krow),
                  pl.BlockSpec((tq, d), lambda b, ki, qi: (b * nq + ki, 1)),
                  pl.BlockSpec((tq, d), qrow),
                  pl.BlockSpec((hh, tq), qcol), pl.BlockSpec((hh, tq), qcol)],
        out_specs=[pl.BlockSpec(memory_space=pl.ANY), pl.BlockSpec((tq, d), krow),
                   pl.BlockSpec((tq, d), krow), pl.BlockSpec((tq, LANES), krow)],
        out_shape=[jax.ShapeDtypeStruct((cfg.batch, nq, d, tq), F32), jax.ShapeDtypeStruct((t, d), F32),
                   jax.ShapeDtypeStruct((t, d), F32), jax.ShapeDtypeStruct((t, LANES), F32)],
        scratch_shapes=[pltpu.VMEM((nq, d, tq), F32), pltpu.SemaphoreType.DMA],
        compiler_params=_params(("arbitrary", "arbitrary", "arbitrary")),
    )(q, k_aug, kv, do, lse, dsum)


def _fwd(a, w, *, name, res=None, scale=1.0, norms=()):
    bm = FUSED_ROWS if norms else 1024
    out = _mm(a, w, form="F", out_dtype=F32, name=name, bm=bm, bn=WIDE, bk=WIDE, res=res, scale=scale, norms=norms)
    return (out[0], list(out[1:])) if norms else out


def _bwd(dy, w, *, name, scale=1.0, norm_bwd=None):
    bm = FUSED_ROWS if norm_bwd is not None else 1024
    return _mm(dy, w, form="B", out_dtype=F32, name=name, bm=bm, bn=WIDE, bk=WIDE, scale=scale, norm_bwd=norm_bwd)


def _wgrad(a, dy, w, *, name, scale=1.0):
    return _mm_grad(a, dy, w.shape[0], name=name, bm=WIDE, bn=WIDE, scale=scale)


def _ffn_fwd(h, n, w_in, w_out, tag, norms=()):
    gate, up, a = _ffn_in_act(n, w_in, name=f"{tag}_in")
    out = _fwd(a, w_out, name=f"{tag}_out", res=h, scale=0.5, norms=norms)
    h_out, normed = out if norms else (out, [])
    return h_out, normed, (n, gate, up, a)


def _ffn_bwd(dh_out, h, g, w_in, w_out, saved, tag):
    n, gate, up, a = saved
    du = _ffn_out_dx_act(dh_out, w_out, gate, up, name=f"{tag}_out_dx", scale=0.5)
    dw_out = _wgrad(a, dh_out, w_out, name=f"{tag}_out_dw", scale=0.5)
    dh, dg = _mm_back2(du, w_in, (h, g, dh_out), name=f"{tag}_in_dx", bk=WIDE)
    dw_in = _mm_grad(n, du, w_in.shape[0], name=f"{tag}_in_dw", bm=WIDE, bn=WIDE)
    return dh, dg, dw_in, dw_out


def _head_gain(g, heads, scale=1.0):
    return jnp.tile(g.astype(F32) * scale, heads)


def _local_step(cfg, x, positions, target, w, s):
    d, hh = cfg.d_model, cfg.heads
    cos, sin = _rope_tables(positions)
    ones = jnp.ones((d,), F32)

    n00 = _rms_fwd(x, s["ffn_norm"][0, 0], name="ffn00_norm")
    h1, (hn_a,), ffn0 = _ffn_fwd(x, n00, w["ffn_w_in"][0][0], w["ffn_w_out"][0][0], "ffn00", [s["mix_norm"][0]])
    qkv = _fwd(hn_a, w["a_w_qkv"], name="a_qkv")
    kinds_a = ["rope", "rope", "cast"] * len(DILATIONS)
    gains_a = jnp.stack([z for g in range(len(DILATIONS)) for z in (
        _head_gain(s["a_q_norm"][g], hh, Q_SCALE), _head_gain(s["a_k_norm"][g], hh), ones)])
    qkvp = [_hn_fwd(qkv, gains_a[3 * g:3 * g + 3], kinds_a[:3], d, cos, sin, name=f"a_qk_norm{g}", col0=3 * g)
            for g in range(len(DILATIONS))]
    lay = [qkvp[g].reshape(cfg.tokens // dil, dil * 3 * d) for g, dil in enumerate(DILATIONS)]
    band = [_band_fwd_n(lay[g], dil, cfg, name=f"a_band{g}") for g, dil in enumerate(DILATIONS)]
    mixed, lse_a = _mix_fwd([o.reshape(cfg.tokens, d) for o, _ in band], [_from_classes_t(l) for _, l in band],
                            name="a_mix")
    h2, (n01,) = _fwd(mixed, w["a_w_o"], name="a_out", res=h1, norms=[s["ffn_norm"][0, 1]])
    h3, (kn, n10), ffn1 = _ffn_fwd(h2, n01, w["ffn_w_in"][0][1], w["ffn_w_out"][0][1], "ffn01",
                                   [s["kv_norm"], s["ffn_norm"][1, 0]])

    proj = _fwd(kn, w["kv_w"], name="kv_proj")
    kinds_kv = ["norm", "cast"]
    gains_kv = jnp.stack([_head_gain(s["kv_k_norm"], hh), ones])
    kvp = _hn_fwd(proj, gains_kv, kinds_kv, d, cos, sin, name="kv_k_norm")
    gate_col = 2 * d // LANES
    bias = jnp.pad(s["kv_b_f"].astype(F32), (0, LANES - hh))
    cum = _gate_fwd(proj, gate_col, bias, cfg, name="kv_gate")
    k_aug = _fox_aug_k_call(kvp, cum, cfg, name="kv_aug")

    h4, (hn_b,), ffn2 = _ffn_fwd(h3, n10, w["ffn_w_in"][1][0], w["ffn_w_out"][1][0], "ffn10", [s["mix_norm"][1]])
    qraw = _fwd(hn_b, w["b_w_q"], name="b_q")
    gains_b = _head_gain(s["b_q_norm"][0], hh, Q_SCALE)[None]
    qp = _hn_fwd(qraw, gains_b, ["norm"], d, cos, sin, name="b_q_norm")
    o_b, lse_b = _fox_fwd_n(qp, k_aug, kvp, cfg, name="b_fox")
    h5, (n11,) = _fwd(o_b, w["b_w_o"], name="b_out", res=h4, norms=[s["ffn_norm"][1, 1]])
    h6, _, ffn3 = _ffn_fwd(h5, n11, w["ffn_w_in"][1][1], w["ffn_w_out"][1][1], "ffn11")

    loss, dh6 = _loss_fwd_bwd(h6, target, name="loss")

    dh5, dg11, dwi11, dwo11 = _ffn_bwd(dh6, h5, s["ffn_norm"][1, 1], w["ffn_w_in"][1][1], w["ffn_w_out"][1][1],
                                       ffn3, "ffn11")
    do_b = _bwd(dh5, w["b_w_o"], name="b_out_dx")
    dw_bo = _wgrad(o_b, dh5, w["b_w_o"], name="b_out_dw")
    do_bf = do_b.astype(BF16)
    dsum_b = _head_dot_c(do_bf, o_b, cfg, name="b_dsum")
    dq4, dk_b, dv_b, dcum = _fox_bwd_n(qp, k_aug, kvp, do_bf, lse_b, dsum_b, cfg, name="b_fox_bwd")
    dq_b = dq4.transpose(0, 1, 3, 2).reshape(cfg.tokens, d)
    dqraw, dgq = _hn_bwd(qraw, [dq_b], gains_b, ["norm"], d, cos, sin, name="b_q_norm_bwd")
    dh4, dmix1 = _bwd(dqraw, w["b_w_q"], name="b_q_dx", norm_bwd=(h4, s["mix_norm"][1], dh5))
    dw_bq = _wgrad(hn_b, dqraw, w["b_w_q"], name="b_q_dw")
    dh3, dg10, dwi10, dwo10 = _ffn_bwd(dh4, h3, s["ffn_norm"][1, 0], w["ffn_w_in"][1][0], w["ffn_w_out"][1][0],
                                       ffn2, "ffn10")

    dkvraw, dgk = _hn_bwd(proj, [dk_b, dv_b], gains_kv, kinds_kv, d, cos, sin, name="kv_k_norm_bwd")
    dz, dbias = _gate_bwd(proj, gate_col, bias, dcum, cfg, name="kv_gate_bwd")
    pad_cols = w["kv_w"].shape[2] - 2 * d - LANES
    dproj = jnp.concatenate([dkvraw, dz.astype(BF16), jnp.zeros((cfg.tokens, pad_cols), BF16)], axis=1)
    dw_kv = _wgrad(kn, dproj, w["kv_w"], name="kv_proj_dw")
    dh3, dkvn = _bwd(dproj, w["kv_w"], name="kv_proj_dx", norm_bwd=(h3, s["kv_norm"], dh3))

    dh2, dg01, dwi01, dwo01 = _ffn_bwd(dh3, h2, s["ffn_norm"][0, 1], w["ffn_w_in"][0][1], w["ffn_w_out"][0][1],
                                       ffn1, "ffn01")
    dmixed = _bwd(dh2, w["a_w_o"], name="a_out_dx")
    dw_ao = _wgrad(mixed, dh2, w["a_w_o"], name="a_out_dw")
    dmixed_bf = dmixed.astype(BF16)
    dsum_a = _head_dot_c(dmixed_bf, mixed, cfg, name="a_dsum")
    dqkvp = []
    for g, dil in enumerate(DILATIONS):
        dsum_c = dsum_a.reshape(hh, cfg.tokens // dil, dil).transpose(2, 0, 1)
        grads = _band_bwd_n(lay[g], dmixed_bf.reshape(cfg.tokens // dil, dil * d), _to_classes_t(lse_a, dil, hh),
                            dsum_c, dil, cfg, name=f"a_band{g}_bwd")
        dqkvp += [z.reshape(cfg.tokens, d) for z in grads]
    dqkv, dga = _hn_bwd(qkv, dqkvp, gains_a, kinds_a, d, cos, sin, name="a_qk_norm_bwd")
    dh1, dmix0 = _bwd(dqkv, w["a_w_qkv"], name="a_qkv_dx", norm_bwd=(h1, s["mix_norm"][0], dh2))
    dw_qkv = _wgrad(hn_a, dqkv, w["a_w_qkv"], name="a_qkv_dw")
    dx, dg00, dwi00, dwo00 = _ffn_bwd(dh1, x, s["ffn_norm"][0, 0], w["ffn_w_in"][0][0], w["ffn_w_out"][0][0],
                                      ffn0, "ffn00")

    dw = {
        "ffn_w_in": [[dwi00, dwi01], [dwi10, dwi11]],
        "ffn_w_out": [[dwo00, dwo01], [dwo10, dwo11]],
        "a_w_qkv": dw_qkv, "a_w_o": dw_ao, "kv_w": dw_kv, "b_w_q": dw_bq, "b_w_o": dw_bo,
    }
    ds = {
        "ffn_norm": jnp.stack([jnp.stack([dg00, dg01]), jnp.stack([dg10, dg11])]),
        "mix_norm": jnp.stack([dmix0, dmix1]),
        "a_q_norm": jnp.stack([dga[3 * g] for g in range(len(DILATIONS))])[None] * Q_SCALE,
        "a_k_norm": jnp.stack([dga[3 * g + 1] for g in range(len(DILATIONS))])[None],
        "kv_norm": dkvn,
        "kv_b_f": dbias[:hh],
        "kv_k_norm": dgk[0],
        "b_q_norm": dgq * Q_SCALE,
    }
    return loss, dx, dw, ds


MESH_ID = pl.DeviceIdType.MESH
ANY = pl.BlockSpec(memory_space=pl.ANY)
PACK_COLS = 1024
PACK_ROW_ALIGN = 32


def _me():
    return lax.axis_index("x"), lax.axis_index("y"), lax.axis_index("c")


def _other_chips(x, y):
    return [(1 - x, y), (x, 1 - y), (1 - x, 1 - y)]


def _all_gather_small(v, *, name):
    r = v.shape[0]

    def body(v_ref, out_ref, send_sems, recv_sems):
        x, y, c = _me()
        me = 4 * x + 2 * y + c
        out_ref[me] = v_ref[...]
        copies = []
        for k in range(1, N_DEV):
            fx, fy, fc = (k >> 2) & 1, (k >> 1) & 1, k & 1
            peer = (1 - x if fx else x, 1 - y if fy else y, 1 - c if fc else c)
            copies.append(pltpu.make_async_remote_copy(
                src_ref=v_ref, dst_ref=out_ref.at[me], send_sem=send_sems.at[k - 1], recv_sem=recv_sems.at[k - 1],
                device_id=peer, device_id_type=MESH_ID))
        for cp in copies:
            cp.start()
        for cp in copies:
            cp.wait()

    return pl.pallas_call(
        body, name=name,
        in_specs=[pl.BlockSpec(memory_space=pltpu.VMEM)], out_specs=pl.BlockSpec(memory_space=pltpu.VMEM),
        out_shape=jax.ShapeDtypeStruct((N_DEV, r, LANES), v.dtype),
        scratch_shapes=[pltpu.SemaphoreType.DMA((N_DEV - 1,)), pltpu.SemaphoreType.DMA((N_DEV - 1,))],
    )(v)


def _all_gather_chips(v, *, name):
    rh = v.shape[0] // 2

    def body(v_ref, out_ref, send_sems, recv_sems):
        x, y, c = _me()
        j = 2 * x + y
        chips = _other_chips(x, y)

        def half(chip, core):
            return out_ref.at[chip, pl.ds(core * rh, rh)]

        first = [pltpu.make_async_remote_copy(
            src_ref=v_ref.at[pl.ds(c * rh, rh)], dst_ref=half(j, c), send_sem=send_sems.at[k],
            recv_sem=recv_sems.at[k], device_id=(px, py, c), device_id_type=MESH_ID)
            for k, (px, py) in enumerate(chips)]
        for cp in first:
            cp.start()
        passed = [pltpu.make_async_remote_copy(
            src_ref=half(2 * px + py, c), dst_ref=half(2 * px + py, c), send_sem=send_sems.at[3 + k],
            recv_sem=recv_sems.at[3 + k], device_id=(x, y, 1 - c), device_id_type=MESH_ID)
            for k, (px, py) in enumerate(chips)]
        for k in range(len(chips)):
            first[k].wait_recv()
            passed[k].start()
        for k, (px, py) in enumerate(chips):
            pltpu.make_async_remote_copy(
                src_ref=half(2 * px + py, 1 - c), dst_ref=half(2 * px + py, 1 - c), send_sem=send_sems.at[3 + k],
                recv_sem=recv_sems.at[3 + k], device_id=(x, y, 1 - c), device_id_type=MESH_ID).wait_recv()
        for cp in first + passed:
            cp.wait_send()

    return pl.pallas_call(
        body, name=name, in_specs=[ANY], out_specs=ANY,
        out_shape=jax.ShapeDtypeStruct((N_CHIPS,) + v.shape, v.dtype),
        scratch_shapes=[pltpu.SemaphoreType.DMA((2 * (N_CHIPS - 1),)), pltpu.SemaphoreType.DMA((2 * (N_CHIPS - 1),))],
    )(v)


def _swap_halves(g, *, name):
    n, r, cols = g.shape
    rh = r // 2

    def body(g_ref, out_ref, send_sem, recv_sem):
        x, y, c = _me()
        cp = pltpu.make_async_remote_copy(
            src_ref=g_ref.at[:, pl.ds((1 - c) * rh, rh)], dst_ref=out_ref, send_sem=send_sem, recv_sem=recv_sem,
            device_id=(x, y, 1 - c), device_id_type=MESH_ID)
        cp.start()
        cp.wait()

    return pl.pallas_call(
        body, name=name, in_specs=[ANY], out_specs=ANY,
        out_shape=jax.ShapeDtypeStruct((n, rh, cols), g.dtype),
        scratch_shapes=[pltpu.SemaphoreType.DMA, pltpu.SemaphoreType.DMA],
    )(g)


def _scatter_chips(v, *, name):
    def body(v_ref, out_ref, send_sems, recv_sems):
        x, y, c = _me()
        j = 2 * x + y
        copies = [pltpu.make_async_remote_copy(
            src_ref=v_ref.at[2 * px + py], dst_ref=out_ref.at[j], send_sem=send_sems.at[k], recv_sem=recv_sems.at[k],
            device_id=(px, py, c), device_id_type=MESH_ID) for k, (px, py) in enumerate(_other_chips(x, y))]
        for cp in copies:
            cp.start()
        for cp in copies:
            cp.wait()

    return pl.pallas_call(
        body, name=name, in_specs=[ANY], out_specs=ANY,
        out_shape=jax.ShapeDtypeStruct(v.shape, v.dtype),
        scratch_shapes=[pltpu.SemaphoreType.DMA((N_CHIPS - 1,)), pltpu.SemaphoreType.DMA((N_CHIPS - 1,))],
    )(v)


def _join_halves(v, *, name):
    def body(v_ref, out_ref, send_sem, recv_sem):
        x, y, c = _me()
        cp = pltpu.make_async_remote_copy(
            src_ref=v_ref, dst_ref=out_ref.at[c], send_sem=send_sem, recv_sem=recv_sem,
            device_id=(x, y, 1 - c), device_id_type=MESH_ID)
        cp.start()
        cp.wait()

    return pl.pallas_call(
        body, name=name, in_specs=[ANY], out_specs=ANY,
        out_shape=jax.ShapeDtypeStruct((2,) + v.shape, v.dtype),
        scratch_shapes=[pltpu.SemaphoreType.DMA, pltpu.SemaphoreType.DMA],
    )(v)


def _row_blk(rows, want):
    for b in range(min(rows, want) // SUBLANES * SUBLANES, 0, -SUBLANES):
        if rows % b == 0:
            return b
    return rows


def _add_own_half(g, got, *, name):
    n, r, cols = g.shape
    rh = r // 2
    tr = _row_blk(rh, 512)
    nb = rh // tr

    def body(c_ref, g_ref, got_ref, o_ref):
        del c_ref
        o_ref[...] = (g_ref[...] + got_ref[...]).astype(BF16)

    grid_spec = pltpu.PrefetchScalarGridSpec(
        num_scalar_prefetch=1, grid=(n, nb),
        in_specs=[pl.BlockSpec((None, tr, cols), lambda j, i, c: (j, c[0] * nb + i, 0)),
                  pl.BlockSpec((None, tr, cols), lambda j, i, c: (j, i, 0))],
        out_specs=pl.BlockSpec((None, tr, cols), lambda j, i, c: (j, i, 0)))
    return pl.pallas_call(
        body, name=name, grid_spec=grid_spec, out_shape=jax.ShapeDtypeStruct((n, rh, cols), BF16),
        compiler_params=_params(("parallel", "parallel")),
    )(lax.axis_index("c").astype(jnp.int32).reshape(1), g, got)


def _sum_parts(parts, *, name):
    n, r, cols = parts.shape
    tr = _row_blk(r, 512)

    def body(*refs):
        o_ref = refs[n]
        acc = refs[0][...].astype(F32)
        for p_ref in refs[1:n]:
            acc = acc + p_ref[...].astype(F32)
        o_ref[...] = acc

    return pl.pallas_call(
        body, name=name, grid=(r // tr,),
        in_specs=[pl.BlockSpec((None, tr, cols), functools.partial(lambda j, i: (j, i, 0), j)) for j in range(n)],
        out_specs=pl.BlockSpec((tr, cols), lambda i: (i, 0)),
        out_shape=jax.ShapeDtypeStruct((r, cols), F32),
        compiler_params=_params(("parallel",)),
    )(*([parts] * n))


def _adamw(w, m, v, g, *, name):
    shape = w.shape
    cols = shape[-1]
    w2, m2, v2, g2 = (z.reshape(-1, cols) for z in (w, m, v, g))
    rows = w2.shape[0]
    tr = _row_blk(rows, max(SUBLANES, (1 << 20) // (4 * cols)))

    def body(w_ref, m_ref, v_ref, g_ref, d_ref, nm_ref, nv_ref):
        gv = g_ref[...]
        nm = ADAM_B1 * m_ref[...] + (1.0 - ADAM_B1) * gv
        nv = ADAM_B2 * v_ref[...] + (1.0 - ADAM_B2) * jnp.square(gv)
        m_hat = nm / (1.0 - ADAM_B1 ** ADAM_STEP)
        v_hat = nv / (1.0 - ADAM_B2 ** ADAM_STEP)
        d_ref[...] = -ADAM_LR * (m_hat / (jnp.sqrt(v_hat) + ADAM_EPS) + ADAM_WD * w_ref[...])
        nm_ref[...] = nm
        nv_ref[...] = nv

    spec = pl.BlockSpec((tr, cols), lambda i: (i, 0))
    out = jax.ShapeDtypeStruct((rows, cols), F32)
    d, nm, nv = pl.pallas_call(
        body, name=name, grid=(rows // tr,), in_specs=[spec] * 4, out_specs=[spec] * 3, out_shape=[out] * 3,
        compiler_params=_params(("parallel",)),
    )(w2, m2, v2, g2)
    return d.reshape(shape), nm.reshape(shape), nv.reshape(shape)


def _pack_rows(size, cols, align):
    return -(-size // (cols * align)) * align


def _pack(arrs, lead, cols, align, total_align):
    lead_shape = arrs[0].shape[:lead]
    parts = []
    for a in arrs:
        flat = a.reshape(lead_shape + (-1,))
        size = flat.shape[-1]
        rows = _pack_rows(size, cols, align)
        flat = jnp.pad(flat, [(0, 0)] * lead + [(0, rows * cols - size)])
        parts.append(flat.reshape(lead_shape + (rows, cols)))
    total = sum(p.shape[lead] for p in parts)
    extra = -total % total_align
    if extra:
        parts.append(jnp.zeros(lead_shape + (extra, cols), parts[0].dtype))
    return jnp.concatenate(parts, axis=lead)


def _unpack(buf, shapes, lead, cols, align):
    lead_shape = buf.shape[:lead]
    out, row = [], 0
    for shp in shapes:
        size = 1
        for n in shp:
            size *= n
        rows = _pack_rows(size, cols, align)
        piece = lax.slice_in_dim(buf, row, row + rows, axis=lead).reshape(lead_shape + (-1,))
        out.append(piece[..., :size].reshape(lead_shape + tuple(shp)))
        row += rows
    return out


BIG = ("ffn_w_in", "ffn_w_out", "a_w_qkv", "a_w_o", "kv_w", "b_w_q", "b_w_o")
SMALL = ("ffn_norm", "mix_norm", "a_q_norm", "a_k_norm", "kv_norm", "kv_b_f", "kv_k_norm", "b_q_norm")
WEIGHTS = ("ffn_norm", "ffn_w_in", "ffn_w_out", "mix_norm", "a_w_qkv", "a_q_norm", "a_k_norm", "a_w_o",
           "kv_norm", "kv_w", "kv_b_f", "kv_k_norm", "b_w_q", "b_q_norm", "b_w_o")
GATE_PAD = 2 * LANES


def _stack_weights(sh, d):
    depth = sh["ffn_w_in"].shape[1]
    kv = sh["kv_w"].transpose(1, 0, 2).reshape(d, -1)
    kv = jnp.pad(kv, ((0, 0), (0, 2 * d + GATE_PAD - kv.shape[1])))
    return {
        "ffn_w_in": [[sh["ffn_w_in"][:, l, i] for i in range(2)] for l in range(depth)],
        "ffn_w_out": [[sh["ffn_w_out"][:, l, i].reshape(1, -1, d) for i in range(2)] for l in range(depth)],
        "a_w_qkv": sh["a_w_qkv"][:, 0],
        "a_w_o": sh["a_w_o"].reshape(1, d, d),
        "kv_w": kv[None],
        "b_w_q": sh["b_w_q"].reshape(1, d, d),
        "b_w_o": sh["b_w_o"].reshape(1, d, d),
    }


def _unstack_grads(dw, d, heads):
    def rows4(z):
        return z.reshape(N_CHIPS, -1, d)

    kv_cols = 2 * d + heads
    kv = dw["kv_w"][0][:, :kv_cols].reshape(d, N_CHIPS, kv_cols // N_CHIPS).transpose(1, 0, 2)
    return [
        jnp.stack([jnp.stack(row, axis=1) for row in dw["ffn_w_in"]], axis=1),
        jnp.stack([jnp.stack([rows4(z) for z in row], axis=1) for row in dw["ffn_w_out"]], axis=1),
        dw["a_w_qkv"][:, None],
        rows4(dw["a_w_o"])[:, None],
        kv,
        rows4(dw["b_w_q"])[:, None],
        rows4(dw["b_w_o"])[:, None],
    ]


def kernel(x, positions, ffn_norm, ffn_w_in, ffn_w_out, mix_norm, a_w_qkv, a_q_norm, a_k_norm, a_w_o, kv_norm, kv_w, kv_b_f, kv_k_norm, b_w_q, b_q_norm, b_w_o, loss_target, m_ffn_norm, m_ffn_w_in, m_ffn_w_out, m_mix_norm, m_a_w_qkv, m_a_q_norm, m_a_k_norm, m_a_w_o, m_kv_norm, m_kv_w, m_kv_b_f, m_kv_k_norm, m_b_w_q, m_b_q_norm, m_b_w_o, v_ffn_norm, v_ffn_w_in, v_ffn_w_out, v_mix_norm, v_a_w_qkv, v_a_q_norm, v_a_k_norm, v_a_w_o, v_kv_norm, v_kv_w, v_kv_b_f, v_kv_k_norm, v_b_w_q, v_b_q_norm, v_b_w_o):
    wts = dict(zip(WEIGHTS, (ffn_norm, ffn_w_in, ffn_w_out, mix_norm, a_w_qkv, a_q_norm, a_k_norm, a_w_o, kv_norm,
                             kv_w, kv_b_f, kv_k_norm, b_w_q, b_q_norm, b_w_o)))
    mom = dict(zip(WEIGHTS, (m_ffn_norm, m_ffn_w_in, m_ffn_w_out, m_mix_norm, m_a_w_qkv, m_a_q_norm, m_a_k_norm,
                             m_a_w_o, m_kv_norm, m_kv_w, m_kv_b_f, m_kv_k_norm, m_b_w_q, m_b_q_norm, m_b_w_o)))
    var = dict(zip(WEIGHTS, (v_ffn_norm, v_ffn_w_in, v_ffn_w_out, v_mix_norm, v_a_w_qkv, v_a_q_norm, v_a_k_norm,
                             v_a_w_o, v_kv_norm, v_kv_w, v_kv_b_f, v_kv_k_norm, v_b_w_q, v_b_q_norm, v_b_w_o)))
    batch, seq, d = x.shape
    cfg = Cfg(d_model=d, d_ff=ffn_w_out.shape[2] * N_CHIPS, seq=seq, batch=batch)
    chip = 2 * lax.axis_index("x") + lax.axis_index("y")
    big_shapes = [wts[n].shape for n in BIG]

    shard = _pack([wts[n].astype(BF16) for n in BIG], 0, PACK_COLS, PACK_ROW_ALIGN, PACK_COLS)
    gathered = _all_gather_chips(shard, name="gather_weights")
    gathered = lax.dynamic_update_slice_in_dim(gathered, shard[None], chip, axis=0)
    w = _stack_weights(dict(zip(BIG, _unpack(gathered, big_shapes, 1, PACK_COLS, PACK_ROW_ALIGN))), d)
    norm_shard = _pack([ffn_norm], 0, LANES, SUBLANES, SUBLANES)
    norms = _all_gather_small(norm_shard, name="gather_ffn_norm")[0::2]
    (norms,) = _unpack(norms, [ffn_norm.shape], 1, LANES, SUBLANES)
    small = {"ffn_norm": jnp.moveaxis(norms, 0, 2).reshape(ffn_norm.shape[:2] + (d,)),
             "mix_norm": mix_norm, "a_q_norm": a_q_norm[0], "a_k_norm": a_k_norm[0], "kv_norm": kv_norm,
             "kv_b_f": kv_b_f, "kv_k_norm": kv_k_norm, "b_q_norm": b_q_norm}

    loss, dx, dw, ds = _local_step(cfg, x.reshape(cfg.tokens, d), positions.reshape(cfg.tokens),
                                   loss_target.reshape(cfg.tokens, d), w, small)
    loss = lax.psum(loss, ("x", "y", "c"))

    g = _pack(_unstack_grads(dw, d, cfg.heads), 1, PACK_COLS, PACK_ROW_ALIGN, PACK_COLS)
    chip_half = _add_own_half(g, _swap_halves(g, name="swap_halves"), name="add_halves")
    parts = _scatter_chips(chip_half, name="scatter_chips")
    parts = lax.dynamic_update_slice_in_dim(parts, lax.dynamic_slice_in_dim(chip_half, chip, 1, axis=0), chip, axis=0)
    mine = _sum_parts(parts, name="sum_chips")
    both = _join_halves(mine, name="join_halves")
    g_big = lax.dynamic_update_slice_in_dim(both, mine[None], lax.axis_index("c"), axis=0).reshape(g.shape[1:])
    grads = dict(zip(BIG, _unpack(g_big, big_shapes, 0, PACK_COLS, PACK_ROW_ALIGN)))

    small_shapes = [ds[n].shape for n in SMALL]
    parts = _all_gather_small(_pack([ds[n] for n in SMALL], 0, LANES, SUBLANES, SUBLANES), name="gather_small")
    g_small = dict(zip(SMALL, _unpack(_sum_parts(parts, name="sum_small"), small_shapes, 0, LANES, SUBLANES)))
    quarter = d // N_CHIPS
    g_small["ffn_norm"] = lax.dynamic_slice_in_dim(g_small["ffn_norm"], chip * quarter, quarter, axis=2)
    grads.update(g_small)

    delta, new_m, new_v = {}, {}, {}
    for n in BIG:
        delta[n], new_m[n], new_v[n] = _adamw(wts[n], mom[n], var[n], grads[n], name=f"adamw_{n}")
    packed = [_pack([z[n] for n in SMALL], 0, LANES, SUBLANES, SUBLANES) for z in (wts, mom, var, grads)]
    small_out = _adamw(*packed, name="adamw_small")
    shard_shapes = [wts[n].shape for n in SMALL]
    for out, res in zip((delta, new_m, new_v), small_out):
        out.update(zip(SMALL, _unpack(res, shard_shapes, 0, LANES, SUBLANES)))

    return (loss, dx.reshape(x.shape), *[grads[n] for n in WEIGHTS], *[delta[n] for n in WEIGHTS],
            *[new_m[n] for n in WEIGHTS], *[new_v[n] for n in WEIGHTS])
```

```python
import functools
from typing import NamedTuple

import jax
import jax.numpy as jnp
from jax import lax
from jax.experimental import pallas as pl
from jax.experimental.pallas import tpu as pltpu

F32 = jnp.float32
BF16 = jnp.bfloat16

HEAD_DIM = 64
LANES = 128
SUBLANES = 8
ROT_DIM = HEAD_DIM // 4
ROPE_THETA = 500000.0
NORM_EPS = 1e-6
BAND = 128
DILATIONS = (1, 4, 16)
NEG = -1e30
Q_SCALE = HEAD_DIM ** -0.5
N_CHIPS = 4
N_DEV = 8
VMEM_LIMIT = 48 * 1024 * 1024
WIDE = 1536
ROW_BLOCK = 512
FUSED_ROWS = 512
_NT = (((1,), (1,)), ((), ()))

ADAM_LR = 0.001
ADAM_B1 = 0.9
ADAM_B2 = 0.999
ADAM_EPS = 1e-08
ADAM_WD = 0.01
ADAM_STEP = 10


class Cfg(NamedTuple):
    d_model: int
    d_ff: int
    seq: int
    batch: int

    @property
    def heads(self):
        return self.d_model // HEAD_DIM

    @property
    def tokens(self):
        return self.batch * self.seq


def _params(sem):
    return pltpu.CompilerParams(dimension_semantics=sem, vmem_limit_bytes=VMEM_LIMIT)


def _blk(dim, want):
    if dim <= want:
        return dim
    for b in range(want // LANES * LANES, 0, -LANES):
        if dim % b == 0:
            return b
    b = want
    while dim % b:
        b //= 2
    return b


def _fold8(x):
    return jnp.sum(x.reshape(x.shape[0] // SUBLANES, SUBLANES, x.shape[1]), axis=0)


def _rms_bwd_tile(xv, g, dyv, dres):
    rstd = lax.rsqrt(jnp.mean(xv * xv, axis=-1, keepdims=True) + NORM_EPS)
    xhat = xv * rstd
    dyg = dyv * g
    proj = jnp.mean(dyg * xhat, axis=-1, keepdims=True)
    return dres + rstd * (dyg - xhat * proj), _fold8(dyv * xhat)


def _mm(a, b, *, form, out_dtype, name, bm=1024, bn=1024, bk=1024, res=None, scale=1.0, norms=(), norm_bwd=None):
    if form == "F":
        m, kdim = a.shape
        jn, _, ns = b.shape
        bm, bn, bk = _blk(m, bm), _blk(ns, bn), _blk(kdim, bk)
        npj = ns // bn
        grid = (m // bm, jn * npj, kdim // bk)
        a_spec = pl.BlockSpec((bm, bk), lambda i, n, k: (i, k))
        b_spec = pl.BlockSpec((None, bk, bn), lambda i, n, k: (n // npj, k, n % npj))
        o_spec = pl.BlockSpec((bm, bn), lambda i, n, k: (i, n))
        o_shape = jax.ShapeDtypeStruct((m, jn * ns), out_dtype)
        dims = (((1,), (0,)), ((), ()))
    elif form == "B":
        m = a.shape[0]
        jn, kdim, ns = b.shape
        bm, bn, bk = _blk(m, bm), _blk(kdim, bn), _blk(ns, bk)
        kpj = ns // bk
        grid = (m // bm, kdim // bn, jn * kpj)
        a_spec = pl.BlockSpec((bm, bk), lambda i, n, k: (i, k))
        b_spec = pl.BlockSpec((None, bn, bk), lambda i, n, k: (k // kpj, n, k % kpj))
        o_spec = pl.BlockSpec((bm, bn), lambda i, n, k: (i, n))
        o_shape = jax.ShapeDtypeStruct((m, kdim), out_dtype)
        dims = _NT
    else:
        raise ValueError(form)
    nk = grid[2]
    n_norms = len(norms)
    full_rows = grid[1] == 1
    assert full_rows or (not norms and norm_bwd is None)

    def body(*refs):
        a_ref, b_ref = refs[:2]
        pos = 2
        r_ref = None
        if res is not None:
            r_ref = refs[pos]
            pos += 1
        g_refs = refs[pos:pos + n_norms]
        pos += n_norms
        if norm_bwd is not None:
            x_ref, gb_ref, dres_ref = refs[pos:pos + 3]
            pos += 3
        o_ref = refs[pos]
        n_refs = refs[pos + 1:pos + 1 + n_norms]
        acc_ref = refs[-1]
        i, k = pl.program_id(0), pl.program_id(2)

        @pl.when(k == 0)
        def _():
            acc_ref[...] = jnp.zeros_like(acc_ref)

        acc_ref[...] += lax.dot_general(a_ref[...].astype(BF16), b_ref[...].astype(BF16), dims,
                                        preferred_element_type=F32)

        if norm_bwd is not None:
            dg_ref = refs[pos + 1 + n_norms]

            @pl.when((i == 0) & (k == 0))
            def _():
                dg_ref[...] = jnp.zeros_like(dg_ref)

        @pl.when(k == nk - 1)
        def _():
            r = acc_ref[...]
            if scale != 1.0:
                r = r * scale
            if r_ref is not None:
                r = r_ref[...] + r
            if norm_bwd is not None:
                dx, dg8 = _rms_bwd_tile(x_ref[...], gb_ref[...], r, dres_ref[...])
                o_ref[...] = dx
                dg_ref[...] += dg8
            else:
                o_ref[...] = r.astype(o_ref.dtype)
            if n_norms:
                rstd = lax.rsqrt(jnp.mean(r * r, axis=-1, keepdims=True) + NORM_EPS)
                for g_ref, n_ref in zip(g_refs, n_refs):
                    n_ref[...] = ((r * rstd) * g_ref[...]).astype(BF16)

    row = pl.BlockSpec((bm, bn), lambda i, n, k: (i, n))
    vec = pl.BlockSpec((1, bn), lambda i, n, k: (0, 0))
    in_specs = [a_spec, b_spec]
    args = [a, b]
    if res is not None:
        in_specs.append(row)
        args.append(res)
    for g in norms:
        in_specs.append(vec)
        args.append(g.reshape(1, -1))
    out_specs, out_shapes = [o_spec], [o_shape]
    for _ in norms:
        out_specs.append(row)
        out_shapes.append(jax.ShapeDtypeStruct(o_shape.shape, BF16))
    if norm_bwd is not None:
        x, g, dres = norm_bwd
        in_specs += [row, vec, row]
        args += [x, g.reshape(1, -1), dres]
        out_specs.append(pl.BlockSpec((SUBLANES, bn), lambda i, n, k: (0, 0)))
        out_shapes.append(jax.ShapeDtypeStruct((SUBLANES, o_shape.shape[1]), F32))
    sem = ("arbitrary",) * 3 if norm_bwd is not None else ("parallel", "parallel", "arbitrary")
    single = len(out_specs) == 1
    out = pl.pallas_call(
        body, name=name, grid=grid, in_specs=in_specs, out_specs=out_specs[0] if single else out_specs,
        out_shape=out_shapes[0] if single else out_shapes,
        scratch_shapes=[pltpu.VMEM((bm, bn), F32)],
        compiler_params=_params(sem),
    )(*args)
    if norm_bwd is not None:
        return out[0], jnp.sum(out[1], axis=0)
    return out


def _mm_grad(a, dy, jn, *, name, scale=1.0, bm=1024, bn=1024, bk=1024):
    halves = dy if isinstance(dy, (tuple, list)) else (dy,)
    t, kdim = a.shape
    ns = len(halves) * halves[0].shape[1] // jn
    bm, bn, bk = _blk(kdim, bm), _blk(ns, bn), _blk(t, bk)
    npj = ns // bn
    grid = (kdim // bm, jn * npj, t // bk)
    nk = grid[2]
    nhalf = jn * npj // len(halves)
    dims = (((0,), (0,)), ((), ()))

    def body(a_ref, *refs):
        b_refs, o_ref, acc_ref = refs[:len(halves)], refs[-2], refs[-1]
        n, k = pl.program_id(1), pl.program_id(2)

        @pl.when(k == 0)
        def _():
            acc_ref[...] = jnp.zeros_like(acc_ref)

        for which, b_ref in enumerate(b_refs):
            @pl.when(n // nhalf == which)
            def _(b_ref=b_ref):
                acc_ref[...] += lax.dot_general(a_ref[...].astype(BF16), b_ref[...].astype(BF16), dims,
                                                preferred_element_type=F32)

        @pl.when(k == nk - 1)
        def _():
            r = acc_ref[...]
            if scale != 1.0:
                r = r * scale
            o_ref[...] = r

    def half_spec(which):
        return pl.BlockSpec((bk, bn), lambda m, n, k: (jnp.where(n // nhalf == which, k, 0),
                                                        jnp.where(n // nhalf == which, n % nhalf, 0)))

    return pl.pallas_call(
        body, name=name, grid=grid,
        in_specs=[pl.BlockSpec((bk, bm), lambda m, n, k: (k, m))] + [half_spec(w) for w in range(len(halves))],
        out_specs=pl.BlockSpec((None, bm, bn), lambda m, n, k: (n // npj, m, n % npj)),
        out_shape=jax.ShapeDtypeStruct((jn, kdim, ns), F32),
        scratch_shapes=[pltpu.VMEM((bm, bn), F32)],
        compiler_params=_params(("parallel", "parallel", "arbitrary")),
    )(a, *halves)


def _mm_back2(dy_halves, w, norm_bwd, *, name, bm=FUSED_ROWS, bk=1024):
    x, g, dres = norm_bwd
    m = dy_halves[0].shape[0]
    jn, kdim, ns = w.shape
    bm, bk = _blk(m, bm), _blk(ns, bk)
    kpj = ns // bk
    nk = jn * kpj
    khalf = nk // 2

    def body(a0_ref, a1_ref, b_ref, x_ref, g_ref, dres_ref, o_ref, dg_ref, acc_ref):
        i, k = pl.program_id(0), pl.program_id(1)

        @pl.when(k == 0)
        def _():
            acc_ref[...] = jnp.zeros_like(acc_ref)

        @pl.when((i == 0) & (k == 0))
        def _():
            dg_ref[...] = jnp.zeros_like(dg_ref)

        for which, a_ref in enumerate((a0_ref, a1_ref)):
            @pl.when(k // khalf == which)
            def _(a_ref=a_ref):
                acc_ref[...] += lax.dot_general(a_ref[...], b_ref[...], _NT, preferred_element_type=F32)

        @pl.when(k == nk - 1)
        def _():
            dx, dg8 = _rms_bwd_tile(x_ref[...], g_ref[...], acc_ref[...], dres_ref[...])
            o_ref[...] = dx
            dg_ref[...] += dg8

    def half_spec(which):
        return pl.BlockSpec((bm, bk), lambda i, k: (i, jnp.clip(k - which * khalf, 0, khalf - 1)))

    row = pl.BlockSpec((bm, kdim), lambda i, k: (i, 0))
    dx, dg = pl.pallas_call(
        body, name=name, grid=(m // bm, nk),
        in_specs=[half_spec(0), half_spec(1),
                  pl.BlockSpec((None, kdim, bk), lambda i, k: (k // kpj, 0, k % kpj)),
                  row, pl.BlockSpec((1, kdim), lambda i, k: (0, 0)), row],
        out_specs=[row, pl.BlockSpec((SUBLANES, kdim), lambda i, k: (0, 0))],
        out_shape=[jax.ShapeDtypeStruct((m, kdim), F32), jax.ShapeDtypeStruct((SUBLANES, kdim), F32)],
        scratch_shapes=[pltpu.VMEM((bm, kdim), F32)],
        compiler_params=_params(("arbitrary", "arbitrary")),
    )(dy_halves[0], dy_halves[1], w, x, g.reshape(1, -1), dres)
    return dx, jnp.sum(dg, axis=0)


def _ffn_in_act(n, w_in, *, name, bm=512):
    m, kdim = n.shape
    jn, _, ns = w_in.shape
    f = jn * ns // 2
    bm = _blk(m, bm)
    bn = _blk(ns, WIDE)
    npj = ns // bn
    nf = f // bn

    def body(n_ref, wg_ref, wu_ref, g_ref, u_ref, a_ref):
        nv = n_ref[...]
        g = jnp.dot(nv, wg_ref[...], preferred_element_type=F32)
        u = jnp.dot(nv, wu_ref[...], preferred_element_type=F32)
        g_ref[...] = g.astype(BF16)
        u_ref[...] = u.astype(BF16)
        a_ref[...] = (g * jax.nn.sigmoid(g) * u).astype(BF16)

    out = jax.ShapeDtypeStruct((m, f), BF16)
    ospec = pl.BlockSpec((bm, bn), lambda c, i: (i, c))
    return pl.pallas_call(
        body, name=name, grid=(nf, m // bm),
        in_specs=[pl.BlockSpec((bm, kdim), lambda c, i: (i, 0)),
                  pl.BlockSpec((None, kdim, bn), lambda c, i: (c // npj, 0, c % npj)),
                  pl.BlockSpec((None, kdim, bn), lambda c, i: ((c + nf) // npj, 0, (c + nf) % npj))],
        out_specs=[ospec, ospec, ospec], out_shape=[out, out, out],
        compiler_params=_params(("parallel", "parallel")),
    )(n, w_in, w_in)


def _ffn_out_dx_act(dh, w_out, gate, up, *, name, scale, bm=512):
    m, d = dh.shape
    f = w_out.shape[1]
    bm = _blk(m, bm)
    bn = _blk(f, WIDE)

    def body(dh_ref, w_ref, g_ref, u_ref, dg_ref, du_ref):
        da = lax.dot_general(dh_ref[...].astype(BF16), w_ref[...], _NT, preferred_element_type=F32) * scale
        g = g_ref[...].astype(F32)
        sg = jax.nn.sigmoid(g)
        silu = g * sg
        dg_ref[...] = (da * u_ref[...].astype(F32) * (sg + silu * (1.0 - sg))).astype(BF16)
        du_ref[...] = (da * silu).astype(BF16)

    out = jax.ShapeDtypeStruct((m, f), BF16)
    spec = pl.BlockSpec((bm, bn), lambda c, i: (i, c))
    return pl.pallas_call(
        body, name=name, grid=(f // bn, m // bm),
        in_specs=[pl.BlockSpec((bm, d), lambda c, i: (i, 0)), pl.BlockSpec((None, bn, d), lambda c, i: (0, c, 0)),
                  spec, spec],
        out_specs=[spec, spec], out_shape=[out, out],
        compiler_params=_params(("parallel", "parallel")),
    )(dh, w_out, gate, up)


def _rms_fwd(x, g, *, name):
    t, d = x.shape
    tr = _blk(t, ROW_BLOCK)

    def body(x_ref, g_ref, o_ref):
        xv = x_ref[...]
        rstd = lax.rsqrt(jnp.mean(xv * xv, axis=-1, keepdims=True) + NORM_EPS)
        o_ref[...] = ((xv * rstd) * g_ref[...]).astype(BF16)

    return pl.pallas_call(
        body, name=name, grid=(t // tr,),
        in_specs=[pl.BlockSpec((tr, d), lambda i: (i, 0)), pl.BlockSpec((1, d), lambda i: (0, 0))],
        out_specs=pl.BlockSpec((tr, d), lambda i: (i, 0)),
        out_shape=jax.ShapeDtypeStruct((t, d), BF16),
        compiler_params=_params(("parallel",)),
    )(x, g.reshape(1, d))


def _loss_fwd_bwd(h, target, *, name):
    t, d = h.shape
    tr = _blk(t, ROW_BLOCK)

    def body(h_ref, t_ref, dh_ref, l_ref):
        i = pl.program_id(0)
        err = h_ref[...] - t_ref[...]
        dh_ref[...] = err * (1.0 / d)

        @pl.when(i == 0)
        def _():
            l_ref[...] = jnp.zeros_like(l_ref)

        l_ref[...] += _fold8(err * err)

    dh, part = pl.pallas_call(
        body, name=name, grid=(t // tr,),
        in_specs=[pl.BlockSpec((tr, d), lambda i: (i, 0)), pl.BlockSpec((tr, d), lambda i: (i, 0))],
        out_specs=[pl.BlockSpec((tr, d), lambda i: (i, 0)), pl.BlockSpec((SUBLANES, d), lambda i: (0, 0))],
        out_shape=[jax.ShapeDtypeStruct((t, d), F32), jax.ShapeDtypeStruct((SUBLANES, d), F32)],
        compiler_params=_params(("arbitrary",)),
    )(h, target)
    return jnp.sum(part) * (0.5 / d), dh


def _seg_matrix():
    r = lax.broadcasted_iota(jnp.int32, (LANES, LANES), 0) // HEAD_DIM
    c = lax.broadcasted_iota(jnp.int32, (LANES, LANES), 1) // HEAD_DIM
    return (r == c).astype(BF16)


def _head_sum(x, seg, terms=3):
    hi = x.astype(BF16)
    r1 = x - hi.astype(F32)
    mid = r1.astype(BF16)
    dot = functools.partial(jnp.dot, preferred_element_type=F32)
    if terms == 2:
        return dot(hi, seg) + dot(mid, seg)
    lo = (r1 - mid.astype(F32)).astype(BF16)
    return dot(hi, seg) + dot(mid, seg) + dot(lo, seg)


def _lane_in_head(shape):
    return lax.broadcasted_iota(jnp.int32, shape, 1) % HEAD_DIM


def _half_mask(shape):
    return lax.broadcasted_iota(jnp.int32, shape, 1) < HEAD_DIM


def _rot_partner(x):
    up = pltpu.roll(x, LANES - ROT_DIM // 2, 1)
    down = pltpu.roll(x, ROT_DIM // 2, 1)
    return jnp.where(_lane_in_head(x.shape) < ROT_DIM // 2, up, down)


def _rope_tables(positions):
    inv_freq = ROPE_THETA ** (-jnp.arange(0, ROT_DIM, 2, dtype=F32) / ROT_DIM)
    ang = positions.astype(F32)[:, None] * inv_freq
    t = ang.shape[0]
    rest = HEAD_DIM - ROT_DIM
    cos = jnp.concatenate([jnp.cos(ang), jnp.cos(ang), jnp.ones((t, rest), F32)], axis=1)
    sin = jnp.concatenate([-jnp.sin(ang), jnp.sin(ang), jnp.zeros((t, rest), F32)], axis=1)
    return jnp.tile(cos, (1, LANES // HEAD_DIM)), jnp.tile(sin, (1, LANES // HEAD_DIM))


def _kind_is(j, kinds, kind):
    hits = [j == jj for jj, k in enumerate(kinds) if k == kind]
    return functools.reduce(jnp.logical_or, hits) if hits else None


def _hn_fwd(x, gains, kinds, d, cos, sin, *, name, col0=0):
    t = x.shape[0]
    n = len(kinds)
    tr = _blk(t, ROW_BLOCK)
    seg = _seg_matrix()
    g8 = jnp.repeat(gains.astype(F32), SUBLANES, axis=0)

    def body(x_ref, g_ref, seg_ref, cos_ref, sin_ref, o_ref):
        j = pl.program_id(1)

        def normed(rope):
            for c in range(d // LANES):
                sl = slice(c * LANES, (c + 1) * LANES)
                xv = x_ref[:, sl]
                ms = _head_sum(xv * xv, seg_ref[...], terms=2) * (1.0 / HEAD_DIM)
                y = (xv * lax.rsqrt(ms + NORM_EPS)) * g_ref[0:1, sl]
                if rope:
                    y = y * cos_ref[...] + _rot_partner(y) * sin_ref[...]
                o_ref[:, sl] = y.astype(BF16)

        for kind in ("rope", "norm"):
            hit = _kind_is(j, kinds, kind)
            if hit is not None:
                pl.when(hit)(functools.partial(normed, kind == "rope"))
        hit = _kind_is(j, kinds, "cast")
        if hit is not None:
            @pl.when(hit)
            def _():
                o_ref[...] = x_ref[...].astype(BF16)

    return pl.pallas_call(
        body, name=name, grid=(t // tr, n),
        in_specs=[pl.BlockSpec((tr, d), lambda i, j: (i, col0 + j)), pl.BlockSpec((SUBLANES, d), lambda i, j: (j, 0)),
                  pl.BlockSpec((LANES, LANES), lambda i, j: (0, 0)),
                  pl.BlockSpec((tr, LANES), lambda i, j: (i, 0)), pl.BlockSpec((tr, LANES), lambda i, j: (i, 0))],
        out_specs=pl.BlockSpec((tr, d), lambda i, j: (i, j)),
        out_shape=jax.ShapeDtypeStruct((t, n * d), BF16),
        compiler_params=_params(("parallel", "parallel")),
    )(x, g8, seg, cos, sin)


def _hn_bwd(x, dys, gains, kinds, d, cos, sin, *, name, col0=0):
    t = x.shape[0]
    n = len(kinds)
    tr = _blk(t, ROW_BLOCK // 2)
    seg = _seg_matrix()
    g8 = jnp.repeat(gains.astype(F32), SUBLANES, axis=0)

    def body(x_ref, *refs):
        dy_refs = refs[:n]
        g_ref, seg_ref, cos_ref, sin_ref, dx_ref, dg_ref = refs[n:]
        j = pl.program_id(0)
        i = pl.program_id(1)

        @pl.when(i == 0)
        def _():
            dg_ref[...] = jnp.zeros_like(dg_ref)

        def normed(rope, dy_ref):
            for c in range(d // LANES):
                sl = slice(c * LANES, (c + 1) * LANES)
                xv = x_ref[:, sl]
                dyv = dy_ref[:, sl]
                if rope:
                    dyv = dyv * cos_ref[...] - _rot_partner(dyv) * sin_ref[...]
                ms = _head_sum(xv * xv, seg_ref[...], terms=2) * (1.0 / HEAD_DIM)
                rstd = lax.rsqrt(ms + NORM_EPS)
                xhat = xv * rstd
                dg_ref[:, sl] += _fold8(dyv * xhat)
                dyg = dyv * g_ref[0:1, sl]
                proj = _head_sum(dyg * xhat, seg_ref[...], terms=2) * (1.0 / HEAD_DIM)
                dx_ref[:, sl] = (rstd * (dyg - xhat * proj)).astype(BF16)

        def cast(dy_ref):
            dx_ref[...] = dy_ref[...].astype(BF16)

        for jj, kind in enumerate(kinds):
            if kind == "cast":
                pl.when(j == jj)(functools.partial(cast, dy_refs[jj]))
            else:
                pl.when(j == jj)(functools.partial(normed, kind == "rope", dy_refs[jj]))

    def dy_spec(jj):
        return pl.BlockSpec((tr, d), lambda j, i: (jnp.where(j == jj, i, 0), 0))

    dx, dg = pl.pallas_call(
        body, name=name, grid=(n, t // tr),
        in_specs=[pl.BlockSpec((tr, d), lambda j, i: (i, col0 + j))] + [dy_spec(jj) for jj in range(n)] + [
                  pl.BlockSpec((SUBLANES, d), lambda j, i: (j, 0)),
                  pl.BlockSpec((LANES, LANES), lambda j, i: (0, 0)),
                  pl.BlockSpec((tr, LANES), lambda j, i: (i, 0)), pl.BlockSpec((tr, LANES), lambda j, i: (i, 0))],
        out_specs=[pl.BlockSpec((tr, d), lambda j, i: (i, j)), pl.BlockSpec((SUBLANES, d), lambda j, i: (j, 0))],
        out_shape=[jax.ShapeDtypeStruct((t, n * d), BF16), jax.ShapeDtypeStruct((n * SUBLANES, d), F32)],
        compiler_params=_params(("arbitrary", "arbitrary")),
    )(x, *dys, g8, seg, cos, sin)
    dg = dg.reshape(n, SUBLANES, d // HEAD_DIM, HEAD_DIM).sum(axis=(1, 2))
    return dx, dg


def _band_valid_t(first):
    s = lax.broadcasted_iota(jnp.int32, (2 * BAND, BAND), 0)
    t = lax.broadcasted_iota(jnp.int32, (2 * BAND, BAND), 1)
    dist = t + BAND - s
    return (dist >= 0) & (dist <= BAND) & ((s >= BAND) | jnp.logical_not(first))


def _to_classes_t(z, dil, width):
    return z.reshape(z.shape[0] // dil, dil, width).transpose(1, 2, 0)


def _from_classes_t(z):
    dil, width, rows = z.shape
    return z.transpose(2, 0, 1).reshape(rows * dil, width)


def _band_fwd_n(nat, dil, cfg, *, name):
    d, hh = cfg.d_model, cfg.heads
    rows = cfg.tokens // dil
    nbt = rows // BAND
    nb = cfg.seq // (dil * BAND)

    def body(q_ref, kp_ref, kc_ref, vp_ref, vc_ref, o_ref, lse_ref):
        i = pl.program_id(1)
        valid = _band_valid_t(i % nb == 0)
        upper = lax.broadcasted_iota(jnp.int32, (LANES, BAND), 0) < HEAD_DIM
        for hp in range(d // LANES):
            pair = slice(hp * LANES, (hp + 1) * LANES)
            qt2 = q_ref[:, pair].T
            kk = jnp.concatenate([kp_ref[:, pair], kc_ref[:, pair]], axis=0)
            vvt = jnp.concatenate([vp_ref[:, pair], vc_ref[:, pair]], axis=0).T
            outs = []
            for e in range(2):
                h = 2 * hp + e
                qte = jnp.where(upper == (e == 0), qt2, jnp.zeros_like(qt2))
                s = jnp.where(valid, jnp.dot(kk, qte, preferred_element_type=F32), NEG)
                m = jnp.max(s, axis=0, keepdims=True)
                p = jnp.exp(s - m)
                l = jnp.sum(p, axis=0, keepdims=True)
                hi = p.astype(BF16)
                lo = (p - hi.astype(F32)).astype(BF16)
                vt = vvt[e * HEAD_DIM:(e + 1) * HEAD_DIM]
                o = jnp.dot(vt, hi, preferred_element_type=F32) + jnp.dot(vt, lo, preferred_element_type=F32)
                outs.append(o * (1.0 / l))
                lse_ref[h:h + 1, :] = m + jnp.log(l)
            o_ref[:, pair] = jnp.concatenate(outs, axis=0).T

    def prev(i):
        return jnp.maximum(i - 1, 0)

    blk = (BAND, d)
    return pl.pallas_call(
        body, name=name, grid=(dil, nbt),
        in_specs=[pl.BlockSpec(blk, lambda r, i: (i, r * 3)),
                  pl.BlockSpec(blk, lambda r, i: (prev(i), r * 3 + 1)),
                  pl.BlockSpec(blk, lambda r, i: (i, r * 3 + 1)),
                  pl.BlockSpec(blk, lambda r, i: (prev(i), r * 3 + 2)),
                  pl.BlockSpec(blk, lambda r, i: (i, r * 3 + 2))],
        out_specs=[pl.BlockSpec(blk, lambda r, i: (i, r)),
                   pl.BlockSpec((None, hh, BAND), lambda r, i: (r, 0, i))],
        out_shape=[jax.ShapeDtypeStruct((rows, dil * d), F32), jax.ShapeDtypeStruct((dil, hh, rows), F32)],
        compiler_params=_params(("parallel", "arbitrary")),
    )(nat, nat, nat, nat, nat)


def _band_bwd_n(nat, do_nat, lse_c, dsum_c, dil, cfg, *, name):
    d, hh = cfg.d_model, cfg.heads
    rows = cfg.tokens // dil
    nbt = rows // BAND
    nb = cfg.seq // (dil * BAND)

    def body(q_ref, kp_ref, kc_ref, vp_ref, vc_ref, do_ref, l_ref, ds_ref, dq_ref, dk_ref, dv_ref, ck_ref, cv_ref):
        i = pl.program_id(1)

        @pl.when(i < nbt)
        def _():
            @pl.when(i == 0)
            def _():
                ck_ref[...] = jnp.zeros_like(ck_ref)
                cv_ref[...] = jnp.zeros_like(cv_ref)

            valid1 = _band_valid_t(i % nb == 0)
            valid = jnp.concatenate([valid1, valid1], axis=1)
            upper = lax.broadcasted_iota(jnp.int32, (LANES, BAND), 0) < HEAD_DIM
            half2 = _half_mask((2 * BAND, LANES))

            def both(z):
                zero = jnp.zeros_like(z)
                return jnp.concatenate([jnp.where(upper, z, zero), jnp.where(upper, zero, z)], axis=1)

            def stack(z):
                return jnp.concatenate([z[:, :BAND], z[:, BAND:]], axis=0)

            for hp in range(d // LANES):
                pair = slice(hp * LANES, (hp + 1) * LANES)
                h0, h1 = 2 * hp, 2 * hp + 1
                qn2, don2 = q_ref[:, pair], do_ref[:, pair]
                kk = jnp.concatenate([kp_ref[:, pair], kc_ref[:, pair]], axis=0)
                vv = jnp.concatenate([vp_ref[:, pair], vc_ref[:, pair]], axis=0)
                lse2 = jnp.concatenate([l_ref[h0:h0 + 1, :], l_ref[h1:h1 + 1, :]], axis=1)
                dsum2 = jnp.concatenate([ds_ref[h0:h0 + 1, :], ds_ref[h1:h1 + 1, :]], axis=1)
                s = jnp.where(valid, jnp.dot(kk, both(qn2.T), preferred_element_type=F32), NEG)
                p = jnp.exp(s - lse2)
                dp = jnp.dot(vv, both(don2.T), preferred_element_type=F32)
                dsb = (p * (dp - dsum2)).astype(BF16)
                dq2 = jnp.dot(kk.T, dsb, preferred_element_type=F32)
                dq_ref[:, pair] = jnp.concatenate([dq2[:HEAD_DIM, :BAND], dq2[HEAD_DIM:, BAND:]], axis=0).T
                dk2 = jnp.dot(stack(dsb), qn2, preferred_element_type=F32)
                dv2 = jnp.dot(stack(p.astype(BF16)), don2, preferred_element_type=F32)
                dkk = jnp.where(half2, dk2[:2 * BAND], dk2[2 * BAND:])
                dvv = jnp.where(half2, dv2[:2 * BAND], dv2[2 * BAND:])
                dk_ref[:, pair] = ck_ref[:, pair] + dkk[:BAND]
                dv_ref[:, pair] = cv_ref[:, pair] + dvv[:BAND]
                ck_ref[:, pair] = dkk[BAND:]
                cv_ref[:, pair] = dvv[BAND:]

        @pl.when(i == nbt)
        def _():
            dk_ref[...] = ck_ref[...]
            dv_ref[...] = cv_ref[...]

    def cur(i):
        return jnp.minimum(i, nbt - 1)

    def prev(i):
        return jnp.maximum(cur(i) - 1, 0)

    cblk = (None, hh, BAND)
    blk = (BAND, d)
    here = pl.BlockSpec(blk, lambda r, i: (cur(i), r))
    behind = pl.BlockSpec(blk, lambda r, i: (jnp.maximum(i - 1, 0), r))
    shape = jax.ShapeDtypeStruct((rows, dil * d), F32)
    return pl.pallas_call(
        body, name=name, grid=(dil, nbt + 1),
        in_specs=[pl.BlockSpec(blk, lambda r, i: (cur(i), r * 3)),
                  pl.BlockSpec(blk, lambda r, i: (prev(i), r * 3 + 1)),
                  pl.BlockSpec(blk, lambda r, i: (cur(i), r * 3 + 1)),
                  pl.BlockSpec(blk, lambda r, i: (prev(i), r * 3 + 2)),
                  pl.BlockSpec(blk, lambda r, i: (cur(i), r * 3 + 2)),
                  here,
                  pl.BlockSpec(cblk, lambda r, i: (r, 0, cur(i))),
                  pl.BlockSpec(cblk, lambda r, i: (r, 0, cur(i)))],
        out_specs=[here, behind, behind],
        out_shape=[shape, shape, shape],
        scratch_shapes=[pltpu.VMEM(blk, F32), pltpu.VMEM(blk, F32)],
        compiler_params=_params(("arbitrary", "arbitrary")),
    )(nat, nat, nat, nat, nat, do_nat, lse_c, dsum_c)


def _mix_fwd(outs, lses, *, name):
    t, d = outs[0].shape
    hh = lses[0].shape[1]
    tr = _blk(t, ROW_BLOCK)
    ng = len(outs)
    spread = (lax.broadcasted_iota(jnp.int32, (hh, d), 0)
              == lax.broadcasted_iota(jnp.int32, (hh, d), 1) // HEAD_DIM).astype(BF16)

    def body(*refs):
        o_refs, l_refs = refs[:ng], refs[ng:2 * ng]
        spread_ref, mixed_ref, lse_ref = refs[2 * ng:]
        ls = [r[...] for r in l_refs]
        m = functools.reduce(jnp.maximum, ls)
        es = [jnp.exp(l - m) for l in ls]
        tot = functools.reduce(jnp.add, es)
        inv = 1.0 / tot
        mixed_ref[...] = functools.reduce(
            jnp.add, [_head_sum(e * inv, spread_ref[...]) * r[...] for e, r in zip(es, o_refs)])
        lse_ref[...] = m + jnp.log(tot)

    spec = pl.BlockSpec((tr, d), lambda i: (i, 0))
    cspec = pl.BlockSpec((tr, hh), lambda i: (i, 0))
    return pl.pallas_call(
        body, name=name, grid=(t // tr,),
        in_specs=[spec] * ng + [cspec] * ng + [pl.BlockSpec((hh, d), lambda i: (0, 0))], out_specs=[spec, cspec],
        out_shape=[jax.ShapeDtypeStruct((t, d), F32), jax.ShapeDtypeStruct((t, hh), F32)],
        compiler_params=_params(("parallel",)),
    )(*outs, *lses, spread)


GATE_BLOCK = 256


def _tri(n, upper):
    r = lax.broadcasted_iota(jnp.int32, (n, n), 0)
    c = lax.broadcasted_iota(jnp.int32, (n, n), 1)
    return ((c >= r) if upper else (c <= r)).astype(BF16)


def _tri_dot(tri, x):
    hi = x.astype(BF16)
    r1 = x - hi.astype(F32)
    mid = r1.astype(BF16)
    lo = (r1 - mid.astype(F32)).astype(BF16)
    dot = functools.partial(jnp.dot, preferred_element_type=F32)
    return dot(tri, hi) + dot(tri, mid) + dot(tri, lo)


def _log_sigmoid(z):
    return jnp.minimum(z, 0.0) - jnp.log(1.0 + jnp.exp(-jnp.abs(z)))


def _gate_fwd(proj, col_block, bias, cfg, *, name):
    tr = _blk(cfg.seq, GATE_BLOCK)
    nblk = cfg.seq // tr

    def body(z_ref, b_ref, tri_ref, o_ref, carry_ref):
        i = pl.program_id(1)

        @pl.when(i == 0)
        def _():
            carry_ref[...] = jnp.zeros_like(carry_ref)

        logf = _log_sigmoid(z_ref[...] + b_ref[0:1, :])
        cum = _tri_dot(tri_ref[...], logf) + carry_ref[0:1, :]
        o_ref[...] = cum
        carry_ref[...] = jnp.broadcast_to(cum[tr - 1:tr, :], carry_ref.shape)

    return pl.pallas_call(
        body, name=name, grid=(cfg.batch, nblk),
        in_specs=[pl.BlockSpec((tr, LANES), lambda b, i: (b * nblk + i, col_block)),
                  pl.BlockSpec((SUBLANES, LANES), lambda b, i: (0, 0)),
                  pl.BlockSpec((tr, tr), lambda b, i: (0, 0))],
        out_specs=pl.BlockSpec((tr, LANES), lambda b, i: (b * nblk + i, 0)),
        out_shape=jax.ShapeDtypeStruct((cfg.tokens, LANES), F32),
        scratch_shapes=[pltpu.VMEM((SUBLANES, LANES), F32)],
        compiler_params=_params(("arbitrary", "arbitrary")),
    )(proj, jnp.broadcast_to(bias, (SUBLANES, LANES)), _tri(tr, upper=False))


def _gate_bwd(proj, col_block, bias, dcum, cfg, *, name):
    tr = _blk(cfg.seq, GATE_BLOCK)
    nblk = cfg.seq // tr

    def body(z_ref, b_ref, tri_ref, dc_ref, dz_ref, db_ref, carry_ref):
        b = pl.program_id(0)
        i = pl.program_id(1)

        @pl.when(i == 0)
        def _():
            carry_ref[...] = jnp.zeros_like(carry_ref)

        @pl.when((i == 0) & (b == 0))
        def _():
            db_ref[...] = jnp.zeros_like(db_ref)

        dcv = dc_ref[...]
        dlogf = _tri_dot(tri_ref[...], dcv) + carry_ref[0:1, :]
        carry_ref[...] = jnp.broadcast_to(dlogf[0:1, :], carry_ref.shape)
        dz = dlogf * jax.nn.sigmoid(-(z_ref[...] + b_ref[0:1, :]))
        dz_ref[...] = dz
        db_ref[...] += _fold8(dz)

    def rev(b, i):
        return (b * nblk + nblk - 1 - i, 0)

    dz, db = pl.pallas_call(
        body, name=name, grid=(cfg.batch, nblk),
        in_specs=[pl.BlockSpec((tr, LANES), lambda b, i: (b * nblk + nblk - 1 - i, col_block)),
                  pl.BlockSpec((SUBLANES, LANES), lambda b, i: (0, 0)),
                  pl.BlockSpec((tr, tr), lambda b, i: (0, 0)),
                  pl.BlockSpec((tr, LANES), rev)],
        out_specs=[pl.BlockSpec((tr, LANES), rev), pl.BlockSpec((SUBLANES, LANES), lambda b, i: (0, 0))],
        out_shape=[jax.ShapeDtypeStruct((cfg.tokens, LANES), F32), jax.ShapeDtypeStruct((SUBLANES, LANES), F32)],
        scratch_shapes=[pltpu.VMEM((SUBLANES, LANES), F32)],
        compiler_params=_params(("arbitrary", "arbitrary")),
    )(proj, jnp.broadcast_to(bias, (SUBLANES, LANES)), _tri(tr, upper=True), dcum)
    return dz, jnp.sum(db, axis=0)


FOX_BLOCK = 256
AUG = LANES
BIAS_TERMS = 3


def _fox_aug_k_call(kv, cum, cfg, *, name):
    t, d, hh = cfg.tokens, cfg.d_model, cfg.heads
    tr = _blk(t, ROW_BLOCK)

    def body(k_ref, c_ref, o_ref):
        lane = lax.broadcasted_iota(jnp.int32, (tr, LANES), 1)
        for hp in range(hh // 2):
            k2 = k_ref[:, hp * LANES:(hp + 1) * LANES].astype(F32)
            for e in range(2):
                h = 2 * hp + e
                kh = k2 if e == 0 else pltpu.roll(k2, HEAD_DIM, 1)
                c = -c_ref[:, h:h + 1]
                hi = c.astype(BF16).astype(F32)
                mid = (c - hi).astype(BF16).astype(F32)
                lo = c - hi - mid
                bias = jnp.where(lane == HEAD_DIM, hi, jnp.where(lane == HEAD_DIM + 1, mid,
                                 jnp.where(lane == HEAD_DIM + 2, lo, 0.0)))
                o_ref[:, h * AUG:(h + 1) * AUG] = jnp.where(lane < HEAD_DIM, kh, bias).astype(BF16)

    return pl.pallas_call(
        body, name=name, grid=(t // tr,),
        in_specs=[pl.BlockSpec((tr, d), lambda i: (i, 0)), pl.BlockSpec((tr, LANES), lambda i: (i, 0))],
        out_specs=pl.BlockSpec((tr, hh * AUG), lambda i: (i, 0)),
        out_shape=jax.ShapeDtypeStruct((t, hh * AUG), BF16),
        compiler_params=_params(("parallel",)),
    )(kv, cum)


def _keys_visible(tq):
    s = lax.broadcasted_iota(jnp.int32, (tq, tq), 0)
    t = lax.broadcasted_iota(jnp.int32, (tq, tq), 1)
    return s <= t


def _aug_q_t(q2, e, tq):
    ones = (lax.broadcasted_iota(jnp.int32, (AUG - HEAD_DIM, tq), 0) < BIAS_TERMS).astype(q2.dtype)
    return jnp.concatenate([q2[e * HEAD_DIM:(e + 1) * HEAD_DIM], ones], axis=0)


def _fox_fwd_n(q, k_aug, kv, cfg, *, name):
    t, d, hh = cfg.tokens, cfg.d_model, cfg.heads
    tq = _blk(cfg.seq, FOX_BLOCK)
    nq = cfg.seq // tq

    def body(q_ref, ka_ref, v_ref, o_ref, lse_ref, qa_ref, m_ref, l_ref, acc_ref):
        qi, ki = pl.program_id(1), pl.program_id(2)

        @pl.when(ki == 0)
        def _():
            m_ref[...] = jnp.full_like(m_ref, NEG)
            l_ref[...] = jnp.zeros_like(l_ref)
            acc_ref[...] = jnp.zeros_like(acc_ref)
            for hp in range(hh // 2):
                q2 = q_ref[:, hp * LANES:(hp + 1) * LANES].T
                for e in range(2):
                    h = 2 * hp + e
                    qa_ref[h * AUG:(h + 1) * AUG, :] = _aug_q_t(q2, e, tq)

        def step(diagonal):
            for hp in range(hh // 2):
                vt2 = v_ref[:, hp * LANES:(hp + 1) * LANES].T
                for e in range(2):
                    h = 2 * hp + e
                    rows = slice(h * HEAD_DIM, (h + 1) * HEAD_DIM)
                    s = jnp.dot(ka_ref[:, h * AUG:(h + 1) * AUG], qa_ref[h * AUG:(h + 1) * AUG, :],
                                preferred_element_type=F32)
                    if diagonal:
                        s = jnp.where(_keys_visible(tq), s, NEG)
                    m_prev = m_ref[h:h + 1, :]
                    m_new = jnp.maximum(m_prev, jnp.max(s, axis=0, keepdims=True))
                    alpha = jnp.exp(m_prev - m_new)
                    p = jnp.exp(s - m_new)
                    l_ref[h:h + 1, :] = alpha * l_ref[h:h + 1, :] + jnp.sum(p, axis=0, keepdims=True)
                    m_ref[h:h + 1, :] = m_new
                    hi = p.astype(BF16)
                    lo = (p - hi.astype(F32)).astype(BF16)
                    vt = vt2[e * HEAD_DIM:(e + 1) * HEAD_DIM]
                    acc_ref[rows, :] = (alpha * acc_ref[rows, :] + jnp.dot(vt, hi, preferred_element_type=F32)
                                        + jnp.dot(vt, lo, preferred_element_type=F32))

        pl.when(ki < qi)(functools.partial(step, False))
        pl.when(ki == qi)(functools.partial(step, True))

        @pl.when(ki == qi)
        def _():
            for hp in range(hh // 2):
                halves = [acc_ref[h * HEAD_DIM:(h + 1) * HEAD_DIM, :] * (1.0 / l_ref[h:h + 1, :])
                          for h in (2 * hp, 2 * hp + 1)]
                o_ref[:, hp * LANES:(hp + 1) * LANES] = jnp.concatenate(halves, axis=0).T
            lse_ref[...] = m_ref[...] + jnp.log(l_ref[...])

    def qrow(b, qi, ki):
        return (b * nq + qi, 0)

    return pl.pallas_call(
        body, name=name, grid=(cfg.batch, nq, nq),
        in_specs=[pl.BlockSpec((tq, d), qrow),
                  pl.BlockSpec((tq, hh * AUG), lambda b, qi, ki: (b * nq + jnp.minimum(ki, qi), 0)),
                  pl.BlockSpec((tq, d), lambda b, qi, ki: (b * nq + jnp.minimum(ki, qi), 1))],
        out_specs=[pl.BlockSpec((tq, d), qrow), pl.BlockSpec((hh, tq), lambda b, qi, ki: (0, b * nq + qi))],
        out_shape=[jax.ShapeDtypeStruct((t, d), F32), jax.ShapeDtypeStruct((hh, t), F32)],
        scratch_shapes=[pltpu.VMEM((hh * AUG, tq), BF16), pltpu.VMEM((hh, tq), F32), pltpu.VMEM((hh, tq), F32),
                        pltpu.VMEM((d, tq), F32)],
        compiler_params=_params(("parallel", "parallel", "arbitrary")),
    )(q, k_aug, kv)


def _head_dot_c(a, b, cfg, *, name):
    t, d, hh = cfg.tokens, cfg.d_model, cfg.heads
    tc = _blk(t, ROW_BLOCK)

    def body(a_ref, b_ref, o_ref):
        for hp in range(hh // 2):
            pair = slice(hp * LANES, (hp + 1) * LANES)
            prod = (a_ref[:, pair].astype(F32) * b_ref[:, pair]).T
            for e in range(2):
                h = 2 * hp + e
                o_ref[h:h + 1, :] = jnp.sum(prod[e * HEAD_DIM:(e + 1) * HEAD_DIM], axis=0, keepdims=True)

    return pl.pallas_call(
        body, name=name, grid=(t // tc,),
        in_specs=[pl.BlockSpec((tc, d), lambda i: (i, 0)), pl.BlockSpec((tc, d), lambda i: (i, 0))],
        out_specs=pl.BlockSpec((hh, tc), lambda i: (0, i)),
        out_shape=jax.ShapeDtypeStruct((hh, t), F32),
        compiler_params=_params(("parallel",)),
    )(a, b)


def _fox_bwd_n(q, k_aug, kv, do, lse, dsum, cfg, *, name):
    t, d, hh = cfg.tokens, cfg.d_model, cfg.heads
    tq = _blk(cfg.seq, FOX_BLOCK)
    nq = cfg.seq // tq

    def body(q_ref, ka_ref, v_ref, do_ref, lse_ref, ds_ref, dq_hbm, dk_ref, dv_ref, dc_ref, dq_acc, sem):
        b, ki, qi = pl.program_id(0), pl.program_id(1), pl.program_id(2)
        qq = jnp.maximum(qi, ki)

        @pl.when((ki == 0) & (qi == 0))
        def _():
            dq_acc[...] = jnp.zeros_like(dq_acc)

        @pl.when(qi == 0)
        def _():
            dk_ref[...] = jnp.zeros_like(dk_ref)
            dv_ref[...] = jnp.zeros_like(dv_ref)
            dc_ref[...] = jnp.zeros_like(dc_ref)

        def step(diagonal):
            upper = lax.broadcasted_iota(jnp.int32, (LANES, tq), 0) < HEAD_DIM
            half = _half_mask((tq, LANES))
            for hp in range(hh // 2):
                pair = slice(hp * LANES, (hp + 1) * LANES)
                q2 = q_ref[:, pair].T
                don2 = do_ref[:, pair]
                dot2 = don2.T
                dvs, dks = [], []
                for e in range(2):
                    h = 2 * hp + e
                    rows = slice(h * HEAD_DIM, (h + 1) * HEAD_DIM)
                    ka = ka_ref[:, h * AUG:(h + 1) * AUG]
                    qa = _aug_q_t(q2, e, tq)
                    s = jnp.dot(ka, qa, preferred_element_type=F32)
                    if diagonal:
                        s = jnp.where(_keys_visible(tq), s, NEG)
                    p = jnp.exp(s - lse_ref[h:h + 1, :])
                    dote = jnp.where(upper == (e == 0), dot2, jnp.zeros_like(dot2))
                    dp = jnp.dot(v_ref[:, pair], dote, preferred_element_type=F32)
                    dsf = p * (dp - ds_ref[h:h + 1, :])
                    dc_ref[:, h:h + 1] -= jnp.sum(dsf, axis=1, keepdims=True)
                    dsc = dsf.astype(BF16)
                    dvs.append(jnp.dot(p.astype(BF16), don2, preferred_element_type=F32))
                    dks.append(lax.dot_general(dsc, qa, _NT, preferred_element_type=F32))
                    dq_acc[qq, rows, :] += jnp.dot(ka.T[:HEAD_DIM], dsc, preferred_element_type=F32)
                dv_ref[:, pair] += jnp.where(half, dvs[0], dvs[1])
                dk_ref[:, pair] += jnp.where(half, dks[0], pltpu.roll(dks[1], HEAD_DIM, 1))

        pl.when(qi > ki)(functools.partial(step, False))
        pl.when(qi == ki)(functools.partial(step, True))

        @pl.when((ki == nq - 1) & (qi == nq - 1))
        def _():
            cp = pltpu.make_async_copy(dq_acc, dq_hbm.at[b], sem)
            cp.start()
            cp.wait()

    def qrow(b, ki, qi):
        return (b * nq + jnp.maximum(qi, ki), 0)

    def qcol(b, ki, qi):
        return (0, b * nq + jnp.maximum(qi, ki))

    def krow(b, ki, qi):
        return (b * nq + ki, 0)

    return pl.pallas_call(
        body, name=name, grid=(cfg.batch, nq, nq),
        in_specs=[pl.BlockSpec((tq, d), qrow),
                  pl.BlockSpec((tq, hh * AUG), krow),
                  pl.BlockSpec((tq, d), lambda b, ki, qi: (b * nq + ki, 1)),
                  pl.BlockSpec((tq, d), qrow),
                  pl.BlockSpec((hh, tq), qcol), pl.BlockSpec((hh, tq), qcol)],
        out_specs=[pl.BlockSpec(memory_space=pl.ANY), pl.BlockSpec((tq, d), krow),
                   pl.BlockSpec((tq, d), krow), pl.BlockSpec((tq, LANES), krow)],
        out_shape=[jax.ShapeDtypeStruct((cfg.batch, nq, d, tq), F32), jax.ShapeDtypeStruct((t, d), F32),
                   jax.ShapeDtypeStruct((t, d), F32), jax.ShapeDtypeStruct((t, LANES), F32)],
        scratch_shapes=[pltpu.VMEM((nq, d, tq), F32), pltpu.SemaphoreType.DMA],
        compiler_params=_params(("arbitrary", "arbitrary", "arbitrary")),
    )(q, k_aug, kv, do, lse, dsum)


def _fwd(a, w, *, name, res=None, scale=1.0, norms=()):
    bm = FUSED_ROWS if norms else 1024
    out = _mm(a, w, form="F", out_dtype=F32, name=name, bm=bm, bn=WIDE, bk=WIDE, res=res, scale=scale, norms=norms)
    return (out[0], list(out[1:])) if norms else out


def _bwd(dy, w, *, name, scale=1.0, norm_bwd=None):
    bm = FUSED_ROWS if norm_bwd is not None else 1024
    return _mm(dy, w, form="B", out_dtype=F32, name=name, bm=bm, bn=WIDE, bk=WIDE, scale=scale, norm_bwd=norm_bwd)


def _wgrad(a, dy, w, *, name, scale=1.0):
    return _mm_grad(a, dy, w.shape[0], name=name, bm=WIDE, bn=WIDE, scale=scale)


def _ffn_fwd(h, n, w_in, w_out, tag, norms=()):
    gate, up, a = _ffn_in_act(n, w_in, name=f"{tag}_in")
    out = _fwd(a, w_out, name=f"{tag}_out", res=h, scale=0.5, norms=norms)
    h_out, normed = out if norms else (out, [])
    return h_out, normed, (n, gate, up, a)


def _ffn_bwd(dh_out, h, g, w_in, w_out, saved, tag):
    n, gate, up, a = saved
    du = _ffn_out_dx_act(dh_out, w_out, gate, up, name=f"{tag}_out_dx", scale=0.5)
    dw_out = _wgrad(a, dh_out, w_out, name=f"{tag}_out_dw", scale=0.5)
    dh, dg = _mm_back2(du, w_in, (h, g, dh_out), name=f"{tag}_in_dx", bk=WIDE)
    dw_in = _mm_grad(n, du, w_in.shape[0], name=f"{tag}_in_dw", bm=WIDE, bn=WIDE)
    return dh, dg, dw_in, dw_out


def _head_gain(g, heads, scale=1.0):
    return jnp.tile(g.astype(F32) * scale, heads)


def _local_step(cfg, x, positions, target, w, s):
    d, hh = cfg.d_model, cfg.heads
    cos, sin = _rope_tables(positions)
    ones = jnp.ones((d,), F32)

    n00 = _rms_fwd(x, s["ffn_norm"][0, 0], name="ffn00_norm")
    h1, (hn_a,), ffn0 = _ffn_fwd(x, n00, w["ffn_w_in"][0][0], w["ffn_w_out"][0][0], "ffn00", [s["mix_norm"][0]])
    qkv = _fwd(hn_a, w["a_w_qkv"], name="a_qkv")
    kinds_a = ["rope", "rope", "cast"] * len(DILATIONS)
    gains_a = jnp.stack([z for g in range(len(DILATIONS)) for z in (
        _head_gain(s["a_q_norm"][g], hh, Q_SCALE), _head_gain(s["a_k_norm"][g], hh), ones)])
    qkvp = [_hn_fwd(qkv, gains_a[3 * g:3 * g + 3], kinds_a[:3], d, cos, sin, name=f"a_qk_norm{g}", col0=3 * g)
            for g in range(len(DILATIONS))]
    lay = [qkvp[g].reshape(cfg.tokens // dil, dil * 3 * d) for g, dil in enumerate(DILATIONS)]
    band = [_band_fwd_n(lay[g], dil, cfg, name=f"a_band{g}") for g, dil in enumerate(DILATIONS)]
    mixed, lse_a = _mix_fwd([o.reshape(cfg.tokens, d) for o, _ in band], [_from_classes_t(l) for _, l in band],
                            name="a_mix")
    h2, (n01,) = _fwd(mixed, w["a_w_o"], name="a_out", res=h1, norms=[s["ffn_norm"][0, 1]])
    h3, (kn, n10), ffn1 = _ffn_fwd(h2, n01, w["ffn_w_in"][0][1], w["ffn_w_out"][0][1], "ffn01",
                                   [s["kv_norm"], s["ffn_norm"][1, 0]])

    proj = _fwd(kn, w["kv_w"], name="kv_proj")
    kinds_kv = ["norm", "cast"]
    gains_kv = jnp.stack([_head_gain(s["kv_k_norm"], hh), ones])
    kvp = _hn_fwd(proj, gains_kv, kinds_kv, d, cos, sin, name="kv_k_norm")
    gate_col = 2 * d // LANES
    bias = jnp.pad(s["kv_b_f"].astype(F32), (0, LANES - hh))
    cum = _gate_fwd(proj, gate_col, bias, cfg, name="kv_gate")
    k_aug = _fox_aug_k_call(kvp, cum, cfg, name="kv_aug")

    h4, (hn_b,), ffn2 = _ffn_fwd(h3, n10, w["ffn_w_in"][1][0], w["ffn_w_out"][1][0], "ffn10", [s["mix_norm"][1]])
    qraw = _fwd(hn_b, w["b_w_q"], name="b_q")
    gains_b = _head_gain(s["b_q_norm"][0], hh, Q_SCALE)[None]
    qp = _hn_fwd(qraw, gains_b, ["norm"], d, cos, sin, name="b_q_norm")
    o_b, lse_b = _fox_fwd_n(qp, k_aug, kvp, cfg, name="b_fox")
    h5, (n11,) = _fwd(o_b, w["b_w_o"], name="b_out", res=h4, norms=[s["ffn_norm"][1, 1]])
    h6, _, ffn3 = _ffn_fwd(h5, n11, w["ffn_w_in"][1][1], w["ffn_w_out"][1][1], "ffn11")

    loss, dh6 = _loss_fwd_bwd(h6, target, name="loss")

    dh5, dg11, dwi11, dwo11 = _ffn_bwd(dh6, h5, s["ffn_norm"][1, 1], w["ffn_w_in"][1][1], w["ffn_w_out"][1][1],
                                       ffn3, "ffn11")
    do_b = _bwd(dh5, w["b_w_o"], name="b_out_dx")
    dw_bo = _wgrad(o_b, dh5, w["b_w_o"], name="b_out_dw")
    do_bf = do_b.astype(BF16)
    dsum_b = _head_dot_c(do_bf, o_b, cfg, name="b_dsum")
    dq4, dk_b, dv_b, dcum = _fox_bwd_n(qp, k_aug, kvp, do_bf, lse_b, dsum_b, cfg, name="b_fox_bwd")
    dq_b = dq4.transpose(0, 1, 3, 2).reshape(cfg.tokens, d)
    dqraw, dgq = _hn_bwd(qraw, [dq_b], gains_b, ["norm"], d, cos, sin, name="b_q_norm_bwd")
    dh4, dmix1 = _bwd(dqraw, w["b_w_q"], name="b_q_dx", norm_bwd=(h4, s["mix_norm"][1], dh5))
    dw_bq = _wgrad(hn_b, dqraw, w["b_w_q"], name="b_q_dw")
    dh3, dg10, dwi10, dwo10 = _ffn_bwd(dh4, h3, s["ffn_norm"][1, 0], w["ffn_w_in"][1][0], w["ffn_w_out"][1][0],
                                       ffn2, "ffn10")

    dkvraw, dgk = _hn_bwd(proj, [dk_b, dv_b], gains_kv, kinds_kv, d, cos, sin, name="kv_k_norm_bwd")
    dz, dbias = _gate_bwd(proj, gate_col, bias, dcum, cfg, name="kv_gate_bwd")
    pad_cols = w["kv_w"].shape[2] - 2 * d - LANES
    dproj = jnp.concatenate([dkvraw, dz.astype(BF16), jnp.zeros((cfg.tokens, pad_cols), BF16)], axis=1)
    dw_kv = _wgrad(kn, dproj, w["kv_w"], name="kv_proj_dw")
    dh3, dkvn = _bwd(dproj, w["kv_w"], name="kv_proj_dx", norm_bwd=(h3, s["kv_norm"], dh3))

    dh2, dg01, dwi01, dwo01 = _ffn_bwd(dh3, h2, s["ffn_norm"][0, 1], w["ffn_w_in"][0][1], w["ffn_w_out"][0][1],
                                       ffn1, "ffn01")
    dmixed = _bwd(dh2, w["a_w_o"], name="a_out_dx")
    dw_ao = _wgrad(mixed, dh2, w["a_w_o"], name="a_out_dw")
    dmixed_bf = dmixed.astype(BF16)
    dsum_a = _head_dot_c(dmixed_bf, mixed, cfg, name="a_dsum")
    dqkvp = []
    for g, dil in enumerate(DILATIONS):
        dsum_c = dsum_a.reshape(hh, cfg.tokens // dil, dil).transpose(2, 0, 1)
        grads = _band_bwd_n(lay[g], dmixed_bf.reshape(cfg.tokens // dil, dil * d), _to_classes_t(lse_a, dil, hh),
                            dsum_c, dil, cfg, name=f"a_band{g}_bwd")
        dqkvp += [z.reshape(cfg.tokens, d) for z in grads]
    dqkv, dga = _hn_bwd(qkv, dqkvp, gains_a, kinds_a, d, cos, sin, name="a_qk_norm_bwd")
    dh1, dmix0 = _bwd(dqkv, w["a_w_qkv"], name="a_qkv_dx", norm_bwd=(h1, s["mix_norm"][0], dh2))
    dw_qkv = _wgrad(hn_a, dqkv, w["a_w_qkv"], name="a_qkv_dw")
    dx, dg00, dwi00, dwo00 = _ffn_bwd(dh1, x, s["ffn_norm"][0, 0], w["ffn_w_in"][0][0], w["ffn_w_out"][0][0],
                                      ffn0, "ffn00")

    dw = {
        "ffn_w_in": [[dwi00, dwi01], [dwi10, dwi11]],
        "ffn_w_out": [[dwo00, dwo01], [dwo10, dwo11]],
        "a_w_qkv": dw_qkv, "a_w_o": dw_ao, "kv_w": dw_kv, "b_w_q": dw_bq, "b_w_o": dw_bo,
    }
    ds = {
        "ffn_norm": jnp.stack([jnp.stack([dg00, dg01]), jnp.stack([dg10, dg11])]),
        "mix_norm": jnp.stack([dmix0, dmix1]),
        "a_q_norm": jnp.stack([dga[3 * g] for g in range(len(DILATIONS))])[None] * Q_SCALE,
        "a_k_norm": jnp.stack([dga[3 * g + 1] for g in range(len(DILATIONS))])[None],
        "kv_norm": dkvn,
        "kv_b_f": dbias[:hh],
        "kv_k_norm": dgk[0],
        "b_q_norm": dgq * Q_SCALE,
    }
    return loss, dx, dw, ds


MESH_ID = pl.DeviceIdType.MESH
ANY = pl.BlockSpec(memory_space=pl.ANY)
PACK_COLS = 1024
PACK_ROW_ALIGN = 32


def _me():
    return lax.axis_index("x"), lax.axis_index("y"), lax.axis_index("c")


def _other_chips(x, y):
    return [(1 - x, y), (x, 1 - y), (1 - x, 1 - y)]


def _all_gather_small(v, *, name):
    r = v.shape[0]

    def body(v_ref, out_ref, send_sems, recv_sems):
        x, y, c = _me()
        me = 4 * x + 2 * y + c
        out_ref[me] = v_ref[...]
        copies = []
        for k in range(1, N_DEV):
            fx, fy, fc = (k >> 2) & 1, (k >> 1) & 1, k & 1
            peer = (1 - x if fx else x, 1 - y if fy else y, 1 - c if fc else c)
            copies.append(pltpu.make_async_remote_copy(
                src_ref=v_ref, dst_ref=out_ref.at[me], send_sem=send_sems.at[k - 1], recv_sem=recv_sems.at[k - 1],
                device_id=peer, device_id_type=MESH_ID))
        for cp in copies:
            cp.start()
        for cp in copies:
            cp.wait()

    return pl.pallas_call(
        body, name=name,
        in_specs=[pl.BlockSpec(memory_space=pltpu.VMEM)], out_specs=pl.BlockSpec(memory_space=pltpu.VMEM),
        out_shape=jax.ShapeDtypeStruct((N_DEV, r, LANES), v.dtype),
        scratch_shapes=[pltpu.SemaphoreType.DMA((N_DEV - 1,)), pltpu.SemaphoreType.DMA((N_DEV - 1,))],
    )(v)


def _all_gather_chips(v, *, name):
    rh = v.shape[0] // 2

    def body(v_ref, out_ref, send_sems, recv_sems):
        x, y, c = _me()
        j = 2 * x + y
        chips = _other_chips(x, y)

        def half(chip, core):
            return out_ref.at[chip, pl.ds(core * rh, rh)]

        first = [pltpu.make_async_remote_copy(
            src_ref=v_ref.at[pl.ds(c * rh, rh)], dst_ref=half(j, c), send_sem=send_sems.at[k],
            recv_sem=recv_sems.at[k], device_id=(px, py, c), device_id_type=MESH_ID)
            for k, (px, py) in enumerate(chips)]
        for cp in first:
            cp.start()
        passed = [pltpu.make_async_remote_copy(
            src_ref=half(2 * px + py, c), dst_ref=half(2 * px + py, c), send_sem=send_sems.at[3 + k],
            recv_sem=recv_sems.at[3 + k], device_id=(x, y, 1 - c), device_id_type=MESH_ID)
            for k, (px, py) in enumerate(chips)]
        for k in range(len(chips)):
            first[k].wait_recv()
            passed[k].start()
        for k, (px, py) in enumerate(chips):
            pltpu.make_async_remote_copy(
                src_ref=half(2 * px + py, 1 - c), dst_ref=half(2 * px + py, 1 - c), send_sem=send_sems.at[3 + k],
                recv_sem=recv_sems.at[3 + k], device_id=(x, y, 1 - c), device_id_type=MESH_ID).wait_recv()
        for cp in first + passed:
            cp.wait_send()

    return pl.pallas_call(
        body, name=name, in_specs=[ANY], out_specs=ANY,
        out_shape=jax.ShapeDtypeStruct((N_CHIPS,) + v.shape, v.dtype),
        scratch_shapes=[pltpu.SemaphoreType.DMA((2 * (N_CHIPS - 1),)), pltpu.SemaphoreType.DMA((2 * (N_CHIPS - 1),))],
    )(v)


def _swap_halves(g, *, name):
    n, r, cols = g.shape
    rh = r // 2

    def body(g_ref, out_ref, send_sem, recv_sem):
        x, y, c = _me()
        cp = pltpu.make_async_remote_copy(
            src_ref=g_ref.at[:, pl.ds((1 - c) * rh, rh)], dst_ref=out_ref, send_sem=send_sem, recv_sem=recv_sem,
            device_id=(x, y, 1 - c), device_id_type=MESH_ID)
        cp.start()
        cp.wait()

    return pl.pallas_call(
        body, name=name, in_specs=[ANY], out_specs=ANY,
        out_shape=jax.ShapeDtypeStruct((n, rh, cols), g.dtype),
        scratch_shapes=[pltpu.SemaphoreType.DMA, pltpu.SemaphoreType.DMA],
    )(g)


def _scatter_chips(v, *, name):
    def body(v_ref, out_ref, send_sems, recv_sems):
        x, y, c = _me()
        j = 2 * x + y
        copies = [pltpu.make_async_remote_copy(
            src_ref=v_ref.at[2 * px + py], dst_ref=out_ref.at[j], send_sem=send_sems.at[k], recv_sem=recv_sems.at[k],
            device_id=(px, py, c), device_id_type=MESH_ID) for k, (px, py) in enumerate(_other_chips(x, y))]
        for cp in copies:
            cp.start()
        for cp in copies:
            cp.wait()

    return pl.pallas_call(
        body, name=name, in_specs=[ANY], out_specs=ANY,
        out_shape=jax.ShapeDtypeStruct(v.shape, v.dtype),
        scratch_shapes=[pltpu.SemaphoreType.DMA((N_CHIPS - 1,)), pltpu.SemaphoreType.DMA((N_CHIPS - 1,))],
    )(v)


def _join_halves(v, *, name):
    def body(v_ref, out_ref, send_sem, recv_sem):
        x, y, c = _me()
        cp = pltpu.make_async_remote_copy(
            src_ref=v_ref, dst_ref=out_ref.at[c], send_sem=send_sem, recv_sem=recv_sem,
            device_id=(x, y, 1 - c), device_id_type=MESH_ID)
        cp.start()
        cp.wait()

    return pl.pallas_call(
        body, name=name, in_specs=[ANY], out_specs=ANY,
        out_shape=jax.ShapeDtypeStruct((2,) + v.shape, v.dtype),
        scratch_shapes=[pltpu.SemaphoreType.DMA, pltpu.SemaphoreType.DMA],
    )(v)


def _row_blk(rows, want):
    for b in range(min(rows, want) // SUBLANES * SUBLANES, 0, -SUBLANES):
        if rows % b == 0:
            return b
    return rows


def _add_own_half(g, got, *, name):
    n, r, cols = g.shape
    rh = r // 2
    tr = _row_blk(rh, 512)
    nb = rh // tr

    def body(c_ref, g_ref, got_ref, o_ref):
        del c_ref
        o_ref[...] = (g_ref[...] + got_ref[...]).astype(BF16)

    grid_spec = pltpu.PrefetchScalarGridSpec(
        num_scalar_prefetch=1, grid=(n, nb),
        in_specs=[pl.BlockSpec((None, tr, cols), lambda j, i, c: (j, c[0] * nb + i, 0)),
                  pl.BlockSpec((None, tr, cols), lambda j, i, c: (j, i, 0))],
        out_specs=pl.BlockSpec((None, tr, cols), lambda j, i, c: (j, i, 0)))
    return pl.pallas_call(
        body, name=name, grid_spec=grid_spec, out_shape=jax.ShapeDtypeStruct((n, rh, cols), BF16),
        compiler_params=_params(("parallel", "parallel")),
    )(lax.axis_index("c").astype(jnp.int32).reshape(1), g, got)


def _sum_parts(parts, *, name):
    n, r, cols = parts.shape
    tr = _row_blk(r, 512)

    def body(*refs):
        o_ref = refs[n]
        acc = refs[0][...].astype(F32)
        for p_ref in refs[1:n]:
            acc = acc + p_ref[...].astype(F32)
        o_ref[...] = acc

    return pl.pallas_call(
        body, name=name, grid=(r // tr,),
        in_specs=[pl.BlockSpec((None, tr, cols), functools.partial(lambda j, i: (j, i, 0), j)) for j in range(n)],
        out_specs=pl.BlockSpec((tr, cols), lambda i: (i, 0)),
        out_shape=jax.ShapeDtypeStruct((r, cols), F32),
        compiler_params=_params(("parallel",)),
    )(*([parts] * n))


def _adamw(w, m, v, g, *, name):
    shape = w.shape
    cols = shape[-1]
    w2, m2, v2, g2 = (z.reshape(-1, cols) for z in (w, m, v, g))
    rows = w2.shape[0]
    tr = _row_blk(rows, max(SUBLANES, (1 << 20) // (4 * cols)))

    def body(w_ref, m_ref, v_ref, g_ref, d_ref, nm_ref, nv_ref):
        gv = g_ref[...]
        nm = ADAM_B1 * m_ref[...] + (1.0 - ADAM_B1) * gv
        nv = ADAM_B2 * v_ref[...] + (1.0 - ADAM_B2) * jnp.square(gv)
        m_hat = nm / (1.0 - ADAM_B1 ** ADAM_STEP)
        v_hat = nv / (1.0 - ADAM_B2 ** ADAM_STEP)
        d_ref[...] = -ADAM_LR * (m_hat / (jnp.sqrt(v_hat) + ADAM_EPS) + ADAM_WD * w_ref[...])
        nm_ref[...] = nm
        nv_ref[...] = nv

    spec = pl.BlockSpec((tr, cols), lambda i: (i, 0))
    out = jax.ShapeDtypeStruct((rows, cols), F32)
    d, nm, nv = pl.pallas_call(
        body, name=name, grid=(rows // tr,), in_specs=[spec] * 4, out_specs=[spec] * 3, out_shape=[out] * 3,
        compiler_params=_params(("parallel",)),
    )(w2, m2, v2, g2)
    return d.reshape(shape), nm.reshape(shape), nv.reshape(shape)


def _pack_rows(size, cols, align):
    return -(-size // (cols * align)) * align


def _pack(arrs, lead, cols, align, total_align):
    lead_shape = arrs[0].shape[:lead]
    parts = []
    for a in arrs:
        flat = a.reshape(lead_shape + (-1,))
        size = flat.shape[-1]
        rows = _pack_rows(size, cols, align)
        flat = jnp.pad(flat, [(0, 0)] * lead + [(0, rows * cols - size)])
        parts.append(flat.reshape(lead_shape + (rows, cols)))
    total = sum(p.shape[lead] for p in parts)
    extra = -total % total_align
    if extra:
        parts.append(jnp.zeros(lead_shape + (extra, cols), parts[0].dtype))
    return jnp.concatenate(parts, axis=lead)


def _unpack(buf, shapes, lead, cols, align):
    lead_shape = buf.shape[:lead]
    out, row = [], 0
    for shp in shapes:
        size = 1
        for n in shp:
            size *= n
        rows = _pack_rows(size, cols, align)
        piece = lax.slice_in_dim(buf, row, row + rows, axis=lead).reshape(lead_shape + (-1,))
        out.append(piece[..., :size].reshape(lead_shape + tuple(shp)))
        row += rows
    return out


BIG = ("ffn_w_in", "ffn_w_out", "a_w_qkv", "a_w_o", "kv_w", "b_w_q", "b_w_o")
SMALL = ("ffn_norm", "mix_norm", "a_q_norm", "a_k_norm", "kv_norm", "kv_b_f", "kv_k_norm", "b_q_norm")
WEIGHTS = ("ffn_norm", "ffn_w_in", "ffn_w_out", "mix_norm", "a_w_qkv", "a_q_norm", "a_k_norm", "a_w_o",
           "kv_norm", "kv_w", "kv_b_f", "kv_k_norm", "b_w_q", "b_q_norm", "b_w_o")
GATE_PAD = 2 * LANES


def _stack_weights(sh, d):
    depth = sh["ffn_w_in"].shape[1]
    kv = sh["kv_w"].transpose(1, 0, 2).reshape(d, -1)
    kv = jnp.pad(kv, ((0, 0), (0, 2 * d + GATE_PAD - kv.shape[1])))
    return {
        "ffn_w_in": [[sh["ffn_w_in"][:, l, i] for i in range(2)] for l in range(depth)],
        "ffn_w_out": [[sh["ffn_w_out"][:, l, i].reshape(1, -1, d) for i in range(2)] for l in range(depth)],
        "a_w_qkv": sh["a_w_qkv"][:, 0],
        "a_w_o": sh["a_w_o"].reshape(1, d, d),
        "kv_w": kv[None],
        "b_w_q": sh["b_w_q"].reshape(1, d, d),
        "b_w_o": sh["b_w_o"].reshape(1, d, d),
    }


def _unstack_grads(dw, d, heads):
    def rows4(z):
        return z.reshape(N_CHIPS, -1, d)

    kv_cols = 2 * d + heads
    kv = dw["kv_w"][0][:, :kv_cols].reshape(d, N_CHIPS, kv_cols // N_CHIPS).transpose(1, 0, 2)
    return [
        jnp.stack([jnp.stack(row, axis=1) for row in dw["ffn_w_in"]], axis=1),
        jnp.stack([jnp.stack([rows4(z) for z in row], axis=1) for row in dw["ffn_w_out"]], axis=1),
        dw["a_w_qkv"][:, None],
        rows4(dw["a_w_o"])[:, None],
        kv,
        rows4(dw["b_w_q"])[:, None],
        rows4(dw["b_w_o"])[:, None],
    ]


def kernel(x, positions, ffn_norm, ffn_w_in, ffn_w_out, mix_norm, a_w_qkv, a_q_norm, a_k_norm, a_w_o, kv_norm, kv_w, kv_b_f, kv_k_norm, b_w_q, b_q_norm, b_w_o, loss_target, m_ffn_norm, m_ffn_w_in, m_ffn_w_out, m_mix_norm, m_a_w_qkv, m_a_q_norm, m_a_k_norm, m_a_w_o, m_kv_norm, m_kv_w, m_kv_b_f, m_kv_k_norm, m_b_w_q, m_b_q_norm, m_b_w_o, v_ffn_norm, v_ffn_w_in, v_ffn_w_out, v_mix_norm, v_a_w_qkv, v_a_q_norm, v_a_k_norm, v_a_w_o, v_kv_norm, v_kv_w, v_kv_b_f, v_kv_k_norm, v_b_w_q, v_b_q_norm, v_b_w_o):
    wts = dict(zip(WEIGHTS, (ffn_norm, ffn_w_in, ffn_w_out, mix_norm, a_w_qkv, a_q_norm, a_k_norm, a_w_o, kv_norm,
                             kv_w, kv_b_f, kv_k_norm, b_w_q, b_q_norm, b_w_o)))
    mom = dict(zip(WEIGHTS, (m_ffn_norm, m_ffn_w_in, m_ffn_w_out, m_mix_norm, m_a_w_qkv, m_a_q_norm, m_a_k_norm,
                             m_a_w_o, m_kv_norm, m_kv_w, m_kv_b_f, m_kv_k_norm, m_b_w_q, m_b_q_norm, m_b_w_o)))
    var = dict(zip(WEIGHTS, (v_ffn_norm, v_ffn_w_in, v_ffn_w_out, v_mix_norm, v_a_w_qkv, v_a_q_norm, v_a_k_norm,
                             v_a_w_o, v_kv_norm, v_kv_w, v_kv_b_f, v_kv_k_norm, v_b_w_q, v_b_q_norm, v_b_w_o)))
    batch, seq, d = x.shape
    cfg = Cfg(d_model=d, d_ff=ffn_w_out.shape[2] * N_CHIPS, seq=seq, batch=batch)
    chip = 2 * lax.axis_index("x") + lax.axis_index("y")
    big_shapes = [wts[n].shape for n in BIG]

    shard = _pack([wts[n].astype(BF16) for n in BIG], 0, PACK_COLS, PACK_ROW_ALIGN, PACK_COLS)
    gathered = _all_gather_chips(shard, name="gather_weights")
    gathered = lax.dynamic_update_slice_in_dim(gathered, shard[None], chip, axis=0)
    w = _stack_weights(dict(zip(BIG, _unpack(gathered, big_shapes, 1, PACK_COLS, PACK_ROW_ALIGN))), d)
    norm_shard = _pack([ffn_norm], 0, LANES, SUBLANES, SUBLANES)
    norms = _all_gather_small(norm_shard, name="gather_ffn_norm")[0::2]
    (norms,) = _unpack(norms, [ffn_norm.shape], 1, LANES, SUBLANES)
    small = {"ffn_norm": jnp.moveaxis(norms, 0, 2).reshape(ffn_norm.shape[:2] + (d,)),
             "mix_norm": mix_norm, "a_q_norm": a_q_norm[0], "a_k_norm": a_k_norm[0], "kv_norm": kv_norm,
             "kv_b_f": kv_b_f, "kv_k_norm": kv_k_norm, "b_q_norm": b_q_norm}

    loss, dx, dw, ds = _local_step(cfg, x.reshape(cfg.tokens, d), positions.reshape(cfg.tokens),
                                   loss_target.reshape(cfg.tokens, d), w, small)
    loss = lax.psum(loss, ("x", "y", "c"))

    g = _pack(_unstack_grads(dw, d, cfg.heads), 1, PACK_COLS, PACK_ROW_ALIGN, PACK_COLS)
    chip_half = _add_own_half(g, _swap_halves(g, name="swap_halves"), name="add_halves")
    parts = _scatter_chips(chip_half, name="scatter_chips")
    parts = lax.dynamic_update_slice_in_dim(parts, lax.dynamic_slice_in_dim(chip_half, chip, 1, axis=0), chip, axis=0)
    mine = _sum_parts(parts, name="sum_chips")
    both = _join_halves(mine, name="join_halves")
    g_big = lax.dynamic_update_slice_in_dim(both, mine[None], lax.axis_index("c"), axis=0).reshape(g.shape[1:])
    grads = dict(zip(BIG, _unpack(g_big, big_shapes, 0, PACK_COLS, PACK_ROW_ALIGN)))

    small_shapes = [ds[n].shape for n in SMALL]
    parts = _all_gather_small(_pack([ds[n] for n in SMALL], 0, LANES, SUBLANES, SUBLANES), name="gather_small")
    g_small = dict(zip(SMALL, _unpack(_sum_parts(parts, name="sum_small"), small_shapes, 0, LANES, SUBLANES)))
    quarter = d // N_CHIPS
    g_small["ffn_norm"] = lax.dynamic_slice_in_dim(g_small["ffn_norm"], chip * quarter, quarter, axis=2)
    grads.update(g_small)

    delta, new_m, new_v = {}, {}, {}
    for n in BIG:
        delta[n], new_m[n], new_v[n] = _adamw(wts[n], mom[n], var[n], grads[n], name=f"adamw_{n}")
    packed = [_pack([z[n] for n in SMALL], 0, LANES, SUBLANES, SUBLANES) for z in (wts, mom, var, grads)]
    small_out = _adamw(*packed, name="adamw_small")
    shard_shapes = [wts[n].shape for n in SMALL]
    for out, res in zip((delta, new_m, new_v), small_out):
        out.update(zip(SMALL, _unpack(res, shard_shapes, 0, LANES, SUBLANES)))

    return (loss, dx.reshape(x.shape), *[grads[n] for n in WEIGHTS], *[delta[n] for n in WEIGHTS],
            *[new_m[n] for n in WEIGHTS], *[new_v[n] for n in WEIGHTS])
```

```python
import functools
from typing import NamedTuple

import jax
import jax.numpy as jnp
from jax import lax
from jax.experimental import pallas as pl
from jax.experimental.pallas import tpu as pltpu

F32 = jnp.float32
BF16 = jnp.bfloat16

HEAD_DIM = 64
LANES = 128
SUBLANES = 8
ROT_DIM = HEAD_DIM // 4
ROPE_THETA = 500000.0
NORM_EPS = 1e-6
BAND = 128
DILATIONS = (1, 4, 16)
NEG = -1e30
Q_SCALE = HEAD_DIM ** -0.5
N_CHIPS = 4
N_DEV = 8
VMEM_LIMIT = 48 * 1024 * 1024
WIDE = 1536
ROW_BLOCK = 512
FUSED_ROWS = 512
_NT = (((1,), (1,)), ((), ()))

ADAM_LR = 0.001
ADAM_B1 = 0.9
ADAM_B2 = 0.999
ADAM_EPS = 1e-08
ADAM_WD = 0.01
ADAM_STEP = 10


class Cfg(NamedTuple):
    d_model: int
    d_ff: int
    seq: int
    batch: int

    @property
    def heads(self):
        return self.d_model // HEAD_DIM

    @property
    def tokens(self):
        return self.batch * self.seq


def _params(sem):
    return pltpu.CompilerParams(dimension_semantics=sem, vmem_limit_bytes=VMEM_LIMIT)


def _blk(dim, want):
    if dim <= want:
        return dim
    for b in range(want // LANES * LANES, 0, -LANES):
        if dim % b == 0:
            return b
    b = want
    while dim % b:
        b //= 2
    return b


def _fold8(x):
    return jnp.sum(x.reshape(x.shape[0] // SUBLANES, SUBLANES, x.shape[1]), axis=0)


def _rms_bwd_tile(xv, g, dyv, dres):
    rstd = lax.rsqrt(jnp.mean(xv * xv, axis=-1, keepdims=True) + NORM_EPS)
    xhat = xv * rstd
    dyg = dyv * g
    proj = jnp.mean(dyg * xhat, axis=-1, keepdims=True)
    return dres + rstd * (dyg - xhat * proj), _fold8(dyv * xhat)


def _mm(a, b, *, form, out_dtype, name, bm=1024, bn=1024, bk=1024, res=None, scale=1.0, norms=(), norm_bwd=None):
    if form == "F":
        m, kdim = a.shape
        jn, _, ns = b.shape
        bm, bn, bk = _blk(m, bm), _blk(ns, bn), _blk(kdim, bk)
        npj = ns // bn
        grid = (m // bm, jn * npj, kdim // bk)
        a_spec = pl.BlockSpec((bm, bk), lambda i, n, k: (i, k))
        b_spec = pl.BlockSpec((None, bk, bn), lambda i, n, k: (n // npj, k, n % npj))
        o_spec = pl.BlockSpec((bm, bn), lambda i, n, k: (i, n))
        o_shape = jax.ShapeDtypeStruct((m, jn * ns), out_dtype)
        dims = (((1,), (0,)), ((), ()))
    elif form == "B":
        m = a.shape[0]
        jn, kdim, ns = b.shape
        bm, bn, bk = _blk(m, bm), _blk(kdim, bn), _blk(ns, bk)
        kpj = ns // bk
        grid = (m // bm, kdim // bn, jn * kpj)
        a_spec = pl.BlockSpec((bm, bk), lambda i, n, k: (i, k))
        b_spec = pl.BlockSpec((None, bn, bk), lambda i, n, k: (k // kpj, n, k % kpj))
        o_spec = pl.BlockSpec((bm, bn), lambda i, n, k: (i, n))
        o_shape = jax.ShapeDtypeStruct((m, kdim), out_dtype)
        dims = _NT
    else:
        raise ValueError(form)
    nk = grid[2]
    n_norms = len(norms)
    full_rows = grid[1] == 1
    assert full_rows or (not norms and norm_bwd is None)

    def body(*refs):
        a_ref, b_ref = refs[:2]
        pos = 2
        r_ref = None
        if res is not None:
            r_ref = refs[pos]
            pos += 1
        g_refs = refs[pos:pos + n_norms]
        pos += n_norms
        if norm_bwd is not None:
            x_ref, gb_ref, dres_ref = refs[pos:pos + 3]
            pos += 3
        o_ref = refs[pos]
        n_refs = refs[pos + 1:pos + 1 + n_norms]
        acc_ref = refs[-1]
        i, k = pl.program_id(0), pl.program_id(2)

        @pl.when(k == 0)
        def _():
            acc_ref[...] = jnp.zeros_like(acc_ref)

        acc_ref[...] += lax.dot_general(a_ref[...].astype(BF16), b_ref[...].astype(BF16), dims,
                                        preferred_element_type=F32)

        if norm_bwd is not None:
            dg_ref = refs[pos + 1 + n_norms]

            @pl.when((i == 0) & (k == 0))
            def _():
                dg_ref[...] = jnp.zeros_like(dg_ref)

        @pl.when(k == nk - 1)
        def _():
            r = acc_ref[...]
            if scale != 1.0:
                r = r * scale
            if r_ref is not None:
                r = r_ref[...] + r
            if norm_bwd is not None:
                dx, dg8 = _rms_bwd_tile(x_ref[...], gb_ref[...], r, dres_ref[...])
                o_ref[...] = dx
                dg_ref[...] += dg8
            else:
                o_ref[...] = r.astype(o_ref.dtype)
            if n_norms:
                rstd = lax.rsqrt(jnp.mean(r * r, axis=-1, keepdims=True) + NORM_EPS)
                for g_ref, n_ref in zip(g_refs, n_refs):
                    n_ref[...] = ((r * rstd) * g_ref[...]).astype(BF16)

    row = pl.BlockSpec((bm, bn), lambda i, n, k: (i, n))
    vec = pl.BlockSpec((1, bn), lambda i, n, k: (0, 0))
    in_specs = [a_spec, b_spec]
    args = [a, b]
    if res is not None:
        in_specs.append(row)
        args.append(res)
    for g in norms:
        in_specs.append(vec)
        args.append(g.reshape(1, -1))
    out_specs, out_shapes = [o_spec], [o_shape]
    for _ in norms:
        out_specs.append(row)
        out_shapes.append(jax.ShapeDtypeStruct(o_shape.shape, BF16))
    if norm_bwd is not None:
        x, g, dres = norm_bwd
        in_specs += [row, vec, row]
        args += [x, g.reshape(1, -1), dres]
        out_specs.append(pl.BlockSpec((SUBLANES, bn), lambda i, n, k: (0, 0)))
        out_shapes.append(jax.ShapeDtypeStruct((SUBLANES, o_shape.shape[1]), F32))
    sem = ("arbitrary",) * 3 if norm_bwd is not None else ("parallel", "parallel", "arbitrary")
    single = len(out_specs) == 1
    out = pl.pallas_call(
        body, name=name, grid=grid, in_specs=in_specs, out_specs=out_specs[0] if single else out_specs,
        out_shape=out_shapes[0] if single else out_shapes,
        scratch_shapes=[pltpu.VMEM((bm, bn), F32)],
        compiler_params=_params(sem),
    )(*args)
    if norm_bwd is not None:
        return out[0], jnp.sum(out[1], axis=0)
    return out


def _mm_grad(a, dy, jn, *, name, scale=1.0, bm=1024, bn=1024, bk=1024):
    halves = dy if isinstance(dy, (tuple, list)) else (dy,)
    t, kdim = a.shape
    ns = len(halves) * halves[0].shape[1] // jn
    bm, bn, bk = _blk(kdim, bm), _blk(ns, bn), _blk(t, bk)
    npj = ns // bn
    grid = (kdim // bm, jn * npj, t // bk)
    nk = grid[2]
    nhalf = jn * npj // len(halves)
    dims = (((0,), (0,)), ((), ()))

    def body(a_ref, *refs):
        b_refs, o_ref, acc_ref = refs[:len(halves)], refs[-2], refs[-1]
        n, k = pl.program_id(1), pl.program_id(2)

        @pl.when(k == 0)
        def _():
            acc_ref[...] = jnp.zeros_like(acc_ref)

        for which, b_ref in enumerate(b_refs):
            @pl.when(n // nhalf == which)
            def _(b_ref=b_ref):
                acc_ref[...] += lax.dot_general(a_ref[...].astype(BF16), b_ref[...].astype(BF16), dims,
                                                preferred_element_type=F32)

        @pl.when(k == nk - 1)
        def _():
            r = acc_ref[...]
            if scale != 1.0:
                r = r * scale
            o_ref[...] = r

    def half_spec(which):
        return pl.BlockSpec((bk, bn), lambda m, n, k: (jnp.where(n // nhalf == which, k, 0),
                                                        jnp.where(n // nhalf == which, n % nhalf, 0)))

    return pl.pallas_call(
        body, name=name, grid=grid,
        in_specs=[pl.BlockSpec((bk, bm), lambda m, n, k: (k, m))] + [half_spec(w) for w in range(len(halves))],
        out_specs=pl.BlockSpec((None, bm, bn), lambda m, n, k: (n // npj, m, n % npj)),
        out_shape=jax.ShapeDtypeStruct((jn, kdim, ns), F32),
        scratch_shapes=[pltpu.VMEM((bm, bn), F32)],
        compiler_params=_params(("parallel", "parallel", "arbitrary")),
    )(a, *halves)


def _mm_back2(dy_halves, w, norm_bwd, *, name, bm=FUSED_ROWS, bk=1024):
    x, g, dres = norm_bwd
    m = dy_halves[0].shape[0]
    jn, kdim, ns = w.shape
    bm, bk = _blk(m, bm), _blk(ns, bk)
    kpj = ns // bk
    nk = jn * kpj
    khalf = nk // 2

    def body(a0_ref, a1_ref, b_ref, x_ref, g_ref, dres_ref, o_ref, dg_ref, acc_ref):
        i, k = pl.program_id(0), pl.program_id(1)

        @pl.when(k == 0)
        def _():
            acc_ref[...] = jnp.zeros_like(acc_ref)

        @pl.when((i == 0) & (k == 0))
        def _():
            dg_ref[...] = jnp.zeros_like(dg_ref)

        for which, a_ref in enumerate((a0_ref, a1_ref)):
            @pl.when(k // khalf == which)
            def _(a_ref=a_ref):
                acc_ref[...] += lax.dot_general(a_ref[...], b_ref[...], _NT, preferred_element_type=F32)

        @pl.when(k == nk - 1)
        def _():
            dx, dg8 = _rms_bwd_tile(x_ref[...], g_ref[...], acc_ref[...], dres_ref[...])
            o_ref[...] = dx
            dg_ref[...] += dg8

    def half_spec(which):
        return pl.BlockSpec((bm, bk), lambda i, k: (i, jnp.clip(k - which * khalf, 0, khalf - 1)))

    row = pl.BlockSpec((bm, kdim), lambda i, k: (i, 0))
    dx, dg = pl.pallas_call(
        body, name=name, grid=(m // bm, nk),
        in_specs=[half_spec(0), half_spec(1),
                  pl.BlockSpec((None, kdim, bk), lambda i, k: (k // kpj, 0, k % kpj)),
                  row, pl.BlockSpec((1, kdim), lambda i, k: (0, 0)), row],
        out_specs=[row, pl.BlockSpec((SUBLANES, kdim), lambda i, k: (0, 0))],
        out_shape=[jax.ShapeDtypeStruct((m, kdim), F32), jax.ShapeDtypeStruct((SUBLANES, kdim), F32)],
        scratch_shapes=[pltpu.VMEM((bm, kdim), F32)],
        compiler_params=_params(("arbitrary", "arbitrary")),
    )(dy_halves[0], dy_halves[1], w, x, g.reshape(1, -1), dres)
    return dx, jnp.sum(dg, axis=0)


def _ffn_in_act(n, w_in, *, name, bm=512):
    m, kdim = n.shape
    jn, _, ns = w_in.shape
    f = jn * ns // 2
    bm = _blk(m, bm)
    bn = _blk(ns, WIDE)
    npj = ns // bn
    nf = f // bn

    def body(n_ref, wg_ref, wu_ref, g_ref, u_ref, a_ref):
        nv = n_ref[...]
        g = jnp.dot(nv, wg_ref[...], preferred_element_type=F32)
        u = jnp.dot(nv, wu_ref[...], preferred_element_type=F32)
        g_ref[...] = g.astype(BF16)
        u_ref[...] = u.astype(BF16)
        a_ref[...] = (g * jax.nn.sigmoid(g) * u).astype(BF16)

    out = jax.ShapeDtypeStruct((m, f), BF16)
    ospec = pl.BlockSpec((bm, bn), lambda c, i: (i, c))
    return pl.pallas_call(
        body, name=name, grid=(nf, m // bm),
        in_specs=[pl.BlockSpec((bm, kdim), lambda c, i: (i, 0)),
                  pl.BlockSpec((None, kdim, bn), lambda c, i: (c // npj, 0, c % npj)),
                  pl.BlockSpec((None, kdim, bn), lambda c, i: ((c + nf) // npj, 0, (c + nf) % npj))],
        out_specs=[ospec, ospec, ospec], out_shape=[out, out, out],
        compiler_params=_params(("parallel", "parallel")),
    )(n, w_in, w_in)


def _ffn_out_dx_act(dh, w_out, gate, up, *, name, scale, bm=512):
    m, d = dh.shape
    f = w_out.shape[1]
    bm = _blk(m, bm)
    bn = _blk(f, WIDE)

    def body(dh_ref, w_ref, g_ref, u_ref, dg_ref, du_ref):
        da = lax.dot_general(dh_ref[...].astype(BF16), w_ref[...], _NT, preferred_element_type=F32) * scale
        g = g_ref[...].astype(F32)
        sg = jax.nn.sigmoid(g)
        silu = g * sg
        dg_ref[...] = (da * u_ref[...].astype(F32) * (sg + silu * (1.0 - sg))).astype(BF16)
        du_ref[...] = (da * silu).astype(BF16)

    out = jax.ShapeDtypeStruct((m, f), BF16)
    spec = pl.BlockSpec((bm, bn), lambda i, c: (i, c))
    return pl.pallas_call(
        body, name=name, grid=(m // bm, f // bn),
        in_specs=[pl.BlockSpec((bm, d), lambda i, c: (i, 0)), pl.BlockSpec((None, bn, d), lambda i, c: (0, c, 0)),
                  spec, spec],
        out_specs=[spec, spec], out_shape=[out, out],
        compiler_params=_params(("parallel", "parallel")),
    )(dh, w_out, gate, up)


def _rms_fwd(x, g, *, name):
    t, d = x.shape
    tr = _blk(t, ROW_BLOCK)

    def body(x_ref, g_ref, o_ref):
        xv = x_ref[...]
        rstd = lax.rsqrt(jnp.mean(xv * xv, axis=-1, keepdims=True) + NORM_EPS)
        o_ref[...] = ((xv * rstd) * g_ref[...]).astype(BF16)

    return pl.pallas_call(
        body, name=name, grid=(t // tr,),
        in_specs=[pl.BlockSpec((tr, d), lambda i: (i, 0)), pl.BlockSpec((1, d), lambda i: (0, 0))],
        out_specs=pl.BlockSpec((tr, d), lambda i: (i, 0)),
        out_shape=jax.ShapeDtypeStruct((t, d), BF16),
        compiler_params=_params(("parallel",)),
    )(x, g.reshape(1, d))


def _loss_fwd_bwd(h, target, *, name):
    t, d = h.shape
    tr = _blk(t, ROW_BLOCK)

    def body(h_ref, t_ref, dh_ref, l_ref):
        i = pl.program_id(0)
        err = h_ref[...] - t_ref[...]
        dh_ref[...] = err * (1.0 / d)

        @pl.when(i == 0)
        def _():
            l_ref[...] = jnp.zeros_like(l_ref)

        l_ref[...] += _fold8(err * err)

    dh, part = pl.pallas_call(
        body, name=name, grid=(t // tr,),
        in_specs=[pl.BlockSpec((tr, d), lambda i: (i, 0)), pl.BlockSpec((tr, d), lambda i: (i, 0))],
        out_specs=[pl.BlockSpec((tr, d), lambda i: (i, 0)), pl.BlockSpec((SUBLANES, d), lambda i: (0, 0))],
        out_shape=[jax.ShapeDtypeStruct((t, d), F32), jax.ShapeDtypeStruct((SUBLANES, d), F32)],
        compiler_params=_params(("arbitrary",)),
    )(h, target)
    return jnp.sum(part) * (0.5 / d), dh


def _seg_matrix():
    r = lax.broadcasted_iota(jnp.int32, (LANES, LANES), 0) // HEAD_DIM
    c = lax.broadcasted_iota(jnp.int32, (LANES, LANES), 1) // HEAD_DIM
    return (r == c).astype(BF16)


def _head_sum(x, seg, terms=3):
    hi = x.astype(BF16)
    r1 = x - hi.astype(F32)
    mid = r1.astype(BF16)
    dot = functools.partial(jnp.dot, preferred_element_type=F32)
    if terms == 2:
        return dot(hi, seg) + dot(mid, seg)
    lo = (r1 - mid.astype(F32)).astype(BF16)
    return dot(hi, seg) + dot(mid, seg) + dot(lo, seg)


def _lane_in_head(shape):
    return lax.broadcasted_iota(jnp.int32, shape, 1) % HEAD_DIM


def _half_mask(shape):
    return lax.broadcasted_iota(jnp.int32, shape, 1) < HEAD_DIM


def _rot_partner(x):
    up = pltpu.roll(x, LANES - ROT_DIM // 2, 1)
    down = pltpu.roll(x, ROT_DIM // 2, 1)
    return jnp.where(_lane_in_head(x.shape) < ROT_DIM // 2, up, down)


def _rope_tables(positions):
    inv_freq = ROPE_THETA ** (-jnp.arange(0, ROT_DIM, 2, dtype=F32) / ROT_DIM)
    ang = positions.astype(F32)[:, None] * inv_freq
    t = ang.shape[0]
    rest = HEAD_DIM - ROT_DIM
    cos = jnp.concatenate([jnp.cos(ang), jnp.cos(ang), jnp.ones((t, rest), F32)], axis=1)
    sin = jnp.concatenate([-jnp.sin(ang), jnp.sin(ang), jnp.zeros((t, rest), F32)], axis=1)
    return jnp.tile(cos, (1, LANES // HEAD_DIM)), jnp.tile(sin, (1, LANES // HEAD_DIM))


def _kind_is(j, kinds, kind):
    hits = [j == jj for jj, k in enumerate(kinds) if k == kind]
    return functools.reduce(jnp.logical_or, hits) if hits else None


def _hn_fwd(x, gains, kinds, d, cos, sin, *, name, col0=0):
    t = x.shape[0]
    n = len(kinds)
    tr = _blk(t, ROW_BLOCK)
    seg = _seg_matrix()
    g8 = jnp.repeat(gains.astype(F32), SUBLANES, axis=0)

    def body(x_ref, g_ref, seg_ref, cos_ref, sin_ref, o_ref):
        j = pl.program_id(1)

        def normed(rope):
            for c in range(d // LANES):
                sl = slice(c * LANES, (c + 1) * LANES)
                xv = x_ref[:, sl]
                ms = _head_sum(xv * xv, seg_ref[...], terms=2) * (1.0 / HEAD_DIM)
                y = (xv * lax.rsqrt(ms + NORM_EPS)) * g_ref[0:1, sl]
                if rope:
                    y = y * cos_ref[...] + _rot_partner(y) * sin_ref[...]
                o_ref[:, sl] = y.astype(BF16)

        for kind in ("rope", "norm"):
            hit = _kind_is(j, kinds, kind)
            if hit is not None:
                pl.when(hit)(functools.partial(normed, kind == "rope"))
        hit = _kind_is(j, kinds, "cast")
        if hit is not None:
            @pl.when(hit)
            def _():
                o_ref[...] = x_ref[...].astype(BF16)

    return pl.pallas_call(
        body, name=name, grid=(t // tr, n),
        in_specs=[pl.BlockSpec((tr, d), lambda i, j: (i, col0 + j)), pl.BlockSpec((SUBLANES, d), lambda i, j: (j, 0)),
                  pl.BlockSpec((LANES, LANES), lambda i, j: (0, 0)),
                  pl.BlockSpec((tr, LANES), lambda i, j: (i, 0)), pl.BlockSpec((tr, LANES), lambda i, j: (i, 0))],
        out_specs=pl.BlockSpec((tr, d), lambda i, j: (i, j)),
        out_shape=jax.ShapeDtypeStruct((t, n * d), BF16),
        compiler_params=_params(("parallel", "parallel")),
    )(x, g8, seg, cos, sin)


def _hn_bwd(x, dys, gains, kinds, d, cos, sin, *, name, col0=0):
    t = x.shape[0]
    n = len(kinds)
    tr = _blk(t, ROW_BLOCK // 2)
    seg = _seg_matrix()
    g8 = jnp.repeat(gains.astype(F32), SUBLANES, axis=0)

    def body(x_ref, *refs):
        dy_refs = refs[:n]
        g_ref, seg_ref, cos_ref, sin_ref, dx_ref, dg_ref = refs[n:]
        j = pl.program_id(0)
        i = pl.program_id(1)

        @pl.when(i == 0)
        def _():
            dg_ref[...] = jnp.zeros_like(dg_ref)

        def normed(rope, dy_ref):
            for c in range(d // LANES):
                sl = slice(c * LANES, (c + 1) * LANES)
                xv = x_ref[:, sl]
                dyv = dy_ref[:, sl]
                if rope:
                    dyv = dyv * cos_ref[...] - _rot_partner(dyv) * sin_ref[...]
                ms = _head_sum(xv * xv, seg_ref[...], terms=2) * (1.0 / HEAD_DIM)
                rstd = lax.rsqrt(ms + NORM_EPS)
                xhat = xv * rstd
                dg_ref[:, sl] += _fold8(dyv * xhat)
                dyg = dyv * g_ref[0:1, sl]
                proj = _head_sum(dyg * xhat, seg_ref[...], terms=2) * (1.0 / HEAD_DIM)
                dx_ref[:, sl] = (rstd * (dyg - xhat * proj)).astype(BF16)

        def cast(dy_ref):
            dx_ref[...] = dy_ref[...].astype(BF16)

        for jj, kind in enumerate(kinds):
            if kind == "cast":
                pl.when(j == jj)(functools.partial(cast, dy_refs[jj]))
            else:
                pl.when(j == jj)(functools.partial(normed, kind == "rope", dy_refs[jj]))

    def dy_spec(jj):
        return pl.BlockSpec((tr, d), lambda j, i: (jnp.where(j == jj, i, 0), 0))

    dx, dg = pl.pallas_call(
        body, name=name, grid=(n, t // tr),
        in_specs=[pl.BlockSpec((tr, d), lambda j, i: (i, col0 + j))] + [dy_spec(jj) for jj in range(n)] + [
                  pl.BlockSpec((SUBLANES, d), lambda j, i: (j, 0)),
                  pl.BlockSpec((LANES, LANES), lambda j, i: (0, 0)),
                  pl.BlockSpec((tr, LANES), lambda j, i: (i, 0)), pl.BlockSpec((tr, LANES), lambda j, i: (i, 0))],
        out_specs=[pl.BlockSpec((tr, d), lambda j, i: (i, j)), pl.BlockSpec((SUBLANES, d), lambda j, i: (j, 0))],
        out_shape=[jax.ShapeDtypeStruct((t, n * d), BF16), jax.ShapeDtypeStruct((n * SUBLANES, d), F32)],
        compiler_params=_params(("arbitrary", "arbitrary")),
    )(x, *dys, g8, seg, cos, sin)
    dg = dg.reshape(n, SUBLANES, d // HEAD_DIM, HEAD_DIM).sum(axis=(1, 2))
    return dx, dg


def _band_valid_t(first):
    s = lax.broadcasted_iota(jnp.int32, (2 * BAND, BAND), 0)
    t = lax.broadcasted_iota(jnp.int32, (2 * BAND, BAND), 1)
    dist = t + BAND - s
    return (dist >= 0) & (dist <= BAND) & ((s >= BAND) | jnp.logical_not(first))


def _to_classes_t(z, dil, width):
    return z.reshape(z.shape[0] // dil, dil, width).transpose(1, 2, 0)


def _from_classes_t(z):
    dil, width, rows = z.shape
    return z.transpose(2, 0, 1).reshape(rows * dil, width)


def _band_fwd_n(nat, dil, cfg, *, name):
    d, hh = cfg.d_model, cfg.heads
    rows = cfg.tokens // dil
    nbt = rows // BAND
    nb = cfg.seq // (dil * BAND)

    def body(q_ref, kp_ref, kc_ref, vp_ref, vc_ref, o_ref, lse_ref):
        i = pl.program_id(1)
        valid1 = _band_valid_t(i % nb == 0)
        valid = jnp.concatenate([valid1, valid1], axis=1)
        upper = lax.broadcasted_iota(jnp.int32, (LANES, BAND), 0) < HEAD_DIM
        for hp in range(d // LANES):
            pair = slice(hp * LANES, (hp + 1) * LANES)
            h0, h1 = 2 * hp, 2 * hp + 1
            qt2 = q_ref[:, pair].T
            zero = jnp.zeros_like(qt2)
            qboth = jnp.concatenate([jnp.where(upper, qt2, zero), jnp.where(upper, zero, qt2)], axis=1)
            kk = jnp.concatenate([kp_ref[:, pair], kc_ref[:, pair]], axis=0)
            vvt = jnp.concatenate([vp_ref[:, pair], vc_ref[:, pair]], axis=0).T
            s = jnp.where(valid, jnp.dot(kk, qboth, preferred_element_type=F32), NEG)
            m = jnp.max(s, axis=0, keepdims=True)
            p = jnp.exp(s - m)
            l = jnp.sum(p, axis=0, keepdims=True)
            hi = p.astype(BF16)
            lo = (p - hi.astype(F32)).astype(BF16)
            o2 = (jnp.dot(vvt, hi, preferred_element_type=F32) + jnp.dot(vvt, lo, preferred_element_type=F32)) * (1.0 / l)
            o_ref[:, pair] = jnp.concatenate([o2[:HEAD_DIM, :BAND], o2[HEAD_DIM:, BAND:]], axis=0).T
            lse2 = m + jnp.log(l)
            lse_ref[h0:h0 + 1, :] = lse2[:, :BAND]
            lse_ref[h1:h1 + 1, :] = lse2[:, BAND:]

    def prev(i):
        return jnp.maximum(i - 1, 0)

    blk = (BAND, d)
    return pl.pallas_call(
        body, name=name, grid=(dil, nbt),
        in_specs=[pl.BlockSpec(blk, lambda r, i: (i, r * 3)),
                  pl.BlockSpec(blk, lambda r, i: (prev(i), r * 3 + 1)),
                  pl.BlockSpec(blk, lambda r, i: (i, r * 3 + 1)),
                  pl.BlockSpec(blk, lambda r, i: (prev(i), r * 3 + 2)),
                  pl.BlockSpec(blk, lambda r, i: (i, r * 3 + 2))],
        out_specs=[pl.BlockSpec(blk, lambda r, i: (i, r)),
                   pl.BlockSpec((None, hh, BAND), lambda r, i: (r, 0, i))],
        out_shape=[jax.ShapeDtypeStruct((rows, dil * d), F32), jax.ShapeDtypeStruct((dil, hh, rows), F32)],
        compiler_params=_params(("parallel", "arbitrary")),
    )(nat, nat, nat, nat, nat)


def _band_bwd_n(nat, do_nat, lse_c, dsum_c, dil, cfg, *, name):
    d, hh = cfg.d_model, cfg.heads
    rows = cfg.tokens // dil
    nbt = rows // BAND
    nb = cfg.seq // (dil * BAND)

    def body(q_ref, kp_ref, kc_ref, vp_ref, vc_ref, do_ref, l_ref, ds_ref, dq_ref, dk_ref, dv_ref, ck_ref, cv_ref):
        i = pl.program_id(1)

        @pl.when(i < nbt)
        def _():
            @pl.when(i == 0)
            def _():
                ck_ref[...] = jnp.zeros_like(ck_ref)
                cv_ref[...] = jnp.zeros_like(cv_ref)

            valid1 = _band_valid_t(i % nb == 0)
            valid = jnp.concatenate([valid1, valid1], axis=1)
            upper = lax.broadcasted_iota(jnp.int32, (LANES, BAND), 0) < HEAD_DIM
            half2 = _half_mask((2 * BAND, LANES))

            def both(z):
                zero = jnp.zeros_like(z)
                return jnp.concatenate([jnp.where(upper, z, zero), jnp.where(upper, zero, z)], axis=1)

            def stack(z):
                return jnp.concatenate([z[:, :BAND], z[:, BAND:]], axis=0)

            for hp in range(d // LANES):
                pair = slice(hp * LANES, (hp + 1) * LANES)
                h0, h1 = 2 * hp, 2 * hp + 1
                qn2, don2 = q_ref[:, pair], do_ref[:, pair]
                kk = jnp.concatenate([kp_ref[:, pair], kc_ref[:, pair]], axis=0)
                vv = jnp.concatenate([vp_ref[:, pair], vc_ref[:, pair]], axis=0)
                lse2 = jnp.concatenate([l_ref[h0:h0 + 1, :], l_ref[h1:h1 + 1, :]], axis=1)
                dsum2 = jnp.concatenate([ds_ref[h0:h0 + 1, :], ds_ref[h1:h1 + 1, :]], axis=1)
                s = jnp.where(valid, jnp.dot(kk, both(qn2.T), preferred_element_type=F32), NEG)
                p = jnp.exp(s - lse2)
                dp = jnp.dot(vv, both(don2.T), preferred_element_type=F32)
                dsb = (p * (dp - dsum2)).astype(BF16)
                dq2 = jnp.dot(kk.T, dsb, preferred_element_type=F32)
                dq_ref[:, pair] = jnp.concatenate([dq2[:HEAD_DIM, :BAND], dq2[HEAD_DIM:, BAND:]], axis=0).T
                dk2 = jnp.dot(stack(dsb), qn2, preferred_element_type=F32)
                dv2 = jnp.dot(stack(p.astype(BF16)), don2, preferred_element_type=F32)
                dkk = jnp.where(half2, dk2[:2 * BAND], dk2[2 * BAND:])
                dvv = jnp.where(half2, dv2[:2 * BAND], dv2[2 * BAND:])
                dk_ref[:, pair] = ck_ref[:, pair] + dkk[:BAND]
                dv_ref[:, pair] = cv_ref[:, pair] + dvv[:BAND]
                ck_ref[:, pair] = dkk[BAND:]
                cv_ref[:, pair] = dvv[BAND:]

        @pl.when(i == nbt)
        def _():
            dk_ref[...] = ck_ref[...]
            dv_ref[...] = cv_ref[...]

    def cur(i):
        return jnp.minimum(i, nbt - 1)

    def prev(i):
        return jnp.maximum(cur(i) - 1, 0)

    cblk = (None, hh, BAND)
    blk = (BAND, d)
    here = pl.BlockSpec(blk, lambda r, i: (cur(i), r))
    behind = pl.BlockSpec(blk, lambda r, i: (jnp.maximum(i - 1, 0), r))
    shape = jax.ShapeDtypeStruct((rows, dil * d), F32)
    return pl.pallas_call(
        body, name=name, grid=(dil, nbt + 1),
        in_specs=[pl.BlockSpec(blk, lambda r, i: (cur(i), r * 3)),
                  pl.BlockSpec(blk, lambda r, i: (prev(i), r * 3 + 1)),
                  pl.BlockSpec(blk, lambda r, i: (cur(i), r * 3 + 1)),
                  pl.BlockSpec(blk, lambda r, i: (prev(i), r * 3 + 2)),
                  pl.BlockSpec(blk, lambda r, i: (cur(i), r * 3 + 2)),
                  here,
                  pl.BlockSpec(cblk, lambda r, i: (r, 0, cur(i))),
                  pl.BlockSpec(cblk, lambda r, i: (r, 0, cur(i)))],
        out_specs=[here, behind, behind],
        out_shape=[shape, shape, shape],
        scratch_shapes=[pltpu.VMEM(blk, F32), pltpu.VMEM(blk, F32)],
        compiler_params=_params(("arbitrary", "arbitrary")),
    )(nat, nat, nat, nat, nat, do_nat, lse_c, dsum_c)


def _mix_fwd(outs, lses, *, name):
    t, d = outs[0].shape
    hh = lses[0].shape[1]
    tr = _blk(t, ROW_BLOCK)
    ng = len(outs)
    spread = (lax.broadcasted_iota(jnp.int32, (hh, d), 0)
              == lax.broadcasted_iota(jnp.int32, (hh, d), 1) // HEAD_DIM).astype(BF16)

    def body(*refs):
        o_refs, l_refs = refs[:ng], refs[ng:2 * ng]
        spread_ref, mixed_ref, lse_ref = refs[2 * ng:]
        ls = [r[...] for r in l_refs]
        m = functools.reduce(jnp.maximum, ls)
        es = [jnp.exp(l - m) for l in ls]
        tot = functools.reduce(jnp.add, es)
        inv = 1.0 / tot
        mixed_ref[...] = functools.reduce(
            jnp.add, [_head_sum(e * inv, spread_ref[...]) * r[...] for e, r in zip(es, o_refs)])
        lse_ref[...] = m + jnp.log(tot)

    spec = pl.BlockSpec((tr, d), lambda i: (i, 0))
    cspec = pl.BlockSpec((tr, hh), lambda i: (i, 0))
    return pl.pallas_call(
        body, name=name, grid=(t // tr,),
        in_specs=[spec] * ng + [cspec] * ng + [pl.BlockSpec((hh, d), lambda i: (0, 0))], out_specs=[spec, cspec],
        out_shape=[jax.ShapeDtypeStruct((t, d), F32), jax.ShapeDtypeStruct((t, hh), F32)],
        compiler_params=_params(("parallel",)),
    )(*outs, *lses, spread)


GATE_BLOCK = 256


def _tri(n, upper):
    r = lax.broadcasted_iota(jnp.int32, (n, n), 0)
    c = lax.broadcasted_iota(jnp.int32, (n, n), 1)
    return ((c >= r) if upper else (c <= r)).astype(BF16)


def _tri_dot(tri, x):
    hi = x.astype(BF16)
    r1 = x - hi.astype(F32)
    mid = r1.astype(BF16)
    lo = (r1 - mid.astype(F32)).astype(BF16)
    dot = functools.partial(jnp.dot, preferred_element_type=F32)
    return dot(tri, hi) + dot(tri, mid) + dot(tri, lo)


def _log_sigmoid(z):
    return jnp.minimum(z, 0.0) - jnp.log(1.0 + jnp.exp(-jnp.abs(z)))


def _gate_fwd(proj, col_block, bias, cfg, *, name):
    tr = _blk(cfg.seq, GATE_BLOCK)
    nblk = cfg.seq // tr

    def body(z_ref, b_ref, tri_ref, o_ref, carry_ref):
        i = pl.program_id(1)

        @pl.when(i == 0)
        def _():
            carry_ref[...] = jnp.zeros_like(carry_ref)

        logf = _log_sigmoid(z_ref[...] + b_ref[0:1, :])
        cum = _tri_dot(tri_ref[...], logf) + carry_ref[0:1, :]
        o_ref[...] = cum
        carry_ref[...] = jnp.broadcast_to(cum[tr - 1:tr, :], carry_ref.shape)

    return pl.pallas_call(
        body, name=name, grid=(cfg.batch, nblk),
        in_specs=[pl.BlockSpec((tr, LANES), lambda b, i: (b * nblk + i, col_block)),
                  pl.BlockSpec((SUBLANES, LANES), lambda b, i: (0, 0)),
                  pl.BlockSpec((tr, tr), lambda b, i: (0, 0))],
        out_specs=pl.BlockSpec((tr, LANES), lambda b, i: (b * nblk + i, 0)),
        out_shape=jax.ShapeDtypeStruct((cfg.tokens, LANES), F32),
        scratch_shapes=[pltpu.VMEM((SUBLANES, LANES), F32)],
        compiler_params=_params(("arbitrary", "arbitrary")),
    )(proj, jnp.broadcast_to(bias, (SUBLANES, LANES)), _tri(tr, upper=False))


def _gate_bwd(proj, col_block, bias, dcum, cfg, *, name):
    tr = _blk(cfg.seq, GATE_BLOCK)
    nblk = cfg.seq // tr

    def body(z_ref, b_ref, tri_ref, dc_ref, dz_ref, db_ref, carry_ref):
        b = pl.program_id(0)
        i = pl.program_id(1)

        @pl.when(i == 0)
        def _():
            carry_ref[...] = jnp.zeros_like(carry_ref)

        @pl.when((i == 0) & (b == 0))
        def _():
            db_ref[...] = jnp.zeros_like(db_ref)

        dcv = dc_ref[...]
        dlogf = _tri_dot(tri_ref[...], dcv) + carry_ref[0:1, :]
        carry_ref[...] = jnp.broadcast_to(dlogf[0:1, :], carry_ref.shape)
        dz = dlogf * jax.nn.sigmoid(-(z_ref[...] + b_ref[0:1, :]))
        dz_ref[...] = dz
        db_ref[...] += _fold8(dz)

    def rev(b, i):
        return (b * nblk + nblk - 1 - i, 0)

    dz, db = pl.pallas_call(
        body, name=name, grid=(cfg.batch, nblk),
        in_specs=[pl.BlockSpec((tr, LANES), lambda b, i: (b * nblk + nblk - 1 - i, col_block)),
                  pl.BlockSpec((SUBLANES, LANES), lambda b, i: (0, 0)),
                  pl.BlockSpec((tr, tr), lambda b, i: (0, 0)),
                  pl.BlockSpec((tr, LANES), rev)],
        out_specs=[pl.BlockSpec((tr, LANES), rev), pl.BlockSpec((SUBLANES, LANES), lambda b, i: (0, 0))],
        out_shape=[jax.ShapeDtypeStruct((cfg.tokens, LANES), F32), jax.ShapeDtypeStruct((SUBLANES, LANES), F32)],
        scratch_shapes=[pltpu.VMEM((SUBLANES, LANES), F32)],
        compiler_params=_params(("arbitrary", "arbitrary")),
    )(proj, jnp.broadcast_to(bias, (SUBLANES, LANES)), _tri(tr, upper=True), dcum)
    return dz, jnp.sum(db, axis=0)


FOX_BLOCK = 256
AUG = LANES
BIAS_TERMS = 3


def _fox_aug_k_call(kv, cum, cfg, *, name):
    t, d, hh = cfg.tokens, cfg.d_model, cfg.heads
    tr = _blk(t, ROW_BLOCK)

    def body(k_ref, c_ref, o_ref):
        lane = lax.broadcasted_iota(jnp.int32, (tr, LANES), 1)
        for hp in range(hh // 2):
            k2 = k_ref[:, hp * LANES:(hp + 1) * LANES].astype(F32)
            for e in range(2):
                h = 2 * hp + e
                kh = k2 if e == 0 else pltpu.roll(k2, HEAD_DIM, 1)
                c = -c_ref[:, h:h + 1]
                hi = c.astype(BF16).astype(F32)
                mid = (c - hi).astype(BF16).astype(F32)
                lo = c - hi - mid
                bias = jnp.where(lane == HEAD_DIM, hi, jnp.where(lane == HEAD_DIM + 1, mid,
                                 jnp.where(lane == HEAD_DIM + 2, lo, 0.0)))
                o_ref[:, h * AUG:(h + 1) * AUG] = jnp.where(lane < HEAD_DIM, kh, bias).astype(BF16)

    return pl.pallas_call(
        body, name=name, grid=(t // tr,),
        in_specs=[pl.BlockSpec((tr, d), lambda i: (i, 0)), pl.BlockSpec((tr, LANES), lambda i: (i, 0))],
        out_specs=pl.BlockSpec((tr, hh * AUG), lambda i: (i, 0)),
        out_shape=jax.ShapeDtypeStruct((t, hh * AUG), BF16),
        compiler_params=_params(("parallel",)),
    )(kv, cum)


def _keys_visible(tq):
    s = lax.broadcasted_iota(jnp.int32, (tq, tq), 0)
    t = lax.broadcasted_iota(jnp.int32, (tq, tq), 1)
    return s <= t


def _aug_q_t(q2, e, tq):
    ones = (lax.broadcasted_iota(jnp.int32, (AUG - HEAD_DIM, tq), 0) < BIAS_TERMS).astype(q2.dtype)
    return jnp.concatenate([q2[e * HEAD_DIM:(e + 1) * HEAD_DIM], ones], axis=0)


def _fox_fwd_n(q, k_aug, kv, cfg, *, name):
    t, d, hh = cfg.tokens, cfg.d_model, cfg.heads
    tq = _blk(cfg.seq, FOX_BLOCK)
    nq = cfg.seq // tq

    def body(q_ref, ka_ref, v_ref, o_ref, lse_ref, qa_ref, m_ref, l_ref, acc_ref):
        qi, ki = pl.program_id(1), pl.program_id(2)

        @pl.when(ki == 0)
        def _():
            m_ref[...] = jnp.full_like(m_ref, NEG)
            l_ref[...] = jnp.zeros_like(l_ref)
            acc_ref[...] = jnp.zeros_like(acc_ref)
            for hp in range(hh // 2):
                q2 = q_ref[:, hp * LANES:(hp + 1) * LANES].T
                for e in range(2):
                    h = 2 * hp + e
                    qa_ref[h * AUG:(h + 1) * AUG, :] = _aug_q_t(q2, e, tq)

        def step(diagonal):
            for hp in range(hh // 2):
                vt2 = v_ref[:, hp * LANES:(hp + 1) * LANES].T
                for e in range(2):
                    h = 2 * hp + e
                    rows = slice(h * HEAD_DIM, (h + 1) * HEAD_DIM)
                    s = jnp.dot(ka_ref[:, h * AUG:(h + 1) * AUG], qa_ref[h * AUG:(h + 1) * AUG, :],
                                preferred_element_type=F32)
                    if diagonal:
                        s = jnp.where(_keys_visible(tq), s, NEG)
                    m_prev = m_ref[h:h + 1, :]
                    m_new = jnp.maximum(m_prev, jnp.max(s, axis=0, keepdims=True))
                    alpha = jnp.exp(m_prev - m_new)
                    p = jnp.exp(s - m_new)
                    l_ref[h:h + 1, :] = alpha * l_ref[h:h + 1, :] + jnp.sum(p, axis=0, keepdims=True)
                    m_ref[h:h + 1, :] = m_new
                    hi = p.astype(BF16)
                    lo = (p - hi.astype(F32)).astype(BF16)
                    vt = vt2[e * HEAD_DIM:(e + 1) * HEAD_DIM]
                    acc_ref[rows, :] = (alpha * acc_ref[rows, :] + jnp.dot(vt, hi, preferred_element_type=F32)
                                        + jnp.dot(vt, lo, preferred_element_type=F32))

        pl.when(ki < qi)(functools.partial(step, False))
        pl.when(ki == qi)(functools.partial(step, True))

        @pl.when(ki == qi)
        def _():
            for hp in range(hh // 2):
                halves = [acc_ref[h * HEAD_DIM:(h + 1) * HEAD_DIM, :] * (1.0 / l_ref[h:h + 1, :])
                          for h in (2 * hp, 2 * hp + 1)]
                o_ref[:, hp * LANES:(hp + 1) * LANES] = jnp.concatenate(halves, axis=0).T
            lse_ref[...] = m_ref[...] + jnp.log(l_ref[...])

    def qrow(b, qi, ki):
        return (b * nq + qi, 0)

    return pl.pallas_call(
        body, name=name, grid=(cfg.batch, nq, nq),
        in_specs=[pl.BlockSpec((tq, d), qrow),
                  pl.BlockSpec((tq, hh * AUG), lambda b, qi, ki: (b * nq + jnp.minimum(ki, qi), 0)),
                  pl.BlockSpec((tq, d), lambda b, qi, ki: (b * nq + jnp.minimum(ki, qi), 1))],
        out_specs=[pl.BlockSpec((tq, d), qrow), pl.BlockSpec((hh, tq), lambda b, qi, ki: (0, b * nq + qi))],
        out_shape=[jax.ShapeDtypeStruct((t, d), F32), jax.ShapeDtypeStruct((hh, t), F32)],
        scratch_shapes=[pltpu.VMEM((hh * AUG, tq), BF16), pltpu.VMEM((hh, tq), F32), pltpu.VMEM((hh, tq), F32),
                        pltpu.VMEM((d, tq), F32)],
        compiler_params=_params(("parallel", "parallel", "arbitrary")),
    )(q, k_aug, kv)


def _head_dot_c(a, b, cfg, *, name):
    t, d, hh = cfg.tokens, cfg.d_model, cfg.heads
    tc = _blk(t, ROW_BLOCK)

    def body(a_ref, b_ref, o_ref):
        for hp in range(hh // 2):
            pair = slice(hp * LANES, (hp + 1) * LANES)
            prod = (a_ref[:, pair].astype(F32) * b_ref[:, pair]).T
            for e in range(2):
                h = 2 * hp + e
                o_ref[h:h + 1, :] = jnp.sum(prod[e * HEAD_DIM:(e + 1) * HEAD_DIM], axis=0, keepdims=True)

    return pl.pallas_call(
        body, name=name, grid=(t // tc,),
        in_specs=[pl.BlockSpec((tc, d), lambda i: (i, 0)), pl.BlockSpec((tc, d), lambda i: (i, 0))],
        out_specs=pl.BlockSpec((hh, tc), lambda i: (0, i)),
        out_shape=jax.ShapeDtypeStruct((hh, t), F32),
        compiler_params=_params(("parallel",)),
    )(a, b)


def _fox_bwd_n(q, k_aug, kv, do, lse, dsum, cfg, *, name):
    t, d, hh = cfg.tokens, cfg.d_model, cfg.heads
    tq = _blk(cfg.seq, FOX_BLOCK)
    nq = cfg.seq // tq

    def body(q_ref, ka_ref, v_ref, do_ref, lse_ref, ds_ref, dq_hbm, dk_ref, dv_ref, dc_ref, dq_acc, sem):
        b, ki, qi = pl.program_id(0), pl.program_id(1), pl.program_id(2)
        qq = jnp.maximum(qi, ki)

        @pl.when((ki == 0) & (qi == 0))
        def _():
            dq_acc[...] = jnp.zeros_like(dq_acc)

        @pl.when(qi == 0)
        def _():
            dk_ref[...] = jnp.zeros_like(dk_ref)
            dv_ref[...] = jnp.zeros_like(dv_ref)
            dc_ref[...] = jnp.zeros_like(dc_ref)

        def step(diagonal):
            upper = lax.broadcasted_iota(jnp.int32, (LANES, tq), 0) < HEAD_DIM
            half = _half_mask((tq, LANES))
            for hp in range(hh // 2):
                pair = slice(hp * LANES, (hp + 1) * LANES)
                q2 = q_ref[:, pair].T
                don2 = do_ref[:, pair]
                dot2 = don2.T
                dvs, dks = [], []
                for e in range(2):
                    h = 2 * hp + e
                    rows = slice(h * HEAD_DIM, (h + 1) * HEAD_DIM)
                    ka = ka_ref[:, h * AUG:(h + 1) * AUG]
                    qa = _aug_q_t(q2, e, tq)
                    s = jnp.dot(ka, qa, preferred_element_type=F32)
                    if diagonal:
                        s = jnp.where(_keys_visible(tq), s, NEG)
                    p = jnp.exp(s - lse_ref[h:h + 1, :])
                    dote = jnp.where(upper == (e == 0), dot2, jnp.zeros_like(dot2))
                    dp = jnp.dot(v_ref[:, pair], dote, preferred_element_type=F32)
                    dsf = p * (dp - ds_ref[h:h + 1, :])
                    dc_ref[:, h:h + 1] -= jnp.sum(dsf, axis=1, keepdims=True)
                    dsc = dsf.astype(BF16)
                    dvs.append(jnp.dot(p.astype(BF16), don2, preferred_element_type=F32))
                    dks.append(lax.dot_general(dsc, qa, _NT, preferred_element_type=F32))
                    dq_acc[qq, rows, :] += jnp.dot(ka.T[:HEAD_DIM], dsc, preferred_element_type=F32)
                dv_ref[:, pair] += jnp.where(half, dvs[0], dvs[1])
                dk_ref[:, pair] += jnp.where(half, dks[0], pltpu.roll(dks[1], HEAD_DIM, 1))

        pl.when(qi > ki)(functools.partial(step, False))
        pl.when(qi == ki)(functools.partial(step, True))

        @pl.when((ki == nq - 1) & (qi == nq - 1))
        def _():
            cp = pltpu.make_async_copy(dq_acc, dq_hbm.at[b], sem)
            cp.start()
            cp.wait()

    def qrow(b, ki, qi):
        return (b * nq + jnp.maximum(qi, ki), 0)

    def qcol(b, ki, qi):
        return (0, b * nq + jnp.maximum(qi, ki))

    def krow(b, ki, qi):
        return (b * nq + ki, 0)

    return pl.pallas_call(
        body, name=name, grid=(cfg.batch, nq, nq),
        in_specs=[pl.BlockSpec((tq, d), qrow),
                  pl.BlockSpec((tq, hh * AUG), krow),
                  pl.BlockSpec((tq, d), lambda b, ki, qi: (b * nq + ki, 1)),
                  pl.BlockSpec((tq, d), qrow),
                  pl.BlockSpec((hh, tq), qcol), pl.BlockSpec((hh, tq), qcol)],
        out_specs=[pl.BlockSpec(memory_space=pl.ANY), pl.BlockSpec((tq, d), krow),
                   pl.BlockSpec((tq, d), krow), pl.BlockSpec((tq, LANES), krow)],
        out_shape=[jax.ShapeDtypeStruct((cfg.batch, nq, d, tq), F32), jax.ShapeDtypeStruct((t, d), F32),
                   jax.ShapeDtypeStruct((t, d), F32), jax.ShapeDtypeStruct((t, LANES), F32)],
        scratch_shapes=[pltpu.VMEM((nq, d, tq), F32), pltpu.SemaphoreType.DMA],
        compiler_params=_params(("arbitrary", "arbitrary", "arbitrary")),
    )(q, k_aug, kv, do, lse, dsum)


def _fwd(a, w, *, name, res=None, scale=1.0, norms=()):
    bm = FUSED_ROWS if norms else 1024
    out = _mm(a, w, form="F", out_dtype=F32, name=name, bm=bm, bn=WIDE, bk=WIDE, res=res, scale=scale, norms=norms)
    return (out[0], list(out[1:])) if norms else out


def _bwd(dy, w, *, name, scale=1.0, norm_bwd=None):
    bm = FUSED_ROWS if norm_bwd is not None else 1024
    return _mm(dy, w, form="B", out_dtype=F32, name=name, bm=bm, bn=WIDE, bk=WIDE, scale=scale, norm_bwd=norm_bwd)


def _wgrad(a, dy, w, *, name, scale=1.0):
    return _mm_grad(a, dy, w.shape[0], name=name, bm=WIDE, bn=WIDE, scale=scale)


def _ffn_fwd(h, n, w_in, w_out, tag, norms=()):
    gate, up, a = _ffn_in_act(n, w_in, name=f"{tag}_in")
    out = _fwd(a, w_out, name=f"{tag}_out", res=h, scale=0.5, norms=norms)
    h_out, normed = out if norms else (out, [])
    return h_out, normed, (n, gate, up, a)


def _ffn_bwd(dh_out, h, g, w_in, w_out, saved, tag):
    n, gate, up, a = saved
    du = _ffn_out_dx_act(dh_out, w_out, gate, up, name=f"{tag}_out_dx", scale=0.5)
    dw_out = _wgrad(a, dh_out, w_out, name=f"{tag}_out_dw", scale=0.5)
    dh, dg = _mm_back2(du, w_in, (h, g, dh_out), name=f"{tag}_in_dx", bk=WIDE)
    dw_in = _mm_grad(n, du, w_in.shape[0], name=f"{tag}_in_dw", bm=WIDE, bn=WIDE)
    return dh, dg, dw_in, dw_out


def _head_gain(g, heads, scale=1.0):
    return jnp.tile(g.astype(F32) * scale, heads)


def _local_step(cfg, x, positions, target, w, s):
    d, hh = cfg.d_model, cfg.heads
    cos, sin = _rope_tables(positions)
    ones = jnp.ones((d,), F32)

    n00 = _rms_fwd(x, s["ffn_norm"][0, 0], name="ffn00_norm")
    h1, (hn_a,), ffn0 = _ffn_fwd(x, n00, w["ffn_w_in"][0][0], w["ffn_w_out"][0][0], "ffn00", [s["mix_norm"][0]])
    qkv = _fwd(hn_a, w["a_w_qkv"], name="a_qkv")
    kinds_a = ["rope", "rope", "cast"] * len(DILATIONS)
    gains_a = jnp.stack([z for g in range(len(DILATIONS)) for z in (
        _head_gain(s["a_q_norm"][g], hh, Q_SCALE), _head_gain(s["a_k_norm"][g], hh), ones)])
    qkvp = [_hn_fwd(qkv, gains_a[3 * g:3 * g + 3], kinds_a[:3], d, cos, sin, name=f"a_qk_norm{g}", col0=3 * g)
            for g in range(len(DILATIONS))]
    lay = [qkvp[g].reshape(cfg.tokens // dil, dil * 3 * d) for g, dil in enumerate(DILATIONS)]
    band = [_band_fwd_n(lay[g], dil, cfg, name=f"a_band{g}") for g, dil in enumerate(DILATIONS)]
    mixed, lse_a = _mix_fwd([o.reshape(cfg.tokens, d) for o, _ in band], [_from_classes_t(l) for _, l in band],
                            name="a_mix")
    h2, (n01,) = _fwd(mixed, w["a_w_o"], name="a_out", res=h1, norms=[s["ffn_norm"][0, 1]])
    h3, (kn, n10), ffn1 = _ffn_fwd(h2, n01, w["ffn_w_in"][0][1], w["ffn_w_out"][0][1], "ffn01",
                                   [s["kv_norm"], s["ffn_norm"][1, 0]])

    proj = _fwd(kn, w["kv_w"], name="kv_proj")
    kinds_kv = ["norm", "cast"]
    gains_kv = jnp.stack([_head_gain(s["kv_k_norm"], hh), ones])
    kvp = _hn_fwd(proj, gains_kv, kinds_kv, d, cos, sin, name="kv_k_norm")
    gate_col = 2 * d // LANES
    bias = jnp.pad(s["kv_b_f"].astype(F32), (0, LANES - hh))
    cum = _gate_fwd(proj, gate_col, bias, cfg, name="kv_gate")
    k_aug = _fox_aug_k_call(kvp, cum, cfg, name="kv_aug")

    h4, (hn_b,), ffn2 = _ffn_fwd(h3, n10, w["ffn_w_in"][1][0], w["ffn_w_out"][1][0], "ffn10", [s["mix_norm"][1]])
    qraw = _fwd(hn_b, w["b_w_q"], name="b_q")
    gains_b = _head_gain(s["b_q_norm"][0], hh, Q_SCALE)[None]
    qp = _hn_fwd(qraw, gains_b, ["norm"], d, cos, sin, name="b_q_norm")
    o_b, lse_b = _fox_fwd_n(qp, k_aug, kvp, cfg, name="b_fox")
    h5, (n11,) = _fwd(o_b, w["b_w_o"], name="b_out", res=h4, norms=[s["ffn_norm"][1, 1]])
    h6, _, ffn3 = _ffn_fwd(h5, n11, w["ffn_w_in"][1][1], w["ffn_w_out"][1][1], "ffn11")

    loss, dh6 = _loss_fwd_bwd(h6, target, name="loss")

    dh5, dg11, dwi11, dwo11 = _ffn_bwd(dh6, h5, s["ffn_norm"][1, 1], w["ffn_w_in"][1][1], w["ffn_w_out"][1][1],
                                       ffn3, "ffn11")
    do_b = _bwd(dh5, w["b_w_o"], name="b_out_dx")
    dw_bo = _wgrad(o_b, dh5, w["b_w_o"], name="b_out_dw")
    do_bf = do_b.astype(BF16)
    dsum_b = _head_dot_c(do_bf, o_b, cfg, name="b_dsum")
    dq4, dk_b, dv_b, dcum = _fox_bwd_n(qp, k_aug, kvp, do_bf, lse_b, dsum_b, cfg, name="b_fox_bwd")
    dq_b = dq4.transpose(0, 1, 3, 2).reshape(cfg.tokens, d)
    dqraw, dgq = _hn_bwd(qraw, [dq_b], gains_b, ["norm"], d, cos, sin, name="b_q_norm_bwd")
    dh4, dmix1 = _bwd(dqraw, w["b_w_q"], name="b_q_dx", norm_bwd=(h4, s["mix_norm"][1], dh5))
    dw_bq = _wgrad(hn_b, dqraw, w["b_w_q"], name="b_q_dw")
    dh3, dg10, dwi10, dwo10 = _ffn_bwd(dh4, h3, s["ffn_norm"][1, 0], w["ffn_w_in"][1][0], w["ffn_w_out"][1][0],
                                       ffn2, "ffn10")

    dkvraw, dgk = _hn_bwd(proj, [dk_b, dv_b], gains_kv, kinds_kv, d, cos, sin, name="kv_k_norm_bwd")
    dz, dbias = _gate_bwd(proj, gate_col, bias, dcum, cfg, name="kv_gate_bwd")
    pad_cols = w["kv_w"].shape[2] - 2 * d - LANES
    dproj = jnp.concatenate([dkvraw, dz.astype(BF16), jnp.zeros((cfg.tokens, pad_cols), BF16)], axis=1)
    dw_kv = _wgrad(kn, dproj, w["kv_w"], name="kv_proj_dw")
    dh3, dkvn = _bwd(dproj, w["kv_w"], name="kv_proj_dx", norm_bwd=(h3, s["kv_norm"], dh3))

    dh2, dg01, dwi01, dwo01 = _ffn_bwd(dh3, h2, s["ffn_norm"][0, 1], w["ffn_w_in"][0][1], w["ffn_w_out"][0][1],
                                       ffn1, "ffn01")
    dmixed = _bwd(dh2, w["a_w_o"], name="a_out_dx")
    dw_ao = _wgrad(mixed, dh2, w["a_w_o"], name="a_out_dw")
    dmixed_bf = dmixed.astype(BF16)
    dsum_a = _head_dot_c(dmixed_bf, mixed, cfg, name="a_dsum")
    dqkvp = []
    for g, dil in enumerate(DILATIONS):
        dsum_c = dsum_a.reshape(hh, cfg.tokens // dil, dil).transpose(2, 0, 1)
        grads = _band_bwd_n(lay[g], dmixed_bf.reshape(cfg.tokens // dil, dil * d), _to_classes_t(lse_a, dil, hh),
                            dsum_c, dil, cfg, name=f"a_band{g}_bwd")
        dqkvp += [z.reshape(cfg.tokens, d) for z in grads]
    dqkv, dga = _hn_bwd(qkv, dqkvp, gains_a, kinds_a, d, cos, sin, name="a_qk_norm_bwd")
    dh1, dmix0 = _bwd(dqkv, w["a_w_qkv"], name="a_qkv_dx", norm_bwd=(h1, s["mix_norm"][0], dh2))
    dw_qkv = _wgrad(hn_a, dqkv, w["a_w_qkv"], name="a_qkv_dw")
    dx, dg00, dwi00, dwo00 = _ffn_bwd(dh1, x, s["ffn_norm"][0, 0], w["ffn_w_in"][0][0], w["ffn_w_out"][0][0],
                                      ffn0, "ffn00")

    dw = {
        "ffn_w_in": [[dwi00, dwi01], [dwi10, dwi11]],
        "ffn_w_out": [[dwo00, dwo01], [dwo10, dwo11]],
        "a_w_qkv": dw_qkv, "a_w_o": dw_ao, "kv_w": dw_kv, "b_w_q": dw_bq, "b_w_o": dw_bo,
    }
    ds = {
        "ffn_norm": jnp.stack([jnp.stack([dg00, dg01]), jnp.stack([dg10, dg11])]),
        "mix_norm": jnp.stack([dmix0, dmix1]),
        "a_q_norm": jnp.stack([dga[3 * g] for g in range(len(DILATIONS))])[None] * Q_SCALE,
        "a_k_norm": jnp.stack([dga[3 * g + 1] for g in range(len(DILATIONS))])[None],
        "kv_norm": dkvn,
        "kv_b_f": dbias[:hh],
        "kv_k_norm": dgk[0],
        "b_q_norm": dgq * Q_SCALE,
    }
    return loss, dx, dw, ds


MESH_ID = pl.DeviceIdType.MESH
ANY = pl.BlockSpec(memory_space=pl.ANY)
PACK_COLS = 1024
PACK_ROW_ALIGN = 32


def _me():
    return lax.axis_index("x"), lax.axis_index("y"), lax.axis_index("c")


def _other_chips(x, y):
    return [(1 - x, y), (x, 1 - y), (1 - x, 1 - y)]


def _all_gather_small(v, *, name):
    r = v.shape[0]

    def body(v_ref, out_ref, send_sems, recv_sems):
        x, y, c = _me()
        me = 4 * x + 2 * y + c
        out_ref[me] = v_ref[...]
        copies = []
        for k in range(1, N_DEV):
            fx, fy, fc = (k >> 2) & 1, (k >> 1) & 1, k & 1
            peer = (1 - x if fx else x, 1 - y if fy else y, 1 - c if fc else c)
            copies.append(pltpu.make_async_remote_copy(
                src_ref=v_ref, dst_ref=out_ref.at[me], send_sem=send_sems.at[k - 1], recv_sem=recv_sems.at[k - 1],
                device_id=peer, device_id_type=MESH_ID))
        for cp in copies:
            cp.start()
        for cp in copies:
            cp.wait()

    return pl.pallas_call(
        body, name=name,
        in_specs=[pl.BlockSpec(memory_space=pltpu.VMEM)], out_specs=pl.BlockSpec(memory_space=pltpu.VMEM),
        out_shape=jax.ShapeDtypeStruct((N_DEV, r, LANES), v.dtype),
        scratch_shapes=[pltpu.SemaphoreType.DMA((N_DEV - 1,)), pltpu.SemaphoreType.DMA((N_DEV - 1,))],
    )(v)


def _all_gather_chips(v, *, name):
    rh = v.shape[0] // 2

    def body(v_ref, out_ref, send_sems, recv_sems):
        x, y, c = _me()
        j = 2 * x + y
        chips = _other_chips(x, y)

        def half(chip, core):
            return out_ref.at[chip, pl.ds(core * rh, rh)]

        first = [pltpu.make_async_remote_copy(
            src_ref=v_ref.at[pl.ds(c * rh, rh)], dst_ref=half(j, c), send_sem=send_sems.at[k],
            recv_sem=recv_sems.at[k], device_id=(px, py, c), device_id_type=MESH_ID)
            for k, (px, py) in enumerate(chips)]
        for cp in first:
            cp.start()
        passed = [pltpu.make_async_remote_copy(
            src_ref=half(2 * px + py, c), dst_ref=half(2 * px + py, c), send_sem=send_sems.at[3 + k],
            recv_sem=recv_sems.at[3 + k], device_id=(x, y, 1 - c), device_id_type=MESH_ID)
            for k, (px, py) in enumerate(chips)]
        for k in range(len(chips)):
            first[k].wait_recv()
            passed[k].start()
        for k, (px, py) in enumerate(chips):
            pltpu.make_async_remote_copy(
                src_ref=half(2 * px + py, 1 - c), dst_ref=half(2 * px + py, 1 - c), send_sem=send_sems.at[3 + k],
                recv_sem=recv_sems.at[3 + k], device_id=(x, y, 1 - c), device_id_type=MESH_ID).wait_recv()
        for cp in first + passed:
            cp.wait_send()

    return pl.pallas_call(
        body, name=name, in_specs=[ANY], out_specs=ANY,
        out_shape=jax.ShapeDtypeStruct((N_CHIPS,) + v.shape, v.dtype),
        scratch_shapes=[pltpu.SemaphoreType.DMA((2 * (N_CHIPS - 1),)), pltpu.SemaphoreType.DMA((2 * (N_CHIPS - 1),))],
    )(v)


def _swap_halves(g, *, name):
    n, r, cols = g.shape
    rh = r // 2

    def body(g_ref, out_ref, send_sem, recv_sem):
        x, y, c = _me()
        cp = pltpu.make_async_remote_copy(
            src_ref=g_ref.at[:, pl.ds((1 - c) * rh, rh)], dst_ref=out_ref, send_sem=send_sem, recv_sem=recv_sem,
            device_id=(x, y, 1 - c), device_id_type=MESH_ID)
        cp.start()
        cp.wait()

    return pl.pallas_call(
        body, name=name, in_specs=[ANY], out_specs=ANY,
        out_shape=jax.ShapeDtypeStruct((n, rh, cols), g.dtype),
        scratch_shapes=[pltpu.SemaphoreType.DMA, pltpu.SemaphoreType.DMA],
    )(g)


def _scatter_chips(v, *, name):
    def body(v_ref, out_ref, send_sems, recv_sems):
        x, y, c = _me()
        j = 2 * x + y
        copies = [pltpu.make_async_remote_copy(
            src_ref=v_ref.at[2 * px + py], dst_ref=out_ref.at[j], send_sem=send_sems.at[k], recv_sem=recv_sems.at[k],
            device_id=(px, py, c), device_id_type=MESH_ID) for k, (px, py) in enumerate(_other_chips(x, y))]
        for cp in copies:
            cp.start()
        for cp in copies:
            cp.wait()

    return pl.pallas_call(
        body, name=name, in_specs=[ANY], out_specs=ANY,
        out_shape=jax.ShapeDtypeStruct(v.shape, v.dtype),
        scratch_shapes=[pltpu.SemaphoreType.DMA((N_CHIPS - 1,)), pltpu.SemaphoreType.DMA((N_CHIPS - 1,))],
    )(v)


def _join_halves(v, *, name):
    def body(v_ref, out_ref, send_sem, recv_sem):
        x, y, c = _me()
        cp = pltpu.make_async_remote_copy(
            src_ref=v_ref, dst_ref=out_ref.at[c], send_sem=send_sem, recv_sem=recv_sem,
            device_id=(x, y, 1 - c), device_id_type=MESH_ID)
        cp.start()
        cp.wait()

    return pl.pallas_call(
        body, name=name, in_specs=[ANY], out_specs=ANY,
        out_shape=jax.ShapeDtypeStruct((2,) + v.shape, v.dtype),
        scratch_shapes=[pltpu.SemaphoreType.DMA, pltpu.SemaphoreType.DMA],
    )(v)


def _row_blk(rows, want):
    for b in range(min(rows, want) // SUBLANES * SUBLANES, 0, -SUBLANES):
        if rows % b == 0:
            return b
    return rows


def _add_own_half(g, got, *, name):
    n, r, cols = g.shape
    rh = r // 2
    tr = _row_blk(rh, 512)
    nb = rh // tr

    def body(c_ref, g_ref, got_ref, o_ref):
        del c_ref
        o_ref[...] = (g_ref[...] + got_ref[...]).astype(BF16)

    grid_spec = pltpu.PrefetchScalarGridSpec(
        num_scalar_prefetch=1, grid=(n, nb),
        in_specs=[pl.BlockSpec((None, tr, cols), lambda j, i, c: (j, c[0] * nb + i, 0)),
                  pl.BlockSpec((None, tr, cols), lambda j, i, c: (j, i, 0))],
        out_specs=pl.BlockSpec((None, tr, cols), lambda j, i, c: (j, i, 0)))
    return pl.pallas_call(
        body, name=name, grid_spec=grid_spec, out_shape=jax.ShapeDtypeStruct((n, rh, cols), BF16),
        compiler_params=_params(("parallel", "parallel")),
    )(lax.axis_index("c").astype(jnp.int32).reshape(1), g, got)


def _sum_parts(parts, *, name):
    n, r, cols = parts.shape
    tr = _row_blk(r, 512)

    def body(*refs):
        o_ref = refs[n]
        acc = refs[0][...].astype(F32)
        for p_ref in refs[1:n]:
            acc = acc + p_ref[...].astype(F32)
        o_ref[...] = acc

    return pl.pallas_call(
        body, name=name, grid=(r // tr,),
        in_specs=[pl.BlockSpec((None, tr, cols), functools.partial(lambda j, i: (j, i, 0), j)) for j in range(n)],
        out_specs=pl.BlockSpec((tr, cols), lambda i: (i, 0)),
        out_shape=jax.ShapeDtypeStruct((r, cols), F32),
        compiler_params=_params(("parallel",)),
    )(*([parts] * n))


def _adamw(w, m, v, g, *, name):
    shape = w.shape
    cols = shape[-1]
    w2, m2, v2, g2 = (z.reshape(-1, cols) for z in (w, m, v, g))
    rows = w2.shape[0]
    tr = _row_blk(rows, max(SUBLANES, (1 << 20) // (4 * cols)))

    def body(w_ref, m_ref, v_ref, g_ref, d_ref, nm_ref, nv_ref):
        gv = g_ref[...]
        nm = ADAM_B1 * m_ref[...] + (1.0 - ADAM_B1) * gv
        nv = ADAM_B2 * v_ref[...] + (1.0 - ADAM_B2) * jnp.square(gv)
        m_hat = nm / (1.0 - ADAM_B1 ** ADAM_STEP)
        v_hat = nv / (1.0 - ADAM_B2 ** ADAM_STEP)
        d_ref[...] = -ADAM_LR * (m_hat / (jnp.sqrt(v_hat) + ADAM_EPS) + ADAM_WD * w_ref[...])
        nm_ref[...] = nm
        nv_ref[...] = nv

    spec = pl.BlockSpec((tr, cols), lambda i: (i, 0))
    out = jax.ShapeDtypeStruct((rows, cols), F32)
    d, nm, nv = pl.pallas_call(
        body, name=name, grid=(rows // tr,), in_specs=[spec] * 4, out_specs=[spec] * 3, out_shape=[out] * 3,
        compiler_params=_params(("parallel",)),
    )(w2, m2, v2, g2)
    return d.reshape(shape), nm.reshape(shape), nv.reshape(shape)


def _pack_rows(size, cols, align):
    return -(-size // (cols * align)) * align


def _pack(arrs, lead, cols, align, total_align):
    lead_shape = arrs[0].shape[:lead]
    parts = []
    for a in arrs:
        flat = a.reshape(lead_shape + (-1,))
        size = flat.shape[-1]
        rows = _pack_rows(size, cols, align)
        flat = jnp.pad(flat, [(0, 0)] * lead + [(0, rows * cols - size)])
        parts.append(flat.reshape(lead_shape + (rows, cols)))
    total = sum(p.shape[lead] for p in parts)
    extra = -total % total_align
    if extra:
        parts.append(jnp.zeros(lead_shape + (extra, cols), parts[0].dtype))
    return jnp.concatenate(parts, axis=lead)


def _unpack(buf, shapes, lead, cols, align):
    lead_shape = buf.shape[:lead]
    out, row = [], 0
    for shp in shapes:
        size = 1
        for n in shp:
            size *= n
        rows = _pack_rows(size, cols, align)
        piece = lax.slice_in_dim(buf, row, row + rows, axis=lead).reshape(lead_shape + (-1,))
        out.append(piece[..., :size].reshape(lead_shape + tuple(shp)))
        row += rows
    return out


BIG = ("ffn_w_in", "ffn_w_out", "a_w_qkv", "a_w_o", "kv_w", "b_w_q", "b_w_o")
SMALL = ("ffn_norm", "mix_norm", "a_q_norm", "a_k_norm", "kv_norm", "kv_b_f", "kv_k_norm", "b_q_norm")
WEIGHTS = ("ffn_norm", "ffn_w_in", "ffn_w_out", "mix_norm", "a_w_qkv", "a_q_norm", "a_k_norm", "a_w_o",
           "kv_norm", "kv_w", "kv_b_f", "kv_k_norm", "b_w_q", "b_q_norm", "b_w_o")
GATE_PAD = 2 * LANES


def _stack_weights(sh, d):
    depth = sh["ffn_w_in"].shape[1]
    kv = sh["kv_w"].transpose(1, 0, 2).reshape(d, -1)
    kv = jnp.pad(kv, ((0, 0), (0, 2 * d + GATE_PAD - kv.shape[1])))
    return {
        "ffn_w_in": [[sh["ffn_w_in"][:, l, i] for i in range(2)] for l in range(depth)],
        "ffn_w_out": [[sh["ffn_w_out"][:, l, i].reshape(1, -1, d) for i in range(2)] for l in range(depth)],
        "a_w_qkv": sh["a_w_qkv"][:, 0],
        "a_w_o": sh["a_w_o"].reshape(1, d, d),
        "kv_w": kv[None],
        "b_w_q": sh["b_w_q"].reshape(1, d, d),
        "b_w_o": sh["b_w_o"].reshape(1, d, d),
    }


def _unstack_grads(dw, d, heads):
    def rows4(z):
        return z.reshape(N_CHIPS, -1, d)

    kv_cols = 2 * d + heads
    kv = dw["kv_w"][0][:, :kv_cols].reshape(d, N_CHIPS, kv_cols // N_CHIPS).transpose(1, 0, 2)
    return [
        jnp.stack([jnp.stack(row, axis=1) for row in dw["ffn_w_in"]], axis=1),
        jnp.stack([jnp.stack([rows4(z) for z in row], axis=1) for row in dw["ffn_w_out"]], axis=1),
        dw["a_w_qkv"][:, None],
        rows4(dw["a_w_o"])[:, None],
        kv,
        rows4(dw["b_w_q"])[:, None],
        rows4(dw["b_w_o"])[:, None],
    ]


def kernel(x, positions, ffn_norm, ffn_w_in, ffn_w_out, mix_norm, a_w_qkv, a_q_norm, a_k_norm, a_w_o, kv_norm, kv_w, kv_b_f, kv_k_norm, b_w_q, b_q_norm, b_w_o, loss_target, m_ffn_norm, m_ffn_w_in, m_ffn_w_out, m_mix_norm, m_a_w_qkv, m_a_q_norm, m_a_k_norm, m_a_w_o, m_kv_norm, m_kv_w, m_kv_b_f, m_kv_k_norm, m_b_w_q, m_b_q_norm, m_b_w_o, v_ffn_norm, v_ffn_w_in, v_ffn_w_out, v_mix_norm, v_a_w_qkv, v_a_q_norm, v_a_k_norm, v_a_w_o, v_kv_norm, v_kv_w, v_kv_b_f, v_kv_k_norm, v_b_w_q, v_b_q_norm, v_b_w_o):
    wts = dict(zip(WEIGHTS, (ffn_norm, ffn_w_in, ffn_w_out, mix_norm, a_w_qkv, a_q_norm, a_k_norm, a_w_o, kv_norm,
                             kv_w, kv_b_f, kv_k_norm, b_w_q, b_q_norm, b_w_o)))
    mom = dict(zip(WEIGHTS, (m_ffn_norm, m_ffn_w_in, m_ffn_w_out, m_mix_norm, m_a_w_qkv, m_a_q_norm, m_a_k_norm,
                             m_a_w_o, m_kv_norm, m_kv_w, m_kv_b_f, m_kv_k_norm, m_b_w_q, m_b_q_norm, m_b_w_o)))
    var = dict(zip(WEIGHTS, (v_ffn_norm, v_ffn_w_in, v_ffn_w_out, v_mix_norm, v_a_w_qkv, v_a_q_norm, v_a_k_norm,
                             v_a_w_o, v_kv_norm, v_kv_w, v_kv_b_f, v_kv_k_norm, v_b_w_q, v_b_q_norm, v_b_w_o)))
    batch, seq, d = x.shape
    cfg = Cfg(d_model=d, d_ff=ffn_w_out.shape[2] * N_CHIPS, seq=seq, batch=batch)
    chip = 2 * lax.axis_index("x") + lax.axis_index("y")
    big_shapes = [wts[n].shape for n in BIG]

    shard = _pack([wts[n].astype(BF16) for n in BIG], 0, PACK_COLS, PACK_ROW_ALIGN, PACK_COLS)
    gathered = _all_gather_chips(shard, name="gather_weights")
    gathered = lax.dynamic_update_slice_in_dim(gathered, shard[None], chip, axis=0)
    w = _stack_weights(dict(zip(BIG, _unpack(gathered, big_shapes, 1, PACK_COLS, PACK_ROW_ALIGN))), d)
    norm_shard = _pack([ffn_norm], 0, LANES, SUBLANES, SUBLANES)
    norms = _all_gather_small(norm_shard, name="gather_ffn_norm")[0::2]
    (norms,) = _unpack(norms, [ffn_norm.shape], 1, LANES, SUBLANES)
    small = {"ffn_norm": jnp.moveaxis(norms, 0, 2).reshape(ffn_norm.shape[:2] + (d,)),
             "mix_norm": mix_norm, "a_q_norm": a_q_norm[0], "a_k_norm": a_k_norm[0], "kv_norm": kv_norm,
             "kv_b_f": kv_b_f, "kv_k_norm": kv_k_norm, "b_q_norm": b_q_norm}

    loss, dx, dw, ds = _local_step(cfg, x.reshape(cfg.tokens, d), positions.reshape(cfg.tokens),
                                   loss_target.reshape(cfg.tokens, d), w, small)
    loss = lax.psum(loss, ("x", "y", "c"))

    g = _pack(_unstack_grads(dw, d, cfg.heads), 1, PACK_COLS, PACK_ROW_ALIGN, PACK_COLS)
    chip_half = _add_own_half(g, _swap_halves(g, name="swap_halves"), name="add_halves")
    parts = _scatter_chips(chip_half, name="scatter_chips")
    parts = lax.dynamic_update_slice_in_dim(parts, lax.dynamic_slice_in_dim(chip_half, chip, 1, axis=0), chip, axis=0)
    mine = _sum_parts(parts, name="sum_chips")
    both = _join_halves(mine, name="join_halves")
    g_big = lax.dynamic_update_slice_in_dim(both, mine[None], lax.axis_index("c"), axis=0).reshape(g.shape[1:])
    grads = dict(zip(BIG, _unpack(g_big, big_shapes, 0, PACK_COLS, PACK_ROW_ALIGN)))

    small_shapes = [ds[n].shape for n in SMALL]
    parts = _all_gather_small(_pack([ds[n] for n in SMALL], 0, LANES, SUBLANES, SUBLANES), name="gather_small")
    g_small = dict(zip(SMALL, _unpack(_sum_parts(parts, name="sum_small"), small_shapes, 0, LANES, SUBLANES)))
    quarter = d // N_CHIPS
    g_small["ffn_norm"] = lax.dynamic_slice_in_dim(g_small["ffn_norm"], chip * quarter, quarter, axis=2)
    grads.update(g_small)

    delta, new_m, new_v = {}, {}, {}
    for n in BIG:
        delta[n], new_m[n], new_v[n] = _adamw(wts[n], mom[n], var[n], grads[n], name=f"adamw_{n}")
    packed = [_pack([z[n] for n in SMALL], 0, LANES, SUBLANES, SUBLANES) for z in (wts, mom, var, grads)]
    small_out = _adamw(*packed, name="adamw_small")
    shard_shapes = [wts[n].shape for n in SMALL]
    for out, res in zip((delta, new_m, new_v), small_out):
        out.update(zip(SMALL, _unpack(res, shard_shapes, 0, LANES, SUBLANES)))

    return (loss, dx.reshape(x.shape), *[grads[n] for n in WEIGHTS], *[delta[n] for n in WEIGHTS],
            *[new_m[n] for n in WEIGHTS], *[new_v[n] for n in WEIGHTS])
```

```python
import functools
from typing import NamedTuple

import jax
import jax.numpy as jnp
from jax import lax
from jax.experimental import pallas as pl
from jax.experimental.pallas import tpu as pltpu

F32 = jnp.float32
BF16 = jnp.bfloat16

HEAD_DIM = 64
LANES = 128
SUBLANES = 8
ROT_DIM = HEAD_DIM // 4
ROPE_THETA = 500000.0
NORM_EPS = 1e-6
BAND = 128
DILATIONS = (1, 4, 16)
NEG = -1e30
Q_SCALE = HEAD_DIM ** -0.5
N_CHIPS = 4
N_DEV = 8
VMEM_LIMIT = 48 * 1024 * 1024
WIDE = 1536
ROW_BLOCK = 512
FUSED_ROWS = 512
_NT = (((1,), (1,)), ((), ()))

ADAM_LR = 0.001
ADAM_B1 = 0.9
ADAM_B2 = 0.999
ADAM_EPS = 1e-08
ADAM_WD = 0.01
ADAM_STEP = 10


class Cfg(NamedTuple):
    d_model: int
    d_ff: int
    seq: int
    batch: int

    @property
    def heads(self):
        return self.d_model // HEAD_DIM

    @property
    def tokens(self):
        return self.batch * self.seq


def _params(sem):
    return pltpu.CompilerParams(dimension_semantics=sem, vmem_limit_bytes=VMEM_LIMIT)


def _blk(dim, want):
    if dim <= want:
        return dim
    for b in range(want // LANES * LANES, 0, -LANES):
        if dim % b == 0:
            return b
    b = want
    while dim % b:
        b //= 2
    return b


def _fold8(x):
    return jnp.sum(x.reshape(x.shape[0] // SUBLANES, SUBLANES, x.shape[1]), axis=0)


def _rms_bwd_tile(xv, g, dyv, dres):
    rstd = lax.rsqrt(jnp.mean(xv * xv, axis=-1, keepdims=True) + NORM_EPS)
    xhat = xv * rstd
    dyg = dyv * g
    proj = jnp.mean(dyg * xhat, axis=-1, keepdims=True)
    return dres + rstd * (dyg - xhat * proj), _fold8(dyv * xhat)


def _mm(a, b, *, form, out_dtype, name, bm=1024, bn=1024, bk=1024, res=None, scale=1.0, norms=(), norm_bwd=None):
    if form == "F":
        m, kdim = a.shape
        jn, _, ns = b.shape
        bm, bn, bk = _blk(m, bm), _blk(ns, bn), _blk(kdim, bk)
        npj = ns // bn
        grid = (m // bm, jn * npj, kdim // bk)
        a_spec = pl.BlockSpec((bm, bk), lambda i, n, k: (i, k))
        b_spec = pl.BlockSpec((None, bk, bn), lambda i, n, k: (n // npj, k, n % npj))
        o_spec = pl.BlockSpec((bm, bn), lambda i, n, k: (i, n))
        o_shape = jax.ShapeDtypeStruct((m, jn * ns), out_dtype)
        dims = (((1,), (0,)), ((), ()))
    elif form == "B":
        m = a.shape[0]
        jn, kdim, ns = b.shape
        bm, bn, bk = _blk(m, bm), _blk(kdim, bn), _blk(ns, bk)
        kpj = ns // bk
        grid = (m // bm, kdim // bn, jn * kpj)
        a_spec = pl.BlockSpec((bm, bk), lambda i, n, k: (i, k))
        b_spec = pl.BlockSpec((None, bn, bk), lambda i, n, k: (k // kpj, n, k % kpj))
        o_spec = pl.BlockSpec((bm, bn), lambda i, n, k: (i, n))
        o_shape = jax.ShapeDtypeStruct((m, kdim), out_dtype)
        dims = _NT
    else:
        raise ValueError(form)
    nk = grid[2]
    n_norms = len(norms)
    full_rows = grid[1] == 1
    assert full_rows or (not norms and norm_bwd is None)

    def body(*refs):
        a_ref, b_ref = refs[:2]
        pos = 2
        r_ref = None
        if res is not None:
            r_ref = refs[pos]
            pos += 1
        g_refs = refs[pos:pos + n_norms]
        pos += n_norms
        if norm_bwd is not None:
            x_ref, gb_ref, dres_ref = refs[pos:pos + 3]
            pos += 3
        o_ref = refs[pos]
        n_refs = refs[pos + 1:pos + 1 + n_norms]
        acc_ref = refs[-1]
        i, k = pl.program_id(0), pl.program_id(2)

        @pl.when(k == 0)
        def _():
            acc_ref[...] = jnp.zeros_like(acc_ref)

        acc_ref[...] += lax.dot_general(a_ref[...].astype(BF16), b_ref[...].astype(BF16), dims,
                                        preferred_element_type=F32)

        if norm_bwd is not None:
            dg_ref = refs[pos + 1 + n_norms]

            @pl.when((i == 0) & (k == 0))
            def _():
                dg_ref[...] = jnp.zeros_like(dg_ref)

        @pl.when(k == nk - 1)
        def _():
            r = acc_ref[...]
            if scale != 1.0:
                r = r * scale
            if r_ref is not None:
                r = r_ref[...] + r
            if norm_bwd is not None:
                dx, dg8 = _rms_bwd_tile(x_ref[...], gb_ref[...], r, dres_ref[...])
                o_ref[...] = dx
                dg_ref[...] += dg8
            else:
                o_ref[...] = r.astype(o_ref.dtype)
            if n_norms:
                rstd = lax.rsqrt(jnp.mean(r * r, axis=-1, keepdims=True) + NORM_EPS)
                for g_ref, n_ref in zip(g_refs, n_refs):
                    n_ref[...] = ((r * rstd) * g_ref[...]).astype(BF16)

    row = pl.BlockSpec((bm, bn), lambda i, n, k: (i, n))
    vec = pl.BlockSpec((1, bn), lambda i, n, k: (0, 0))
    in_specs = [a_spec, b_spec]
    args = [a, b]
    if res is not None:
        in_specs.append(row)
        args.append(res)
    for g in norms:
        in_specs.append(vec)
        args.append(g.reshape(1, -1))
    out_specs, out_shapes = [o_spec], [o_shape]
    for _ in norms:
        out_specs.append(row)
        out_shapes.append(jax.ShapeDtypeStruct(o_shape.shape, BF16))
    if norm_bwd is not None:
        x, g, dres = norm_bwd
        in_specs += [row, vec, row]
        args += [x, g.reshape(1, -1), dres]
        out_specs.append(pl.BlockSpec((SUBLANES, bn), lambda i, n, k: (0, 0)))
        out_shapes.append(jax.ShapeDtypeStruct((SUBLANES, o_shape.shape[1]), F32))
    sem = ("arbitrary",) * 3 if norm_bwd is not None else ("parallel", "parallel", "arbitrary")
    single = len(out_specs) == 1
    out = pl.pallas_call(
        body, name=name, grid=grid, in_specs=in_specs, out_specs=out_specs[0] if single else out_specs,
        out_shape=out_shapes[0] if single else out_shapes,
        scratch_shapes=[pltpu.VMEM((bm, bn), F32)],
        compiler_params=_params(sem),
    )(*args)
    if norm_bwd is not None:
        return out[0], jnp.sum(out[1], axis=0)
    return out


def _mm_grad(a, dy, jn, *, name, scale=1.0, bm=1024, bn=1024, bk=1024):
    halves = dy if isinstance(dy, (tuple, list)) else (dy,)
    t, kdim = a.shape
    ns = len(halves) * halves[0].shape[1] // jn
    bm, bn, bk = _blk(kdim, bm), _blk(ns, bn), _blk(t, bk)
    npj = ns // bn
    grid = (kdim // bm, jn * npj, t // bk)
    nk = grid[2]
    nhalf = jn * npj // len(halves)
    dims = (((0,), (0,)), ((), ()))

    def body(a_ref, *refs):
        b_refs, o_ref, acc_ref = refs[:len(halves)], refs[-2], refs[-1]
        n, k = pl.program_id(1), pl.program_id(2)

        @pl.when(k == 0)
        def _():
            acc_ref[...] = jnp.zeros_like(acc_ref)

        for which, b_ref in enumerate(b_refs):
            @pl.when(n // nhalf == which)
            def _(b_ref=b_ref):
                acc_ref[...] += lax.dot_general(a_ref[...].astype(BF16), b_ref[...].astype(BF16), dims,
                                                preferred_element_type=F32)

        @pl.when(k == nk - 1)
        def _():
            r = acc_ref[...]
            if scale != 1.0:
                r = r * scale
            o_ref[...] = r

    def half_spec(which):
        return pl.BlockSpec((bk, bn), lambda m, n, k: (jnp.where(n // nhalf == which, k, 0),
                                                        jnp.where(n // nhalf == which, n % nhalf, 0)))

    return pl.pallas_call(
        body, name=name, grid=grid,
        in_specs=[pl.BlockSpec((bk, bm), lambda m, n, k: (k, m))] + [half_spec(w) for w in range(len(halves))],
        out_specs=pl.BlockSpec((None, bm, bn), lambda m, n, k: (n // npj, m, n % npj)),
        out_shape=jax.ShapeDtypeStruct((jn, kdim, ns), F32),
        scratch_shapes=[pltpu.VMEM((bm, bn), F32)],
        compiler_params=_params(("parallel", "parallel", "arbitrary")),
    )(a, *halves)


def _mm_back2(dy_halves, w, norm_bwd, *, name, bm=FUSED_ROWS, bk=1024):
    x, g, dres = norm_bwd
    m = dy_halves[0].shape[0]
    jn, kdim, ns = w.shape
    bm, bk = _blk(m, bm), _blk(ns, bk)
    kpj = ns // bk
    nk = jn * kpj
    khalf = nk // 2

    def body(a0_ref, a1_ref, b_ref, x_ref, g_ref, dres_ref, o_ref, dg_ref, acc_ref):
        i, k = pl.program_id(0), pl.program_id(1)

        @pl.when(k == 0)
        def _():
            acc_ref[...] = jnp.zeros_like(acc_ref)

        @pl.when((i == 0) & (k == 0))
        def _():
            dg_ref[...] = jnp.zeros_like(dg_ref)

        for which, a_ref in enumerate((a0_ref, a1_ref)):
            @pl.when(k // khalf == which)
            def _(a_ref=a_ref):
                acc_ref[...] += lax.dot_general(a_ref[...], b_ref[...], _NT, preferred_element_type=F32)

        @pl.when(k == nk - 1)
        def _():
            dx, dg8 = _rms_bwd_tile(x_ref[...], g_ref[...], acc_ref[...], dres_ref[...])
            o_ref[...] = dx
            dg_ref[...] += dg8

    def half_spec(which):
        return pl.BlockSpec((bm, bk), lambda i, k: (i, jnp.clip(k - which * khalf, 0, khalf - 1)))

    row = pl.BlockSpec((bm, kdim), lambda i, k: (i, 0))
    dx, dg = pl.pallas_call(
        body, name=name, grid=(m // bm, nk),
        in_specs=[half_spec(0), half_spec(1),
                  pl.BlockSpec((None, kdim, bk), lambda i, k: (k // kpj, 0, k % kpj)),
                  row, pl.BlockSpec((1, kdim), lambda i, k: (0, 0)), row],
        out_specs=[row, pl.BlockSpec((SUBLANES, kdim), lambda i, k: (0, 0))],
        out_shape=[jax.ShapeDtypeStruct((m, kdim), F32), jax.ShapeDtypeStruct((SUBLANES, kdim), F32)],
        scratch_shapes=[pltpu.VMEM((bm, kdim), F32)],
        compiler_params=_params(("arbitrary", "arbitrary")),
    )(dy_halves[0], dy_halves[1], w, x, g.reshape(1, -1), dres)
    return dx, jnp.sum(dg, axis=0)


def _ffn_in_act(n, w_in, *, name, bm=512):
    m, kdim = n.shape
    jn, _, ns = w_in.shape
    f = jn * ns // 2
    bm = _blk(m, bm)
    bn = _blk(ns, WIDE)
    npj = ns // bn
    nf = f // bn

    def body(n_ref, wg_ref, wu_ref, g_ref, u_ref, a_ref):
        nv = n_ref[...]
        g = jnp.dot(nv, wg_ref[...], preferred_element_type=F32)
        u = jnp.dot(nv, wu_ref[...], preferred_element_type=F32)
        g_ref[...] = g.astype(BF16)
        u_ref[...] = u.astype(BF16)
        a_ref[...] = (g * jax.nn.sigmoid(g) * u).astype(BF16)

    out = jax.ShapeDtypeStruct((m, f), BF16)
    ospec = pl.BlockSpec((bm, bn), lambda c, i: (i, c))
    return pl.pallas_call(
        body, name=name, grid=(nf, m // bm),
        in_specs=[pl.BlockSpec((bm, kdim), lambda c, i: (i, 0)),
                  pl.BlockSpec((None, kdim, bn), lambda c, i: (c // npj, 0, c % npj)),
                  pl.BlockSpec((None, kdim, bn), lambda c, i: ((c + nf) // npj, 0, (c + nf) % npj))],
        out_specs=[ospec, ospec, ospec], out_shape=[out, out, out],
        compiler_params=_params(("parallel", "parallel")),
    )(n, w_in, w_in)


def _ffn_out_dx_act(dh, w_out, gate, up, *, name, scale, bm=512):
    m, d = dh.shape
    f = w_out.shape[1]
    bm = _blk(m, bm)
    bn = _blk(f, WIDE)

    def body(dh_ref, w_ref, g_ref, u_ref, dg_ref, du_ref):
        da = lax.dot_general(dh_ref[...].astype(BF16), w_ref[...], _NT, preferred_element_type=F32) * scale
        g = g_ref[...].astype(F32)
        sg = jax.nn.sigmoid(g)
        silu = g * sg
        dg_ref[...] = (da * u_ref[...].astype(F32) * (sg + silu * (1.0 - sg))).astype(BF16)
        du_ref[...] = (da * silu).astype(BF16)

    out = jax.ShapeDtypeStruct((m, f), BF16)
    spec = pl.BlockSpec((bm, bn), lambda i, c: (i, c))
    return pl.pallas_call(
        body, name=name, grid=(m // bm, f // bn),
        in_specs=[pl.BlockSpec((bm, d), lambda i, c: (i, 0)), pl.BlockSpec((None, bn, d), lambda i, c: (0, c, 0)),
                  spec, spec],
        out_specs=[spec, spec], out_shape=[out, out],
        compiler_params=_params(("parallel", "parallel")),
    )(dh, w_out, gate, up)


def _rms_fwd(x, g, *, name):
    t, d = x.shape
    tr = _blk(t, ROW_BLOCK)

    def body(x_ref, g_ref, o_ref):
        xv = x_ref[...]
        rstd = lax.rsqrt(jnp.mean(xv * xv, axis=-1, keepdims=True) + NORM_EPS)
        o_ref[...] = ((xv * rstd) * g_ref[...]).astype(BF16)

    return pl.pallas_call(
        body, name=name, grid=(t // tr,),
        in_specs=[pl.BlockSpec((tr, d), lambda i: (i, 0)), pl.BlockSpec((1, d), lambda i: (0, 0))],
        out_specs=pl.BlockSpec((tr, d), lambda i: (i, 0)),
        out_shape=jax.ShapeDtypeStruct((t, d), BF16),
        compiler_params=_params(("parallel",)),
    )(x, g.reshape(1, d))


def _loss_fwd_bwd(h, target, *, name):
    t, d = h.shape
    tr = _blk(t, ROW_BLOCK)

    def body(h_ref, t_ref, dh_ref, l_ref):
        i = pl.program_id(0)
        err = h_ref[...] - t_ref[...]
        dh_ref[...] = err * (1.0 / d)

        @pl.when(i == 0)
        def _():
            l_ref[...] = jnp.zeros_like(l_ref)

        l_ref[...] += _fold8(err * err)

    dh, part = pl.pallas_call(
        body, name=name, grid=(t // tr,),
        in_specs=[pl.BlockSpec((tr, d), lambda i: (i, 0)), pl.BlockSpec((tr, d), lambda i: (i, 0))],
        out_specs=[pl.BlockSpec((tr, d), lambda i: (i, 0)), pl.BlockSpec((SUBLANES, d), lambda i: (0, 0))],
        out_shape=[jax.ShapeDtypeStruct((t, d), F32), jax.ShapeDtypeStruct((SUBLANES, d), F32)],
        compiler_params=_params(("arbitrary",)),
    )(h, target)
    return jnp.sum(part) * (0.5 / d), dh


def _seg_matrix():
    r = lax.broadcasted_iota(jnp.int32, (LANES, LANES), 0) // HEAD_DIM
    c = lax.broadcasted_iota(jnp.int32, (LANES, LANES), 1) // HEAD_DIM
    return (r == c).astype(BF16)


def _head_sum(x, seg, terms=3):
    hi = x.astype(BF16)
    r1 = x - hi.astype(F32)
    mid = r1.astype(BF16)
    dot = functools.partial(jnp.dot, preferred_element_type=F32)
    if terms == 2:
        return dot(hi, seg) + dot(mid, seg)
    lo = (r1 - mid.astype(F32)).astype(BF16)
    return dot(hi, seg) + dot(mid, seg) + dot(lo, seg)


def _lane_in_head(shape):
    return lax.broadcasted_iota(jnp.int32, shape, 1) % HEAD_DIM


def _half_mask(shape):
    return lax.broadcasted_iota(jnp.int32, shape, 1) < HEAD_DIM


def _rot_partner(x):
    up = pltpu.roll(x, LANES - ROT_DIM // 2, 1)
    down = pltpu.roll(x, ROT_DIM // 2, 1)
    return jnp.where(_lane_in_head(x.shape) < ROT_DIM // 2, up, down)


def _rope_tables(positions):
    inv_freq = ROPE_THETA ** (-jnp.arange(0, ROT_DIM, 2, dtype=F32) / ROT_DIM)
    ang = positions.astype(F32)[:, None] * inv_freq
    t = ang.shape[0]
    rest = HEAD_DIM - ROT_DIM
    cos = jnp.concatenate([jnp.cos(ang), jnp.cos(ang), jnp.ones((t, rest), F32)], axis=1)
    sin = jnp.concatenate([-jnp.sin(ang), jnp.sin(ang), jnp.zeros((t, rest), F32)], axis=1)
    return jnp.tile(cos, (1, LANES // HEAD_DIM)), jnp.tile(sin, (1, LANES // HEAD_DIM))


def _kind_is(j, kinds, kind):
    hits = [j == jj for jj, k in enumerate(kinds) if k == kind]
    return functools.reduce(jnp.logical_or, hits) if hits else None


def _hn_fwd(x, gains, kinds, d, cos, sin, *, name, col0=0):
    t = x.shape[0]
    n = len(kinds)
    tr = _blk(t, ROW_BLOCK)
    seg = _seg_matrix()
    g8 = jnp.repeat(gains.astype(F32), SUBLANES, axis=0)

    def body(x_ref, g_ref, seg_ref, cos_ref, sin_ref, o_ref):
        j = pl.program_id(1)

        def normed(rope):
            for c in range(d // LANES):
                sl = slice(c * LANES, (c + 1) * LANES)
                xv = x_ref[:, sl]
                ms = _head_sum(xv * xv, seg_ref[...], terms=2) * (1.0 / HEAD_DIM)
                y = (xv * lax.rsqrt(ms + NORM_EPS)) * g_ref[0:1, sl]
                if rope:
                    y = y * cos_ref[...] + _rot_partner(y) * sin_ref[...]
                o_ref[:, sl] = y.astype(BF16)

        for kind in ("rope", "norm"):
            hit = _kind_is(j, kinds, kind)
            if hit is not None:
                pl.when(hit)(functools.partial(normed, kind == "rope"))
        hit = _kind_is(j, kinds, "cast")
        if hit is not None:
            @pl.when(hit)
            def _():
                o_ref[...] = x_ref[...].astype(BF16)

    return pl.pallas_call(
        body, name=name, grid=(t // tr, n),
        in_specs=[pl.BlockSpec((tr, d), lambda i, j: (i, col0 + j)), pl.BlockSpec((SUBLANES, d), lambda i, j: (j, 0)),
                  pl.BlockSpec((LANES, LANES), lambda i, j: (0, 0)),
                  pl.BlockSpec((tr, LANES), lambda i, j: (i, 0)), pl.BlockSpec((tr, LANES), lambda i, j: (i, 0))],
        out_specs=pl.BlockSpec((tr, d), lambda i, j: (i, j)),
        out_shape=jax.ShapeDtypeStruct((t, n * d), BF16),
        compiler_params=_params(("parallel", "parallel")),
    )(x, g8, seg, cos, sin)


def _hn_bwd(x, dys, gains, kinds, d, cos, sin, *, name, col0=0):
    t = x.shape[0]
    n = len(kinds)
    tr = _blk(t, ROW_BLOCK // 2)
    seg = _seg_matrix()
    g8 = jnp.repeat(gains.astype(F32), SUBLANES, axis=0)

    def body(x_ref, *refs):
        dy_refs = refs[:n]
        g_ref, seg_ref, cos_ref, sin_ref, dx_ref, dg_ref = refs[n:]
        j = pl.program_id(0)
        i = pl.program_id(1)

        @pl.when(i == 0)
        def _():
            dg_ref[...] = jnp.zeros_like(dg_ref)

        def normed(rope, dy_ref):
            for c in range(d // LANES):
                sl = slice(c * LANES, (c + 1) * LANES)
                xv = x_ref[:, sl]
                dyv = dy_ref[:, sl]
                if rope:
                    dyv = dyv * cos_ref[...] - _rot_partner(dyv) * sin_ref[...]
                ms = _head_sum(xv * xv, seg_ref[...], terms=2) * (1.0 / HEAD_DIM)
                rstd = lax.rsqrt(ms + NORM_EPS)
                xhat = xv * rstd
                dg_ref[:, sl] += _fold8(dyv * xhat)
                dyg = dyv * g_ref[0:1, sl]
                proj = _head_sum(dyg * xhat, seg_ref[...], terms=2) * (1.0 / HEAD_DIM)
                dx_ref[:, sl] = (rstd * (dyg - xhat * proj)).astype(BF16)

        def cast(dy_ref):
            dx_ref[...] = dy_ref[...].astype(BF16)

        for jj, kind in enumerate(kinds):
            if kind == "cast":
                pl.when(j == jj)(functools.partial(cast, dy_refs[jj]))
            else:
                pl.when(j == jj)(functools.partial(normed, kind == "rope", dy_refs[jj]))

    def dy_spec(jj):
        return pl.BlockSpec((tr, d), lambda j, i: (jnp.where(j == jj, i, 0), 0))

    dx, dg = pl.pallas_call(
        body, name=name, grid=(n, t // tr),
        in_specs=[pl.BlockSpec((tr, d), lambda j, i: (i, col0 + j))] + [dy_spec(jj) for jj in range(n)] + [
                  pl.BlockSpec((SUBLANES, d), lambda j, i: (j, 0)),
                  pl.BlockSpec((LANES, LANES), lambda j, i: (0, 0)),
                  pl.BlockSpec((tr, LANES), lambda j, i: (i, 0)), pl.BlockSpec((tr, LANES), lambda j, i: (i, 0))],
        out_specs=[pl.BlockSpec((tr, d), lambda j, i: (i, j)), pl.BlockSpec((SUBLANES, d), lambda j, i: (j, 0))],
        out_shape=[jax.ShapeDtypeStruct((t, n * d), BF16), jax.ShapeDtypeStruct((n * SUBLANES, d), F32)],
        compiler_params=_params(("arbitrary", "arbitrary")),
    )(x, *dys, g8, seg, cos, sin)
    dg = dg.reshape(n, SUBLANES, d // HEAD_DIM, HEAD_DIM).sum(axis=(1, 2))
    return dx, dg


def _band_valid_t(first):
    s = lax.broadcasted_iota(jnp.int32, (2 * BAND, BAND), 0)
    t = lax.broadcasted_iota(jnp.int32, (2 * BAND, BAND), 1)
    dist = t + BAND - s
    return (dist >= 0) & (dist <= BAND) & ((s >= BAND) | jnp.logical_not(first))


def _to_classes_t(z, dil, width):
    return z.reshape(z.shape[0] // dil, dil, width).transpose(1, 2, 0)


def _from_classes_t(z):
    dil, width, rows = z.shape
    return z.transpose(2, 0, 1).reshape(rows * dil, width)


def _band_fwd_n(nat, dil, cfg, *, name):
    d, hh = cfg.d_model, cfg.heads
    rows = cfg.tokens // dil
    nbt = rows // BAND
    nb = cfg.seq // (dil * BAND)

    def body(q_ref, kp_ref, kc_ref, vp_ref, vc_ref, o_ref, lse_ref):
        i = pl.program_id(1)
        valid = _band_valid_t(i % nb == 0)
        upper = lax.broadcasted_iota(jnp.int32, (LANES, BAND), 0) < HEAD_DIM
        for hp in range(d // LANES):
            pair = slice(hp * LANES, (hp + 1) * LANES)
            qt2 = q_ref[:, pair].T
            kk = jnp.concatenate([kp_ref[:, pair], kc_ref[:, pair]], axis=0)
            vvt = jnp.concatenate([vp_ref[:, pair], vc_ref[:, pair]], axis=0).T
            outs = []
            for e in range(2):
                h = 2 * hp + e
                qte = jnp.where(upper == (e == 0), qt2, jnp.zeros_like(qt2))
                s = jnp.where(valid, jnp.dot(kk, qte, preferred_element_type=F32), NEG)
                m = jnp.max(s, axis=0, keepdims=True)
                p = jnp.exp(s - m)
                l = jnp.sum(p, axis=0, keepdims=True)
                hi = p.astype(BF16)
                lo = (p - hi.astype(F32)).astype(BF16)
                vt = vvt[e * HEAD_DIM:(e + 1) * HEAD_DIM]
                o = jnp.dot(vt, hi, preferred_element_type=F32) + jnp.dot(vt, lo, preferred_element_type=F32)
                outs.append(o * (1.0 / l))
                lse_ref[h:h + 1, :] = m + jnp.log(l)
            o_ref[:, pair] = jnp.concatenate(outs, axis=0).T

    def prev(i):
        return jnp.maximum(i - 1, 0)

    blk = (BAND, d)
    return pl.pallas_call(
        body, name=name, grid=(dil, nbt),
        in_specs=[pl.BlockSpec(blk, lambda r, i: (i, r * 3)),
                  pl.BlockSpec(blk, lambda r, i: (prev(i), r * 3 + 1)),
                  pl.BlockSpec(blk, lambda r, i: (i, r * 3 + 1)),
                  pl.BlockSpec(blk, lambda r, i: (prev(i), r * 3 + 2)),
                  pl.BlockSpec(blk, lambda r, i: (i, r * 3 + 2))],
        out_specs=[pl.BlockSpec(blk, lambda r, i: (i, r)),
                   pl.BlockSpec((None, hh, BAND), lambda r, i: (r, 0, i))],
        out_shape=[jax.ShapeDtypeStruct((rows, dil * d), F32), jax.ShapeDtypeStruct((dil, hh, rows), F32)],
        compiler_params=_params(("parallel", "arbitrary")),
    )(nat, nat, nat, nat, nat)


def _band_bwd_n(nat, do_nat, lse_c, dsum_c, dil, cfg, *, name):
    d, hh = cfg.d_model, cfg.heads
    rows = cfg.tokens // dil
    nbt = rows // BAND
    nb = cfg.seq // (dil * BAND)

    def body(q_ref, kp_ref, kc_ref, vp_ref, vc_ref, do_ref, l_ref, ds_ref, dq_ref, dk_ref, dv_ref, ck_ref, cv_ref):
        i = pl.program_id(1)

        @pl.when(i < nbt)
        def _():
            @pl.when(i == 0)
            def _():
                ck_ref[...] = jnp.zeros_like(ck_ref)
                cv_ref[...] = jnp.zeros_like(cv_ref)

            valid1 = _band_valid_t(i % nb == 0)
            valid = jnp.concatenate([valid1, valid1], axis=1)
            upper = lax.broadcasted_iota(jnp.int32, (LANES, BAND), 0) < HEAD_DIM
            half2 = _half_mask((2 * BAND, LANES))

            def both(z):
                zero = jnp.zeros_like(z)
                return jnp.concatenate([jnp.where(upper, z, zero), jnp.where(upper, zero, z)], axis=1)

            def stack(z):
                return jnp.concatenate([z[:, :BAND], z[:, BAND:]], axis=0)

            for hp in range(d // LANES):
                pair = slice(hp * LANES, (hp + 1) * LANES)
                h0, h1 = 2 * hp, 2 * hp + 1
                qn2, don2 = q_ref[:, pair], do_ref[:, pair]
                kk = jnp.concatenate([kp_ref[:, pair], kc_ref[:, pair]], axis=0)
                vv = jnp.concatenate([vp_ref[:, pair], vc_ref[:, pair]], axis=0)
                lse2 = jnp.concatenate([l_ref[h0:h0 + 1, :], l_ref[h1:h1 + 1, :]], axis=1)
                dsum2 = jnp.concatenate([ds_ref[h0:h0 + 1, :], ds_ref[h1:h1 + 1, :]], axis=1)
                s = jnp.where(valid, jnp.dot(kk, both(qn2.T), preferred_element_type=F32), NEG)
                p = jnp.exp(s - lse2)
                dp = jnp.dot(vv, both(don2.T), preferred_element_type=F32)
                dsb = (p * (dp - dsum2)).astype(BF16)
                dq2 = jnp.dot(kk.T, dsb, preferred_element_type=F32)
                dq_ref[:, pair] = jnp.concatenate([dq2[:HEAD_DIM, :BAND], dq2[HEAD_DIM:, BAND:]], axis=0).T
                dk2 = jnp.dot(stack(dsb), qn2, preferred_element_type=F32)
                dv2 = jnp.dot(stack(p.astype(BF16)), don2, preferred_element_type=F32)
                dkk = jnp.where(half2, dk2[:2 * BAND], dk2[2 * BAND:])
                dvv = jnp.where(half2, dv2[:2 * BAND], dv2[2 * BAND:])
                dk_ref[:, pair] = ck_ref[:, pair] + dkk[:BAND]
                dv_ref[:, pair] = cv_ref[:, pair] + dvv[:BAND]
                ck_ref[:, pair] = dkk[BAND:]
                cv_ref[:, pair] = dvv[BAND:]

        @pl.when(i == nbt)
        def _():
            dk_ref[...] = ck_ref[...]
            dv_ref[...] = cv_ref[...]

    def cur(i):
        return jnp.minimum(i, nbt - 1)

    def prev(i):
        return jnp.maximum(cur(i) - 1, 0)

    cblk = (None, hh, BAND)
    blk = (BAND, d)
    here = pl.BlockSpec(blk, lambda r, i: (cur(i), r))
    behind = pl.BlockSpec(blk, lambda r, i: (jnp.maximum(i - 1, 0), r))
    shape = jax.ShapeDtypeStruct((rows, dil * d), F32)
    return pl.pallas_call(
        body, name=name, grid=(dil, nbt + 1),
        in_specs=[pl.BlockSpec(blk, lambda r, i: (cur(i), r * 3)),
                  pl.BlockSpec(blk, lambda r, i: (prev(i), r * 3 + 1)),
                  pl.BlockSpec(blk, lambda r, i: (cur(i), r * 3 + 1)),
                  pl.BlockSpec(blk, lambda r, i: (prev(i), r * 3 + 2)),
                  pl.BlockSpec(blk, lambda r, i: (cur(i), r * 3 + 2)),
                  here,
                  pl.BlockSpec(cblk, lambda r, i: (r, 0, cur(i))),
                  pl.BlockSpec(cblk, lambda r, i: (r, 0, cur(i)))],
        out_specs=[here, behind, behind],
        out_shape=[shape, shape, shape],
        scratch_shapes=[pltpu.VMEM(blk, F32), pltpu.VMEM(blk, F32)],
        compiler_params=_params(("arbitrary", "arbitrary")),
    )(nat, nat, nat, nat, nat, do_nat, lse_c, dsum_c)


def _mix_fwd(outs, lses, *, name):
    t, d = outs[0].shape
    hh = lses[0].shape[1]
    tr = _blk(t, ROW_BLOCK)
    ng = len(outs)
    spread = (lax.broadcasted_iota(jnp.int32, (hh, d), 0)
              == lax.broadcasted_iota(jnp.int32, (hh, d), 1) // HEAD_DIM).astype(BF16)

    def body(*refs):
        o_refs, l_refs = refs[:ng], refs[ng:2 * ng]
        spread_ref, mixed_ref, lse_ref = refs[2 * ng:]
        ls = [r[...] for r in l_refs]
        m = functools.reduce(jnp.maximum, ls)
        es = [jnp.exp(l - m) for l in ls]
        tot = functools.reduce(jnp.add, es)
        inv = 1.0 / tot
        mixed_ref[...] = functools.reduce(
            jnp.add, [_head_sum(e * inv, spread_ref[...]) * r[...] for e, r in zip(es, o_refs)])
        lse_ref[...] = m + jnp.log(tot)

    spec = pl.BlockSpec((tr, d), lambda i: (i, 0))
    cspec = pl.BlockSpec((tr, hh), lambda i: (i, 0))
    return pl.pallas_call(
        body, name=name, grid=(t // tr,),
        in_specs=[spec] * ng + [cspec] * ng + [pl.BlockSpec((hh, d), lambda i: (0, 0))], out_specs=[spec, cspec],
        out_shape=[jax.ShapeDtypeStruct((t, d), F32), jax.ShapeDtypeStruct((t, hh), F32)],
        compiler_params=_params(("parallel",)),
    )(*outs, *lses, spread)


GATE_BLOCK = 256


def _tri(n, upper):
    r = lax.broadcasted_iota(jnp.int32, (n, n), 0)
    c = lax.broadcasted_iota(jnp.int32, (n, n), 1)
    return ((c >= r) if upper else (c <= r)).astype(BF16)


def _tri_dot(tri, x):
    hi = x.astype(BF16)
    r1 = x - hi.astype(F32)
    mid = r1.astype(BF16)
    lo = (r1 - mid.astype(F32)).astype(BF16)
    dot = functools.partial(jnp.dot, preferred_element_type=F32)
    return dot(tri, hi) + dot(tri, mid) + dot(tri, lo)


def _log_sigmoid(z):
    return jnp.minimum(z, 0.0) - jnp.log(1.0 + jnp.exp(-jnp.abs(z)))


def _gate_fwd(proj, col_block, bias, cfg, *, name):
    tr = _blk(cfg.seq, GATE_BLOCK)
    nblk = cfg.seq // tr

    def body(z_ref, b_ref, tri_ref, o_ref, carry_ref):
        i = pl.program_id(1)

        @pl.when(i == 0)
        def _():
            carry_ref[...] = jnp.zeros_like(carry_ref)

        logf = _log_sigmoid(z_ref[...] + b_ref[0:1, :])
        cum = _tri_dot(tri_ref[...], logf) + carry_ref[0:1, :]
        o_ref[...] = cum
        carry_ref[...] = jnp.broadcast_to(cum[tr - 1:tr, :], carry_ref.shape)

    return pl.pallas_call(
        body, name=name, grid=(cfg.batch, nblk),
        in_specs=[pl.BlockSpec((tr, LANES), lambda b, i: (b * nblk + i, col_block)),
                  pl.BlockSpec((SUBLANES, LANES), lambda b, i: (0, 0)),
                  pl.BlockSpec((tr, tr), lambda b, i: (0, 0))],
        out_specs=pl.BlockSpec((tr, LANES), lambda b, i: (b * nblk + i, 0)),
        out_shape=jax.ShapeDtypeStruct((cfg.tokens, LANES), F32),
        scratch_shapes=[pltpu.VMEM((SUBLANES, LANES), F32)],
        compiler_params=_params(("arbitrary", "arbitrary")),
    )(proj, jnp.broadcast_to(bias, (SUBLANES, LANES)), _tri(tr, upper=False))


def _gate_bwd(proj, col_block, bias, dcum, cfg, *, name):
    tr = _blk(cfg.seq, GATE_BLOCK)
    nblk = cfg.seq // tr

    def body(z_ref, b_ref, tri_ref, dc_ref, dz_ref, db_ref, carry_ref):
        b = pl.program_id(0)
        i = pl.program_id(1)

        @pl.when(i == 0)
        def _():
            carry_ref[...] = jnp.zeros_like(carry_ref)

        @pl.when((i == 0) & (b == 0))
        def _():
            db_ref[...] = jnp.zeros_like(db_ref)

        dcv = dc_ref[...]
        dlogf = _tri_dot(tri_ref[...], dcv) + carry_ref[0:1, :]
        carry_ref[...] = jnp.broadcast_to(dlogf[0:1, :], carry_ref.shape)
        dz = dlogf * jax.nn.sigmoid(-(z_ref[...] + b_ref[0:1, :]))
        dz_ref[...] = dz
        db_ref[...] += _fold8(dz)

    def rev(b, i):
        return (b * nblk + nblk - 1 - i, 0)

    dz, db = pl.pallas_call(
        body, name=name, grid=(cfg.batch, nblk),
        in_specs=[pl.BlockSpec((tr, LANES), lambda b, i: (b * nblk + nblk - 1 - i, col_block)),
                  pl.BlockSpec((SUBLANES, LANES), lambda b, i: (0, 0)),
                  pl.BlockSpec((tr, tr), lambda b, i: (0, 0)),
                  pl.BlockSpec((tr, LANES), rev)],
        out_specs=[pl.BlockSpec((tr, LANES), rev), pl.BlockSpec((SUBLANES, LANES), lambda b, i: (0, 0))],
        out_shape=[jax.ShapeDtypeStruct((cfg.tokens, LANES), F32), jax.ShapeDtypeStruct((SUBLANES, LANES), F32)],
        scratch_shapes=[pltpu.VMEM((SUBLANES, LANES), F32)],
        compiler_params=_params(("arbitrary", "arbitrary")),
    )(proj, jnp.broadcast_to(bias, (SUBLANES, LANES)), _tri(tr, upper=True), dcum)
    return dz, jnp.sum(db, axis=0)


FOX_BLOCK = 256
AUG = LANES
BIAS_TERMS = 3


def _fox_aug_k_call(kv, cum, cfg, *, name):
    t, d, hh = cfg.tokens, cfg.d_model, cfg.heads
    tr = _blk(t, ROW_BLOCK)

    def body(k_ref, c_ref, o_ref):
        lane = lax.broadcasted_iota(jnp.int32, (tr, LANES), 1)
        for hp in range(hh // 2):
            k2 = k_ref[:, hp * LANES:(hp + 1) * LANES].astype(F32)
            for e in range(2):
                h = 2 * hp + e
                kh = k2 if e == 0 else pltpu.roll(k2, HEAD_DIM, 1)
                c = -c_ref[:, h:h + 1]
                hi = c.astype(BF16).astype(F32)
                mid = (c - hi).astype(BF16).astype(F32)
                lo = c - hi - mid
                bias = jnp.where(lane == HEAD_DIM, hi, jnp.where(lane == HEAD_DIM + 1, mid,
                                 jnp.where(lane == HEAD_DIM + 2, lo, 0.0)))
                o_ref[:, h * AUG:(h + 1) * AUG] = jnp.where(lane < HEAD_DIM, kh, bias).astype(BF16)

    return pl.pallas_call(
        body, name=name, grid=(t // tr,),
        in_specs=[pl.BlockSpec((tr, d), lambda i: (i, 0)), pl.BlockSpec((tr, LANES), lambda i: (i, 0))],
        out_specs=pl.BlockSpec((tr, hh * AUG), lambda i: (i, 0)),
        out_shape=jax.ShapeDtypeStruct((t, hh * AUG), BF16),
        compiler_params=_params(("parallel",)),
    )(kv, cum)


def _keys_visible(tq):
    s = lax.broadcasted_iota(jnp.int32, (tq, tq), 0)
    t = lax.broadcasted_iota(jnp.int32, (tq, tq), 1)
    return s <= t


def _aug_q_t(q2, e, tq):
    ones = (lax.broadcasted_iota(jnp.int32, (AUG - HEAD_DIM, tq), 0) < BIAS_TERMS).astype(q2.dtype)
    return jnp.concatenate([q2[e * HEAD_DIM:(e + 1) * HEAD_DIM], ones], axis=0)


def _fox_fwd_n(q, k_aug, kv, cfg, *, name):
    t, d, hh = cfg.tokens, cfg.d_model, cfg.heads
    tq = _blk(cfg.seq, FOX_BLOCK)
    nq = cfg.seq // tq

    def body(q_ref, ka_ref, v_ref, o_ref, lse_ref, qa_ref, m_ref, l_ref, acc_ref):
        qi, ki = pl.program_id(1), pl.program_id(2)

        @pl.when(ki == 0)
        def _():
            m_ref[...] = jnp.full_like(m_ref, NEG)
            l_ref[...] = jnp.zeros_like(l_ref)
            acc_ref[...] = jnp.zeros_like(acc_ref)
            for hp in range(hh // 2):
                q2 = q_ref[:, hp * LANES:(hp + 1) * LANES].T
                for e in range(2):
                    h = 2 * hp + e
                    qa_ref[h * AUG:(h + 1) * AUG, :] = _aug_q_t(q2, e, tq)

        def step(diagonal):
            for hp in range(hh // 2):
                vt2 = v_ref[:, hp * LANES:(hp + 1) * LANES].T
                for e in range(2):
                    h = 2 * hp + e
                    rows = slice(h * HEAD_DIM, (h + 1) * HEAD_DIM)
                    s = jnp.dot(ka_ref[:, h * AUG:(h + 1) * AUG], qa_ref[h * AUG:(h + 1) * AUG, :],
                                preferred_element_type=F32)
                    if diagonal:
                        s = jnp.where(_keys_visible(tq), s, NEG)
                    m_prev = m_ref[h:h + 1, :]
                    m_new = jnp.maximum(m_prev, jnp.max(s, axis=0, keepdims=True))
                    alpha = jnp.exp(m_prev - m_new)
                    p = jnp.exp(s - m_new)
                    l_ref[h:h + 1, :] = alpha * l_ref[h:h + 1, :] + jnp.sum(p, axis=0, keepdims=True)
                    m_ref[h:h + 1, :] = m_new
                    hi = p.astype(BF16)
                    lo = (p - hi.astype(F32)).astype(BF16)
                    vt = vt2[e * HEAD_DIM:(e + 1) * HEAD_DIM]
                    acc_ref[rows, :] = (alpha * acc_ref[rows, :] + jnp.dot(vt, hi, preferred_element_type=F32)
                                        + jnp.dot(vt, lo, preferred_element_type=F32))

        pl.when(ki < qi)(functools.partial(step, False))
        pl.when(ki == qi)(functools.partial(step, True))

        @pl.when(ki == qi)
        def _():
            for hp in range(hh // 2):
                halves = [acc_ref[h * HEAD_DIM:(h + 1) * HEAD_DIM, :] * (1.0 / l_ref[h:h + 1, :])
                          for h in (2 * hp, 2 * hp + 1)]
                o_ref[:, hp * LANES:(hp + 1) * LANES] = jnp.concatenate(halves, axis=0).T
            lse_ref[...] = m_ref[...] + jnp.log(l_ref[...])

    def qrow(b, qi, ki):
        return (b * nq + qi, 0)

    return pl.pallas_call(
        body, name=name, grid=(cfg.batch, nq, nq),
        in_specs=[pl.BlockSpec((tq, d), qrow),
                  pl.BlockSpec((tq, hh * AUG), lambda b, qi, ki: (b * nq + jnp.minimum(ki, qi), 0)),
                  pl.BlockSpec((tq, d), lambda b, qi, ki: (b * nq + jnp.minimum(ki, qi), 1))],
        out_specs=[pl.BlockSpec((tq, d), qrow), pl.BlockSpec((hh, tq), lambda b, qi, ki: (0, b * nq + qi))],
        out_shape=[jax.ShapeDtypeStruct((t, d), F32), jax.ShapeDtypeStruct((hh, t), F32)],
        scratch_shapes=[pltpu.VMEM((hh * AUG, tq), BF16), pltpu.VMEM((hh, tq), F32), pltpu.VMEM((hh, tq), F32),
                        pltpu.VMEM((d, tq), F32)],
        compiler_params=_params(("parallel", "parallel", "arbitrary")),
    )(q, k_aug, kv)


def _head_dot_c(a, b, cfg, *, name):
    t, d, hh = cfg.tokens, cfg.d_model, cfg.heads
    tc = _blk(t, ROW_BLOCK)

    def body(a_ref, b_ref, o_ref):
        for hp in range(hh // 2):
            pair = slice(hp * LANES, (hp + 1) * LANES)
            prod = (a_ref[:, pair].astype(F32) * b_ref[:, pair]).T
            for e in range(2):
                h = 2 * hp + e
                o_ref[h:h + 1, :] = jnp.sum(prod[e * HEAD_DIM:(e + 1) * HEAD_DIM], axis=0, keepdims=True)

    return pl.pallas_call(
        body, name=name, grid=(t // tc,),
        in_specs=[pl.BlockSpec((tc, d), lambda i: (i, 0)), pl.BlockSpec((tc, d), lambda i: (i, 0))],
        out_specs=pl.BlockSpec((hh, tc), lambda i: (0, i)),
        out_shape=jax.ShapeDtypeStruct((hh, t), F32),
        compiler_params=_params(("parallel",)),
    )(a, b)


def _fox_bwd_n(q, k_aug, kv, do, lse, dsum, cfg, *, name):
    t, d, hh = cfg.tokens, cfg.d_model, cfg.heads
    tq = _blk(cfg.seq, FOX_BLOCK)
    nq = cfg.seq // tq

    def body(q_ref, ka_ref, v_ref, do_ref, lse_ref, ds_ref, dq_hbm, dk_ref, dv_ref, dc_ref, dq_acc, sem):
        b, ki, qi = pl.program_id(0), pl.program_id(1), pl.program_id(2)
        qq = jnp.maximum(qi, ki)

        @pl.when((ki == 0) & (qi == 0))
        def _():
            dq_acc[...] = jnp.zeros_like(dq_acc)

        @pl.when(qi == 0)
        def _():
            dk_ref[...] = jnp.zeros_like(dk_ref)
            dv_ref[...] = jnp.zeros_like(dv_ref)
            dc_ref[...] = jnp.zeros_like(dc_ref)

        def step(diagonal):
            upper = lax.broadcasted_iota(jnp.int32, (LANES, tq), 0) < HEAD_DIM
            half = _half_mask((tq, LANES))
            for hp in range(hh // 2):
                pair = slice(hp * LANES, (hp + 1) * LANES)
                q2 = q_ref[:, pair].T
                don2 = do_ref[:, pair]
                dot2 = don2.T
                dvs, dks = [], []
                for e in range(2):
                    h = 2 * hp + e
                    rows = slice(h * HEAD_DIM, (h + 1) * HEAD_DIM)
                    ka = ka_ref[:, h * AUG:(h + 1) * AUG]
                    qa = _aug_q_t(q2, e, tq)
                    s = jnp.dot(ka, qa, preferred_element_type=F32)
                    if diagonal:
                        s = jnp.where(_keys_visible(tq), s, NEG)
                    p = jnp.exp(s - lse_ref[h:h + 1, :])
                    dote = jnp.where(upper == (e == 0), dot2, jnp.zeros_like(dot2))
                    dp = jnp.dot(v_ref[:, pair], dote, preferred_element_type=F32)
                    dsf = p * (dp - ds_ref[h:h + 1, :])
                    dc_ref[:, h:h + 1] -= jnp.sum(dsf, axis=1, keepdims=True)
                    dsc = dsf.astype(BF16)
                    dvs.append(jnp.dot(p.astype(BF16), don2, preferred_element_type=F32))
                    dks.append(lax.dot_general(dsc, qa, _NT, preferred_element_type=F32))
                    dq_acc[qq, rows, :] += jnp.dot(ka.T[:HEAD_DIM], dsc, preferred_element_type=F32)
                dv_ref[:, pair] += jnp.where(half, dvs[0], dvs[1])
                dk_ref[:, pair] += jnp.where(half, dks[0], pltpu.roll(dks[1], HEAD_DIM, 1))

        pl.when(qi > ki)(functools.partial(step, False))
        pl.when(qi == ki)(functools.partial(step, True))

        @pl.when((ki == nq - 1) & (qi == nq - 1))
        def _():
            cp = pltpu.make_async_copy(dq_acc, dq_hbm.at[b], sem)
            cp.start()
            cp.wait()

    def qrow(b, ki, qi):
        return (b * nq + jnp.maximum(qi, ki), 0)

    def qcol(b, ki, qi):
        return (0, b * nq + jnp.maximum(qi, ki))

    def krow(b, ki, qi):
        return (b * nq + ki, 0)

    return pl.pallas_call(
        body, name=name, grid=(cfg.batch, nq, nq),
        in_specs=[pl.BlockSpec((tq, d), qrow),
                  pl.BlockSpec((tq, hh * AUG), krow),
                  pl.BlockSpec((tq, d), lambda b, ki, qi: (b * nq + ki, 1)),
                  pl.BlockSpec((tq, d), qrow),
                  pl.BlockSpec((hh, tq), qcol), pl.BlockSpec((hh, tq), qcol)],
        out_specs=[pl.BlockSpec(memory_space=pl.ANY), pl.BlockSpec((tq, d), krow),
                   pl.BlockSpec((tq, d), krow), pl.BlockSpec((tq, LANES), krow)],
        out_shape=[jax.ShapeDtypeStruct((cfg.batch, nq, d, tq), F32), jax.ShapeDtypeStruct((t, d), F32),
                   jax.ShapeDtypeStruct((t, d), F32), jax.ShapeDtypeStruct((t, LANES), F32)],
        scratch_shapes=[pltpu.VMEM((nq, d, tq), F32), pltpu.SemaphoreType.DMA],
        compiler_params=_params(("arbitrary", "arbitrary", "arbitrary")),
    )(q, k_aug, kv, do, lse, dsum)


def _fwd(a, w, *, name, res=None, scale=1.0, norms=()):
    bm = FUSED_ROWS if len(norms) > 1 else 1024
    out = _mm(a, w, form="F", out_dtype=F32, name=name, bm=bm, bn=WIDE, bk=WIDE, res=res, scale=scale, norms=norms)
    return (out[0], list(out[1:])) if norms else out


def _bwd(dy, w, *, name, scale=1.0, norm_bwd=None):
    bm = FUSED_ROWS if norm_bwd is not None else 1024
    return _mm(dy, w, form="B", out_dtype=F32, name=name, bm=bm, bn=WIDE, bk=WIDE, scale=scale, norm_bwd=norm_bwd)


def _wgrad(a, dy, w, *, name, scale=1.0):
    return _mm_grad(a, dy, w.shape[0], name=name, bm=WIDE, bn=WIDE, scale=scale)


def _ffn_fwd(h, n, w_in, w_out, tag, norms=()):
    gate, up, a = _ffn_in_act(n, w_in, name=f"{tag}_in")
    out = _fwd(a, w_out, name=f"{tag}_out", res=h, scale=0.5, norms=norms)
    h_out, normed = out if norms else (out, [])
    return h_out, normed, (n, gate, up, a)


def _ffn_bwd(dh_out, h, g, w_in, w_out, saved, tag):
    n, gate, up, a = saved
    du = _ffn_out_dx_act(dh_out, w_out, gate, up, name=f"{tag}_out_dx", scale=0.5)
    dw_out = _wgrad(a, dh_out, w_out, name=f"{tag}_out_dw", scale=0.5)
    dh, dg = _mm_back2(du, w_in, (h, g, dh_out), name=f"{tag}_in_dx", bk=WIDE)
    dw_in = _mm_grad(n, du, w_in.shape[0], name=f"{tag}_in_dw", bm=WIDE, bn=WIDE)
    return dh, dg, dw_in, dw_out


def _head_gain(g, heads, scale=1.0):
    return jnp.tile(g.astype(F32) * scale, heads)


def _local_step(cfg, x, positions, target, w, s):
    d, hh = cfg.d_model, cfg.heads
    cos, sin = _rope_tables(positions)
    ones = jnp.ones((d,), F32)

    n00 = _rms_fwd(x, s["ffn_norm"][0, 0], name="ffn00_norm")
    h1, (hn_a,), ffn0 = _ffn_fwd(x, n00, w["ffn_w_in"][0][0], w["ffn_w_out"][0][0], "ffn00", [s["mix_norm"][0]])
    qkv = _fwd(hn_a, w["a_w_qkv"], name="a_qkv")
    kinds_a = ["rope", "rope", "cast"] * len(DILATIONS)
    gains_a = jnp.stack([z for g in range(len(DILATIONS)) for z in (
        _head_gain(s["a_q_norm"][g], hh, Q_SCALE), _head_gain(s["a_k_norm"][g], hh), ones)])
    qkvp = [_hn_fwd(qkv, gains_a[3 * g:3 * g + 3], kinds_a[:3], d, cos, sin, name=f"a_qk_norm{g}", col0=3 * g)
            for g in range(len(DILATIONS))]
    lay = [qkvp[g].reshape(cfg.tokens // dil, dil * 3 * d) for g, dil in enumerate(DILATIONS)]
    band = [_band_fwd_n(lay[g], dil, cfg, name=f"a_band{g}") for g, dil in enumerate(DILATIONS)]
    mixed, lse_a = _mix_fwd([o.reshape(cfg.tokens, d) for o, _ in band], [_from_classes_t(l) for _, l in band],
                            name="a_mix")
    h2, (n01,) = _fwd(mixed, w["a_w_o"], name="a_out", res=h1, norms=[s["ffn_norm"][0, 1]])
    h3, (kn, n10), ffn1 = _ffn_fwd(h2, n01, w["ffn_w_in"][0][1], w["ffn_w_out"][0][1], "ffn01",
                                   [s["kv_norm"], s["ffn_norm"][1, 0]])

    proj = _fwd(kn, w["kv_w"], name="kv_proj")
    kinds_kv = ["norm", "cast"]
    gains_kv = jnp.stack([_head_gain(s["kv_k_norm"], hh), ones])
    kvp = _hn_fwd(proj, gains_kv, kinds_kv, d, cos, sin, name="kv_k_norm")
    gate_col = 2 * d // LANES
    bias = jnp.pad(s["kv_b_f"].astype(F32), (0, LANES - hh))
    cum = _gate_fwd(proj, gate_col, bias, cfg, name="kv_gate")
    k_aug = _fox_aug_k_call(kvp, cum, cfg, name="kv_aug")

    h4, (hn_b,), ffn2 = _ffn_fwd(h3, n10, w["ffn_w_in"][1][0], w["ffn_w_out"][1][0], "ffn10", [s["mix_norm"][1]])
    qraw = _fwd(hn_b, w["b_w_q"], name="b_q")
    gains_b = _head_gain(s["b_q_norm"][0], hh, Q_SCALE)[None]
    qp = _hn_fwd(qraw, gains_b, ["norm"], d, cos, sin, name="b_q_norm")
    o_b, lse_b = _fox_fwd_n(qp, k_aug, kvp, cfg, name="b_fox")
    h5, (n11,) = _fwd(o_b, w["b_w_o"], name="b_out", res=h4, norms=[s["ffn_norm"][1, 1]])
    h6, _, ffn3 = _ffn_fwd(h5, n11, w["ffn_w_in"][1][1], w["ffn_w_out"][1][1], "ffn11")

    loss, dh6 = _loss_fwd_bwd(h6, target, name="loss")

    dh5, dg11, dwi11, dwo11 = _ffn_bwd(dh6, h5, s["ffn_norm"][1, 1], w["ffn_w_in"][1][1], w["ffn_w_out"][1][1],
                                       ffn3, "ffn11")
    do_b = _bwd(dh5, w["b_w_o"], name="b_out_dx")
    dw_bo = _wgrad(o_b, dh5, w["b_w_o"], name="b_out_dw")
    do_bf = do_b.astype(BF16)
    dsum_b = _head_dot_c(do_bf, o_b, cfg, name="b_dsum")
    dq4, dk_b, dv_b, dcum = _fox_bwd_n(qp, k_aug, kvp, do_bf, lse_b, dsum_b, cfg, name="b_fox_bwd")
    dq_b = dq4.transpose(0, 1, 3, 2).reshape(cfg.tokens, d)
    dqraw, dgq = _hn_bwd(qraw, [dq_b], gains_b, ["norm"], d, cos, sin, name="b_q_norm_bwd")
    dh4, dmix1 = _bwd(dqraw, w["b_w_q"], name="b_q_dx", norm_bwd=(h4, s["mix_norm"][1], dh5))
    dw_bq = _wgrad(hn_b, dqraw, w["b_w_q"], name="b_q_dw")
    dh3, dg10, dwi10, dwo10 = _ffn_bwd(dh4, h3, s["ffn_norm"][1, 0], w["ffn_w_in"][1][0], w["ffn_w_out"][1][0],
                                       ffn2, "ffn10")

    dkvraw, dgk = _hn_bwd(proj, [dk_b, dv_b], gains_kv, kinds_kv, d, cos, sin, name="kv_k_norm_bwd")
    dz, dbias = _gate_bwd(proj, gate_col, bias, dcum, cfg, name="kv_gate_bwd")
    pad_cols = w["kv_w"].shape[2] - 2 * d - LANES
    dproj = jnp.concatenate([dkvraw, dz.astype(BF16), jnp.zeros((cfg.tokens, pad_cols), BF16)], axis=1)
    dw_kv = _wgrad(kn, dproj, w["kv_w"], name="kv_proj_dw")
    dh3, dkvn = _bwd(dproj, w["kv_w"], name="kv_proj_dx", norm_bwd=(h3, s["kv_norm"], dh3))

    dh2, dg01, dwi01, dwo01 = _ffn_bwd(dh3, h2, s["ffn_norm"][0, 1], w["ffn_w_in"][0][1], w["ffn_w_out"][0][1],
                                       ffn1, "ffn01")
    dmixed = _bwd(dh2, w["a_w_o"], name="a_out_dx")
    dw_ao = _wgrad(mixed, dh2, w["a_w_o"], name="a_out_dw")
    dmixed_bf = dmixed.astype(BF16)
    dsum_a = _head_dot_c(dmixed_bf, mixed, cfg, name="a_dsum")
    dqkvp = []
    for g, dil in enumerate(DILATIONS):
        dsum_c = dsum_a.reshape(hh, cfg.tokens // dil, dil).transpose(2, 0, 1)
        grads = _band_bwd_n(lay[g], dmixed_bf.reshape(cfg.tokens // dil, dil * d), _to_classes_t(lse_a, dil, hh),
                            dsum_c, dil, cfg, name=f"a_band{g}_bwd")
        dqkvp += [z.reshape(cfg.tokens, d) for z in grads]
    dqkv, dga = _hn_bwd(qkv, dqkvp, gains_a, kinds_a, d, cos, sin, name="a_qk_norm_bwd")
    dh1, dmix0 = _bwd(dqkv, w["a_w_qkv"], name="a_qkv_dx", norm_bwd=(h1, s["mix_norm"][0], dh2))
    dw_qkv = _wgrad(hn_a, dqkv, w["a_w_qkv"], name="a_qkv_dw")
    dx, dg00, dwi00, dwo00 = _ffn_bwd(dh1, x, s["ffn_norm"][0, 0], w["ffn_w_in"][0][0], w["ffn_w_out"][0][0],
                                      ffn0, "ffn00")

    dw = {
        "ffn_w_in": [[dwi00, dwi01], [dwi10, dwi11]],
        "ffn_w_out": [[dwo00, dwo01], [dwo10, dwo11]],
        "a_w_qkv": dw_qkv, "a_w_o": dw_ao, "kv_w": dw_kv, "b_w_q": dw_bq, "b_w_o": dw_bo,
    }
    ds = {
        "ffn_norm": jnp.stack([jnp.stack([dg00, dg01]), jnp.stack([dg10, dg11])]),
        "mix_norm": jnp.stack([dmix0, dmix1]),
        "a_q_norm": jnp.stack([dga[3 * g] for g in range(len(DILATIONS))])[None] * Q_SCALE,
        "a_k_norm": jnp.stack([dga[3 * g + 1] for g in range(len(DILATIONS))])[None],
        "kv_norm": dkvn,
        "kv_b_f": dbias[:hh],
        "kv_k_norm": dgk[0],
        "b_q_norm": dgq * Q_SCALE,
    }
    return loss, dx, dw, ds


MESH_ID = pl.DeviceIdType.MESH
ANY = pl.BlockSpec(memory_space=pl.ANY)
PACK_COLS = 1024
PACK_ROW_ALIGN = 32


def _me():
    return lax.axis_index("x"), lax.axis_index("y"), lax.axis_index("c")


def _other_chips(x, y):
    return [(1 - x, y), (x, 1 - y), (1 - x, 1 - y)]


def _all_gather_small(v, *, name):
    r = v.shape[0]

    def body(v_ref, out_ref, send_sems, recv_sems):
        x, y, c = _me()
        me = 4 * x + 2 * y + c
        out_ref[me] = v_ref[...]
        copies = []
        for k in range(1, N_DEV):
            fx, fy, fc = (k >> 2) & 1, (k >> 1) & 1, k & 1
            peer = (1 - x if fx else x, 1 - y if fy else y, 1 - c if fc else c)
            copies.append(pltpu.make_async_remote_copy(
                src_ref=v_ref, dst_ref=out_ref.at[me], send_sem=send_sems.at[k - 1], recv_sem=recv_sems.at[k - 1],
                device_id=peer, device_id_type=MESH_ID))
        for cp in copies:
            cp.start()
        for cp in copies:
            cp.wait()

    return pl.pallas_call(
        body, name=name,
        in_specs=[pl.BlockSpec(memory_space=pltpu.VMEM)], out_specs=pl.BlockSpec(memory_space=pltpu.VMEM),
        out_shape=jax.ShapeDtypeStruct((N_DEV, r, LANES), v.dtype),
        scratch_shapes=[pltpu.SemaphoreType.DMA((N_DEV - 1,)), pltpu.SemaphoreType.DMA((N_DEV - 1,))],
    )(v)


def _all_gather_chips(v, *, name):
    rh = v.shape[0] // 2

    def body(v_ref, out_ref, send_sems, recv_sems):
        x, y, c = _me()
        j = 2 * x + y
        chips = _other_chips(x, y)

        def half(chip, core):
            return out_ref.at[chip, pl.ds(core * rh, rh)]

        first = [pltpu.make_async_remote_copy(
            src_ref=v_ref.at[pl.ds(c * rh, rh)], dst_ref=half(j, c), send_sem=send_sems.at[k],
            recv_sem=recv_sems.at[k], device_id=(px, py, c), device_id_type=MESH_ID)
            for k, (px, py) in enumerate(chips)]
        for cp in first:
            cp.start()
        passed = [pltpu.make_async_remote_copy(
            src_ref=half(2 * px + py, c), dst_ref=half(2 * px + py, c), send_sem=send_sems.at[3 + k],
            recv_sem=recv_sems.at[3 + k], device_id=(x, y, 1 - c), device_id_type=MESH_ID)
            for k, (px, py) in enumerate(chips)]
        for k in range(len(chips)):
            first[k].wait_recv()
            passed[k].start()
        for k, (px, py) in enumerate(chips):
            pltpu.make_async_remote_copy(
                src_ref=half(2 * px + py, 1 - c), dst_ref=half(2 * px + py, 1 - c), send_sem=send_sems.at[3 + k],
                recv_sem=recv_sems.at[3 + k], device_id=(x, y, 1 - c), device_id_type=MESH_ID).wait_recv()
        for cp in first + passed:
            cp.wait_send()

    return pl.pallas_call(
        body, name=name, in_specs=[ANY], out_specs=ANY,
        out_shape=jax.ShapeDtypeStruct((N_CHIPS,) + v.shape, v.dtype),
        scratch_shapes=[pltpu.SemaphoreType.DMA((2 * (N_CHIPS - 1),)), pltpu.SemaphoreType.DMA((2 * (N_CHIPS - 1),))],
    )(v)


def _swap_halves(g, *, name):
    n, r, cols = g.shape
    rh = r // 2

    def body(g_ref, out_ref, send_sem, recv_sem):
        x, y, c = _me()
        cp = pltpu.make_async_remote_copy(
            src_ref=g_ref.at[:, pl.ds((1 - c) * rh, rh)], dst_ref=out_ref, send_sem=send_sem, recv_sem=recv_sem,
            device_id=(x, y, 1 - c), device_id_type=MESH_ID)
        cp.start()
        cp.wait()

    return pl.pallas_call(
        body, name=name, in_specs=[ANY], out_specs=ANY,
        out_shape=jax.ShapeDtypeStruct((n, rh, cols), g.dtype),
        scratch_shapes=[pltpu.SemaphoreType.DMA, pltpu.SemaphoreType.DMA],
    )(g)


def _scatter_chips(v, *, name):
    def body(v_ref, out_ref, send_sems, recv_sems):
        x, y, c = _me()
        j = 2 * x + y
        copies = [pltpu.make_async_remote_copy(
            src_ref=v_ref.at[2 * px + py], dst_ref=out_ref.at[j], send_sem=send_sems.at[k], recv_sem=recv_sems.at[k],
            device_id=(px, py, c), device_id_type=MESH_ID) for k, (px, py) in enumerate(_other_chips(x, y))]
        for cp in copies:
            cp.start()
        for cp in copies:
            cp.wait()

    return pl.pallas_call(
        body, name=name, in_specs=[ANY], out_specs=ANY,
        out_shape=jax.ShapeDtypeStruct(v.shape, v.dtype),
        scratch_shapes=[pltpu.SemaphoreType.DMA((N_CHIPS - 1,)), pltpu.SemaphoreType.DMA((N_CHIPS - 1,))],
    )(v)


def _join_halves(v, *, name):
    def body(v_ref, out_ref, send_sem, recv_sem):
        x, y, c = _me()
        cp = pltpu.make_async_remote_copy(
            src_ref=v_ref, dst_ref=out_ref.at[c], send_sem=send_sem, recv_sem=recv_sem,
            device_id=(x, y, 1 - c), device_id_type=MESH_ID)
        cp.start()
        cp.wait()

    return pl.pallas_call(
        body, name=name, in_specs=[ANY], out_specs=ANY,
        out_shape=jax.ShapeDtypeStruct((2,) + v.shape, v.dtype),
        scratch_shapes=[pltpu.SemaphoreType.DMA, pltpu.SemaphoreType.DMA],
    )(v)


def _row_blk(rows, want):
    for b in range(min(rows, want) // SUBLANES * SUBLANES, 0, -SUBLANES):
        if rows % b == 0:
            return b
    return rows


def _add_own_half(g, got, *, name):
    n, r, cols = g.shape
    rh = r // 2
    tr = _row_blk(rh, 512)
    nb = rh // tr

    def body(c_ref, g_ref, got_ref, o_ref):
        del c_ref
        o_ref[...] = (g_ref[...] + got_ref[...]).astype(BF16)

    grid_spec = pltpu.PrefetchScalarGridSpec(
        num_scalar_prefetch=1, grid=(n, nb),
        in_specs=[pl.BlockSpec((None, tr, cols), lambda j, i, c: (j, c[0] * nb + i, 0)),
                  pl.BlockSpec((None, tr, cols), lambda j, i, c: (j, i, 0))],
        out_specs=pl.BlockSpec((None, tr, cols), lambda j, i, c: (j, i, 0)))
    return pl.pallas_call(
        body, name=name, grid_spec=grid_spec, out_shape=jax.ShapeDtypeStruct((n, rh, cols), BF16),
        compiler_params=_params(("parallel", "parallel")),
    )(lax.axis_index("c").astype(jnp.int32).reshape(1), g, got)


def _sum_parts(parts, *, name):
    n, r, cols = parts.shape
    tr = _row_blk(r, 512)

    def body(*refs):
        o_ref = refs[n]
        acc = refs[0][...].astype(F32)
        for p_ref in refs[1:n]:
            acc = acc + p_ref[...].astype(F32)
        o_ref[...] = acc

    return pl.pallas_call(
        body, name=name, grid=(r // tr,),
        in_specs=[pl.BlockSpec((None, tr, cols), functools.partial(lambda j, i: (j, i, 0), j)) for j in range(n)],
        out_specs=pl.BlockSpec((tr, cols), lambda i: (i, 0)),
        out_shape=jax.ShapeDtypeStruct((r, cols), F32),
        compiler_params=_params(("parallel",)),
    )(*([parts] * n))


def _adamw(w, m, v, g, *, name):
    shape = w.shape
    cols = shape[-1]
    w2, m2, v2, g2 = (z.reshape(-1, cols) for z in (w, m, v, g))
    rows = w2.shape[0]
    tr = _row_blk(rows, max(SUBLANES, (1 << 20) // (4 * cols)))

    def body(w_ref, m_ref, v_ref, g_ref, d_ref, nm_ref, nv_ref):
        gv = g_ref[...]
        nm = ADAM_B1 * m_ref[...] + (1.0 - ADAM_B1) * gv
        nv = ADAM_B2 * v_ref[...] + (1.0 - ADAM_B2) * jnp.square(gv)
        m_hat = nm / (1.0 - ADAM_B1 ** ADAM_STEP)
        v_hat = nv / (1.0 - ADAM_B2 ** ADAM_STEP)
        d_ref[...] = -ADAM_LR * (m_hat / (jnp.sqrt(v_hat) + ADAM_EPS) + ADAM_WD * w_ref[...])
        nm_ref[...] = nm
        nv_ref[...] = nv

    spec = pl.BlockSpec((tr, cols), lambda i: (i, 0))
    out = jax.ShapeDtypeStruct((rows, cols), F32)
    d, nm, nv = pl.pallas_call(
        body, name=name, grid=(rows // tr,), in_specs=[spec] * 4, out_specs=[spec] * 3, out_shape=[out] * 3,
        compiler_params=_params(("parallel",)),
    )(w2, m2, v2, g2)
    return d.reshape(shape), nm.reshape(shape), nv.reshape(shape)


def _pack_rows(size, cols, align):
    return -(-size // (cols * align)) * align


def _pack(arrs, lead, cols, align, total_align):
    lead_shape = arrs[0].shape[:lead]
    parts = []
    for a in arrs:
        flat = a.reshape(lead_shape + (-1,))
        size = flat.shape[-1]
        rows = _pack_rows(size, cols, align)
        flat = jnp.pad(flat, [(0, 0)] * lead + [(0, rows * cols - size)])
        parts.append(flat.reshape(lead_shape + (rows, cols)))
    total = sum(p.shape[lead] for p in parts)
    extra = -total % total_align
    if extra:
        parts.append(jnp.zeros(lead_shape + (extra, cols), parts[0].dtype))
    return jnp.concatenate(parts, axis=lead)


def _unpack(buf, shapes, lead, cols, align):
    lead_shape = buf.shape[:lead]
    out, row = [], 0
    for shp in shapes:
        size = 1
        for n in shp:
            size *= n
        rows = _pack_rows(size, cols, align)
        piece = lax.slice_in_dim(buf, row, row + rows, axis=lead).reshape(lead_shape + (-1,))
        out.append(piece[..., :size].reshape(lead_shape + tuple(shp)))
        row += rows
    return out


BIG = ("ffn_w_in", "ffn_w_out", "a_w_qkv", "a_w_o", "kv_w", "b_w_q", "b_w_o")
SMALL = ("ffn_norm", "mix_norm", "a_q_norm", "a_k_norm", "kv_norm", "kv_b_f", "kv_k_norm", "b_q_norm")
WEIGHTS = ("ffn_norm", "ffn_w_in", "ffn_w_out", "mix_norm", "a_w_qkv", "a_q_norm", "a_k_norm", "a_w_o",
           "kv_norm", "kv_w", "kv_b_f", "kv_k_norm", "b_w_q", "b_q_norm", "b_w_o")
GATE_PAD = 2 * LANES


def _stack_weights(sh, d):
    depth = sh["ffn_w_in"].shape[1]
    kv = sh["kv_w"].transpose(1, 0, 2).reshape(d, -1)
    kv = jnp.pad(kv, ((0, 0), (0, 2 * d + GATE_PAD - kv.shape[1])))
    return {
        "ffn_w_in": [[sh["ffn_w_in"][:, l, i] for i in range(2)] for l in range(depth)],
        "ffn_w_out": [[sh["ffn_w_out"][:, l, i].reshape(1, -1, d) for i in range(2)] for l in range(depth)],
        "a_w_qkv": sh["a_w_qkv"][:, 0],
        "a_w_o": sh["a_w_o"].reshape(1, d, d),
        "kv_w": kv[None],
        "b_w_q": sh["b_w_q"].reshape(1, d, d),
        "b_w_o": sh["b_w_o"].reshape(1, d, d),
    }


def _unstack_grads(dw, d, heads):
    def rows4(z):
        return z.reshape(N_CHIPS, -1, d)

    kv_cols = 2 * d + heads
    kv = dw["kv_w"][0][:, :kv_cols].reshape(d, N_CHIPS, kv_cols // N_CHIPS).transpose(1, 0, 2)
    return [
        jnp.stack([jnp.stack(row, axis=1) for row in dw["ffn_w_in"]], axis=1),
        jnp.stack([jnp.stack([rows4(z) for z in row], axis=1) for row in dw["ffn_w_out"]], axis=1),
        dw["a_w_qkv"][:, None],
        rows4(dw["a_w_o"])[:, None],
        kv,
        rows4(dw["b_w_q"])[:, None],
        rows4(dw["b_w_o"])[:, None],
    ]


def kernel(x, positions, ffn_norm, ffn_w_in, ffn_w_out, mix_norm, a_w_qkv, a_q_norm, a_k_norm, a_w_o, kv_norm, kv_w, kv_b_f, kv_k_norm, b_w_q, b_q_norm, b_w_o, loss_target, m_ffn_norm, m_ffn_w_in, m_ffn_w_out, m_mix_norm, m_a_w_qkv, m_a_q_norm, m_a_k_norm, m_a_w_o, m_kv_norm, m_kv_w, m_kv_b_f, m_kv_k_norm, m_b_w_q, m_b_q_norm, m_b_w_o, v_ffn_norm, v_ffn_w_in, v_ffn_w_out, v_mix_norm, v_a_w_qkv, v_a_q_norm, v_a_k_norm, v_a_w_o, v_kv_norm, v_kv_w, v_kv_b_f, v_kv_k_norm, v_b_w_q, v_b_q_norm, v_b_w_o):
    wts = dict(zip(WEIGHTS, (ffn_norm, ffn_w_in, ffn_w_out, mix_norm, a_w_qkv, a_q_norm, a_k_norm, a_w_o, kv_norm,
                             kv_w, kv_b_f, kv_k_norm, b_w_q, b_q_norm, b_w_o)))
    mom = dict(zip(WEIGHTS, (m_ffn_norm, m_ffn_w_in, m_ffn_w_out, m_mix_norm, m_a_w_qkv, m_a_q_norm, m_a_k_norm,
                             m_a_w_o, m_kv_norm, m_kv_w, m_kv_b_f, m_kv_k_norm, m_b_w_q, m_b_q_norm, m_b_w_o)))
    var = dict(zip(WEIGHTS, (v_ffn_norm, v_ffn_w_in, v_ffn_w_out, v_mix_norm, v_a_w_qkv, v_a_q_norm, v_a_k_norm,
                             v_a_w_o, v_kv_norm, v_kv_w, v_kv_b_f, v_kv_k_norm, v_b_w_q, v_b_q_norm, v_b_w_o)))
    batch, seq, d = x.shape
    cfg = Cfg(d_model=d, d_ff=ffn_w_out.shape[2] * N_CHIPS, seq=seq, batch=batch)
    chip = 2 * lax.axis_index("x") + lax.axis_index("y")
    big_shapes = [wts[n].shape for n in BIG]

    shard = _pack([wts[n].astype(BF16) for n in BIG], 0, PACK_COLS, PACK_ROW_ALIGN, PACK_COLS)
    gathered = _all_gather_chips(shard, name="gather_weights")
    gathered = lax.dynamic_update_slice_in_dim(gathered, shard[None], chip, axis=0)
    w = _stack_weights(dict(zip(BIG, _unpack(gathered, big_shapes, 1, PACK_COLS, PACK_ROW_ALIGN))), d)
    norm_shard = _pack([ffn_norm], 0, LANES, SUBLANES, SUBLANES)
    norms = _all_gather_small(norm_shard, name="gather_ffn_norm")[0::2]
    (norms,) = _unpack(norms, [ffn_norm.shape], 1, LANES, SUBLANES)
    small = {"ffn_norm": jnp.moveaxis(norms, 0, 2).reshape(ffn_norm.shape[:2] + (d,)),
             "mix_norm": mix_norm, "a_q_norm": a_q_norm[0], "a_k_norm": a_k_norm[0], "kv_norm": kv_norm,
             "kv_b_f": kv_b_f, "kv_k_norm": kv_k_norm, "b_q_norm": b_q_norm}

    loss, dx, dw, ds = _local_step(cfg, x.reshape(cfg.tokens, d), positions.reshape(cfg.tokens),
                                   loss_target.reshape(cfg.tokens, d), w, small)
    loss = lax.psum(loss, ("x", "y", "c"))

    g = _pack(_unstack_grads(dw, d, cfg.heads), 1, PACK_COLS, PACK_ROW_ALIGN, PACK_COLS)
    chip_half = _add_own_half(g, _swap_halves(g, name="swap_halves"), name="add_halves")
    parts = _scatter_chips(chip_half, name="scatter_chips")
    parts = lax.dynamic_update_slice_in_dim(parts, lax.dynamic_slice_in_dim(chip_half, chip, 1, axis=0), chip, axis=0)
    mine = _sum_parts(parts, name="sum_chips")
    both = _join_halves(mine, name="join_halves")
    g_big = lax.dynamic_update_slice_in_dim(both, mine[None], lax.axis_index("c"), axis=0).reshape(g.shape[1:])
    grads = dict(zip(BIG, _unpack(g_big, big_shapes, 0, PACK_COLS, PACK_ROW_ALIGN)))

    small_shapes = [ds[n].shape for n in SMALL]
    parts = _all_gather_small(_pack([ds[n] for n in SMALL], 0, LANES, SUBLANES, SUBLANES), name="gather_small")
    g_small = dict(zip(SMALL, _unpack(_sum_parts(parts, name="sum_small"), small_shapes, 0, LANES, SUBLANES)))
    quarter = d // N_CHIPS
    g_small["ffn_norm"] = lax.dynamic_slice_in_dim(g_small["ffn_norm"], chip * quarter, quarter, axis=2)
    grads.update(g_small)

    delta, new_m, new_v = {}, {}, {}
    for n in BIG:
        delta[n], new_m[n], new_v[n] = _adamw(wts[n], mom[n], var[n], grads[n], name=f"adamw_{n}")
    packed = [_pack([z[n] for n in SMALL], 0, LANES, SUBLANES, SUBLANES) for z in (wts, mom, var, grads)]
    small_out = _adamw(*packed, name="adamw_small")
    shard_shapes = [wts[n].shape for n in SMALL]
    for out, res in zip((delta, new_m, new_v), small_out):
        out.update(zip(SMALL, _unpack(res, shard_shapes, 0, LANES, SUBLANES)))

    return (loss, dx.reshape(x.shape), *[grads[n] for n in WEIGHTS], *[delta[n] for n in WEIGHTS],
            *[new_m[n] for n in WEIGHTS], *[new_v[n] for n in WEIGHTS])
```
